```python
import jax, jax.numpy as jnp
from jax import lax
import numpy as np

D_MODEL = 1024
BATCH = 32
SEQ = 2048
DEPTH = 1

MEM_LEN = 256
D_MIX = D_MODEL
SB_HEAD_DIM = 64
SB_WIDTH = D_MIX // 2
SB_HEADS = SB_WIDTH // SB_HEAD_DIM
POOL_WIDTH = D_MIX - SB_WIDTH
POOL_WINDOWS = (2, 4, 8, 16)
POOL_GROUPS = len(POOL_WINDOWS)
POOL_GROUP_DIM = POOL_WIDTH // POOL_GROUPS
IN_COLS = 3 * SB_WIDTH + POOL_WIDTH
Q_BLOCK = 128
MEM_HEADS = 4
MEM_HEAD_DIM = D_MODEL // MEM_HEADS
D_FF = 256 * ((8 * D_MODEL // 3 + 255) // 256)
FFN_RESIDUAL_WEIGHT = 0.5
EPS = 1e-6

kernel_name = "hybrid_stickbreaking_pool_macaron_layer"


def rmsnorm(x, g):
    xf = x.astype(jnp.float32)
    xf = xf * lax.rsqrt(jnp.mean(xf * xf, axis=-1, keepdims=True) + EPS)
    return xf.astype(x.dtype) * g


def swiglu(h, w_gate, w_up, w_down):
    return (jax.nn.silu(h @ w_gate) * (h @ w_up)) @ w_down


def stick_breaking_attention(q, k, v):
    S = q.shape[2]
    scale = q.shape[-1] ** -0.5
    outs = []
    for i in range(S // Q_BLOCK):
        q0 = i * Q_BLOCK
        kv_len = q0 + Q_BLOCK
        q_blk = q[:, :, q0:kv_len]
        k_pre = k[:, :, :kv_len]
        v_pre = v[:, :, :kv_len]
        z = jnp.einsum("bhqd,bhkd->bhqk", q_blk, k_pre).astype(jnp.float32) * scale
        q_pos = q0 + jnp.arange(Q_BLOCK)
        k_pos = jnp.arange(kv_len)
        strict = k_pos[None, :] < q_pos[:, None]
        log_rest = jnp.where(strict, jax.nn.log_sigmoid(-z), 0.0)
        suffix = lax.cumsum(log_rest, axis=3, reverse=True) - log_rest
        a = jnp.where(strict, jnp.exp(jax.nn.log_sigmoid(z) + suffix), 0.0)
        outs.append(jnp.einsum("bhqk,bhkd->bhqd", a.astype(v.dtype), v_pre))
    return jnp.concatenate(outs, axis=2)


def causal_multiscale_pool(u):
    B, S, _ = u.shape
    uf = u.astype(jnp.float32).reshape(B, S, POOL_GROUPS, POOL_GROUP_DIM)
    cs = jnp.concatenate([jnp.zeros_like(uf[:, :1]), jnp.cumsum(uf, axis=1)], axis=1)
    pos = jnp.arange(S)
    pooled = []
    for g, w in enumerate(POOL_WINDOWS):
        hi = cs[:, 1:, g]
        lo = cs[:, jnp.maximum(pos + 1 - w, 0), g]
        count = jnp.minimum(pos + 1, w).astype(jnp.float32)
        pooled.append((hi - lo) / count[None, :, None])
    pooled = jnp.stack(pooled, axis=2)
    return (pooled - uf).astype(u.dtype)


def memory_cross_attention(h, mem_n, w_q, w_kv, w_o):
    B, S, _ = h.shape
    M = mem_n.shape[1]
    q = (h @ w_q).reshape(B, S, MEM_HEADS, MEM_HEAD_DIM)
    kv = (mem_n @ w_kv).reshape(B, M, 2, MEM_HEADS, MEM_HEAD_DIM)
    k, v = kv[:, :, 0], kv[:, :, 1]
    s = jnp.einsum("bshd,bmhd->bhsm", q, k).astype(jnp.float32) * (MEM_HEAD_DIM ** -0.5)
    p = jax.nn.softmax(s, axis=-1).astype(v.dtype)
    o = jnp.einsum("bhsm,bmhd->bshd", p, v).reshape(B, S, D_MODEL)
    return o @ w_o


def parallel_head_group_mixer(h, w_in, w_pool, pool_scale, w_out):
    B, S, _ = h.shape
    proj = h @ w_in
    q, k, v, u = jnp.split(proj, [SB_WIDTH, 2 * SB_WIDTH, 3 * SB_WIDTH], axis=-1)
    to_heads = lambda t: t.reshape(B, S, SB_HEADS, SB_HEAD_DIM).transpose(0, 2, 1, 3)
    o_sb = stick_breaking_attention(to_heads(q), to_heads(k), to_heads(v))
    o_sb = o_sb.transpose(0, 2, 1, 3).reshape(B, S, SB_WIDTH)
    pooled = causal_multiscale_pool(u)
    o_pool = jnp.einsum("bsgc,gcd->bsgd", pooled, w_pool).reshape(B, S, POOL_WIDTH) * pool_scale
    return jnp.concatenate([o_sb, o_pool], axis=-1) @ w_out


def _fwd_setup_inputs(seed: int = 0) -> dict:
    key = jax.random.key(seed)
    ks = jax.random.split(key, 24)

    def dense(k, shape, fan_in):
        return jax.random.normal(k, shape, jnp.float32) * fan_in ** -0.5

    def gain(k, shape):
        return 1.0 + 0.02 * jax.random.normal(k, shape, jnp.float32)

    L, D = DEPTH, D_MODEL
    return {
        "x": jax.random.normal(ks[0], (BATCH, SEQ, D), jnp.float32),
        "mem": jax.random.normal(ks[1], (BATCH, MEM_LEN, D), jnp.float32),
        "ffn1_norm": gain(ks[2], (L, D)),
        "ffn1_w_gate": dense(ks[3], (L, D, D_FF), D),
        "ffn1_w_up": dense(ks[4], (L, D, D_FF), D),
        "ffn1_w_down": dense(ks[5], (L, D_FF, D), D_FF),
        "mix_norm": gain(ks[6], (L, D)),
        "w_in": dense(ks[7], (L, D, IN_COLS), D),
        "w_pool": dense(ks[8], (L, POOL_GROUPS, POOL_GROUP_DIM, POOL_GROUP_DIM), POOL_GROUP_DIM),
        "pool_scale": gain(ks[9], (L, POOL_WIDTH)),
        "w_out": dense(ks[10], (L, D_MIX, D), D_MIX),
        "mem_q_norm": gain(ks[11], (L, D)),
        "mem_kv_norm": gain(ks[12], (L, D)),
        "mem_w_q": dense(ks[13], (L, D, D), D),
        "mem_w_kv": dense(ks[14], (L, D, 2 * D), D),
        "mem_w_o": dense(ks[15], (L, D, D), D),
        "ffn2_norm": gain(ks[16], (L, D)),
        "ffn2_w_gate": dense(ks[17], (L, D, D_FF), D),
        "ffn2_w_up": dense(ks[18], (L, D, D_FF), D),
        "ffn2_w_down": dense(ks[19], (L, D_FF, D), D_FF),
        "final_norm": gain(ks[20], (D,)),
    }


def _fwd_reference(x, mem, ffn1_norm, ffn1_w_gate, ffn1_w_up, ffn1_w_down, mix_norm, w_in, w_pool,
              pool_scale, w_out, mem_q_norm, mem_kv_norm, mem_w_q, mem_w_kv, mem_w_o,
              ffn2_norm, ffn2_w_gate, ffn2_w_up, ffn2_w_down, final_norm):
    for l in range(DEPTH):
        x = x + FFN_RESIDUAL_WEIGHT * swiglu(rmsnorm(x, ffn1_norm[l]), ffn1_w_gate[l], ffn1_w_up[l], ffn1_w_down[l])
        x = x + parallel_head_group_mixer(rmsnorm(x, mix_norm[l]), w_in[l], w_pool[l], pool_scale[l], w_out[l])
        x = x + memory_cross_attention(rmsnorm(x, mem_q_norm[l]), rmsnorm(mem, mem_kv_norm[l]),
                                       mem_w_q[l], mem_w_kv[l], mem_w_o[l])
        x = x + FFN_RESIDUAL_WEIGHT * swiglu(rmsnorm(x, ffn2_norm[l]), ffn2_w_gate[l], ffn2_w_up[l], ffn2_w_down[l])
    return rmsnorm(x, final_norm)


import jax as _jax
import jax.numpy as _jnp

TWIN_FORMAT = 'train_step'
FWD_PARAMS = ['x', 'mem', 'ffn1_norm', 'ffn1_w_gate', 'ffn1_w_up', 'ffn1_w_down', 'mix_norm', 'w_in', 'w_pool', 'pool_scale', 'w_out', 'mem_q_norm', 'mem_kv_norm', 'mem_w_q', 'mem_w_kv', 'mem_w_o', 'ffn2_norm', 'ffn2_w_gate', 'ffn2_w_up', 'ffn2_w_down', 'final_norm']
TWIN_WEIGHTS = ['ffn1_norm', 'ffn1_w_gate', 'ffn1_w_up', 'ffn1_w_down', 'mix_norm', 'w_in', 'w_pool', 'pool_scale', 'w_out', 'mem_q_norm', 'mem_kv_norm', 'mem_w_q', 'mem_w_kv', 'mem_w_o', 'ffn2_norm', 'ffn2_w_gate', 'ffn2_w_up', 'ffn2_w_down', 'final_norm']
TWIN_DIFF_INPUT = 'x'
TWIN_INPUTS = ['x', 'mem', 'ffn1_norm', 'ffn1_w_gate', 'ffn1_w_up', 'ffn1_w_down', 'mix_norm', 'w_in', 'w_pool', 'pool_scale', 'w_out', 'mem_q_norm', 'mem_kv_norm', 'mem_w_q', 'mem_w_kv', 'mem_w_o', 'ffn2_norm', 'ffn2_w_gate', 'ffn2_w_up', 'ffn2_w_down', 'final_norm', 'loss_target', 'm_ffn1_norm', 'm_ffn1_w_gate', 'm_ffn1_w_up', 'm_ffn1_w_down', 'm_mix_norm', 'm_w_in', 'm_w_pool', 'm_pool_scale', 'm_w_out', 'm_mem_q_norm', 'm_mem_kv_norm', 'm_mem_w_q', 'm_mem_w_kv', 'm_mem_w_o', 'm_ffn2_norm', 'm_ffn2_w_gate', 'm_ffn2_w_up', 'm_ffn2_w_down', 'm_final_norm', 'v_ffn1_norm', 'v_ffn1_w_gate', 'v_ffn1_w_up', 'v_ffn1_w_down', 'v_mix_norm', 'v_w_in', 'v_w_pool', 'v_pool_scale', 'v_w_out', 'v_mem_q_norm', 'v_mem_kv_norm', 'v_mem_w_q', 'v_mem_w_kv', 'v_mem_w_o', 'v_ffn2_norm', 'v_ffn2_w_gate', 'v_ffn2_w_up', 'v_ffn2_w_down', 'v_final_norm']
TWIN_OUTPUTS = ['loss', 'grad_x', 'grad_ffn1_norm', 'grad_ffn1_w_gate', 'grad_ffn1_w_up', 'grad_ffn1_w_down', 'grad_mix_norm', 'grad_w_in', 'grad_w_pool', 'grad_pool_scale', 'grad_w_out', 'grad_mem_q_norm', 'grad_mem_kv_norm', 'grad_mem_w_q', 'grad_mem_w_kv', 'grad_mem_w_o', 'grad_ffn2_norm', 'grad_ffn2_w_gate', 'grad_ffn2_w_up', 'grad_ffn2_w_down', 'grad_final_norm', 'delta_ffn1_norm', 'delta_ffn1_w_gate', 'delta_ffn1_w_up', 'delta_ffn1_w_down', 'delta_mix_norm', 'delta_w_in', 'delta_w_pool', 'delta_pool_scale', 'delta_w_out', 'delta_mem_q_norm', 'delta_mem_kv_norm', 'delta_mem_w_q', 'delta_mem_w_kv', 'delta_mem_w_o', 'delta_ffn2_norm', 'delta_ffn2_w_gate', 'delta_ffn2_w_up', 'delta_ffn2_w_down', 'delta_final_norm', 'new_m_ffn1_norm', 'new_m_ffn1_w_gate', 'new_m_ffn1_w_up', 'new_m_ffn1_w_down', 'new_m_mix_norm', 'new_m_w_in', 'new_m_w_pool', 'new_m_pool_scale', 'new_m_w_out', 'new_m_mem_q_norm', 'new_m_mem_kv_norm', 'new_m_mem_w_q', 'new_m_mem_w_kv', 'new_m_mem_w_o', 'new_m_ffn2_norm', 'new_m_ffn2_w_gate', 'new_m_ffn2_w_up', 'new_m_ffn2_w_down', 'new_m_final_norm', 'new_v_ffn1_norm', 'new_v_ffn1_w_gate', 'new_v_ffn1_w_up', 'new_v_ffn1_w_down', 'new_v_mix_norm', 'new_v_w_in', 'new_v_w_pool', 'new_v_pool_scale', 'new_v_w_out', 'new_v_mem_q_norm', 'new_v_mem_kv_norm', 'new_v_mem_w_q', 'new_v_mem_w_kv', 'new_v_mem_w_o', 'new_v_ffn2_norm', 'new_v_ffn2_w_gate', 'new_v_ffn2_w_up', 'new_v_ffn2_w_down', 'new_v_final_norm']
TWIN_LEAF_KINDS = {'loss': 'loss', 'grad_x': 'grad_x', 'grad_ffn1_norm': 'grad_w', 'grad_ffn1_w_gate': 'grad_w', 'grad_ffn1_w_up': 'grad_w', 'grad_ffn1_w_down': 'grad_w', 'grad_mix_norm': 'grad_w', 'grad_w_in': 'grad_w', 'grad_w_pool': 'grad_w', 'grad_pool_scale': 'grad_w', 'grad_w_out': 'grad_w', 'grad_mem_q_norm': 'grad_w', 'grad_mem_kv_norm': 'grad_w', 'grad_mem_w_q': 'grad_w', 'grad_mem_w_kv': 'grad_w', 'grad_mem_w_o': 'grad_w', 'grad_ffn2_norm': 'grad_w', 'grad_ffn2_w_gate': 'grad_w', 'grad_ffn2_w_up': 'grad_w', 'grad_ffn2_w_down': 'grad_w', 'grad_final_norm': 'grad_w', 'delta_ffn1_norm': 'delta_w', 'delta_ffn1_w_gate': 'delta_w', 'delta_ffn1_w_up': 'delta_w', 'delta_ffn1_w_down': 'delta_w', 'delta_mix_norm': 'delta_w', 'delta_w_in': 'delta_w', 'delta_w_pool': 'delta_w', 'delta_pool_scale': 'delta_w', 'delta_w_out': 'delta_w', 'delta_mem_q_norm': 'delta_w', 'delta_mem_kv_norm': 'delta_w', 'delta_mem_w_q': 'delta_w', 'delta_mem_w_kv': 'delta_w', 'delta_mem_w_o': 'delta_w', 'delta_ffn2_norm': 'delta_w', 'delta_ffn2_w_gate': 'delta_w', 'delta_ffn2_w_up': 'delta_w', 'delta_ffn2_w_down': 'delta_w', 'delta_final_norm': 'delta_w', 'new_m_ffn1_norm': 'new_m', 'new_m_ffn1_w_gate': 'new_m', 'new_m_ffn1_w_up': 'new_m', 'new_m_ffn1_w_down': 'new_m', 'new_m_mix_norm': 'new_m', 'new_m_w_in': 'new_m', 'new_m_w_pool': 'new_m', 'new_m_pool_scale': 'new_m', 'new_m_w_out': 'new_m', 'new_m_mem_q_norm': 'new_m', 'new_m_mem_kv_norm': 'new_m', 'new_m_mem_w_q': 'new_m', 'new_m_mem_w_kv': 'new_m', 'new_m_mem_w_o': 'new_m', 'new_m_ffn2_norm': 'new_m', 'new_m_ffn2_w_gate': 'new_m', 'new_m_ffn2_w_up': 'new_m', 'new_m_ffn2_w_down': 'new_m', 'new_m_final_norm': 'new_m', 'new_v_ffn1_norm': 'new_v', 'new_v_ffn1_w_gate': 'new_v', 'new_v_ffn1_w_up': 'new_v', 'new_v_ffn1_w_down': 'new_v', 'new_v_mix_norm': 'new_v', 'new_v_w_in': 'new_v', 'new_v_w_pool': 'new_v', 'new_v_pool_scale': 'new_v', 'new_v_w_out': 'new_v', 'new_v_mem_q_norm': 'new_v', 'new_v_mem_kv_norm': 'new_v', 'new_v_mem_w_q': 'new_v', 'new_v_mem_w_kv': 'new_v', 'new_v_mem_w_o': 'new_v', 'new_v_ffn2_norm': 'new_v', 'new_v_ffn2_w_gate': 'new_v', 'new_v_ffn2_w_up': 'new_v', 'new_v_ffn2_w_down': 'new_v', 'new_v_final_norm': 'new_v'}


def _forward(args):
    return _fwd_reference(*[args[k] for k in FWD_PARAMS])


def _output_shape():
    out = _jax.eval_shape(lambda: _forward(_fwd_setup_inputs(0)))
    return out.shape, out.dtype

N_MICROBATCH = 1
ADAM_LR = 0.001
ADAM_B1 = 0.9
ADAM_B2 = 0.999
ADAM_EPS = 1e-08
ADAM_WD = 0.01
ADAM_STEP = 10
PER_EXAMPLE_BATCH_AXIS = {'x': 0, 'mem': 0, 'loss_target': 0}
SHARED_INPUTS = []
_WEIGHT_DTYPES = {'ffn1_norm': _jnp.float32, 'ffn1_w_gate': _jnp.float32, 'ffn1_w_up': _jnp.float32, 'ffn1_w_down': _jnp.float32, 'mix_norm': _jnp.float32, 'w_in': _jnp.float32, 'w_pool': _jnp.float32, 'pool_scale': _jnp.float32, 'w_out': _jnp.float32, 'mem_q_norm': _jnp.float32, 'mem_kv_norm': _jnp.float32, 'mem_w_q': _jnp.float32, 'mem_w_kv': _jnp.float32, 'mem_w_o': _jnp.float32, 'ffn2_norm': _jnp.float32, 'ffn2_w_gate': _jnp.float32, 'ffn2_w_up': _jnp.float32, 'ffn2_w_down': _jnp.float32, 'final_norm': _jnp.float32}
MOMENT_SCALE = {'ffn1_norm': 1.110492e-01, 'ffn1_w_gate': 4.692550e-02, 'ffn1_w_up': 4.550887e-02, 'ffn1_w_down': 7.541944e-02, 'mix_norm': 1.637059e-01, 'w_in': 1.162906e-01, 'w_pool': 1.760122e-01, 'pool_scale': 1.645649e-01, 'w_out': 1.533624e-01, 'mem_q_norm': 2.119944e-02, 'mem_kv_norm': 3.911844e-02, 'mem_w_q': 2.099411e-02, 'mem_w_kv': 2.146203e-02, 'mem_w_o': 2.113651e-02, 'ffn2_norm': 8.203138e-02, 'ffn2_w_gate': 3.512183e-02, 'ffn2_w_up': 3.395971e-02, 'ffn2_w_down': 5.635745e-02, 'final_norm': 6.404764e+01}


def _to_microbatches(a, axis):
    t = _jnp.moveaxis(a, axis, 0)
    t = t.reshape((N_MICROBATCH, t.shape[0] // N_MICROBATCH) + t.shape[1:])
    return _jnp.moveaxis(t, 1, axis + 1)


def setup_inputs(seed: int = 0) -> dict:
    inp = _fwd_setup_inputs(seed)
    key = _jax.random.fold_in(_jax.random.key(seed), 7919)
    shape, _ = _output_shape()
    out = dict(inp)
    out["loss_target"] = _jax.random.normal(_jax.random.fold_in(key, 0), shape, _jnp.float32)
    for i, name in enumerate(TWIN_WEIGHTS):
        w = inp[name].astype(_jnp.float32)
        if MOMENT_SCALE is None:
            s = _jnp.sqrt(_jnp.mean(_jnp.square(w)) + 1e-30)
        else:
            s = MOMENT_SCALE[name]
        km, kv = _jax.random.split(_jax.random.fold_in(key, i + 1))
        out[name] = w
        out["m_" + name] = s * _jax.random.normal(km, w.shape, _jnp.float32)
        out["v_" + name] = (s * s) * _jax.random.uniform(kv, w.shape, _jnp.float32, 0.5, 1.5)
    if N_MICROBATCH > 1:
        for name, axis in PER_EXAMPLE_BATCH_AXIS.items():
            out[name] = _to_microbatches(out[name], axis)
    return {'x': out['x'], 'mem': out['mem'], 'ffn1_norm': out['ffn1_norm'], 'ffn1_w_gate': out['ffn1_w_gate'], 'ffn1_w_up': out['ffn1_w_up'], 'ffn1_w_down': out['ffn1_w_down'], 'mix_norm': out['mix_norm'], 'w_in': out['w_in'], 'w_pool': out['w_pool'], 'pool_scale': out['pool_scale'], 'w_out': out['w_out'], 'mem_q_norm': out['mem_q_norm'], 'mem_kv_norm': out['mem_kv_norm'], 'mem_w_q': out['mem_w_q'], 'mem_w_kv': out['mem_w_kv'], 'mem_w_o': out['mem_w_o'], 'ffn2_norm': out['ffn2_norm'], 'ffn2_w_gate': out['ffn2_w_gate'], 'ffn2_w_up': out['ffn2_w_up'], 'ffn2_w_down': out['ffn2_w_down'], 'final_norm': out['final_norm'], 'loss_target': out['loss_target'], 'm_ffn1_norm': out['m_ffn1_norm'], 'm_ffn1_w_gate': out['m_ffn1_w_gate'], 'm_ffn1_w_up': out['m_ffn1_w_up'], 'm_ffn1_w_down': out['m_ffn1_w_down'], 'm_mix_norm': out['m_mix_norm'], 'm_w_in': out['m_w_in'], 'm_w_pool': out['m_w_pool'], 'm_pool_scale': out['m_pool_scale'], 'm_w_out': out['m_w_out'], 'm_mem_q_norm': out['m_mem_q_norm'], 'm_mem_kv_norm': out['m_mem_kv_norm'], 'm_mem_w_q': out['m_mem_w_q'], 'm_mem_w_kv': out['m_mem_w_kv'], 'm_mem_w_o': out['m_mem_w_o'], 'm_ffn2_norm': out['m_ffn2_norm'], 'm_ffn2_w_gate': out['m_ffn2_w_gate'], 'm_ffn2_w_up': out['m_ffn2_w_up'], 'm_ffn2_w_down': out['m_ffn2_w_down'], 'm_final_norm': out['m_final_norm'], 'v_ffn1_norm': out['v_ffn1_norm'], 'v_ffn1_w_gate': out['v_ffn1_w_gate'], 'v_ffn1_w_up': out['v_ffn1_w_up'], 'v_ffn1_w_down': out['v_ffn1_w_down'], 'v_mix_norm': out['v_mix_norm'], 'v_w_in': out['v_w_in'], 'v_w_pool': out['v_w_pool'], 'v_pool_scale': out['v_pool_scale'], 'v_w_out': out['v_w_out'], 'v_mem_q_norm': out['v_mem_q_norm'], 'v_mem_kv_norm': out['v_mem_kv_norm'], 'v_mem_w_q': out['v_mem_w_q'], 'v_mem_w_kv': out['v_mem_w_kv'], 'v_mem_w_o': out['v_mem_w_o'], 'v_ffn2_norm': out['v_ffn2_norm'], 'v_ffn2_w_gate': out['v_ffn2_w_gate'], 'v_ffn2_w_up': out['v_ffn2_w_up'], 'v_ffn2_w_down': out['v_ffn2_w_down'], 'v_final_norm': out['v_final_norm']}


def _loss(weights, diff, rest, loss_target):
    with _jax.named_scope("forward"):
        args = {**rest, TWIN_DIFF_INPUT: diff, **{k: w.astype(_WEIGHT_DTYPES[k]) for k, w in weights.items()}}
        y = _forward(args)
    with _jax.named_scope("loss_head"):
        err = _jnp.square(y.astype(_jnp.float32) - loss_target)
        return 0.5 * _jnp.sum(_jnp.mean(err, axis=-1)) if err.ndim else 0.5 * err


def _adamw(w, g, m, v):
    m = ADAM_B1 * m + (1.0 - ADAM_B1) * g
    v = ADAM_B2 * v + (1.0 - ADAM_B2) * _jnp.square(g)
    m_hat = m / (1.0 - ADAM_B1 ** ADAM_STEP)
    v_hat = v / (1.0 - ADAM_B2 ** ADAM_STEP)
    delta = -ADAM_LR * (m_hat / (_jnp.sqrt(v_hat) + ADAM_EPS) + ADAM_WD * w)
    return delta, m, v


def reference(x, mem, ffn1_norm, ffn1_w_gate, ffn1_w_up, ffn1_w_down, mix_norm, w_in, w_pool, pool_scale, w_out, mem_q_norm, mem_kv_norm, mem_w_q, mem_w_kv, mem_w_o, ffn2_norm, ffn2_w_gate, ffn2_w_up, ffn2_w_down, final_norm, loss_target, m_ffn1_norm, m_ffn1_w_gate, m_ffn1_w_up, m_ffn1_w_down, m_mix_norm, m_w_in, m_w_pool, m_pool_scale, m_w_out, m_mem_q_norm, m_mem_kv_norm, m_mem_w_q, m_mem_w_kv, m_mem_w_o, m_ffn2_norm, m_ffn2_w_gate, m_ffn2_w_up, m_ffn2_w_down, m_final_norm, v_ffn1_norm, v_ffn1_w_gate, v_ffn1_w_up, v_ffn1_w_down, v_mix_norm, v_w_in, v_w_pool, v_pool_scale, v_w_out, v_mem_q_norm, v_mem_kv_norm, v_mem_w_q, v_mem_w_kv, v_mem_w_o, v_ffn2_norm, v_ffn2_w_gate, v_ffn2_w_up, v_ffn2_w_down, v_final_norm):
    given = dict(x=x, mem=mem, ffn1_norm=ffn1_norm, ffn1_w_gate=ffn1_w_gate, ffn1_w_up=ffn1_w_up, ffn1_w_down=ffn1_w_down, mix_norm=mix_norm, w_in=w_in, w_pool=w_pool, pool_scale=pool_scale, w_out=w_out, mem_q_norm=mem_q_norm, mem_kv_norm=mem_kv_norm, mem_w_q=mem_w_q, mem_w_kv=mem_w_kv, mem_w_o=mem_w_o, ffn2_norm=ffn2_norm, ffn2_w_gate=ffn2_w_gate, ffn2_w_up=ffn2_w_up, ffn2_w_down=ffn2_w_down, final_norm=final_norm, loss_target=loss_target, m_ffn1_norm=m_ffn1_norm, m_ffn1_w_gate=m_ffn1_w_gate, m_ffn1_w_up=m_ffn1_w_up, m_ffn1_w_down=m_ffn1_w_down, m_mix_norm=m_mix_norm, m_w_in=m_w_in, m_w_pool=m_w_pool, m_pool_scale=m_pool_scale, m_w_out=m_w_out, m_mem_q_norm=m_mem_q_norm, m_mem_kv_norm=m_mem_kv_norm, m_mem_w_q=m_mem_w_q, m_mem_w_kv=m_mem_w_kv, m_mem_w_o=m_mem_w_o, m_ffn2_norm=m_ffn2_norm, m_ffn2_w_gate=m_ffn2_w_gate, m_ffn2_w_up=m_ffn2_w_up, m_ffn2_w_down=m_ffn2_w_down, m_final_norm=m_final_norm, v_ffn1_norm=v_ffn1_norm, v_ffn1_w_gate=v_ffn1_w_gate, v_ffn1_w_up=v_ffn1_w_up, v_ffn1_w_down=v_ffn1_w_down, v_mix_norm=v_mix_norm, v_w_in=v_w_in, v_w_pool=v_w_pool, v_pool_scale=v_pool_scale, v_w_out=v_w_out, v_mem_q_norm=v_mem_q_norm, v_mem_kv_norm=v_mem_kv_norm, v_mem_w_q=v_mem_w_q, v_mem_w_kv=v_mem_w_kv, v_mem_w_o=v_mem_w_o, v_ffn2_norm=v_ffn2_norm, v_ffn2_w_gate=v_ffn2_w_gate, v_ffn2_w_up=v_ffn2_w_up, v_ffn2_w_down=v_ffn2_w_down, v_final_norm=v_final_norm)
    weights = {n: given[n] for n in TWIN_WEIGHTS}
    shared = {n: given[n] for n in SHARED_INPUTS}
    per_example = {n: given[n] for n in ['x', 'mem']}
    grad_fn = _jax.value_and_grad(_loss, argnums=(0, 1))

    def one_microbatch(ex, loss_target):
        ex = dict(ex)
        diff = ex.pop(TWIN_DIFF_INPUT)
        return grad_fn(weights, diff, {**shared, **ex}, loss_target)

    if N_MICROBATCH == 1:
        loss, (grad_w, grad_x) = one_microbatch(per_example, given["loss_target"])
    else:
        def body(carry, xs):
            loss_sum, grad_sum = carry
            l_k, (gw_k, gx_k) = one_microbatch(xs[0], xs[1])
            with _jax.named_scope("update"):
                return (loss_sum + l_k, _jax.tree.map(_jnp.add, grad_sum, gw_k)), gx_k

        init = (_jnp.zeros((), _jnp.float32), _jax.tree.map(_jnp.zeros_like, weights))
        (loss, grad_w), grad_x = _jax.lax.scan(body, init, (per_example, given["loss_target"]))
    with _jax.named_scope("update"):
        delta_w, new_m, new_v = {}, {}, {}
        for n in TWIN_WEIGHTS:
            delta_w[n], new_m[n], new_v[n] = _adamw(weights[n], grad_w[n], given["m_" + n], given["v_" + n])
    return (loss, grad_x, *[grad_w[n] for n in TWIN_WEIGHTS], *[delta_w[n] for n in TWIN_WEIGHTS],
            *[new_m[n] for n in TWIN_WEIGHTS], *[new_v[n] for n in TWIN_WEIGHTS])
```

```python
import functools

import jax
import jax.numpy as jnp
from jax import lax
from jax.experimental import pallas as pl
from jax.experimental.pallas import tpu as pltpu

F32 = jnp.float32
BF16 = jnp.bfloat16

N_DEV = 8
EPS = 1e-6
SB_HEAD_DIM = 64
LANES = 128
POOL_WINDOWS = (2, 4, 8, 16)
POOL_GROUP_DIM = 128
MEM_HEADS = 4
FFN_RESIDUAL_WEIGHT = 0.5
ADAM_LR = 0.001
ADAM_B1 = 0.9
ADAM_B2 = 0.999
ADAM_EPS = 1e-08
ADAM_WD = 0.01
ADAM_STEP = 10
VMEM_LIMIT = 56 * 1024 * 1024

MESH_ID = pl.DeviceIdType.MESH


def _params(*sem):
    return pltpu.CompilerParams(dimension_semantics=sem, vmem_limit_bytes=VMEM_LIMIT)


def _tile(n, pref):
    if n <= pref:
        return n
    t = pref - pref % 8
    while n % t:
        t -= 8
    return t


def _mm(a, b):
    return jnp.dot(a, b, preferred_element_type=F32)


def _mm_nt(a, b):
    return lax.dot_general(a, b, (((1,), (1,)), ((), ())), preferred_element_type=F32)


def _mm_tn(a, b):
    return lax.dot_general(a, b, (((0,), (0,)), ((), ())), preferred_element_type=F32)


def _rms(xv):
    r = lax.rsqrt(jnp.mean(xv * xv, axis=-1, keepdims=True) + EPS)
    return r, xv * r


def _rms_bwd(dhn, gain, r, xhat):
    dxh = dhn * gain
    return r * (dxh - xhat * jnp.mean(dxh * xhat, axis=-1, keepdims=True))


def _sigmoid(z):
    return 1.0 / (1.0 + jnp.exp(-z))


def _exchange(arrs, gather, name):
    n = len(arrs)
    out_shape = []
    for a in arrs:
        shp = (N_DEV,) + tuple(a.shape) if gather else tuple(a.shape)
        out_shape.append(jax.ShapeDtypeStruct(shp, a.dtype))

    def body(*refs):
        ins, outs = refs[:n], refs[n:2 * n]
        send_sems, recv_sems, local_sems = refs[2 * n:]
        x, y, c = lax.axis_index("x"), lax.axis_index("y"), lax.axis_index("c")
        me = 4 * x + 2 * y + c
        local = []
        for i in range(n):
            src = ins[i] if gather else ins[i].at[me]
            cp = pltpu.make_async_copy(src, outs[i].at[me], local_sems.at[i])
            cp.start()
            local.append(cp)
        for k in range(1, N_DEV):
            px = 1 - x if k & 4 else x
            py = 1 - y if k & 2 else y
            pc = 1 - c if k & 1 else c
            peer = 4 * px + 2 * py + pc
            for i in range(n):
                src = ins[i] if gather else ins[i].at[peer]
                pltpu.make_async_remote_copy(
                    src_ref=src, dst_ref=outs[i].at[me],
                    send_sem=send_sems.at[i], recv_sem=recv_sems.at[i],
                    device_id=(px, py, pc), device_id_type=MESH_ID).start()
        for i in range(n):
            seven = outs[i].at[pl.ds(0, N_DEV - 1)]
            done = pltpu.make_async_remote_copy(
                src_ref=seven, dst_ref=seven,
                send_sem=send_sems.at[i], recv_sem=recv_sems.at[i],
                device_id=(x, y, c), device_id_type=MESH_ID)
            done.wait_send()
            done.wait_recv()
            local[i].wait()

    any_spec = pl.BlockSpec(memory_space=pl.ANY)
    outs = pl.pallas_call(
        body, name=name, out_shape=tuple(out_shape),
        in_specs=[any_spec] * n, out_specs=tuple([any_spec] * n),
        scratch_shapes=[pltpu.SemaphoreType.DMA((n,)), pltpu.SemaphoreType.DMA((n,)),
                        pltpu.SemaphoreType.DMA((n,))],
    )(*arrs)
    return list(outs)


def _ffn_fwd(x, gain, wgt, wut, wd, name):
    T, D = x.shape
    F = wd.shape[0]
    tm, tf = _tile(T, 1024), _tile(F, 256)
    nj = F // tf

    def body(x_ref, gain_ref, wg_ref, wu_ref, wd_ref, out_ref, hn_ref, g_ref, u_ref, acc, hn_s):
        j = pl.program_id(1)

        @pl.when(j == 0)
        def _():
            _, xhat = _rms(x_ref[...])
            hn = (xhat * gain_ref[...]).astype(BF16)
            hn_s[...] = hn
            hn_ref[...] = hn
            acc[...] = jnp.zeros_like(acc)

        hn = hn_s[...]
        g = _mm_nt(hn, wg_ref[...])
        u = _mm_nt(hn, wu_ref[...])
        g_ref[...] = g.astype(BF16)
        u_ref[...] = u.astype(BF16)
        a = (g * _sigmoid(g) * u).astype(BF16)
        acc[...] += _mm(a, wd_ref[...])

        @pl.when(j == nj - 1)
        def _():
            out_ref[...] = x_ref[...] + FFN_RESIDUAL_WEIGHT * acc[...]

    row = lambda i, j: (i, 0)
    wspec = pl.BlockSpec((tf, D), lambda i, j: (j, 0))
    return pl.pallas_call(
        body, name=name, grid=(T // tm, nj),
        in_specs=[pl.BlockSpec((tm, D), row), pl.BlockSpec((1, D), lambda i, j: (0, 0)), wspec, wspec, wspec],
        out_specs=(pl.BlockSpec((tm, D), row), pl.BlockSpec((tm, D), row),
                   pl.BlockSpec((tm, tf), lambda i, j: (i, j)), pl.BlockSpec((tm, tf), lambda i, j: (i, j))),
        out_shape=(jax.ShapeDtypeStruct((T, D), F32), jax.ShapeDtypeStruct((T, D), BF16),
                   jax.ShapeDtypeStruct((T, F), BF16), jax.ShapeDtypeStruct((T, F), BF16)),
        scratch_shapes=[pltpu.VMEM((tm, D), F32), pltpu.VMEM((tm, D), BF16)],
        compiler_params=_params("arbitrary", "arbitrary"),
    )(x, gain, wgt, wut, wd)


def _ffn_bwd(dy, x, gain, g, u, wgt, wut, wd, name):
    T, D = x.shape
    F = wd.shape[0]
    tm, tf = _tile(T, 1024), _tile(F, 256)
    nj = F // tf

    def body(dy_ref, x_ref, gain_ref, g_ref, u_ref, wg_ref, wu_ref, wd_ref,
             dx_ref, a_ref, dg_ref, du_ref, dyh_ref, dgain_ref, acc, dyh_s):
        i, j = pl.program_id(0), pl.program_id(1)

        @pl.when(j == 0)
        def _():
            dyh = (FFN_RESIDUAL_WEIGHT * dy_ref[...]).astype(BF16)
            dyh_s[...] = dyh
            dyh_ref[...] = dyh
            acc[...] = jnp.zeros_like(acc)

        @pl.when((i == 0) & (j == 0))
        def _():
            dgain_ref[...] = jnp.zeros_like(dgain_ref)

        gv = g_ref[...].astype(F32)
        uv = u_ref[...].astype(F32)
        da = _mm_nt(dyh_s[...], wd_ref[...])
        sig = _sigmoid(gv)
        s = gv * sig
        a_ref[...] = (s * uv).astype(BF16)
        dg = (da * uv * (sig * (1.0 + gv * (1.0 - sig)))).astype(BF16)
        du = (da * s).astype(BF16)
        dg_ref[...] = dg
        du_ref[...] = du
        acc[...] += _mm(dg, wg_ref[...]) + _mm(du, wu_ref[...])

        @pl.when(j == nj - 1)
        def _():
            r, xhat = _rms(x_ref[...])
            dhn = acc[...]
            dgain_ref[...] += jnp.sum(dhn * xhat, axis=0, keepdims=True)
            dx_ref[...] = dy_ref[...] + _rms_bwd(dhn, gain_ref[...], r, xhat)

    row = lambda i, j: (i, 0)
    tile = lambda i, j: (i, j)
    wspec = pl.BlockSpec((tf, D), lambda i, j: (j, 0))
    one = pl.BlockSpec((1, D), lambda i, j: (0, 0))
    return pl.pallas_call(
        body, name=name, grid=(T // tm, nj),
        in_specs=[pl.BlockSpec((tm, D), row), pl.BlockSpec((tm, D), row), one,
                  pl.BlockSpec((tm, tf), tile), pl.BlockSpec((tm, tf), tile), wspec, wspec, wspec],
        out_specs=(pl.BlockSpec((tm, D), row), pl.BlockSpec((tm, tf), tile), pl.BlockSpec((tm, tf), tile),
                   pl.BlockSpec((tm, tf), tile), pl.BlockSpec((tm, D), row), one),
        out_shape=(jax.ShapeDtypeStruct((T, D), F32), jax.ShapeDtypeStruct((T, F), BF16),
                   jax.ShapeDtypeStruct((T, F), BF16), jax.ShapeDtypeStruct((T, F), BF16),
                   jax.ShapeDtypeStruct((T, D), BF16), jax.ShapeDtypeStruct((1, D), F32)),
        scratch_shapes=[pltpu.VMEM((tm, D), F32), pltpu.VMEM((tm, D), BF16)],
        compiler_params=_params("arbitrary", "arbitrary"),
    )(dy, x, gain, g, u, wgt, wut, wd)


def _wgrad(a, b, name, col_slab=None):
    T, M = a.shape
    N = b.shape[1]
    tmm = M if M <= 1024 else _tile(M, 1408)
    tn = col_slab if col_slab else _tile(N, 1024)
    tk = _tile(T, 512)
    nk = T // tk

    def body(a_ref, b_ref, out_ref, acc):
        k = pl.program_id(2)

        @pl.when(k == 0)
        def _():
            acc[...] = jnp.zeros_like(acc)

        acc[...] += _mm_tn(a_ref[...], b_ref[...])

        @pl.when(k == nk - 1)
        def _():
            out_ref[...] = acc[...].astype(BF16)

    if col_slab:
        out_spec = pl.BlockSpec((None, tmm, tn), lambda m, n, k: (n, m, 0))
        out_shape = jax.ShapeDtypeStruct((N // tn, M, tn), BF16)
    else:
        out_spec = pl.BlockSpec((tmm, tn), lambda m, n, k: (m, n))
        out_shape = jax.ShapeDtypeStruct((M, N), BF16)
    return pl.pallas_call(
        body, name=name, grid=(M // tmm, N // tn, nk),
        in_specs=[pl.BlockSpec((tk, tmm), lambda m, n, k: (k, m)), pl.BlockSpec((tk, tn), lambda m, n, k: (k, n))],
        out_specs=out_spec, out_shape=out_shape,
        scratch_shapes=[pltpu.VMEM((tmm, tn), F32)],
        compiler_params=_params("arbitrary", "arbitrary", "arbitrary"),
    )(a, b)


def _mix_in_fwd(x, gain, w_in):
    T, D = x.shape
    cs = w_in.shape[2]
    n_qkv = 3 * (N_DEV // 4)
    tm = _tile(T, 512)

    def body(x_ref, gain_ref, w_ref, hn_ref, qkv_ref, u_ref):
        _, xhat = _rms(x_ref[...])
        hn = (xhat * gain_ref[...]).astype(BF16)
        hn_ref[...] = hn
        for b in range(N_DEV):
            p = _mm(hn, w_ref[b])
            if b < n_qkv:
                qkv_ref[:, b * cs:(b + 1) * cs] = p.astype(BF16)
            else:
                u_ref[:, (b - n_qkv) * cs:(b - n_qkv + 1) * cs] = p

    row = lambda i: (i, 0)
    return pl.pallas_call(
        body, name="mix_in_fwd", grid=(T // tm,),
        in_specs=[pl.BlockSpec((tm, D), row), pl.BlockSpec((1, D), lambda i: (0, 0)),
                  pl.BlockSpec((N_DEV, D, cs), lambda i: (0, 0, 0))],
        out_specs=(pl.BlockSpec((tm, D), row), pl.BlockSpec((tm, n_qkv * cs), row),
                   pl.BlockSpec((tm, (N_DEV - n_qkv) * cs), row)),
        out_shape=(jax.ShapeDtypeStruct((T, D), BF16), jax.ShapeDtypeStruct((T, n_qkv * cs), BF16),
                   jax.ShapeDtypeStruct((T, (N_DEV - n_qkv) * cs), F32)),
        compiler_params=_params("arbitrary"),
    )(x, gain, w_in)


def _mix_in_bwd(dres, dq, dk, dv, du, x, gain, w_in):
    T, D = x.shape
    cs = w_in.shape[2]
    W = dq.shape[1]
    per = W // cs
    tm = _tile(T, 512)

    def body(dres_ref, dq_ref, dk_ref, dv_ref, du_ref, x_ref, gain_ref, w_ref, dx_ref, dproj_ref, dgain_ref):
        i = pl.program_id(0)

        @pl.when(i == 0)
        def _():
            dgain_ref[...] = jnp.zeros_like(dgain_ref)

        dhn = jnp.zeros((tm, D), F32)
        for part, ref in enumerate((dq_ref, dk_ref, dv_ref, du_ref)):
            for h in range(per):
                b = part * per + h
                d = ref[:, h * cs:(h + 1) * cs]
                dproj_ref[:, b * cs:(b + 1) * cs] = d
                dhn = dhn + _mm_nt(d, w_ref[b])
        r, xhat = _rms(x_ref[...])
        dgain_ref[...] += jnp.sum(dhn * xhat, axis=0, keepdims=True)
        dx_ref[...] = dres_ref[...] + _rms_bwd(dhn, gain_ref[...], r, xhat)

    row = lambda i: (i, 0)
    one = pl.BlockSpec((1, D), lambda i: (0, 0))
    part = pl.BlockSpec((tm, W), row)
    return pl.pallas_call(
        body, name="mix_in_bwd", grid=(T // tm,),
        in_specs=[pl.BlockSpec((tm, D), row), part, part, part, part, pl.BlockSpec((tm, D), row), one,
                  pl.BlockSpec((N_DEV, D, cs), lambda i: (0, 0, 0))],
        out_specs=(pl.BlockSpec((tm, D), row), pl.BlockSpec((tm, 4 * W), row), one),
        out_shape=(jax.ShapeDtypeStruct((T, D), F32), jax.ShapeDtypeStruct((T, 4 * W), BF16),
                   jax.ShapeDtypeStruct((1, D), F32)),
        compiler_params=_params("arbitrary"),
    )(dres, dq, dk, dv, du, x, gain, w_in)


def _log_sigmoid(z):
    return jnp.minimum(z, 0.0) - jnp.log1p(jnp.exp(-jnp.abs(z)))


def _split(v):
    hi = v.astype(BF16)
    return hi, (v - hi.astype(F32)).astype(BF16)


def _tri_sum(v, tri):
    hi, lo = _split(v)
    return _mm(hi, tri) + _mm(lo, tri)


def _sb_fwd(qkv, B, S):
    W = qkv.shape[2] // 3
    n_pair = W // LANES
    bq = _tile(S, 256)
    nq = S // bq
    scale = SB_HEAD_DIM ** -0.5

    def body(q_ref, k_ref, v_ref, o_ref, lt_ref):
        lane = lax.broadcasted_iota(jnp.int32, (1, LANES), 1)
        head0 = lane < SB_HEAD_DIM
        rr = lax.broadcasted_iota(jnp.int32, (bq, bq), 0)
        cc = lax.broadcasted_iota(jnp.int32, (bq, bq), 1)
        strict = cc < rr
        after = jnp.where(rr > cc, 1.0, 0.0).astype(BF16)

        def block(qh, ks, carry, diag):
            o_acc, c = carry
            z = _mm_nt(qh, k_ref[ks, :]) * scale
            ls = _log_sigmoid(z)
            lr = ls - z
            if diag:
                lr = jnp.where(strict, lr, 0.0)
            suf = _tri_sum(lr, after) + c
            a = jnp.exp(ls + suf)
            if diag:
                a = jnp.where(strict, a, 0.0)
            o_acc = o_acc + _mm(a.astype(BF16), v_ref[ks, :])
            return o_acc, c + jnp.sum(lr, axis=1, keepdims=True)

        def q_tile(i, _):
            qs = pl.ds(pl.multiple_of(i * bq, bq), bq)
            qv = q_ref[qs, :]
            q0 = jnp.where(head0, qv, jnp.zeros_like(qv))
            q1 = jnp.where(head0, jnp.zeros_like(qv), qv)
            zero = (jnp.zeros((bq, LANES), F32), jnp.zeros((bq, 1), F32))
            c0 = block(q0, qs, zero, True)
            c1 = block(q1, qs, zero, True)

            def left(t, cr):
                ks = pl.ds(pl.multiple_of((i - 1 - t) * bq, bq), bq)
                return block(q0, ks, cr[0], False), block(q1, ks, cr[1], False)

            (o0, l0), (o1, l1) = lax.fori_loop(0, i, left, (c0, c1))
            o_ref[qs, :] = jnp.where(head0, o0, o1).astype(BF16)
            lt_ref[qs, :] = jnp.where(head0, l0, l1)
            return 0

        lax.fori_loop(0, nq, q_tile, 0)

    def col(off):
        return pl.BlockSpec((None, S, LANES), lambda b, p: (b, 0, off + p))

    return pl.pallas_call(
        body, name="sb_fwd", grid=(B, n_pair),
        in_specs=[col(0), col(n_pair), col(2 * n_pair)],
        out_specs=(col(0), col(0)),
        out_shape=(jax.ShapeDtypeStruct((B, S, W), BF16), jax.ShapeDtypeStruct((B, S, W), F32)),
        compiler_params=_params("arbitrary", "arbitrary"),
    )(qkv, qkv, qkv)


def _sb_bwd(qkv, do, ltot, B, S):
    W = qkv.shape[2] // 3
    n_pair = W // LANES
    bq = _tile(S, 256)
    nq = S // bq
    scale = SB_HEAD_DIM ** -0.5

    def body(q_ref, k_ref, v_ref, do_ref, lt_ref, dq_ref, dk_ref, dv_ref, dk_s, dv_s):
        lane = lax.broadcasted_iota(jnp.int32, (1, LANES), 1)
        head0 = lane < SB_HEAD_DIM
        rr = lax.broadcasted_iota(jnp.int32, (bq, bq), 0)
        cc = lax.broadcasted_iota(jnp.int32, (bq, bq), 1)
        strict = cc < rr
        upto = jnp.where(rr <= cc, 1.0, 0.0).astype(BF16)
        before = jnp.where(rr < cc, 1.0, 0.0).astype(BF16)
        dk_s[...] = jnp.zeros_like(dk_s)
        dv_s[...] = jnp.zeros_like(dv_s)

        def block(qh, doh, lt, ks, carry, diag):
            dq_acc, pl_sum, pe_sum = carry
            kk = k_ref[ks, :]
            z = _mm_nt(qh, kk) * scale
            ls = _log_sigmoid(z)
            lr = ls - z
            if diag:
                lr = jnp.where(strict, lr, 0.0)
            suf = lt - (_tri_sum(lr, upto) + pl_sum)
            a = jnp.exp(ls + suf)
            if diag:
                a = jnp.where(strict, a, 0.0)
            e = a * _mm_nt(doh, v_ref[ks, :])
            p_excl = _tri_sum(e, before) + pe_sum
            sig = jnp.exp(ls)
            dz = e * (1.0 - sig) - sig * p_excl
            if diag:
                dz = jnp.where(strict, dz, 0.0)
            dzb = (dz * scale).astype(BF16)
            dk_s[ks, :] += _mm_tn(dzb, qh)
            dv_s[ks, :] += _mm_tn(a.astype(BF16), doh)
            return (dq_acc + _mm(dzb, kk), pl_sum + jnp.sum(lr, axis=1, keepdims=True),
                    pe_sum + jnp.sum(e, axis=1, keepdims=True))

        def q_tile(i, _):
            qs = pl.ds(pl.multiple_of(i * bq, bq), bq)
            qv, dov, ltv = q_ref[qs, :], do_ref[qs, :], lt_ref[qs, :]
            zq, zd = jnp.zeros_like(qv), jnp.zeros_like(dov)
            q0, q1 = jnp.where(head0, qv, zq), jnp.where(head0, zq, qv)
            do0, do1 = jnp.where(head0, dov, zd), jnp.where(head0, zd, dov)
            lt0 = jnp.max(jnp.where(head0, ltv, -jnp.inf), axis=1, keepdims=True)
            lt1 = jnp.max(jnp.where(head0, -jnp.inf, ltv), axis=1, keepdims=True)
            zero = (jnp.zeros((bq, LANES), F32), jnp.zeros((bq, 1), F32), jnp.zeros((bq, 1), F32))

            def left(t, cr):
                ks = pl.ds(pl.multiple_of(t * bq, bq), bq)
                return block(q0, do0, lt0, ks, cr[0], False), block(q1, do1, lt1, ks, cr[1], False)

            c0, c1 = lax.fori_loop(0, i, left, (zero, zero))
            c0 = block(q0, do0, lt0, qs, c0, True)
            c1 = block(q1, do1, lt1, qs, c1, True)
            dq_ref[qs, :] = jnp.where(head0, c0[0], c1[0]).astype(BF16)
            return 0

        lax.fori_loop(0, nq, q_tile, 0)
        dk_ref[...] = dk_s[...].astype(BF16)
        dv_ref[...] = dv_s[...].astype(BF16)

    def col(off):
        return pl.BlockSpec((None, S, LANES), lambda b, p: (b, 0, off + p))

    shp = jax.ShapeDtypeStruct((B, S, W), BF16)
    return pl.pallas_call(
        body, name="sb_bwd", grid=(B, n_pair),
        in_specs=[col(0), col(n_pair), col(2 * n_pair), col(0), col(0)],
        out_specs=(col(0), col(0), col(0)),
        out_shape=(shp, shp, shp),
        scratch_shapes=[pltpu.VMEM((S, LANES), F32), pltpu.VMEM((S, LANES), F32)],
        compiler_params=_params("arbitrary", "arbitrary"),
    )(qkv, qkv, qkv, do, ltot)


def _pool_counts(S):
    t = lax.broadcasted_iota(jnp.int32, (S, 1), 0)
    return t, [jnp.minimum(t + 1, w).astype(F32) for w in POOL_WINDOWS]


def _pool_fwd(u, B, S):
    W = u.shape[2]

    def body(u_ref, out_ref):
        t, counts = _pool_counts(S)
        for gi, w in enumerate(POOL_WINDOWS):
            cols = slice(gi * POOL_GROUP_DIM, (gi + 1) * POOL_GROUP_DIM)
            ug = u_ref[:, cols]
            s, k = ug, 1
            while k < w:
                s = s + jnp.where(t >= k, pltpu.roll(s, k, axis=0), 0.0)
                k *= 2
            out_ref[:, cols] = (s / counts[gi] - ug).astype(BF16)

    spec = pl.BlockSpec((None, S, W), lambda b: (b, 0, 0))
    return pl.pallas_call(
        body, name="pool_fwd", grid=(B,), in_specs=[spec], out_specs=spec,
        out_shape=jax.ShapeDtypeStruct((B, S, W), BF16), compiler_params=_params("arbitrary"),
    )(u)


def _pool_bwd(dpooled, B, S):
    W = dpooled.shape[2]

    def body(d_ref, out_ref):
        t, counts = _pool_counts(S)
        for gi, w in enumerate(POOL_WINDOWS):
            cols = slice(gi * POOL_GROUP_DIM, (gi + 1) * POOL_GROUP_DIM)
            d = d_ref[:, cols]
            s, k = d / counts[gi], 1
            while k < w:
                s = s + jnp.where(t < S - k, pltpu.roll(s, S - k, axis=0), 0.0)
                k *= 2
            out_ref[:, cols] = (s - d).astype(BF16)

    spec = pl.BlockSpec((None, S, W), lambda b: (b, 0, 0))
    return pl.pallas_call(
        body, name="pool_bwd", grid=(B,), in_specs=[spec], out_specs=spec,
        out_shape=jax.ShapeDtypeStruct((B, S, W), BF16), compiler_params=_params("arbitrary"),
    )(dpooled)


def _mix_out_fwd(x, o_sb, pooled, w_pool, pool_scale, w_out):
    T, D = x.shape
    W = o_sb.shape[1]
    G = w_pool.shape[0]
    gd = POOL_GROUP_DIM
    tm = _tile(T, 512)

    def body(x_ref, osb_ref, pooled_ref, wp_ref, ps_ref, wo_ref, out_ref, mixed_ref):
        mixed_ref[:, :W] = osb_ref[...]
        for gi in range(G):
            cols = slice(gi * gd, (gi + 1) * gd)
            pw = _mm(pooled_ref[:, cols], wp_ref[gi])
            mixed_ref[:, W + gi * gd:W + (gi + 1) * gd] = (pw * ps_ref[:, cols]).astype(BF16)
        out_ref[...] = x_ref[...] + _mm(mixed_ref[...], wo_ref[...].reshape(D, D))

    row = lambda i: (i, 0)
    return pl.pallas_call(
        body, name="mix_out_fwd", grid=(T // tm,),
        in_specs=[pl.BlockSpec((tm, D), row), pl.BlockSpec((tm, W), row), pl.BlockSpec((tm, W), row),
                  pl.BlockSpec((G, gd, gd), lambda i: (0, 0, 0)), pl.BlockSpec((1, W), lambda i: (0, 0)),
                  pl.BlockSpec(w_out.shape, lambda i: (0, 0, 0))],
        out_specs=(pl.BlockSpec((tm, D), row), pl.BlockSpec((tm, D), row)),
        out_shape=(jax.ShapeDtypeStruct((T, D), F32), jax.ShapeDtypeStruct((T, D), BF16)),
        compiler_params=_params("arbitrary"),
    )(x, o_sb, pooled, w_pool, pool_scale, w_out)


def _mix_out_bwd(dx, pooled, w_pool, pool_scale, w_out):
    T, D = dx.shape
    W = pooled.shape[1]
    G = w_pool.shape[0]
    gd = POOL_GROUP_DIM
    tm = _tile(T, 512)

    def body(dx_ref, pooled_ref, wp_ref, ps_ref, wo_ref, dxb_ref, dosb_ref, dpooled_ref, dwp_ref, dps_ref):
        i = pl.program_id(0)

        @pl.when(i == 0)
        def _():
            dwp_ref[...] = jnp.zeros_like(dwp_ref)
            dps_ref[...] = jnp.zeros_like(dps_ref)

        dxb = dx_ref[...].astype(BF16)
        dxb_ref[...] = dxb
        dmixed = _mm_nt(dxb, wo_ref[...].reshape(D, D))
        dosb_ref[...] = dmixed[:, :W].astype(BF16)
        for gi in range(G):
            cols = slice(gi * gd, (gi + 1) * gd)
            pg = pooled_ref[:, cols]
            dop = dmixed[:, W + gi * gd:W + (gi + 1) * gd]
            pw = _mm(pg, wp_ref[gi])
            dps_ref[:, cols] += jnp.sum(dop * pw, axis=0, keepdims=True)
            dpw = (dop * ps_ref[:, cols]).astype(BF16)
            dwp_ref[gi] += _mm_tn(pg, dpw)
            dpooled_ref[:, cols] = _mm_nt(dpw, wp_ref[gi])

    row = lambda i: (i, 0)
    return pl.pallas_call(
        body, name="mix_out_bwd", grid=(T // tm,),
        in_specs=[pl.BlockSpec((tm, D), row), pl.BlockSpec((tm, W), row),
                  pl.BlockSpec((G, gd, gd), lambda i: (0, 0, 0)), pl.BlockSpec((1, W), lambda i: (0, 0)),
                  pl.BlockSpec(w_out.shape, lambda i: (0, 0, 0))],
        out_specs=(pl.BlockSpec((tm, D), row), pl.BlockSpec((tm, W), row), pl.BlockSpec((tm, W), row),
                   pl.BlockSpec((G, gd, gd), lambda i: (0, 0, 0)), pl.BlockSpec((1, W), lambda i: (0, 0))),
        out_shape=(jax.ShapeDtypeStruct((T, D), BF16), jax.ShapeDtypeStruct((T, W), BF16),
                   jax.ShapeDtypeStruct((T, W), F32), jax.ShapeDtypeStruct((G, gd, gd), F32),
                   jax.ShapeDtypeStruct((1, W), F32)),
        compiler_params=_params("arbitrary"),
    )(dx, pooled, w_pool, pool_scale, w_out)


def _mem_kv_fwd(mem, gain, w_kv):
    B, M, D = mem.shape
    cs = w_kv.shape[2]

    def body(mem_ref, gain_ref, w_ref, memn_ref, kv_ref):
        _, xhat = _rms(mem_ref[...])
        mn = (xhat * gain_ref[...]).astype(BF16)
        memn_ref[...] = mn
        for b in range(N_DEV):
            kv_ref[:, b * cs:(b + 1) * cs] = _mm(mn, w_ref[b]).astype(BF16)

    return pl.pallas_call(
        body, name="mem_kv_fwd", grid=(B,),
        in_specs=[pl.BlockSpec((None, M, D), lambda b: (b, 0, 0)), pl.BlockSpec((1, D), lambda b: (0, 0)),
                  pl.BlockSpec((N_DEV, D, cs), lambda b: (0, 0, 0))],
        out_specs=(pl.BlockSpec((M, D), lambda b: (b, 0)), pl.BlockSpec((None, M, N_DEV * cs), lambda b: (b, 0, 0))),
        out_shape=(jax.ShapeDtypeStruct((B * M, D), BF16), jax.ShapeDtypeStruct((B, M, N_DEV * cs), BF16)),
        compiler_params=_params("arbitrary"),
    )(mem, gain, w_kv)


def _mem_kv_bwd(dkv, mem, w_kv):
    B, M, D = mem.shape
    cs = w_kv.shape[2]

    def body(dkv_ref, mem_ref, w_ref, dkvb_ref, dgain_ref):
        b_id = pl.program_id(0)

        @pl.when(b_id == 0)
        def _():
            dgain_ref[...] = jnp.zeros_like(dgain_ref)

        dkvb = dkv_ref[...].astype(BF16)
        dkvb_ref[...] = dkvb
        dmn = jnp.zeros((M, D), F32)
        for b in range(N_DEV):
            dmn = dmn + _mm_nt(dkvb[:, b * cs:(b + 1) * cs], w_ref[b])
        _, xhat = _rms(mem_ref[...])
        dgain_ref[...] += jnp.sum(dmn * xhat, axis=0, keepdims=True)

    return pl.pallas_call(
        body, name="mem_kv_bwd", grid=(B,),
        in_specs=[pl.BlockSpec((None, M, N_DEV * cs), lambda b: (b, 0, 0)),
                  pl.BlockSpec((None, M, D), lambda b: (b, 0, 0)),
                  pl.BlockSpec((N_DEV, D, cs), lambda b: (0, 0, 0))],
        out_specs=(pl.BlockSpec((M, N_DEV * cs), lambda b: (b, 0)), pl.BlockSpec((1, D), lambda b: (0, 0))),
        out_shape=(jax.ShapeDtypeStruct((B * M, N_DEV * cs), BF16), jax.ShapeDtypeStruct((1, D), F32)),
        compiler_params=_params("arbitrary"),
    )(dkv, mem, w_kv)


def _softmax_rows(s):
    p = jnp.exp(s - jnp.max(s, axis=1, keepdims=True))
    return p / jnp.sum(p, axis=1, keepdims=True)


def _cross_fwd(x, gain, kv, w_q, w_o, B, S):
    T, D = x.shape
    M = kv.shape[1]
    hd = D // MEM_HEADS
    tm = _tile(S, 512)
    per = S // tm
    scale = hd ** -0.5

    def body(x_ref, gain_ref, kv_ref, wq_ref, wo_ref, out_ref, hq_ref, q_ref, ocat_ref):
        _, xhat = _rms(x_ref[...])
        hq = (xhat * gain_ref[...]).astype(BF16)
        hq_ref[...] = hq
        q = _mm(hq, wq_ref[...].reshape(D, D)).astype(BF16)
        q_ref[...] = q
        for h in range(MEM_HEADS):
            cols = slice(h * hd, (h + 1) * hd)
            s = _mm_nt(q[:, cols], kv_ref[:, cols]) * scale
            p = _softmax_rows(s).astype(BF16)
            ocat_ref[:, cols] = _mm(p, kv_ref[:, D + h * hd:D + (h + 1) * hd]).astype(BF16)
        out_ref[...] = x_ref[...] + _mm(ocat_ref[...], wo_ref[...].reshape(D, D))

    row = lambda b, t: (b * per + t, 0)
    wspec = pl.BlockSpec(w_q.shape, lambda b, t: (0, 0, 0))
    return pl.pallas_call(
        body, name="cross_fwd", grid=(B, per),
        in_specs=[pl.BlockSpec((tm, D), row), pl.BlockSpec((1, D), lambda b, t: (0, 0)),
                  pl.BlockSpec((None, M, 2 * D), lambda b, t: (b, 0, 0)), wspec, wspec],
        out_specs=tuple(pl.BlockSpec((tm, D), row) for _ in range(4)),
        out_shape=(jax.ShapeDtypeStruct((T, D), F32),) + tuple(jax.ShapeDtypeStruct((T, D), BF16) for _ in range(3)),
        compiler_params=_params("arbitrary", "arbitrary"),
    )(x, gain, kv, w_q, w_o)


def _cross_bwd(dy, x, gain, q, kv, w_q, w_o, B, S):
    T, D = x.shape
    M = kv.shape[1]
    hd = D // MEM_HEADS
    tm = _tile(S, 512)
    per = S // tm
    scale = hd ** -0.5

    def body(dy_ref, x_ref, gain_ref, q_ref, kv_ref, wq_ref, wo_ref,
             dx_ref, dyb_ref, dqb_ref, dkv_ref, dgain_ref):
        b_id, t_id = pl.program_id(0), pl.program_id(1)

        @pl.when((b_id == 0) & (t_id == 0))
        def _():
            dgain_ref[...] = jnp.zeros_like(dgain_ref)

        @pl.when(t_id == 0)
        def _():
            dkv_ref[...] = jnp.zeros_like(dkv_ref)

        dyb = dy_ref[...].astype(BF16)
        dyb_ref[...] = dyb
        docat = _mm_nt(dyb, wo_ref[...].reshape(D, D)).astype(BF16)
        for h in range(MEM_HEADS):
            cols = slice(h * hd, (h + 1) * hd)
            vcols = slice(D + h * hd, D + (h + 1) * hd)
            qh, kh, vh, doh = q_ref[:, cols], kv_ref[:, cols], kv_ref[:, vcols], docat[:, cols]
            p = _softmax_rows(_mm_nt(qh, kh) * scale)
            dp = _mm_nt(doh, vh)
            ds = (p * (dp - jnp.sum(dp * p, axis=1, keepdims=True)) * scale).astype(BF16)
            dqb_ref[:, cols] = _mm(ds, kh).astype(BF16)
            dkv_ref[:, cols] += _mm_tn(ds, qh)
            dkv_ref[:, vcols] += _mm_tn(p.astype(BF16), doh)
        dhq = _mm_nt(dqb_ref[...], wq_ref[...].reshape(D, D))
        r, xhat = _rms(x_ref[...])
        dgain_ref[...] += jnp.sum(dhq * xhat, axis=0, keepdims=True)
        dx_ref[...] = dy_ref[...] + _rms_bwd(dhq, gain_ref[...], r, xhat)

    row = lambda b, t: (b * per + t, 0)
    wspec = pl.BlockSpec(w_q.shape, lambda b, t: (0, 0, 0))
    one = pl.BlockSpec((1, D), lambda b, t: (0, 0))
    kvspec = pl.BlockSpec((None, M, 2 * D), lambda b, t: (b, 0, 0))
    return pl.pallas_call(
        body, name="cross_bwd", grid=(B, per),
        in_specs=[pl.BlockSpec((tm, D), row), pl.BlockSpec((tm, D), row), one, pl.BlockSpec((tm, D), row),
                  kvspec, wspec, wspec],
        out_specs=(pl.BlockSpec((tm, D), row), pl.BlockSpec((tm, D), row), pl.BlockSpec((tm, D), row), kvspec, one),
        out_shape=(jax.ShapeDtypeStruct((T, D), F32), jax.ShapeDtypeStruct((T, D), BF16),
                   jax.ShapeDtypeStruct((T, D), BF16), jax.ShapeDtypeStruct((B, M, 2 * D), F32),
                   jax.ShapeDtypeStruct((1, D), F32)),
        compiler_params=_params("arbitrary", "arbitrary"),
    )(dy, x, gain, q, kv, w_q, w_o)


def _final(x, gain, target):
    T, D = x.shape
    tm = _tile(T, 512)

    def body(x_ref, gain_ref, tgt_ref, dx_ref, dgain_ref, loss_ref):
        i = pl.program_id(0)

        @pl.when(i == 0)
        def _():
            dgain_ref[...] = jnp.zeros_like(dgain_ref)
            loss_ref[...] = jnp.zeros_like(loss_ref)

        r, xhat = _rms(x_ref[...])
        err = xhat * gain_ref[...] - tgt_ref[...]
        loss_ref[...] += 0.5 * jnp.sum(jnp.mean(err * err, axis=-1, keepdims=True), axis=0, keepdims=True)
        dy = err * (1.0 / D)
        dgain_ref[...] += jnp.sum(dy * xhat, axis=0, keepdims=True)
        dx_ref[...] = _rms_bwd(dy, gain_ref[...], r, xhat)

    row = lambda i: (i, 0)
    one = pl.BlockSpec((1, D), lambda i: (0, 0))
    return pl.pallas_call(
        body, name="final_loss", grid=(T // tm,),
        in_specs=[pl.BlockSpec((tm, D), row), one, pl.BlockSpec((tm, D), row)],
        out_specs=(pl.BlockSpec((tm, D), row), one, pl.BlockSpec((8, LANES), lambda i: (0, 0))),
        out_shape=(jax.ShapeDtypeStruct((T, D), F32), jax.ShapeDtypeStruct((1, D), F32),
                   jax.ShapeDtypeStruct((8, LANES), F32)),
        compiler_params=_params("arbitrary"),
    )(x, gain, target)


def _adamw(gparts, w, m, v, name):
    R, C = w.shape
    tr = _tile(R, 256)

    def body(gp_ref, w_ref, m_ref, v_ref, g_ref, d_ref, nm_ref, nv_ref):
        g = gp_ref[0].astype(F32)
        for s in range(1, N_DEV):
            g = g + gp_ref[s].astype(F32)
        nm = ADAM_B1 * m_ref[...] + (1.0 - ADAM_B1) * g
        nv = ADAM_B2 * v_ref[...] + (1.0 - ADAM_B2) * (g * g)
        m_hat = nm / (1.0 - ADAM_B1 ** ADAM_STEP)
        v_hat = nv / (1.0 - ADAM_B2 ** ADAM_STEP)
        g_ref[...] = g
        nm_ref[...] = nm
        nv_ref[...] = nv
        d_ref[...] = -ADAM_LR * (m_hat / (jnp.sqrt(v_hat) + ADAM_EPS) + ADAM_WD * w_ref[...])

    spec = pl.BlockSpec((tr, C), lambda i: (i, 0))
    shp = jax.ShapeDtypeStruct((R, C), F32)
    return pl.pallas_call(
        body, name=name, grid=(R // tr,),
        in_specs=[pl.BlockSpec((N_DEV, tr, C), lambda i: (0, i, 0)), spec, spec, spec],
        out_specs=(spec, spec, spec, spec), out_shape=(shp, shp, shp, shp),
        compiler_params=_params("arbitrary"),
    )(gparts, w, m, v)


def _rows128(a, rows):
    a = a.reshape(-1, LANES)
    return jnp.pad(a, ((0, rows - a.shape[0]), (0, 0)))


def kernel(x, mem, ffn1_norm, ffn1_w_gate, ffn1_w_up, ffn1_w_down, mix_norm, w_in, w_pool, pool_scale, w_out, mem_q_norm, mem_kv_norm, mem_w_q, mem_w_kv, mem_w_o, ffn2_norm, ffn2_w_gate, ffn2_w_up, ffn2_w_down, final_norm, loss_target, m_ffn1_norm, m_ffn1_w_gate, m_ffn1_w_up, m_ffn1_w_down, m_mix_norm, m_w_in, m_w_pool, m_pool_scale, m_w_out, m_mem_q_norm, m_mem_kv_norm, m_mem_w_q, m_mem_w_kv, m_mem_w_o, m_ffn2_norm, m_ffn2_w_gate, m_ffn2_w_up, m_ffn2_w_down, m_final_norm, v_ffn1_norm, v_ffn1_w_gate, v_ffn1_w_up, v_ffn1_w_down, v_mix_norm, v_w_in, v_w_pool, v_pool_scale, v_w_out, v_mem_q_norm, v_mem_kv_norm, v_mem_w_q, v_mem_w_kv, v_mem_w_o, v_ffn2_norm, v_ffn2_w_gate, v_ffn2_w_up, v_ffn2_w_down, v_final_norm):
    B, S, D = x.shape
    T = B * S
    x0 = x.reshape(T, D)
    target = loss_target.reshape(T, D)
    final_gain = final_norm.reshape(1, D)

    big = dict(
        g1=ffn1_w_gate[0].T, u1=ffn1_w_up[0].T, d1=ffn1_w_down[0],
        g2=ffn2_w_gate[0].T, u2=ffn2_w_up[0].T, d2=ffn2_w_down[0],
        w_in=w_in[0], w_out=w_out[0], w_q=mem_w_q[0], w_kv=mem_w_kv[0], w_o=mem_w_o[0])
    names = list(big)
    full = dict(zip(names, _exchange([big[k].astype(BF16) for k in names], True, "gather_weights")))
    ffn_w = {k: full[k].reshape(-1, D) for k in ("g1", "u1", "d1", "g2", "u2", "d2")}
    wp = w_pool[0].astype(BF16)

    x1, hn1, gg1, uu1 = _ffn_fwd(x0, ffn1_norm, ffn_w["g1"], ffn_w["u1"], ffn_w["d1"], "ffn1_fwd")
    hn2, qkv, u = _mix_in_fwd(x1, mix_norm, full["w_in"])
    qkv3 = qkv.reshape(B, S, -1)
    o_sb, ltot = _sb_fwd(qkv3, B, S)
    pooled = _pool_fwd(u.reshape(B, S, -1), B, S).reshape(T, -1)
    x2, mixed = _mix_out_fwd(x1, o_sb.reshape(T, -1), pooled, wp, pool_scale, full["w_out"])
    memn, kv = _mem_kv_fwd(mem, mem_kv_norm, full["w_kv"])
    x3, hq, q, ocat = _cross_fwd(x2, mem_q_norm, kv, full["w_q"], full["w_o"], B, S)
    x4, hn4, gg2, uu2 = _ffn_fwd(x3, ffn2_norm, ffn_w["g2"], ffn_w["u2"], ffn_w["d2"], "ffn2_fwd")
    dx4, d_final, loss_part = _final(x4, final_gain, target)

    dx3, a2, dg2, du2, dyh2, d_ffn2 = _ffn_bwd(dx4, x3, ffn2_norm, gg2, uu2, ffn_w["g2"], ffn_w["u2"], ffn_w["d2"], "ffn2_bwd")
    grads = dict(g2=_wgrad(dg2, hn4, "dw_gate2"), u2=_wgrad(du2, hn4, "dw_up2"), d2=_wgrad(a2, dyh2, "dw_down2"))
    dx2, dx3b, dqb, dkv, d_q = _cross_bwd(dx3, x2, mem_q_norm, q, kv, full["w_q"], full["w_o"], B, S)
    grads["w_o"] = _wgrad(ocat, dx3b, "dw_o")
    grads["w_q"] = _wgrad(hq, dqb, "dw_q")
    dkvb, d_kv = _mem_kv_bwd(dkv, mem, full["w_kv"])
    grads["w_kv"] = _wgrad(memn, dkvb, "dw_kv", col_slab=full["w_kv"].shape[2])
    dx2b, do_sb, dpooled, d_wpool, d_ps = _mix_out_bwd(dx2, pooled, wp, pool_scale, full["w_out"])
    grads["w_out"] = _wgrad(mixed, dx2b, "dw_out")
    du = _pool_bwd(dpooled.reshape(B, S, -1), B, S).reshape(T, -1)
    dq, dk, dv = _sb_bwd(qkv3, do_sb.reshape(B, S, -1), ltot, B, S)
    dx1, dproj, d_mix = _mix_in_bwd(dx2, dq.reshape(T, -1), dk.reshape(T, -1), dv.reshape(T, -1), du,
                                    x1, mix_norm, full["w_in"])
    grads["w_in"] = _wgrad(hn2, dproj, "dw_in", col_slab=full["w_in"].shape[2])
    dx0, a1, dg1, du1, dyh1, d_ffn1 = _ffn_bwd(dx1, x0, ffn1_norm, gg1, uu1, ffn_w["g1"], ffn_w["u1"], ffn_w["d1"], "ffn1_bwd")
    grads.update(g1=_wgrad(dg1, hn1, "dw_gate1"), u1=_wgrad(du1, hn1, "dw_up1"), d1=_wgrad(a1, dyh1, "dw_down1"))

    slabs = [grads[k].reshape((N_DEV, -1) + grads[k].shape[-1:]) for k in names]
    got = dict(zip(names, _exchange(slabs, False, "scatter_grads")))
    state = dict(
        g1=(ffn1_w_gate, m_ffn1_w_gate, v_ffn1_w_gate), u1=(ffn1_w_up, m_ffn1_w_up, v_ffn1_w_up),
        d1=(ffn1_w_down, m_ffn1_w_down, v_ffn1_w_down), g2=(ffn2_w_gate, m_ffn2_w_gate, v_ffn2_w_gate),
        u2=(ffn2_w_up, m_ffn2_w_up, v_ffn2_w_up), d2=(ffn2_w_down, m_ffn2_w_down, v_ffn2_w_down),
        w_in=(w_in, m_w_in, v_w_in), w_out=(w_out, m_w_out, v_w_out), w_q=(mem_w_q, m_mem_w_q, v_mem_w_q),
        w_kv=(mem_w_kv, m_mem_w_kv, v_mem_w_kv), w_o=(mem_w_o, m_mem_w_o, v_mem_w_o))
    big_out = {}
    for k in names:
        parts = got[k]
        if k in ("g1", "u1", "g2", "u2"):
            parts = jnp.swapaxes(parts, 1, 2)
        w_, m_, v_ = (t[0] for t in state[k])
        big_out[k] = [t[None] for t in _adamw(parts, w_, m_, v_, "adamw_" + k)]

    small = [("ffn1_norm", d_ffn1, ffn1_norm, m_ffn1_norm, v_ffn1_norm),
             ("mix_norm", d_mix, mix_norm, m_mix_norm, v_mix_norm),
             ("w_pool", d_wpool, w_pool, m_w_pool, v_w_pool),
             ("pool_scale", d_ps, pool_scale, m_pool_scale, v_pool_scale),
             ("mem_q_norm", d_q, mem_q_norm, m_mem_q_norm, v_mem_q_norm),
             ("mem_kv_norm", d_kv, mem_kv_norm, m_mem_kv_norm, v_mem_kv_norm),
             ("ffn2_norm", d_ffn2, ffn2_norm, m_ffn2_norm, v_ffn2_norm),
             ("final_norm", d_final, final_norm, m_final_norm, v_final_norm)]
    rows = [max(8, t[2].size // LANES) for t in small]
    pack = lambda idx: jnp.concatenate([_rows128(t[idx], r) for t, r in zip(small, rows)]
                                       + ([loss_part] if idx == 1 else [jnp.zeros((8, LANES), F32)]))
    gathered = _exchange([pack(1)], True, "gather_small")[0]
    small_res = _adamw(gathered, pack(2), pack(3), pack(4), "adamw_small")
    small_out, off = {}, 0
    for t, r in zip(small, rows):
        n = t[2].size // LANES
        small_out[t[0]] = [res[off:off + n].reshape(t[2].shape) for res in small_res]
        off += r
    loss = small_res[0][off, 0]

    order = [("ffn1_norm", None), ("ffn1_w_gate", "g1"), ("ffn1_w_up", "u1"), ("ffn1_w_down", "d1"),
             ("mix_norm", None), ("w_in", "w_in"), ("w_pool", None), ("pool_scale", None), ("w_out", "w_out"),
             ("mem_q_norm", None), ("mem_kv_norm", None), ("mem_w_q", "w_q"), ("mem_w_kv", "w_kv"),
             ("mem_w_o", "w_o"), ("ffn2_norm", None), ("ffn2_w_gate", "g2"), ("ffn2_w_up", "u2"),
             ("ffn2_w_down", "d2"), ("final_norm", None)]
    res = [loss, dx0.reshape(B, S, D)]
    for which in range(4):
        for name, key in order:
            res.append(big_out[key][which] if key else small_out[name][which])
    return tuple(res)
```

```python
import functools

import jax
import jax.numpy as jnp
from jax import lax
from jax.experimental import pallas as pl
from jax.experimental.pallas import tpu as pltpu

F32 = jnp.float32
BF16 = jnp.bfloat16

N_DEV = 8
EPS = 1e-6
SB_HEAD_DIM = 64
LANES = 128
POOL_WINDOWS = (2, 4, 8, 16)
POOL_GROUP_DIM = 128
MEM_HEADS = 4
FFN_RESIDUAL_WEIGHT = 0.5
ADAM_LR = 0.001
ADAM_B1 = 0.9
ADAM_B2 = 0.999
ADAM_EPS = 1e-08
ADAM_WD = 0.01
ADAM_STEP = 10
VMEM_LIMIT = 56 * 1024 * 1024

MESH_ID = pl.DeviceIdType.MESH


def _params(*sem):
    return pltpu.CompilerParams(dimension_semantics=sem, vmem_limit_bytes=VMEM_LIMIT)


def _tile(n, pref):
    if n <= pref:
        return n
    t = pref - pref % 8
    while n % t:
        t -= 8
    return t


def _mm(a, b):
    return jnp.dot(a, b, preferred_element_type=F32)


def _mm_nt(a, b):
    return lax.dot_general(a, b, (((1,), (1,)), ((), ())), preferred_element_type=F32)


def _mm_tn(a, b):
    return lax.dot_general(a, b, (((0,), (0,)), ((), ())), preferred_element_type=F32)


def _rms(xv):
    r = lax.rsqrt(jnp.mean(xv * xv, axis=-1, keepdims=True) + EPS)
    return r, xv * r


def _rms_bwd(dhn, gain, r, xhat):
    dxh = dhn * gain
    return r * (dxh - xhat * jnp.mean(dxh * xhat, axis=-1, keepdims=True))


def _sigmoid(z):
    return 1.0 / (1.0 + jnp.exp(-z))


def _comm_shapes(arrs, gather):
    return tuple(jax.ShapeDtypeStruct(((N_DEV,) + tuple(a.shape)) if gather else tuple(a.shape), a.dtype)
                 for a in arrs)


def _comm_start(ins, outs, sems, gather):
    send_sems, recv_sems, local_sems = sems
    x, y, c = lax.axis_index("x"), lax.axis_index("y"), lax.axis_index("c")
    me = 4 * x + 2 * y + c
    for i in range(len(ins)):
        src = ins[i] if gather else ins[i].at[me]
        pltpu.make_async_copy(src, outs[i].at[me], local_sems.at[i]).start()
    for k in range(1, N_DEV):
        px = 1 - x if k & 4 else x
        py = 1 - y if k & 2 else y
        pc = 1 - c if k & 1 else c
        peer = 4 * px + 2 * py + pc
        for i in range(len(ins)):
            src = ins[i] if gather else ins[i].at[peer]
            pltpu.make_async_remote_copy(
                src_ref=src, dst_ref=outs[i].at[me],
                send_sem=send_sems.at[i], recv_sem=recv_sems.at[i],
                device_id=(px, py, pc), device_id_type=MESH_ID).start()


def _comm_wait(ins, outs, sems, gather):
    send_sems, recv_sems, local_sems = sems
    x, y, c = lax.axis_index("x"), lax.axis_index("y"), lax.axis_index("c")
    me = 4 * x + 2 * y + c
    for i in range(len(ins)):
        seven = outs[i].at[pl.ds(0, N_DEV - 1)]
        done = pltpu.make_async_remote_copy(
            src_ref=seven, dst_ref=seven,
            send_sem=send_sems.at[i], recv_sem=recv_sems.at[i],
            device_id=(x, y, c), device_id_type=MESH_ID)
        done.wait_send()
        done.wait_recv()
        src = ins[i] if gather else ins[i].at[me]
        pltpu.make_async_copy(src, outs[i].at[me], local_sems.at[i]).wait()


def _comm_sems(n):
    return [pltpu.SemaphoreType.DMA((n,)) for _ in range(3)]


def _exchange(arrs, gather, name):
    n = len(arrs)

    def body(*refs):
        ins, outs, sems = refs[:n], refs[n:2 * n], refs[2 * n:]
        _comm_start(ins, outs, sems, gather)
        _comm_wait(ins, outs, sems, gather)

    any_spec = pl.BlockSpec(memory_space=pl.ANY)
    outs = pl.pallas_call(
        body, name=name, out_shape=_comm_shapes(arrs, gather),
        in_specs=[any_spec] * n, out_specs=tuple([any_spec] * n), scratch_shapes=_comm_sems(n),
    )(*arrs)
    return list(outs)


def _call(body, *, name, grid, in_specs, out_specs, out_shape, args, scratch=(), comm=None):
    sem = ("arbitrary",) * len(grid)
    if comm is None:
        res = pl.pallas_call(body, name=name, grid=grid, in_specs=list(in_specs), out_specs=tuple(out_specs),
                             out_shape=tuple(out_shape), scratch_shapes=list(scratch),
                             compiler_params=_params(*sem))(*args)
        return tuple(res), []
    arrs, gather = comm
    n, n_in, n_out, n_sc = len(arrs), len(args), len(out_shape), len(scratch)

    def wrapped(*refs):
        ins, cin = refs[:n_in], refs[n_in:n_in + n]
        outs, cout = refs[n_in + n:n_in + n + n_out], refs[n_in + n + n_out:n_in + 2 * n + n_out]
        sc, sems = refs[n_in + 2 * n + n_out:n_in + 2 * n + n_out + n_sc], refs[n_in + 2 * n + n_out + n_sc:]
        ids = [pl.program_id(a) for a in range(len(grid))]
        first = functools.reduce(jnp.logical_and, [i == 0 for i in ids])
        last = functools.reduce(jnp.logical_and, [i == g - 1 for i, g in zip(ids, grid)])

        @pl.when(first)
        def _():
            _comm_start(cin, cout, sems, gather)

        body(*ins, *outs, *sc)

        @pl.when(last)
        def _():
            _comm_wait(cin, cout, sems, gather)

    any_spec = pl.BlockSpec(memory_space=pl.ANY)
    res = pl.pallas_call(
        wrapped, name=name, grid=grid, in_specs=list(in_specs) + [any_spec] * n,
        out_specs=tuple(out_specs) + (any_spec,) * n, out_shape=tuple(out_shape) + _comm_shapes(arrs, gather),
        scratch_shapes=list(scratch) + _comm_sems(n), compiler_params=_params(*sem))(*args, *arrs)
    return tuple(res[:n_out]), list(res[n_out:])


def _ffn_fwd(x, gain, wgt, wut, wd, name, comm=None):
    T, D = x.shape
    F = wd.shape[0]
    tm, tf = _tile(T, 1024), _tile(F, 256)
    nj = F // tf

    def body(x_ref, gain_ref, wg_ref, wu_ref, wd_ref, out_ref, hn_ref, g_ref, u_ref, acc, hn_s):
        j = pl.program_id(1)

        @pl.when(j == 0)
        def _():
            _, xhat = _rms(x_ref[...])
            hn = (xhat * gain_ref[...]).astype(BF16)
            hn_s[...] = hn
            hn_ref[...] = hn
            acc[...] = jnp.zeros_like(acc)

        hn = hn_s[...]
        g = _mm_nt(hn, wg_ref[...])
        u = _mm_nt(hn, wu_ref[...])
        g_ref[...] = g.astype(BF16)
        u_ref[...] = u.astype(BF16)
        a = (g * _sigmoid(g) * u).astype(BF16)
        acc[...] += _mm(a, wd_ref[...])

        @pl.when(j == nj - 1)
        def _():
            out_ref[...] = x_ref[...] + FFN_RESIDUAL_WEIGHT * acc[...]

    row = lambda i, j: (i, 0)
    wspec = pl.BlockSpec((tf, D), lambda i, j: (j, 0))
    return _call(
        body, name=name, grid=(T // tm, nj), comm=comm,
        in_specs=[pl.BlockSpec((tm, D), row), pl.BlockSpec((1, D), lambda i, j: (0, 0)), wspec, wspec, wspec],
        out_specs=(pl.BlockSpec((tm, D), row), pl.BlockSpec((tm, D), row),
                   pl.BlockSpec((tm, tf), lambda i, j: (i, j)), pl.BlockSpec((tm, tf), lambda i, j: (i, j))),
        out_shape=(jax.ShapeDtypeStruct((T, D), F32), jax.ShapeDtypeStruct((T, D), BF16),
                   jax.ShapeDtypeStruct((T, F), BF16), jax.ShapeDtypeStruct((T, F), BF16)),
        scratch=[pltpu.VMEM((tm, D), F32), pltpu.VMEM((tm, D), BF16)],
        args=(x, gain, wgt, wut, wd))


def _ffn_bwd(dy, x, gain, g, u, wgt, wut, wd, name, comm=None):
    T, D = x.shape
    F = wd.shape[0]
    tm, tf = _tile(T, 1024), _tile(F, 256)
    nj = F // tf

    def body(dy_ref, x_ref, gain_ref, g_ref, u_ref, wg_ref, wu_ref, wd_ref,
             dx_ref, a_ref, dg_ref, du_ref, dyh_ref, dgain_ref, acc, dyh_s):
        i, j = pl.program_id(0), pl.program_id(1)

        @pl.when(j == 0)
        def _():
            dyh = (FFN_RESIDUAL_WEIGHT * dy_ref[...]).astype(BF16)
            dyh_s[...] = dyh
            dyh_ref[...] = dyh
            acc[...] = jnp.zeros_like(acc)

        @pl.when((i == 0) & (j == 0))
        def _():
            dgain_ref[...] = jnp.zeros_like(dgain_ref)

        gv = g_ref[...].astype(F32)
        uv = u_ref[...].astype(F32)
        da = _mm_nt(dyh_s[...], wd_ref[...])
        sig = _sigmoid(gv)
        s = gv * sig
        a_ref[...] = (s * uv).astype(BF16)
        dg = (da * uv * (sig * (1.0 + gv * (1.0 - sig)))).astype(BF16)
        du = (da * s).astype(BF16)
        dg_ref[...] = dg
        du_ref[...] = du
        acc[...] += _mm(dg, wg_ref[...]) + _mm(du, wu_ref[...])

        @pl.when(j == nj - 1)
        def _():
            r, xhat = _rms(x_ref[...])
            dhn = acc[...]
            dgain_ref[...] += jnp.sum(dhn * xhat, axis=0, keepdims=True)
            dx_ref[...] = dy_ref[...] + _rms_bwd(dhn, gain_ref[...], r, xhat)

    row = lambda i, j: (i, 0)
    tile = lambda i, j: (i, j)
    wspec = pl.BlockSpec((tf, D), lambda i, j: (j, 0))
    one = pl.BlockSpec((1, D), lambda i, j: (0, 0))
    return _call(
        body, name=name, grid=(T // tm, nj), comm=comm,
        in_specs=[pl.BlockSpec((tm, D), row), pl.BlockSpec((tm, D), row), one,
                  pl.BlockSpec((tm, tf), tile), pl.BlockSpec((tm, tf), tile), wspec, wspec, wspec],
        out_specs=(pl.BlockSpec((tm, D), row), pl.BlockSpec((tm, tf), tile), pl.BlockSpec((tm, tf), tile),
                   pl.BlockSpec((tm, tf), tile), pl.BlockSpec((tm, D), row), one),
        out_shape=(jax.ShapeDtypeStruct((T, D), F32), jax.ShapeDtypeStruct((T, F), BF16),
                   jax.ShapeDtypeStruct((T, F), BF16), jax.ShapeDtypeStruct((T, F), BF16),
                   jax.ShapeDtypeStruct((T, D), BF16), jax.ShapeDtypeStruct((1, D), F32)),
        scratch=[pltpu.VMEM((tm, D), F32), pltpu.VMEM((tm, D), BF16)],
        args=(dy, x, gain, g, u, wgt, wut, wd))


def _wgrad(a, b, name, col_slab=None, comm=None):
    T, M = a.shape
    N = b.shape[1]
    tmm = M if M <= 1024 else _tile(M, 1408)
    tn = col_slab if col_slab else _tile(N, 1024)
    tk = _tile(T, 512)
    nk = T // tk

    def body(a_ref, b_ref, out_ref, acc):
        k = pl.program_id(2)

        @pl.when(k == 0)
        def _():
            acc[...] = jnp.zeros_like(acc)

        acc[...] += _mm_tn(a_ref[...], b_ref[...])

        @pl.when(k == nk - 1)
        def _():
            out_ref[...] = acc[...].astype(BF16)

    if col_slab:
        out_spec = pl.BlockSpec((None, tmm, tn), lambda m, n, k: (n, m, 0))
        out_shape = jax.ShapeDtypeStruct((N // tn, M, tn), BF16)
    else:
        out_spec = pl.BlockSpec((tmm, tn), lambda m, n, k: (m, n))
        out_shape = jax.ShapeDtypeStruct((M, N), BF16)
    (out,), got = _call(
        body, name=name, grid=(M // tmm, N // tn, nk), comm=comm,
        in_specs=[pl.BlockSpec((tk, tmm), lambda m, n, k: (k, m)), pl.BlockSpec((tk, tn), lambda m, n, k: (k, n))],
        out_specs=(out_spec,), out_shape=(out_shape,),
        scratch=[pltpu.VMEM((tmm, tn), F32)], args=(a, b))
    return (out, got) if comm else out


def _mix_in_fwd(x, gain, w_in):
    T, D = x.shape
    cs = w_in.shape[2]
    n_qkv = 3 * (N_DEV // 4)
    tm = _tile(T, 512)

    def body(x_ref, gain_ref, w_ref, hn_ref, qkv_ref, u_ref):
        _, xhat = _rms(x_ref[...])
        hn = (xhat * gain_ref[...]).astype(BF16)
        hn_ref[...] = hn
        for b in range(N_DEV):
            p = _mm(hn, w_ref[b])
            if b < n_qkv:
                qkv_ref[:, b * cs:(b + 1) * cs] = p.astype(BF16)
            else:
                u_ref[:, (b - n_qkv) * cs:(b - n_qkv + 1) * cs] = p

    row = lambda i: (i, 0)
    return pl.pallas_call(
        body, name="mix_in_fwd", grid=(T // tm,),
        in_specs=[pl.BlockSpec((tm, D), row), pl.BlockSpec((1, D), lambda i: (0, 0)),
                  pl.BlockSpec((N_DEV, D, cs), lambda i: (0, 0, 0))],
        out_specs=(pl.BlockSpec((tm, D), row), pl.BlockSpec((tm, n_qkv * cs), row),
                   pl.BlockSpec((tm, (N_DEV - n_qkv) * cs), row)),
        out_shape=(jax.ShapeDtypeStruct((T, D), BF16), jax.ShapeDtypeStruct((T, n_qkv * cs), BF16),
                   jax.ShapeDtypeStruct((T, (N_DEV - n_qkv) * cs), F32)),
        compiler_params=_params("arbitrary"),
    )(x, gain, w_in)


def _mix_in_bwd(dres, dq, dk, dv, du, x, gain, w_in):
    T, D = x.shape
    cs = w_in.shape[2]
    W = dq.shape[1]
    per = W // cs
    tm = _tile(T, 512)

    def body(dres_ref, dq_ref, dk_ref, dv_ref, du_ref, x_ref, gain_ref, w_ref, dx_ref, dproj_ref, dgain_ref):
        i = pl.program_id(0)

        @pl.when(i == 0)
        def _():
            dgain_ref[...] = jnp.zeros_like(dgain_ref)

        dhn = jnp.zeros((tm, D), F32)
        for part, ref in enumerate((dq_ref, dk_ref, dv_ref, du_ref)):
            for h in range(per):
                b = part * per + h
                d = ref[:, h * cs:(h + 1) * cs]
                dproj_ref[:, b * cs:(b + 1) * cs] = d
                dhn = dhn + _mm_nt(d, w_ref[b])
        r, xhat = _rms(x_ref[...])
        dgain_ref[...] += jnp.sum(dhn * xhat, axis=0, keepdims=True)
        dx_ref[...] = dres_ref[...] + _rms_bwd(dhn, gain_ref[...], r, xhat)

    row = lambda i: (i, 0)
    one = pl.BlockSpec((1, D), lambda i: (0, 0))
    part = pl.BlockSpec((tm, W), row)
    return pl.pallas_call(
        body, name="mix_in_bwd", grid=(T // tm,),
        in_specs=[pl.BlockSpec((tm, D), row), part, part, part, part, pl.BlockSpec((tm, D), row), one,
                  pl.BlockSpec((N_DEV, D, cs), lambda i: (0, 0, 0))],
        out_specs=(pl.BlockSpec((tm, D), row), pl.BlockSpec((tm, 4 * W), row), one),
        out_shape=(jax.ShapeDtypeStruct((T, D), F32), jax.ShapeDtypeStruct((T, 4 * W), BF16),
                   jax.ShapeDtypeStruct((1, D), F32)),
        compiler_params=_params("arbitrary"),
    )(dres, dq, dk, dv, du, x, gain, w_in)


def _log_sigmoid(z):
    return jnp.minimum(z, 0.0) - jnp.log1p(jnp.exp(-jnp.abs(z)))


def _split(v):
    hi = v.astype(BF16)
    return hi, (v - hi.astype(F32)).astype(BF16)


def _tri_sum(v, tri):
    hi, lo = _split(v)
    return _mm(hi, tri) + _mm(lo, tri)


def _sb_fwd(qkv, B, S, comm=None):
    W = qkv.shape[2] // 3
    n_pair = W // LANES
    bq = _tile(S, 256)
    nq = S // bq
    scale = SB_HEAD_DIM ** -0.5

    def body(q_ref, k_ref, v_ref, o_ref, lt_ref):
        lane = lax.broadcasted_iota(jnp.int32, (1, LANES), 1)
        head0 = lane < SB_HEAD_DIM
        rr = lax.broadcasted_iota(jnp.int32, (bq, bq), 0)
        cc = lax.broadcasted_iota(jnp.int32, (bq, bq), 1)
        strict = cc < rr
        after = jnp.where(rr > cc, 1.0, 0.0).astype(BF16)

        def block(qh, ks, carry, diag):
            o_acc, c = carry
            z = _mm_nt(qh, k_ref[ks, :]) * scale
            ls = _log_sigmoid(z)
            lr = ls - z
            if diag:
                lr = jnp.where(strict, lr, 0.0)
            suf = _tri_sum(lr, after) + c
            a = jnp.exp(ls + suf)
            if diag:
                a = jnp.where(strict, a, 0.0)
            o_acc = o_acc + _mm(a.astype(BF16), v_ref[ks, :])
            return o_acc, c + jnp.sum(lr, axis=1, keepdims=True)

        def q_tile(i, _):
            qs = pl.ds(pl.multiple_of(i * bq, bq), bq)
            qv = q_ref[qs, :]
            q0 = jnp.where(head0, qv, jnp.zeros_like(qv))
            q1 = jnp.where(head0, jnp.zeros_like(qv), qv)
            zero = (jnp.zeros((bq, LANES), F32), jnp.zeros((bq, 1), F32))
            c0 = block(q0, qs, zero, True)
            c1 = block(q1, qs, zero, True)

            def left(t, cr):
                ks = pl.ds(pl.multiple_of((i - 1 - t) * bq, bq), bq)
                return block(q0, ks, cr[0], False), block(q1, ks, cr[1], False)

            (o0, l0), (o1, l1) = lax.fori_loop(0, i, left, (c0, c1))
            o_ref[qs, :] = jnp.where(head0, o0, o1).astype(BF16)
            lt_ref[qs, :] = jnp.where(head0, l0, l1)
            return 0

        lax.fori_loop(0, nq, q_tile, 0)

    def col(off):
        return pl.BlockSpec((None, S, LANES), lambda b, p: (b, 0, off + p))

    return _call(
        body, name="sb_fwd", grid=(B, n_pair), comm=comm,
        in_specs=[col(0), col(n_pair), col(2 * n_pair)],
        out_specs=(col(0), col(0)),
        out_shape=(jax.ShapeDtypeStruct((B, S, W), BF16), jax.ShapeDtypeStruct((B, S, W), F32)),
        args=(qkv, qkv, qkv))


def _sb_bwd(qkv, do, ltot, B, S, comm=None):
    W = qkv.shape[2] // 3
    n_pair = W // LANES
    bq = _tile(S, 256)
    nq = S // bq
    scale = SB_HEAD_DIM ** -0.5

    def body(q_ref, k_ref, v_ref, do_ref, lt_ref, dq_ref, dk_ref, dv_ref, dk_s, dv_s):
        lane = lax.broadcasted_iota(jnp.int32, (1, LANES), 1)
        head0 = lane < SB_HEAD_DIM
        rr = lax.broadcasted_iota(jnp.int32, (bq, bq), 0)
        cc = lax.broadcasted_iota(jnp.int32, (bq, bq), 1)
        strict = cc < rr
        upto = jnp.where(rr <= cc, 1.0, 0.0).astype(BF16)
        before = jnp.where(rr < cc, 1.0, 0.0).astype(BF16)
        dk_s[...] = jnp.zeros_like(dk_s)
        dv_s[...] = jnp.zeros_like(dv_s)

        def block(qh, doh, lt, ks, carry, diag):
            dq_acc, pl_sum, pe_sum = carry
            kk = k_ref[ks, :]
            z = _mm_nt(qh, kk) * scale
            ls = _log_sigmoid(z)
            lr = ls - z
            if diag:
                lr = jnp.where(strict, lr, 0.0)
            suf = lt - (_tri_sum(lr, upto) + pl_sum)
            a = jnp.exp(ls + suf)
            if diag:
                a = jnp.where(strict, a, 0.0)
            e = a * _mm_nt(doh, v_ref[ks, :])
            p_excl = _tri_sum(e, before) + pe_sum
            sig = jnp.exp(ls)
            dz = e * (1.0 - sig) - sig * p_excl
            if diag:
                dz = jnp.where(strict, dz, 0.0)
            dzb = (dz * scale).astype(BF16)
            dk_s[ks, :] += _mm_tn(dzb, qh)
            dv_s[ks, :] += _mm_tn(a.astype(BF16), doh)
            return (dq_acc + _mm(dzb, kk), pl_sum + jnp.sum(lr, axis=1, keepdims=True),
                    pe_sum + jnp.sum(e, axis=1, keepdims=True))

        def q_tile(i, _):
            qs = pl.ds(pl.multiple_of(i * bq, bq), bq)
            qv, dov, ltv = q_ref[qs, :], do_ref[qs, :], lt_ref[qs, :]
            zq, zd = jnp.zeros_like(qv), jnp.zeros_like(dov)
            q0, q1 = jnp.where(head0, qv, zq), jnp.where(head0, zq, qv)
            do0, do1 = jnp.where(head0, dov, zd), jnp.where(head0, zd, dov)
            lt0 = jnp.max(jnp.where(head0, ltv, -jnp.inf), axis=1, keepdims=True)
            lt1 = jnp.max(jnp.where(head0, -jnp.inf, ltv), axis=1, keepdims=True)
            zero = (jnp.zeros((bq, LANES), F32), jnp.zeros((bq, 1), F32), jnp.zeros((bq, 1), F32))

            def left(t, cr):
                ks = pl.ds(pl.multiple_of(t * bq, bq), bq)
                return block(q0, do0, lt0, ks, cr[0], False), block(q1, do1, lt1, ks, cr[1], False)

            c0, c1 = lax.fori_loop(0, i, left, (zero, zero))
            c0 = block(q0, do0, lt0, qs, c0, True)
            c1 = block(q1, do1, lt1, qs, c1, True)
            dq_ref[qs, :] = jnp.where(head0, c0[0], c1[0]).astype(BF16)
            return 0

        lax.fori_loop(0, nq, q_tile, 0)
        dk_ref[...] = dk_s[...].astype(BF16)
        dv_ref[...] = dv_s[...].astype(BF16)

    def col(off):
        return pl.BlockSpec((None, S, LANES), lambda b, p: (b, 0, off + p))

    shp = jax.ShapeDtypeStruct((B, S, W), BF16)
    return _call(
        body, name="sb_bwd", grid=(B, n_pair), comm=comm,
        in_specs=[col(0), col(n_pair), col(2 * n_pair), col(0), col(0)],
        out_specs=(col(0), col(0), col(0)),
        out_shape=(shp, shp, shp),
        scratch=[pltpu.VMEM((S, LANES), F32), pltpu.VMEM((S, LANES), F32)],
        args=(qkv, qkv, qkv, do, ltot))


def _pool_counts(S):
    t = lax.broadcasted_iota(jnp.int32, (S, 1), 0)
    return t, [jnp.minimum(t + 1, w).astype(F32) for w in POOL_WINDOWS]


def _pool_fwd(u, B, S):
    W = u.shape[2]

    def body(u_ref, out_ref):
        t, counts = _pool_counts(S)
        for gi, w in enumerate(POOL_WINDOWS):
            cols = slice(gi * POOL_GROUP_DIM, (gi + 1) * POOL_GROUP_DIM)
            ug = u_ref[:, cols]
            s, k = ug, 1
            while k < w:
                s = s + jnp.where(t >= k, pltpu.roll(s, k, axis=0), 0.0)
                k *= 2
            out_ref[:, cols] = (s / counts[gi] - ug).astype(BF16)

    spec = pl.BlockSpec((None, S, W), lambda b: (b, 0, 0))
    return pl.pallas_call(
        body, name="pool_fwd", grid=(B,), in_specs=[spec], out_specs=spec,
        out_shape=jax.ShapeDtypeStruct((B, S, W), BF16), compiler_params=_params("arbitrary"),
    )(u)


def _pool_bwd(dpooled, B, S):
    W = dpooled.shape[2]

    def body(d_ref, out_ref):
        t, counts = _pool_counts(S)
        for gi, w in enumerate(POOL_WINDOWS):
            cols = slice(gi * POOL_GROUP_DIM, (gi + 1) * POOL_GROUP_DIM)
            d = d_ref[:, cols]
            s, k = d / counts[gi], 1
            while k < w:
                s = s + jnp.where(t < S - k, pltpu.roll(s, S - k, axis=0), 0.0)
                k *= 2
            out_ref[:, cols] = (s - d).astype(BF16)

    spec = pl.BlockSpec((None, S, W), lambda b: (b, 0, 0))
    return pl.pallas_call(
        body, name="pool_bwd", grid=(B,), in_specs=[spec], out_specs=spec,
        out_shape=jax.ShapeDtypeStruct((B, S, W), BF16), compiler_params=_params("arbitrary"),
    )(dpooled)


def _mix_out_fwd(x, o_sb, pooled, w_pool, pool_scale, w_out):
    T, D = x.shape
    W = o_sb.shape[1]
    G = w_pool.shape[0]
    gd = POOL_GROUP_DIM
    tm = _tile(T, 512)

    def body(x_ref, osb_ref, pooled_ref, wp_ref, ps_ref, wo_ref, out_ref, mixed_ref):
        mixed_ref[:, :W] = osb_ref[...]
        for gi in range(G):
            cols = slice(gi * gd, (gi + 1) * gd)
            pw = _mm(pooled_ref[:, cols], wp_ref[gi])
            mixed_ref[:, W + gi * gd:W + (gi + 1) * gd] = (pw * ps_ref[:, cols]).astype(BF16)
        out_ref[...] = x_ref[...] + _mm(mixed_ref[...], wo_ref[...].reshape(D, D))

    row = lambda i: (i, 0)
    return pl.pallas_call(
        body, name="mix_out_fwd", grid=(T // tm,),
        in_specs=[pl.BlockSpec((tm, D), row), pl.BlockSpec((tm, W), row), pl.BlockSpec((tm, W), row),
                  pl.BlockSpec((G, gd, gd), lambda i: (0, 0, 0)), pl.BlockSpec((1, W), lambda i: (0, 0)),
                  pl.BlockSpec(w_out.shape, lambda i: (0, 0, 0))],
        out_specs=(pl.BlockSpec((tm, D), row), pl.BlockSpec((tm, D), row)),
        out_shape=(jax.ShapeDtypeStruct((T, D), F32), jax.ShapeDtypeStruct((T, D), BF16)),
        compiler_params=_params("arbitrary"),
    )(x, o_sb, pooled, w_pool, pool_scale, w_out)


def _mix_out_bwd(dx, pooled, w_pool, pool_scale, w_out):
    T, D = dx.shape
    W = pooled.shape[1]
    G = w_pool.shape[0]
    gd = POOL_GROUP_DIM
    tm = _tile(T, 512)

    def body(dx_ref, pooled_ref, wp_ref, ps_ref, wo_ref, dxb_ref, dosb_ref, dpooled_ref, dwp_ref, dps_ref):
        i = pl.program_id(0)

        @pl.when(i == 0)
        def _():
            dwp_ref[...] = jnp.zeros_like(dwp_ref)
            dps_ref[...] = jnp.zeros_like(dps_ref)

        dxb = dx_ref[...].astype(BF16)
        dxb_ref[...] = dxb
        dmixed = _mm_nt(dxb, wo_ref[...].reshape(D, D))
        dosb_ref[...] = dmixed[:, :W].astype(BF16)
        for gi in range(G):
            cols = slice(gi * gd, (gi + 1) * gd)
            pg = pooled_ref[:, cols]
            dop = dmixed[:, W + gi * gd:W + (gi + 1) * gd]
            pw = _mm(pg, wp_ref[gi])
            dps_ref[:, cols] += jnp.sum(dop * pw, axis=0, keepdims=True)
            dpw = (dop * ps_ref[:, cols]).astype(BF16)
            dwp_ref[gi] += _mm_tn(pg, dpw)
            dpooled_ref[:, cols] = _mm_nt(dpw, wp_ref[gi])

    row = lambda i: (i, 0)
    return pl.pallas_call(
        body, name="mix_out_bwd", grid=(T // tm,),
        in_specs=[pl.BlockSpec((tm, D), row), pl.BlockSpec((tm, W), row),
                  pl.BlockSpec((G, gd, gd), lambda i: (0, 0, 0)), pl.BlockSpec((1, W), lambda i: (0, 0)),
                  pl.BlockSpec(w_out.shape, lambda i: (0, 0, 0))],
        out_specs=(pl.BlockSpec((tm, D), row), pl.BlockSpec((tm, W), row), pl.BlockSpec((tm, W), row),
                   pl.BlockSpec((G, gd, gd), lambda i: (0, 0, 0)), pl.BlockSpec((1, W), lambda i: (0, 0))),
        out_shape=(jax.ShapeDtypeStruct((T, D), BF16), jax.ShapeDtypeStruct((T, W), BF16),
                   jax.ShapeDtypeStruct((T, W), F32), jax.ShapeDtypeStruct((G, gd, gd), F32),
                   jax.ShapeDtypeStruct((1, W), F32)),
        compiler_params=_params("arbitrary"),
    )(dx, pooled, w_pool, pool_scale, w_out)


def _mem_kv_fwd(mem, gain, w_kv):
    B, M, D = mem.shape
    cs = w_kv.shape[2]

    def body(mem_ref, gain_ref, w_ref, memn_ref, kv_ref):
        _, xhat = _rms(mem_ref[...])
        mn = (xhat * gain_ref[...]).astype(BF16)
        memn_ref[...] = mn
        for b in range(N_DEV):
            kv_ref[:, b * cs:(b + 1) * cs] = _mm(mn, w_ref[b]).astype(BF16)

    return pl.pallas_call(
        body, name="mem_kv_fwd", grid=(B,),
        in_specs=[pl.BlockSpec((None, M, D), lambda b: (b, 0, 0)), pl.BlockSpec((1, D), lambda b: (0, 0)),
                  pl.BlockSpec((N_DEV, D, cs), lambda b: (0, 0, 0))],
        out_specs=(pl.BlockSpec((M, D), lambda b: (b, 0)), pl.BlockSpec((None, M, N_DEV * cs), lambda b: (b, 0, 0))),
        out_shape=(jax.ShapeDtypeStruct((B * M, D), BF16), jax.ShapeDtypeStruct((B, M, N_DEV * cs), BF16)),
        compiler_params=_params("arbitrary"),
    )(mem, gain, w_kv)


def _mem_kv_bwd(dkv, mem, w_kv):
    B, M, D = mem.shape
    cs = w_kv.shape[2]

    def body(dkv_ref, mem_ref, w_ref, dkvb_ref, dgain_ref):
        b_id = pl.program_id(0)

        @pl.when(b_id == 0)
        def _():
            dgain_ref[...] = jnp.zeros_like(dgain_ref)

        dkvb = dkv_ref[...].astype(BF16)
        dkvb_ref[...] = dkvb
        dmn = jnp.zeros((M, D), F32)
        for b in range(N_DEV):
            dmn = dmn + _mm_nt(dkvb[:, b * cs:(b + 1) * cs], w_ref[b])
        _, xhat = _rms(mem_ref[...])
        dgain_ref[...] += jnp.sum(dmn * xhat, axis=0, keepdims=True)

    return pl.pallas_call(
        body, name="mem_kv_bwd", grid=(B,),
        in_specs=[pl.BlockSpec((None, M, N_DEV * cs), lambda b: (b, 0, 0)),
                  pl.BlockSpec((None, M, D), lambda b: (b, 0, 0)),
                  pl.BlockSpec((N_DEV, D, cs), lambda b: (0, 0, 0))],
        out_specs=(pl.BlockSpec((M, N_DEV * cs), lambda b: (b, 0)), pl.BlockSpec((1, D), lambda b: (0, 0))),
        out_shape=(jax.ShapeDtypeStruct((B * M, N_DEV * cs), BF16), jax.ShapeDtypeStruct((1, D), F32)),
        compiler_params=_params("arbitrary"),
    )(dkv, mem, w_kv)


def _softmax_rows(s):
    p = jnp.exp(s - jnp.max(s, axis=1, keepdims=True))
    return p / jnp.sum(p, axis=1, keepdims=True)


def _cross_fwd(x, gain, kv, w_q, w_o, B, S):
    T, D = x.shape
    M = kv.shape[1]
    hd = D // MEM_HEADS
    tm = _tile(S, 512)
    per = S // tm
    scale = hd ** -0.5

    def body(x_ref, gain_ref, kv_ref, wq_ref, wo_ref, out_ref, hq_ref, q_ref, ocat_ref):
        _, xhat = _rms(x_ref[...])
        hq = (xhat * gain_ref[...]).astype(BF16)
        hq_ref[...] = hq
        q = _mm(hq, wq_ref[...].reshape(D, D)).astype(BF16)
        q_ref[...] = q
        for h in range(MEM_HEADS):
            cols = slice(h * hd, (h + 1) * hd)
            s = _mm_nt(q[:, cols], kv_ref[:, cols]) * scale
            p = _softmax_rows(s).astype(BF16)
            ocat_ref[:, cols] = _mm(p, kv_ref[:, D + h * hd:D + (h + 1) * hd]).astype(BF16)
        out_ref[...] = x_ref[...] + _mm(ocat_ref[...], wo_ref[...].reshape(D, D))

    row = lambda b, t: (b * per + t, 0)
    wspec = pl.BlockSpec(w_q.shape, lambda b, t: (0, 0, 0))
    return pl.pallas_call(
        body, name="cross_fwd", grid=(B, per),
        in_specs=[pl.BlockSpec((tm, D), row), pl.BlockSpec((1, D), lambda b, t: (0, 0)),
                  pl.BlockSpec((None, M, 2 * D), lambda b, t: (b, 0, 0)), wspec, wspec],
        out_specs=tuple(pl.BlockSpec((tm, D), row) for _ in range(4)),
        out_shape=(jax.ShapeDtypeStruct((T, D), F32),) + tuple(jax.ShapeDtypeStruct((T, D), BF16) for _ in range(3)),
        compiler_params=_params("arbitrary", "arbitrary"),
    )(x, gain, kv, w_q, w_o)


def _cross_bwd(dy, x, gain, q, kv, w_q, w_o, B, S):
    T, D = x.shape
    M = kv.shape[1]
    hd = D // MEM_HEADS
    tm = _tile(S, 512)
    per = S // tm
    scale = hd ** -0.5

    def body(dy_ref, x_ref, gain_ref, q_ref, kv_ref, wq_ref, wo_ref,
             dx_ref, dyb_ref, dqb_ref, dkv_ref, dgain_ref):
        b_id, t_id = pl.program_id(0), pl.program_id(1)

        @pl.when((b_id == 0) & (t_id == 0))
        def _():
            dgain_ref[...] = jnp.zeros_like(dgain_ref)

        @pl.when(t_id == 0)
        def _():
            dkv_ref[...] = jnp.zeros_like(dkv_ref)

        dyb = dy_ref[...].astype(BF16)
        dyb_ref[...] = dyb
        docat = _mm_nt(dyb, wo_ref[...].reshape(D, D)).astype(BF16)
        for h in range(MEM_HEADS):
            cols = slice(h * hd, (h + 1) * hd)
            vcols = slice(D + h * hd, D + (h + 1) * hd)
            qh, kh, vh, doh = q_ref[:, cols], kv_ref[:, cols], kv_ref[:, vcols], docat[:, cols]
            p = _softmax_rows(_mm_nt(qh, kh) * scale)
            dp = _mm_nt(doh, vh)
            ds = (p * (dp - jnp.sum(dp * p, axis=1, keepdims=True)) * scale).astype(BF16)
            dqb_ref[:, cols] = _mm(ds, kh).astype(BF16)
            dkv_ref[:, cols] += _mm_tn(ds, qh)
            dkv_ref[:, vcols] += _mm_tn(p.astype(BF16), doh)
        dhq = _mm_nt(dqb_ref[...], wq_ref[...].reshape(D, D))
        r, xhat = _rms(x_ref[...])
        dgain_ref[...] += jnp.sum(dhq * xhat, axis=0, keepdims=True)
        dx_ref[...] = dy_ref[...] + _rms_bwd(dhq, gain_ref[...], r, xhat)

    row = lambda b, t: (b * per + t, 0)
    wspec = pl.BlockSpec(w_q.shape, lambda b, t: (0, 0, 0))
    one = pl.BlockSpec((1, D), lambda b, t: (0, 0))
    kvspec = pl.BlockSpec((None, M, 2 * D), lambda b, t: (b, 0, 0))
    return pl.pallas_call(
        body, name="cross_bwd", grid=(B, per),
        in_specs=[pl.BlockSpec((tm, D), row), pl.BlockSpec((tm, D), row), one, pl.BlockSpec((tm, D), row),
                  kvspec, wspec, wspec],
        out_specs=(pl.BlockSpec((tm, D), row), pl.BlockSpec((tm, D), row), pl.BlockSpec((tm, D), row), kvspec, one),
        out_shape=(jax.ShapeDtypeStruct((T, D), F32), jax.ShapeDtypeStruct((T, D), BF16),
                   jax.ShapeDtypeStruct((T, D), BF16), jax.ShapeDtypeStruct((B, M, 2 * D), F32),
                   jax.ShapeDtypeStruct((1, D), F32)),
        compiler_params=_params("arbitrary", "arbitrary"),
    )(dy, x, gain, q, kv, w_q, w_o)


def _final(x, gain, target):
    T, D = x.shape
    tm = _tile(T, 512)

    def body(x_ref, gain_ref, tgt_ref, dx_ref, dgain_ref, loss_ref):
        i = pl.program_id(0)

        @pl.when(i == 0)
        def _():
            dgain_ref[...] = jnp.zeros_like(dgain_ref)
            loss_ref[...] = jnp.zeros_like(loss_ref)

        r, xhat = _rms(x_ref[...])
        err = xhat * gain_ref[...] - tgt_ref[...]
        loss_ref[...] += 0.5 * jnp.sum(jnp.mean(err * err, axis=-1, keepdims=True), axis=0, keepdims=True)
        dy = err * (1.0 / D)
        dgain_ref[...] += jnp.sum(dy * xhat, axis=0, keepdims=True)
        dx_ref[...] = _rms_bwd(dy, gain_ref[...], r, xhat)

    row = lambda i: (i, 0)
    one = pl.BlockSpec((1, D), lambda i: (0, 0))
    return pl.pallas_call(
        body, name="final_loss", grid=(T // tm,),
        in_specs=[pl.BlockSpec((tm, D), row), one, pl.BlockSpec((tm, D), row)],
        out_specs=(pl.BlockSpec((tm, D), row), one, pl.BlockSpec((8, LANES), lambda i: (0, 0))),
        out_shape=(jax.ShapeDtypeStruct((T, D), F32), jax.ShapeDtypeStruct((1, D), F32),
                   jax.ShapeDtypeStruct((8, LANES), F32)),
        compiler_params=_params("arbitrary"),
    )(x, gain, target)


def _adamw(gparts, w, m, v, name):
    R, C = w.shape
    tr = _tile(R, 256)

    def body(gp_ref, w_ref, m_ref, v_ref, g_ref, d_ref, nm_ref, nv_ref):
        g = gp_ref[0].astype(F32)
        for s in range(1, N_DEV):
            g = g + gp_ref[s].astype(F32)
        nm = ADAM_B1 * m_ref[...] + (1.0 - ADAM_B1) * g
        nv = ADAM_B2 * v_ref[...] + (1.0 - ADAM_B2) * (g * g)
        m_hat = nm / (1.0 - ADAM_B1 ** ADAM_STEP)
        v_hat = nv / (1.0 - ADAM_B2 ** ADAM_STEP)
        g_ref[...] = g
        nm_ref[...] = nm
        nv_ref[...] = nv
        d_ref[...] = -ADAM_LR * (m_hat / (jnp.sqrt(v_hat) + ADAM_EPS) + ADAM_WD * w_ref[...])

    spec = pl.BlockSpec((tr, C), lambda i: (i, 0))
    shp = jax.ShapeDtypeStruct((R, C), F32)
    return pl.pallas_call(
        body, name=name, grid=(R // tr,),
        in_specs=[pl.BlockSpec((N_DEV, tr, C), lambda i: (0, i, 0)), spec, spec, spec],
        out_specs=(spec, spec, spec, spec), out_shape=(shp, shp, shp, shp),
        compiler_params=_params("arbitrary"),
    )(gparts, w, m, v)


def _rows128(a, rows):
    a = a.reshape(-1, LANES)
    return jnp.pad(a, ((0, rows - a.shape[0]), (0, 0)))


def kernel(x, mem, ffn1_norm, ffn1_w_gate, ffn1_w_up, ffn1_w_down, mix_norm, w_in, w_pool, pool_scale, w_out, mem_q_norm, mem_kv_norm, mem_w_q, mem_w_kv, mem_w_o, ffn2_norm, ffn2_w_gate, ffn2_w_up, ffn2_w_down, final_norm, loss_target, m_ffn1_norm, m_ffn1_w_gate, m_ffn1_w_up, m_ffn1_w_down, m_mix_norm, m_w_in, m_w_pool, m_pool_scale, m_w_out, m_mem_q_norm, m_mem_kv_norm, m_mem_w_q, m_mem_w_kv, m_mem_w_o, m_ffn2_norm, m_ffn2_w_gate, m_ffn2_w_up, m_ffn2_w_down, m_final_norm, v_ffn1_norm, v_ffn1_w_gate, v_ffn1_w_up, v_ffn1_w_down, v_mix_norm, v_w_in, v_w_pool, v_pool_scale, v_w_out, v_mem_q_norm, v_mem_kv_norm, v_mem_w_q, v_mem_w_kv, v_mem_w_o, v_ffn2_norm, v_ffn2_w_gate, v_ffn2_w_up, v_ffn2_w_down, v_final_norm):
    B, S, D = x.shape
    T = B * S
    x0 = x.reshape(T, D)
    target = loss_target.reshape(T, D)
    final_gain = final_norm.reshape(1, D)

    big = dict(
        g1=ffn1_w_gate[0].T, u1=ffn1_w_up[0].T, d1=ffn1_w_down[0],
        g2=ffn2_w_gate[0].T, u2=ffn2_w_up[0].T, d2=ffn2_w_down[0],
        w_in=w_in[0], w_out=w_out[0], w_q=mem_w_q[0], w_kv=mem_w_kv[0], w_o=mem_w_o[0])
    names = list(big)
    shard = {k: big[k].astype(BF16) for k in names}
    wp = w_pool[0].astype(BF16)
    full, ffn_w = {}, {}

    def gathered(keys, arrs):
        full.update(zip(keys, arrs))
        ffn_w.update({k: full[k].reshape(-1, D) for k in keys if k[0] in "gud"})

    first, mid, late = ("g1", "u1", "d1"), ("w_in", "w_out", "w_q", "w_kv", "w_o"), ("g2", "u2", "d2")
    gathered(first, _exchange([shard[k] for k in first], True, "gather_ffn1"))
    (x1, hn1, gg1, uu1), got = _ffn_fwd(x0, ffn1_norm, ffn_w["g1"], ffn_w["u1"], ffn_w["d1"], "ffn1_fwd",
                                        comm=([shard[k] for k in mid], True))
    gathered(mid, got)
    hn2, qkv, u = _mix_in_fwd(x1, mix_norm, full["w_in"])
    qkv3 = qkv.reshape(B, S, -1)
    (o_sb, ltot), got = _sb_fwd(qkv3, B, S, comm=([shard[k] for k in late], True))
    gathered(late, got)
    pooled = _pool_fwd(u.reshape(B, S, -1), B, S).reshape(T, -1)
    x2, mixed = _mix_out_fwd(x1, o_sb.reshape(T, -1), pooled, wp, pool_scale, full["w_out"])
    memn, kv = _mem_kv_fwd(mem, mem_kv_norm, full["w_kv"])
    x3, hq, q, ocat = _cross_fwd(x2, mem_q_norm, kv, full["w_q"], full["w_o"], B, S)
    (x4, hn4, gg2, uu2), _ = _ffn_fwd(x3, ffn2_norm, ffn_w["g2"], ffn_w["u2"], ffn_w["d2"], "ffn2_fwd")
    dx4, d_final, loss_part = _final(x4, final_gain, target)

    slab = lambda k: grads[k].reshape((N_DEV, -1) + grads[k].shape[-1:])
    got = {}
    (dx3, a2, dg2, du2, dyh2, d_ffn2), _ = _ffn_bwd(dx4, x3, ffn2_norm, gg2, uu2, ffn_w["g2"], ffn_w["u2"],
                                                   ffn_w["d2"], "ffn2_bwd")
    grads = dict(g2=_wgrad(dg2, hn4, "dw_gate2"), u2=_wgrad(du2, hn4, "dw_up2"), d2=_wgrad(a2, dyh2, "dw_down2"))
    dx2, dx3b, dqb, dkv, d_q = _cross_bwd(dx3, x2, mem_q_norm, q, kv, full["w_q"], full["w_o"], B, S)
    grads["w_o"] = _wgrad(ocat, dx3b, "dw_o")
    grads["w_q"] = _wgrad(hq, dqb, "dw_q")
    dkvb, d_kv = _mem_kv_bwd(dkv, mem, full["w_kv"])
    grads["w_kv"] = _wgrad(memn, dkvb, "dw_kv", col_slab=full["w_kv"].shape[2])
    dx2b, do_sb, dpooled, d_wpool, d_ps = _mix_out_bwd(dx2, pooled, wp, pool_scale, full["w_out"])
    grads["w_out"] = _wgrad(mixed, dx2b, "dw_out")
    du = _pool_bwd(dpooled.reshape(B, S, -1), B, S).reshape(T, -1)
    early = ("g2", "u2", "d2", "w_o", "w_q", "w_kv", "w_out")
    (dq, dk, dv), res = _sb_bwd(qkv3, do_sb.reshape(B, S, -1), ltot, B, S, comm=([slab(k) for k in early], False))
    got.update(zip(early, res))
    dx1, dproj, d_mix = _mix_in_bwd(dx2, dq.reshape(T, -1), dk.reshape(T, -1), dv.reshape(T, -1), du,
                                    x1, mix_norm, full["w_in"])
    grads["w_in"] = _wgrad(hn2, dproj, "dw_in", col_slab=full["w_in"].shape[2])
    (dx0, a1, dg1, du1, dyh1, d_ffn1), res = _ffn_bwd(dx1, x0, ffn1_norm, gg1, uu1, ffn_w["g1"], ffn_w["u1"],
                                                     ffn_w["d1"], "ffn1_bwd", comm=([slab("w_in")], False))
    got["w_in"] = res[0]

    small = [("ffn1_norm", d_ffn1, ffn1_norm, m_ffn1_norm, v_ffn1_norm),
             ("mix_norm", d_mix, mix_norm, m_mix_norm, v_mix_norm),
             ("w_pool", d_wpool, w_pool, m_w_pool, v_w_pool),
             ("pool_scale", d_ps, pool_scale, m_pool_scale, v_pool_scale),
             ("mem_q_norm", d_q, mem_q_norm, m_mem_q_norm, v_mem_q_norm),
             ("mem_kv_norm", d_kv, mem_kv_norm, m_mem_kv_norm, v_mem_kv_norm),
             ("ffn2_norm", d_ffn2, ffn2_norm, m_ffn2_norm, v_ffn2_norm),
             ("final_norm", d_final, final_norm, m_final_norm, v_final_norm)]
    rows = [max(8, t[2].size // LANES) for t in small]
    pack = lambda idx: jnp.concatenate([_rows128(t[idx], r) for t, r in zip(small, rows)]
                                       + ([loss_part] if idx == 1 else [jnp.zeros((8, LANES), F32)]))
    grads["g1"], (small_parts,) = _wgrad(dg1, hn1, "dw_gate1", comm=([pack(1)], True))
    grads["u1"], (got["g1"],) = _wgrad(du1, hn1, "dw_up1", comm=([slab("g1")], False))
    grads["d1"], (got["u1"],) = _wgrad(a1, dyh1, "dw_down1", comm=([slab("u1")], False))
    got["d1"] = _exchange([slab("d1")], False, "scatter_last")[0]

    state = dict(
        g1=(ffn1_w_gate, m_ffn1_w_gate, v_ffn1_w_gate), u1=(ffn1_w_up, m_ffn1_w_up, v_ffn1_w_up),
        d1=(ffn1_w_down, m_ffn1_w_down, v_ffn1_w_down), g2=(ffn2_w_gate, m_ffn2_w_gate, v_ffn2_w_gate),
        u2=(ffn2_w_up, m_ffn2_w_up, v_ffn2_w_up), d2=(ffn2_w_down, m_ffn2_w_down, v_ffn2_w_down),
        w_in=(w_in, m_w_in, v_w_in), w_out=(w_out, m_w_out, v_w_out), w_q=(mem_w_q, m_mem_w_q, v_mem_w_q),
        w_kv=(mem_w_kv, m_mem_w_kv, v_mem_w_kv), w_o=(mem_w_o, m_mem_w_o, v_mem_w_o))
    big_out = {}
    for k in names:
        parts = got[k]
        if k in ("g1", "u1", "g2", "u2"):
            parts = jnp.swapaxes(parts, 1, 2)
        w_, m_, v_ = (t[0] for t in state[k])
        big_out[k] = [t[None] for t in _adamw(parts, w_, m_, v_, "adamw_" + k)]

    small_res = _adamw(small_parts, pack(2), pack(3), pack(4), "adamw_small")
    small_out, off = {}, 0
    for t, r in zip(small, rows):
        n = t[2].size // LANES
        small_out[t[0]] = [res[off:off + n].reshape(t[2].shape) for res in small_res]
        off += r
    loss = small_res[0][off, 0]

    order = [("ffn1_norm", None), ("ffn1_w_gate", "g1"), ("ffn1_w_up", "u1"), ("ffn1_w_down", "d1"),
             ("mix_norm", None), ("w_in", "w_in"), ("w_pool", None), ("pool_scale", None), ("w_out", "w_out"),
             ("mem_q_norm", None), ("mem_kv_norm", None), ("mem_w_q", "w_q"), ("mem_w_kv", "w_kv"),
             ("mem_w_o", "w_o"), ("ffn2_norm", None), ("ffn2_w_gate", "g2"), ("ffn2_w_up", "u2"),
             ("ffn2_w_down", "d2"), ("final_norm", None)]
    res = [loss, dx0.reshape(B, S, D)]
    for which in range(4):
        for name, key in order:
            res.append(big_out[key][which] if key else small_out[name][which])
    return tuple(res)
```

```python
import functools

import jax
import jax.numpy as jnp
from jax import lax
from jax.experimental import pallas as pl
from jax.experimental.pallas import tpu as pltpu

F32 = jnp.float32
BF16 = jnp.bfloat16

N_DEV = 8
EPS = 1e-6
SB_HEAD_DIM = 64
LANES = 128
POOL_WINDOWS = (2, 4, 8, 16)
POOL_GROUP_DIM = 128
MEM_HEADS = 4
FFN_RESIDUAL_WEIGHT = 0.5
ADAM_LR = 0.001
ADAM_B1 = 0.9
ADAM_B2 = 0.999
ADAM_EPS = 1e-08
ADAM_WD = 0.01
ADAM_STEP = 10
VMEM_LIMIT = 56 * 1024 * 1024

MESH_ID = pl.DeviceIdType.MESH


def _params(*sem):
    return pltpu.CompilerParams(dimension_semantics=sem, vmem_limit_bytes=VMEM_LIMIT)


def _tile(n, pref):
    if n <= pref:
        return n
    t = pref - pref % 8
    while n % t:
        t -= 8
    return t


def _mm(a, b):
    return jnp.dot(a, b, preferred_element_type=F32)


def _mm_nt(a, b):
    return lax.dot_general(a, b, (((1,), (1,)), ((), ())), preferred_element_type=F32)


def _mm_tn(a, b):
    return lax.dot_general(a, b, (((0,), (0,)), ((), ())), preferred_element_type=F32)


def _rms(xv):
    r = lax.rsqrt(jnp.mean(xv * xv, axis=-1, keepdims=True) + EPS)
    return r, xv * r


def _rms_bwd(dhn, gain, r, xhat):
    dxh = dhn * gain
    return r * (dxh - xhat * jnp.mean(dxh * xhat, axis=-1, keepdims=True))


def _sigmoid(z):
    return 1.0 / (1.0 + jnp.exp(-z))


def _comm_shapes(arrs, gather):
    return tuple(jax.ShapeDtypeStruct(((N_DEV,) + tuple(a.shape)) if gather else tuple(a.shape), a.dtype)
                 for a in arrs)


def _comm_start(ins, outs, sems, gather):
    send_sems, recv_sems, local_sems = sems
    x, y, c = lax.axis_index("x"), lax.axis_index("y"), lax.axis_index("c")
    me = 4 * x + 2 * y + c
    for i in range(len(ins)):
        src = ins[i] if gather else ins[i].at[me]
        pltpu.make_async_copy(src, outs[i].at[me], local_sems.at[i]).start()
    for k in range(1, N_DEV):
        px = 1 - x if k & 4 else x
        py = 1 - y if k & 2 else y
        pc = 1 - c if k & 1 else c
        peer = 4 * px + 2 * py + pc
        for i in range(len(ins)):
            src = ins[i] if gather else ins[i].at[peer]
            pltpu.make_async_remote_copy(
                src_ref=src, dst_ref=outs[i].at[me],
                send_sem=send_sems.at[i], recv_sem=recv_sems.at[i],
                device_id=(px, py, pc), device_id_type=MESH_ID).start()


def _comm_wait(ins, outs, sems, gather):
    send_sems, recv_sems, local_sems = sems
    x, y, c = lax.axis_index("x"), lax.axis_index("y"), lax.axis_index("c")
    me = 4 * x + 2 * y + c
    for i in range(len(ins)):
        seven = outs[i].at[pl.ds(0, N_DEV - 1)]
        done = pltpu.make_async_remote_copy(
            src_ref=seven, dst_ref=seven,
            send_sem=send_sems.at[i], recv_sem=recv_sems.at[i],
            device_id=(x, y, c), device_id_type=MESH_ID)
        done.wait_send()
        done.wait_recv()
        src = ins[i] if gather else ins[i].at[me]
        pltpu.make_async_copy(src, outs[i].at[me], local_sems.at[i]).wait()


def _comm_sems(n):
    return [pltpu.SemaphoreType.DMA((n,)) for _ in range(3)]


def _exchange(arrs, gather, name):
    n = len(arrs)

    def body(*refs):
        ins, outs, sems = refs[:n], refs[n:2 * n], refs[2 * n:]
        _comm_start(ins, outs, sems, gather)
        _comm_wait(ins, outs, sems, gather)

    any_spec = pl.BlockSpec(memory_space=pl.ANY)
    outs = pl.pallas_call(
        body, name=name, out_shape=_comm_shapes(arrs, gather),
        in_specs=[any_spec] * n, out_specs=tuple([any_spec] * n), scratch_shapes=_comm_sems(n),
    )(*arrs)
    return list(outs)


def _call(body, *, name, grid, in_specs, out_specs, out_shape, args, scratch=(), comm=None):
    sem = ("arbitrary",) * len(grid)
    if comm is None:
        res = pl.pallas_call(body, name=name, grid=grid, in_specs=list(in_specs), out_specs=tuple(out_specs),
                             out_shape=tuple(out_shape), scratch_shapes=list(scratch),
                             compiler_params=_params(*sem))(*args)
        return tuple(res), []
    arrs, gather = comm
    n, n_in, n_out, n_sc = len(arrs), len(args), len(out_shape), len(scratch)

    def wrapped(*refs):
        ins, cin = refs[:n_in], refs[n_in:n_in + n]
        outs, cout = refs[n_in + n:n_in + n + n_out], refs[n_in + n + n_out:n_in + 2 * n + n_out]
        sc, sems = refs[n_in + 2 * n + n_out:n_in + 2 * n + n_out + n_sc], refs[n_in + 2 * n + n_out + n_sc:]
        ids = [pl.program_id(a) for a in range(len(grid))]
        first = functools.reduce(jnp.logical_and, [i == 0 for i in ids])
        last = functools.reduce(jnp.logical_and, [i == g - 1 for i, g in zip(ids, grid)])

        @pl.when(first)
        def _():
            _comm_start(cin, cout, sems, gather)

        body(*ins, *outs, *sc)

        @pl.when(last)
        def _():
            _comm_wait(cin, cout, sems, gather)

    any_spec = pl.BlockSpec(memory_space=pl.ANY)
    res = pl.pallas_call(
        wrapped, name=name, grid=grid, in_specs=list(in_specs) + [any_spec] * n,
        out_specs=tuple(out_specs) + (any_spec,) * n, out_shape=tuple(out_shape) + _comm_shapes(arrs, gather),
        scratch_shapes=list(scratch) + _comm_sems(n), compiler_params=_params(*sem))(*args, *arrs)
    return tuple(res[:n_out]), list(res[n_out:])


def _ffn_fwd(x, gain, wgt, wut, wd, name, comm=None):
    T, D = x.shape
    F = wd.shape[0]
    tm, tf = _tile(T, 1024), _tile(F, 256)
    nj = F // tf

    def body(x_ref, gain_ref, wg_ref, wu_ref, wd_ref, out_ref, hn_ref, g_ref, u_ref, acc, hn_s):
        j = pl.program_id(1)

        @pl.when(j == 0)
        def _():
            _, xhat = _rms(x_ref[...])
            hn = (xhat * gain_ref[...]).astype(BF16)
            hn_s[...] = hn
            hn_ref[...] = hn
            acc[...] = jnp.zeros_like(acc)

        hn = hn_s[...]
        g = _mm_nt(hn, wg_ref[...])
        u = _mm_nt(hn, wu_ref[...])
        g_ref[...] = g.astype(BF16)
        u_ref[...] = u.astype(BF16)
        a = (g * _sigmoid(g) * u).astype(BF16)
        acc[...] += _mm(a, wd_ref[...])

        @pl.when(j == nj - 1)
        def _():
            out_ref[...] = x_ref[...] + FFN_RESIDUAL_WEIGHT * acc[...]

    row = lambda i, j: (i, 0)
    wspec = pl.BlockSpec((tf, D), lambda i, j: (j, 0))
    return _call(
        body, name=name, grid=(T // tm, nj), comm=comm,
        in_specs=[pl.BlockSpec((tm, D), row), pl.BlockSpec((1, D), lambda i, j: (0, 0)), wspec, wspec, wspec],
        out_specs=(pl.BlockSpec((tm, D), row), pl.BlockSpec((tm, D), row),
                   pl.BlockSpec((tm, tf), lambda i, j: (i, j)), pl.BlockSpec((tm, tf), lambda i, j: (i, j))),
        out_shape=(jax.ShapeDtypeStruct((T, D), F32), jax.ShapeDtypeStruct((T, D), BF16),
                   jax.ShapeDtypeStruct((T, F), BF16), jax.ShapeDtypeStruct((T, F), BF16)),
        scratch=[pltpu.VMEM((tm, D), F32), pltpu.VMEM((tm, D), BF16)],
        args=(x, gain, wgt, wut, wd))


def _ffn_bwd(dy, x, gain, g, u, wgt, wut, wd, name, comm=None):
    T, D = x.shape
    F = wd.shape[0]
    tm, tf = _tile(T, 1024), _tile(F, 256)
    nj = F // tf

    def body(dy_ref, x_ref, gain_ref, g_ref, u_ref, wg_ref, wu_ref, wd_ref,
             dx_ref, a_ref, dg_ref, du_ref, dyh_ref, dgain_ref, acc, dyh_s):
        i, j = pl.program_id(0), pl.program_id(1)

        @pl.when(j == 0)
        def _():
            dyh = (FFN_RESIDUAL_WEIGHT * dy_ref[...]).astype(BF16)
            dyh_s[...] = dyh
            dyh_ref[...] = dyh
            acc[...] = jnp.zeros_like(acc)

        @pl.when((i == 0) & (j == 0))
        def _():
            dgain_ref[...] = jnp.zeros_like(dgain_ref)

        gv = g_ref[...].astype(F32)
        uv = u_ref[...].astype(F32)
        da = _mm_nt(dyh_s[...], wd_ref[...])
        sig = _sigmoid(gv)
        s = gv * sig
        a_ref[...] = (s * uv).astype(BF16)
        dg = (da * uv * (sig * (1.0 + gv * (1.0 - sig)))).astype(BF16)
        du = (da * s).astype(BF16)
        dg_ref[...] = dg
        du_ref[...] = du
        acc[...] += _mm(dg, wg_ref[...]) + _mm(du, wu_ref[...])

        @pl.when(j == nj - 1)
        def _():
            r, xhat = _rms(x_ref[...])
            dhn = acc[...]
            dgain_ref[...] += jnp.sum(dhn * xhat, axis=0, keepdims=True)
            dx_ref[...] = dy_ref[...] + _rms_bwd(dhn, gain_ref[...], r, xhat)

    row = lambda i, j: (i, 0)
    tile = lambda i, j: (i, j)
    wspec = pl.BlockSpec((tf, D), lambda i, j: (j, 0))
    one = pl.BlockSpec((1, D), lambda i, j: (0, 0))
    return _call(
        body, name=name, grid=(T // tm, nj), comm=comm,
        in_specs=[pl.BlockSpec((tm, D), row), pl.BlockSpec((tm, D), row), one,
                  pl.BlockSpec((tm, tf), tile), pl.BlockSpec((tm, tf), tile), wspec, wspec, wspec],
        out_specs=(pl.BlockSpec((tm, D), row), pl.BlockSpec((tm, tf), tile), pl.BlockSpec((tm, tf), tile),
                   pl.BlockSpec((tm, tf), tile), pl.BlockSpec((tm, D), row), one),
        out_shape=(jax.ShapeDtypeStruct((T, D), F32), jax.ShapeDtypeStruct((T, F), BF16),
                   jax.ShapeDtypeStruct((T, F), BF16), jax.ShapeDtypeStruct((T, F), BF16),
                   jax.ShapeDtypeStruct((T, D), BF16), jax.ShapeDtypeStruct((1, D), F32)),
        scratch=[pltpu.VMEM((tm, D), F32), pltpu.VMEM((tm, D), BF16)],
        args=(dy, x, gain, g, u, wgt, wut, wd))


def _wgrad(a, b, name, col_slab=None, comm=None):
    T, M = a.shape
    N = b.shape[1]
    tmm = M if M <= 1024 else _tile(M, 1408)
    tn = col_slab if col_slab else _tile(N, 1024)
    tk = _tile(T, 512)
    nk = T // tk

    def body(a_ref, b_ref, out_ref, acc):
        k = pl.program_id(2)

        @pl.when(k == 0)
        def _():
            acc[...] = jnp.zeros_like(acc)

        acc[...] += _mm_tn(a_ref[...], b_ref[...])

        @pl.when(k == nk - 1)
        def _():
            out_ref[...] = acc[...].astype(BF16)

    if col_slab:
        out_spec = pl.BlockSpec((None, tmm, tn), lambda m, n, k: (n, m, 0))
        out_shape = jax.ShapeDtypeStruct((N // tn, M, tn), BF16)
    else:
        out_spec = pl.BlockSpec((tmm, tn), lambda m, n, k: (m, n))
        out_shape = jax.ShapeDtypeStruct((M, N), BF16)
    (out,), got = _call(
        body, name=name, grid=(M // tmm, N // tn, nk), comm=comm,
        in_specs=[pl.BlockSpec((tk, tmm), lambda m, n, k: (k, m)), pl.BlockSpec((tk, tn), lambda m, n, k: (k, n))],
        out_specs=(out_spec,), out_shape=(out_shape,),
        scratch=[pltpu.VMEM((tmm, tn), F32)], args=(a, b))
    return (out, got) if comm else out


def _mix_in_fwd(x, gain, w_in):
    T, D = x.shape
    cs = w_in.shape[2]
    n_qkv = 3 * (N_DEV // 4)
    tm = _tile(T, 512)

    def body(x_ref, gain_ref, w_ref, hn_ref, qkv_ref, u_ref):
        _, xhat = _rms(x_ref[...])
        hn = (xhat * gain_ref[...]).astype(BF16)
        hn_ref[...] = hn
        for b in range(N_DEV):
            p = _mm(hn, w_ref[b])
            if b < n_qkv:
                qkv_ref[:, b * cs:(b + 1) * cs] = p.astype(BF16)
            else:
                u_ref[:, (b - n_qkv) * cs:(b - n_qkv + 1) * cs] = p

    row = lambda i: (i, 0)
    return pl.pallas_call(
        body, name="mix_in_fwd", grid=(T // tm,),
        in_specs=[pl.BlockSpec((tm, D), row), pl.BlockSpec((1, D), lambda i: (0, 0)),
                  pl.BlockSpec((N_DEV, D, cs), lambda i: (0, 0, 0))],
        out_specs=(pl.BlockSpec((tm, D), row), pl.BlockSpec((tm, n_qkv * cs), row),
                   pl.BlockSpec((tm, (N_DEV - n_qkv) * cs), row)),
        out_shape=(jax.ShapeDtypeStruct((T, D), BF16), jax.ShapeDtypeStruct((T, n_qkv * cs), BF16),
                   jax.ShapeDtypeStruct((T, (N_DEV - n_qkv) * cs), F32)),
        compiler_params=_params("arbitrary"),
    )(x, gain, w_in)


def _mix_in_bwd(dres, dq, dk, dv, du, x, gain, w_in):
    T, D = x.shape
    cs = w_in.shape[2]
    W = dq.shape[1]
    per = W // cs
    tm = _tile(T, 512)

    def body(dres_ref, dq_ref, dk_ref, dv_ref, du_ref, x_ref, gain_ref, w_ref, dx_ref, dproj_ref, dgain_ref):
        i = pl.program_id(0)

        @pl.when(i == 0)
        def _():
            dgain_ref[...] = jnp.zeros_like(dgain_ref)

        dhn = jnp.zeros((tm, D), F32)
        for part, ref in enumerate((dq_ref, dk_ref, dv_ref, du_ref)):
            for h in range(per):
                b = part * per + h
                d = ref[:, h * cs:(h + 1) * cs]
                dproj_ref[:, b * cs:(b + 1) * cs] = d
                dhn = dhn + _mm_nt(d, w_ref[b])
        r, xhat = _rms(x_ref[...])
        dgain_ref[...] += jnp.sum(dhn * xhat, axis=0, keepdims=True)
        dx_ref[...] = dres_ref[...] + _rms_bwd(dhn, gain_ref[...], r, xhat)

    row = lambda i: (i, 0)
    one = pl.BlockSpec((1, D), lambda i: (0, 0))
    part = pl.BlockSpec((tm, W), row)
    return pl.pallas_call(
        body, name="mix_in_bwd", grid=(T // tm,),
        in_specs=[pl.BlockSpec((tm, D), row), part, part, part, part, pl.BlockSpec((tm, D), row), one,
                  pl.BlockSpec((N_DEV, D, cs), lambda i: (0, 0, 0))],
        out_specs=(pl.BlockSpec((tm, D), row), pl.BlockSpec((tm, 4 * W), row), one),
        out_shape=(jax.ShapeDtypeStruct((T, D), F32), jax.ShapeDtypeStruct((T, 4 * W), BF16),
                   jax.ShapeDtypeStruct((1, D), F32)),
        compiler_params=_params("arbitrary"),
    )(dres, dq, dk, dv, du, x, gain, w_in)


SB_PAIRS_PER_PROGRAM = 2
LOG2_E = 1.4426950408889634
EXP2_CLAMP = 126.0


def _neg_log2_sigmoid(nz2):
    w = jnp.minimum(nz2, EXP2_CLAMP)
    return w, jnp.log2(1.0 + jnp.exp2(w))


def _split(v):
    hi = v.astype(BF16)
    return hi, (v - hi.astype(F32)).astype(BF16)


def _tri_sum(v, tri):
    hi, lo = _split(v)
    return _mm(hi, tri) + _mm(lo, tri)


def _sb_fwd(qkv, B, S, comm=None):
    W = qkv.shape[2] // 3
    n_pair = W // LANES
    bq = _tile(S, 256)
    nq = S // bq
    hp = SB_PAIRS_PER_PROGRAM
    nscale2 = -(SB_HEAD_DIM ** -0.5) * LOG2_E

    def body(q_ref, k_ref, v_ref, o_ref, lt_ref):
        lane = lax.broadcasted_iota(jnp.int32, (1, LANES), 1)
        head0 = lane < SB_HEAD_DIM
        rr = lax.broadcasted_iota(jnp.int32, (bq, bq), 0)
        cc = lax.broadcasted_iota(jnp.int32, (bq, bq), 1)
        strict = cc < rr
        after = jnp.where(rr > cc, 1.0, 0.0).astype(BF16)

        def blocks(heads, ks, carries, diag):
            n = range(len(heads))
            keep = (lambda t: jnp.where(strict, t, 0.0)) if diag else (lambda t: t)
            z = [_mm_nt(qh, k_ref[ks, cols]) for qh, cols in heads]
            wl = [_neg_log2_sigmoid(z[h] * nscale2) for h in n]
            lr = [keep(wl[h][0] - wl[h][1]) for h in n]
            parts = [_split(lr[h]) for h in n]
            suf = [_mm(parts[h][0], after) + _mm(parts[h][1], after) for h in n]
            a = [keep(jnp.exp2(suf[h] + carries[h][1] - wl[h][1])).astype(BF16) for h in n]
            o = [carries[h][0] + _mm(a[h], v_ref[ks, heads[h][1]]) for h in n]
            return tuple((o[h], carries[h][1] + (suf[h][:, :1] + lr[h][:, :1])) for h in n)

        def q_tile(i, _):
            qs = pl.ds(pl.multiple_of(i * bq, bq), bq)
            heads = []
            for pr in range(hp):
                cols = slice(pr * LANES, (pr + 1) * LANES)
                qv = q_ref[qs, cols]
                heads += [(jnp.where(head0, qv, jnp.zeros_like(qv)), cols),
                          (jnp.where(head0, jnp.zeros_like(qv), qv), cols)]
            zero = (jnp.zeros((bq, LANES), F32), jnp.zeros((bq, 1), F32))
            init = blocks(heads, qs, (zero,) * len(heads), True)

            def left(t, cr):
                ks = pl.ds(pl.multiple_of((i - 1 - t) * bq, bq), bq)
                return blocks(heads, ks, cr, False)

            res = lax.fori_loop(0, i, left, init)
            for pr in range(hp):
                (o0, l0), (o1, l1) = res[2 * pr], res[2 * pr + 1]
                o_ref[qs, heads[2 * pr][1]] = jnp.where(head0, o0, o1).astype(BF16)
                lt_ref[qs, heads[2 * pr][1]] = jnp.where(head0, l0, l1)
            return 0

        lax.fori_loop(0, nq, q_tile, 0)

    def col(off):
        return pl.BlockSpec((None, S, hp * LANES), lambda b, p: (b, 0, off + p))

    n_pair //= hp
    return _call(
        body, name="sb_fwd", grid=(B, n_pair), comm=comm,
        in_specs=[col(0), col(n_pair), col(2 * n_pair)],
        out_specs=(col(0), col(0)),
        out_shape=(jax.ShapeDtypeStruct((B, S, W), BF16), jax.ShapeDtypeStruct((B, S, W), F32)),
        args=(qkv, qkv, qkv))


def _sb_bwd(qkv, do, ltot, B, S, comm=None):
    W = qkv.shape[2] // 3
    n_pair = W // LANES
    bq = _tile(S, 256)
    nq = S // bq
    hp = SB_PAIRS_PER_PROGRAM
    scale = SB_HEAD_DIM ** -0.5
    nscale2 = -scale * LOG2_E

    def body(q_ref, k_ref, v_ref, do_ref, lt_ref, dq_ref, dk_ref, dv_ref, dk_s, dv_s):
        lane = lax.broadcasted_iota(jnp.int32, (1, LANES), 1)
        head0 = lane < SB_HEAD_DIM
        rr = lax.broadcasted_iota(jnp.int32, (bq, bq), 0)
        cc = lax.broadcasted_iota(jnp.int32, (bq, bq), 1)
        strict = cc < rr
        upto = jnp.where(rr <= cc, 1.0, 0.0).astype(BF16)
        before = jnp.where(rr < cc, 1.0, 0.0).astype(BF16)
        dk_s[...] = jnp.zeros_like(dk_s)
        dv_s[...] = jnp.zeros_like(dv_s)

        def blocks(heads, ks, carries, diag):
            n = range(len(heads))
            keep = (lambda t: jnp.where(strict, t, 0.0)) if diag else (lambda t: t)
            kk = [k_ref[ks, hd[3]] for hd in heads]
            z = [_mm_nt(heads[h][0], kk[h]) for h in n]
            da = [_mm_nt(heads[h][1], v_ref[ks, heads[h][3]]) for h in n]
            wl = [_neg_log2_sigmoid(z[h] * nscale2) for h in n]
            lr = [keep(wl[h][0] - wl[h][1]) for h in n]
            parts = [_split(lr[h]) for h in n]
            pin = [_mm(parts[h][0], upto) + _mm(parts[h][1], upto) for h in n]
            a = [keep(jnp.exp2((heads[h][2] - carries[h][1]) - pin[h] - wl[h][1])) for h in n]
            e = [a[h] * da[h] for h in n]
            parts = [_split(e[h]) for h in n]
            pex = [_mm(parts[h][0], before) + _mm(parts[h][1], before) for h in n]
            dz = [keep(e[h] - jnp.exp2(-wl[h][1]) * (e[h] + pex[h] + carries[h][2])).astype(BF16) for h in n]
            ab = [a[h].astype(BF16) for h in n]
            dq = [carries[h][0] + _mm(dz[h], kk[h]) for h in n]
            for h in n:
                dk_s[ks, heads[h][3]] += _mm_tn(dz[h], heads[h][0])
                dv_s[ks, heads[h][3]] += _mm_tn(ab[h], heads[h][1])
            return tuple((dq[h], carries[h][1] + pin[h][:, bq - 1:],
                          carries[h][2] + (pex[h][:, bq - 1:] + e[h][:, bq - 1:])) for h in n)

        def q_tile(i, _):
            qs = pl.ds(pl.multiple_of(i * bq, bq), bq)
            heads = []
            for pr in range(hp):
                cols = slice(pr * LANES, (pr + 1) * LANES)
                qv, dov, ltv = q_ref[qs, cols], do_ref[qs, cols], lt_ref[qs, cols]
                zq, zd = jnp.zeros_like(qv), jnp.zeros_like(dov)
                lt0 = jnp.max(jnp.where(head0, ltv, -jnp.inf), axis=1, keepdims=True)
                lt1 = jnp.max(jnp.where(head0, -jnp.inf, ltv), axis=1, keepdims=True)
                heads += [(jnp.where(head0, qv, zq), jnp.where(head0, dov, zd), lt0, cols),
                          (jnp.where(head0, zq, qv), jnp.where(head0, zd, dov), lt1, cols)]
            zero = (jnp.zeros((bq, LANES), F32), jnp.zeros((bq, 1), F32), jnp.zeros((bq, 1), F32))

            def left(t, cr):
                ks = pl.ds(pl.multiple_of(t * bq, bq), bq)
                return blocks(heads, ks, cr, False)

            res = lax.fori_loop(0, i, left, (zero,) * len(heads))
            res = blocks(heads, qs, res, True)
            for pr in range(hp):
                dq = jnp.where(head0, res[2 * pr][0], res[2 * pr + 1][0])
                dq_ref[qs, heads[2 * pr][3]] = (dq * scale).astype(BF16)
            return 0

        lax.fori_loop(0, nq, q_tile, 0)
        dk_ref[...] = (dk_s[...] * scale).astype(BF16)
        dv_ref[...] = dv_s[...].astype(BF16)

    def col(off):
        return pl.BlockSpec((None, S, hp * LANES), lambda b, p: (b, 0, off + p))

    n_pair //= hp
    shp = jax.ShapeDtypeStruct((B, S, W), BF16)
    return _call(
        body, name="sb_bwd", grid=(B, n_pair), comm=comm,
        in_specs=[col(0), col(n_pair), col(2 * n_pair), col(0), col(0)],
        out_specs=(col(0), col(0), col(0)),
        out_shape=(shp, shp, shp),
        scratch=[pltpu.VMEM((S, hp * LANES), F32), pltpu.VMEM((S, hp * LANES), F32)],
        args=(qkv, qkv, qkv, do, ltot))


def _pool_counts(S):
    t = lax.broadcasted_iota(jnp.int32, (S, 1), 0)
    return t, [jnp.minimum(t + 1, w).astype(F32) for w in POOL_WINDOWS]


def _pool_fwd(u, B, S):
    W = u.shape[2]

    def body(u_ref, out_ref):
        t, counts = _pool_counts(S)
        for gi, w in enumerate(POOL_WINDOWS):
            cols = slice(gi * POOL_GROUP_DIM, (gi + 1) * POOL_GROUP_DIM)
            ug = u_ref[:, cols]
            s, k = ug, 1
            while k < w:
                s = s + jnp.where(t >= k, pltpu.roll(s, k, axis=0), 0.0)
                k *= 2
            out_ref[:, cols] = (s / counts[gi] - ug).astype(BF16)

    spec = pl.BlockSpec((None, S, W), lambda b: (b, 0, 0))
    return pl.pallas_call(
        body, name="pool_fwd", grid=(B,), in_specs=[spec], out_specs=spec,
        out_shape=jax.ShapeDtypeStruct((B, S, W), BF16), compiler_params=_params("arbitrary"),
    )(u)


def _pool_bwd(dpooled, B, S):
    W = dpooled.shape[2]

    def body(d_ref, out_ref):
        t, counts = _pool_counts(S)
        for gi, w in enumerate(POOL_WINDOWS):
            cols = slice(gi * POOL_GROUP_DIM, (gi + 1) * POOL_GROUP_DIM)
            d = d_ref[:, cols]
            s, k = d / counts[gi], 1
            while k < w:
                s = s + jnp.where(t < S - k, pltpu.roll(s, S - k, axis=0), 0.0)
                k *= 2
            out_ref[:, cols] = (s - d).astype(BF16)

    spec = pl.BlockSpec((None, S, W), lambda b: (b, 0, 0))
    return pl.pallas_call(
        body, name="pool_bwd", grid=(B,), in_specs=[spec], out_specs=spec,
        out_shape=jax.ShapeDtypeStruct((B, S, W), BF16), compiler_params=_params("arbitrary"),
    )(dpooled)


def _mix_out_fwd(x, o_sb, pooled, w_pool, pool_scale, w_out):
    T, D = x.shape
    W = o_sb.shape[1]
    G = w_pool.shape[0]
    gd = POOL_GROUP_DIM
    tm = _tile(T, 512)

    def body(x_ref, osb_ref, pooled_ref, wp_ref, ps_ref, wo_ref, out_ref, mixed_ref):
        mixed_ref[:, :W] = osb_ref[...]
        for gi in range(G):
            cols = slice(gi * gd, (gi + 1) * gd)
            pw = _mm(pooled_ref[:, cols], wp_ref[gi])
            mixed_ref[:, W + gi * gd:W + (gi + 1) * gd] = (pw * ps_ref[:, cols]).astype(BF16)
        out_ref[...] = x_ref[...] + _mm(mixed_ref[...], wo_ref[...].reshape(D, D))

    row = lambda i: (i, 0)
    return pl.pallas_call(
        body, name="mix_out_fwd", grid=(T // tm,),
        in_specs=[pl.BlockSpec((tm, D), row), pl.BlockSpec((tm, W), row), pl.BlockSpec((tm, W), row),
                  pl.BlockSpec((G, gd, gd), lambda i: (0, 0, 0)), pl.BlockSpec((1, W), lambda i: (0, 0)),
                  pl.BlockSpec(w_out.shape, lambda i: (0, 0, 0))],
        out_specs=(pl.BlockSpec((tm, D), row), pl.BlockSpec((tm, D), row)),
        out_shape=(jax.ShapeDtypeStruct((T, D), F32), jax.ShapeDtypeStruct((T, D), BF16)),
        compiler_params=_params("arbitrary"),
    )(x, o_sb, pooled, w_pool, pool_scale, w_out)


def _mix_out_bwd(dx, pooled, w_pool, pool_scale, w_out):
    T, D = dx.shape
    W = pooled.shape[1]
    G = w_pool.shape[0]
    gd = POOL_GROUP_DIM
    tm = _tile(T, 512)

    def body(dx_ref, pooled_ref, wp_ref, ps_ref, wo_ref, dxb_ref, dosb_ref, dpooled_ref, dwp_ref, dps_ref):
        i = pl.program_id(0)

        @pl.when(i == 0)
        def _():
            dwp_ref[...] = jnp.zeros_like(dwp_ref)
            dps_ref[...] = jnp.zeros_like(dps_ref)

        dxb = dx_ref[...].astype(BF16)
        dxb_ref[...] = dxb
        dmixed = _mm_nt(dxb, wo_ref[...].reshape(D, D))
        dosb_ref[...] = dmixed[:, :W].astype(BF16)
        for gi in range(G):
            cols = slice(gi * gd, (gi + 1) * gd)
            pg = pooled_ref[:, cols]
            dop = dmixed[:, W + gi * gd:W + (gi + 1) * gd]
            pw = _mm(pg, wp_ref[gi])
            dps_ref[:, cols] += jnp.sum(dop * pw, axis=0, keepdims=True)
            dpw = (dop * ps_ref[:, cols]).astype(BF16)
            dwp_ref[gi] += _mm_tn(pg, dpw)
            dpooled_ref[:, cols] = _mm_nt(dpw, wp_ref[gi])

    row = lambda i: (i, 0)
    return pl.pallas_call(
        body, name="mix_out_bwd", grid=(T // tm,),
        in_specs=[pl.BlockSpec((tm, D), row), pl.BlockSpec((tm, W), row),
                  pl.BlockSpec((G, gd, gd), lambda i: (0, 0, 0)), pl.BlockSpec((1, W), lambda i: (0, 0)),
                  pl.BlockSpec(w_out.shape, lambda i: (0, 0, 0))],
        out_specs=(pl.BlockSpec((tm, D), row), pl.BlockSpec((tm, W), row), pl.BlockSpec((tm, W), row),
                   pl.BlockSpec((G, gd, gd), lambda i: (0, 0, 0)), pl.BlockSpec((1, W), lambda i: (0, 0))),
        out_shape=(jax.ShapeDtypeStruct((T, D), BF16), jax.ShapeDtypeStruct((T, W), BF16),
                   jax.ShapeDtypeStruct((T, W), F32), jax.ShapeDtypeStruct((G, gd, gd), F32),
                   jax.ShapeDtypeStruct((1, W), F32)),
        compiler_params=_params("arbitrary"),
    )(dx, pooled, w_pool, pool_scale, w_out)


def _mem_kv_fwd(mem, gain, w_kv):
    B, M, D = mem.shape
    cs = w_kv.shape[2]

    def body(mem_ref, gain_ref, w_ref, memn_ref, kv_ref):
        _, xhat = _rms(mem_ref[...])
        mn = (xhat * gain_ref[...]).astype(BF16)
        memn_ref[...] = mn
        for b in range(N_DEV):
            kv_ref[:, b * cs:(b + 1) * cs] = _mm(mn, w_ref[b]).astype(BF16)

    return pl.pallas_call(
        body, name="mem_kv_fwd", grid=(B,),
        in_specs=[pl.BlockSpec((None, M, D), lambda b: (b, 0, 0)), pl.BlockSpec((1, D), lambda b: (0, 0)),
                  pl.BlockSpec((N_DEV, D, cs), lambda b: (0, 0, 0))],
        out_specs=(pl.BlockSpec((M, D), lambda b: (b, 0)), pl.BlockSpec((None, M, N_DEV * cs), lambda b: (b, 0, 0))),
        out_shape=(jax.ShapeDtypeStruct((B * M, D), BF16), jax.ShapeDtypeStruct((B, M, N_DEV * cs), BF16)),
        compiler_params=_params("arbitrary"),
    )(mem, gain, w_kv)


def _mem_kv_bwd(dkv, mem, w_kv):
    B, M, D = mem.shape
    cs = w_kv.shape[2]

    def body(dkv_ref, mem_ref, w_ref, dkvb_ref, dgain_ref):
        b_id = pl.program_id(0)

        @pl.when(b_id == 0)
        def _():
            dgain_ref[...] = jnp.zeros_like(dgain_ref)

        dkvb = dkv_ref[...].astype(BF16)
        dkvb_ref[...] = dkvb
        dmn = jnp.zeros((M, D), F32)
        for b in range(N_DEV):
            dmn = dmn + _mm_nt(dkvb[:, b * cs:(b + 1) * cs], w_ref[b])
        _, xhat = _rms(mem_ref[...])
        dgain_ref[...] += jnp.sum(dmn * xhat, axis=0, keepdims=True)

    return pl.pallas_call(
        body, name="mem_kv_bwd", grid=(B,),
        in_specs=[pl.BlockSpec((None, M, N_DEV * cs), lambda b: (b, 0, 0)),
                  pl.BlockSpec((None, M, D), lambda b: (b, 0, 0)),
                  pl.BlockSpec((N_DEV, D, cs), lambda b: (0, 0, 0))],
        out_specs=(pl.BlockSpec((M, N_DEV * cs), lambda b: (b, 0)), pl.BlockSpec((1, D), lambda b: (0, 0))),
        out_shape=(jax.ShapeDtypeStruct((B * M, N_DEV * cs), BF16), jax.ShapeDtypeStruct((1, D), F32)),
        compiler_params=_params("arbitrary"),
    )(dkv, mem, w_kv)


def _softmax_rows(s):
    p = jnp.exp(s - jnp.max(s, axis=1, keepdims=True))
    return p / jnp.sum(p, axis=1, keepdims=True)


def _cross_fwd(x, gain, kv, w_q, w_o, B, S):
    T, D = x.shape
    M = kv.shape[1]
    hd = D // MEM_HEADS
    tm = _tile(S, 512)
    per = S // tm
    scale = hd ** -0.5

    def body(x_ref, gain_ref, kv_ref, wq_ref, wo_ref, out_ref, hq_ref, q_ref, ocat_ref):
        _, xhat = _rms(x_ref[...])
        hq = (xhat * gain_ref[...]).astype(BF16)
        hq_ref[...] = hq
        q = _mm(hq, wq_ref[...].reshape(D, D)).astype(BF16)
        q_ref[...] = q
        for h in range(MEM_HEADS):
            cols = slice(h * hd, (h + 1) * hd)
            s = _mm_nt(q[:, cols], kv_ref[:, cols]) * scale
            p = _softmax_rows(s).astype(BF16)
            ocat_ref[:, cols] = _mm(p, kv_ref[:, D + h * hd:D + (h + 1) * hd]).astype(BF16)
        out_ref[...] = x_ref[...] + _mm(ocat_ref[...], wo_ref[...].reshape(D, D))

    row = lambda b, t: (b * per + t, 0)
    wspec = pl.BlockSpec(w_q.shape, lambda b, t: (0, 0, 0))
    return pl.pallas_call(
        body, name="cross_fwd", grid=(B, per),
        in_specs=[pl.BlockSpec((tm, D), row), pl.BlockSpec((1, D), lambda b, t: (0, 0)),
                  pl.BlockSpec((None, M, 2 * D), lambda b, t: (b, 0, 0)), wspec, wspec],
        out_specs=tuple(pl.BlockSpec((tm, D), row) for _ in range(4)),
        out_shape=(jax.ShapeDtypeStruct((T, D), F32),) + tuple(jax.ShapeDtypeStruct((T, D), BF16) for _ in range(3)),
        compiler_params=_params("arbitrary", "arbitrary"),
    )(x, gain, kv, w_q, w_o)


def _cross_bwd(dy, x, gain, q, kv, w_q, w_o, B, S):
    T, D = x.shape
    M = kv.shape[1]
    hd = D // MEM_HEADS
    tm = _tile(S, 512)
    per = S // tm
    scale = hd ** -0.5

    def body(dy_ref, x_ref, gain_ref, q_ref, kv_ref, wq_ref, wo_ref,
             dx_ref, dyb_ref, dqb_ref, dkv_ref, dgain_ref):
        b_id, t_id = pl.program_id(0), pl.program_id(1)

        @pl.when((b_id == 0) & (t_id == 0))
        def _():
            dgain_ref[...] = jnp.zeros_like(dgain_ref)

        @pl.when(t_id == 0)
        def _():
            dkv_ref[...] = jnp.zeros_like(dkv_ref)

        dyb = dy_ref[...].astype(BF16)
        dyb_ref[...] = dyb
        docat = _mm_nt(dyb, wo_ref[...].reshape(D, D)).astype(BF16)
        for h in range(MEM_HEADS):
            cols = slice(h * hd, (h + 1) * hd)
            vcols = slice(D + h * hd, D + (h + 1) * hd)
            qh, kh, vh, doh = q_ref[:, cols], kv_ref[:, cols], kv_ref[:, vcols], docat[:, cols]
            p = _softmax_rows(_mm_nt(qh, kh) * scale)
            dp = _mm_nt(doh, vh)
            ds = (p * (dp - jnp.sum(dp * p, axis=1, keepdims=True)) * scale).astype(BF16)
            dqb_ref[:, cols] = _mm(ds, kh).astype(BF16)
            dkv_ref[:, cols] += _mm_tn(ds, qh)
            dkv_ref[:, vcols] += _mm_tn(p.astype(BF16), doh)
        dhq = _mm_nt(dqb_ref[...], wq_ref[...].reshape(D, D))
        r, xhat = _rms(x_ref[...])
        dgain_ref[...] += jnp.sum(dhq * xhat, axis=0, keepdims=True)
        dx_ref[...] = dy_ref[...] + _rms_bwd(dhq, gain_ref[...], r, xhat)

    row = lambda b, t: (b * per + t, 0)
    wspec = pl.BlockSpec(w_q.shape, lambda b, t: (0, 0, 0))
    one = pl.BlockSpec((1, D), lambda b, t: (0, 0))
    kvspec = pl.BlockSpec((None, M, 2 * D), lambda b, t: (b, 0, 0))
    return pl.pallas_call(
        body, name="cross_bwd", grid=(B, per),
        in_specs=[pl.BlockSpec((tm, D), row), pl.BlockSpec((tm, D), row), one, pl.BlockSpec((tm, D), row),
                  kvspec, wspec, wspec],
        out_specs=(pl.BlockSpec((tm, D), row), pl.BlockSpec((tm, D), row), pl.BlockSpec((tm, D), row), kvspec, one),
        out_shape=(jax.ShapeDtypeStruct((T, D), F32), jax.ShapeDtypeStruct((T, D), BF16),
                   jax.ShapeDtypeStruct((T, D), BF16), jax.ShapeDtypeStruct((B, M, 2 * D), F32),
                   jax.ShapeDtypeStruct((1, D), F32)),
        compiler_params=_params("arbitrary", "arbitrary"),
    )(dy, x, gain, q, kv, w_q, w_o)


def _final(x, gain, target):
    T, D = x.shape
    tm = _tile(T, 512)

    def body(x_ref, gain_ref, tgt_ref, dx_ref, dgain_ref, loss_ref):
        i = pl.program_id(0)

        @pl.when(i == 0)
        def _():
            dgain_ref[...] = jnp.zeros_like(dgain_ref)
            loss_ref[...] = jnp.zeros_like(loss_ref)

        r, xhat = _rms(x_ref[...])
        err = xhat * gain_ref[...] - tgt_ref[...]
        loss_ref[...] += 0.5 * jnp.sum(jnp.mean(err * err, axis=-1, keepdims=True), axis=0, keepdims=True)
        dy = err * (1.0 / D)
        dgain_ref[...] += jnp.sum(dy * xhat, axis=0, keepdims=True)
        dx_ref[...] = _rms_bwd(dy, gain_ref[...], r, xhat)

    row = lambda i: (i, 0)
    one = pl.BlockSpec((1, D), lambda i: (0, 0))
    return pl.pallas_call(
        body, name="final_loss", grid=(T // tm,),
        in_specs=[pl.BlockSpec((tm, D), row), one, pl.BlockSpec((tm, D), row)],
        out_specs=(pl.BlockSpec((tm, D), row), one, pl.BlockSpec((8, LANES), lambda i: (0, 0))),
        out_shape=(jax.ShapeDtypeStruct((T, D), F32), jax.ShapeDtypeStruct((1, D), F32),
                   jax.ShapeDtypeStruct((8, LANES), F32)),
        compiler_params=_params("arbitrary"),
    )(x, gain, target)


def _adamw(gparts, w, m, v, name):
    R, C = w.shape
    tr = _tile(R, 256)

    def body(gp_ref, w_ref, m_ref, v_ref, g_ref, d_ref, nm_ref, nv_ref):
        g = gp_ref[0].astype(F32)
        for s in range(1, N_DEV):
            g = g + gp_ref[s].astype(F32)
        nm = ADAM_B1 * m_ref[...] + (1.0 - ADAM_B1) * g
        nv = ADAM_B2 * v_ref[...] + (1.0 - ADAM_B2) * (g * g)
        m_hat = nm / (1.0 - ADAM_B1 ** ADAM_STEP)
        v_hat = nv / (1.0 - ADAM_B2 ** ADAM_STEP)
        g_ref[...] = g
        nm_ref[...] = nm
        nv_ref[...] = nv
        d_ref[...] = -ADAM_LR * (m_hat / (jnp.sqrt(v_hat) + ADAM_EPS) + ADAM_WD * w_ref[...])

    spec = pl.BlockSpec((tr, C), lambda i: (i, 0))
    shp = jax.ShapeDtypeStruct((R, C), F32)
    return pl.pallas_call(
        body, name=name, grid=(R // tr,),
        in_specs=[pl.BlockSpec((N_DEV, tr, C), lambda i: (0, i, 0)), spec, spec, spec],
        out_specs=(spec, spec, spec, spec), out_shape=(shp, shp, shp, shp),
        compiler_params=_params("arbitrary"),
    )(gparts, w, m, v)


def _rows128(a, rows):
    a = a.reshape(-1, LANES)
    return jnp.pad(a, ((0, rows - a.shape[0]), (0, 0)))


def kernel(x, mem, ffn1_norm, ffn1_w_gate, ffn1_w_up, ffn1_w_down, mix_norm, w_in, w_pool, pool_scale, w_out, mem_q_norm, mem_kv_norm, mem_w_q, mem_w_kv, mem_w_o, ffn2_norm, ffn2_w_gate, ffn2_w_up, ffn2_w_down, final_norm, loss_target, m_ffn1_norm, m_ffn1_w_gate, m_ffn1_w_up, m_ffn1_w_down, m_mix_norm, m_w_in, m_w_pool, m_pool_scale, m_w_out, m_mem_q_norm, m_mem_kv_norm, m_mem_w_q, m_mem_w_kv, m_mem_w_o, m_ffn2_norm, m_ffn2_w_gate, m_ffn2_w_up, m_ffn2_w_down, m_final_norm, v_ffn1_norm, v_ffn1_w_gate, v_ffn1_w_up, v_ffn1_w_down, v_mix_norm, v_w_in, v_w_pool, v_pool_scale, v_w_out, v_mem_q_norm, v_mem_kv_norm, v_mem_w_q, v_mem_w_kv, v_mem_w_o, v_ffn2_norm, v_ffn2_w_gate, v_ffn2_w_up, v_ffn2_w_down, v_final_norm):
    B, S, D = x.shape
    T = B * S
    x0 = x.reshape(T, D)
    target = loss_target.reshape(T, D)
    final_gain = final_norm.reshape(1, D)

    big = dict(
        g1=ffn1_w_gate[0].T, u1=ffn1_w_up[0].T, d1=ffn1_w_down[0],
        g2=ffn2_w_gate[0].T, u2=ffn2_w_up[0].T, d2=ffn2_w_down[0],
        w_in=w_in[0], w_out=w_out[0], w_q=mem_w_q[0], w_kv=mem_w_kv[0], w_o=mem_w_o[0])
    names = list(big)
    shard = {k: big[k].astype(BF16) for k in names}
    wp = w_pool[0].astype(BF16)
    full, ffn_w = {}, {}

    def gathered(keys, arrs):
        full.update(zip(keys, arrs))
        ffn_w.update({k: full[k].reshape(-1, D) for k in keys if k[0] in "gud"})

    first, mid, late = ("g1", "u1", "d1"), ("w_in", "w_out", "w_q", "w_kv", "w_o"), ("g2", "u2", "d2")
    gathered(first, _exchange([shard[k] for k in first], True, "gather_ffn1"))
    (x1, hn1, gg1, uu1), got = _ffn_fwd(x0, ffn1_norm, ffn_w["g1"], ffn_w["u1"], ffn_w["d1"], "ffn1_fwd",
                                        comm=([shard[k] for k in mid], True))
    gathered(mid, got)
    hn2, qkv, u = _mix_in_fwd(x1, mix_norm, full["w_in"])
    qkv3 = qkv.reshape(B, S, -1)
    (o_sb, ltot), got = _sb_fwd(qkv3, B, S, comm=([shard[k] for k in late], True))
    gathered(late, got)
    pooled = _pool_fwd(u.reshape(B, S, -1), B, S).reshape(T, -1)
    x2, mixed = _mix_out_fwd(x1, o_sb.reshape(T, -1), pooled, wp, pool_scale, full["w_out"])
    memn, kv = _mem_kv_fwd(mem, mem_kv_norm, full["w_kv"])
    x3, hq, q, ocat = _cross_fwd(x2, mem_q_norm, kv, full["w_q"], full["w_o"], B, S)
    (x4, hn4, gg2, uu2), _ = _ffn_fwd(x3, ffn2_norm, ffn_w["g2"], ffn_w["u2"], ffn_w["d2"], "ffn2_fwd")
    dx4, d_final, loss_part = _final(x4, final_gain, target)

    slab = lambda k: grads[k].reshape((N_DEV, -1) + grads[k].shape[-1:])
    got = {}
    (dx3, a2, dg2, du2, dyh2, d_ffn2), _ = _ffn_bwd(dx4, x3, ffn2_norm, gg2, uu2, ffn_w["g2"], ffn_w["u2"],
                                                   ffn_w["d2"], "ffn2_bwd")
    grads = dict(g2=_wgrad(dg2, hn4, "dw_gate2"), u2=_wgrad(du2, hn4, "dw_up2"), d2=_wgrad(a2, dyh2, "dw_down2"))
    dx2, dx3b, dqb, dkv, d_q = _cross_bwd(dx3, x2, mem_q_norm, q, kv, full["w_q"], full["w_o"], B, S)
    grads["w_o"] = _wgrad(ocat, dx3b, "dw_o")
    grads["w_q"] = _wgrad(hq, dqb, "dw_q")
    dkvb, d_kv = _mem_kv_bwd(dkv, mem, full["w_kv"])
    grads["w_kv"] = _wgrad(memn, dkvb, "dw_kv", col_slab=full["w_kv"].shape[2])
    dx2b, do_sb, dpooled, d_wpool, d_ps = _mix_out_bwd(dx2, pooled, wp, pool_scale, full["w_out"])
    grads["w_out"] = _wgrad(mixed, dx2b, "dw_out")
    du = _pool_bwd(dpooled.reshape(B, S, -1), B, S).reshape(T, -1)
    early = ("g2", "u2", "d2", "w_o", "w_q", "w_kv", "w_out")
    (dq, dk, dv), res = _sb_bwd(qkv3, do_sb.reshape(B, S, -1), ltot, B, S, comm=([slab(k) for k in early], False))
    got.update(zip(early, res))
    dx1, dproj, d_mix = _mix_in_bwd(dx2, dq.reshape(T, -1), dk.reshape(T, -1), dv.reshape(T, -1), du,
                                    x1, mix_norm, full["w_in"])
    grads["w_in"] = _wgrad(hn2, dproj, "dw_in", col_slab=full["w_in"].shape[2])
    (dx0, a1, dg1, du1, dyh1, d_ffn1), res = _ffn_bwd(dx1, x0, ffn1_norm, gg1, uu1, ffn_w["g1"], ffn_w["u1"],
                                                     ffn_w["d1"], "ffn1_bwd", comm=([slab("w_in")], False))
    got["w_in"] = res[0]

    small = [("ffn1_norm", d_ffn1, ffn1_norm, m_ffn1_norm, v_ffn1_norm),
             ("mix_norm", d_mix, mix_norm, m_mix_norm, v_mix_norm),
             ("w_pool", d_wpool, w_pool, m_w_pool, v_w_pool),
             ("pool_scale", d_ps, pool_scale, m_pool_scale, v_pool_scale),
             ("mem_q_norm", d_q, mem_q_norm, m_mem_q_norm, v_mem_q_norm),
             ("mem_kv_norm", d_kv, mem_kv_norm, m_mem_kv_norm, v_mem_kv_norm),
             ("ffn2_norm", d_ffn2, ffn2_norm, m_ffn2_norm, v_ffn2_norm),
             ("final_norm", d_final, final_norm, m_final_norm, v_final_norm)]
    rows = [max(8, t[2].size // LANES) for t in small]
    pack = lambda idx: jnp.concatenate([_rows128(t[idx], r) for t, r in zip(small, rows)]
                                       + ([loss_part] if idx == 1 else [jnp.zeros((8, LANES), F32)]))
    grads["g1"], (small_parts,) = _wgrad(dg1, hn1, "dw_gate1", comm=([pack(1)], True))
    grads["u1"], (got["g1"],) = _wgrad(du1, hn1, "dw_up1", comm=([slab("g1")], False))
    grads["d1"], (got["u1"],) = _wgrad(a1, dyh1, "dw_down1", comm=([slab("u1")], False))
    got["d1"] = _exchange([slab("d1")], False, "scatter_last")[0]

    state = dict(
        g1=(ffn1_w_gate, m_ffn1_w_gate, v_ffn1_w_gate), u1=(ffn1_w_up, m_ffn1_w_up, v_ffn1_w_up),
        d1=(ffn1_w_down, m_ffn1_w_down, v_ffn1_w_down), g2=(ffn2_w_gate, m_ffn2_w_gate, v_ffn2_w_gate),
        u2=(ffn2_w_up, m_ffn2_w_up, v_ffn2_w_up), d2=(ffn2_w_down, m_ffn2_w_down, v_ffn2_w_down),
        w_in=(w_in, m_w_in, v_w_in), w_out=(w_out, m_w_out, v_w_out), w_q=(mem_w_q, m_mem_w_q, v_mem_w_q),
        w_kv=(mem_w_kv, m_mem_w_kv, v_mem_w_kv), w_o=(mem_w_o, m_mem_w_o, v_mem_w_o))
    big_out = {}
    for k in names:
        parts = got[k]
        if k in ("g1", "u1", "g2", "u2"):
            parts = jnp.swapaxes(parts, 1, 2)
        w_, m_, v_ = (t[0] for t in state[k])
        big_out[k] = [t[None] for t in _adamw(parts, w_, m_, v_, "adamw_" + k)]

    small_res = _adamw(small_parts, pack(2), pack(3), pack(4), "adamw_small")
    small_out, off = {}, 0
    for t, r in zip(small, rows):
        n = t[2].size // LANES
        small_out[t[0]] = [res[off:off + n].reshape(t[2].shape) for res in small_res]
        off += r
    loss = small_res[0][off, 0]

    order = [("ffn1_norm", None), ("ffn1_w_gate", "g1"), ("ffn1_w_up", "u1"), ("ffn1_w_down", "d1"),
             ("mix_norm", None), ("w_in", "w_in"), ("w_pool", None), ("pool_scale", None), ("w_out", "w_out"),
             ("mem_q_norm", None), ("mem_kv_norm", None), ("mem_w_q", "w_q"), ("mem_w_kv", "w_kv"),
             ("mem_w_o", "w_o"), ("ffn2_norm", None), ("ffn2_w_gate", "g2"), ("ffn2_w_up", "u2"),
             ("ffn2_w_down", "d2"), ("final_norm", None)]
    res = [loss, dx0.reshape(B, S, D)]
    for which in range(4):
        for name, key in order:
            res.append(big_out[key][which] if key else small_out[name][which])
    return tuple(res)
```

```python
import functools

import jax
import jax.numpy as jnp
from jax import lax
from jax.experimental import pallas as pl
from jax.experimental.pallas import tpu as pltpu

F32 = jnp.float32
BF16 = jnp.bfloat16

N_DEV = 8
EPS = 1e-6
SB_HEAD_DIM = 64
LANES = 128
POOL_WINDOWS = (2, 4, 8, 16)
POOL_GROUP_DIM = 128
MEM_HEADS = 4
FFN_RESIDUAL_WEIGHT = 0.5
ADAM_LR = 0.001
ADAM_B1 = 0.9
ADAM_B2 = 0.999
ADAM_EPS = 1e-08
ADAM_WD = 0.01
ADAM_STEP = 10
VMEM_LIMIT = 56 * 1024 * 1024

MESH_ID = pl.DeviceIdType.MESH


def _params(*sem):
    return pltpu.CompilerParams(dimension_semantics=sem, vmem_limit_bytes=VMEM_LIMIT)


def _tile(n, pref):
    if n <= pref:
        return n
    t = pref - pref % 8
    while n % t:
        t -= 8
    return t


def _mm(a, b):
    return jnp.dot(a, b, preferred_element_type=F32)


def _mm_nt(a, b):
    return lax.dot_general(a, b, (((1,), (1,)), ((), ())), preferred_element_type=F32)


def _mm_tn(a, b):
    return lax.dot_general(a, b, (((0,), (0,)), ((), ())), preferred_element_type=F32)


def _rms(xv):
    r = lax.rsqrt(jnp.mean(xv * xv, axis=-1, keepdims=True) + EPS)
    return r, xv * r


def _rms_bwd(dhn, gain, r, xhat):
    dxh = dhn * gain
    return r * (dxh - xhat * jnp.mean(dxh * xhat, axis=-1, keepdims=True))


def _sigmoid(z):
    return 0.5 * jnp.tanh(0.5 * z) + 0.5


def _flags(arrs, gather):
    return [gather] * len(arrs) if isinstance(gather, bool) else list(gather)


def _comm_shapes(arrs, gather):
    return tuple(jax.ShapeDtypeStruct(((N_DEV,) + tuple(a.shape)) if f else tuple(a.shape), a.dtype)
                 for a, f in zip(arrs, _flags(arrs, gather)))


def _comm_start(ins, outs, sems, gather):
    send_sems, recv_sems, local_sems = sems
    gather = _flags(ins, gather)
    x, y, c = lax.axis_index("x"), lax.axis_index("y"), lax.axis_index("c")
    me = 4 * x + 2 * y + c
    for i in range(len(ins)):
        src = ins[i] if gather[i] else ins[i].at[me]
        pltpu.make_async_copy(src, outs[i].at[me], local_sems.at[i]).start()
    for k in range(1, N_DEV):
        px = 1 - x if k & 4 else x
        py = 1 - y if k & 2 else y
        pc = 1 - c if k & 1 else c
        peer = 4 * px + 2 * py + pc
        for i in range(len(ins)):
            src = ins[i] if gather[i] else ins[i].at[peer]
            pltpu.make_async_remote_copy(
                src_ref=src, dst_ref=outs[i].at[me],
                send_sem=send_sems.at[i], recv_sem=recv_sems.at[i],
                device_id=(px, py, pc), device_id_type=MESH_ID).start()


def _comm_wait(ins, outs, sems, gather):
    send_sems, recv_sems, local_sems = sems
    gather = _flags(ins, gather)
    x, y, c = lax.axis_index("x"), lax.axis_index("y"), lax.axis_index("c")
    me = 4 * x + 2 * y + c
    for i in range(len(ins)):
        seven = outs[i].at[pl.ds(0, N_DEV - 1)]
        done = pltpu.make_async_remote_copy(
            src_ref=seven, dst_ref=seven,
            send_sem=send_sems.at[i], recv_sem=recv_sems.at[i],
            device_id=(x, y, c), device_id_type=MESH_ID)
        done.wait_send()
        done.wait_recv()
        src = ins[i] if gather[i] else ins[i].at[me]
        pltpu.make_async_copy(src, outs[i].at[me], local_sems.at[i]).wait()


def _comm_sems(n):
    return [pltpu.SemaphoreType.DMA((n,)) for _ in range(3)]


def _exchange(arrs, gather, name):
    n = len(arrs)

    def body(*refs):
        ins, outs, sems = refs[:n], refs[n:2 * n], refs[2 * n:]
        _comm_start(ins, outs, sems, gather)
        _comm_wait(ins, outs, sems, gather)

    any_spec = pl.BlockSpec(memory_space=pl.ANY)
    outs = pl.pallas_call(
        body, name=name, out_shape=_comm_shapes(arrs, gather),
        in_specs=[any_spec] * n, out_specs=tuple([any_spec] * n), scratch_shapes=_comm_sems(n),
    )(*arrs)
    return list(outs)


def _gather_two_level(arrs, name):
    n = len(arrs)

    def body(*refs):
        ins, outs = refs[:n], refs[n:2 * n]
        send_sems, recv_sems, local_sems = refs[2 * n:]
        x, y, c = lax.axis_index("x"), lax.axis_index("y"), lax.axis_index("c")
        me, sibling = (x, y, c), (x, y, 1 - c)
        chips = [(1 - x, y), (x, 1 - y), (1 - x, 1 - y)]

        def copy(i, k, block, to, own=False):
            slab = outs[i].at[4 * block[0] + 2 * block[1] + block[2]]
            return pltpu.make_async_remote_copy(
                src_ref=ins[i] if own else slab, dst_ref=slab,
                send_sem=send_sems.at[i, k], recv_sem=recv_sems.at[i, k],
                device_id=to, device_id_type=MESH_ID)

        mine = [pltpu.make_async_copy(ins[i], outs[i].at[4 * x + 2 * y + c], local_sems.at[i]) for i in range(n)]
        first = [copy(i, 0, me, sibling, own=True) for i in range(n)]
        first += [copy(i, 1 + j, me, (*chip, c), own=True) for j, chip in enumerate(chips) for i in range(n)]
        for cp in mine + first:
            cp.start()
        passed = []
        for j, chip in enumerate(chips):
            for i in range(n):
                copy(i, 1 + j, (*chip, c), me).wait_recv()
                passed.append(copy(i, 4 + j, (*chip, c), sibling))
                passed[-1].start()
        for i in range(n):
            copy(i, 0, sibling, me).wait_recv()
            for j, chip in enumerate(chips):
                copy(i, 4 + j, (*chip, 1 - c), me).wait_recv()
        for cp in first + passed:
            cp.wait_send()
        for cp in mine:
            cp.wait()

    any_spec = pl.BlockSpec(memory_space=pl.ANY)
    outs = pl.pallas_call(
        body, name=name, out_shape=_comm_shapes(arrs, True),
        in_specs=[any_spec] * n, out_specs=tuple([any_spec] * n),
        scratch_shapes=[pltpu.SemaphoreType.DMA((n, N_DEV - 1)), pltpu.SemaphoreType.DMA((n, N_DEV - 1)),
                        pltpu.SemaphoreType.DMA((n,))],
    )(*arrs)
    return list(outs)


def _call(body, *, name, grid, in_specs, out_specs, out_shape, args, scratch=(), comm=None):
    sem = ("arbitrary",) * len(grid)
    if comm is None:
        res = pl.pallas_call(body, name=name, grid=grid, in_specs=list(in_specs), out_specs=tuple(out_specs),
                             out_shape=tuple(out_shape), scratch_shapes=list(scratch),
                             compiler_params=_params(*sem))(*args)
        return tuple(res), []
    arrs, gather = comm
    n, n_in, n_out, n_sc = len(arrs), len(args), len(out_shape), len(scratch)

    def wrapped(*refs):
        ins, cin = refs[:n_in], refs[n_in:n_in + n]
        outs, cout = refs[n_in + n:n_in + n + n_out], refs[n_in + n + n_out:n_in + 2 * n + n_out]
        sc, sems = refs[n_in + 2 * n + n_out:n_in + 2 * n + n_out + n_sc], refs[n_in + 2 * n + n_out + n_sc:]
        ids = [pl.program_id(a) for a in range(len(grid))]
        first = functools.reduce(jnp.logical_and, [i == 0 for i in ids])
        last = functools.reduce(jnp.logical_and, [i == g - 1 for i, g in zip(ids, grid)])

        @pl.when(first)
        def _():
            _comm_start(cin, cout, sems, gather)

        body(*ins, *outs, *sc)

        @pl.when(last)
        def _():
            _comm_wait(cin, cout, sems, gather)

    any_spec = pl.BlockSpec(memory_space=pl.ANY)
    res = pl.pallas_call(
        wrapped, name=name, grid=grid, in_specs=list(in_specs) + [any_spec] * n,
        out_specs=tuple(out_specs) + (any_spec,) * n, out_shape=tuple(out_shape) + _comm_shapes(arrs, gather),
        scratch_shapes=list(scratch) + _comm_sems(n), compiler_params=_params(*sem))(*args, *arrs)
    return tuple(res[:n_out]), list(res[n_out:])


def _ffn_fwd(x, gain, wgt, wut, wd, name, comm=None):
    T, D = x.shape
    F = wd.shape[0]
    tm, tf = _tile(T, 1024), _tile(F, 256)
    nj = F // tf
    rc = _tile(tm, 512)

    def body(x_ref, gain_ref, wg_ref, wu_ref, wd_ref, out_ref, hn_ref, g_ref, u_ref, acc, hn_s):
        j = pl.program_id(1)

        @pl.when(j == 0)
        def _():
            _, xhat = _rms(x_ref[...])
            hn = (xhat * gain_ref[...]).astype(BF16)
            hn_s[...] = hn
            hn_ref[...] = hn
            acc[...] = jnp.zeros_like(acc)

        for r0 in range(0, tm, rc):
            rows = slice(r0, r0 + rc)
            hn = hn_s[rows, :]
            g = _mm_nt(hn, wg_ref[...])
            u = _mm_nt(hn, wu_ref[...])
            g_ref[rows, :] = g.astype(BF16)
            u_ref[rows, :] = u.astype(BF16)
            a = (g * _sigmoid(g) * u).astype(BF16)
            acc[rows, :] += _mm(a, wd_ref[...])

        @pl.when(j == nj - 1)
        def _():
            out_ref[...] = x_ref[...] + FFN_RESIDUAL_WEIGHT * acc[...]

    row = lambda i, j: (i, 0)
    wspec = pl.BlockSpec((tf, D), lambda i, j: (j, 0))
    return _call(
        body, name=name, grid=(T // tm, nj), comm=comm,
        in_specs=[pl.BlockSpec((tm, D), row), pl.BlockSpec((1, D), lambda i, j: (0, 0)), wspec, wspec, wspec],
        out_specs=(pl.BlockSpec((tm, D), row), pl.BlockSpec((tm, D), row),
                   pl.BlockSpec((tm, tf), lambda i, j: (i, j)), pl.BlockSpec((tm, tf), lambda i, j: (i, j))),
        out_shape=(jax.ShapeDtypeStruct((T, D), F32), jax.ShapeDtypeStruct((T, D), BF16),
                   jax.ShapeDtypeStruct((T, F), BF16), jax.ShapeDtypeStruct((T, F), BF16)),
        scratch=[pltpu.VMEM((tm, D), F32), pltpu.VMEM((tm, D), BF16)],
        args=(x, gain, wgt, wut, wd))


def _ffn_bwd(dy, x, gain, g, u, wgt, wut, wd, name, comm=None):
    T, D = x.shape
    F = wd.shape[0]
    tm, tf = _tile(T, 1024), _tile(F, 256)
    nj = F // tf
    rc = _tile(tm, 512)

    def body(dy_ref, x_ref, gain_ref, g_ref, u_ref, wg_ref, wu_ref, wd_ref,
             dx_ref, a_ref, dg_ref, du_ref, dyh_ref, dgain_ref, acc, dyh_s):
        i, j = pl.program_id(0), pl.program_id(1)

        @pl.when(j == 0)
        def _():
            dyh = (FFN_RESIDUAL_WEIGHT * dy_ref[...]).astype(BF16)
            dyh_s[...] = dyh
            dyh_ref[...] = dyh
            acc[...] = jnp.zeros_like(acc)

        @pl.when((i == 0) & (j == 0))
        def _():
            dgain_ref[...] = jnp.zeros_like(dgain_ref)

        for r0 in range(0, tm, rc):
            rows = slice(r0, r0 + rc)
            gv = g_ref[rows, :].astype(F32)
            uv = u_ref[rows, :].astype(F32)
            da = _mm_nt(dyh_s[rows, :], wd_ref[...])
            sig = _sigmoid(gv)
            s = gv * sig
            a_ref[rows, :] = (s * uv).astype(BF16)
            dg = (da * uv * (sig + s * (1.0 - sig))).astype(BF16)
            du = (da * s).astype(BF16)
            dg_ref[rows, :] = dg
            du_ref[rows, :] = du
            acc[rows, :] += _mm(dg, wg_ref[...]) + _mm(du, wu_ref[...])

        @pl.when(j == nj - 1)
        def _():
            r, xhat = _rms(x_ref[...])
            dhn = acc[...]
            dgain_ref[...] += jnp.sum(dhn * xhat, axis=0, keepdims=True)
            dx_ref[...] = dy_ref[...] + _rms_bwd(dhn, gain_ref[...], r, xhat)

    row = lambda i, j: (i, 0)
    tile = lambda i, j: (i, j)
    wspec = pl.BlockSpec((tf, D), lambda i, j: (j, 0))
    one = pl.BlockSpec((1, D), lambda i, j: (0, 0))
    return _call(
        body, name=name, grid=(T // tm, nj), comm=comm,
        in_specs=[pl.BlockSpec((tm, D), row), pl.BlockSpec((tm, D), row), one,
                  pl.BlockSpec((tm, tf), tile), pl.BlockSpec((tm, tf), tile), wspec, wspec, wspec],
        out_specs=(pl.BlockSpec((tm, D), row), pl.BlockSpec((tm, tf), tile), pl.BlockSpec((tm, tf), tile),
                   pl.BlockSpec((tm, tf), tile), pl.BlockSpec((tm, D), row), one),
        out_shape=(jax.ShapeDtypeStruct((T, D), F32), jax.ShapeDtypeStruct((T, F), BF16),
                   jax.ShapeDtypeStruct((T, F), BF16), jax.ShapeDtypeStruct((T, F), BF16),
                   jax.ShapeDtypeStruct((T, D), BF16), jax.ShapeDtypeStruct((1, D), F32)),
        scratch=[pltpu.VMEM((tm, D), F32), pltpu.VMEM((tm, D), BF16)],
        args=(dy, x, gain, g, u, wgt, wut, wd))


def _wgrad(a, b, name, col_slab=None, comm=None):
    T, M = a.shape
    N = b.shape[1]
    tmm = M if M <= 1024 else _tile(M, 1408)
    tn = _tile(N, 1024)
    tk = _tile(T, 512)
    nk = T // tk
    per = tn // col_slab if col_slab else 0

    def body(a_ref, b_ref, out_ref, acc):
        k = pl.program_id(2)

        @pl.when(k == 0)
        def _():
            acc[...] = jnp.zeros_like(acc)

        acc[...] += _mm_tn(a_ref[...], b_ref[...])

        @pl.when(k == nk - 1)
        def _():
            if col_slab:
                for s in range(per):
                    out_ref[s] = acc[:, s * col_slab:(s + 1) * col_slab].astype(BF16)
            else:
                out_ref[...] = acc[...].astype(BF16)

    if col_slab:
        out_spec = pl.BlockSpec((per, tmm, col_slab), lambda m, n, k: (n, m, 0))
        out_shape = jax.ShapeDtypeStruct((N // col_slab, M, col_slab), BF16)
    else:
        out_spec = pl.BlockSpec((tmm, tn), lambda m, n, k: (m, n))
        out_shape = jax.ShapeDtypeStruct((M, N), BF16)
    (out,), got = _call(
        body, name=name, grid=(M // tmm, N // tn, nk), comm=comm,
        in_specs=[pl.BlockSpec((tk, tmm), lambda m, n, k: (k, m)), pl.BlockSpec((tk, tn), lambda m, n, k: (k, n))],
        out_specs=(out_spec,), out_shape=(out_shape,),
        scratch=[pltpu.VMEM((tmm, tn), F32)], args=(a, b))
    return (out, got) if comm else out


def _mix_in_fwd(x, gain, w_in):
    T, D = x.shape
    cs = w_in.shape[2]
    n_qkv = 3 * (N_DEV // 4)
    tm = _tile(T, 512)

    def body(x_ref, gain_ref, w_ref, hn_ref, qkv_ref, u_ref):
        _, xhat = _rms(x_ref[...])
        hn = (xhat * gain_ref[...]).astype(BF16)
        hn_ref[...] = hn
        for b in range(N_DEV):
            p = _mm(hn, w_ref[b])
            if b < n_qkv:
                qkv_ref[:, b * cs:(b + 1) * cs] = p.astype(BF16)
            else:
                u_ref[:, (b - n_qkv) * cs:(b - n_qkv + 1) * cs] = p

    row = lambda i: (i, 0)
    return pl.pallas_call(
        body, name="mix_in_fwd", grid=(T // tm,),
        in_specs=[pl.BlockSpec((tm, D), row), pl.BlockSpec((1, D), lambda i: (0, 0)),
                  pl.BlockSpec((N_DEV, D, cs), lambda i: (0, 0, 0))],
        out_specs=(pl.BlockSpec((tm, D), row), pl.BlockSpec((tm, n_qkv * cs), row),
                   pl.BlockSpec((tm, (N_DEV - n_qkv) * cs), row)),
        out_shape=(jax.ShapeDtypeStruct((T, D), BF16), jax.ShapeDtypeStruct((T, n_qkv * cs), BF16),
                   jax.ShapeDtypeStruct((T, (N_DEV - n_qkv) * cs), F32)),
        compiler_params=_params("arbitrary"),
    )(x, gain, w_in)


def _mix_in_bwd(dres, dq, dk, dv, du, x, gain, w_in):
    T, D = x.shape
    cs = w_in.shape[2]
    W = dq.shape[1]
    per = W // cs
    tm = _tile(T, 512)

    def body(dres_ref, dq_ref, dk_ref, dv_ref, du_ref, x_ref, gain_ref, w_ref, dx_ref, dproj_ref, dgain_ref):
        i = pl.program_id(0)

        @pl.when(i == 0)
        def _():
            dgain_ref[...] = jnp.zeros_like(dgain_ref)

        dhn = jnp.zeros((tm, D), F32)
        for part, ref in enumerate((dq_ref, dk_ref, dv_ref, du_ref)):
            for h in range(per):
                b = part * per + h
                d = ref[:, h * cs:(h + 1) * cs]
                dproj_ref[:, b * cs:(b + 1) * cs] = d
                dhn = dhn + _mm_nt(d, w_ref[b])
        r, xhat = _rms(x_ref[...])
        dgain_ref[...] += jnp.sum(dhn * xhat, axis=0, keepdims=True)
        dx_ref[...] = dres_ref[...] + _rms_bwd(dhn, gain_ref[...], r, xhat)

    row = lambda i: (i, 0)
    one = pl.BlockSpec((1, D), lambda i: (0, 0))
    part = pl.BlockSpec((tm, W), row)
    return pl.pallas_call(
        body, name="mix_in_bwd", grid=(T // tm,),
        in_specs=[pl.BlockSpec((tm, D), row), part, part, part, part, pl.BlockSpec((tm, D), row), one,
                  pl.BlockSpec((N_DEV, D, cs), lambda i: (0, 0, 0))],
        out_specs=(pl.BlockSpec((tm, D), row), pl.BlockSpec((tm, 4 * W), row), one),
        out_shape=(jax.ShapeDtypeStruct((T, D), F32), jax.ShapeDtypeStruct((T, 4 * W), BF16),
                   jax.ShapeDtypeStruct((1, D), F32)),
        compiler_params=_params("arbitrary"),
    )(dres, dq, dk, dv, du, x, gain, w_in)


SB_PAIRS_PER_PROGRAM = 2
LOG2_E = 1.4426950408889634
EXP2_CLAMP = 126.0


def _neg_log2_sigmoid(nz2):
    w = jnp.minimum(nz2, EXP2_CLAMP)
    return w, jnp.log2(1.0 + jnp.exp2(w))


def _split(v):
    hi = v.astype(BF16)
    return hi, (v - hi.astype(F32)).astype(BF16)


def _tri_sum(v, tri):
    hi, lo = _split(v)
    return _mm(hi, tri) + _mm(lo, tri)


def _sb_fwd(qkv, B, S, comm=None):
    W = qkv.shape[2] // 3
    n_pair = W // LANES
    bq = _tile(S, 256)
    nq = S // bq
    hp = SB_PAIRS_PER_PROGRAM
    nscale2 = -(SB_HEAD_DIM ** -0.5) * LOG2_E

    def body(q_ref, k_ref, v_ref, o_ref, lt_ref):
        lane = lax.broadcasted_iota(jnp.int32, (1, LANES), 1)
        head0 = lane < SB_HEAD_DIM
        rr = lax.broadcasted_iota(jnp.int32, (bq, bq), 0)
        cc = lax.broadcasted_iota(jnp.int32, (bq, bq), 1)
        strict = cc < rr
        after = jnp.where(rr > cc, 1.0, 0.0).astype(BF16)

        def blocks(heads, ks, carries, diag):
            n = range(len(heads))
            keep = (lambda t: jnp.where(strict, t, 0.0)) if diag else (lambda t: t)
            z = [_mm_nt(qh, k_ref[ks, cols]) for qh, cols in heads]
            wl = [_neg_log2_sigmoid(z[h] * nscale2) for h in n]
            lr = [keep(wl[h][0] - wl[h][1]) for h in n]
            parts = [_split(lr[h]) for h in n]
            suf = [_mm(parts[h][0], after) + _mm(parts[h][1], after) for h in n]
            a = [keep(jnp.exp2(suf[h] + carries[h][1] - wl[h][1])).astype(BF16) for h in n]
            o = [carries[h][0] + _mm(a[h], v_ref[ks, heads[h][1]]) for h in n]
            return tuple((o[h], carries[h][1] + (suf[h][:, :1] + lr[h][:, :1])) for h in n)

        def q_tile(i, _):
            qs = pl.ds(pl.multiple_of(i * bq, bq), bq)
            heads = []
            for pr in range(hp):
                cols = slice(pr * LANES, (pr + 1) * LANES)
                qv = q_ref[qs, cols]
                heads += [(jnp.where(head0, qv, jnp.zeros_like(qv)), cols),
                          (jnp.where(head0, jnp.zeros_like(qv), qv), cols)]
            zero = (jnp.zeros((bq, LANES), F32), jnp.zeros((bq, 1), F32))
            init = blocks(heads, qs, (zero,) * len(heads), True)

            def left(t, cr):
                ks = pl.ds(pl.multiple_of((i - 1 - t) * bq, bq), bq)
                return blocks(heads, ks, cr, False)

            res = lax.fori_loop(0, i, left, init)
            for pr in range(hp):
                (o0, l0), (o1, l1) = res[2 * pr], res[2 * pr + 1]
                o_ref[qs, heads[2 * pr][1]] = jnp.where(head0, o0, o1).astype(BF16)
                lt_ref[qs, heads[2 * pr][1]] = jnp.where(head0, l0, l1)
            return 0

        lax.fori_loop(0, nq, q_tile, 0)

    def col(off):
        return pl.BlockSpec((None, S, hp * LANES), lambda b, p: (b, 0, off + p))

    n_pair //= hp
    return _call(
        body, name="sb_fwd", grid=(B, n_pair), comm=comm,
        in_specs=[col(0), col(n_pair), col(2 * n_pair)],
        out_specs=(col(0), col(0)),
        out_shape=(jax.ShapeDtypeStruct((B, S, W), BF16), jax.ShapeDtypeStruct((B, S, W), F32)),
        args=(qkv, qkv, qkv))


def _sb_bwd(qkv, do, ltot, B, S, comm=None):
    W = qkv.shape[2] // 3
    n_pair = W // LANES
    bq = _tile(S, 256)
    nq = S // bq
    hp = SB_PAIRS_PER_PROGRAM
    scale = SB_HEAD_DIM ** -0.5
    nscale2 = -scale * LOG2_E

    def body(q_ref, k_ref, v_ref, do_ref, lt_ref, dq_ref, dk_ref, dv_ref, dk_s, dv_s):
        lane = lax.broadcasted_iota(jnp.int32, (1, LANES), 1)
        head0 = lane < SB_HEAD_DIM
        rr = lax.broadcasted_iota(jnp.int32, (bq, bq), 0)
        cc = lax.broadcasted_iota(jnp.int32, (bq, bq), 1)
        strict = cc < rr
        upto = jnp.where(rr <= cc, 1.0, 0.0).astype(BF16)
        before = jnp.where(rr < cc, 1.0, 0.0).astype(BF16)
        dk_s[...] = jnp.zeros_like(dk_s)
        dv_s[...] = jnp.zeros_like(dv_s)

        def blocks(heads, ks, carries, diag):
            n = range(len(heads))
            keep = (lambda t: jnp.where(strict, t, 0.0)) if diag else (lambda t: t)
            kk = [k_ref[ks, hd[3]] for hd in heads]
            z = [_mm_nt(heads[h][0], kk[h]) for h in n]
            da = [_mm_nt(heads[h][1], v_ref[ks, heads[h][3]]) for h in n]
            wl = [_neg_log2_sigmoid(z[h] * nscale2) for h in n]
            lr = [keep(wl[h][0] - wl[h][1]) for h in n]
            parts = [_split(lr[h]) for h in n]
            pin = [_mm(parts[h][0], upto) + _mm(parts[h][1], upto) for h in n]
            a = [keep(jnp.exp2((heads[h][2] - carries[h][1]) - pin[h] - wl[h][1])) for h in n]
            e = [a[h] * da[h] for h in n]
            parts = [_split(e[h]) for h in n]
            pex = [_mm(parts[h][0], before) + _mm(parts[h][1], before) for h in n]
            dz = [keep(e[h] - jnp.exp2(-wl[h][1]) * (e[h] + pex[h] + carries[h][2])).astype(BF16) for h in n]
            ab = [a[h].astype(BF16) for h in n]
            dq = [carries[h][0] + _mm(dz[h], kk[h]) for h in n]
            for h in n:
                dk_s[ks, heads[h][3]] += _mm_tn(dz[h], heads[h][0])
                dv_s[ks, heads[h][3]] += _mm_tn(ab[h], heads[h][1])
            return tuple((dq[h], carries[h][1] + pin[h][:, bq - 1:],
                          carries[h][2] + (pex[h][:, bq - 1:] + e[h][:, bq - 1:])) for h in n)

        def q_tile(i, _):
            qs = pl.ds(pl.multiple_of(i * bq, bq), bq)
            heads = []
            for pr in range(hp):
                cols = slice(pr * LANES, (pr + 1) * LANES)
                qv, dov, ltv = q_ref[qs, cols], do_ref[qs, cols], lt_ref[qs, cols]
                zq, zd = jnp.zeros_like(qv), jnp.zeros_like(dov)
                lt0 = jnp.max(jnp.where(head0, ltv, -jnp.inf), axis=1, keepdims=True)
                lt1 = jnp.max(jnp.where(head0, -jnp.inf, ltv), axis=1, keepdims=True)
                heads += [(jnp.where(head0, qv, zq), jnp.where(head0, dov, zd), lt0, cols),
                          (jnp.where(head0, zq, qv), jnp.where(head0, zd, dov), lt1, cols)]
            zero = (jnp.zeros((bq, LANES), F32), jnp.zeros((bq, 1), F32), jnp.zeros((bq, 1), F32))

            def left(t, cr):
                ks = pl.ds(pl.multiple_of(t * bq, bq), bq)
                return blocks(heads, ks, cr, False)

            res = lax.fori_loop(0, i, left, (zero,) * len(heads))
            res = blocks(heads, qs, res, True)
            for pr in range(hp):
                dq = jnp.where(head0, res[2 * pr][0], res[2 * pr + 1][0])
                dq_ref[qs, heads[2 * pr][3]] = (dq * scale).astype(BF16)
            return 0

        lax.fori_loop(0, nq, q_tile, 0)
        dk_ref[...] = (dk_s[...] * scale).astype(BF16)
        dv_ref[...] = dv_s[...].astype(BF16)

    def col(off):
        return pl.BlockSpec((None, S, hp * LANES), lambda b, p: (b, 0, off + p))

    n_pair //= hp
    shp = jax.ShapeDtypeStruct((B, S, W), BF16)
    return _call(
        body, name="sb_bwd", grid=(B, n_pair), comm=comm,
        in_specs=[col(0), col(n_pair), col(2 * n_pair), col(0), col(0)],
        out_specs=(col(0), col(0), col(0)),
        out_shape=(shp, shp, shp),
        scratch=[pltpu.VMEM((S, hp * LANES), F32), pltpu.VMEM((S, hp * LANES), F32)],
        args=(qkv, qkv, qkv, do, ltot))


def _pool_counts(S):
    t = lax.broadcasted_iota(jnp.int32, (S, 1), 0)
    return t, [jnp.minimum(t + 1, w).astype(F32) for w in POOL_WINDOWS]


def _pool_fwd(u, B, S):
    W = u.shape[2]

    def body(u_ref, out_ref):
        t, counts = _pool_counts(S)
        for gi, w in enumerate(POOL_WINDOWS):
            cols = slice(gi * POOL_GROUP_DIM, (gi + 1) * POOL_GROUP_DIM)
            ug = u_ref[:, cols]
            s, k = ug, 1
            while k < w:
                s = s + jnp.where(t >= k, pltpu.roll(s, k, axis=0), 0.0)
                k *= 2
            out_ref[:, cols] = (s / counts[gi] - ug).astype(BF16)

    spec = pl.BlockSpec((None, S, W), lambda b: (b, 0, 0))
    return pl.pallas_call(
        body, name="pool_fwd", grid=(B,), in_specs=[spec], out_specs=spec,
        out_shape=jax.ShapeDtypeStruct((B, S, W), BF16), compiler_params=_params("arbitrary"),
    )(u)


def _pool_bwd(dpooled, B, S):
    W = dpooled.shape[2]

    def body(d_ref, out_ref):
        t, counts = _pool_counts(S)
        for gi, w in enumerate(POOL_WINDOWS):
            cols = slice(gi * POOL_GROUP_DIM, (gi + 1) * POOL_GROUP_DIM)
            d = d_ref[:, cols]
            s, k = d / counts[gi], 1
            while k < w:
                s = s + jnp.where(t < S - k, pltpu.roll(s, S - k, axis=0), 0.0)
                k *= 2
            out_ref[:, cols] = (s - d).astype(BF16)

    spec = pl.BlockSpec((None, S, W), lambda b: (b, 0, 0))
    return pl.pallas_call(
        body, name="pool_bwd", grid=(B,), in_specs=[spec], out_specs=spec,
        out_shape=jax.ShapeDtypeStruct((B, S, W), BF16), compiler_params=_params("arbitrary"),
    )(dpooled)


def _mix_out_fwd(x, o_sb, pooled, w_pool, pool_scale, w_out):
    T, D = x.shape
    W = o_sb.shape[1]
    G = w_pool.shape[0]
    gd = POOL_GROUP_DIM
    tm = _tile(T, 512)

    def body(x_ref, osb_ref, pooled_ref, wp_ref, ps_ref, wo_ref, out_ref, mixed_ref):
        mixed_ref[:, :W] = osb_ref[...]
        for gi in range(G):
            cols = slice(gi * gd, (gi + 1) * gd)
            pw = _mm(pooled_ref[:, cols], wp_ref[gi])
            mixed_ref[:, W + gi * gd:W + (gi + 1) * gd] = (pw * ps_ref[:, cols]).astype(BF16)
        out_ref[...] = x_ref[...] + _mm(mixed_ref[...], wo_ref[...].reshape(D, D))

    row = lambda i: (i, 0)
    return pl.pallas_call(
        body, name="mix_out_fwd", grid=(T // tm,),
        in_specs=[pl.BlockSpec((tm, D), row), pl.BlockSpec((tm, W), row), pl.BlockSpec((tm, W), row),
                  pl.BlockSpec((G, gd, gd), lambda i: (0, 0, 0)), pl.BlockSpec((1, W), lambda i: (0, 0)),
                  pl.BlockSpec(w_out.shape, lambda i: (0, 0, 0))],
        out_specs=(pl.BlockSpec((tm, D), row), pl.BlockSpec((tm, D), row)),
        out_shape=(jax.ShapeDtypeStruct((T, D), F32), jax.ShapeDtypeStruct((T, D), BF16)),
        compiler_params=_params("arbitrary"),
    )(x, o_sb, pooled, w_pool, pool_scale, w_out)


def _mix_out_bwd(dx, pooled, w_pool, pool_scale, w_out):
    T, D = dx.shape
    W = pooled.shape[1]
    G = w_pool.shape[0]
    gd = POOL_GROUP_DIM
    tm = _tile(T, 512)

    def body(dx_ref, pooled_ref, wp_ref, ps_ref, wo_ref, dxb_ref, dosb_ref, dpooled_ref, dwp_ref, dps_ref):
        i = pl.program_id(0)

        @pl.when(i == 0)
        def _():
            dwp_ref[...] = jnp.zeros_like(dwp_ref)
            dps_ref[...] = jnp.zeros_like(dps_ref)

        dxb = dx_ref[...].astype(BF16)
        dxb_ref[...] = dxb
        dmixed = _mm_nt(dxb, wo_ref[...].reshape(D, D))
        dosb_ref[...] = dmixed[:, :W].astype(BF16)
        for gi in range(G):
            cols = slice(gi * gd, (gi + 1) * gd)
            pg = pooled_ref[:, cols]
            dop = dmixed[:, W + gi * gd:W + (gi + 1) * gd]
            pw = _mm(pg, wp_ref[gi])
            dps_ref[:, cols] += jnp.sum(dop * pw, axis=0, keepdims=True)
            dpw = (dop * ps_ref[:, cols]).astype(BF16)
            dwp_ref[gi] += _mm_tn(pg, dpw)
            dpooled_ref[:, cols] = _mm_nt(dpw, wp_ref[gi])

    row = lambda i: (i, 0)
    return pl.pallas_call(
        body, name="mix_out_bwd", grid=(T // tm,),
        in_specs=[pl.BlockSpec((tm, D), row), pl.BlockSpec((tm, W), row),
                  pl.BlockSpec((G, gd, gd), lambda i: (0, 0, 0)), pl.BlockSpec((1, W), lambda i: (0, 0)),
                  pl.BlockSpec(w_out.shape, lambda i: (0, 0, 0))],
        out_specs=(pl.BlockSpec((tm, D), row), pl.BlockSpec((tm, W), row), pl.BlockSpec((tm, W), row),
                   pl.BlockSpec((G, gd, gd), lambda i: (0, 0, 0)), pl.BlockSpec((1, W), lambda i: (0, 0))),
        out_shape=(jax.ShapeDtypeStruct((T, D), BF16), jax.ShapeDtypeStruct((T, W), BF16),
                   jax.ShapeDtypeStruct((T, W), F32), jax.ShapeDtypeStruct((G, gd, gd), F32),
                   jax.ShapeDtypeStruct((1, W), F32)),
        compiler_params=_params("arbitrary"),
    )(dx, pooled, w_pool, pool_scale, w_out)


def _mem_kv_fwd(mem, gain, w_kv):
    B, M, D = mem.shape
    cs = w_kv.shape[2]

    def body(mem_ref, gain_ref, w_ref, memn_ref, kv_ref):
        _, xhat = _rms(mem_ref[...])
        mn = (xhat * gain_ref[...]).astype(BF16)
        memn_ref[...] = mn
        for b in range(N_DEV):
            kv_ref[:, b * cs:(b + 1) * cs] = _mm(mn, w_ref[b]).astype(BF16)

    return pl.pallas_call(
        body, name="mem_kv_fwd", grid=(B,),
        in_specs=[pl.BlockSpec((None, M, D), lambda b: (b, 0, 0)), pl.BlockSpec((1, D), lambda b: (0, 0)),
                  pl.BlockSpec((N_DEV, D, cs), lambda b: (0, 0, 0))],
        out_specs=(pl.BlockSpec((M, D), lambda b: (b, 0)), pl.BlockSpec((None, M, N_DEV * cs), lambda b: (b, 0, 0))),
        out_shape=(jax.ShapeDtypeStruct((B * M, D), BF16), jax.ShapeDtypeStruct((B, M, N_DEV * cs), BF16)),
        compiler_params=_params("arbitrary"),
    )(mem, gain, w_kv)


def _mem_kv_bwd(dkv, mem, w_kv):
    B, M, D = mem.shape
    cs = w_kv.shape[2]

    def body(dkv_ref, mem_ref, w_ref, dkvb_ref, dgain_ref):
        b_id = pl.program_id(0)

        @pl.when(b_id == 0)
        def _():
            dgain_ref[...] = jnp.zeros_like(dgain_ref)

        dkvb = dkv_ref[...].astype(BF16)
        dkvb_ref[...] = dkvb
        dmn = jnp.zeros((M, D), F32)
        for b in range(N_DEV):
            dmn = dmn + _mm_nt(dkvb[:, b * cs:(b + 1) * cs], w_ref[b])
        _, xhat = _rms(mem_ref[...])
        dgain_ref[...] += jnp.sum(dmn * xhat, axis=0, keepdims=True)

    return pl.pallas_call(
        body, name="mem_kv_bwd", grid=(B,),
        in_specs=[pl.BlockSpec((None, M, N_DEV * cs), lambda b: (b, 0, 0)),
                  pl.BlockSpec((None, M, D), lambda b: (b, 0, 0)),
                  pl.BlockSpec((N_DEV, D, cs), lambda b: (0, 0, 0))],
        out_specs=(pl.BlockSpec((M, N_DEV * cs), lambda b: (b, 0)), pl.BlockSpec((1, D), lambda b: (0, 0))),
        out_shape=(jax.ShapeDtypeStruct((B * M, N_DEV * cs), BF16), jax.ShapeDtypeStruct((1, D), F32)),
        compiler_params=_params("arbitrary"),
    )(dkv, mem, w_kv)


def _softmax_rows(s):
    p = jnp.exp(s - jnp.max(s, axis=1, keepdims=True))
    return p / jnp.sum(p, axis=1, keepdims=True)


def _cross_fwd(x, gain, kv, w_q, w_o, B, S):
    T, D = x.shape
    M = kv.shape[1]
    hd = D // MEM_HEADS
    tm = _tile(S, 512)
    per = S // tm
    scale = hd ** -0.5

    def body(x_ref, gain_ref, kv_ref, wq_ref, wo_ref, out_ref, hq_ref, q_ref, ocat_ref):
        _, xhat = _rms(x_ref[...])
        hq = (xhat * gain_ref[...]).astype(BF16)
        hq_ref[...] = hq
        q = _mm(hq, wq_ref[...].reshape(D, D)).astype(BF16)
        q_ref[...] = q
        for h in range(MEM_HEADS):
            cols = slice(h * hd, (h + 1) * hd)
            s = _mm_nt(q[:, cols], kv_ref[:, cols]) * scale
            p = _softmax_rows(s).astype(BF16)
            ocat_ref[:, cols] = _mm(p, kv_ref[:, D + h * hd:D + (h + 1) * hd]).astype(BF16)
        out_ref[...] = x_ref[...] + _mm(ocat_ref[...], wo_ref[...].reshape(D, D))

    row = lambda b, t: (b * per + t, 0)
    wspec = pl.BlockSpec(w_q.shape, lambda b, t: (0, 0, 0))
    return pl.pallas_call(
        body, name="cross_fwd", grid=(B, per),
        in_specs=[pl.BlockSpec((tm, D), row), pl.BlockSpec((1, D), lambda b, t: (0, 0)),
                  pl.BlockSpec((None, M, 2 * D), lambda b, t: (b, 0, 0)), wspec, wspec],
        out_specs=tuple(pl.BlockSpec((tm, D), row) for _ in range(4)),
        out_shape=(jax.ShapeDtypeStruct((T, D), F32),) + tuple(jax.ShapeDtypeStruct((T, D), BF16) for _ in range(3)),
        compiler_params=_params("arbitrary", "arbitrary"),
    )(x, gain, kv, w_q, w_o)


def _cross_bwd(dy, x, gain, q, kv, w_q, w_o, B, S):
    T, D = x.shape
    M = kv.shape[1]
    hd = D // MEM_HEADS
    tm = _tile(S, 512)
    per = S // tm
    scale = hd ** -0.5

    def body(dy_ref, x_ref, gain_ref, q_ref, kv_ref, wq_ref, wo_ref,
             dx_ref, dyb_ref, dqb_ref, dkv_ref, dgain_ref):
        b_id, t_id = pl.program_id(0), pl.program_id(1)

        @pl.when((b_id == 0) & (t_id == 0))
        def _():
            dgain_ref[...] = jnp.zeros_like(dgain_ref)

        @pl.when(t_id == 0)
        def _():
            dkv_ref[...] = jnp.zeros_like(dkv_ref)

        dyb = dy_ref[...].astype(BF16)
        dyb_ref[...] = dyb
        docat = _mm_nt(dyb, wo_ref[...].reshape(D, D)).astype(BF16)
        for h in range(MEM_HEADS):
            cols = slice(h * hd, (h + 1) * hd)
            vcols = slice(D + h * hd, D + (h + 1) * hd)
            qh, kh, vh, doh = q_ref[:, cols], kv_ref[:, cols], kv_ref[:, vcols], docat[:, cols]
            p = _softmax_rows(_mm_nt(qh, kh) * scale)
            dp = _mm_nt(doh, vh)
            ds = (p * (dp - jnp.sum(dp * p, axis=1, keepdims=True)) * scale).astype(BF16)
            dqb_ref[:, cols] = _mm(ds, kh).astype(BF16)
            dkv_ref[:, cols] += _mm_tn(ds, qh)
            dkv_ref[:, vcols] += _mm_tn(p.astype(BF16), doh)
        dhq = _mm_nt(dqb_ref[...], wq_ref[...].reshape(D, D))
        r, xhat = _rms(x_ref[...])
        dgain_ref[...] += jnp.sum(dhq * xhat, axis=0, keepdims=True)
        dx_ref[...] = dy_ref[...] + _rms_bwd(dhq, gain_ref[...], r, xhat)

    row = lambda b, t: (b * per + t, 0)
    wspec = pl.BlockSpec(w_q.shape, lambda b, t: (0, 0, 0))
    one = pl.BlockSpec((1, D), lambda b, t: (0, 0))
    kvspec = pl.BlockSpec((None, M, 2 * D), lambda b, t: (b, 0, 0))
    return pl.pallas_call(
        body, name="cross_bwd", grid=(B, per),
        in_specs=[pl.BlockSpec((tm, D), row), pl.BlockSpec((tm, D), row), one, pl.BlockSpec((tm, D), row),
                  kvspec, wspec, wspec],
        out_specs=(pl.BlockSpec((tm, D), row), pl.BlockSpec((tm, D), row), pl.BlockSpec((tm, D), row), kvspec, one),
        out_shape=(jax.ShapeDtypeStruct((T, D), F32), jax.ShapeDtypeStruct((T, D), BF16),
                   jax.ShapeDtypeStruct((T, D), BF16), jax.ShapeDtypeStruct((B, M, 2 * D), F32),
                   jax.ShapeDtypeStruct((1, D), F32)),
        compiler_params=_params("arbitrary", "arbitrary"),
    )(dy, x, gain, q, kv, w_q, w_o)


def _final(x, gain, target):
    T, D = x.shape
    tm = _tile(T, 512)

    def body(x_ref, gain_ref, tgt_ref, dx_ref, dgain_ref, loss_ref):
        i = pl.program_id(0)

        @pl.when(i == 0)
        def _():
            dgain_ref[...] = jnp.zeros_like(dgain_ref)
            loss_ref[...] = jnp.zeros_like(loss_ref)

        r, xhat = _rms(x_ref[...])
        err = xhat * gain_ref[...] - tgt_ref[...]
        loss_ref[...] += 0.5 * jnp.sum(jnp.mean(err * err, axis=-1, keepdims=True), axis=0, keepdims=True)
        dy = err * (1.0 / D)
        dgain_ref[...] += jnp.sum(dy * xhat, axis=0, keepdims=True)
        dx_ref[...] = _rms_bwd(dy, gain_ref[...], r, xhat)

    row = lambda i: (i, 0)
    one = pl.BlockSpec((1, D), lambda i: (0, 0))
    return pl.pallas_call(
        body, name="final_loss", grid=(T // tm,),
        in_specs=[pl.BlockSpec((tm, D), row), one, pl.BlockSpec((tm, D), row)],
        out_specs=(pl.BlockSpec((tm, D), row), one, pl.BlockSpec((8, LANES), lambda i: (0, 0))),
        out_shape=(jax.ShapeDtypeStruct((T, D), F32), jax.ShapeDtypeStruct((1, D), F32),
                   jax.ShapeDtypeStruct((8, LANES), F32)),
        compiler_params=_params("arbitrary"),
    )(x, gain, target)


def _adamw(gparts, w, m, v, name):
    R, C = w.shape
    tr = _tile(R, 256)

    def body(gp_ref, w_ref, m_ref, v_ref, g_ref, d_ref, nm_ref, nv_ref):
        g = gp_ref[0].astype(F32)
        for s in range(1, N_DEV):
            g = g + gp_ref[s].astype(F32)
        nm = ADAM_B1 * m_ref[...] + (1.0 - ADAM_B1) * g
        nv = ADAM_B2 * v_ref[...] + (1.0 - ADAM_B2) * (g * g)
        m_hat = nm / (1.0 - ADAM_B1 ** ADAM_STEP)
        v_hat = nv / (1.0 - ADAM_B2 ** ADAM_STEP)
        g_ref[...] = g
        nm_ref[...] = nm
        nv_ref[...] = nv
        d_ref[...] = -ADAM_LR * (m_hat / (jnp.sqrt(v_hat) + ADAM_EPS) + ADAM_WD * w_ref[...])

    spec = pl.BlockSpec((tr, C), lambda i: (i, 0))
    shp = jax.ShapeDtypeStruct((R, C), F32)
    return pl.pallas_call(
        body, name=name, grid=(R // tr,),
        in_specs=[pl.BlockSpec((N_DEV, tr, C), lambda i: (0, i, 0)), spec, spec, spec],
        out_specs=(spec, spec, spec, spec), out_shape=(shp, shp, shp, shp),
        compiler_params=_params("arbitrary"),
    )(gparts, w, m, v)


def _rows128(a, rows):
    a = a.reshape(-1, LANES)
    return jnp.pad(a, ((0, rows - a.shape[0]), (0, 0)))


def kernel(x, mem, ffn1_norm, ffn1_w_gate, ffn1_w_up, ffn1_w_down, mix_norm, w_in, w_pool, pool_scale, w_out, mem_q_norm, mem_kv_norm, mem_w_q, mem_w_kv, mem_w_o, ffn2_norm, ffn2_w_gate, ffn2_w_up, ffn2_w_down, final_norm, loss_target, m_ffn1_norm, m_ffn1_w_gate, m_ffn1_w_up, m_ffn1_w_down, m_mix_norm, m_w_in, m_w_pool, m_pool_scale, m_w_out, m_mem_q_norm, m_mem_kv_norm, m_mem_w_q, m_mem_w_kv, m_mem_w_o, m_ffn2_norm, m_ffn2_w_gate, m_ffn2_w_up, m_ffn2_w_down, m_final_norm, v_ffn1_norm, v_ffn1_w_gate, v_ffn1_w_up, v_ffn1_w_down, v_mix_norm, v_w_in, v_w_pool, v_pool_scale, v_w_out, v_mem_q_norm, v_mem_kv_norm, v_mem_w_q, v_mem_w_kv, v_mem_w_o, v_ffn2_norm, v_ffn2_w_gate, v_ffn2_w_up, v_ffn2_w_down, v_final_norm):
    B, S, D = x.shape
    T = B * S
    x0 = x.reshape(T, D)
    target = loss_target.reshape(T, D)
    final_gain = final_norm.reshape(1, D)

    big = dict(
        g1=ffn1_w_gate[0].T, u1=ffn1_w_up[0].T, d1=ffn1_w_down[0],
        g2=ffn2_w_gate[0].T, u2=ffn2_w_up[0].T, d2=ffn2_w_down[0],
        w_in=w_in[0], w_out=w_out[0], w_q=mem_w_q[0], w_kv=mem_w_kv[0], w_o=mem_w_o[0])
    names = list(big)
    shard = {k: big[k].astype(BF16) for k in names}
    wp = w_pool[0].astype(BF16)
    full, ffn_w = {}, {}

    def gathered(keys, arrs):
        full.update(zip(keys, arrs))
        ffn_w.update({k: full[k].reshape(-1, D) for k in keys if k[0] in "gud"})

    first, mid, late = ("g1", "u1", "d1"), ("w_in", "w_out", "w_q", "w_kv", "w_o"), ("g2", "u2", "d2")
    gathered(first, _gather_two_level([shard[k] for k in first], "gather_ffn1"))
    (x1, hn1, gg1, uu1), got = _ffn_fwd(x0, ffn1_norm, ffn_w["g1"], ffn_w["u1"], ffn_w["d1"], "ffn1_fwd",
                                        comm=([shard[k] for k in mid], True))
    gathered(mid, got)
    hn2, qkv, u = _mix_in_fwd(x1, mix_norm, full["w_in"])
    qkv3 = qkv.reshape(B, S, -1)
    (o_sb, ltot), got = _sb_fwd(qkv3, B, S, comm=([shard[k] for k in late], True))
    gathered(late, got)
    pooled = _pool_fwd(u.reshape(B, S, -1), B, S).reshape(T, -1)
    x2, mixed = _mix_out_fwd(x1, o_sb.reshape(T, -1), pooled, wp, pool_scale, full["w_out"])
    memn, kv = _mem_kv_fwd(mem, mem_kv_norm, full["w_kv"])
    x3, hq, q, ocat = _cross_fwd(x2, mem_q_norm, kv, full["w_q"], full["w_o"], B, S)
    (x4, hn4, gg2, uu2), _ = _ffn_fwd(x3, ffn2_norm, ffn_w["g2"], ffn_w["u2"], ffn_w["d2"], "ffn2_fwd")
    dx4, d_final, loss_part = _final(x4, final_gain, target)

    slab = lambda k: grads[k].reshape((N_DEV, -1) + grads[k].shape[-1:])
    got = {}
    (dx3, a2, dg2, du2, dyh2, d_ffn2), _ = _ffn_bwd(dx4, x3, ffn2_norm, gg2, uu2, ffn_w["g2"], ffn_w["u2"],
                                                   ffn_w["d2"], "ffn2_bwd")
    grads = dict(g2=_wgrad(dg2, hn4, "dw_gate2"), u2=_wgrad(du2, hn4, "dw_up2"), d2=_wgrad(a2, dyh2, "dw_down2"))
    dx2, dx3b, dqb, dkv, d_q = _cross_bwd(dx3, x2, mem_q_norm, q, kv, full["w_q"], full["w_o"], B, S)
    grads["w_o"] = _wgrad(ocat, dx3b, "dw_o")
    grads["w_q"] = _wgrad(hq, dqb, "dw_q")
    dkvb, d_kv = _mem_kv_bwd(dkv, mem, full["w_kv"])
    grads["w_kv"] = _wgrad(memn, dkvb, "dw_kv", col_slab=full["w_kv"].shape[2])
    dx2b, do_sb, dpooled, d_wpool, d_ps = _mix_out_bwd(dx2, pooled, wp, pool_scale, full["w_out"])
    grads["w_out"] = _wgrad(mixed, dx2b, "dw_out")
    du = _pool_bwd(dpooled.reshape(B, S, -1), B, S).reshape(T, -1)
    early = ("g2", "u2", "d2", "w_o", "w_q", "w_kv", "w_out")
    (dq, dk, dv), res = _sb_bwd(qkv3, do_sb.reshape(B, S, -1), ltot, B, S, comm=([slab(k) for k in early], False))
    got.update(zip(early, res))
    dx1, dproj, d_mix = _mix_in_bwd(dx2, dq.reshape(T, -1), dk.reshape(T, -1), dv.reshape(T, -1), du,
                                    x1, mix_norm, full["w_in"])
    grads["w_in"] = _wgrad(hn2, dproj, "dw_in", col_slab=full["w_in"].shape[2])
    (dx0, a1, dg1, du1, dyh1, d_ffn1), _ = _ffn_bwd(dx1, x0, ffn1_norm, gg1, uu1, ffn_w["g1"], ffn_w["u1"],
                                                   ffn_w["d1"], "ffn1_bwd")

    small = [("ffn1_norm", d_ffn1, ffn1_norm, m_ffn1_norm, v_ffn1_norm),
             ("mix_norm", d_mix, mix_norm, m_mix_norm, v_mix_norm),
             ("w_pool", d_wpool, w_pool, m_w_pool, v_w_pool),
             ("pool_scale", d_ps, pool_scale, m_pool_scale, v_pool_scale),
             ("mem_q_norm", d_q, mem_q_norm, m_mem_q_norm, v_mem_q_norm),
             ("mem_kv_norm", d_kv, mem_kv_norm, m_mem_kv_norm, v_mem_kv_norm),
             ("ffn2_norm", d_ffn2, ffn2_norm, m_ffn2_norm, v_ffn2_norm),
             ("final_norm", d_final, final_norm, m_final_norm, v_final_norm)]
    rows = [max(8, t[2].size // LANES) for t in small]
    pack = lambda idx: jnp.concatenate([_rows128(t[idx], r) for t, r in zip(small, rows)]
                                       + ([loss_part] if idx == 1 else [jnp.zeros((8, LANES), F32)]))
    grads["g1"], (small_parts, got["w_in"]) = _wgrad(dg1, hn1, "dw_gate1",
                                                     comm=([pack(1), slab("w_in")], [True, False]))
    grads["u1"], (got["g1"],) = _wgrad(du1, hn1, "dw_up1", comm=([slab("g1")], False))
    grads["d1"], (got["u1"],) = _wgrad(a1, dyh1, "dw_down1", comm=([slab("u1")], False))
    got["d1"] = _exchange([slab("d1")], False, "scatter_last")[0]

    state = dict(
        g1=(ffn1_w_gate, m_ffn1_w_gate, v_ffn1_w_gate), u1=(ffn1_w_up, m_ffn1_w_up, v_ffn1_w_up),
        d1=(ffn1_w_down, m_ffn1_w_down, v_ffn1_w_down), g2=(ffn2_w_gate, m_ffn2_w_gate, v_ffn2_w_gate),
        u2=(ffn2_w_up, m_ffn2_w_up, v_ffn2_w_up), d2=(ffn2_w_down, m_ffn2_w_down, v_ffn2_w_down),
        w_in=(w_in, m_w_in, v_w_in), w_out=(w_out, m_w_out, v_w_out), w_q=(mem_w_q, m_mem_w_q, v_mem_w_q),
        w_kv=(mem_w_kv, m_mem_w_kv, v_mem_w_kv), w_o=(mem_w_o, m_mem_w_o, v_mem_w_o))
    big_out = {}
    for k in names:
        parts = got[k]
        if k in ("g1", "u1", "g2", "u2"):
            parts = jnp.swapaxes(parts, 1, 2)
        w_, m_, v_ = (t[0] for t in state[k])
        big_out[k] = [t[None] for t in _adamw(parts, w_, m_, v_, "adamw_" + k)]

    small_res = _adamw(small_parts, pack(2), pack(3), pack(4), "adamw_small")
    small_out, off = {}, 0
    for t, r in zip(small, rows):
        n = t[2].size // LANES
        small_out[t[0]] = [res[off:off + n].reshape(t[2].shape) for res in small_res]
        off += r
    loss = small_res[0][off, 0]

    order = [("ffn1_norm", None), ("ffn1_w_gate", "g1"), ("ffn1_w_up", "u1"), ("ffn1_w_down", "d1"),
             ("mix_norm", None), ("w_in", "w_in"), ("w_pool", None), ("pool_scale", None), ("w_out", "w_out"),
             ("mem_q_norm", None), ("mem_kv_norm", None), ("mem_w_q", "w_q"), ("mem_w_kv", "w_kv"),
             ("mem_w_o", "w_o"), ("ffn2_norm", None), ("ffn2_w_gate", "g2"), ("ffn2_w_up", "u2"),
             ("ffn2_w_down", "d2"), ("final_norm", None)]
    res = [loss, dx0.reshape(B, S, D)]
    for which in range(4):
        for name, key in order:
            res.append(big_out[key][which] if key else small_out[name][which])
    return tuple(res)
```

```python
import functools

import jax
import jax.numpy as jnp
from jax import lax
from jax.experimental import pallas as pl
from jax.experimental.pallas import tpu as pltpu

F32 = jnp.float32
BF16 = jnp.bfloat16

N_DEV = 8
EPS = 1e-6
SB_HEAD_DIM = 64
LANES = 128
POOL_WINDOWS = (2, 4, 8, 16)
POOL_GROUP_DIM = 128
MEM_HEADS = 4
FFN_RESIDUAL_WEIGHT = 0.5
ADAM_LR = 0.001
ADAM_B1 = 0.9
ADAM_B2 = 0.999
ADAM_EPS = 1e-08
ADAM_WD = 0.01
ADAM_STEP = 10
VMEM_LIMIT = 56 * 1024 * 1024

MESH_ID = pl.DeviceIdType.MESH


def _params(*sem):
    return pltpu.CompilerParams(dimension_semantics=sem, vmem_limit_bytes=VMEM_LIMIT)


def _tile(n, pref):
    if n <= pref:
        return n
    t = pref - pref % 8
    while n % t:
        t -= 8
    return t


def _mm(a, b):
    return jnp.dot(a, b, preferred_element_type=F32)


def _mm_nt(a, b):
    return lax.dot_general(a, b, (((1,), (1,)), ((), ())), preferred_element_type=F32)


def _mm_tn(a, b):
    return lax.dot_general(a, b, (((0,), (0,)), ((), ())), preferred_element_type=F32)


def _rms(xv):
    r = lax.rsqrt(jnp.mean(xv * xv, axis=-1, keepdims=True) + EPS)
    return r, xv * r


def _rms_bwd(dhn, gain, r, xhat):
    dxh = dhn * gain
    return r * (dxh - xhat * jnp.mean(dxh * xhat, axis=-1, keepdims=True))


def _sigmoid(z):
    return 0.5 * jnp.tanh(0.5 * z) + 0.5


def _flags(arrs, gather):
    return [gather] * len(arrs) if isinstance(gather, bool) else list(gather)


def _comm_shapes(arrs, gather):
    return tuple(jax.ShapeDtypeStruct(((N_DEV,) + tuple(a.shape)) if f else tuple(a.shape), a.dtype)
                 for a, f in zip(arrs, _flags(arrs, gather)))


def _comm_start(ins, outs, sems, gather):
    send_sems, recv_sems, local_sems = sems
    gather = _flags(ins, gather)
    x, y, c = lax.axis_index("x"), lax.axis_index("y"), lax.axis_index("c")
    me = 4 * x + 2 * y + c
    for i in range(len(ins)):
        src = ins[i] if gather[i] else ins[i].at[me]
        pltpu.make_async_copy(src, outs[i].at[me], local_sems.at[i]).start()
    for k in range(1, N_DEV):
        px = 1 - x if k & 4 else x
        py = 1 - y if k & 2 else y
        pc = 1 - c if k & 1 else c
        peer = 4 * px + 2 * py + pc
        for i in range(len(ins)):
            src = ins[i] if gather[i] else ins[i].at[peer]
            pltpu.make_async_remote_copy(
                src_ref=src, dst_ref=outs[i].at[me],
                send_sem=send_sems.at[i], recv_sem=recv_sems.at[i],
                device_id=(px, py, pc), device_id_type=MESH_ID).start()


def _comm_wait(ins, outs, sems, gather):
    send_sems, recv_sems, local_sems = sems
    gather = _flags(ins, gather)
    x, y, c = lax.axis_index("x"), lax.axis_index("y"), lax.axis_index("c")
    me = 4 * x + 2 * y + c
    for i in range(len(ins)):
        seven = outs[i].at[pl.ds(0, N_DEV - 1)]
        done = pltpu.make_async_remote_copy(
            src_ref=seven, dst_ref=seven,
            send_sem=send_sems.at[i], recv_sem=recv_sems.at[i],
            device_id=(x, y, c), device_id_type=MESH_ID)
        done.wait_send()
        done.wait_recv()
        src = ins[i] if gather[i] else ins[i].at[me]
        pltpu.make_async_copy(src, outs[i].at[me], local_sems.at[i]).wait()


def _comm_sems(n):
    return [pltpu.SemaphoreType.DMA((n,)) for _ in range(3)]


def _exchange(arrs, gather, name):
    n = len(arrs)

    def body(*refs):
        ins, outs, sems = refs[:n], refs[n:2 * n], refs[2 * n:]
        _comm_start(ins, outs, sems, gather)
        _comm_wait(ins, outs, sems, gather)

    any_spec = pl.BlockSpec(memory_space=pl.ANY)
    outs = pl.pallas_call(
        body, name=name, out_shape=_comm_shapes(arrs, gather),
        in_specs=[any_spec] * n, out_specs=tuple([any_spec] * n), scratch_shapes=_comm_sems(n),
    )(*arrs)
    return list(outs)


def _gather_two_level(arrs, name):
    n = len(arrs)

    def body(*refs):
        ins, outs = refs[:n], refs[n:2 * n]
        send_sems, recv_sems, local_sems = refs[2 * n:]
        x, y, c = lax.axis_index("x"), lax.axis_index("y"), lax.axis_index("c")
        me, sibling = (x, y, c), (x, y, 1 - c)
        chips = [(1 - x, y), (x, 1 - y), (1 - x, 1 - y)]

        def copy(i, k, block, to, own=False):
            slab = outs[i].at[4 * block[0] + 2 * block[1] + block[2]]
            return pltpu.make_async_remote_copy(
                src_ref=ins[i] if own else slab, dst_ref=slab,
                send_sem=send_sems.at[i, k], recv_sem=recv_sems.at[i, k],
                device_id=to, device_id_type=MESH_ID)

        mine = [pltpu.make_async_copy(ins[i], outs[i].at[4 * x + 2 * y + c], local_sems.at[i]) for i in range(n)]
        first = [copy(i, 0, me, sibling, own=True) for i in range(n)]
        first += [copy(i, 1 + j, me, (*chip, c), own=True) for j, chip in enumerate(chips) for i in range(n)]
        for cp in mine + first:
            cp.start()
        passed = []
        for j, chip in enumerate(chips):
            for i in range(n):
                copy(i, 1 + j, (*chip, c), me).wait_recv()
                passed.append(copy(i, 4 + j, (*chip, c), sibling))
                passed[-1].start()
        for i in range(n):
            copy(i, 0, sibling, me).wait_recv()
            for j, chip in enumerate(chips):
                copy(i, 4 + j, (*chip, 1 - c), me).wait_recv()
        for cp in first + passed:
            cp.wait_send()
        for cp in mine:
            cp.wait()

    any_spec = pl.BlockSpec(memory_space=pl.ANY)
    outs = pl.pallas_call(
        body, name=name, out_shape=_comm_shapes(arrs, True),
        in_specs=[any_spec] * n, out_specs=tuple([any_spec] * n),
        scratch_shapes=[pltpu.SemaphoreType.DMA((n, N_DEV - 1)), pltpu.SemaphoreType.DMA((n, N_DEV - 1)),
                        pltpu.SemaphoreType.DMA((n,))],
    )(*arrs)
    return list(outs)


def _call(body, *, name, grid, in_specs, out_specs, out_shape, args, scratch=(), comm=None):
    sem = ("arbitrary",) * len(grid)
    if comm is None:
        res = pl.pallas_call(body, name=name, grid=grid, in_specs=list(in_specs), out_specs=tuple(out_specs),
                             out_shape=tuple(out_shape), scratch_shapes=list(scratch),
                             compiler_params=_params(*sem))(*args)
        return tuple(res), []
    arrs, gather = comm
    n, n_in, n_out, n_sc = len(arrs), len(args), len(out_shape), len(scratch)

    def wrapped(*refs):
        ins, cin = refs[:n_in], refs[n_in:n_in + n]
        outs, cout = refs[n_in + n:n_in + n + n_out], refs[n_in + n + n_out:n_in + 2 * n + n_out]
        sc, sems = refs[n_in + 2 * n + n_out:n_in + 2 * n + n_out + n_sc], refs[n_in + 2 * n + n_out + n_sc:]
        ids = [pl.program_id(a) for a in range(len(grid))]
        first = functools.reduce(jnp.logical_and, [i == 0 for i in ids])
        last = functools.reduce(jnp.logical_and, [i == g - 1 for i, g in zip(ids, grid)])

        @pl.when(first)
        def _():
            _comm_start(cin, cout, sems, gather)

        body(*ins, *outs, *sc)

        @pl.when(last)
        def _():
            _comm_wait(cin, cout, sems, gather)

    any_spec = pl.BlockSpec(memory_space=pl.ANY)
    res = pl.pallas_call(
        wrapped, name=name, grid=grid, in_specs=list(in_specs) + [any_spec] * n,
        out_specs=tuple(out_specs) + (any_spec,) * n, out_shape=tuple(out_shape) + _comm_shapes(arrs, gather),
        scratch_shapes=list(scratch) + _comm_sems(n), compiler_params=_params(*sem))(*args, *arrs)
    return tuple(res[:n_out]), list(res[n_out:])


def _ffn_fwd(x, gain, wgt, wut, wd, name, comm=None):
    T, D = x.shape
    F = wd.shape[0]
    tm, tf = _tile(T, 1024), _tile(F, 256)
    nj = F // tf
    rc = _tile(tm, 512)

    def body(x_ref, gain_ref, wg_ref, wu_ref, wd_ref, out_ref, hn_ref, g_ref, u_ref, acc, hn_s):
        j = pl.program_id(1)

        @pl.when(j == 0)
        def _():
            _, xhat = _rms(x_ref[...])
            hn = (xhat * gain_ref[...]).astype(BF16)
            hn_s[...] = hn
            hn_ref[...] = hn
            acc[...] = jnp.zeros_like(acc)

        for r0 in range(0, tm, rc):
            rows = slice(r0, r0 + rc)
            hn = hn_s[rows, :]
            g = _mm_nt(hn, wg_ref[...])
            u = _mm_nt(hn, wu_ref[...])
            g_ref[rows, :] = g.astype(BF16)
            u_ref[rows, :] = u.astype(BF16)
            a = (g * _sigmoid(g) * u).astype(BF16)
            acc[rows, :] += _mm(a, wd_ref[...])

        @pl.when(j == nj - 1)
        def _():
            out_ref[...] = x_ref[...] + FFN_RESIDUAL_WEIGHT * acc[...]

    row = lambda i, j: (i, 0)
    wspec = pl.BlockSpec((tf, D), lambda i, j: (j, 0))
    return _call(
        body, name=name, grid=(T // tm, nj), comm=comm,
        in_specs=[pl.BlockSpec((tm, D), row), pl.BlockSpec((1, D), lambda i, j: (0, 0)), wspec, wspec, wspec],
        out_specs=(pl.BlockSpec((tm, D), row), pl.BlockSpec((tm, D), row),
                   pl.BlockSpec((tm, tf), lambda i, j: (i, j)), pl.BlockSpec((tm, tf), lambda i, j: (i, j))),
        out_shape=(jax.ShapeDtypeStruct((T, D), F32), jax.ShapeDtypeStruct((T, D), BF16),
                   jax.ShapeDtypeStruct((T, F), BF16), jax.ShapeDtypeStruct((T, F), BF16)),
        scratch=[pltpu.VMEM((tm, D), F32), pltpu.VMEM((tm, D), BF16)],
        args=(x, gain, wgt, wut, wd))


def _ffn_bwd(dy, x, gain, g, u, wgt, wut, wd, name, comm=None):
    T, D = x.shape
    F = wd.shape[0]
    tm, tf = _tile(T, 1024), _tile(F, 256)
    nj = F // tf
    rc = _tile(tm, 512)

    def body(dy_ref, x_ref, gain_ref, g_ref, u_ref, wg_ref, wu_ref, wd_ref,
             dx_ref, a_ref, dg_ref, du_ref, dyh_ref, dgain_ref, acc, dyh_s):
        i, j = pl.program_id(0), pl.program_id(1)

        @pl.when(j == 0)
        def _():
            dyh = (FFN_RESIDUAL_WEIGHT * dy_ref[...]).astype(BF16)
            dyh_s[...] = dyh
            dyh_ref[...] = dyh
            acc[...] = jnp.zeros_like(acc)

        @pl.when((i == 0) & (j == 0))
        def _():
            dgain_ref[...] = jnp.zeros_like(dgain_ref)

        for r0 in range(0, tm, rc):
            rows = slice(r0, r0 + rc)
            gv = g_ref[rows, :].astype(F32)
            uv = u_ref[rows, :].astype(F32)
            da = _mm_nt(dyh_s[rows, :], wd_ref[...])
            sig = _sigmoid(gv)
            s = gv * sig
            a_ref[rows, :] = (s * uv).astype(BF16)
            dg = (da * uv * (sig + s * (1.0 - sig))).astype(BF16)
            du = (da * s).astype(BF16)
            dg_ref[rows, :] = dg
            du_ref[rows, :] = du
            acc[rows, :] += _mm(dg, wg_ref[...]) + _mm(du, wu_ref[...])

        @pl.when(j == nj - 1)
        def _():
            r, xhat = _rms(x_ref[...])
            dhn = acc[...]
            dgain_ref[...] += jnp.sum(dhn * xhat, axis=0, keepdims=True)
            dx_ref[...] = dy_ref[...] + _rms_bwd(dhn, gain_ref[...], r, xhat)

    row = lambda i, j: (i, 0)
    tile = lambda i, j: (i, j)
    wspec = pl.BlockSpec((tf, D), lambda i, j: (j, 0))
    one = pl.BlockSpec((1, D), lambda i, j: (0, 0))
    return _call(
        body, name=name, grid=(T // tm, nj), comm=comm,
        in_specs=[pl.BlockSpec((tm, D), row), pl.BlockSpec((tm, D), row), one,
                  pl.BlockSpec((tm, tf), tile), pl.BlockSpec((tm, tf), tile), wspec, wspec, wspec],
        out_specs=(pl.BlockSpec((tm, D), row), pl.BlockSpec((tm, tf), tile), pl.BlockSpec((tm, tf), tile),
                   pl.BlockSpec((tm, tf), tile), pl.BlockSpec((tm, D), row), one),
        out_shape=(jax.ShapeDtypeStruct((T, D), F32), jax.ShapeDtypeStruct((T, F), BF16),
                   jax.ShapeDtypeStruct((T, F), BF16), jax.ShapeDtypeStruct((T, F), BF16),
                   jax.ShapeDtypeStruct((T, D), BF16), jax.ShapeDtypeStruct((1, D), F32)),
        scratch=[pltpu.VMEM((tm, D), F32), pltpu.VMEM((tm, D), BF16)],
        args=(dy, x, gain, g, u, wgt, wut, wd))


def _wgrad(a, b, name, col_slab=None, comm=None):
    T, M = a.shape
    N = b.shape[1]
    tmm = M if M <= 1024 else _tile(M, 1408)
    tn = _tile(N, 1024)
    tk = _tile(T, 512)
    nk = T // tk
    per = tn // col_slab if col_slab else 0

    def body(a_ref, b_ref, out_ref, acc):
        k = pl.program_id(2)

        @pl.when(k == 0)
        def _():
            acc[...] = jnp.zeros_like(acc)

        acc[...] += _mm_tn(a_ref[...], b_ref[...])

        @pl.when(k == nk - 1)
        def _():
            if col_slab:
                for s in range(per):
                    out_ref[s] = acc[:, s * col_slab:(s + 1) * col_slab].astype(BF16)
            else:
                out_ref[...] = acc[...].astype(BF16)

    if col_slab:
        out_spec = pl.BlockSpec((per, tmm, col_slab), lambda m, n, k: (n, m, 0))
        out_shape = jax.ShapeDtypeStruct((N // col_slab, M, col_slab), BF16)
    else:
        out_spec = pl.BlockSpec((tmm, tn), lambda m, n, k: (m, n))
        out_shape = jax.ShapeDtypeStruct((M, N), BF16)
    (out,), got = _call(
        body, name=name, grid=(M // tmm, N // tn, nk), comm=comm,
        in_specs=[pl.BlockSpec((tk, tmm), lambda m, n, k: (k, m)), pl.BlockSpec((tk, tn), lambda m, n, k: (k, n))],
        out_specs=(out_spec,), out_shape=(out_shape,),
        scratch=[pltpu.VMEM((tmm, tn), F32)], args=(a, b))
    return (out, got) if comm else out


def _mix_in_fwd(x, gain, w_in):
    T, D = x.shape
    cs = w_in.shape[2]
    n_qkv = 3 * (N_DEV // 4)
    tm = _tile(T, 512)

    def body(x_ref, gain_ref, w_ref, hn_ref, qkv_ref, u_ref):
        _, xhat = _rms(x_ref[...])
        hn = (xhat * gain_ref[...]).astype(BF16)
        hn_ref[...] = hn
        for b in range(N_DEV):
            p = _mm(hn, w_ref[b])
            if b < n_qkv:
                qkv_ref[:, b * cs:(b + 1) * cs] = p.astype(BF16)
            else:
                u_ref[:, (b - n_qkv) * cs:(b - n_qkv + 1) * cs] = p

    row = lambda i: (i, 0)
    return pl.pallas_call(
        body, name="mix_in_fwd", grid=(T // tm,),
        in_specs=[pl.BlockSpec((tm, D), row), pl.BlockSpec((1, D), lambda i: (0, 0)),
                  pl.BlockSpec((N_DEV, D, cs), lambda i: (0, 0, 0))],
        out_specs=(pl.BlockSpec((tm, D), row), pl.BlockSpec((tm, n_qkv * cs), row),
                   pl.BlockSpec((tm, (N_DEV - n_qkv) * cs), row)),
        out_shape=(jax.ShapeDtypeStruct((T, D), BF16), jax.ShapeDtypeStruct((T, n_qkv * cs), BF16),
                   jax.ShapeDtypeStruct((T, (N_DEV - n_qkv) * cs), F32)),
        compiler_params=_params("arbitrary"),
    )(x, gain, w_in)


def _mix_in_bwd(dres, dq, dk, dv, du, x, gain, w_in):
    T, D = x.shape
    cs = w_in.shape[2]
    W = dq.shape[1]
    per = W // cs
    tm = _tile(T, 512)

    def body(dres_ref, dq_ref, dk_ref, dv_ref, du_ref, x_ref, gain_ref, w_ref, dx_ref, dproj_ref, dgain_ref):
        i = pl.program_id(0)

        @pl.when(i == 0)
        def _():
            dgain_ref[...] = jnp.zeros_like(dgain_ref)

        dhn = jnp.zeros((tm, D), F32)
        for part, ref in enumerate((dq_ref, dk_ref, dv_ref, du_ref)):
            for h in range(per):
                b = part * per + h
                d = ref[:, h * cs:(h + 1) * cs]
                dproj_ref[:, b * cs:(b + 1) * cs] = d
                dhn = dhn + _mm_nt(d, w_ref[b])
        r, xhat = _rms(x_ref[...])
        dgain_ref[...] += jnp.sum(dhn * xhat, axis=0, keepdims=True)
        dx_ref[...] = dres_ref[...] + _rms_bwd(dhn, gain_ref[...], r, xhat)

    row = lambda i: (i, 0)
    one = pl.BlockSpec((1, D), lambda i: (0, 0))
    part = pl.BlockSpec((tm, W), row)
    return pl.pallas_call(
        body, name="mix_in_bwd", grid=(T // tm,),
        in_specs=[pl.BlockSpec((tm, D), row), part, part, part, part, pl.BlockSpec((tm, D), row), one,
                  pl.BlockSpec((N_DEV, D, cs), lambda i: (0, 0, 0))],
        out_specs=(pl.BlockSpec((tm, D), row), pl.BlockSpec((tm, 4 * W), row), one),
        out_shape=(jax.ShapeDtypeStruct((T, D), F32), jax.ShapeDtypeStruct((T, 4 * W), BF16),
                   jax.ShapeDtypeStruct((1, D), F32)),
        compiler_params=_params("arbitrary"),
    )(dres, dq, dk, dv, du, x, gain, w_in)


SB_PAIRS_PER_PROGRAM = 2
LOG2_E = 1.4426950408889634
EXP2_CLAMP = 126.0


def _neg_log2_sigmoid(nz2):
    w = jnp.minimum(nz2, EXP2_CLAMP)
    return w, jnp.log2(1.0 + jnp.exp2(w))


SB_DEAD_LOG2 = -160.0


def _sb_live(rests):
    worst = functools.reduce(jnp.maximum, rests)
    return (jnp.max(worst) > SB_DEAD_LOG2).astype(jnp.int32)


def _split(v):
    hi = v.astype(BF16)
    return hi, (v - hi.astype(F32)).astype(BF16)


def _tri_sum(v, tri):
    hi, lo = _split(v)
    return _mm(hi, tri) + _mm(lo, tri)


def _sb_fwd(qkv, B, S, comm=None):
    W = qkv.shape[2] // 3
    n_pair = W // LANES
    bq = _tile(S, 256)
    nq = S // bq
    hp = SB_PAIRS_PER_PROGRAM
    nscale2 = -(SB_HEAD_DIM ** -0.5) * LOG2_E

    def body(q_ref, k_ref, v_ref, o_ref):
        lane = lax.broadcasted_iota(jnp.int32, (1, LANES), 1)
        head0 = lane < SB_HEAD_DIM
        rr = lax.broadcasted_iota(jnp.int32, (bq, bq), 0)
        cc = lax.broadcasted_iota(jnp.int32, (bq, bq), 1)
        strict = cc < rr
        after = jnp.where(rr > cc, 1.0, 0.0).astype(BF16)

        def blocks(heads, ks, carries, diag):
            n = range(len(heads))
            keep = (lambda t: jnp.where(strict, t, 0.0)) if diag else (lambda t: t)
            z = [_mm_nt(qh, k_ref[ks, cols]) for qh, cols in heads]
            wl = [_neg_log2_sigmoid(z[h] * nscale2) for h in n]
            lr = [keep(wl[h][0] - wl[h][1]) for h in n]
            parts = [_split(lr[h]) for h in n]
            suf = [_mm(parts[h][0], after) + _mm(parts[h][1], after) for h in n]
            a = [keep(jnp.exp2(suf[h] + carries[h][1] - wl[h][1])).astype(BF16) for h in n]
            o = [carries[h][0] + _mm(a[h], v_ref[ks, heads[h][1]]) for h in n]
            return tuple((o[h], carries[h][1] + (suf[h][:, :1] + lr[h][:, :1])) for h in n)

        def q_tile(i, _):
            qs = pl.ds(pl.multiple_of(i * bq, bq), bq)
            heads = []
            for pr in range(hp):
                cols = slice(pr * LANES, (pr + 1) * LANES)
                qv = q_ref[qs, cols]
                heads += [(jnp.where(head0, qv, jnp.zeros_like(qv)), cols),
                          (jnp.where(head0, jnp.zeros_like(qv), qv), cols)]
            zero = (jnp.zeros((bq, LANES), F32), jnp.zeros((bq, 1), F32))
            init = blocks(heads, qs, (zero,) * len(heads), True)

            def left(st):
                t, _, cr = st
                ks = pl.ds(pl.multiple_of((i - 1 - t) * bq, bq), bq)
                cr = blocks(heads, ks, cr, False)
                return t + 1, _sb_live([c for _, c in cr]), cr

            _, _, res = lax.while_loop(lambda st: jnp.logical_and(st[0] < i, st[1] > 0), left,
                                       (jnp.int32(0), _sb_live([c for _, c in init]), init))
            for pr in range(hp):
                o_ref[qs, heads[2 * pr][1]] = jnp.where(head0, res[2 * pr][0], res[2 * pr + 1][0]).astype(BF16)
            return 0

        lax.fori_loop(0, nq, q_tile, 0)

    def col(off):
        return pl.BlockSpec((None, S, hp * LANES), lambda b, p: (b, 0, off + p))

    n_pair //= hp
    return _call(
        body, name="sb_fwd", grid=(B, n_pair), comm=comm,
        in_specs=[col(0), col(n_pair), col(2 * n_pair)],
        out_specs=(col(0),),
        out_shape=(jax.ShapeDtypeStruct((B, S, W), BF16),),
        args=(qkv, qkv, qkv))


def _sb_bwd(qkv, do, B, S, comm=None):
    W = qkv.shape[2] // 3
    n_pair = W // LANES
    bq = _tile(S, 256)
    nq = S // bq
    hp = SB_PAIRS_PER_PROGRAM
    scale = SB_HEAD_DIM ** -0.5
    nscale2 = -scale * LOG2_E

    def body(q_ref, k_ref, v_ref, do_ref, dq_ref, dk_ref, dv_ref, dk_s, dv_s, e_s, sg_s, a_s):
        lane = lax.broadcasted_iota(jnp.int32, (1, LANES), 1)
        head0 = lane < SB_HEAD_DIM
        rr = lax.broadcasted_iota(jnp.int32, (bq, bq), 0)
        cc = lax.broadcasted_iota(jnp.int32, (bq, bq), 1)
        strict = cc < rr
        after = jnp.where(rr > cc, 1.0, 0.0).astype(BF16)
        before = jnp.where(rr < cc, 1.0, 0.0).astype(BF16)
        dk_s[...] = jnp.zeros_like(dk_s)
        dv_s[...] = jnp.zeros_like(dv_s)

        def weights(heads, ks, kb, rests, diag):
            n = range(len(heads))
            keep = (lambda t: jnp.where(strict, t, 0.0)) if diag else (lambda t: t)
            z = [_mm_nt(heads[h][0], k_ref[ks, heads[h][2]]) for h in n]
            da = [_mm_nt(heads[h][1], v_ref[ks, heads[h][2]]) for h in n]
            wl = [_neg_log2_sigmoid(z[h] * nscale2) for h in n]
            lr = [keep(wl[h][0] - wl[h][1]) for h in n]
            parts = [_split(lr[h]) for h in n]
            suf = [_mm(parts[h][0], after) + _mm(parts[h][1], after) for h in n]
            a = [keep(jnp.exp2(suf[h] + rests[h] - wl[h][1])) for h in n]
            for h in n:
                a_s[h * nq + kb] = a[h].astype(BF16)
                e_s[h * nq + kb] = a[h] * da[h]
                sg_s[h * nq + kb] = jnp.exp2(-wl[h][1])
            return tuple(rests[h] + (suf[h][:, :1] + lr[h][:, :1]) for h in n)

        def grads(heads, ks, kb, carries, diag):
            n = range(len(heads))
            keep = (lambda t: jnp.where(strict, t, 0.0)) if diag else (lambda t: t)
            e = [e_s[h * nq + kb] for h in n]
            parts = [_split(e[h]) for h in n]
            pex = [_mm(parts[h][0], before) + _mm(parts[h][1], before) for h in n]
            dz = [keep(e[h] - sg_s[h * nq + kb] * (e[h] + pex[h] + carries[h][1])).astype(BF16) for h in n]
            dq = [carries[h][0] + _mm(dz[h], k_ref[ks, heads[h][2]]) for h in n]
            for h in n:
                dk_s[ks, heads[h][2]] += _mm_tn(dz[h], heads[h][0])
                dv_s[ks, heads[h][2]] += _mm_tn(a_s[h * nq + kb], heads[h][1])
            return tuple((dq[h], carries[h][1] + (pex[h][:, bq - 1:] + e[h][:, bq - 1:])) for h in n)

        def q_tile(i, _):
            qs = pl.ds(pl.multiple_of(i * bq, bq), bq)
            heads = []
            for pr in range(hp):
                cols = slice(pr * LANES, (pr + 1) * LANES)
                qv, dov = q_ref[qs, cols], do_ref[qs, cols]
                zq, zd = jnp.zeros_like(qv), jnp.zeros_like(dov)
                heads += [(jnp.where(head0, qv, zq), jnp.where(head0, dov, zd), cols),
                          (jnp.where(head0, zq, qv), jnp.where(head0, zd, dov), cols)]
            key_block = lambda kb: pl.ds(pl.multiple_of(kb * bq, bq), bq)
            rests = weights(heads, qs, i, (jnp.zeros((bq, 1), F32),) * len(heads), True)

            def left(st):
                t, _, rs = st
                rs = weights(heads, key_block(i - 1 - t), i - 1 - t, rs, False)
                return t + 1, _sb_live(rs), rs

            n_left, _, _ = lax.while_loop(lambda st: jnp.logical_and(st[0] < i, st[1] > 0), left,
                                          (jnp.int32(0), _sb_live(rests), rests))
            zero = (jnp.zeros((bq, LANES), F32), jnp.zeros((bq, 1), F32))
            res = lax.fori_loop(0, n_left, lambda t, cr: grads(heads, key_block(i - n_left + t), i - n_left + t, cr, False),
                                (zero,) * len(heads))
            res = grads(heads, qs, i, res, True)
            for pr in range(hp):
                dq = jnp.where(head0, res[2 * pr][0], res[2 * pr + 1][0])
                dq_ref[qs, heads[2 * pr][2]] = (dq * scale).astype(BF16)
            return 0

        lax.fori_loop(0, nq, q_tile, 0)
        dk_ref[...] = (dk_s[...] * scale).astype(BF16)
        dv_ref[...] = dv_s[...].astype(BF16)

    def col(off):
        return pl.BlockSpec((None, S, hp * LANES), lambda b, p: (b, 0, off + p))

    n_pair //= hp
    shp = jax.ShapeDtypeStruct((B, S, W), BF16)
    slots = 2 * hp * nq
    return _call(
        body, name="sb_bwd", grid=(B, n_pair), comm=comm,
        in_specs=[col(0), col(n_pair), col(2 * n_pair), col(0)],
        out_specs=(col(0), col(0), col(0)),
        out_shape=(shp, shp, shp),
        scratch=[pltpu.VMEM((S, hp * LANES), F32), pltpu.VMEM((S, hp * LANES), F32),
                 pltpu.VMEM((slots, bq, bq), F32), pltpu.VMEM((slots, bq, bq), F32),
                 pltpu.VMEM((slots, bq, bq), BF16)],
        args=(qkv, qkv, qkv, do))


def _pool_counts(S):
    t = lax.broadcasted_iota(jnp.int32, (S, 1), 0)
    return t, [jnp.minimum(t + 1, w).astype(F32) for w in POOL_WINDOWS]


def _pool_fwd(u, B, S):
    W = u.shape[2]

    def body(u_ref, out_ref):
        t, counts = _pool_counts(S)
        for gi, w in enumerate(POOL_WINDOWS):
            cols = slice(gi * POOL_GROUP_DIM, (gi + 1) * POOL_GROUP_DIM)
            ug = u_ref[:, cols]
            s, k = ug, 1
            while k < w:
                s = s + jnp.where(t >= k, pltpu.roll(s, k, axis=0), 0.0)
                k *= 2
            out_ref[:, cols] = (s / counts[gi] - ug).astype(BF16)

    spec = pl.BlockSpec((None, S, W), lambda b: (b, 0, 0))
    return pl.pallas_call(
        body, name="pool_fwd", grid=(B,), in_specs=[spec], out_specs=spec,
        out_shape=jax.ShapeDtypeStruct((B, S, W), BF16), compiler_params=_params("arbitrary"),
    )(u)


def _pool_bwd(dpooled, B, S):
    W = dpooled.shape[2]

    def body(d_ref, out_ref):
        t, counts = _pool_counts(S)
        for gi, w in enumerate(POOL_WINDOWS):
            cols = slice(gi * POOL_GROUP_DIM, (gi + 1) * POOL_GROUP_DIM)
            d = d_ref[:, cols]
            s, k = d / counts[gi], 1
            while k < w:
                s = s + jnp.where(t < S - k, pltpu.roll(s, S - k, axis=0), 0.0)
                k *= 2
            out_ref[:, cols] = (s - d).astype(BF16)

    spec = pl.BlockSpec((None, S, W), lambda b: (b, 0, 0))
    return pl.pallas_call(
        body, name="pool_bwd", grid=(B,), in_specs=[spec], out_specs=spec,
        out_shape=jax.ShapeDtypeStruct((B, S, W), BF16), compiler_params=_params("arbitrary"),
    )(dpooled)


def _mix_out_fwd(x, o_sb, pooled, w_pool, pool_scale, w_out):
    T, D = x.shape
    W = o_sb.shape[1]
    G = w_pool.shape[0]
    gd = POOL_GROUP_DIM
    tm = _tile(T, 512)

    def body(x_ref, osb_ref, pooled_ref, wp_ref, ps_ref, wo_ref, out_ref, mixed_ref):
        mixed_ref[:, :W] = osb_ref[...]
        for gi in range(G):
            cols = slice(gi * gd, (gi + 1) * gd)
            pw = _mm(pooled_ref[:, cols], wp_ref[gi])
            mixed_ref[:, W + gi * gd:W + (gi + 1) * gd] = (pw * ps_ref[:, cols]).astype(BF16)
        out_ref[...] = x_ref[...] + _mm(mixed_ref[...], wo_ref[...].reshape(D, D))

    row = lambda i: (i, 0)
    return pl.pallas_call(
        body, name="mix_out_fwd", grid=(T // tm,),
        in_specs=[pl.BlockSpec((tm, D), row), pl.BlockSpec((tm, W), row), pl.BlockSpec((tm, W), row),
                  pl.BlockSpec((G, gd, gd), lambda i: (0, 0, 0)), pl.BlockSpec((1, W), lambda i: (0, 0)),
                  pl.BlockSpec(w_out.shape, lambda i: (0, 0, 0))],
        out_specs=(pl.BlockSpec((tm, D), row), pl.BlockSpec((tm, D), row)),
        out_shape=(jax.ShapeDtypeStruct((T, D), F32), jax.ShapeDtypeStruct((T, D), BF16)),
        compiler_params=_params("arbitrary"),
    )(x, o_sb, pooled, w_pool, pool_scale, w_out)


def _mix_out_bwd(dx, pooled, w_pool, pool_scale, w_out):
    T, D = dx.shape
    W = pooled.shape[1]
    G = w_pool.shape[0]
    gd = POOL_GROUP_DIM
    tm = _tile(T, 512)

    def body(dx_ref, pooled_ref, wp_ref, ps_ref, wo_ref, dxb_ref, dosb_ref, dpooled_ref, dwp_ref, dps_ref):
        i = pl.program_id(0)

        @pl.when(i == 0)
        def _():
            dwp_ref[...] = jnp.zeros_like(dwp_ref)
            dps_ref[...] = jnp.zeros_like(dps_ref)

        dxb = dx_ref[...].astype(BF16)
        dxb_ref[...] = dxb
        dmixed = _mm_nt(dxb, wo_ref[...].reshape(D, D))
        dosb_ref[...] = dmixed[:, :W].astype(BF16)
        for gi in range(G):
            cols = slice(gi * gd, (gi + 1) * gd)
            pg = pooled_ref[:, cols]
            dop = dmixed[:, W + gi * gd:W + (gi + 1) * gd]
            pw = _mm(pg, wp_ref[gi])
            dps_ref[:, cols] += jnp.sum(dop * pw, axis=0, keepdims=True)
            dpw = (dop * ps_ref[:, cols]).astype(BF16)
            dwp_ref[gi] += _mm_tn(pg, dpw)
            dpooled_ref[:, cols] = _mm_nt(dpw, wp_ref[gi])

    row = lambda i: (i, 0)
    return pl.pallas_call(
        body, name="mix_out_bwd", grid=(T // tm,),
        in_specs=[pl.BlockSpec((tm, D), row), pl.BlockSpec((tm, W), row),
                  pl.BlockSpec((G, gd, gd), lambda i: (0, 0, 0)), pl.BlockSpec((1, W), lambda i: (0, 0)),
                  pl.BlockSpec(w_out.shape, lambda i: (0, 0, 0))],
        out_specs=(pl.BlockSpec((tm, D), row), pl.BlockSpec((tm, W), row), pl.BlockSpec((tm, W), row),
                   pl.BlockSpec((G, gd, gd), lambda i: (0, 0, 0)), pl.BlockSpec((1, W), lambda i: (0, 0))),
        out_shape=(jax.ShapeDtypeStruct((T, D), BF16), jax.ShapeDtypeStruct((T, W), BF16),
                   jax.ShapeDtypeStruct((T, W), F32), jax.ShapeDtypeStruct((G, gd, gd), F32),
                   jax.ShapeDtypeStruct((1, W), F32)),
        compiler_params=_params("arbitrary"),
    )(dx, pooled, w_pool, pool_scale, w_out)


def _mem_kv_fwd(mem, gain, w_kv):
    B, M, D = mem.shape
    cs = w_kv.shape[2]

    def body(mem_ref, gain_ref, w_ref, memn_ref, kv_ref):
        _, xhat = _rms(mem_ref[...])
        mn = (xhat * gain_ref[...]).astype(BF16)
        memn_ref[...] = mn
        for b in range(N_DEV):
            kv_ref[:, b * cs:(b + 1) * cs] = _mm(mn, w_ref[b]).astype(BF16)

    return pl.pallas_call(
        body, name="mem_kv_fwd", grid=(B,),
        in_specs=[pl.BlockSpec((None, M, D), lambda b: (b, 0, 0)), pl.BlockSpec((1, D), lambda b: (0, 0)),
                  pl.BlockSpec((N_DEV, D, cs), lambda b: (0, 0, 0))],
        out_specs=(pl.BlockSpec((M, D), lambda b: (b, 0)), pl.BlockSpec((None, M, N_DEV * cs), lambda b: (b, 0, 0))),
        out_shape=(jax.ShapeDtypeStruct((B * M, D), BF16), jax.ShapeDtypeStruct((B, M, N_DEV * cs), BF16)),
        compiler_params=_params("arbitrary"),
    )(mem, gain, w_kv)


def _mem_kv_bwd(dkv, mem, w_kv):
    B, M, D = mem.shape
    cs = w_kv.shape[2]

    def body(dkv_ref, mem_ref, w_ref, dkvb_ref, dgain_ref):
        b_id = pl.program_id(0)

        @pl.when(b_id == 0)
        def _():
            dgain_ref[...] = jnp.zeros_like(dgain_ref)

        dkvb = dkv_ref[...].astype(BF16)
        dkvb_ref[...] = dkvb
        dmn = jnp.zeros((M, D), F32)
        for b in range(N_DEV):
            dmn = dmn + _mm_nt(dkvb[:, b * cs:(b + 1) * cs], w_ref[b])
        _, xhat = _rms(mem_ref[...])
        dgain_ref[...] += jnp.sum(dmn * xhat, axis=0, keepdims=True)

    return pl.pallas_call(
        body, name="mem_kv_bwd", grid=(B,),
        in_specs=[pl.BlockSpec((None, M, N_DEV * cs), lambda b: (b, 0, 0)),
                  pl.BlockSpec((None, M, D), lambda b: (b, 0, 0)),
                  pl.BlockSpec((N_DEV, D, cs), lambda b: (0, 0, 0))],
        out_specs=(pl.BlockSpec((M, N_DEV * cs), lambda b: (b, 0)), pl.BlockSpec((1, D), lambda b: (0, 0))),
        out_shape=(jax.ShapeDtypeStruct((B * M, N_DEV * cs), BF16), jax.ShapeDtypeStruct((1, D), F32)),
        compiler_params=_params("arbitrary"),
    )(dkv, mem, w_kv)


def _softmax_rows(s):
    p = jnp.exp(s - jnp.max(s, axis=1, keepdims=True))
    return p / jnp.sum(p, axis=1, keepdims=True)


def _cross_fwd(x, gain, kv, w_q, w_o, B, S):
    T, D = x.shape
    M = kv.shape[1]
    hd = D // MEM_HEADS
    tm = _tile(S, 512)
    per = S // tm
    scale = hd ** -0.5

    def body(x_ref, gain_ref, kv_ref, wq_ref, wo_ref, out_ref, hq_ref, q_ref, ocat_ref):
        _, xhat = _rms(x_ref[...])
        hq = (xhat * gain_ref[...]).astype(BF16)
        hq_ref[...] = hq
        q = _mm(hq, wq_ref[...].reshape(D, D)).astype(BF16)
        q_ref[...] = q
        for h in range(MEM_HEADS):
            cols = slice(h * hd, (h + 1) * hd)
            s = _mm_nt(q[:, cols], kv_ref[:, cols]) * scale
            p = _softmax_rows(s).astype(BF16)
            ocat_ref[:, cols] = _mm(p, kv_ref[:, D + h * hd:D + (h + 1) * hd]).astype(BF16)
        out_ref[...] = x_ref[...] + _mm(ocat_ref[...], wo_ref[...].reshape(D, D))

    row = lambda b, t: (b * per + t, 0)
    wspec = pl.BlockSpec(w_q.shape, lambda b, t: (0, 0, 0))
    return pl.pallas_call(
        body, name="cross_fwd", grid=(B, per),
        in_specs=[pl.BlockSpec((tm, D), row), pl.BlockSpec((1, D), lambda b, t: (0, 0)),
                  pl.BlockSpec((None, M, 2 * D), lambda b, t: (b, 0, 0)), wspec, wspec],
        out_specs=tuple(pl.BlockSpec((tm, D), row) for _ in range(4)),
        out_shape=(jax.ShapeDtypeStruct((T, D), F32),) + tuple(jax.ShapeDtypeStruct((T, D), BF16) for _ in range(3)),
        compiler_params=_params("arbitrary", "arbitrary"),
    )(x, gain, kv, w_q, w_o)


def _cross_bwd(dy, x, gain, q, kv, w_q, w_o, B, S):
    T, D = x.shape
    M = kv.shape[1]
    hd = D // MEM_HEADS
    tm = _tile(S, 512)
    per = S // tm
    scale = hd ** -0.5

    def body(dy_ref, x_ref, gain_ref, q_ref, kv_ref, wq_ref, wo_ref,
             dx_ref, dyb_ref, dqb_ref, dkv_ref, dgain_ref):
        b_id, t_id = pl.program_id(0), pl.program_id(1)

        @pl.when((b_id == 0) & (t_id == 0))
        def _():
            dgain_ref[...] = jnp.zeros_like(dgain_ref)

        @pl.when(t_id == 0)
        def _():
            dkv_ref[...] = jnp.zeros_like(dkv_ref)

        dyb = dy_ref[...].astype(BF16)
        dyb_ref[...] = dyb
        docat = _mm_nt(dyb, wo_ref[...].reshape(D, D)).astype(BF16)
        for h in range(MEM_HEADS):
            cols = slice(h * hd, (h + 1) * hd)
            vcols = slice(D + h * hd, D + (h + 1) * hd)
            qh, kh, vh, doh = q_ref[:, cols], kv_ref[:, cols], kv_ref[:, vcols], docat[:, cols]
            p = _softmax_rows(_mm_nt(qh, kh) * scale)
            dp = _mm_nt(doh, vh)
            ds = (p * (dp - jnp.sum(dp * p, axis=1, keepdims=True)) * scale).astype(BF16)
            dqb_ref[:, cols] = _mm(ds, kh).astype(BF16)
            dkv_ref[:, cols] += _mm_tn(ds, qh)
            dkv_ref[:, vcols] += _mm_tn(p.astype(BF16), doh)
        dhq = _mm_nt(dqb_ref[...], wq_ref[...].reshape(D, D))
        r, xhat = _rms(x_ref[...])
        dgain_ref[...] += jnp.sum(dhq * xhat, axis=0, keepdims=True)
        dx_ref[...] = dy_ref[...] + _rms_bwd(dhq, gain_ref[...], r, xhat)

    row = lambda b, t: (b * per + t, 0)
    wspec = pl.BlockSpec(w_q.shape, lambda b, t: (0, 0, 0))
    one = pl.BlockSpec((1, D), lambda b, t: (0, 0))
    kvspec = pl.BlockSpec((None, M, 2 * D), lambda b, t: (b, 0, 0))
    return pl.pallas_call(
        body, name="cross_bwd", grid=(B, per),
        in_specs=[pl.BlockSpec((tm, D), row), pl.BlockSpec((tm, D), row), one, pl.BlockSpec((tm, D), row),
                  kvspec, wspec, wspec],
        out_specs=(pl.BlockSpec((tm, D), row), pl.BlockSpec((tm, D), row), pl.BlockSpec((tm, D), row), kvspec, one),
        out_shape=(jax.ShapeDtypeStruct((T, D), F32), jax.ShapeDtypeStruct((T, D), BF16),
                   jax.ShapeDtypeStruct((T, D), BF16), jax.ShapeDtypeStruct((B, M, 2 * D), F32),
                   jax.ShapeDtypeStruct((1, D), F32)),
        compiler_params=_params("arbitrary", "arbitrary"),
    )(dy, x, gain, q, kv, w_q, w_o)


def _final(x, gain, target):
    T, D = x.shape
    tm = _tile(T, 512)

    def body(x_ref, gain_ref, tgt_ref, dx_ref, dgain_ref, loss_ref):
        i = pl.program_id(0)

        @pl.when(i == 0)
        def _():
            dgain_ref[...] = jnp.zeros_like(dgain_ref)
            loss_ref[...] = jnp.zeros_like(loss_ref)

        r, xhat = _rms(x_ref[...])
        err = xhat * gain_ref[...] - tgt_ref[...]
        loss_ref[...] += 0.5 * jnp.sum(jnp.mean(err * err, axis=-1, keepdims=True), axis=0, keepdims=True)
        dy = err * (1.0 / D)
        dgain_ref[...] += jnp.sum(dy * xhat, axis=0, keepdims=True)
        dx_ref[...] = _rms_bwd(dy, gain_ref[...], r, xhat)

    row = lambda i: (i, 0)
    one = pl.BlockSpec((1, D), lambda i: (0, 0))
    return pl.pallas_call(
        body, name="final_loss", grid=(T // tm,),
        in_specs=[pl.BlockSpec((tm, D), row), one, pl.BlockSpec((tm, D), row)],
        out_specs=(pl.BlockSpec((tm, D), row), one, pl.BlockSpec((8, LANES), lambda i: (0, 0))),
        out_shape=(jax.ShapeDtypeStruct((T, D), F32), jax.ShapeDtypeStruct((1, D), F32),
                   jax.ShapeDtypeStruct((8, LANES), F32)),
        compiler_params=_params("arbitrary"),
    )(x, gain, target)


def _adamw(gparts, w, m, v, name):
    R, C = w.shape
    tr = _tile(R, 256)

    def body(gp_ref, w_ref, m_ref, v_ref, g_ref, d_ref, nm_ref, nv_ref):
        g = gp_ref[0].astype(F32)
        for s in range(1, N_DEV):
            g = g + gp_ref[s].astype(F32)
        nm = ADAM_B1 * m_ref[...] + (1.0 - ADAM_B1) * g
        nv = ADAM_B2 * v_ref[...] + (1.0 - ADAM_B2) * (g * g)
        m_hat = nm / (1.0 - ADAM_B1 ** ADAM_STEP)
        v_hat = nv / (1.0 - ADAM_B2 ** ADAM_STEP)
        g_ref[...] = g
        nm_ref[...] = nm
        nv_ref[...] = nv
        d_ref[...] = -ADAM_LR * (m_hat / (jnp.sqrt(v_hat) + ADAM_EPS) + ADAM_WD * w_ref[...])

    spec = pl.BlockSpec((tr, C), lambda i: (i, 0))
    shp = jax.ShapeDtypeStruct((R, C), F32)
    return pl.pallas_call(
        body, name=name, grid=(R // tr,),
        in_specs=[pl.BlockSpec((N_DEV, tr, C), lambda i: (0, i, 0)), spec, spec, spec],
        out_specs=(spec, spec, spec, spec), out_shape=(shp, shp, shp, shp),
        compiler_params=_params("arbitrary"),
    )(gparts, w, m, v)


def _rows128(a, rows):
    a = a.reshape(-1, LANES)
    return jnp.pad(a, ((0, rows - a.shape[0]), (0, 0)))


def kernel(x, mem, ffn1_norm, ffn1_w_gate, ffn1_w_up, ffn1_w_down, mix_norm, w_in, w_pool, pool_scale, w_out, mem_q_norm, mem_kv_norm, mem_w_q, mem_w_kv, mem_w_o, ffn2_norm, ffn2_w_gate, ffn2_w_up, ffn2_w_down, final_norm, loss_target, m_ffn1_norm, m_ffn1_w_gate, m_ffn1_w_up, m_ffn1_w_down, m_mix_norm, m_w_in, m_w_pool, m_pool_scale, m_w_out, m_mem_q_norm, m_mem_kv_norm, m_mem_w_q, m_mem_w_kv, m_mem_w_o, m_ffn2_norm, m_ffn2_w_gate, m_ffn2_w_up, m_ffn2_w_down, m_final_norm, v_ffn1_norm, v_ffn1_w_gate, v_ffn1_w_up, v_ffn1_w_down, v_mix_norm, v_w_in, v_w_pool, v_pool_scale, v_w_out, v_mem_q_norm, v_mem_kv_norm, v_mem_w_q, v_mem_w_kv, v_mem_w_o, v_ffn2_norm, v_ffn2_w_gate, v_ffn2_w_up, v_ffn2_w_down, v_final_norm):
    B, S, D = x.shape
    T = B * S
    x0 = x.reshape(T, D)
    target = loss_target.reshape(T, D)
    final_gain = final_norm.reshape(1, D)

    big = dict(
        g1=ffn1_w_gate[0].T, u1=ffn1_w_up[0].T, d1=ffn1_w_down[0],
        g2=ffn2_w_gate[0].T, u2=ffn2_w_up[0].T, d2=ffn2_w_down[0],
        w_in=w_in[0], w_out=w_out[0], w_q=mem_w_q[0], w_kv=mem_w_kv[0], w_o=mem_w_o[0])
    names = list(big)
    shard = {k: big[k].astype(BF16) for k in names}
    wp = w_pool[0].astype(BF16)
    full, ffn_w = {}, {}

    def gathered(keys, arrs):
        full.update(zip(keys, arrs))
        ffn_w.update({k: full[k].reshape(-1, D) for k in keys if k[0] in "gud"})

    first, mid, late = ("g1", "u1", "d1"), ("w_in", "w_out", "w_q", "w_kv", "w_o"), ("g2", "u2", "d2")
    gathered(first, _gather_two_level([shard[k] for k in first], "gather_ffn1"))
    (x1, hn1, gg1, uu1), got = _ffn_fwd(x0, ffn1_norm, ffn_w["g1"], ffn_w["u1"], ffn_w["d1"], "ffn1_fwd",
                                        comm=([shard[k] for k in mid], True))
    gathered(mid, got)
    hn2, qkv, u = _mix_in_fwd(x1, mix_norm, full["w_in"])
    qkv3 = qkv.reshape(B, S, -1)
    (o_sb,), got = _sb_fwd(qkv3, B, S, comm=([shard[k] for k in late], True))
    gathered(late, got)
    pooled = _pool_fwd(u.reshape(B, S, -1), B, S).reshape(T, -1)
    x2, mixed = _mix_out_fwd(x1, o_sb.reshape(T, -1), pooled, wp, pool_scale, full["w_out"])
    memn, kv = _mem_kv_fwd(mem, mem_kv_norm, full["w_kv"])
    x3, hq, q, ocat = _cross_fwd(x2, mem_q_norm, kv, full["w_q"], full["w_o"], B, S)
    (x4, hn4, gg2, uu2), _ = _ffn_fwd(x3, ffn2_norm, ffn_w["g2"], ffn_w["u2"], ffn_w["d2"], "ffn2_fwd")
    dx4, d_final, loss_part = _final(x4, final_gain, target)

    slab = lambda k: grads[k].reshape((N_DEV, -1) + grads[k].shape[-1:])
    got = {}
    (dx3, a2, dg2, du2, dyh2, d_ffn2), _ = _ffn_bwd(dx4, x3, ffn2_norm, gg2, uu2, ffn_w["g2"], ffn_w["u2"],
                                                   ffn_w["d2"], "ffn2_bwd")
    grads = dict(g2=_wgrad(dg2, hn4, "dw_gate2"), u2=_wgrad(du2, hn4, "dw_up2"), d2=_wgrad(a2, dyh2, "dw_down2"))
    dx2, dx3b, dqb, dkv, d_q = _cross_bwd(dx3, x2, mem_q_norm, q, kv, full["w_q"], full["w_o"], B, S)
    grads["w_o"] = _wgrad(ocat, dx3b, "dw_o")
    grads["w_q"] = _wgrad(hq, dqb, "dw_q")
    dkvb, d_kv = _mem_kv_bwd(dkv, mem, full["w_kv"])
    grads["w_kv"] = _wgrad(memn, dkvb, "dw_kv", col_slab=full["w_kv"].shape[2])
    dx2b, do_sb, dpooled, d_wpool, d_ps = _mix_out_bwd(dx2, pooled, wp, pool_scale, full["w_out"])
    grads["w_out"] = _wgrad(mixed, dx2b, "dw_out")
    du = _pool_bwd(dpooled.reshape(B, S, -1), B, S).reshape(T, -1)
    early = ("g2", "u2", "d2", "w_o", "w_q", "w_kv", "w_out")
    (dq, dk, dv), res = _sb_bwd(qkv3, do_sb.reshape(B, S, -1), B, S, comm=([slab(k) for k in early], False))
    got.update(zip(early, res))
    dx1, dproj, d_mix = _mix_in_bwd(dx2, dq.reshape(T, -1), dk.reshape(T, -1), dv.reshape(T, -1), du,
                                    x1, mix_norm, full["w_in"])
    grads["w_in"] = _wgrad(hn2, dproj, "dw_in", col_slab=full["w_in"].shape[2])
    (dx0, a1, dg1, du1, dyh1, d_ffn1), _ = _ffn_bwd(dx1, x0, ffn1_norm, gg1, uu1, ffn_w["g1"], ffn_w["u1"],
                                                   ffn_w["d1"], "ffn1_bwd")

    small = [("ffn1_norm", d_ffn1, ffn1_norm, m_ffn1_norm, v_ffn1_norm),
             ("mix_norm", d_mix, mix_norm, m_mix_norm, v_mix_norm),
             ("w_pool", d_wpool, w_pool, m_w_pool, v_w_pool),
             ("pool_scale", d_ps, pool_scale, m_pool_scale, v_pool_scale),
             ("mem_q_norm", d_q, mem_q_norm, m_mem_q_norm, v_mem_q_norm),
             ("mem_kv_norm", d_kv, mem_kv_norm, m_mem_kv_norm, v_mem_kv_norm),
             ("ffn2_norm", d_ffn2, ffn2_norm, m_ffn2_norm, v_ffn2_norm),
             ("final_norm", d_final, final_norm, m_final_norm, v_final_norm)]
    rows = [max(8, t[2].size // LANES) for t in small]
    pack = lambda idx: jnp.concatenate([_rows128(t[idx], r) for t, r in zip(small, rows)]
                                       + ([loss_part] if idx == 1 else [jnp.zeros((8, LANES), F32)]))
    grads["g1"], (small_parts, got["w_in"]) = _wgrad(dg1, hn1, "dw_gate1",
                                                     comm=([pack(1), slab("w_in")], [True, False]))
    grads["u1"], (got["g1"],) = _wgrad(du1, hn1, "dw_up1", comm=([slab("g1")], False))
    grads["d1"], (got["u1"],) = _wgrad(a1, dyh1, "dw_down1", comm=([slab("u1")], False))
    got["d1"] = _exchange([slab("d1")], False, "scatter_last")[0]

    state = dict(
        g1=(ffn1_w_gate, m_ffn1_w_gate, v_ffn1_w_gate), u1=(ffn1_w_up, m_ffn1_w_up, v_ffn1_w_up),
        d1=(ffn1_w_down, m_ffn1_w_down, v_ffn1_w_down), g2=(ffn2_w_gate, m_ffn2_w_gate, v_ffn2_w_gate),
        u2=(ffn2_w_up, m_ffn2_w_up, v_ffn2_w_up), d2=(ffn2_w_down, m_ffn2_w_down, v_ffn2_w_down),
        w_in=(w_in, m_w_in, v_w_in), w_out=(w_out, m_w_out, v_w_out), w_q=(mem_w_q, m_mem_w_q, v_mem_w_q),
        w_kv=(mem_w_kv, m_mem_w_kv, v_mem_w_kv), w_o=(mem_w_o, m_mem_w_o, v_mem_w_o))
    big_out = {}
    for k in names:
        parts = got[k]
        if k in ("g1", "u1", "g2", "u2"):
            parts = jnp.swapaxes(parts, 1, 2)
        w_, m_, v_ = (t[0] for t in state[k])
        big_out[k] = [t[None] for t in _adamw(parts, w_, m_, v_, "adamw_" + k)]

    small_res = _adamw(small_parts, pack(2), pack(3), pack(4), "adamw_small")
    small_out, off = {}, 0
    for t, r in zip(small, rows):
        n = t[2].size // LANES
        small_out[t[0]] = [res[off:off + n].reshape(t[2].shape) for res in small_res]
        off += r
    loss = small_res[0][off, 0]

    order = [("ffn1_norm", None), ("ffn1_w_gate", "g1"), ("ffn1_w_up", "u1"), ("ffn1_w_down", "d1"),
             ("mix_norm", None), ("w_in", "w_in"), ("w_pool", None), ("pool_scale", None), ("w_out", "w_out"),
             ("mem_q_norm", None), ("mem_kv_norm", None), ("mem_w_q", "w_q"), ("mem_w_kv", "w_kv"),
             ("mem_w_o", "w_o"), ("ffn2_norm", None), ("ffn2_w_gate", "g2"), ("ffn2_w_up", "u2"),
             ("ffn2_w_down", "d2"), ("final_norm", None)]
    res = [loss, dx0.reshape(B, S, D)]
    for which in range(4):
        for name, key in order:
            res.append(big_out[key][which] if key else small_out[name][which])
    return tuple(res)
```

```python
import functools

import jax
import jax.numpy as jnp
from jax import lax
from jax.experimental import pallas as pl
from jax.experimental.pallas import tpu as pltpu

F32 = jnp.float32
BF16 = jnp.bfloat16

N_DEV = 8
EPS = 1e-6
SB_HEAD_DIM = 64
LANES = 128
POOL_WINDOWS = (2, 4, 8, 16)
POOL_GROUP_DIM = 128
MEM_HEADS = 4
FFN_RESIDUAL_WEIGHT = 0.5
ADAM_LR = 0.001
ADAM_B1 = 0.9
ADAM_B2 = 0.999
ADAM_EPS = 1e-08
ADAM_WD = 0.01
ADAM_STEP = 10
VMEM_LIMIT = 56 * 1024 * 1024

MESH_ID = pl.DeviceIdType.MESH


def _params(*sem):
    return pltpu.CompilerParams(dimension_semantics=sem, vmem_limit_bytes=VMEM_LIMIT)


def _tile(n, pref):
    if n <= pref:
        return n
    t = pref - pref % 8
    while n % t:
        t -= 8
    return t


def _mm(a, b):
    return jnp.dot(a, b, preferred_element_type=F32)


def _mm_nt(a, b):
    return lax.dot_general(a, b, (((1,), (1,)), ((), ())), preferred_element_type=F32)


def _mm_tn(a, b):
    return lax.dot_general(a, b, (((0,), (0,)), ((), ())), preferred_element_type=F32)


def _rms(xv):
    r = lax.rsqrt(jnp.mean(xv * xv, axis=-1, keepdims=True) + EPS)
    return r, xv * r


def _rms_bwd(dhn, gain, r, xhat):
    dxh = dhn * gain
    return r * (dxh - xhat * jnp.mean(dxh * xhat, axis=-1, keepdims=True))


def _sigmoid(z):
    return 0.5 * jnp.tanh(0.5 * z) + 0.5


def _flags(arrs, gather):
    return [gather] * len(arrs) if isinstance(gather, bool) else list(gather)


def _comm_shapes(arrs, gather):
    return tuple(jax.ShapeDtypeStruct(((N_DEV,) + tuple(a.shape)) if f else tuple(a.shape), a.dtype)
                 for a, f in zip(arrs, _flags(arrs, gather)))


def _comm_start(ins, outs, sems, gather):
    send_sems, recv_sems, local_sems = sems
    gather = _flags(ins, gather)
    x, y, c = lax.axis_index("x"), lax.axis_index("y"), lax.axis_index("c")
    me = 4 * x + 2 * y + c
    for i in range(len(ins)):
        src = ins[i] if gather[i] else ins[i].at[me]
        pltpu.make_async_copy(src, outs[i].at[me], local_sems.at[i]).start()
    for k in range(1, N_DEV):
        px = 1 - x if k & 4 else x
        py = 1 - y if k & 2 else y
        pc = 1 - c if k & 1 else c
        peer = 4 * px + 2 * py + pc
        for i in range(len(ins)):
            src = ins[i] if gather[i] else ins[i].at[peer]
            pltpu.make_async_remote_copy(
                src_ref=src, dst_ref=outs[i].at[me],
                send_sem=send_sems.at[i], recv_sem=recv_sems.at[i],
                device_id=(px, py, pc), device_id_type=MESH_ID).start()


def _comm_wait(ins, outs, sems, gather):
    send_sems, recv_sems, local_sems = sems
    gather = _flags(ins, gather)
    x, y, c = lax.axis_index("x"), lax.axis_index("y"), lax.axis_index("c")
    me = 4 * x + 2 * y + c
    for i in range(len(ins)):
        seven = outs[i].at[pl.ds(0, N_DEV - 1)]
        done = pltpu.make_async_remote_copy(
            src_ref=seven, dst_ref=seven,
            send_sem=send_sems.at[i], recv_sem=recv_sems.at[i],
            device_id=(x, y, c), device_id_type=MESH_ID)
        done.wait_send()
        done.wait_recv()
        src = ins[i] if gather[i] else ins[i].at[me]
        pltpu.make_async_copy(src, outs[i].at[me], local_sems.at[i]).wait()


def _comm_sems(n):
    return [pltpu.SemaphoreType.DMA((n,)) for _ in range(3)]


def _exchange(arrs, gather, name):
    n = len(arrs)

    def body(*refs):
        ins, outs, sems = refs[:n], refs[n:2 * n], refs[2 * n:]
        _comm_start(ins, outs, sems, gather)
        _comm_wait(ins, outs, sems, gather)

    any_spec = pl.BlockSpec(memory_space=pl.ANY)
    outs = pl.pallas_call(
        body, name=name, out_shape=_comm_shapes(arrs, gather),
        in_specs=[any_spec] * n, out_specs=tuple([any_spec] * n), scratch_shapes=_comm_sems(n),
    )(*arrs)
    return list(outs)


def _gather_two_level(arrs, name):
    n = len(arrs)

    def body(*refs):
        ins, outs = refs[:n], refs[n:2 * n]
        send_sems, recv_sems, local_sems = refs[2 * n:]
        x, y, c = lax.axis_index("x"), lax.axis_index("y"), lax.axis_index("c")
        me, sibling = (x, y, c), (x, y, 1 - c)
        chips = [(1 - x, y), (x, 1 - y), (1 - x, 1 - y)]

        def copy(i, k, block, to, own=False):
            slab = outs[i].at[4 * block[0] + 2 * block[1] + block[2]]
            return pltpu.make_async_remote_copy(
                src_ref=ins[i] if own else slab, dst_ref=slab,
                send_sem=send_sems.at[i, k], recv_sem=recv_sems.at[i, k],
                device_id=to, device_id_type=MESH_ID)

        mine = [pltpu.make_async_copy(ins[i], outs[i].at[4 * x + 2 * y + c], local_sems.at[i]) for i in range(n)]
        first = [copy(i, 0, me, sibling, own=True) for i in range(n)]
        first += [copy(i, 1 + j, me, (*chip, c), own=True) for j, chip in enumerate(chips) for i in range(n)]
        for cp in mine + first:
            cp.start()
        passed = []
        for j, chip in enumerate(chips):
            for i in range(n):
                copy(i, 1 + j, (*chip, c), me).wait_recv()
                passed.append(copy(i, 4 + j, (*chip, c), sibling))
                passed[-1].start()
        for i in range(n):
            copy(i, 0, sibling, me).wait_recv()
            for j, chip in enumerate(chips):
                copy(i, 4 + j, (*chip, 1 - c), me).wait_recv()
        for cp in first + passed:
            cp.wait_send()
        for cp in mine:
            cp.wait()

    any_spec = pl.BlockSpec(memory_space=pl.ANY)
    outs = pl.pallas_call(
        body, name=name, out_shape=_comm_shapes(arrs, True),
        in_specs=[any_spec] * n, out_specs=tuple([any_spec] * n),
        scratch_shapes=[pltpu.SemaphoreType.DMA((n, N_DEV - 1)), pltpu.SemaphoreType.DMA((n, N_DEV - 1)),
                        pltpu.SemaphoreType.DMA((n,))],
    )(*arrs)
    return list(outs)


def _call(body, *, name, grid, in_specs, out_specs, out_shape, args, scratch=(), comm=None):
    sem = ("arbitrary",) * len(grid)
    if comm is None:
        res = pl.pallas_call(body, name=name, grid=grid, in_specs=list(in_specs), out_specs=tuple(out_specs),
                             out_shape=tuple(out_shape), scratch_shapes=list(scratch),
                             compiler_params=_params(*sem))(*args)
        return tuple(res), []
    arrs, gather = comm
    n, n_in, n_out, n_sc = len(arrs), len(args), len(out_shape), len(scratch)

    def wrapped(*refs):
        ins, cin = refs[:n_in], refs[n_in:n_in + n]
        outs, cout = refs[n_in + n:n_in + n + n_out], refs[n_in + n + n_out:n_in + 2 * n + n_out]
        sc, sems = refs[n_in + 2 * n + n_out:n_in + 2 * n + n_out + n_sc], refs[n_in + 2 * n + n_out + n_sc:]
        ids = [pl.program_id(a) for a in range(len(grid))]
        first = functools.reduce(jnp.logical_and, [i == 0 for i in ids])
        last = functools.reduce(jnp.logical_and, [i == g - 1 for i, g in zip(ids, grid)])

        @pl.when(first)
        def _():
            _comm_start(cin, cout, sems, gather)

        body(*ins, *outs, *sc)

        @pl.when(last)
        def _():
            _comm_wait(cin, cout, sems, gather)

    any_spec = pl.BlockSpec(memory_space=pl.ANY)
    res = pl.pallas_call(
        wrapped, name=name, grid=grid, in_specs=list(in_specs) + [any_spec] * n,
        out_specs=tuple(out_specs) + (any_spec,) * n, out_shape=tuple(out_shape) + _comm_shapes(arrs, gather),
        scratch_shapes=list(scratch) + _comm_sems(n), compiler_params=_params(*sem))(*args, *arrs)
    return tuple(res[:n_out]), list(res[n_out:])


def _ffn_fwd(x, gain, wgt, wut, wd, name, comm=None, head=None):
    T, D = x.shape
    F = wd.shape[0]
    tm, tf = _tile(T, 1024), _tile(F, 256)
    nj = F // tf
    rc = _tile(tm, 512)

    def body(*refs):
        if head:
            (x_ref, gain_ref, wg_ref, wu_ref, wd_ref, fgain_ref, tgt_ref,
             out_ref, hn_ref, g_ref, u_ref, dfgain_ref, loss_ref, acc, hn_s) = refs
        else:
            x_ref, gain_ref, wg_ref, wu_ref, wd_ref, out_ref, hn_ref, g_ref, u_ref, acc, hn_s = refs
        i, j = pl.program_id(0), pl.program_id(1)

        if head:
            @pl.when((i == 0) & (j == 0))
            def _():
                dfgain_ref[...] = jnp.zeros_like(dfgain_ref)
                loss_ref[...] = jnp.zeros_like(loss_ref)

        @pl.when(j == 0)
        def _():
            _, xhat = _rms(x_ref[...])
            hn = (xhat * gain_ref[...]).astype(BF16)
            hn_s[...] = hn
            hn_ref[...] = hn
            acc[...] = jnp.zeros_like(acc)

        for r0 in range(0, tm, rc):
            rows = slice(r0, r0 + rc)
            hn = hn_s[rows, :]
            g = _mm_nt(hn, wg_ref[...])
            u = _mm_nt(hn, wu_ref[...])
            g_ref[rows, :] = g.astype(BF16)
            u_ref[rows, :] = u.astype(BF16)
            a = (g * _sigmoid(g) * u).astype(BF16)
            acc[rows, :] += _mm(a, wd_ref[...])

        @pl.when(j == nj - 1)
        def _():
            y = x_ref[...] + FFN_RESIDUAL_WEIGHT * acc[...]
            if head:
                r, yhat = _rms(y)
                err = yhat * fgain_ref[...] - tgt_ref[...]
                loss_ref[...] += 0.5 * jnp.sum(jnp.mean(err * err, axis=-1, keepdims=True), axis=0, keepdims=True)
                dy = err * (1.0 / D)
                dfgain_ref[...] += jnp.sum(dy * yhat, axis=0, keepdims=True)
                out_ref[...] = _rms_bwd(dy, fgain_ref[...], r, yhat)
            else:
                out_ref[...] = y

    row = lambda i, j: (i, 0)
    one = pl.BlockSpec((1, D), lambda i, j: (0, 0))
    wspec = pl.BlockSpec((tf, D), lambda i, j: (j, 0))
    in_specs = [pl.BlockSpec((tm, D), row), one, wspec, wspec, wspec]
    out_specs = [pl.BlockSpec((tm, D), row), pl.BlockSpec((tm, D), row),
                 pl.BlockSpec((tm, tf), lambda i, j: (i, j)), pl.BlockSpec((tm, tf), lambda i, j: (i, j))]
    out_shape = [jax.ShapeDtypeStruct((T, D), F32), jax.ShapeDtypeStruct((T, D), BF16),
                 jax.ShapeDtypeStruct((T, F), BF16), jax.ShapeDtypeStruct((T, F), BF16)]
    args = (x, gain, wgt, wut, wd)
    if head:
        in_specs += [one, pl.BlockSpec((tm, D), row)]
        out_specs += [one, pl.BlockSpec((8, LANES), lambda i, j: (0, 0))]
        out_shape += [jax.ShapeDtypeStruct((1, D), F32), jax.ShapeDtypeStruct((8, LANES), F32)]
        args += tuple(head)
    return _call(
        body, name=name, grid=(T // tm, nj), comm=comm, in_specs=in_specs, out_specs=out_specs,
        out_shape=out_shape, scratch=[pltpu.VMEM((tm, D), F32), pltpu.VMEM((tm, D), BF16)], args=args)


def _ffn_bwd(dy, x, gain, g, u, wgt, wut, wd, name, comm=None):
    T, D = x.shape
    F = wd.shape[0]
    tm, tf = _tile(T, 1024), _tile(F, 256)
    nj = F // tf
    rc = _tile(tm, 512)

    def body(dy_ref, x_ref, gain_ref, g_ref, u_ref, wg_ref, wu_ref, wd_ref,
             dx_ref, a_ref, dg_ref, du_ref, dyh_ref, dgain_ref, acc, dyh_s):
        i, j = pl.program_id(0), pl.program_id(1)

        @pl.when(j == 0)
        def _():
            dyh = (FFN_RESIDUAL_WEIGHT * dy_ref[...]).astype(BF16)
            dyh_s[...] = dyh
            dyh_ref[...] = dyh
            acc[...] = jnp.zeros_like(acc)

        @pl.when((i == 0) & (j == 0))
        def _():
            dgain_ref[...] = jnp.zeros_like(dgain_ref)

        for r0 in range(0, tm, rc):
            rows = slice(r0, r0 + rc)
            gv = g_ref[rows, :].astype(F32)
            uv = u_ref[rows, :].astype(F32)
            da = _mm_nt(dyh_s[rows, :], wd_ref[...])
            sig = _sigmoid(gv)
            s = gv * sig
            a_ref[rows, :] = (s * uv).astype(BF16)
            dg = (da * uv * (sig + s * (1.0 - sig))).astype(BF16)
            du = (da * s).astype(BF16)
            dg_ref[rows, :] = dg
            du_ref[rows, :] = du
            acc[rows, :] += _mm(dg, wg_ref[...]) + _mm(du, wu_ref[...])

        @pl.when(j == nj - 1)
        def _():
            r, xhat = _rms(x_ref[...])
            dhn = acc[...]
            dgain_ref[...] += jnp.sum(dhn * xhat, axis=0, keepdims=True)
            dx_ref[...] = dy_ref[...] + _rms_bwd(dhn, gain_ref[...], r, xhat)

    row = lambda i, j: (i, 0)
    tile = lambda i, j: (i, j)
    wspec = pl.BlockSpec((tf, D), lambda i, j: (j, 0))
    one = pl.BlockSpec((1, D), lambda i, j: (0, 0))
    return _call(
        body, name=name, grid=(T // tm, nj), comm=comm,
        in_specs=[pl.BlockSpec((tm, D), row), pl.BlockSpec((tm, D), row), one,
                  pl.BlockSpec((tm, tf), tile), pl.BlockSpec((tm, tf), tile), wspec, wspec, wspec],
        out_specs=(pl.BlockSpec((tm, D), row), pl.BlockSpec((tm, tf), tile), pl.BlockSpec((tm, tf), tile),
                   pl.BlockSpec((tm, tf), tile), pl.BlockSpec((tm, D), row), one),
        out_shape=(jax.ShapeDtypeStruct((T, D), F32), jax.ShapeDtypeStruct((T, F), BF16),
                   jax.ShapeDtypeStruct((T, F), BF16), jax.ShapeDtypeStruct((T, F), BF16),
                   jax.ShapeDtypeStruct((T, D), BF16), jax.ShapeDtypeStruct((1, D), F32)),
        scratch=[pltpu.VMEM((tm, D), F32), pltpu.VMEM((tm, D), BF16)],
        args=(dy, x, gain, g, u, wgt, wut, wd))


def _wgrad(a, b, name, col_slab=None, comm=None):
    T, M = a.shape
    N = b.shape[1]
    tmm = M if M <= 1024 else _tile(M, 1408)
    tn = _tile(N, 1024)
    tk = _tile(T, 1024)
    nk = T // tk
    per = tn // col_slab if col_slab else 0

    def body(a_ref, b_ref, out_ref, acc):
        k = pl.program_id(2)

        @pl.when(k == 0)
        def _():
            acc[...] = jnp.zeros_like(acc)

        acc[...] += _mm_tn(a_ref[...], b_ref[...])

        @pl.when(k == nk - 1)
        def _():
            if col_slab:
                for s in range(per):
                    out_ref[s] = acc[:, s * col_slab:(s + 1) * col_slab].astype(BF16)
            else:
                out_ref[...] = acc[...].astype(BF16)

    if col_slab:
        out_spec = pl.BlockSpec((per, tmm, col_slab), lambda m, n, k: (n, m, 0))
        out_shape = jax.ShapeDtypeStruct((N // col_slab, M, col_slab), BF16)
    else:
        out_spec = pl.BlockSpec((tmm, tn), lambda m, n, k: (m, n))
        out_shape = jax.ShapeDtypeStruct((M, N), BF16)
    (out,), got = _call(
        body, name=name, grid=(M // tmm, N // tn, nk), comm=comm,
        in_specs=[pl.BlockSpec((tk, tmm), lambda m, n, k: (k, m)), pl.BlockSpec((tk, tn), lambda m, n, k: (k, n))],
        out_specs=(out_spec,), out_shape=(out_shape,),
        scratch=[pltpu.VMEM((tmm, tn), F32)], args=(a, b))
    return (out, got) if comm else out


def _mix_in_fwd(x, gain, w_in):
    T, D = x.shape
    cs = w_in.shape[2]
    n_qkv = 3 * (N_DEV // 4)
    tm = _tile(T, 1024)

    def body(x_ref, gain_ref, w_ref, hn_ref, qkv_ref, u_ref):
        _, xhat = _rms(x_ref[...])
        hn = (xhat * gain_ref[...]).astype(BF16)
        hn_ref[...] = hn
        for b in range(N_DEV):
            p = _mm(hn, w_ref[b])
            if b < n_qkv:
                qkv_ref[:, b * cs:(b + 1) * cs] = p.astype(BF16)
            else:
                u_ref[:, (b - n_qkv) * cs:(b - n_qkv + 1) * cs] = p

    row = lambda i: (i, 0)
    return pl.pallas_call(
        body, name="mix_in_fwd", grid=(T // tm,),
        in_specs=[pl.BlockSpec((tm, D), row), pl.BlockSpec((1, D), lambda i: (0, 0)),
                  pl.BlockSpec((N_DEV, D, cs), lambda i: (0, 0, 0))],
        out_specs=(pl.BlockSpec((tm, D), row), pl.BlockSpec((tm, n_qkv * cs), row),
                   pl.BlockSpec((tm, (N_DEV - n_qkv) * cs), row)),
        out_shape=(jax.ShapeDtypeStruct((T, D), BF16), jax.ShapeDtypeStruct((T, n_qkv * cs), BF16),
                   jax.ShapeDtypeStruct((T, (N_DEV - n_qkv) * cs), F32)),
        compiler_params=_params("arbitrary"),
    )(x, gain, w_in)


def _mix_in_bwd(dres, dq, dk, dv, du, x, gain, w_in):
    T, D = x.shape
    cs = w_in.shape[2]
    W = dq.shape[1]
    per = W // cs
    tm = _tile(T, 512)

    def body(dres_ref, dq_ref, dk_ref, dv_ref, du_ref, x_ref, gain_ref, w_ref, dx_ref, dproj_ref, dgain_ref):
        i = pl.program_id(0)

        @pl.when(i == 0)
        def _():
            dgain_ref[...] = jnp.zeros_like(dgain_ref)

        dhn = jnp.zeros((tm, D), F32)
        for part, ref in enumerate((dq_ref, dk_ref, dv_ref, du_ref)):
            for h in range(per):
                b = part * per + h
                d = ref[:, h * cs:(h + 1) * cs]
                dproj_ref[:, b * cs:(b + 1) * cs] = d
                dhn = dhn + _mm_nt(d, w_ref[b])
        r, xhat = _rms(x_ref[...])
        dgain_ref[...] += jnp.sum(dhn * xhat, axis=0, keepdims=True)
        dx_ref[...] = dres_ref[...] + _rms_bwd(dhn, gain_ref[...], r, xhat)

    row = lambda i: (i, 0)
    one = pl.BlockSpec((1, D), lambda i: (0, 0))
    part = pl.BlockSpec((tm, W), row)
    return pl.pallas_call(
        body, name="mix_in_bwd", grid=(T // tm,),
        in_specs=[pl.BlockSpec((tm, D), row), part, part, part, part, pl.BlockSpec((tm, D), row), one,
                  pl.BlockSpec((N_DEV, D, cs), lambda i: (0, 0, 0))],
        out_specs=(pl.BlockSpec((tm, D), row), pl.BlockSpec((tm, 4 * W), row), one),
        out_shape=(jax.ShapeDtypeStruct((T, D), F32), jax.ShapeDtypeStruct((T, 4 * W), BF16),
                   jax.ShapeDtypeStruct((1, D), F32)),
        compiler_params=_params("arbitrary"),
    )(dres, dq, dk, dv, du, x, gain, w_in)


SB_PAIRS_PER_PROGRAM = 2
LOG2_E = 1.4426950408889634
EXP2_CLAMP = 126.0


def _neg_log2_sigmoid(nz2):
    w = jnp.minimum(nz2, EXP2_CLAMP)
    return w, jnp.log2(1.0 + jnp.exp2(w))


SB_DEAD_LOG2 = -160.0


def _sb_live(rests):
    worst = functools.reduce(jnp.maximum, rests)
    return (jnp.max(worst) > SB_DEAD_LOG2).astype(jnp.int32)


def _split(v):
    hi = v.astype(BF16)
    return hi, (v - hi.astype(F32)).astype(BF16)


def _tri_sum(v, tri):
    hi, lo = _split(v)
    return _mm(hi, tri) + _mm(lo, tri)


def _sb_fwd(qkv, B, S, comm=None):
    W = qkv.shape[2] // 3
    n_pair = W // LANES
    bq = _tile(S, 256)
    nq = S // bq
    hp = SB_PAIRS_PER_PROGRAM
    nscale2 = -(SB_HEAD_DIM ** -0.5) * LOG2_E

    def body(q_ref, k_ref, v_ref, o_ref):
        lane = lax.broadcasted_iota(jnp.int32, (1, LANES), 1)
        head0 = lane < SB_HEAD_DIM
        rr = lax.broadcasted_iota(jnp.int32, (bq, bq), 0)
        cc = lax.broadcasted_iota(jnp.int32, (bq, bq), 1)
        strict = cc < rr
        after = jnp.where(rr > cc, 1.0, 0.0).astype(BF16)

        def blocks(heads, ks, carries, diag):
            n = range(len(heads))
            keep = (lambda t: jnp.where(strict, t, 0.0)) if diag else (lambda t: t)
            z = [_mm_nt(qh, k_ref[ks, cols]) for qh, cols in heads]
            wl = [_neg_log2_sigmoid(z[h] * nscale2) for h in n]
            lr = [keep(wl[h][0] - wl[h][1]) for h in n]
            parts = [_split(lr[h]) for h in n]
            suf = [_mm(parts[h][0], after) + _mm(parts[h][1], after) for h in n]
            a = [keep(jnp.exp2(suf[h] + carries[h][1] - wl[h][1])).astype(BF16) for h in n]
            o = [carries[h][0] + _mm(a[h], v_ref[ks, heads[h][1]]) for h in n]
            return tuple((o[h], carries[h][1] + (suf[h][:, :1] + lr[h][:, :1])) for h in n)

        def q_tile(i, _):
            qs = pl.ds(pl.multiple_of(i * bq, bq), bq)
            heads = []
            for pr in range(hp):
                cols = slice(pr * LANES, (pr + 1) * LANES)
                qv = q_ref[qs, cols]
                heads += [(jnp.where(head0, qv, jnp.zeros_like(qv)), cols),
                          (jnp.where(head0, jnp.zeros_like(qv), qv), cols)]
            zero = (jnp.zeros((bq, LANES), F32), jnp.zeros((bq, 1), F32))
            init = blocks(heads, qs, (zero,) * len(heads), True)

            def left(st):
                t, _, cr = st
                ks = pl.ds(pl.multiple_of((i - 1 - t) * bq, bq), bq)
                cr = blocks(heads, ks, cr, False)
                return t + 1, _sb_live([c for _, c in cr]), cr

            _, _, res = lax.while_loop(lambda st: jnp.logical_and(st[0] < i, st[1] > 0), left,
                                       (jnp.int32(0), _sb_live([c for _, c in init]), init))
            for pr in range(hp):
                o_ref[qs, heads[2 * pr][1]] = jnp.where(head0, res[2 * pr][0], res[2 * pr + 1][0]).astype(BF16)
            return 0

        lax.fori_loop(0, nq, q_tile, 0)

    def col(off):
        return pl.BlockSpec((None, S, hp * LANES), lambda b, p: (b, 0, off + p))

    n_pair //= hp
    return _call(
        body, name="sb_fwd", grid=(B, n_pair), comm=comm,
        in_specs=[col(0), col(n_pair), col(2 * n_pair)],
        out_specs=(col(0),),
        out_shape=(jax.ShapeDtypeStruct((B, S, W), BF16),),
        args=(qkv, qkv, qkv))


def _sb_bwd(qkv, do, B, S, comm=None):
    W = qkv.shape[2] // 3
    n_pair = W // LANES
    bq = _tile(S, 256)
    nq = S // bq
    hp = SB_PAIRS_PER_PROGRAM
    scale = SB_HEAD_DIM ** -0.5
    nscale2 = -scale * LOG2_E

    def body(q_ref, k_ref, v_ref, do_ref, dq_ref, dk_ref, dv_ref, dk_s, dv_s, e_s, sg_s, a_s):
        lane = lax.broadcasted_iota(jnp.int32, (1, LANES), 1)
        head0 = lane < SB_HEAD_DIM
        rr = lax.broadcasted_iota(jnp.int32, (bq, bq), 0)
        cc = lax.broadcasted_iota(jnp.int32, (bq, bq), 1)
        strict = cc < rr
        after = jnp.where(rr > cc, 1.0, 0.0).astype(BF16)
        before = jnp.where(rr < cc, 1.0, 0.0).astype(BF16)
        dk_s[...] = jnp.zeros_like(dk_s)
        dv_s[...] = jnp.zeros_like(dv_s)

        def weights(heads, ks, kb, rests, diag):
            n = range(len(heads))
            keep = (lambda t: jnp.where(strict, t, 0.0)) if diag else (lambda t: t)
            z = [_mm_nt(heads[h][0], k_ref[ks, heads[h][2]]) for h in n]
            da = [_mm_nt(heads[h][1], v_ref[ks, heads[h][2]]) for h in n]
            wl = [_neg_log2_sigmoid(z[h] * nscale2) for h in n]
            lr = [keep(wl[h][0] - wl[h][1]) for h in n]
            parts = [_split(lr[h]) for h in n]
            suf = [_mm(parts[h][0], after) + _mm(parts[h][1], after) for h in n]
            a = [keep(jnp.exp2(suf[h] + rests[h] - wl[h][1])) for h in n]
            for h in n:
                a_s[h * nq + kb] = a[h].astype(BF16)
                e_s[h * nq + kb] = a[h] * da[h]
                sg_s[h * nq + kb] = jnp.exp2(-wl[h][1])
            return tuple(rests[h] + (suf[h][:, :1] + lr[h][:, :1]) for h in n)

        def grads(heads, ks, kb, carries, diag):
            n = range(len(heads))
            keep = (lambda t: jnp.where(strict, t, 0.0)) if diag else (lambda t: t)
            e = [e_s[h * nq + kb] for h in n]
            parts = [_split(e[h]) for h in n]
            pex = [_mm(parts[h][0], before) + _mm(parts[h][1], before) for h in n]
            dz = [keep(e[h] - sg_s[h * nq + kb] * (e[h] + pex[h] + carries[h][1])).astype(BF16) for h in n]
            dq = [carries[h][0] + _mm(dz[h], k_ref[ks, heads[h][2]]) for h in n]
            for h in n:
                dk_s[ks, heads[h][2]] += _mm_tn(dz[h], heads[h][0])
                dv_s[ks, heads[h][2]] += _mm_tn(a_s[h * nq + kb], heads[h][1])
            return tuple((dq[h], carries[h][1] + (pex[h][:, bq - 1:] + e[h][:, bq - 1:])) for h in n)

        def q_tile(i, _):
            qs = pl.ds(pl.multiple_of(i * bq, bq), bq)
            heads = []
            for pr in range(hp):
                cols = slice(pr * LANES, (pr + 1) * LANES)
                qv, dov = q_ref[qs, cols], do_ref[qs, cols]
                zq, zd = jnp.zeros_like(qv), jnp.zeros_like(dov)
                heads += [(jnp.where(head0, qv, zq), jnp.where(head0, dov, zd), cols),
                          (jnp.where(head0, zq, qv), jnp.where(head0, zd, dov), cols)]
            key_block = lambda kb: pl.ds(pl.multiple_of(kb * bq, bq), bq)
            rests = weights(heads, qs, i, (jnp.zeros((bq, 1), F32),) * len(heads), True)

            def left(st):
                t, _, rs = st
                rs = weights(heads, key_block(i - 1 - t), i - 1 - t, rs, False)
                return t + 1, _sb_live(rs), rs

            n_left, _, _ = lax.while_loop(lambda st: jnp.logical_and(st[0] < i, st[1] > 0), left,
                                          (jnp.int32(0), _sb_live(rests), rests))
            zero = (jnp.zeros((bq, LANES), F32), jnp.zeros((bq, 1), F32))
            res = lax.fori_loop(0, n_left, lambda t, cr: grads(heads, key_block(i - n_left + t), i - n_left + t, cr, False),
                                (zero,) * len(heads))
            res = grads(heads, qs, i, res, True)
            for pr in range(hp):
                dq = jnp.where(head0, res[2 * pr][0], res[2 * pr + 1][0])
                dq_ref[qs, heads[2 * pr][2]] = (dq * scale).astype(BF16)
            return 0

        lax.fori_loop(0, nq, q_tile, 0)
        dk_ref[...] = (dk_s[...] * scale).astype(BF16)
        dv_ref[...] = dv_s[...].astype(BF16)

    def col(off):
        return pl.BlockSpec((None, S, hp * LANES), lambda b, p: (b, 0, off + p))

    n_pair //= hp
    shp = jax.ShapeDtypeStruct((B, S, W), BF16)
    slots = 2 * hp * nq
    return _call(
        body, name="sb_bwd", grid=(B, n_pair), comm=comm,
        in_specs=[col(0), col(n_pair), col(2 * n_pair), col(0)],
        out_specs=(col(0), col(0), col(0)),
        out_shape=(shp, shp, shp),
        scratch=[pltpu.VMEM((S, hp * LANES), F32), pltpu.VMEM((S, hp * LANES), F32),
                 pltpu.VMEM((slots, bq, bq), F32), pltpu.VMEM((slots, bq, bq), F32),
                 pltpu.VMEM((slots, bq, bq), BF16)],
        args=(qkv, qkv, qkv, do))


def _pool_counts(S):
    t = lax.broadcasted_iota(jnp.int32, (S, 1), 0)
    return t, [jnp.minimum(t + 1, w).astype(F32) for w in POOL_WINDOWS]


def _pool_fwd(u, B, S):
    W = u.shape[2]

    def body(u_ref, out_ref):
        t, counts = _pool_counts(S)
        for gi, w in enumerate(POOL_WINDOWS):
            cols = slice(gi * POOL_GROUP_DIM, (gi + 1) * POOL_GROUP_DIM)
            ug = u_ref[:, cols]
            s, k = ug, 1
            while k < w:
                s = s + jnp.where(t >= k, pltpu.roll(s, k, axis=0), 0.0)
                k *= 2
            out_ref[:, cols] = (s / counts[gi] - ug).astype(BF16)

    spec = pl.BlockSpec((None, S, W), lambda b: (b, 0, 0))
    return pl.pallas_call(
        body, name="pool_fwd", grid=(B,), in_specs=[spec], out_specs=spec,
        out_shape=jax.ShapeDtypeStruct((B, S, W), BF16), compiler_params=_params("arbitrary"),
    )(u)


def _pool_bwd(dpooled, B, S):
    W = dpooled.shape[2]

    def body(d_ref, out_ref):
        t, counts = _pool_counts(S)
        for gi, w in enumerate(POOL_WINDOWS):
            cols = slice(gi * POOL_GROUP_DIM, (gi + 1) * POOL_GROUP_DIM)
            d = d_ref[:, cols]
            s, k = d / counts[gi], 1
            while k < w:
                s = s + jnp.where(t < S - k, pltpu.roll(s, S - k, axis=0), 0.0)
                k *= 2
            out_ref[:, cols] = (s - d).astype(BF16)

    spec = pl.BlockSpec((None, S, W), lambda b: (b, 0, 0))
    return pl.pallas_call(
        body, name="pool_bwd", grid=(B,), in_specs=[spec], out_specs=spec,
        out_shape=jax.ShapeDtypeStruct((B, S, W), BF16), compiler_params=_params("arbitrary"),
    )(dpooled)


def _mix_out_fwd(x, o_sb, pooled, w_pool, pool_scale, w_out):
    T, D = x.shape
    W = o_sb.shape[1]
    G = w_pool.shape[0]
    gd = POOL_GROUP_DIM
    tm = _tile(T, 1024)

    def body(x_ref, osb_ref, pooled_ref, wp_ref, ps_ref, wo_ref, out_ref, mixed_ref):
        mixed_ref[:, :W] = osb_ref[...]
        for gi in range(G):
            cols = slice(gi * gd, (gi + 1) * gd)
            pw = _mm(pooled_ref[:, cols], wp_ref[gi])
            mixed_ref[:, W + gi * gd:W + (gi + 1) * gd] = (pw * ps_ref[:, cols]).astype(BF16)
        out_ref[...] = x_ref[...] + _mm(mixed_ref[...], wo_ref[...].reshape(D, D))

    row = lambda i: (i, 0)
    return pl.pallas_call(
        body, name="mix_out_fwd", grid=(T // tm,),
        in_specs=[pl.BlockSpec((tm, D), row), pl.BlockSpec((tm, W), row), pl.BlockSpec((tm, W), row),
                  pl.BlockSpec((G, gd, gd), lambda i: (0, 0, 0)), pl.BlockSpec((1, W), lambda i: (0, 0)),
                  pl.BlockSpec(w_out.shape, lambda i: (0, 0, 0))],
        out_specs=(pl.BlockSpec((tm, D), row), pl.BlockSpec((tm, D), row)),
        out_shape=(jax.ShapeDtypeStruct((T, D), F32), jax.ShapeDtypeStruct((T, D), BF16)),
        compiler_params=_params("arbitrary"),
    )(x, o_sb, pooled, w_pool, pool_scale, w_out)


def _mix_out_bwd(dx, pooled, w_pool, pool_scale, w_out):
    T, D = dx.shape
    W = pooled.shape[1]
    G = w_pool.shape[0]
    gd = POOL_GROUP_DIM
    tm = _tile(T, 1024)

    def body(dx_ref, pooled_ref, wp_ref, ps_ref, wo_ref, dxb_ref, dosb_ref, dpooled_ref, dwp_ref, dps_ref):
        i = pl.program_id(0)

        @pl.when(i == 0)
        def _():
            dwp_ref[...] = jnp.zeros_like(dwp_ref)
            dps_ref[...] = jnp.zeros_like(dps_ref)

        dxb = dx_ref[...].astype(BF16)
        dxb_ref[...] = dxb
        dmixed = _mm_nt(dxb, wo_ref[...].reshape(D, D))
        dosb_ref[...] = dmixed[:, :W].astype(BF16)
        for gi in range(G):
            cols = slice(gi * gd, (gi + 1) * gd)
            pg = pooled_ref[:, cols]
            dop = dmixed[:, W + gi * gd:W + (gi + 1) * gd]
            pw = _mm(pg, wp_ref[gi])
            dps_ref[:, cols] += jnp.sum(dop * pw, axis=0, keepdims=True)
            dpw = (dop * ps_ref[:, cols]).astype(BF16)
            dwp_ref[gi] += _mm_tn(pg, dpw)
            dpooled_ref[:, cols] = _mm_nt(dpw, wp_ref[gi])

    row = lambda i: (i, 0)
    return pl.pallas_call(
        body, name="mix_out_bwd", grid=(T // tm,),
        in_specs=[pl.BlockSpec((tm, D), row), pl.BlockSpec((tm, W), row),
                  pl.BlockSpec((G, gd, gd), lambda i: (0, 0, 0)), pl.BlockSpec((1, W), lambda i: (0, 0)),
                  pl.BlockSpec(w_out.shape, lambda i: (0, 0, 0))],
        out_specs=(pl.BlockSpec((tm, D), row), pl.BlockSpec((tm, W), row), pl.BlockSpec((tm, W), row),
                   pl.BlockSpec((G, gd, gd), lambda i: (0, 0, 0)), pl.BlockSpec((1, W), lambda i: (0, 0))),
        out_shape=(jax.ShapeDtypeStruct((T, D), BF16), jax.ShapeDtypeStruct((T, W), BF16),
                   jax.ShapeDtypeStruct((T, W), F32), jax.ShapeDtypeStruct((G, gd, gd), F32),
                   jax.ShapeDtypeStruct((1, W), F32)),
        compiler_params=_params("arbitrary"),
    )(dx, pooled, w_pool, pool_scale, w_out)


def _mem_kv_fwd(mem, gain, w_kv):
    B, M, D = mem.shape
    cs = w_kv.shape[2]

    def body(mem_ref, gain_ref, w_ref, memn_ref, kv_ref):
        _, xhat = _rms(mem_ref[...])
        mn = (xhat * gain_ref[...]).astype(BF16)
        memn_ref[...] = mn
        for b in range(N_DEV):
            kv_ref[:, b * cs:(b + 1) * cs] = _mm(mn, w_ref[b]).astype(BF16)

    return pl.pallas_call(
        body, name="mem_kv_fwd", grid=(B,),
        in_specs=[pl.BlockSpec((None, M, D), lambda b: (b, 0, 0)), pl.BlockSpec((1, D), lambda b: (0, 0)),
                  pl.BlockSpec((N_DEV, D, cs), lambda b: (0, 0, 0))],
        out_specs=(pl.BlockSpec((M, D), lambda b: (b, 0)), pl.BlockSpec((None, M, N_DEV * cs), lambda b: (b, 0, 0))),
        out_shape=(jax.ShapeDtypeStruct((B * M, D), BF16), jax.ShapeDtypeStruct((B, M, N_DEV * cs), BF16)),
        compiler_params=_params("arbitrary"),
    )(mem, gain, w_kv)


def _mem_kv_bwd(dkv, mem, w_kv):
    B, M, D = mem.shape
    cs = w_kv.shape[2]

    def body(dkv_ref, mem_ref, w_ref, dkvb_ref, dgain_ref):
        b_id = pl.program_id(0)

        @pl.when(b_id == 0)
        def _():
            dgain_ref[...] = jnp.zeros_like(dgain_ref)

        dkvb = dkv_ref[...].astype(BF16)
        dkvb_ref[...] = dkvb
        dmn = jnp.zeros((M, D), F32)
        for b in range(N_DEV):
            dmn = dmn + _mm_nt(dkvb[:, b * cs:(b + 1) * cs], w_ref[b])
        _, xhat = _rms(mem_ref[...])
        dgain_ref[...] += jnp.sum(dmn * xhat, axis=0, keepdims=True)

    return pl.pallas_call(
        body, name="mem_kv_bwd", grid=(B,),
        in_specs=[pl.BlockSpec((None, M, N_DEV * cs), lambda b: (b, 0, 0)),
                  pl.BlockSpec((None, M, D), lambda b: (b, 0, 0)),
                  pl.BlockSpec((N_DEV, D, cs), lambda b: (0, 0, 0))],
        out_specs=(pl.BlockSpec((M, N_DEV * cs), lambda b: (b, 0)), pl.BlockSpec((1, D), lambda b: (0, 0))),
        out_shape=(jax.ShapeDtypeStruct((B * M, N_DEV * cs), BF16), jax.ShapeDtypeStruct((1, D), F32)),
        compiler_params=_params("arbitrary"),
    )(dkv, mem, w_kv)


def _softmax_rows(s):
    p = jnp.exp(s - jnp.max(s, axis=1, keepdims=True))
    return p / jnp.sum(p, axis=1, keepdims=True)


def _cross_fwd(x, gain, kv, w_q, w_o, B, S, comm=None):
    T, D = x.shape
    M = kv.shape[1]
    hd = D // MEM_HEADS
    tm = _tile(S, 1024)
    per = S // tm
    scale = hd ** -0.5

    def body(x_ref, gain_ref, kv_ref, wq_ref, wo_ref, out_ref, hq_ref, q_ref, ocat_ref):
        _, xhat = _rms(x_ref[...])
        hq = (xhat * gain_ref[...]).astype(BF16)
        hq_ref[...] = hq
        q = _mm(hq, wq_ref[...].reshape(D, D)).astype(BF16)
        q_ref[...] = q
        for h in range(MEM_HEADS):
            cols = slice(h * hd, (h + 1) * hd)
            s = _mm_nt(q[:, cols], kv_ref[:, cols]) * scale
            p = _softmax_rows(s).astype(BF16)
            ocat_ref[:, cols] = _mm(p, kv_ref[:, D + h * hd:D + (h + 1) * hd]).astype(BF16)
        out_ref[...] = x_ref[...] + _mm(ocat_ref[...], wo_ref[...].reshape(D, D))

    row = lambda b, t: (b * per + t, 0)
    wspec = pl.BlockSpec(w_q.shape, lambda b, t: (0, 0, 0))
    return _call(
        body, name="cross_fwd", grid=(B, per), comm=comm,
        in_specs=[pl.BlockSpec((tm, D), row), pl.BlockSpec((1, D), lambda b, t: (0, 0)),
                  pl.BlockSpec((None, M, 2 * D), lambda b, t: (b, 0, 0)), wspec, wspec],
        out_specs=tuple(pl.BlockSpec((tm, D), row) for _ in range(4)),
        out_shape=(jax.ShapeDtypeStruct((T, D), F32),) + tuple(jax.ShapeDtypeStruct((T, D), BF16) for _ in range(3)),
        args=(x, gain, kv, w_q, w_o))


def _cross_bwd(dy, x, gain, q, kv, w_q, w_o, B, S, comm=None):
    T, D = x.shape
    M = kv.shape[1]
    hd = D // MEM_HEADS
    tm = _tile(S, 512)
    per = S // tm
    scale = hd ** -0.5

    def body(dy_ref, x_ref, gain_ref, q_ref, kv_ref, wq_ref, wo_ref,
             dx_ref, dyb_ref, dqb_ref, dkv_ref, dgain_ref):
        b_id, t_id = pl.program_id(0), pl.program_id(1)

        @pl.when((b_id == 0) & (t_id == 0))
        def _():
            dgain_ref[...] = jnp.zeros_like(dgain_ref)

        @pl.when(t_id == 0)
        def _():
            dkv_ref[...] = jnp.zeros_like(dkv_ref)

        dyb = dy_ref[...].astype(BF16)
        dyb_ref[...] = dyb
        docat = _mm_nt(dyb, wo_ref[...].reshape(D, D)).astype(BF16)
        for h in range(MEM_HEADS):
            cols = slice(h * hd, (h + 1) * hd)
            vcols = slice(D + h * hd, D + (h + 1) * hd)
            qh, kh, vh, doh = q_ref[:, cols], kv_ref[:, cols], kv_ref[:, vcols], docat[:, cols]
            p = _softmax_rows(_mm_nt(qh, kh) * scale)
            dp = _mm_nt(doh, vh)
            ds = (p * (dp - jnp.sum(dp * p, axis=1, keepdims=True)) * scale).astype(BF16)
            dqb_ref[:, cols] = _mm(ds, kh).astype(BF16)
            dkv_ref[:, cols] += _mm_tn(ds, qh)
            dkv_ref[:, vcols] += _mm_tn(p.astype(BF16), doh)
        dhq = _mm_nt(dqb_ref[...], wq_ref[...].reshape(D, D))
        r, xhat = _rms(x_ref[...])
        dgain_ref[...] += jnp.sum(dhq * xhat, axis=0, keepdims=True)
        dx_ref[...] = dy_ref[...] + _rms_bwd(dhq, gain_ref[...], r, xhat)

    row = lambda b, t: (b * per + t, 0)
    wspec = pl.BlockSpec(w_q.shape, lambda b, t: (0, 0, 0))
    one = pl.BlockSpec((1, D), lambda b, t: (0, 0))
    kvspec = pl.BlockSpec((None, M, 2 * D), lambda b, t: (b, 0, 0))
    return _call(
        body, name="cross_bwd", grid=(B, per), comm=comm,
        in_specs=[pl.BlockSpec((tm, D), row), pl.BlockSpec((tm, D), row), one, pl.BlockSpec((tm, D), row),
                  kvspec, wspec, wspec],
        out_specs=(pl.BlockSpec((tm, D), row), pl.BlockSpec((tm, D), row), pl.BlockSpec((tm, D), row), kvspec, one),
        out_shape=(jax.ShapeDtypeStruct((T, D), F32), jax.ShapeDtypeStruct((T, D), BF16),
                   jax.ShapeDtypeStruct((T, D), BF16), jax.ShapeDtypeStruct((B, M, 2 * D), F32),
                   jax.ShapeDtypeStruct((1, D), F32)),
        args=(dy, x, gain, q, kv, w_q, w_o))


def _adamw(gparts, w, m, v, name):
    R, C = w.shape
    tr = _tile(R, 256)

    def body(gp_ref, w_ref, m_ref, v_ref, g_ref, d_ref, nm_ref, nv_ref):
        g = gp_ref[0].astype(F32)
        for s in range(1, N_DEV):
            g = g + gp_ref[s].astype(F32)
        nm = ADAM_B1 * m_ref[...] + (1.0 - ADAM_B1) * g
        nv = ADAM_B2 * v_ref[...] + (1.0 - ADAM_B2) * (g * g)
        m_hat = nm / (1.0 - ADAM_B1 ** ADAM_STEP)
        v_hat = nv / (1.0 - ADAM_B2 ** ADAM_STEP)
        g_ref[...] = g
        nm_ref[...] = nm
        nv_ref[...] = nv
        d_ref[...] = -ADAM_LR * (m_hat / (jnp.sqrt(v_hat) + ADAM_EPS) + ADAM_WD * w_ref[...])

    spec = pl.BlockSpec((tr, C), lambda i: (i, 0))
    shp = jax.ShapeDtypeStruct((R, C), F32)
    return pl.pallas_call(
        body, name=name, grid=(R // tr,),
        in_specs=[pl.BlockSpec((N_DEV, tr, C), lambda i: (0, i, 0)), spec, spec, spec],
        out_specs=(spec, spec, spec, spec), out_shape=(shp, shp, shp, shp),
        compiler_params=_params("arbitrary"),
    )(gparts, w, m, v)


def _rows128(a, rows):
    a = a.reshape(-1, LANES)
    return jnp.pad(a, ((0, rows - a.shape[0]), (0, 0)))


def kernel(x, mem, ffn1_norm, ffn1_w_gate, ffn1_w_up, ffn1_w_down, mix_norm, w_in, w_pool, pool_scale, w_out, mem_q_norm, mem_kv_norm, mem_w_q, mem_w_kv, mem_w_o, ffn2_norm, ffn2_w_gate, ffn2_w_up, ffn2_w_down, final_norm, loss_target, m_ffn1_norm, m_ffn1_w_gate, m_ffn1_w_up, m_ffn1_w_down, m_mix_norm, m_w_in, m_w_pool, m_pool_scale, m_w_out, m_mem_q_norm, m_mem_kv_norm, m_mem_w_q, m_mem_w_kv, m_mem_w_o, m_ffn2_norm, m_ffn2_w_gate, m_ffn2_w_up, m_ffn2_w_down, m_final_norm, v_ffn1_norm, v_ffn1_w_gate, v_ffn1_w_up, v_ffn1_w_down, v_mix_norm, v_w_in, v_w_pool, v_pool_scale, v_w_out, v_mem_q_norm, v_mem_kv_norm, v_mem_w_q, v_mem_w_kv, v_mem_w_o, v_ffn2_norm, v_ffn2_w_gate, v_ffn2_w_up, v_ffn2_w_down, v_final_norm):
    B, S, D = x.shape
    T = B * S
    x0 = x.reshape(T, D)
    target = loss_target.reshape(T, D)
    final_gain = final_norm.reshape(1, D)

    big = dict(
        g1=ffn1_w_gate[0].T, u1=ffn1_w_up[0].T, d1=ffn1_w_down[0],
        g2=ffn2_w_gate[0].T, u2=ffn2_w_up[0].T, d2=ffn2_w_down[0],
        w_in=w_in[0], w_out=w_out[0], w_q=mem_w_q[0], w_kv=mem_w_kv[0], w_o=mem_w_o[0])
    names = list(big)
    shard = {k: big[k].astype(BF16) for k in names}
    wp = w_pool[0].astype(BF16)
    full, ffn_w = {}, {}

    def gathered(keys, arrs):
        full.update(zip(keys, arrs))
        ffn_w.update({k: full[k].reshape(-1, D) for k in keys if k[0] in "gud"})

    first, mid = ("g1", "u1", "d1"), ("w_in", "w_out", "w_q", "w_kv", "w_o")
    gathered(first, _gather_two_level([shard[k] for k in first], "gather_ffn1"))
    (x1, hn1, gg1, uu1), got = _ffn_fwd(x0, ffn1_norm, ffn_w["g1"], ffn_w["u1"], ffn_w["d1"], "ffn1_fwd",
                                        comm=([shard[k] for k in mid], True))
    gathered(mid, got)
    hn2, qkv, u = _mix_in_fwd(x1, mix_norm, full["w_in"])
    qkv3 = qkv.reshape(B, S, -1)
    (o_sb,), got = _sb_fwd(qkv3, B, S, comm=([shard["g2"], shard["u2"]], True))
    gathered(("g2", "u2"), got)
    pooled = _pool_fwd(u.reshape(B, S, -1), B, S).reshape(T, -1)
    x2, mixed = _mix_out_fwd(x1, o_sb.reshape(T, -1), pooled, wp, pool_scale, full["w_out"])
    memn, kv = _mem_kv_fwd(mem, mem_kv_norm, full["w_kv"])
    (x3, hq, q, ocat), got = _cross_fwd(x2, mem_q_norm, kv, full["w_q"], full["w_o"], B, S,
                                        comm=([shard["d2"]], True))
    gathered(("d2",), got)
    (dx4, hn4, gg2, uu2, d_final, loss_part), _ = _ffn_fwd(x3, ffn2_norm, ffn_w["g2"], ffn_w["u2"], ffn_w["d2"],
                                                          "ffn2_fwd", head=(final_gain, target))

    slab = lambda k: grads[k].reshape((N_DEV, -1) + grads[k].shape[-1:])
    got = {}
    (dx3, a2, dg2, du2, dyh2, d_ffn2), _ = _ffn_bwd(dx4, x3, ffn2_norm, gg2, uu2, ffn_w["g2"], ffn_w["u2"],
                                                   ffn_w["d2"], "ffn2_bwd")
    grads = dict(g2=_wgrad(dg2, hn4, "dw_gate2"), u2=_wgrad(du2, hn4, "dw_up2"), d2=_wgrad(a2, dyh2, "dw_down2"))
    (dx2, dx3b, dqb, dkv, d_q), (got["g2"],) = _cross_bwd(dx3, x2, mem_q_norm, q, kv, full["w_q"], full["w_o"], B, S,
                                                         comm=([slab("g2")], False))
    grads["w_o"] = _wgrad(ocat, dx3b, "dw_o")
    grads["w_q"] = _wgrad(hq, dqb, "dw_q")
    dkvb, d_kv = _mem_kv_bwd(dkv, mem, full["w_kv"])
    grads["w_kv"] = _wgrad(memn, dkvb, "dw_kv", col_slab=full["w_kv"].shape[2])
    dx2b, do_sb, dpooled, d_wpool, d_ps = _mix_out_bwd(dx2, pooled, wp, pool_scale, full["w_out"])
    grads["w_out"] = _wgrad(mixed, dx2b, "dw_out")
    du = _pool_bwd(dpooled.reshape(B, S, -1), B, S).reshape(T, -1)
    early = ("u2", "d2", "w_o", "w_q", "w_kv", "w_out")
    (dq, dk, dv), res = _sb_bwd(qkv3, do_sb.reshape(B, S, -1), B, S, comm=([slab(k) for k in early], False))
    got.update(zip(early, res))
    dx1, dproj, d_mix = _mix_in_bwd(dx2, dq.reshape(T, -1), dk.reshape(T, -1), dv.reshape(T, -1), du,
                                    x1, mix_norm, full["w_in"])
    grads["w_in"] = _wgrad(hn2, dproj, "dw_in", col_slab=full["w_in"].shape[2])
    (dx0, a1, dg1, du1, dyh1, d_ffn1), _ = _ffn_bwd(dx1, x0, ffn1_norm, gg1, uu1, ffn_w["g1"], ffn_w["u1"],
                                                   ffn_w["d1"], "ffn1_bwd")

    small = [("ffn1_norm", d_ffn1, ffn1_norm, m_ffn1_norm, v_ffn1_norm),
             ("mix_norm", d_mix, mix_norm, m_mix_norm, v_mix_norm),
             ("w_pool", d_wpool, w_pool, m_w_pool, v_w_pool),
             ("pool_scale", d_ps, pool_scale, m_pool_scale, v_pool_scale),
             ("mem_q_norm", d_q, mem_q_norm, m_mem_q_norm, v_mem_q_norm),
             ("mem_kv_norm", d_kv, mem_kv_norm, m_mem_kv_norm, v_mem_kv_norm),
             ("ffn2_norm", d_ffn2, ffn2_norm, m_ffn2_norm, v_ffn2_norm),
             ("final_norm", d_final, final_norm, m_final_norm, v_final_norm)]
    rows = [max(8, t[2].size // LANES) for t in small]
    pack = lambda idx: jnp.concatenate([_rows128(t[idx], r) for t, r in zip(small, rows)]
                                       + ([loss_part] if idx == 1 else [jnp.zeros((8, LANES), F32)]))
    grads["g1"], (small_parts, got["w_in"]) = _wgrad(dg1, hn1, "dw_gate1",
                                                     comm=([pack(1), slab("w_in")], [True, False]))
    grads["u1"], (got["g1"],) = _wgrad(du1, hn1, "dw_up1", comm=([slab("g1")], False))
    grads["d1"], (got["u1"],) = _wgrad(a1, dyh1, "dw_down1", comm=([slab("u1")], False))
    got["d1"] = _exchange([slab("d1")], False, "scatter_last")[0]

    state = dict(
        g1=(ffn1_w_gate, m_ffn1_w_gate, v_ffn1_w_gate), u1=(ffn1_w_up, m_ffn1_w_up, v_ffn1_w_up),
        d1=(ffn1_w_down, m_ffn1_w_down, v_ffn1_w_down), g2=(ffn2_w_gate, m_ffn2_w_gate, v_ffn2_w_gate),
        u2=(ffn2_w_up, m_ffn2_w_up, v_ffn2_w_up), d2=(ffn2_w_down, m_ffn2_w_down, v_ffn2_w_down),
        w_in=(w_in, m_w_in, v_w_in), w_out=(w_out, m_w_out, v_w_out), w_q=(mem_w_q, m_mem_w_q, v_mem_w_q),
        w_kv=(mem_w_kv, m_mem_w_kv, v_mem_w_kv), w_o=(mem_w_o, m_mem_w_o, v_mem_w_o))
    big_out = {}
    for k in names:
        parts = got[k]
        if k in ("g1", "u1", "g2", "u2"):
            parts = jnp.swapaxes(parts, 1, 2)
        w_, m_, v_ = (t[0] for t in state[k])
        big_out[k] = [t[None] for t in _adamw(parts, w_, m_, v_, "adamw_" + k)]

    small_res = _adamw(small_parts, pack(2), pack(3), pack(4), "adamw_small")
    small_out, off = {}, 0
    for t, r in zip(small, rows):
        n = t[2].size // LANES
        small_out[t[0]] = [res[off:off + n].reshape(t[2].shape) for res in small_res]
        off += r
    loss = small_res[0][off, 0]

    order = [("ffn1_norm", None), ("ffn1_w_gate", "g1"), ("ffn1_w_up", "u1"), ("ffn1_w_down", "d1"),
             ("mix_norm", None), ("w_in", "w_in"), ("w_pool", None), ("pool_scale", None), ("w_out", "w_out"),
             ("mem_q_norm", None), ("mem_kv_norm", None), ("mem_w_q", "w_q"), ("mem_w_kv", "w_kv"),
             ("mem_w_o", "w_o"), ("ffn2_norm", None), ("ffn2_w_gate", "g2"), ("ffn2_w_up", "u2"),
             ("ffn2_w_down", "d2"), ("final_norm", None)]
    res = [loss, dx0.reshape(B, S, D)]
    for which in range(4):
        for name, key in order:
            res.append(big_out[key][which] if key else small_out[name][which])
    return tuple(res)
```

```python
import functools

import jax
import jax.numpy as jnp
from jax import lax
from jax.experimental import pallas as pl
from jax.experimental.pallas import tpu as pltpu

F32 = jnp.float32
BF16 = jnp.bfloat16

N_DEV = 8
EPS = 1e-6
SB_HEAD_DIM = 64
LANES = 128
POOL_WINDOWS = (2, 4, 8, 16)
POOL_GROUP_DIM = 128
MEM_HEADS = 4
FFN_RESIDUAL_WEIGHT = 0.5
ADAM_LR = 0.001
ADAM_B1 = 0.9
ADAM_B2 = 0.999
ADAM_EPS = 1e-08
ADAM_WD = 0.01
ADAM_STEP = 10
VMEM_LIMIT = 56 * 1024 * 1024

MESH_ID = pl.DeviceIdType.MESH


def _params(*sem):
    return pltpu.CompilerParams(dimension_semantics=sem, vmem_limit_bytes=VMEM_LIMIT)


def _tile(n, pref):
    if n <= pref:
        return n
    t = pref - pref % 8
    while n % t:
        t -= 8
    return t


def _mm(a, b):
    return jnp.dot(a, b, preferred_element_type=F32)


def _mm_nt(a, b):
    return lax.dot_general(a, b, (((1,), (1,)), ((), ())), preferred_element_type=F32)


def _mm_tn(a, b):
    return lax.dot_general(a, b, (((0,), (0,)), ((), ())), preferred_element_type=F32)


def _rms(xv):
    r = lax.rsqrt(jnp.mean(xv * xv, axis=-1, keepdims=True) + EPS)
    return r, xv * r


def _rms_bwd(dhn, gain, r, xhat):
    dxh = dhn * gain
    return r * (dxh - xhat * jnp.mean(dxh * xhat, axis=-1, keepdims=True))


def _sigmoid(z):
    return 0.5 * jnp.tanh(0.5 * z) + 0.5


def _flags(arrs, gather):
    return [gather] * len(arrs) if isinstance(gather, bool) else list(gather)


def _comm_shapes(arrs, gather):
    return tuple(jax.ShapeDtypeStruct(((N_DEV,) + tuple(a.shape)) if f else tuple(a.shape), a.dtype)
                 for a, f in zip(arrs, _flags(arrs, gather)))


def _comm_start(ins, outs, sems, gather):
    send_sems, recv_sems, local_sems = sems
    gather = _flags(ins, gather)
    x, y, c = lax.axis_index("x"), lax.axis_index("y"), lax.axis_index("c")
    me = 4 * x + 2 * y + c
    for i in range(len(ins)):
        src = ins[i] if gather[i] else ins[i].at[me]
        pltpu.make_async_copy(src, outs[i].at[me], local_sems.at[i]).start()
    for k in range(1, N_DEV):
        px = 1 - x if k & 4 else x
        py = 1 - y if k & 2 else y
        pc = 1 - c if k & 1 else c
        peer = 4 * px + 2 * py + pc
        for i in range(len(ins)):
            src = ins[i] if gather[i] else ins[i].at[peer]
            pltpu.make_async_remote_copy(
                src_ref=src, dst_ref=outs[i].at[me],
                send_sem=send_sems.at[i], recv_sem=recv_sems.at[i],
                device_id=(px, py, pc), device_id_type=MESH_ID).start()


def _comm_wait(ins, outs, sems, gather):
    send_sems, recv_sems, local_sems = sems
    gather = _flags(ins, gather)
    x, y, c = lax.axis_index("x"), lax.axis_index("y"), lax.axis_index("c")
    me = 4 * x + 2 * y + c
    for i in range(len(ins)):
        seven = outs[i].at[pl.ds(0, N_DEV - 1)]
        done = pltpu.make_async_remote_copy(
            src_ref=seven, dst_ref=seven,
            send_sem=send_sems.at[i], recv_sem=recv_sems.at[i],
            device_id=(x, y, c), device_id_type=MESH_ID)
        done.wait_send()
        done.wait_recv()
        src = ins[i] if gather[i] else ins[i].at[me]
        pltpu.make_async_copy(src, outs[i].at[me], local_sems.at[i]).wait()


def _comm_sems(n):
    return [pltpu.SemaphoreType.DMA((n,)) for _ in range(3)]


def _exchange(arrs, gather, name):
    n = len(arrs)

    def body(*refs):
        ins, outs, sems = refs[:n], refs[n:2 * n], refs[2 * n:]
        _comm_start(ins, outs, sems, gather)
        _comm_wait(ins, outs, sems, gather)

    any_spec = pl.BlockSpec(memory_space=pl.ANY)
    outs = pl.pallas_call(
        body, name=name, out_shape=_comm_shapes(arrs, gather),
        in_specs=[any_spec] * n, out_specs=tuple([any_spec] * n), scratch_shapes=_comm_sems(n),
    )(*arrs)
    return list(outs)


def _gather_two_level(arrs, name):
    n = len(arrs)

    def body(*refs):
        ins, outs = refs[:n], refs[n:2 * n]
        send_sems, recv_sems, local_sems = refs[2 * n:]
        x, y, c = lax.axis_index("x"), lax.axis_index("y"), lax.axis_index("c")
        me, sibling = (x, y, c), (x, y, 1 - c)
        chips = [(1 - x, y), (x, 1 - y), (1 - x, 1 - y)]

        def copy(i, k, block, to, own=False):
            slab = outs[i].at[4 * block[0] + 2 * block[1] + block[2]]
            return pltpu.make_async_remote_copy(
                src_ref=ins[i] if own else slab, dst_ref=slab,
                send_sem=send_sems.at[i, k], recv_sem=recv_sems.at[i, k],
                device_id=to, device_id_type=MESH_ID)

        mine = [pltpu.make_async_copy(ins[i], outs[i].at[4 * x + 2 * y + c], local_sems.at[i]) for i in range(n)]
        first = [copy(i, 0, me, sibling, own=True) for i in range(n)]
        first += [copy(i, 1 + j, me, (*chip, c), own=True) for j, chip in enumerate(chips) for i in range(n)]
        for cp in mine + first:
            cp.start()
        passed = []
        for j, chip in enumerate(chips):
            for i in range(n):
                copy(i, 1 + j, (*chip, c), me).wait_recv()
                passed.append(copy(i, 4 + j, (*chip, c), sibling))
                passed[-1].start()
        for i in range(n):
            copy(i, 0, sibling, me).wait_recv()
            for j, chip in enumerate(chips):
                copy(i, 4 + j, (*chip, 1 - c), me).wait_recv()
        for cp in first + passed:
            cp.wait_send()
        for cp in mine:
            cp.wait()

    any_spec = pl.BlockSpec(memory_space=pl.ANY)
    outs = pl.pallas_call(
        body, name=name, out_shape=_comm_shapes(arrs, True),
        in_specs=[any_spec] * n, out_specs=tuple([any_spec] * n),
        scratch_shapes=[pltpu.SemaphoreType.DMA((n, N_DEV - 1)), pltpu.SemaphoreType.DMA((n, N_DEV - 1)),
                        pltpu.SemaphoreType.DMA((n,))],
    )(*arrs)
    return list(outs)


def _call(body, *, name, grid, in_specs, out_specs, out_shape, args, scratch=(), comm=None):
    sem = ("arbitrary",) * len(grid)
    if comm is None:
        res = pl.pallas_call(body, name=name, grid=grid, in_specs=list(in_specs), out_specs=tuple(out_specs),
                             out_shape=tuple(out_shape), scratch_shapes=list(scratch),
                             compiler_params=_params(*sem))(*args)
        return tuple(res), []
    arrs, gather = comm
    n, n_in, n_out, n_sc = len(arrs), len(args), len(out_shape), len(scratch)

    def wrapped(*refs):
        ins, cin = refs[:n_in], refs[n_in:n_in + n]
        outs, cout = refs[n_in + n:n_in + n + n_out], refs[n_in + n + n_out:n_in + 2 * n + n_out]
        sc, sems = refs[n_in + 2 * n + n_out:n_in + 2 * n + n_out + n_sc], refs[n_in + 2 * n + n_out + n_sc:]
        ids = [pl.program_id(a) for a in range(len(grid))]
        first = functools.reduce(jnp.logical_and, [i == 0 for i in ids])
        last = functools.reduce(jnp.logical_and, [i == g - 1 for i, g in zip(ids, grid)])

        @pl.when(first)
        def _():
            _comm_start(cin, cout, sems, gather)

        body(*ins, *outs, *sc)

        @pl.when(last)
        def _():
            _comm_wait(cin, cout, sems, gather)

    any_spec = pl.BlockSpec(memory_space=pl.ANY)
    res = pl.pallas_call(
        wrapped, name=name, grid=grid, in_specs=list(in_specs) + [any_spec] * n,
        out_specs=tuple(out_specs) + (any_spec,) * n, out_shape=tuple(out_shape) + _comm_shapes(arrs, gather),
        scratch_shapes=list(scratch) + _comm_sems(n), compiler_params=_params(*sem))(*args, *arrs)
    return tuple(res[:n_out]), list(res[n_out:])


def _ffn_fwd(x, gain, wgt, wut, wd, name, comm=None, head=None):
    T, D = x.shape
    F = wd.shape[0]
    tm, tf = _tile(T, 1024), _tile(F, 256)
    nj = F // tf
    rc = _tile(tm, 512)

    def body(*refs):
        if head:
            (x_ref, gain_ref, wg_ref, wu_ref, wd_ref, fgain_ref, tgt_ref,
             out_ref, hn_ref, a_ref, s_ref, t_ref, dfgain_ref, loss_ref, acc, hn_s) = refs
        else:
            x_ref, gain_ref, wg_ref, wu_ref, wd_ref, out_ref, hn_ref, a_ref, s_ref, t_ref, acc, hn_s = refs
        i, j = pl.program_id(0), pl.program_id(1)

        if head:
            @pl.when((i == 0) & (j == 0))
            def _():
                dfgain_ref[...] = jnp.zeros_like(dfgain_ref)
                loss_ref[...] = jnp.zeros_like(loss_ref)

        @pl.when(j == 0)
        def _():
            _, xhat = _rms(x_ref[...])
            hn = (xhat * gain_ref[...]).astype(BF16)
            hn_s[...] = hn
            hn_ref[...] = hn
            acc[...] = jnp.zeros_like(acc)

        for r0 in range(0, tm, rc):
            rows = slice(r0, r0 + rc)
            hn = hn_s[rows, :]
            g = _mm_nt(hn, wg_ref[...])
            u = _mm_nt(hn, wu_ref[...])
            sig = _sigmoid(g)
            s = g * sig
            a = (s * u).astype(BF16)
            a_ref[rows, :] = a
            s_ref[rows, :] = s.astype(BF16)
            t_ref[rows, :] = (u * (sig + s * (1.0 - sig))).astype(BF16)
            acc[rows, :] += _mm(a, wd_ref[...])

        @pl.when(j == nj - 1)
        def _():
            y = x_ref[...] + FFN_RESIDUAL_WEIGHT * acc[...]
            if head:
                r, yhat = _rms(y)
                err = yhat * fgain_ref[...] - tgt_ref[...]
                loss_ref[...] += 0.5 * jnp.sum(jnp.mean(err * err, axis=-1, keepdims=True), axis=0, keepdims=True)
                dy = err * (1.0 / D)
                dfgain_ref[...] += jnp.sum(dy * yhat, axis=0, keepdims=True)
                out_ref[...] = _rms_bwd(dy, fgain_ref[...], r, yhat)
            else:
                out_ref[...] = y

    row = lambda i, j: (i, 0)
    one = pl.BlockSpec((1, D), lambda i, j: (0, 0))
    wspec = pl.BlockSpec((tf, D), lambda i, j: (j, 0))
    in_specs = [pl.BlockSpec((tm, D), row), one, wspec, wspec, wspec]
    out_specs = [pl.BlockSpec((tm, D), row), pl.BlockSpec((tm, D), row)] \
        + [pl.BlockSpec((tm, tf), lambda i, j: (i, j)) for _ in range(3)]
    out_shape = [jax.ShapeDtypeStruct((T, D), F32), jax.ShapeDtypeStruct((T, D), BF16)] \
        + [jax.ShapeDtypeStruct((T, F), BF16) for _ in range(3)]
    args = (x, gain, wgt, wut, wd)
    if head:
        in_specs += [one, pl.BlockSpec((tm, D), row)]
        out_specs += [one, pl.BlockSpec((8, LANES), lambda i, j: (0, 0))]
        out_shape += [jax.ShapeDtypeStruct((1, D), F32), jax.ShapeDtypeStruct((8, LANES), F32)]
        args += tuple(head)
    return _call(
        body, name=name, grid=(T // tm, nj), comm=comm, in_specs=in_specs, out_specs=out_specs,
        out_shape=out_shape, scratch=[pltpu.VMEM((tm, D), F32), pltpu.VMEM((tm, D), BF16)], args=args)


def _ffn_bwd(dy, x, gain, s, t, wgt, wut, wd, name):
    T, D = x.shape
    F = wd.shape[0]
    tm, tf = _tile(T, 1024), _tile(F, 256)
    rc = _tile(tm, 512)

    def act_body(dy_ref, s_ref, t_ref, wd_ref, dg_ref, du_ref, dyh_ref, dyh_s):
        @pl.when(pl.program_id(1) == 0)
        def _():
            dyh = (FFN_RESIDUAL_WEIGHT * dy_ref[...]).astype(BF16)
            dyh_s[...] = dyh
            dyh_ref[...] = dyh

        for r0 in range(0, tm, rc):
            rows = slice(r0, r0 + rc)
            da = _mm_nt(dyh_s[rows, :], wd_ref[...])
            dg_ref[rows, :] = (da * t_ref[rows, :].astype(F32)).astype(BF16)
            du_ref[rows, :] = (da * s_ref[rows, :].astype(F32)).astype(BF16)

    row = lambda i, j: (i, 0)
    tile = lambda i, j: (i, j)
    wide = jax.ShapeDtypeStruct((T, F), BF16)
    dg, du, dyh = pl.pallas_call(
        act_body, name=name + "_act", grid=(T // tm, F // tf),
        in_specs=[pl.BlockSpec((tm, D), row), pl.BlockSpec((tm, tf), tile), pl.BlockSpec((tm, tf), tile),
                  pl.BlockSpec((tf, D), lambda i, j: (j, 0))],
        out_specs=(pl.BlockSpec((tm, tf), tile), pl.BlockSpec((tm, tf), tile), pl.BlockSpec((tm, D), row)),
        out_shape=(wide, wide, jax.ShapeDtypeStruct((T, D), BF16)),
        scratch_shapes=[pltpu.VMEM((tm, D), BF16)],
        compiler_params=_params("arbitrary", "arbitrary"),
    )(dy, s, t, wd)

    tr = _tile(T, 512)

    def in_body(dg_ref, du_ref, dy_ref, x_ref, gain_ref, wg_hbm, wu_hbm, dx_ref, dgain_ref, wg_s, wu_s, sem):
        @pl.when(pl.program_id(0) == 0)
        def _():
            copies = [pltpu.make_async_copy(wg_hbm, wg_s, sem.at[0]), pltpu.make_async_copy(wu_hbm, wu_s, sem.at[1])]
            for cp in copies:
                cp.start()
            for cp in copies:
                cp.wait()
            dgain_ref[...] = jnp.zeros_like(dgain_ref)

        dhn = _mm(dg_ref[...], wg_s[...]) + _mm(du_ref[...], wu_s[...])
        r, xhat = _rms(x_ref[...])
        dgain_ref[...] += jnp.sum(dhn * xhat, axis=0, keepdims=True)
        dx_ref[...] = dy_ref[...] + _rms_bwd(dhn, gain_ref[...], r, xhat)

    rows = lambda i: (i, 0)
    one = pl.BlockSpec((1, D), lambda i: (0, 0))
    any_spec = pl.BlockSpec(memory_space=pl.ANY)
    dx, dgain = pl.pallas_call(
        in_body, name=name + "_in", grid=(T // tr,),
        in_specs=[pl.BlockSpec((tr, F), rows), pl.BlockSpec((tr, F), rows), pl.BlockSpec((tr, D), rows),
                  pl.BlockSpec((tr, D), rows), one, any_spec, any_spec],
        out_specs=(pl.BlockSpec((tr, D), rows), one),
        out_shape=(jax.ShapeDtypeStruct((T, D), F32), jax.ShapeDtypeStruct((1, D), F32)),
        scratch_shapes=[pltpu.VMEM((F, D), BF16), pltpu.VMEM((F, D), BF16), pltpu.SemaphoreType.DMA((2,))],
        compiler_params=_params("arbitrary"),
    )(dg, du, dy, x, gain, wgt, wut)
    return dx, dg, du, dyh, dgain


def _wgrad(a, b, name, col_slab=None, comm=None):
    T, M = a.shape
    N = b.shape[1]
    tmm = M if M <= 1024 else _tile(M, 1408)
    tn = _tile(N, 1024)
    tk = _tile(T, 1024)
    nk = T // tk
    per = tn // col_slab if col_slab else 0

    def body(a_ref, b_ref, out_ref, acc):
        k = pl.program_id(2)

        @pl.when(k == 0)
        def _():
            acc[...] = jnp.zeros_like(acc)

        acc[...] += _mm_tn(a_ref[...], b_ref[...])

        @pl.when(k == nk - 1)
        def _():
            if col_slab:
                for s in range(per):
                    out_ref[s] = acc[:, s * col_slab:(s + 1) * col_slab].astype(BF16)
            else:
                out_ref[...] = acc[...].astype(BF16)

    if col_slab:
        out_spec = pl.BlockSpec((per, tmm, col_slab), lambda m, n, k: (n, m, 0))
        out_shape = jax.ShapeDtypeStruct((N // col_slab, M, col_slab), BF16)
    else:
        out_spec = pl.BlockSpec((tmm, tn), lambda m, n, k: (m, n))
        out_shape = jax.ShapeDtypeStruct((M, N), BF16)
    (out,), got = _call(
        body, name=name, grid=(M // tmm, N // tn, nk), comm=comm,
        in_specs=[pl.BlockSpec((tk, tmm), lambda m, n, k: (k, m)), pl.BlockSpec((tk, tn), lambda m, n, k: (k, n))],
        out_specs=(out_spec,), out_shape=(out_shape,),
        scratch=[pltpu.VMEM((tmm, tn), F32)], args=(a, b))
    return (out, got) if comm else out


def _mix_in_fwd(x, gain, w_in):
    T, D = x.shape
    cs = w_in.shape[2]
    n_qkv = 3 * (N_DEV // 4)
    tm = _tile(T, 1024)

    def body(x_ref, gain_ref, w_ref, hn_ref, qkv_ref, u_ref):
        _, xhat = _rms(x_ref[...])
        hn = (xhat * gain_ref[...]).astype(BF16)
        hn_ref[...] = hn
        for b in range(N_DEV):
            p = _mm(hn, w_ref[b])
            if b < n_qkv:
                qkv_ref[:, b * cs:(b + 1) * cs] = p.astype(BF16)
            else:
                u_ref[:, (b - n_qkv) * cs:(b - n_qkv + 1) * cs] = p

    row = lambda i: (i, 0)
    return pl.pallas_call(
        body, name="mix_in_fwd", grid=(T // tm,),
        in_specs=[pl.BlockSpec((tm, D), row), pl.BlockSpec((1, D), lambda i: (0, 0)),
                  pl.BlockSpec((N_DEV, D, cs), lambda i: (0, 0, 0))],
        out_specs=(pl.BlockSpec((tm, D), row), pl.BlockSpec((tm, n_qkv * cs), row),
                   pl.BlockSpec((tm, (N_DEV - n_qkv) * cs), row)),
        out_shape=(jax.ShapeDtypeStruct((T, D), BF16), jax.ShapeDtypeStruct((T, n_qkv * cs), BF16),
                   jax.ShapeDtypeStruct((T, (N_DEV - n_qkv) * cs), F32)),
        compiler_params=_params("arbitrary"),
    )(x, gain, w_in)


def _mix_in_bwd(dres, dq, dk, dv, du, x, gain, w_in):
    T, D = x.shape
    cs = w_in.shape[2]
    W = dq.shape[1]
    per = W // cs
    tm = _tile(T, 512)

    def body(dres_ref, dq_ref, dk_ref, dv_ref, du_ref, x_ref, gain_ref, w_ref, dx_ref, dproj_ref, dgain_ref):
        i = pl.program_id(0)

        @pl.when(i == 0)
        def _():
            dgain_ref[...] = jnp.zeros_like(dgain_ref)

        dhn = jnp.zeros((tm, D), F32)
        for part, ref in enumerate((dq_ref, dk_ref, dv_ref, du_ref)):
            for h in range(per):
                b = part * per + h
                d = ref[:, h * cs:(h + 1) * cs]
                dproj_ref[:, b * cs:(b + 1) * cs] = d
                dhn = dhn + _mm_nt(d, w_ref[b])
        r, xhat = _rms(x_ref[...])
        dgain_ref[...] += jnp.sum(dhn * xhat, axis=0, keepdims=True)
        dx_ref[...] = dres_ref[...] + _rms_bwd(dhn, gain_ref[...], r, xhat)

    row = lambda i: (i, 0)
    one = pl.BlockSpec((1, D), lambda i: (0, 0))
    part = pl.BlockSpec((tm, W), row)
    return pl.pallas_call(
        body, name="mix_in_bwd", grid=(T // tm,),
        in_specs=[pl.BlockSpec((tm, D), row), part, part, part, part, pl.BlockSpec((tm, D), row), one,
                  pl.BlockSpec((N_DEV, D, cs), lambda i: (0, 0, 0))],
        out_specs=(pl.BlockSpec((tm, D), row), pl.BlockSpec((tm, 4 * W), row), one),
        out_shape=(jax.ShapeDtypeStruct((T, D), F32), jax.ShapeDtypeStruct((T, 4 * W), BF16),
                   jax.ShapeDtypeStruct((1, D), F32)),
        compiler_params=_params("arbitrary"),
    )(dres, dq, dk, dv, du, x, gain, w_in)


SB_PAIRS_PER_PROGRAM = 2
LOG2_E = 1.4426950408889634
EXP2_CLAMP = 126.0


def _neg_log2_sigmoid(nz2):
    w = jnp.minimum(nz2, EXP2_CLAMP)
    return w, jnp.log2(1.0 + jnp.exp2(w))


SB_DEAD_LOG2 = -160.0


def _sb_live(rests):
    worst = functools.reduce(jnp.maximum, rests)
    return (jnp.max(worst) > SB_DEAD_LOG2).astype(jnp.int32)


def _split(v):
    hi = v.astype(BF16)
    return hi, (v - hi.astype(F32)).astype(BF16)


def _tri_sum(v, tri):
    hi, lo = _split(v)
    return _mm(hi, tri) + _mm(lo, tri)


def _sb_fwd(qkv, B, S, comm=None):
    W = qkv.shape[2] // 3
    n_pair = W // LANES
    bq = _tile(S, 256)
    nq = S // bq
    hp = SB_PAIRS_PER_PROGRAM
    nscale2 = -(SB_HEAD_DIM ** -0.5) * LOG2_E

    def body(q_ref, k_ref, v_ref, o_ref):
        lane = lax.broadcasted_iota(jnp.int32, (1, LANES), 1)
        head0 = lane < SB_HEAD_DIM
        rr = lax.broadcasted_iota(jnp.int32, (bq, bq), 0)
        cc = lax.broadcasted_iota(jnp.int32, (bq, bq), 1)
        strict = cc < rr
        after = jnp.where(rr > cc, 1.0, 0.0).astype(BF16)

        def blocks(heads, ks, carries, diag):
            n = range(len(heads))
            keep = (lambda t: jnp.where(strict, t, 0.0)) if diag else (lambda t: t)
            z = [_mm_nt(qh, k_ref[ks, cols]) for qh, cols in heads]
            wl = [_neg_log2_sigmoid(z[h] * nscale2) for h in n]
            lr = [keep(wl[h][0] - wl[h][1]) for h in n]
            parts = [_split(lr[h]) for h in n]
            suf = [_mm(parts[h][0], after) + _mm(parts[h][1], after) for h in n]
            a = [keep(jnp.exp2(suf[h] + carries[h][1] - wl[h][1])).astype(BF16) for h in n]
            o = [carries[h][0] + _mm(a[h], v_ref[ks, heads[h][1]]) for h in n]
            return tuple((o[h], carries[h][1] + (suf[h][:, :1] + lr[h][:, :1])) for h in n)

        def q_tile(i, _):
            qs = pl.ds(pl.multiple_of(i * bq, bq), bq)
            heads = []
            for pr in range(hp):
                cols = slice(pr * LANES, (pr + 1) * LANES)
                qv = q_ref[qs, cols]
                heads += [(jnp.where(head0, qv, jnp.zeros_like(qv)), cols),
                          (jnp.where(head0, jnp.zeros_like(qv), qv), cols)]
            zero = (jnp.zeros((bq, LANES), F32), jnp.zeros((bq, 1), F32))
            init = blocks(heads, qs, (zero,) * len(heads), True)

            def left(st):
                t, _, cr = st
                ks = pl.ds(pl.multiple_of((i - 1 - t) * bq, bq), bq)
                cr = blocks(heads, ks, cr, False)
                return t + 1, _sb_live([c for _, c in cr]), cr

            _, _, res = lax.while_loop(lambda st: jnp.logical_and(st[0] < i, st[1] > 0), left,
                                       (jnp.int32(0), _sb_live([c for _, c in init]), init))
            for pr in range(hp):
                o_ref[qs, heads[2 * pr][1]] = jnp.where(head0, res[2 * pr][0], res[2 * pr + 1][0]).astype(BF16)
            return 0

        lax.fori_loop(0, nq, q_tile, 0)

    def col(off):
        return pl.BlockSpec((None, S, hp * LANES), lambda b, p: (b, 0, off + p))

    n_pair //= hp
    return _call(
        body, name="sb_fwd", grid=(B, n_pair), comm=comm,
        in_specs=[col(0), col(n_pair), col(2 * n_pair)],
        out_specs=(col(0),),
        out_shape=(jax.ShapeDtypeStruct((B, S, W), BF16),),
        args=(qkv, qkv, qkv))


def _sb_bwd(qkv, do, B, S, comm=None):
    W = qkv.shape[2] // 3
    n_pair = W // LANES
    bq = _tile(S, 256)
    nq = S // bq
    hp = SB_PAIRS_PER_PROGRAM
    scale = SB_HEAD_DIM ** -0.5
    nscale2 = -scale * LOG2_E

    def body(q_ref, k_ref, v_ref, do_ref, dq_ref, dk_ref, dv_ref, dk_s, dv_s, e_s, sg_s, a_s):
        lane = lax.broadcasted_iota(jnp.int32, (1, LANES), 1)
        head0 = lane < SB_HEAD_DIM
        rr = lax.broadcasted_iota(jnp.int32, (bq, bq), 0)
        cc = lax.broadcasted_iota(jnp.int32, (bq, bq), 1)
        strict = cc < rr
        after = jnp.where(rr > cc, 1.0, 0.0).astype(BF16)
        before = jnp.where(rr < cc, 1.0, 0.0).astype(BF16)
        dk_s[...] = jnp.zeros_like(dk_s)
        dv_s[...] = jnp.zeros_like(dv_s)

        def weights(heads, ks, kb, rests, diag):
            n = range(len(heads))
            keep = (lambda t: jnp.where(strict, t, 0.0)) if diag else (lambda t: t)
            z = [_mm_nt(heads[h][0], k_ref[ks, heads[h][2]]) for h in n]
            da = [_mm_nt(heads[h][1], v_ref[ks, heads[h][2]]) for h in n]
            wl = [_neg_log2_sigmoid(z[h] * nscale2) for h in n]
            lr = [keep(wl[h][0] - wl[h][1]) for h in n]
            parts = [_split(lr[h]) for h in n]
            suf = [_mm(parts[h][0], after) + _mm(parts[h][1], after) for h in n]
            a = [keep(jnp.exp2(suf[h] + rests[h] - wl[h][1])) for h in n]
            for h in n:
                a_s[h * nq + kb] = a[h].astype(BF16)
                e_s[h * nq + kb] = a[h] * da[h]
                sg_s[h * nq + kb] = jnp.exp2(-wl[h][1])
            return tuple(rests[h] + (suf[h][:, :1] + lr[h][:, :1]) for h in n)

        def grads(heads, ks, kb, carries, diag):
            n = range(len(heads))
            keep = (lambda t: jnp.where(strict, t, 0.0)) if diag else (lambda t: t)
            e = [e_s[h * nq + kb] for h in n]
            parts = [_split(e[h]) for h in n]
            pex = [_mm(parts[h][0], before) + _mm(parts[h][1], before) for h in n]
            dz = [keep(e[h] - sg_s[h * nq + kb] * (e[h] + pex[h] + carries[h][1])).astype(BF16) for h in n]
            dq = [carries[h][0] + _mm(dz[h], k_ref[ks, heads[h][2]]) for h in n]
            for h in n:
                dk_s[ks, heads[h][2]] += _mm_tn(dz[h], heads[h][0])
                dv_s[ks, heads[h][2]] += _mm_tn(a_s[h * nq + kb], heads[h][1])
            return tuple((dq[h], carries[h][1] + (pex[h][:, bq - 1:] + e[h][:, bq - 1:])) for h in n)

        def q_tile(i, _):
            qs = pl.ds(pl.multiple_of(i * bq, bq), bq)
            heads = []
            for pr in range(hp):
                cols = slice(pr * LANES, (pr + 1) * LANES)
                qv, dov = q_ref[qs, cols], do_ref[qs, cols]
                zq, zd = jnp.zeros_like(qv), jnp.zeros_like(dov)
                heads += [(jnp.where(head0, qv, zq), jnp.where(head0, dov, zd), cols),
                          (jnp.where(head0, zq, qv), jnp.where(head0, zd, dov), cols)]
            key_block = lambda kb: pl.ds(pl.multiple_of(kb * bq, bq), bq)
            rests = weights(heads, qs, i, (jnp.zeros((bq, 1), F32),) * len(heads), True)

            def left(st):
                t, _, rs = st
                rs = weights(heads, key_block(i - 1 - t), i - 1 - t, rs, False)
                return t + 1, _sb_live(rs), rs

            n_left, _, _ = lax.while_loop(lambda st: jnp.logical_and(st[0] < i, st[1] > 0), left,
                                          (jnp.int32(0), _sb_live(rests), rests))
            zero = (jnp.zeros((bq, LANES), F32), jnp.zeros((bq, 1), F32))
            res = lax.fori_loop(0, n_left, lambda t, cr: grads(heads, key_block(i - n_left + t), i - n_left + t, cr, False),
                                (zero,) * len(heads))
            res = grads(heads, qs, i, res, True)
            for pr in range(hp):
                dq = jnp.where(head0, res[2 * pr][0], res[2 * pr + 1][0])
                dq_ref[qs, heads[2 * pr][2]] = (dq * scale).astype(BF16)
            return 0

        lax.fori_loop(0, nq, q_tile, 0)
        dk_ref[...] = (dk_s[...] * scale).astype(BF16)
        dv_ref[...] = dv_s[...].astype(BF16)

    def col(off):
        return pl.BlockSpec((None, S, hp * LANES), lambda b, p: (b, 0, off + p))

    n_pair //= hp
    shp = jax.ShapeDtypeStruct((B, S, W), BF16)
    slots = 2 * hp * nq
    return _call(
        body, name="sb_bwd", grid=(B, n_pair), comm=comm,
        in_specs=[col(0), col(n_pair), col(2 * n_pair), col(0)],
        out_specs=(col(0), col(0), col(0)),
        out_shape=(shp, shp, shp),
        scratch=[pltpu.VMEM((S, hp * LANES), F32), pltpu.VMEM((S, hp * LANES), F32),
                 pltpu.VMEM((slots, bq, bq), F32), pltpu.VMEM((slots, bq, bq), F32),
                 pltpu.VMEM((slots, bq, bq), BF16)],
        args=(qkv, qkv, qkv, do))


def _pool_counts(S):
    t = lax.broadcasted_iota(jnp.int32, (S, 1), 0)
    return t, [jnp.minimum(t + 1, w).astype(F32) for w in POOL_WINDOWS]


def _pool_fwd(u, B, S):
    W = u.shape[2]

    def body(u_ref, out_ref):
        t, counts = _pool_counts(S)
        for gi, w in enumerate(POOL_WINDOWS):
            cols = slice(gi * POOL_GROUP_DIM, (gi + 1) * POOL_GROUP_DIM)
            ug = u_ref[:, cols]
            s, k = ug, 1
            while k < w:
                s = s + jnp.where(t >= k, pltpu.roll(s, k, axis=0), 0.0)
                k *= 2
            out_ref[:, cols] = (s / counts[gi] - ug).astype(BF16)

    spec = pl.BlockSpec((None, S, W), lambda b: (b, 0, 0))
    return pl.pallas_call(
        body, name="pool_fwd", grid=(B,), in_specs=[spec], out_specs=spec,
        out_shape=jax.ShapeDtypeStruct((B, S, W), BF16), compiler_params=_params("arbitrary"),
    )(u)


def _pool_bwd(dpooled, B, S):
    W = dpooled.shape[2]

    def body(d_ref, out_ref):
        t, counts = _pool_counts(S)
        for gi, w in enumerate(POOL_WINDOWS):
            cols = slice(gi * POOL_GROUP_DIM, (gi + 1) * POOL_GROUP_DIM)
            d = d_ref[:, cols]
            s, k = d / counts[gi], 1
            while k < w:
                s = s + jnp.where(t < S - k, pltpu.roll(s, S - k, axis=0), 0.0)
                k *= 2
            out_ref[:, cols] = (s - d).astype(BF16)

    spec = pl.BlockSpec((None, S, W), lambda b: (b, 0, 0))
    return pl.pallas_call(
        body, name="pool_bwd", grid=(B,), in_specs=[spec], out_specs=spec,
        out_shape=jax.ShapeDtypeStruct((B, S, W), BF16), compiler_params=_params("arbitrary"),
    )(dpooled)


def _mix_out_fwd(x, o_sb, pooled, w_pool, pool_scale, w_out):
    T, D = x.shape
    W = o_sb.shape[1]
    G = w_pool.shape[0]
    gd = POOL_GROUP_DIM
    tm = _tile(T, 1024)

    def body(x_ref, osb_ref, pooled_ref, wp_ref, ps_ref, wo_ref, out_ref, mixed_ref):
        mixed_ref[:, :W] = osb_ref[...]
        for gi in range(G):
            cols = slice(gi * gd, (gi + 1) * gd)
            pw = _mm(pooled_ref[:, cols], wp_ref[gi])
            mixed_ref[:, W + gi * gd:W + (gi + 1) * gd] = (pw * ps_ref[:, cols]).astype(BF16)
        out_ref[...] = x_ref[...] + _mm(mixed_ref[...], wo_ref[...].reshape(D, D))

    row = lambda i: (i, 0)
    return pl.pallas_call(
        body, name="mix_out_fwd", grid=(T // tm,),
        in_specs=[pl.BlockSpec((tm, D), row), pl.BlockSpec((tm, W), row), pl.BlockSpec((tm, W), row),
                  pl.BlockSpec((G, gd, gd), lambda i: (0, 0, 0)), pl.BlockSpec((1, W), lambda i: (0, 0)),
                  pl.BlockSpec(w_out.shape, lambda i: (0, 0, 0))],
        out_specs=(pl.BlockSpec((tm, D), row), pl.BlockSpec((tm, D), row)),
        out_shape=(jax.ShapeDtypeStruct((T, D), F32), jax.ShapeDtypeStruct((T, D), BF16)),
        compiler_params=_params("arbitrary"),
    )(x, o_sb, pooled, w_pool, pool_scale, w_out)


def _mix_out_bwd(dx, pooled, w_pool, pool_scale, w_out):
    T, D = dx.shape
    W = pooled.shape[1]
    G = w_pool.shape[0]
    gd = POOL_GROUP_DIM
    tm = _tile(T, 1024)

    def body(dx_ref, pooled_ref, wp_ref, ps_ref, wo_ref, dxb_ref, dosb_ref, dpooled_ref, dwp_ref, dps_ref):
        i = pl.program_id(0)

        @pl.when(i == 0)
        def _():
            dwp_ref[...] = jnp.zeros_like(dwp_ref)
            dps_ref[...] = jnp.zeros_like(dps_ref)

        dxb = dx_ref[...].astype(BF16)
        dxb_ref[...] = dxb
        dmixed = _mm_nt(dxb, wo_ref[...].reshape(D, D))
        dosb_ref[...] = dmixed[:, :W].astype(BF16)
        for gi in range(G):
            cols = slice(gi * gd, (gi + 1) * gd)
            pg = pooled_ref[:, cols]
            dop = dmixed[:, W + gi * gd:W + (gi + 1) * gd]
            pw = _mm(pg, wp_ref[gi])
            dps_ref[:, cols] += jnp.sum(dop * pw, axis=0, keepdims=True)
            dpw = (dop * ps_ref[:, cols]).astype(BF16)
            dwp_ref[gi] += _mm_tn(pg, dpw)
            dpooled_ref[:, cols] = _mm_nt(dpw, wp_ref[gi])

    row = lambda i: (i, 0)
    return pl.pallas_call(
        body, name="mix_out_bwd", grid=(T // tm,),
        in_specs=[pl.BlockSpec((tm, D), row), pl.BlockSpec((tm, W), row),
                  pl.BlockSpec((G, gd, gd), lambda i: (0, 0, 0)), pl.BlockSpec((1, W), lambda i: (0, 0)),
                  pl.BlockSpec(w_out.shape, lambda i: (0, 0, 0))],
        out_specs=(pl.BlockSpec((tm, D), row), pl.BlockSpec((tm, W), row), pl.BlockSpec((tm, W), row),
                   pl.BlockSpec((G, gd, gd), lambda i: (0, 0, 0)), pl.BlockSpec((1, W), lambda i: (0, 0))),
        out_shape=(jax.ShapeDtypeStruct((T, D), BF16), jax.ShapeDtypeStruct((T, W), BF16),
                   jax.ShapeDtypeStruct((T, W), F32), jax.ShapeDtypeStruct((G, gd, gd), F32),
                   jax.ShapeDtypeStruct((1, W), F32)),
        compiler_params=_params("arbitrary"),
    )(dx, pooled, w_pool, pool_scale, w_out)


def _mem_kv_fwd(mem, gain, w_kv):
    B, M, D = mem.shape
    cs = w_kv.shape[2]

    def body(mem_ref, gain_ref, w_ref, memn_ref, kv_ref):
        _, xhat = _rms(mem_ref[...])
        mn = (xhat * gain_ref[...]).astype(BF16)
        memn_ref[...] = mn
        for b in range(N_DEV):
            kv_ref[:, b * cs:(b + 1) * cs] = _mm(mn, w_ref[b]).astype(BF16)

    return pl.pallas_call(
        body, name="mem_kv_fwd", grid=(B,),
        in_specs=[pl.BlockSpec((None, M, D), lambda b: (b, 0, 0)), pl.BlockSpec((1, D), lambda b: (0, 0)),
                  pl.BlockSpec((N_DEV, D, cs), lambda b: (0, 0, 0))],
        out_specs=(pl.BlockSpec((M, D), lambda b: (b, 0)), pl.BlockSpec((None, M, N_DEV * cs), lambda b: (b, 0, 0))),
        out_shape=(jax.ShapeDtypeStruct((B * M, D), BF16), jax.ShapeDtypeStruct((B, M, N_DEV * cs), BF16)),
        compiler_params=_params("arbitrary"),
    )(mem, gain, w_kv)


def _mem_kv_bwd(dkv, mem, w_kv):
    B, M, D = mem.shape
    cs = w_kv.shape[2]

    def body(dkv_ref, mem_ref, w_ref, dkvb_ref, dgain_ref):
        b_id = pl.program_id(0)

        @pl.when(b_id == 0)
        def _():
            dgain_ref[...] = jnp.zeros_like(dgain_ref)

        dkvb = dkv_ref[...].astype(BF16)
        dkvb_ref[...] = dkvb
        dmn = jnp.zeros((M, D), F32)
        for b in range(N_DEV):
            dmn = dmn + _mm_nt(dkvb[:, b * cs:(b + 1) * cs], w_ref[b])
        _, xhat = _rms(mem_ref[...])
        dgain_ref[...] += jnp.sum(dmn * xhat, axis=0, keepdims=True)

    return pl.pallas_call(
        body, name="mem_kv_bwd", grid=(B,),
        in_specs=[pl.BlockSpec((None, M, N_DEV * cs), lambda b: (b, 0, 0)),
                  pl.BlockSpec((None, M, D), lambda b: (b, 0, 0)),
                  pl.BlockSpec((N_DEV, D, cs), lambda b: (0, 0, 0))],
        out_specs=(pl.BlockSpec((M, N_DEV * cs), lambda b: (b, 0)), pl.BlockSpec((1, D), lambda b: (0, 0))),
        out_shape=(jax.ShapeDtypeStruct((B * M, N_DEV * cs), BF16), jax.ShapeDtypeStruct((1, D), F32)),
        compiler_params=_params("arbitrary"),
    )(dkv, mem, w_kv)


def _softmax_rows(s):
    p = jnp.exp(s - jnp.max(s, axis=1, keepdims=True))
    return p / jnp.sum(p, axis=1, keepdims=True)


def _cross_fwd(x, gain, kv, w_q, w_o, B, S, comm=None):
    T, D = x.shape
    M = kv.shape[1]
    hd = D // MEM_HEADS
    tm = _tile(S, 1024)
    per = S // tm
    scale = hd ** -0.5

    def body(x_ref, gain_ref, kv_ref, wq_ref, wo_ref, out_ref, hq_ref, q_ref, ocat_ref):
        _, xhat = _rms(x_ref[...])
        hq = (xhat * gain_ref[...]).astype(BF16)
        hq_ref[...] = hq
        q = _mm(hq, wq_ref[...].reshape(D, D)).astype(BF16)
        q_ref[...] = q
        for h in range(MEM_HEADS):
            cols = slice(h * hd, (h + 1) * hd)
            s = _mm_nt(q[:, cols], kv_ref[:, cols]) * scale
            p = _softmax_rows(s).astype(BF16)
            ocat_ref[:, cols] = _mm(p, kv_ref[:, D + h * hd:D + (h + 1) * hd]).astype(BF16)
        out_ref[...] = x_ref[...] + _mm(ocat_ref[...], wo_ref[...].reshape(D, D))

    row = lambda b, t: (b * per + t, 0)
    wspec = pl.BlockSpec(w_q.shape, lambda b, t: (0, 0, 0))
    return _call(
        body, name="cross_fwd", grid=(B, per), comm=comm,
        in_specs=[pl.BlockSpec((tm, D), row), pl.BlockSpec((1, D), lambda b, t: (0, 0)),
                  pl.BlockSpec((None, M, 2 * D), lambda b, t: (b, 0, 0)), wspec, wspec],
        out_specs=tuple(pl.BlockSpec((tm, D), row) for _ in range(4)),
        out_shape=(jax.ShapeDtypeStruct((T, D), F32),) + tuple(jax.ShapeDtypeStruct((T, D), BF16) for _ in range(3)),
        args=(x, gain, kv, w_q, w_o))


def _cross_bwd(dy, x, gain, q, kv, w_q, w_o, B, S, comm=None):
    T, D = x.shape
    M = kv.shape[1]
    hd = D // MEM_HEADS
    tm = _tile(S, 512)
    per = S // tm
    scale = hd ** -0.5

    def body(dy_ref, x_ref, gain_ref, q_ref, kv_ref, wq_ref, wo_ref,
             dx_ref, dyb_ref, dqb_ref, dkv_ref, dgain_ref):
        b_id, t_id = pl.program_id(0), pl.program_id(1)

        @pl.when((b_id == 0) & (t_id == 0))
        def _():
            dgain_ref[...] = jnp.zeros_like(dgain_ref)

        @pl.when(t_id == 0)
        def _():
            dkv_ref[...] = jnp.zeros_like(dkv_ref)

        dyb = dy_ref[...].astype(BF16)
        dyb_ref[...] = dyb
        docat = _mm_nt(dyb, wo_ref[...].reshape(D, D)).astype(BF16)
        for h in range(MEM_HEADS):
            cols = slice(h * hd, (h + 1) * hd)
            vcols = slice(D + h * hd, D + (h + 1) * hd)
            qh, kh, vh, doh = q_ref[:, cols], kv_ref[:, cols], kv_ref[:, vcols], docat[:, cols]
            p = _softmax_rows(_mm_nt(qh, kh) * scale)
            dp = _mm_nt(doh, vh)
            ds = (p * (dp - jnp.sum(dp * p, axis=1, keepdims=True)) * scale).astype(BF16)
            dqb_ref[:, cols] = _mm(ds, kh).astype(BF16)
            dkv_ref[:, cols] += _mm_tn(ds, qh)
            dkv_ref[:, vcols] += _mm_tn(p.astype(BF16), doh)
        dhq = _mm_nt(dqb_ref[...], wq_ref[...].reshape(D, D))
        r, xhat = _rms(x_ref[...])
        dgain_ref[...] += jnp.sum(dhq * xhat, axis=0, keepdims=True)
        dx_ref[...] = dy_ref[...] + _rms_bwd(dhq, gain_ref[...], r, xhat)

    row = lambda b, t: (b * per + t, 0)
    wspec = pl.BlockSpec(w_q.shape, lambda b, t: (0, 0, 0))
    one = pl.BlockSpec((1, D), lambda b, t: (0, 0))
    kvspec = pl.BlockSpec((None, M, 2 * D), lambda b, t: (b, 0, 0))
    return _call(
        body, name="cross_bwd", grid=(B, per), comm=comm,
        in_specs=[pl.BlockSpec((tm, D), row), pl.BlockSpec((tm, D), row), one, pl.BlockSpec((tm, D), row),
                  kvspec, wspec, wspec],
        out_specs=(pl.BlockSpec((tm, D), row), pl.BlockSpec((tm, D), row), pl.BlockSpec((tm, D), row), kvspec, one),
        out_shape=(jax.ShapeDtypeStruct((T, D), F32), jax.ShapeDtypeStruct((T, D), BF16),
                   jax.ShapeDtypeStruct((T, D), BF16), jax.ShapeDtypeStruct((B, M, 2 * D), F32),
                   jax.ShapeDtypeStruct((1, D), F32)),
        args=(dy, x, gain, q, kv, w_q, w_o))


def _adamw(gparts, w, m, v, name):
    R, C = w.shape
    tr = _tile(R, 256)

    def body(gp_ref, w_ref, m_ref, v_ref, g_ref, d_ref, nm_ref, nv_ref):
        g = gp_ref[0].astype(F32)
        for s in range(1, N_DEV):
            g = g + gp_ref[s].astype(F32)
        nm = ADAM_B1 * m_ref[...] + (1.0 - ADAM_B1) * g
        nv = ADAM_B2 * v_ref[...] + (1.0 - ADAM_B2) * (g * g)
        m_hat = nm / (1.0 - ADAM_B1 ** ADAM_STEP)
        v_hat = nv / (1.0 - ADAM_B2 ** ADAM_STEP)
        g_ref[...] = g
        nm_ref[...] = nm
        nv_ref[...] = nv
        d_ref[...] = -ADAM_LR * (m_hat / (jnp.sqrt(v_hat) + ADAM_EPS) + ADAM_WD * w_ref[...])

    spec = pl.BlockSpec((tr, C), lambda i: (i, 0))
    shp = jax.ShapeDtypeStruct((R, C), F32)
    return pl.pallas_call(
        body, name=name, grid=(R // tr,),
        in_specs=[pl.BlockSpec((N_DEV, tr, C), lambda i: (0, i, 0)), spec, spec, spec],
        out_specs=(spec, spec, spec, spec), out_shape=(shp, shp, shp, shp),
        compiler_params=_params("arbitrary"),
    )(gparts, w, m, v)


def _rows128(a, rows):
    a = a.reshape(-1, LANES)
    return jnp.pad(a, ((0, rows - a.shape[0]), (0, 0)))


def kernel(x, mem, ffn1_norm, ffn1_w_gate, ffn1_w_up, ffn1_w_down, mix_norm, w_in, w_pool, pool_scale, w_out, mem_q_norm, mem_kv_norm, mem_w_q, mem_w_kv, mem_w_o, ffn2_norm, ffn2_w_gate, ffn2_w_up, ffn2_w_down, final_norm, loss_target, m_ffn1_norm, m_ffn1_w_gate, m_ffn1_w_up, m_ffn1_w_down, m_mix_norm, m_w_in, m_w_pool, m_pool_scale, m_w_out, m_mem_q_norm, m_mem_kv_norm, m_mem_w_q, m_mem_w_kv, m_mem_w_o, m_ffn2_norm, m_ffn2_w_gate, m_ffn2_w_up, m_ffn2_w_down, m_final_norm, v_ffn1_norm, v_ffn1_w_gate, v_ffn1_w_up, v_ffn1_w_down, v_mix_norm, v_w_in, v_w_pool, v_pool_scale, v_w_out, v_mem_q_norm, v_mem_kv_norm, v_mem_w_q, v_mem_w_kv, v_mem_w_o, v_ffn2_norm, v_ffn2_w_gate, v_ffn2_w_up, v_ffn2_w_down, v_final_norm):
    B, S, D = x.shape
    T = B * S
    x0 = x.reshape(T, D)
    target = loss_target.reshape(T, D)
    final_gain = final_norm.reshape(1, D)

    big = dict(
        g1=ffn1_w_gate[0].T, u1=ffn1_w_up[0].T, d1=ffn1_w_down[0],
        g2=ffn2_w_gate[0].T, u2=ffn2_w_up[0].T, d2=ffn2_w_down[0],
        w_in=w_in[0], w_out=w_out[0], w_q=mem_w_q[0], w_kv=mem_w_kv[0], w_o=mem_w_o[0])
    names = list(big)
    shard = {k: big[k].astype(BF16) for k in names}
    wp = w_pool[0].astype(BF16)
    full, ffn_w = {}, {}

    def gathered(keys, arrs):
        full.update(zip(keys, arrs))
        ffn_w.update({k: full[k].reshape(-1, D) for k in keys if k[0] in "gud"})

    first, mid = ("g1", "u1", "d1"), ("w_in", "w_out", "w_q", "w_kv", "w_o")
    gathered(first, _gather_two_level([shard[k] for k in first], "gather_ffn1"))
    (x1, hn1, a1, s1, t1), got = _ffn_fwd(x0, ffn1_norm, ffn_w["g1"], ffn_w["u1"], ffn_w["d1"], "ffn1_fwd",
                                          comm=([shard[k] for k in mid], True))
    gathered(mid, got)
    hn2, qkv, u = _mix_in_fwd(x1, mix_norm, full["w_in"])
    qkv3 = qkv.reshape(B, S, -1)
    (o_sb,), got = _sb_fwd(qkv3, B, S, comm=([shard["g2"], shard["u2"]], True))
    gathered(("g2", "u2"), got)
    pooled = _pool_fwd(u.reshape(B, S, -1), B, S).reshape(T, -1)
    x2, mixed = _mix_out_fwd(x1, o_sb.reshape(T, -1), pooled, wp, pool_scale, full["w_out"])
    memn, kv = _mem_kv_fwd(mem, mem_kv_norm, full["w_kv"])
    (x3, hq, q, ocat), got = _cross_fwd(x2, mem_q_norm, kv, full["w_q"], full["w_o"], B, S,
                                        comm=([shard["d2"]], True))
    gathered(("d2",), got)
    (dx4, hn4, a2, s2, t2, d_final, loss_part), _ = _ffn_fwd(x3, ffn2_norm, ffn_w["g2"], ffn_w["u2"], ffn_w["d2"],
                                                            "ffn2_fwd", head=(final_gain, target))

    slab = lambda k: grads[k].reshape((N_DEV, -1) + grads[k].shape[-1:])
    got = {}
    dx3, dg2, du2, dyh2, d_ffn2 = _ffn_bwd(dx4, x3, ffn2_norm, s2, t2, ffn_w["g2"], ffn_w["u2"],
                                          ffn_w["d2"], "ffn2_bwd")
    grads = dict(g2=_wgrad(dg2, hn4, "dw_gate2"), u2=_wgrad(du2, hn4, "dw_up2"), d2=_wgrad(a2, dyh2, "dw_down2"))
    (dx2, dx3b, dqb, dkv, d_q), (got["g2"],) = _cross_bwd(dx3, x2, mem_q_norm, q, kv, full["w_q"], full["w_o"], B, S,
                                                         comm=([slab("g2")], False))
    grads["w_o"] = _wgrad(ocat, dx3b, "dw_o")
    grads["w_q"] = _wgrad(hq, dqb, "dw_q")
    dkvb, d_kv = _mem_kv_bwd(dkv, mem, full["w_kv"])
    grads["w_kv"] = _wgrad(memn, dkvb, "dw_kv", col_slab=full["w_kv"].shape[2])
    dx2b, do_sb, dpooled, d_wpool, d_ps = _mix_out_bwd(dx2, pooled, wp, pool_scale, full["w_out"])
    grads["w_out"] = _wgrad(mixed, dx2b, "dw_out")
    du = _pool_bwd(dpooled.reshape(B, S, -1), B, S).reshape(T, -1)
    early = ("u2", "d2", "w_o", "w_q", "w_kv", "w_out")
    (dq, dk, dv), res = _sb_bwd(qkv3, do_sb.reshape(B, S, -1), B, S, comm=([slab(k) for k in early], False))
    got.update(zip(early, res))
    dx1, dproj, d_mix = _mix_in_bwd(dx2, dq.reshape(T, -1), dk.reshape(T, -1), dv.reshape(T, -1), du,
                                    x1, mix_norm, full["w_in"])
    grads["w_in"] = _wgrad(hn2, dproj, "dw_in", col_slab=full["w_in"].shape[2])
    dx0, dg1, du1, dyh1, d_ffn1 = _ffn_bwd(dx1, x0, ffn1_norm, s1, t1, ffn_w["g1"], ffn_w["u1"],
                                          ffn_w["d1"], "ffn1_bwd")

    small = [("ffn1_norm", d_ffn1, ffn1_norm, m_ffn1_norm, v_ffn1_norm),
             ("mix_norm", d_mix, mix_norm, m_mix_norm, v_mix_norm),
             ("w_pool", d_wpool, w_pool, m_w_pool, v_w_pool),
             ("pool_scale", d_ps, pool_scale, m_pool_scale, v_pool_scale),
             ("mem_q_norm", d_q, mem_q_norm, m_mem_q_norm, v_mem_q_norm),
             ("mem_kv_norm", d_kv, mem_kv_norm, m_mem_kv_norm, v_mem_kv_norm),
             ("ffn2_norm", d_ffn2, ffn2_norm, m_ffn2_norm, v_ffn2_norm),
             ("final_norm", d_final, final_norm, m_final_norm, v_final_norm)]
    rows = [max(8, t[2].size // LANES) for t in small]
    pack = lambda idx: jnp.concatenate([_rows128(t[idx], r) for t, r in zip(small, rows)]
                                       + ([loss_part] if idx == 1 else [jnp.zeros((8, LANES), F32)]))
    grads["g1"], (small_parts, got["w_in"]) = _wgrad(dg1, hn1, "dw_gate1",
                                                     comm=([pack(1), slab("w_in")], [True, False]))
    grads["u1"], (got["g1"],) = _wgrad(du1, hn1, "dw_up1", comm=([slab("g1")], False))
    grads["d1"], (got["u1"],) = _wgrad(a1, dyh1, "dw_down1", comm=([slab("u1")], False))
    got["d1"] = _exchange([slab("d1")], False, "scatter_last")[0]

    state = dict(
        g1=(ffn1_w_gate, m_ffn1_w_gate, v_ffn1_w_gate), u1=(ffn1_w_up, m_ffn1_w_up, v_ffn1_w_up),
        d1=(ffn1_w_down, m_ffn1_w_down, v_ffn1_w_down), g2=(ffn2_w_gate, m_ffn2_w_gate, v_ffn2_w_gate),
        u2=(ffn2_w_up, m_ffn2_w_up, v_ffn2_w_up), d2=(ffn2_w_down, m_ffn2_w_down, v_ffn2_w_down),
        w_in=(w_in, m_w_in, v_w_in), w_out=(w_out, m_w_out, v_w_out), w_q=(mem_w_q, m_mem_w_q, v_mem_w_q),
        w_kv=(mem_w_kv, m_mem_w_kv, v_mem_w_kv), w_o=(mem_w_o, m_mem_w_o, v_mem_w_o))
    big_out = {}
    for k in names:
        parts = got[k]
        if k in ("g1", "u1", "g2", "u2"):
            parts = jnp.swapaxes(parts, 1, 2)
        w_, m_, v_ = (t[0] for t in state[k])
        big_out[k] = [t[None] for t in _adamw(parts, w_, m_, v_, "adamw_" + k)]

    small_res = _adamw(small_parts, pack(2), pack(3), pack(4), "adamw_small")
    small_out, off = {}, 0
    for t, r in zip(small, rows):
        n = t[2].size // LANES
        small_out[t[0]] = [res[off:off + n].reshape(t[2].shape) for res in small_res]
        off += r
    loss = small_res[0][off, 0]

    order = [("ffn1_norm", None), ("ffn1_w_gate", "g1"), ("ffn1_w_up", "u1"), ("ffn1_w_down", "d1"),
             ("mix_norm", None), ("w_in", "w_in"), ("w_pool", None), ("pool_scale", None), ("w_out", "w_out"),
             ("mem_q_norm", None), ("mem_kv_norm", None), ("mem_w_q", "w_q"), ("mem_w_kv", "w_kv"),
             ("mem_w_o", "w_o"), ("ffn2_norm", None), ("ffn2_w_gate", "g2"), ("ffn2_w_up", "u2"),
             ("ffn2_w_down", "d2"), ("final_norm", None)]
    res = [loss, dx0.reshape(B, S, D)]
    for which in range(4):
        for name, key in order:
            res.append(big_out[key][which] if key else small_out[name][which])
    return tuple(res)
```

```python
import functools

import jax
import jax.numpy as jnp
from jax import lax
from jax.experimental import pallas as pl
from jax.experimental.pallas import tpu as pltpu

F32 = jnp.float32
BF16 = jnp.bfloat16

N_DEV = 8
EPS = 1e-6
SB_HEAD_DIM = 64
LANES = 128
POOL_WINDOWS = (2, 4, 8, 16)
POOL_GROUP_DIM = 128
MEM_HEADS = 4
FFN_RESIDUAL_WEIGHT = 0.5
ADAM_LR = 0.001
ADAM_B1 = 0.9
ADAM_B2 = 0.999
ADAM_EPS = 1e-08
ADAM_WD = 0.01
ADAM_STEP = 10
VMEM_LIMIT = 56 * 1024 * 1024

MESH_ID = pl.DeviceIdType.MESH


def _params(*sem):
    return pltpu.CompilerParams(dimension_semantics=sem, vmem_limit_bytes=VMEM_LIMIT)


def _tile(n, pref):
    if n <= pref:
        return n
    t = pref - pref % 8
    while n % t:
        t -= 8
    return t


def _mm(a, b):
    return jnp.dot(a, b, preferred_element_type=F32)


def _mm_nt(a, b):
    return lax.dot_general(a, b, (((1,), (1,)), ((), ())), preferred_element_type=F32)


def _mm_tn(a, b):
    return lax.dot_general(a, b, (((0,), (0,)), ((), ())), preferred_element_type=F32)


def _rms(xv):
    r = lax.rsqrt(jnp.mean(xv * xv, axis=-1, keepdims=True) + EPS)
    return r, xv * r


def _rms_bwd(dhn, gain, r, xhat):
    dxh = dhn * gain
    return r * (dxh - xhat * jnp.mean(dxh * xhat, axis=-1, keepdims=True))


def _sigmoid(z):
    return 0.5 * jnp.tanh(0.5 * z) + 0.5


def _flags(arrs, gather):
    return [gather] * len(arrs) if isinstance(gather, bool) else list(gather)


def _comm_shapes(arrs, gather):
    return tuple(jax.ShapeDtypeStruct(((N_DEV,) + tuple(a.shape)) if f else tuple(a.shape), a.dtype)
                 for a, f in zip(arrs, _flags(arrs, gather)))


def _comm_start(ins, outs, sems, gather):
    send_sems, recv_sems, local_sems = sems
    gather = _flags(ins, gather)
    x, y, c = lax.axis_index("x"), lax.axis_index("y"), lax.axis_index("c")
    me = 4 * x + 2 * y + c
    for i in range(len(ins)):
        src = ins[i] if gather[i] else ins[i].at[me]
        pltpu.make_async_copy(src, outs[i].at[me], local_sems.at[i]).start()
    for k in range(1, N_DEV):
        px = 1 - x if k & 4 else x
        py = 1 - y if k & 2 else y
        pc = 1 - c if k & 1 else c
        peer = 4 * px + 2 * py + pc
        for i in range(len(ins)):
            src = ins[i] if gather[i] else ins[i].at[peer]
            pltpu.make_async_remote_copy(
                src_ref=src, dst_ref=outs[i].at[me],
                send_sem=send_sems.at[i], recv_sem=recv_sems.at[i],
                device_id=(px, py, pc), device_id_type=MESH_ID).start()


def _comm_wait(ins, outs, sems, gather):
    send_sems, recv_sems, local_sems = sems
    gather = _flags(ins, gather)
    x, y, c = lax.axis_index("x"), lax.axis_index("y"), lax.axis_index("c")
    me = 4 * x + 2 * y + c
    for i in range(len(ins)):
        seven = outs[i].at[pl.ds(0, N_DEV - 1)]
        done = pltpu.make_async_remote_copy(
            src_ref=seven, dst_ref=seven,
            send_sem=send_sems.at[i], recv_sem=recv_sems.at[i],
            device_id=(x, y, c), device_id_type=MESH_ID)
        done.wait_send()
        done.wait_recv()
        src = ins[i] if gather[i] else ins[i].at[me]
        pltpu.make_async_copy(src, outs[i].at[me], local_sems.at[i]).wait()


def _comm_sems(n):
    return [pltpu.SemaphoreType.DMA((n,)) for _ in range(3)]


def _exchange(arrs, gather, name):
    n = len(arrs)

    def body(*refs):
        ins, outs, sems = refs[:n], refs[n:2 * n], refs[2 * n:]
        _comm_start(ins, outs, sems, gather)
        _comm_wait(ins, outs, sems, gather)

    any_spec = pl.BlockSpec(memory_space=pl.ANY)
    outs = pl.pallas_call(
        body, name=name, out_shape=_comm_shapes(arrs, gather),
        in_specs=[any_spec] * n, out_specs=tuple([any_spec] * n), scratch_shapes=_comm_sems(n),
    )(*arrs)
    return list(outs)


def _gather_two_level(arrs, name):
    n = len(arrs)

    def body(*refs):
        ins, outs = refs[:n], refs[n:2 * n]
        send_sems, recv_sems, local_sems = refs[2 * n:]
        x, y, c = lax.axis_index("x"), lax.axis_index("y"), lax.axis_index("c")
        me, sibling = (x, y, c), (x, y, 1 - c)
        chips = [(1 - x, y), (x, 1 - y), (1 - x, 1 - y)]

        def copy(i, k, block, to, own=False):
            slab = outs[i].at[4 * block[0] + 2 * block[1] + block[2]]
            return pltpu.make_async_remote_copy(
                src_ref=ins[i] if own else slab, dst_ref=slab,
                send_sem=send_sems.at[i, k], recv_sem=recv_sems.at[i, k],
                device_id=to, device_id_type=MESH_ID)

        mine = [pltpu.make_async_copy(ins[i], outs[i].at[4 * x + 2 * y + c], local_sems.at[i]) for i in range(n)]
        first = [copy(i, 0, me, sibling, own=True) for i in range(n)]
        first += [copy(i, 1 + j, me, (*chip, c), own=True) for j, chip in enumerate(chips) for i in range(n)]
        for cp in mine + first:
            cp.start()
        passed = []
        for j, chip in enumerate(chips):
            for i in range(n):
                copy(i, 1 + j, (*chip, c), me).wait_recv()
                passed.append(copy(i, 4 + j, (*chip, c), sibling))
                passed[-1].start()
        for i in range(n):
            copy(i, 0, sibling, me).wait_recv()
            for j, chip in enumerate(chips):
                copy(i, 4 + j, (*chip, 1 - c), me).wait_recv()
        for cp in first + passed:
            cp.wait_send()
        for cp in mine:
            cp.wait()

    any_spec = pl.BlockSpec(memory_space=pl.ANY)
    outs = pl.pallas_call(
        body, name=name, out_shape=_comm_shapes(arrs, True),
        in_specs=[any_spec] * n, out_specs=tuple([any_spec] * n),
        scratch_shapes=[pltpu.SemaphoreType.DMA((n, N_DEV - 1)), pltpu.SemaphoreType.DMA((n, N_DEV - 1)),
                        pltpu.SemaphoreType.DMA((n,))],
    )(*arrs)
    return list(outs)


def _call(body, *, name, grid, in_specs, out_specs, out_shape, args, scratch=(), comm=None):
    sem = ("arbitrary",) * len(grid)
    if comm is None:
        res = pl.pallas_call(body, name=name, grid=grid, in_specs=list(in_specs), out_specs=tuple(out_specs),
                             out_shape=tuple(out_shape), scratch_shapes=list(scratch),
                             compiler_params=_params(*sem))(*args)
        return tuple(res), []
    arrs, gather = comm
    n, n_in, n_out, n_sc = len(arrs), len(args), len(out_shape), len(scratch)

    def wrapped(*refs):
        ins, cin = refs[:n_in], refs[n_in:n_in + n]
        outs, cout = refs[n_in + n:n_in + n + n_out], refs[n_in + n + n_out:n_in + 2 * n + n_out]
        sc, sems = refs[n_in + 2 * n + n_out:n_in + 2 * n + n_out + n_sc], refs[n_in + 2 * n + n_out + n_sc:]
        ids = [pl.program_id(a) for a in range(len(grid))]
        first = functools.reduce(jnp.logical_and, [i == 0 for i in ids])
        last = functools.reduce(jnp.logical_and, [i == g - 1 for i, g in zip(ids, grid)])

        @pl.when(first)
        def _():
            _comm_start(cin, cout, sems, gather)

        body(*ins, *outs, *sc)

        @pl.when(last)
        def _():
            _comm_wait(cin, cout, sems, gather)

    any_spec = pl.BlockSpec(memory_space=pl.ANY)
    res = pl.pallas_call(
        wrapped, name=name, grid=grid, in_specs=list(in_specs) + [any_spec] * n,
        out_specs=tuple(out_specs) + (any_spec,) * n, out_shape=tuple(out_shape) + _comm_shapes(arrs, gather),
        scratch_shapes=list(scratch) + _comm_sems(n), compiler_params=_params(*sem))(*args, *arrs)
    return tuple(res[:n_out]), list(res[n_out:])


def _load_resident(pairs, sem):
    copies = [pltpu.make_async_copy(src, dst, sem.at[k]) for k, (src, dst) in enumerate(pairs)]
    for cp in copies:
        cp.start()
    for cp in copies:
        cp.wait()


def _ffn_fwd(x, gain, wgt, wut, wd, name, comm=None, head=None):
    T, D = x.shape
    F = wd.shape[0]
    tm, tf = _tile(T, 512), _tile(F, 256)

    def body(*refs):
        if head:
            (x_ref, gain_ref, wg_hbm, wu_hbm, wd_hbm, fgain_ref, tgt_ref,
             out_ref, hn_ref, a_ref, s_ref, t_ref, dfgain_ref, loss_ref, wg_s, wu_s, wd_s, sem) = refs
        else:
            (x_ref, gain_ref, wg_hbm, wu_hbm, wd_hbm,
             out_ref, hn_ref, a_ref, s_ref, t_ref, wg_s, wu_s, wd_s, sem) = refs

        @pl.when(pl.program_id(0) == 0)
        def _():
            _load_resident([(wg_hbm, wg_s), (wu_hbm, wu_s), (wd_hbm, wd_s)], sem)
            if head:
                dfgain_ref[...] = jnp.zeros_like(dfgain_ref)
                loss_ref[...] = jnp.zeros_like(loss_ref)

        _, xhat = _rms(x_ref[...])
        hn = (xhat * gain_ref[...]).astype(BF16)
        hn_ref[...] = hn
        for f0 in range(0, F, tf):
            cols = slice(f0, f0 + tf)
            g = _mm_nt(hn, wg_s[cols, :])
            u = _mm_nt(hn, wu_s[cols, :])
            sig = _sigmoid(g)
            s = g * sig
            a_ref[:, cols] = (s * u).astype(BF16)
            s_ref[:, cols] = s.astype(BF16)
            t_ref[:, cols] = (u * (sig + s * (1.0 - sig))).astype(BF16)
        y = x_ref[...] + FFN_RESIDUAL_WEIGHT * _mm(a_ref[...], wd_s[...])
        if head:
            r, yhat = _rms(y)
            err = yhat * fgain_ref[...] - tgt_ref[...]
            loss_ref[...] += 0.5 * jnp.sum(jnp.mean(err * err, axis=-1, keepdims=True), axis=0, keepdims=True)
            dy = err * (1.0 / D)
            dfgain_ref[...] += jnp.sum(dy * yhat, axis=0, keepdims=True)
            out_ref[...] = _rms_bwd(dy, fgain_ref[...], r, yhat)
        else:
            out_ref[...] = y

    row = lambda i: (i, 0)
    one = pl.BlockSpec((1, D), lambda i: (0, 0))
    hbm = pl.BlockSpec(memory_space=pl.ANY)
    in_specs = [pl.BlockSpec((tm, D), row), one, hbm, hbm, hbm]
    out_specs = [pl.BlockSpec((tm, D), row), pl.BlockSpec((tm, D), row)] + [pl.BlockSpec((tm, F), row) for _ in range(3)]
    out_shape = [jax.ShapeDtypeStruct((T, D), F32), jax.ShapeDtypeStruct((T, D), BF16)] \
        + [jax.ShapeDtypeStruct((T, F), BF16) for _ in range(3)]
    args = (x, gain, wgt, wut, wd)
    if head:
        in_specs += [one, pl.BlockSpec((tm, D), row)]
        out_specs += [one, pl.BlockSpec((8, LANES), lambda i: (0, 0))]
        out_shape += [jax.ShapeDtypeStruct((1, D), F32), jax.ShapeDtypeStruct((8, LANES), F32)]
        args += tuple(head)
    return _call(
        body, name=name, grid=(T // tm,), comm=comm, in_specs=in_specs, out_specs=out_specs, out_shape=out_shape,
        scratch=[pltpu.VMEM((F, D), BF16) for _ in range(3)] + [pltpu.SemaphoreType.DMA((3,))], args=args)


def _ffn_bwd(dy, x, gain, s, t, wgt, wut, wd, name):
    T, D = x.shape
    F = wd.shape[0]
    tr, tf = _tile(T, 512), _tile(F, 256)
    rows = lambda i: (i, 0)
    one = pl.BlockSpec((1, D), lambda i: (0, 0))
    any_spec = pl.BlockSpec(memory_space=pl.ANY)

    def act_body(dy_ref, s_ref, t_ref, wd_hbm, dg_ref, du_ref, dyh_ref, wd_s, sem):
        @pl.when(pl.program_id(0) == 0)
        def _():
            _load_resident([(wd_hbm, wd_s)], sem)

        dyh = (FFN_RESIDUAL_WEIGHT * dy_ref[...]).astype(BF16)
        dyh_ref[...] = dyh
        for f0 in range(0, F, tf):
            cols = slice(f0, f0 + tf)
            da = _mm_nt(dyh, wd_s[cols, :])
            dg_ref[:, cols] = (da * t_ref[:, cols].astype(F32)).astype(BF16)
            du_ref[:, cols] = (da * s_ref[:, cols].astype(F32)).astype(BF16)

    wide = jax.ShapeDtypeStruct((T, F), BF16)
    dg, du, dyh = pl.pallas_call(
        act_body, name=name + "_act", grid=(T // tr,),
        in_specs=[pl.BlockSpec((tr, D), rows), pl.BlockSpec((tr, F), rows), pl.BlockSpec((tr, F), rows), any_spec],
        out_specs=(pl.BlockSpec((tr, F), rows), pl.BlockSpec((tr, F), rows), pl.BlockSpec((tr, D), rows)),
        out_shape=(wide, wide, jax.ShapeDtypeStruct((T, D), BF16)),
        scratch_shapes=[pltpu.VMEM((F, D), BF16), pltpu.SemaphoreType.DMA((1,))],
        compiler_params=_params("arbitrary"),
    )(dy, s, t, wd)

    def in_body(dg_ref, du_ref, dy_ref, x_ref, gain_ref, wg_hbm, wu_hbm, dx_ref, dgain_ref, wg_s, wu_s, sem):
        @pl.when(pl.program_id(0) == 0)
        def _():
            _load_resident([(wg_hbm, wg_s), (wu_hbm, wu_s)], sem)
            dgain_ref[...] = jnp.zeros_like(dgain_ref)

        dhn = _mm(dg_ref[...], wg_s[...]) + _mm(du_ref[...], wu_s[...])
        r, xhat = _rms(x_ref[...])
        dgain_ref[...] += jnp.sum(dhn * xhat, axis=0, keepdims=True)
        dx_ref[...] = dy_ref[...] + _rms_bwd(dhn, gain_ref[...], r, xhat)

    dx, dgain = pl.pallas_call(
        in_body, name=name + "_in", grid=(T // tr,),
        in_specs=[pl.BlockSpec((tr, F), rows), pl.BlockSpec((tr, F), rows), pl.BlockSpec((tr, D), rows),
                  pl.BlockSpec((tr, D), rows), one, any_spec, any_spec],
        out_specs=(pl.BlockSpec((tr, D), rows), one),
        out_shape=(jax.ShapeDtypeStruct((T, D), F32), jax.ShapeDtypeStruct((1, D), F32)),
        scratch_shapes=[pltpu.VMEM((F, D), BF16), pltpu.VMEM((F, D), BF16), pltpu.SemaphoreType.DMA((2,))],
        compiler_params=_params("arbitrary"),
    )(dg, du, dy, x, gain, wgt, wut)
    return dx, dg, du, dyh, dgain


def _wgrad(a, b, name, col_slab=None, comm=None):
    T, M = a.shape
    N = b.shape[1]
    tmm = M if M <= 1024 else _tile(M, 1408)
    tn = _tile(N, 1024)
    tk = _tile(T, 1024)
    nk = T // tk
    per = tn // col_slab if col_slab else 0

    def body(a_ref, b_ref, out_ref, acc):
        k = pl.program_id(2)

        @pl.when(k == 0)
        def _():
            acc[...] = jnp.zeros_like(acc)

        acc[...] += _mm_tn(a_ref[...], b_ref[...])

        @pl.when(k == nk - 1)
        def _():
            if col_slab:
                for s in range(per):
                    out_ref[s] = acc[:, s * col_slab:(s + 1) * col_slab].astype(BF16)
            else:
                out_ref[...] = acc[...].astype(BF16)

    if col_slab:
        out_spec = pl.BlockSpec((per, tmm, col_slab), lambda m, n, k: (n, m, 0))
        out_shape = jax.ShapeDtypeStruct((N // col_slab, M, col_slab), BF16)
    else:
        out_spec = pl.BlockSpec((tmm, tn), lambda m, n, k: (m, n))
        out_shape = jax.ShapeDtypeStruct((M, N), BF16)
    (out,), got = _call(
        body, name=name, grid=(M // tmm, N // tn, nk), comm=comm,
        in_specs=[pl.BlockSpec((tk, tmm), lambda m, n, k: (k, m)), pl.BlockSpec((tk, tn), lambda m, n, k: (k, n))],
        out_specs=(out_spec,), out_shape=(out_shape,),
        scratch=[pltpu.VMEM((tmm, tn), F32)], args=(a, b))
    return (out, got) if comm else out


def _mix_in_fwd(x, gain, w_in):
    T, D = x.shape
    cs = w_in.shape[2]
    n_qkv = 3 * (N_DEV // 4)
    tm = _tile(T, 1024)

    def body(x_ref, gain_ref, w_ref, hn_ref, qkv_ref, u_ref):
        _, xhat = _rms(x_ref[...])
        hn = (xhat * gain_ref[...]).astype(BF16)
        hn_ref[...] = hn
        for b in range(N_DEV):
            p = _mm(hn, w_ref[b])
            if b < n_qkv:
                qkv_ref[:, b * cs:(b + 1) * cs] = p.astype(BF16)
            else:
                u_ref[:, (b - n_qkv) * cs:(b - n_qkv + 1) * cs] = p

    row = lambda i: (i, 0)
    return pl.pallas_call(
        body, name="mix_in_fwd", grid=(T // tm,),
        in_specs=[pl.BlockSpec((tm, D), row), pl.BlockSpec((1, D), lambda i: (0, 0)),
                  pl.BlockSpec((N_DEV, D, cs), lambda i: (0, 0, 0))],
        out_specs=(pl.BlockSpec((tm, D), row), pl.BlockSpec((tm, n_qkv * cs), row),
                   pl.BlockSpec((tm, (N_DEV - n_qkv) * cs), row)),
        out_shape=(jax.ShapeDtypeStruct((T, D), BF16), jax.ShapeDtypeStruct((T, n_qkv * cs), BF16),
                   jax.ShapeDtypeStruct((T, (N_DEV - n_qkv) * cs), F32)),
        compiler_params=_params("arbitrary"),
    )(x, gain, w_in)


def _mix_in_bwd(dres, dq, dk, dv, du, x, gain, w_in):
    T, D = x.shape
    cs = w_in.shape[2]
    W = dq.shape[1]
    per = W // cs
    tm = _tile(T, 512)

    def body(dres_ref, dq_ref, dk_ref, dv_ref, du_ref, x_ref, gain_ref, w_ref, dx_ref, dproj_ref, dgain_ref):
        i = pl.program_id(0)

        @pl.when(i == 0)
        def _():
            dgain_ref[...] = jnp.zeros_like(dgain_ref)

        dhn = jnp.zeros((tm, D), F32)
        for part, ref in enumerate((dq_ref, dk_ref, dv_ref, du_ref)):
            for h in range(per):
                b = part * per + h
                d = ref[:, h * cs:(h + 1) * cs]
                dproj_ref[:, b * cs:(b + 1) * cs] = d
                dhn = dhn + _mm_nt(d, w_ref[b])
        r, xhat = _rms(x_ref[...])
        dgain_ref[...] += jnp.sum(dhn * xhat, axis=0, keepdims=True)
        dx_ref[...] = dres_ref[...] + _rms_bwd(dhn, gain_ref[...], r, xhat)

    row = lambda i: (i, 0)
    one = pl.BlockSpec((1, D), lambda i: (0, 0))
    part = pl.BlockSpec((tm, W), row)
    return pl.pallas_call(
        body, name="mix_in_bwd", grid=(T // tm,),
        in_specs=[pl.BlockSpec((tm, D), row), part, part, part, part, pl.BlockSpec((tm, D), row), one,
                  pl.BlockSpec((N_DEV, D, cs), lambda i: (0, 0, 0))],
        out_specs=(pl.BlockSpec((tm, D), row), pl.BlockSpec((tm, 4 * W), row), one),
        out_shape=(jax.ShapeDtypeStruct((T, D), F32), jax.ShapeDtypeStruct((T, 4 * W), BF16),
                   jax.ShapeDtypeStruct((1, D), F32)),
        compiler_params=_params("arbitrary"),
    )(dres, dq, dk, dv, du, x, gain, w_in)


SB_PAIRS_PER_PROGRAM = 2
LOG2_E = 1.4426950408889634
EXP2_CLAMP = 126.0


def _neg_log2_sigmoid(nz2):
    w = jnp.minimum(nz2, EXP2_CLAMP)
    return w, jnp.log2(1.0 + jnp.exp2(w))


SB_DEAD_LOG2 = -160.0


def _sb_live(rests):
    worst = functools.reduce(jnp.maximum, rests)
    return (jnp.max(worst) > SB_DEAD_LOG2).astype(jnp.int32)


def _split(v):
    hi = v.astype(BF16)
    return hi, (v - hi.astype(F32)).astype(BF16)


def _tri_sum(v, tri):
    hi, lo = _split(v)
    return _mm(hi, tri) + _mm(lo, tri)


def _sb_fwd(qkv, B, S, comm=None):
    W = qkv.shape[2] // 3
    n_pair = W // LANES
    bq = _tile(S, 256)
    nq = S // bq
    hp = SB_PAIRS_PER_PROGRAM
    nscale2 = -(SB_HEAD_DIM ** -0.5) * LOG2_E

    def body(q_ref, k_ref, v_ref, o_ref):
        lane = lax.broadcasted_iota(jnp.int32, (1, LANES), 1)
        head0 = lane < SB_HEAD_DIM
        rr = lax.broadcasted_iota(jnp.int32, (bq, bq), 0)
        cc = lax.broadcasted_iota(jnp.int32, (bq, bq), 1)
        strict = cc < rr
        after = jnp.where(rr > cc, 1.0, 0.0).astype(BF16)

        def blocks(heads, ks, carries, diag):
            n = range(len(heads))
            keep = (lambda t: jnp.where(strict, t, 0.0)) if diag else (lambda t: t)
            z = [_mm_nt(qh, k_ref[ks, cols]) for qh, cols in heads]
            wl = [_neg_log2_sigmoid(z[h] * nscale2) for h in n]
            lr = [keep(wl[h][0] - wl[h][1]) for h in n]
            parts = [_split(lr[h]) for h in n]
            suf = [_mm(parts[h][0], after) + _mm(parts[h][1], after) for h in n]
            a = [keep(jnp.exp2(suf[h] + carries[h][1] - wl[h][1])).astype(BF16) for h in n]
            o = [carries[h][0] + _mm(a[h], v_ref[ks, heads[h][1]]) for h in n]
            return tuple((o[h], carries[h][1] + (suf[h][:, :1] + lr[h][:, :1])) for h in n)

        def q_tile(i, _):
            qs = pl.ds(pl.multiple_of(i * bq, bq), bq)
            heads = []
            for pr in range(hp):
                cols = slice(pr * LANES, (pr + 1) * LANES)
                qv = q_ref[qs, cols]
                heads += [(jnp.where(head0, qv, jnp.zeros_like(qv)), cols),
                          (jnp.where(head0, jnp.zeros_like(qv), qv), cols)]
            zero = (jnp.zeros((bq, LANES), F32), jnp.zeros((bq, 1), F32))
            init = blocks(heads, qs, (zero,) * len(heads), True)

            def left(st):
                t, _, cr = st
                ks = pl.ds(pl.multiple_of((i - 1 - t) * bq, bq), bq)
                cr = blocks(heads, ks, cr, False)
                return t + 1, _sb_live([c for _, c in cr]), cr

            _, _, res = lax.while_loop(lambda st: jnp.logical_and(st[0] < i, st[1] > 0), left,
                                       (jnp.int32(0), _sb_live([c for _, c in init]), init))
            for pr in range(hp):
                o_ref[qs, heads[2 * pr][1]] = jnp.where(head0, res[2 * pr][0], res[2 * pr + 1][0]).astype(BF16)
            return 0

        lax.fori_loop(0, nq, q_tile, 0)

    def col(off):
        return pl.BlockSpec((None, S, hp * LANES), lambda b, p: (b, 0, off + p))

    n_pair //= hp
    return _call(
        body, name="sb_fwd", grid=(B, n_pair), comm=comm,
        in_specs=[col(0), col(n_pair), col(2 * n_pair)],
        out_specs=(col(0),),
        out_shape=(jax.ShapeDtypeStruct((B, S, W), BF16),),
        args=(qkv, qkv, qkv))


def _sb_bwd(qkv, do, B, S, comm=None):
    W = qkv.shape[2] // 3
    n_pair = W // LANES
    bq = _tile(S, 256)
    nq = S // bq
    hp = SB_PAIRS_PER_PROGRAM
    scale = SB_HEAD_DIM ** -0.5
    nscale2 = -scale * LOG2_E

    def body(q_ref, k_ref, v_ref, do_ref, dq_ref, dk_ref, dv_ref, dk_s, dv_s, e_s, sg_s, a_s):
        lane = lax.broadcasted_iota(jnp.int32, (1, LANES), 1)
        head0 = lane < SB_HEAD_DIM
        rr = lax.broadcasted_iota(jnp.int32, (bq, bq), 0)
        cc = lax.broadcasted_iota(jnp.int32, (bq, bq), 1)
        strict = cc < rr
        after = jnp.where(rr > cc, 1.0, 0.0).astype(BF16)
        before = jnp.where(rr < cc, 1.0, 0.0).astype(BF16)
        dk_s[...] = jnp.zeros_like(dk_s)
        dv_s[...] = jnp.zeros_like(dv_s)

        def weights(heads, ks, kb, rests, diag):
            n = range(len(heads))
            keep = (lambda t: jnp.where(strict, t, 0.0)) if diag else (lambda t: t)
            z = [_mm_nt(heads[h][0], k_ref[ks, heads[h][2]]) for h in n]
            da = [_mm_nt(heads[h][1], v_ref[ks, heads[h][2]]) for h in n]
            wl = [_neg_log2_sigmoid(z[h] * nscale2) for h in n]
            lr = [keep(wl[h][0] - wl[h][1]) for h in n]
            parts = [_split(lr[h]) for h in n]
            suf = [_mm(parts[h][0], after) + _mm(parts[h][1], after) for h in n]
            a = [keep(jnp.exp2(suf[h] + rests[h] - wl[h][1])) for h in n]
            for h in n:
                a_s[h * nq + kb] = a[h].astype(BF16)
                e_s[h * nq + kb] = a[h] * da[h]
                sg_s[h * nq + kb] = jnp.exp2(-wl[h][1])
            return tuple(rests[h] + (suf[h][:, :1] + lr[h][:, :1]) for h in n)

        def grads(heads, ks, kb, carries, diag):
            n = range(len(heads))
            keep = (lambda t: jnp.where(strict, t, 0.0)) if diag else (lambda t: t)
            e = [e_s[h * nq + kb] for h in n]
            parts = [_split(e[h]) for h in n]
            pex = [_mm(parts[h][0], before) + _mm(parts[h][1], before) for h in n]
            dz = [keep(e[h] - sg_s[h * nq + kb] * (e[h] + pex[h] + carries[h][1])).astype(BF16) for h in n]
            dq = [carries[h][0] + _mm(dz[h], k_ref[ks, heads[h][2]]) for h in n]
            for h in n:
                dk_s[ks, heads[h][2]] += _mm_tn(dz[h], heads[h][0])
                dv_s[ks, heads[h][2]] += _mm_tn(a_s[h * nq + kb], heads[h][1])
            return tuple((dq[h], carries[h][1] + (pex[h][:, bq - 1:] + e[h][:, bq - 1:])) for h in n)

        def q_tile(i, _):
            qs = pl.ds(pl.multiple_of(i * bq, bq), bq)
            heads = []
            for pr in range(hp):
                cols = slice(pr * LANES, (pr + 1) * LANES)
                qv, dov = q_ref[qs, cols], do_ref[qs, cols]
                zq, zd = jnp.zeros_like(qv), jnp.zeros_like(dov)
                heads += [(jnp.where(head0, qv, zq), jnp.where(head0, dov, zd), cols),
                          (jnp.where(head0, zq, qv), jnp.where(head0, zd, dov), cols)]
            key_block = lambda kb: pl.ds(pl.multiple_of(kb * bq, bq), bq)
            rests = weights(heads, qs, i, (jnp.zeros((bq, 1), F32),) * len(heads), True)

            def left(st):
                t, _, rs = st
                rs = weights(heads, key_block(i - 1 - t), i - 1 - t, rs, False)
                return t + 1, _sb_live(rs), rs

            n_left, _, _ = lax.while_loop(lambda st: jnp.logical_and(st[0] < i, st[1] > 0), left,
                                          (jnp.int32(0), _sb_live(rests), rests))
            zero = (jnp.zeros((bq, LANES), F32), jnp.zeros((bq, 1), F32))
            res = lax.fori_loop(0, n_left, lambda t, cr: grads(heads, key_block(i - n_left + t), i - n_left + t, cr, False),
                                (zero,) * len(heads))
            res = grads(heads, qs, i, res, True)
            for pr in range(hp):
                dq = jnp.where(head0, res[2 * pr][0], res[2 * pr + 1][0])
                dq_ref[qs, heads[2 * pr][2]] = (dq * scale).astype(BF16)
            return 0

        lax.fori_loop(0, nq, q_tile, 0)
        dk_ref[...] = (dk_s[...] * scale).astype(BF16)
        dv_ref[...] = dv_s[...].astype(BF16)

    def col(off):
        return pl.BlockSpec((None, S, hp * LANES), lambda b, p: (b, 0, off + p))

    n_pair //= hp
    shp = jax.ShapeDtypeStruct((B, S, W), BF16)
    slots = 2 * hp * nq
    return _call(
        body, name="sb_bwd", grid=(B, n_pair), comm=comm,
        in_specs=[col(0), col(n_pair), col(2 * n_pair), col(0)],
        out_specs=(col(0), col(0), col(0)),
        out_shape=(shp, shp, shp),
        scratch=[pltpu.VMEM((S, hp * LANES), F32), pltpu.VMEM((S, hp * LANES), F32),
                 pltpu.VMEM((slots, bq, bq), F32), pltpu.VMEM((slots, bq, bq), F32),
                 pltpu.VMEM((slots, bq, bq), BF16)],
        args=(qkv, qkv, qkv, do))


def _pool_counts(S):
    t = lax.broadcasted_iota(jnp.int32, (S, 1), 0)
    return t, [jnp.minimum(t + 1, w).astype(F32) for w in POOL_WINDOWS]


def _pool_fwd(u, B, S):
    W = u.shape[2]

    def body(u_ref, out_ref):
        t, counts = _pool_counts(S)
        for gi, w in enumerate(POOL_WINDOWS):
            cols = slice(gi * POOL_GROUP_DIM, (gi + 1) * POOL_GROUP_DIM)
            ug = u_ref[:, cols]
            s, k = ug, 1
            while k < w:
                s = s + jnp.where(t >= k, pltpu.roll(s, k, axis=0), 0.0)
                k *= 2
            out_ref[:, cols] = (s / counts[gi] - ug).astype(BF16)

    spec = pl.BlockSpec((None, S, W), lambda b: (b, 0, 0))
    return pl.pallas_call(
        body, name="pool_fwd", grid=(B,), in_specs=[spec], out_specs=spec,
        out_shape=jax.ShapeDtypeStruct((B, S, W), BF16), compiler_params=_params("arbitrary"),
    )(u)


def _pool_bwd(dpooled, B, S):
    W = dpooled.shape[2]

    def body(d_ref, out_ref):
        t, counts = _pool_counts(S)
        for gi, w in enumerate(POOL_WINDOWS):
            cols = slice(gi * POOL_GROUP_DIM, (gi + 1) * POOL_GROUP_DIM)
            d = d_ref[:, cols]
            s, k = d / counts[gi], 1
            while k < w:
                s = s + jnp.where(t < S - k, pltpu.roll(s, S - k, axis=0), 0.0)
                k *= 2
            out_ref[:, cols] = (s - d).astype(BF16)

    spec = pl.BlockSpec((None, S, W), lambda b: (b, 0, 0))
    return pl.pallas_call(
        body, name="pool_bwd", grid=(B,), in_specs=[spec], out_specs=spec,
        out_shape=jax.ShapeDtypeStruct((B, S, W), BF16), compiler_params=_params("arbitrary"),
    )(dpooled)


def _mix_out_fwd(x, o_sb, pooled, w_pool, pool_scale, w_out):
    T, D = x.shape
    W = o_sb.shape[1]
    G = w_pool.shape[0]
    gd = POOL_GROUP_DIM
    tm = _tile(T, 1024)

    def body(x_ref, osb_ref, pooled_ref, wp_ref, ps_ref, wo_ref, out_ref, mixed_ref):
        mixed_ref[:, :W] = osb_ref[...]
        for gi in range(G):
            cols = slice(gi * gd, (gi + 1) * gd)
            pw = _mm(pooled_ref[:, cols], wp_ref[gi])
            mixed_ref[:, W + gi * gd:W + (gi + 1) * gd] = (pw * ps_ref[:, cols]).astype(BF16)
        out_ref[...] = x_ref[...] + _mm(mixed_ref[...], wo_ref[...].reshape(D, D))

    row = lambda i: (i, 0)
    return pl.pallas_call(
        body, name="mix_out_fwd", grid=(T // tm,),
        in_specs=[pl.BlockSpec((tm, D), row), pl.BlockSpec((tm, W), row), pl.BlockSpec((tm, W), row),
                  pl.BlockSpec((G, gd, gd), lambda i: (0, 0, 0)), pl.BlockSpec((1, W), lambda i: (0, 0)),
                  pl.BlockSpec(w_out.shape, lambda i: (0, 0, 0))],
        out_specs=(pl.BlockSpec((tm, D), row), pl.BlockSpec((tm, D), row)),
        out_shape=(jax.ShapeDtypeStruct((T, D), F32), jax.ShapeDtypeStruct((T, D), BF16)),
        compiler_params=_params("arbitrary"),
    )(x, o_sb, pooled, w_pool, pool_scale, w_out)


def _mix_out_bwd(dx, pooled, w_pool, pool_scale, w_out):
    T, D = dx.shape
    W = pooled.shape[1]
    G = w_pool.shape[0]
    gd = POOL_GROUP_DIM
    tm = _tile(T, 1024)

    def body(dx_ref, pooled_ref, wp_ref, ps_ref, wo_ref, dxb_ref, dosb_ref, dpooled_ref, dwp_ref, dps_ref):
        i = pl.program_id(0)

        @pl.when(i == 0)
        def _():
            dwp_ref[...] = jnp.zeros_like(dwp_ref)
            dps_ref[...] = jnp.zeros_like(dps_ref)

        dxb = dx_ref[...].astype(BF16)
        dxb_ref[...] = dxb
        dmixed = _mm_nt(dxb, wo_ref[...].reshape(D, D))
        dosb_ref[...] = dmixed[:, :W].astype(BF16)
        for gi in range(G):
            cols = slice(gi * gd, (gi + 1) * gd)
            pg = pooled_ref[:, cols]
            dop = dmixed[:, W + gi * gd:W + (gi + 1) * gd]
            pw = _mm(pg, wp_ref[gi])
            dps_ref[:, cols] += jnp.sum(dop * pw, axis=0, keepdims=True)
            dpw = (dop * ps_ref[:, cols]).astype(BF16)
            dwp_ref[gi] += _mm_tn(pg, dpw)
            dpooled_ref[:, cols] = _mm_nt(dpw, wp_ref[gi])

    row = lambda i: (i, 0)
    return pl.pallas_call(
        body, name="mix_out_bwd", grid=(T // tm,),
        in_specs=[pl.BlockSpec((tm, D), row), pl.BlockSpec((tm, W), row),
                  pl.BlockSpec((G, gd, gd), lambda i: (0, 0, 0)), pl.BlockSpec((1, W), lambda i: (0, 0)),
                  pl.BlockSpec(w_out.shape, lambda i: (0, 0, 0))],
        out_specs=(pl.BlockSpec((tm, D), row), pl.BlockSpec((tm, W), row), pl.BlockSpec((tm, W), row),
                   pl.BlockSpec((G, gd, gd), lambda i: (0, 0, 0)), pl.BlockSpec((1, W), lambda i: (0, 0))),
        out_shape=(jax.ShapeDtypeStruct((T, D), BF16), jax.ShapeDtypeStruct((T, W), BF16),
                   jax.ShapeDtypeStruct((T, W), F32), jax.ShapeDtypeStruct((G, gd, gd), F32),
                   jax.ShapeDtypeStruct((1, W), F32)),
        compiler_params=_params("arbitrary"),
    )(dx, pooled, w_pool, pool_scale, w_out)


def _mem_kv_fwd(mem, gain, w_kv):
    B, M, D = mem.shape
    cs = w_kv.shape[2]

    def body(mem_ref, gain_ref, w_ref, memn_ref, kv_ref):
        _, xhat = _rms(mem_ref[...])
        mn = (xhat * gain_ref[...]).astype(BF16)
        memn_ref[...] = mn
        for b in range(N_DEV):
            kv_ref[:, b * cs:(b + 1) * cs] = _mm(mn, w_ref[b]).astype(BF16)

    return pl.pallas_call(
        body, name="mem_kv_fwd", grid=(B,),
        in_specs=[pl.BlockSpec((None, M, D), lambda b: (b, 0, 0)), pl.BlockSpec((1, D), lambda b: (0, 0)),
                  pl.BlockSpec((N_DEV, D, cs), lambda b: (0, 0, 0))],
        out_specs=(pl.BlockSpec((M, D), lambda b: (b, 0)), pl.BlockSpec((None, M, N_DEV * cs), lambda b: (b, 0, 0))),
        out_shape=(jax.ShapeDtypeStruct((B * M, D), BF16), jax.ShapeDtypeStruct((B, M, N_DEV * cs), BF16)),
        compiler_params=_params("arbitrary"),
    )(mem, gain, w_kv)


def _mem_kv_bwd(dkv, mem, w_kv):
    B, M, D = mem.shape
    cs = w_kv.shape[2]

    def body(dkv_ref, mem_ref, w_ref, dkvb_ref, dgain_ref):
        b_id = pl.program_id(0)

        @pl.when(b_id == 0)
        def _():
            dgain_ref[...] = jnp.zeros_like(dgain_ref)

        dkvb = dkv_ref[...].astype(BF16)
        dkvb_ref[...] = dkvb
        dmn = jnp.zeros((M, D), F32)
        for b in range(N_DEV):
            dmn = dmn + _mm_nt(dkvb[:, b * cs:(b + 1) * cs], w_ref[b])
        _, xhat = _rms(mem_ref[...])
        dgain_ref[...] += jnp.sum(dmn * xhat, axis=0, keepdims=True)

    return pl.pallas_call(
        body, name="mem_kv_bwd", grid=(B,),
        in_specs=[pl.BlockSpec((None, M, N_DEV * cs), lambda b: (b, 0, 0)),
                  pl.BlockSpec((None, M, D), lambda b: (b, 0, 0)),
                  pl.BlockSpec((N_DEV, D, cs), lambda b: (0, 0, 0))],
        out_specs=(pl.BlockSpec((M, N_DEV * cs), lambda b: (b, 0)), pl.BlockSpec((1, D), lambda b: (0, 0))),
        out_shape=(jax.ShapeDtypeStruct((B * M, N_DEV * cs), BF16), jax.ShapeDtypeStruct((1, D), F32)),
        compiler_params=_params("arbitrary"),
    )(dkv, mem, w_kv)


def _softmax_rows(s):
    p = jnp.exp(s - jnp.max(s, axis=1, keepdims=True))
    return p / jnp.sum(p, axis=1, keepdims=True)


def _cross_fwd(x, gain, kv, w_q, w_o, B, S, comm=None):
    T, D = x.shape
    M = kv.shape[1]
    hd = D // MEM_HEADS
    tm = _tile(S, 1024)
    per = S // tm
    scale = hd ** -0.5

    def body(x_ref, gain_ref, kv_ref, wq_ref, wo_ref, out_ref, hq_ref, q_ref, ocat_ref):
        _, xhat = _rms(x_ref[...])
        hq = (xhat * gain_ref[...]).astype(BF16)
        hq_ref[...] = hq
        q = _mm(hq, wq_ref[...].reshape(D, D)).astype(BF16)
        q_ref[...] = q
        for h in range(MEM_HEADS):
            cols = slice(h * hd, (h + 1) * hd)
            s = _mm_nt(q[:, cols], kv_ref[:, cols]) * scale
            p = _softmax_rows(s).astype(BF16)
            ocat_ref[:, cols] = _mm(p, kv_ref[:, D + h * hd:D + (h + 1) * hd]).astype(BF16)
        out_ref[...] = x_ref[...] + _mm(ocat_ref[...], wo_ref[...].reshape(D, D))

    row = lambda b, t: (b * per + t, 0)
    wspec = pl.BlockSpec(w_q.shape, lambda b, t: (0, 0, 0))
    return _call(
        body, name="cross_fwd", grid=(B, per), comm=comm,
        in_specs=[pl.BlockSpec((tm, D), row), pl.BlockSpec((1, D), lambda b, t: (0, 0)),
                  pl.BlockSpec((None, M, 2 * D), lambda b, t: (b, 0, 0)), wspec, wspec],
        out_specs=tuple(pl.BlockSpec((tm, D), row) for _ in range(4)),
        out_shape=(jax.ShapeDtypeStruct((T, D), F32),) + tuple(jax.ShapeDtypeStruct((T, D), BF16) for _ in range(3)),
        args=(x, gain, kv, w_q, w_o))


def _cross_bwd(dy, x, gain, q, kv, w_q, w_o, B, S, comm=None):
    T, D = x.shape
    M = kv.shape[1]
    hd = D // MEM_HEADS
    tm = _tile(S, 512)
    per = S // tm
    scale = hd ** -0.5

    def body(dy_ref, x_ref, gain_ref, q_ref, kv_ref, wq_ref, wo_ref,
             dx_ref, dyb_ref, dqb_ref, dkv_ref, dgain_ref):
        b_id, t_id = pl.program_id(0), pl.program_id(1)

        @pl.when((b_id == 0) & (t_id == 0))
        def _():
            dgain_ref[...] = jnp.zeros_like(dgain_ref)

        @pl.when(t_id == 0)
        def _():
            dkv_ref[...] = jnp.zeros_like(dkv_ref)

        dyb = dy_ref[...].astype(BF16)
        dyb_ref[...] = dyb
        docat = _mm_nt(dyb, wo_ref[...].reshape(D, D)).astype(BF16)
        for h in range(MEM_HEADS):
            cols = slice(h * hd, (h + 1) * hd)
            vcols = slice(D + h * hd, D + (h + 1) * hd)
            qh, kh, vh, doh = q_ref[:, cols], kv_ref[:, cols], kv_ref[:, vcols], docat[:, cols]
            p = _softmax_rows(_mm_nt(qh, kh) * scale)
            dp = _mm_nt(doh, vh)
            ds = (p * (dp - jnp.sum(dp * p, axis=1, keepdims=True)) * scale).astype(BF16)
            dqb_ref[:, cols] = _mm(ds, kh).astype(BF16)
            dkv_ref[:, cols] += _mm_tn(ds, qh)
            dkv_ref[:, vcols] += _mm_tn(p.astype(BF16), doh)
        dhq = _mm_nt(dqb_ref[...], wq_ref[...].reshape(D, D))
        r, xhat = _rms(x_ref[...])
        dgain_ref[...] += jnp.sum(dhq * xhat, axis=0, keepdims=True)
        dx_ref[...] = dy_ref[...] + _rms_bwd(dhq, gain_ref[...], r, xhat)

    row = lambda b, t: (b * per + t, 0)
    wspec = pl.BlockSpec(w_q.shape, lambda b, t: (0, 0, 0))
    one = pl.BlockSpec((1, D), lambda b, t: (0, 0))
    kvspec = pl.BlockSpec((None, M, 2 * D), lambda b, t: (b, 0, 0))
    return _call(
        body, name="cross_bwd", grid=(B, per), comm=comm,
        in_specs=[pl.BlockSpec((tm, D), row), pl.BlockSpec((tm, D), row), one, pl.BlockSpec((tm, D), row),
                  kvspec, wspec, wspec],
        out_specs=(pl.BlockSpec((tm, D), row), pl.BlockSpec((tm, D), row), pl.BlockSpec((tm, D), row), kvspec, one),
        out_shape=(jax.ShapeDtypeStruct((T, D), F32), jax.ShapeDtypeStruct((T, D), BF16),
                   jax.ShapeDtypeStruct((T, D), BF16), jax.ShapeDtypeStruct((B, M, 2 * D), F32),
                   jax.ShapeDtypeStruct((1, D), F32)),
        args=(dy, x, gain, q, kv, w_q, w_o))


def _adamw(gparts, w, m, v, name):
    R, C = w.shape
    tr = _tile(R, 256)

    def body(gp_ref, w_ref, m_ref, v_ref, g_ref, d_ref, nm_ref, nv_ref):
        g = gp_ref[0].astype(F32)
        for s in range(1, N_DEV):
            g = g + gp_ref[s].astype(F32)
        nm = ADAM_B1 * m_ref[...] + (1.0 - ADAM_B1) * g
        nv = ADAM_B2 * v_ref[...] + (1.0 - ADAM_B2) * (g * g)
        m_hat = nm / (1.0 - ADAM_B1 ** ADAM_STEP)
        v_hat = nv / (1.0 - ADAM_B2 ** ADAM_STEP)
        g_ref[...] = g
        nm_ref[...] = nm
        nv_ref[...] = nv
        d_ref[...] = -ADAM_LR * (m_hat / (jnp.sqrt(v_hat) + ADAM_EPS) + ADAM_WD * w_ref[...])

    spec = pl.BlockSpec((tr, C), lambda i: (i, 0))
    shp = jax.ShapeDtypeStruct((R, C), F32)
    return pl.pallas_call(
        body, name=name, grid=(R // tr,),
        in_specs=[pl.BlockSpec((N_DEV, tr, C), lambda i: (0, i, 0)), spec, spec, spec],
        out_specs=(spec, spec, spec, spec), out_shape=(shp, shp, shp, shp),
        compiler_params=_params("arbitrary"),
    )(gparts, w, m, v)


def _rows128(a, rows):
    a = a.reshape(-1, LANES)
    return jnp.pad(a, ((0, rows - a.shape[0]), (0, 0)))


def kernel(x, mem, ffn1_norm, ffn1_w_gate, ffn1_w_up, ffn1_w_down, mix_norm, w_in, w_pool, pool_scale, w_out, mem_q_norm, mem_kv_norm, mem_w_q, mem_w_kv, mem_w_o, ffn2_norm, ffn2_w_gate, ffn2_w_up, ffn2_w_down, final_norm, loss_target, m_ffn1_norm, m_ffn1_w_gate, m_ffn1_w_up, m_ffn1_w_down, m_mix_norm, m_w_in, m_w_pool, m_pool_scale, m_w_out, m_mem_q_norm, m_mem_kv_norm, m_mem_w_q, m_mem_w_kv, m_mem_w_o, m_ffn2_norm, m_ffn2_w_gate, m_ffn2_w_up, m_ffn2_w_down, m_final_norm, v_ffn1_norm, v_ffn1_w_gate, v_ffn1_w_up, v_ffn1_w_down, v_mix_norm, v_w_in, v_w_pool, v_pool_scale, v_w_out, v_mem_q_norm, v_mem_kv_norm, v_mem_w_q, v_mem_w_kv, v_mem_w_o, v_ffn2_norm, v_ffn2_w_gate, v_ffn2_w_up, v_ffn2_w_down, v_final_norm):
    B, S, D = x.shape
    T = B * S
    x0 = x.reshape(T, D)
    target = loss_target.reshape(T, D)
    final_gain = final_norm.reshape(1, D)

    big = dict(
        g1=ffn1_w_gate[0].T, u1=ffn1_w_up[0].T, d1=ffn1_w_down[0],
        g2=ffn2_w_gate[0].T, u2=ffn2_w_up[0].T, d2=ffn2_w_down[0],
        w_in=w_in[0], w_out=w_out[0], w_q=mem_w_q[0], w_kv=mem_w_kv[0], w_o=mem_w_o[0])
    names = list(big)
    shard = {k: big[k].astype(BF16) for k in names}
    wp = w_pool[0].astype(BF16)
    full, ffn_w = {}, {}

    def gathered(keys, arrs):
        full.update(zip(keys, arrs))
        ffn_w.update({k: full[k].reshape(-1, D) for k in keys if k[0] in "gud"})

    first, mid = ("g1", "u1", "d1"), ("w_in", "w_out", "w_q", "w_kv", "w_o")
    gathered(first, _gather_two_level([shard[k] for k in first], "gather_ffn1"))
    (x1, hn1, a1, s1, t1), got = _ffn_fwd(x0, ffn1_norm, ffn_w["g1"], ffn_w["u1"], ffn_w["d1"], "ffn1_fwd",
                                          comm=([shard[k] for k in mid], True))
    gathered(mid, got)
    hn2, qkv, u = _mix_in_fwd(x1, mix_norm, full["w_in"])
    qkv3 = qkv.reshape(B, S, -1)
    (o_sb,), got = _sb_fwd(qkv3, B, S, comm=([shard["g2"], shard["u2"]], True))
    gathered(("g2", "u2"), got)
    pooled = _pool_fwd(u.reshape(B, S, -1), B, S).reshape(T, -1)
    x2, mixed = _mix_out_fwd(x1, o_sb.reshape(T, -1), pooled, wp, pool_scale, full["w_out"])
    memn, kv = _mem_kv_fwd(mem, mem_kv_norm, full["w_kv"])
    (x3, hq, q, ocat), got = _cross_fwd(x2, mem_q_norm, kv, full["w_q"], full["w_o"], B, S,
                                        comm=([shard["d2"]], True))
    gathered(("d2",), got)
    (dx4, hn4, a2, s2, t2, d_final, loss_part), _ = _ffn_fwd(x3, ffn2_norm, ffn_w["g2"], ffn_w["u2"], ffn_w["d2"],
                                                            "ffn2_fwd", head=(final_gain, target))

    slab = lambda k: grads[k].reshape((N_DEV, -1) + grads[k].shape[-1:])
    got = {}
    dx3, dg2, du2, dyh2, d_ffn2 = _ffn_bwd(dx4, x3, ffn2_norm, s2, t2, ffn_w["g2"], ffn_w["u2"],
                                          ffn_w["d2"], "ffn2_bwd")
    grads = dict(g2=_wgrad(dg2, hn4, "dw_gate2"), u2=_wgrad(du2, hn4, "dw_up2"), d2=_wgrad(a2, dyh2, "dw_down2"))
    (dx2, dx3b, dqb, dkv, d_q), (got["g2"],) = _cross_bwd(dx3, x2, mem_q_norm, q, kv, full["w_q"], full["w_o"], B, S,
                                                         comm=([slab("g2")], False))
    grads["w_o"] = _wgrad(ocat, dx3b, "dw_o")
    grads["w_q"] = _wgrad(hq, dqb, "dw_q")
    dkvb, d_kv = _mem_kv_bwd(dkv, mem, full["w_kv"])
    grads["w_kv"] = _wgrad(memn, dkvb, "dw_kv", col_slab=full["w_kv"].shape[2])
    dx2b, do_sb, dpooled, d_wpool, d_ps = _mix_out_bwd(dx2, pooled, wp, pool_scale, full["w_out"])
    grads["w_out"] = _wgrad(mixed, dx2b, "dw_out")
    du = _pool_bwd(dpooled.reshape(B, S, -1), B, S).reshape(T, -1)
    early = ("u2", "d2", "w_o", "w_q", "w_kv", "w_out")
    (dq, dk, dv), res = _sb_bwd(qkv3, do_sb.reshape(B, S, -1), B, S, comm=([slab(k) for k in early], False))
    got.update(zip(early, res))
    dx1, dproj, d_mix = _mix_in_bwd(dx2, dq.reshape(T, -1), dk.reshape(T, -1), dv.reshape(T, -1), du,
                                    x1, mix_norm, full["w_in"])
    grads["w_in"] = _wgrad(hn2, dproj, "dw_in", col_slab=full["w_in"].shape[2])
    dx0, dg1, du1, dyh1, d_ffn1 = _ffn_bwd(dx1, x0, ffn1_norm, s1, t1, ffn_w["g1"], ffn_w["u1"],
                                          ffn_w["d1"], "ffn1_bwd")

    small = [("ffn1_norm", d_ffn1, ffn1_norm, m_ffn1_norm, v_ffn1_norm),
             ("mix_norm", d_mix, mix_norm, m_mix_norm, v_mix_norm),
             ("w_pool", d_wpool, w_pool, m_w_pool, v_w_pool),
             ("pool_scale", d_ps, pool_scale, m_pool_scale, v_pool_scale),
             ("mem_q_norm", d_q, mem_q_norm, m_mem_q_norm, v_mem_q_norm),
             ("mem_kv_norm", d_kv, mem_kv_norm, m_mem_kv_norm, v_mem_kv_norm),
             ("ffn2_norm", d_ffn2, ffn2_norm, m_ffn2_norm, v_ffn2_norm),
             ("final_norm", d_final, final_norm, m_final_norm, v_final_norm)]
    rows = [max(8, t[2].size // LANES) for t in small]
    pack = lambda idx: jnp.concatenate([_rows128(t[idx], r) for t, r in zip(small, rows)]
                                       + ([loss_part] if idx == 1 else [jnp.zeros((8, LANES), F32)]))
    grads["g1"], (small_parts, got["w_in"]) = _wgrad(dg1, hn1, "dw_gate1",
                                                     comm=([pack(1), slab("w_in")], [True, False]))
    grads["u1"], (got["g1"],) = _wgrad(du1, hn1, "dw_up1", comm=([slab("g1")], False))
    grads["d1"], (got["u1"],) = _wgrad(a1, dyh1, "dw_down1", comm=([slab("u1")], False))
    got["d1"] = _exchange([slab("d1")], False, "scatter_last")[0]

    state = dict(
        g1=(ffn1_w_gate, m_ffn1_w_gate, v_ffn1_w_gate), u1=(ffn1_w_up, m_ffn1_w_up, v_ffn1_w_up),
        d1=(ffn1_w_down, m_ffn1_w_down, v_ffn1_w_down), g2=(ffn2_w_gate, m_ffn2_w_gate, v_ffn2_w_gate),
        u2=(ffn2_w_up, m_ffn2_w_up, v_ffn2_w_up), d2=(ffn2_w_down, m_ffn2_w_down, v_ffn2_w_down),
        w_in=(w_in, m_w_in, v_w_in), w_out=(w_out, m_w_out, v_w_out), w_q=(mem_w_q, m_mem_w_q, v_mem_w_q),
        w_kv=(mem_w_kv, m_mem_w_kv, v_mem_w_kv), w_o=(mem_w_o, m_mem_w_o, v_mem_w_o))
    big_out = {}
    for k in names:
        parts = got[k]
        if k in ("g1", "u1", "g2", "u2"):
            parts = jnp.swapaxes(parts, 1, 2)
        w_, m_, v_ = (t[0] for t in state[k])
        big_out[k] = [t[None] for t in _adamw(parts, w_, m_, v_, "adamw_" + k)]

    small_res = _adamw(small_parts, pack(2), pack(3), pack(4), "adamw_small")
    small_out, off = {}, 0
    for t, r in zip(small, rows):
        n = t[2].size // LANES
        small_out[t[0]] = [res[off:off + n].reshape(t[2].shape) for res in small_res]
        off += r
    loss = small_res[0][off, 0]

    order = [("ffn1_norm", None), ("ffn1_w_gate", "g1"), ("ffn1_w_up", "u1"), ("ffn1_w_down", "d1"),
             ("mix_norm", None), ("w_in", "w_in"), ("w_pool", None), ("pool_scale", None), ("w_out", "w_out"),
             ("mem_q_norm", None), ("mem_kv_norm", None), ("mem_w_q", "w_q"), ("mem_w_kv", "w_kv"),
             ("mem_w_o", "w_o"), ("ffn2_norm", None), ("ffn2_w_gate", "g2"), ("ffn2_w_up", "u2"),
             ("ffn2_w_down", "d2"), ("final_norm", None)]
    res = [loss, dx0.reshape(B, S, D)]
    for which in range(4):
        for name, key in order:
            res.append(big_out[key][which] if key else small_out[name][which])
    return tuple(res)
```

```python
import functools
import math

import jax
import jax.numpy as jnp
from jax import lax
from jax.experimental import pallas as pl
from jax.experimental.pallas import tpu as pltpu

F32 = jnp.float32
BF16 = jnp.bfloat16

N_DEV = 8
EPS = 1e-6
SB_HEAD_DIM = 64
LANES = 128
POOL_WINDOWS = (2, 4, 8, 16)
POOL_GROUP_DIM = 128
MEM_HEADS = 4
FFN_RESIDUAL_WEIGHT = 0.5
ADAM_LR = 0.001
ADAM_B1 = 0.9
ADAM_B2 = 0.999
ADAM_EPS = 1e-08
ADAM_WD = 0.01
ADAM_STEP = 10
VMEM_LIMIT = 56 * 1024 * 1024

MESH_ID = pl.DeviceIdType.MESH


def _params(*sem):
    return pltpu.CompilerParams(dimension_semantics=sem, vmem_limit_bytes=VMEM_LIMIT)


def _tile(n, pref):
    if n <= pref:
        return n
    t = pref - pref % 8
    while n % t:
        t -= 8
    return t


def _mm(a, b):
    return jnp.dot(a, b, preferred_element_type=F32)


def _mm_nt(a, b):
    return lax.dot_general(a, b, (((1,), (1,)), ((), ())), preferred_element_type=F32)


def _mm_tn(a, b):
    return lax.dot_general(a, b, (((0,), (0,)), ((), ())), preferred_element_type=F32)


def _rms(xv):
    r = lax.rsqrt(jnp.mean(xv * xv, axis=-1, keepdims=True) + EPS)
    return r, xv * r


def _rms_bwd(dhn, gain, r, xhat):
    dxh = dhn * gain
    return r * (dxh - xhat * jnp.mean(dxh * xhat, axis=-1, keepdims=True))


def _sigmoid(z):
    return 0.5 * jnp.tanh(0.5 * z) + 0.5


def _flags(arrs, gather):
    return [gather] * len(arrs) if isinstance(gather, bool) else list(gather)


def _comm_shapes(arrs, gather):
    return tuple(jax.ShapeDtypeStruct(((N_DEV,) + tuple(a.shape)) if f else tuple(a.shape), a.dtype)
                 for a, f in zip(arrs, _flags(arrs, gather)))


def _comm_start(ins, outs, sems, gather):
    send_sems, recv_sems, local_sems = sems
    gather = _flags(ins, gather)
    x, y, c = lax.axis_index("x"), lax.axis_index("y"), lax.axis_index("c")
    me = 4 * x + 2 * y + c
    for i in range(len(ins)):
        src = ins[i] if gather[i] else ins[i].at[me]
        pltpu.make_async_copy(src, outs[i].at[me], local_sems.at[i]).start()
    for k in range(1, N_DEV):
        px = 1 - x if k & 4 else x
        py = 1 - y if k & 2 else y
        pc = 1 - c if k & 1 else c
        peer = 4 * px + 2 * py + pc
        for i in range(len(ins)):
            src = ins[i] if gather[i] else ins[i].at[peer]
            pltpu.make_async_remote_copy(
                src_ref=src, dst_ref=outs[i].at[me],
                send_sem=send_sems.at[i], recv_sem=recv_sems.at[i],
                device_id=(px, py, pc), device_id_type=MESH_ID).start()


def _comm_wait(ins, outs, sems, gather):
    send_sems, recv_sems, local_sems = sems
    gather = _flags(ins, gather)
    x, y, c = lax.axis_index("x"), lax.axis_index("y"), lax.axis_index("c")
    me = 4 * x + 2 * y + c
    for i in range(len(ins)):
        seven = outs[i].at[pl.ds(0, N_DEV - 1)]
        done = pltpu.make_async_remote_copy(
            src_ref=seven, dst_ref=seven,
            send_sem=send_sems.at[i], recv_sem=recv_sems.at[i],
            device_id=(x, y, c), device_id_type=MESH_ID)
        done.wait_send()
        done.wait_recv()
        src = ins[i] if gather[i] else ins[i].at[me]
        pltpu.make_async_copy(src, outs[i].at[me], local_sems.at[i]).wait()


def _comm_sems(n):
    return [pltpu.SemaphoreType.DMA((n,)) for _ in range(3)]


def _exchange(arrs, gather, name):
    n = len(arrs)

    def body(*refs):
        ins, outs, sems = refs[:n], refs[n:2 * n], refs[2 * n:]
        _comm_start(ins, outs, sems, gather)
        _comm_wait(ins, outs, sems, gather)

    any_spec = pl.BlockSpec(memory_space=pl.ANY)
    outs = pl.pallas_call(
        body, name=name, out_shape=_comm_shapes(arrs, gather),
        in_specs=[any_spec] * n, out_specs=tuple([any_spec] * n), scratch_shapes=_comm_sems(n),
    )(*arrs)
    return list(outs)


def _gather_two_level(arrs, name):
    n = len(arrs)

    def body(*refs):
        ins, outs = refs[:n], refs[n:2 * n]
        send_sems, recv_sems, local_sems = refs[2 * n:]
        x, y, c = lax.axis_index("x"), lax.axis_index("y"), lax.axis_index("c")
        me, sibling = (x, y, c), (x, y, 1 - c)
        chips = [(1 - x, y), (x, 1 - y), (1 - x, 1 - y)]

        def copy(i, k, block, to, own=False):
            slab = outs[i].at[4 * block[0] + 2 * block[1] + block[2]]
            return pltpu.make_async_remote_copy(
                src_ref=ins[i] if own else slab, dst_ref=slab,
                send_sem=send_sems.at[i, k], recv_sem=recv_sems.at[i, k],
                device_id=to, device_id_type=MESH_ID)

        mine = [pltpu.make_async_copy(ins[i], outs[i].at[4 * x + 2 * y + c], local_sems.at[i]) for i in range(n)]
        first = [copy(i, 0, me, sibling, own=True) for i in range(n)]
        first += [copy(i, 1 + j, me, (*chip, c), own=True) for j, chip in enumerate(chips) for i in range(n)]
        for cp in mine + first:
            cp.start()
        passed = []
        for j, chip in enumerate(chips):
            for i in range(n):
                copy(i, 1 + j, (*chip, c), me).wait_recv()
                passed.append(copy(i, 4 + j, (*chip, c), sibling))
                passed[-1].start()
        for i in range(n):
            copy(i, 0, sibling, me).wait_recv()
            for j, chip in enumerate(chips):
                copy(i, 4 + j, (*chip, 1 - c), me).wait_recv()
        for cp in first + passed:
            cp.wait_send()
        for cp in mine:
            cp.wait()

    any_spec = pl.BlockSpec(memory_space=pl.ANY)
    outs = pl.pallas_call(
        body, name=name, out_shape=_comm_shapes(arrs, True),
        in_specs=[any_spec] * n, out_specs=tuple([any_spec] * n),
        scratch_shapes=[pltpu.SemaphoreType.DMA((n, N_DEV - 1)), pltpu.SemaphoreType.DMA((n, N_DEV - 1)),
                        pltpu.SemaphoreType.DMA((n,))],
    )(*arrs)
    return list(outs)


def _call(body, *, name, grid, in_specs, out_specs, out_shape, args, scratch=(), comm=None):
    sem = ("arbitrary",) * len(grid)
    if comm is None:
        res = pl.pallas_call(body, name=name, grid=grid, in_specs=list(in_specs), out_specs=tuple(out_specs),
                             out_shape=tuple(out_shape), scratch_shapes=list(scratch),
                             compiler_params=_params(*sem))(*args)
        return tuple(res), []
    arrs, gather = comm
    n, n_in, n_out, n_sc = len(arrs), len(args), len(out_shape), len(scratch)

    def wrapped(*refs):
        ins, cin = refs[:n_in], refs[n_in:n_in + n]
        outs, cout = refs[n_in + n:n_in + n + n_out], refs[n_in + n + n_out:n_in + 2 * n + n_out]
        sc, sems = refs[n_in + 2 * n + n_out:n_in + 2 * n + n_out + n_sc], refs[n_in + 2 * n + n_out + n_sc:]
        ids = [pl.program_id(a) for a in range(len(grid))]
        first = functools.reduce(jnp.logical_and, [i == 0 for i in ids])
        last = functools.reduce(jnp.logical_and, [i == g - 1 for i, g in zip(ids, grid)])

        @pl.when(first)
        def _():
            _comm_start(cin, cout, sems, gather)

        body(*ins, *outs, *sc)

        @pl.when(last)
        def _():
            _comm_wait(cin, cout, sems, gather)

    any_spec = pl.BlockSpec(memory_space=pl.ANY)
    res = pl.pallas_call(
        wrapped, name=name, grid=grid, in_specs=list(in_specs) + [any_spec] * n,
        out_specs=tuple(out_specs) + (any_spec,) * n, out_shape=tuple(out_shape) + _comm_shapes(arrs, gather),
        scratch_shapes=list(scratch) + _comm_sems(n), compiler_params=_params(*sem))(*args, *arrs)
    return tuple(res[:n_out]), list(res[n_out:])


def _load_resident(pairs, sem):
    copies = [pltpu.make_async_copy(src, dst, sem.at[k]) for k, (src, dst) in enumerate(pairs)]
    for cp in copies:
        cp.start()
    for cp in copies:
        cp.wait()


def _ffn_fwd(x, gain, wgt, wut, wd, name, comm=None, head=None):
    T, D = x.shape
    F = wd.shape[0]
    tm, tf = _tile(T, 512), _tile(F, 256)

    def body(*refs):
        if head:
            (x_ref, gain_ref, wg_hbm, wu_hbm, wd_hbm, fgain_ref, tgt_ref,
             out_ref, hn_ref, a_ref, s_ref, t_ref, dfgain_ref, loss_ref, wg_s, wu_s, wd_s, sem) = refs
        else:
            (x_ref, gain_ref, wg_hbm, wu_hbm, wd_hbm,
             out_ref, hn_ref, a_ref, s_ref, t_ref, wg_s, wu_s, wd_s, sem) = refs

        @pl.when(pl.program_id(0) == 0)
        def _():
            _load_resident([(wg_hbm, wg_s), (wu_hbm, wu_s), (wd_hbm, wd_s)], sem)
            if head:
                dfgain_ref[...] = jnp.zeros_like(dfgain_ref)
                loss_ref[...] = jnp.zeros_like(loss_ref)

        _, xhat = _rms(x_ref[...])
        hn = (xhat * gain_ref[...]).astype(BF16)
        hn_ref[...] = hn
        for f0 in range(0, F, tf):
            cols = slice(f0, f0 + tf)
            g = _mm_nt(hn, wg_s[cols, :])
            u = _mm_nt(hn, wu_s[cols, :])
            sig = _sigmoid(g)
            s = g * sig
            a_ref[:, cols] = (s * u).astype(BF16)
            s_ref[:, cols] = s.astype(BF16)
            t_ref[:, cols] = (u * (sig + s * (1.0 - sig))).astype(BF16)
        y = x_ref[...] + FFN_RESIDUAL_WEIGHT * _mm(a_ref[...], wd_s[...])
        if head:
            r, yhat = _rms(y)
            err = yhat * fgain_ref[...] - tgt_ref[...]
            loss_ref[...] += 0.5 * jnp.sum(jnp.mean(err * err, axis=-1, keepdims=True), axis=0, keepdims=True)
            dy = err * (1.0 / D)
            dfgain_ref[...] += jnp.sum(dy * yhat, axis=0, keepdims=True)
            out_ref[...] = _rms_bwd(dy, fgain_ref[...], r, yhat)
        else:
            out_ref[...] = y

    row = lambda i: (i, 0)
    one = pl.BlockSpec((1, D), lambda i: (0, 0))
    hbm = pl.BlockSpec(memory_space=pl.ANY)
    in_specs = [pl.BlockSpec((tm, D), row), one, hbm, hbm, hbm]
    out_specs = [pl.BlockSpec((tm, D), row), pl.BlockSpec((tm, D), row)] + [pl.BlockSpec((tm, F), row) for _ in range(3)]
    out_shape = [jax.ShapeDtypeStruct((T, D), F32), jax.ShapeDtypeStruct((T, D), BF16)] \
        + [jax.ShapeDtypeStruct((T, F), BF16) for _ in range(3)]
    args = (x, gain, wgt, wut, wd)
    if head:
        in_specs += [one, pl.BlockSpec((tm, D), row)]
        out_specs += [one, one]
        out_shape += [jax.ShapeDtypeStruct((1, D), F32), jax.ShapeDtypeStruct((1, D), F32)]
        args += tuple(head)
    return _call(
        body, name=name, grid=(T // tm,), comm=comm, in_specs=in_specs, out_specs=out_specs, out_shape=out_shape,
        scratch=[pltpu.VMEM((F, D), BF16) for _ in range(3)] + [pltpu.SemaphoreType.DMA((3,))], args=args)


def _ffn_bwd(dy, x, gain, s, t, wgt, wut, wd, name):
    T, D = x.shape
    F = wd.shape[0]
    tr, tf = _tile(T, 512), _tile(F, 256)
    rows = lambda i: (i, 0)
    one = pl.BlockSpec((1, D), lambda i: (0, 0))
    any_spec = pl.BlockSpec(memory_space=pl.ANY)

    def act_body(dy_ref, s_ref, t_ref, wd_hbm, dg_ref, du_ref, dyh_ref, wd_s, sem):
        @pl.when(pl.program_id(0) == 0)
        def _():
            _load_resident([(wd_hbm, wd_s)], sem)

        dyh = (FFN_RESIDUAL_WEIGHT * dy_ref[...]).astype(BF16)
        dyh_ref[...] = dyh
        for f0 in range(0, F, tf):
            cols = slice(f0, f0 + tf)
            da = _mm_nt(dyh, wd_s[cols, :])
            dg_ref[:, cols] = (da * t_ref[:, cols].astype(F32)).astype(BF16)
            du_ref[:, cols] = (da * s_ref[:, cols].astype(F32)).astype(BF16)

    wide = jax.ShapeDtypeStruct((T, F), BF16)
    dg, du, dyh = pl.pallas_call(
        act_body, name=name + "_act", grid=(T // tr,),
        in_specs=[pl.BlockSpec((tr, D), rows), pl.BlockSpec((tr, F), rows), pl.BlockSpec((tr, F), rows), any_spec],
        out_specs=(pl.BlockSpec((tr, F), rows), pl.BlockSpec((tr, F), rows), pl.BlockSpec((tr, D), rows)),
        out_shape=(wide, wide, jax.ShapeDtypeStruct((T, D), BF16)),
        scratch_shapes=[pltpu.VMEM((F, D), BF16), pltpu.SemaphoreType.DMA((1,))],
        compiler_params=_params("arbitrary"),
    )(dy, s, t, wd)

    def in_body(dg_ref, du_ref, dy_ref, x_ref, gain_ref, wg_hbm, wu_hbm, dx_ref, dgain_ref, wg_s, wu_s, sem):
        @pl.when(pl.program_id(0) == 0)
        def _():
            _load_resident([(wg_hbm, wg_s), (wu_hbm, wu_s)], sem)
            dgain_ref[...] = jnp.zeros_like(dgain_ref)

        dhn = _mm(dg_ref[...], wg_s[...]) + _mm(du_ref[...], wu_s[...])
        r, xhat = _rms(x_ref[...])
        dgain_ref[...] += jnp.sum(dhn * xhat, axis=0, keepdims=True)
        dx_ref[...] = dy_ref[...] + _rms_bwd(dhn, gain_ref[...], r, xhat)

    dx, dgain = pl.pallas_call(
        in_body, name=name + "_in", grid=(T // tr,),
        in_specs=[pl.BlockSpec((tr, F), rows), pl.BlockSpec((tr, F), rows), pl.BlockSpec((tr, D), rows),
                  pl.BlockSpec((tr, D), rows), one, any_spec, any_spec],
        out_specs=(pl.BlockSpec((tr, D), rows), one),
        out_shape=(jax.ShapeDtypeStruct((T, D), F32), jax.ShapeDtypeStruct((1, D), F32)),
        scratch_shapes=[pltpu.VMEM((F, D), BF16), pltpu.VMEM((F, D), BF16), pltpu.SemaphoreType.DMA((2,))],
        compiler_params=_params("arbitrary"),
    )(dg, du, dy, x, gain, wgt, wut)
    return dx, dg, du, dyh, dgain


def _wgrad(a, b, name, col_slab=None, comm=None):
    T, M = a.shape
    N = b.shape[1]
    tmm = M if M <= 1024 else _tile(M, 1408)
    tn = _tile(N, 1024)
    if col_slab and col_slab % LANES:
        tn = col_slab * LANES // math.gcd(col_slab, LANES)
    tk = _tile(T, 1024)
    nk = T // tk
    per = tn // col_slab if col_slab else 0

    def body(a_ref, b_ref, out_ref, acc):
        k = pl.program_id(2)

        @pl.when(k == 0)
        def _():
            acc[...] = jnp.zeros_like(acc)

        acc[...] += _mm_tn(a_ref[...], b_ref[...])

        @pl.when(k == nk - 1)
        def _():
            if col_slab:
                for s in range(per):
                    out_ref[s] = acc[:, s * col_slab:(s + 1) * col_slab].astype(BF16)
            else:
                out_ref[...] = acc[...].astype(BF16)

    if col_slab:
        out_spec = pl.BlockSpec((per, tmm, col_slab), lambda m, n, k: (n, m, 0))
        out_shape = jax.ShapeDtypeStruct((N // col_slab, M, col_slab), BF16)
    else:
        out_spec = pl.BlockSpec((tmm, tn), lambda m, n, k: (m, n))
        out_shape = jax.ShapeDtypeStruct((M, N), BF16)
    (out,), got = _call(
        body, name=name, grid=(M // tmm, N // tn, nk), comm=comm,
        in_specs=[pl.BlockSpec((tk, tmm), lambda m, n, k: (k, m)), pl.BlockSpec((tk, tn), lambda m, n, k: (k, n))],
        out_specs=(out_spec,), out_shape=(out_shape,),
        scratch=[pltpu.VMEM((tmm, tn), F32)], args=(a, b))
    return (out, got) if comm else out


def _mix_in_fwd(x, gain, w_in):
    T, D = x.shape
    cs = w_in.shape[2]
    n_qkv = 3 * (N_DEV // 4)
    tm = _tile(T, 1024)

    def body(x_ref, gain_ref, w_ref, hn_ref, qkv_ref, u_ref):
        _, xhat = _rms(x_ref[...])
        hn = (xhat * gain_ref[...]).astype(BF16)
        hn_ref[...] = hn
        for b in range(N_DEV):
            p = _mm(hn, w_ref[b])
            if b < n_qkv:
                qkv_ref[:, b * cs:(b + 1) * cs] = p.astype(BF16)
            else:
                u_ref[:, (b - n_qkv) * cs:(b - n_qkv + 1) * cs] = p

    row = lambda i: (i, 0)
    return pl.pallas_call(
        body, name="mix_in_fwd", grid=(T // tm,),
        in_specs=[pl.BlockSpec((tm, D), row), pl.BlockSpec((1, D), lambda i: (0, 0)),
                  pl.BlockSpec((N_DEV, D, cs), lambda i: (0, 0, 0))],
        out_specs=(pl.BlockSpec((tm, D), row), pl.BlockSpec((tm, n_qkv * cs), row),
                   pl.BlockSpec((tm, (N_DEV - n_qkv) * cs), row)),
        out_shape=(jax.ShapeDtypeStruct((T, D), BF16), jax.ShapeDtypeStruct((T, n_qkv * cs), BF16),
                   jax.ShapeDtypeStruct((T, (N_DEV - n_qkv) * cs), F32)),
        compiler_params=_params("arbitrary"),
    )(x, gain, w_in)


def _mix_in_bwd(dres, dq, dk, dv, du, x, gain, w_in):
    T, D = x.shape
    cs = w_in.shape[2]
    W = dq.shape[1]
    per = W // cs
    tm = _tile(T, 512)

    def body(dres_ref, dq_ref, dk_ref, dv_ref, du_ref, x_ref, gain_ref, w_ref, dx_ref, dproj_ref, dgain_ref):
        i = pl.program_id(0)

        @pl.when(i == 0)
        def _():
            dgain_ref[...] = jnp.zeros_like(dgain_ref)

        dhn = jnp.zeros((tm, D), F32)
        for part, ref in enumerate((dq_ref, dk_ref, dv_ref, du_ref)):
            for h in range(per):
                b = part * per + h
                d = ref[:, h * cs:(h + 1) * cs]
                dproj_ref[:, b * cs:(b + 1) * cs] = d
                dhn = dhn + _mm_nt(d, w_ref[b])
        r, xhat = _rms(x_ref[...])
        dgain_ref[...] += jnp.sum(dhn * xhat, axis=0, keepdims=True)
        dx_ref[...] = dres_ref[...] + _rms_bwd(dhn, gain_ref[...], r, xhat)

    row = lambda i: (i, 0)
    one = pl.BlockSpec((1, D), lambda i: (0, 0))
    part = pl.BlockSpec((tm, W), row)
    return pl.pallas_call(
        body, name="mix_in_bwd", grid=(T // tm,),
        in_specs=[pl.BlockSpec((tm, D), row), part, part, part, part, pl.BlockSpec((tm, D), row), one,
                  pl.BlockSpec((N_DEV, D, cs), lambda i: (0, 0, 0))],
        out_specs=(pl.BlockSpec((tm, D), row), pl.BlockSpec((tm, 4 * W), row), one),
        out_shape=(jax.ShapeDtypeStruct((T, D), F32), jax.ShapeDtypeStruct((T, 4 * W), BF16),
                   jax.ShapeDtypeStruct((1, D), F32)),
        compiler_params=_params("arbitrary"),
    )(dres, dq, dk, dv, du, x, gain, w_in)


SB_PAIRS_PER_PROGRAM = 2
LOG2_E = 1.4426950408889634
EXP2_CLAMP = 126.0


def _neg_log2_sigmoid(nz2):
    w = jnp.minimum(nz2, EXP2_CLAMP)
    return w, jnp.log2(1.0 + jnp.exp2(w))


SB_DEAD_LOG2 = -160.0


def _sb_live(rests):
    worst = functools.reduce(jnp.maximum, rests)
    return (jnp.max(worst) > SB_DEAD_LOG2).astype(jnp.int32)


def _split(v):
    hi = v.astype(BF16)
    return hi, (v - hi.astype(F32)).astype(BF16)


def _tri_sum(v, tri):
    hi, lo = _split(v)
    return _mm(hi, tri) + _mm(lo, tri)


def _sb_fwd(qkv, B, S, comm=None):
    W = qkv.shape[2] // 3
    n_pair = W // LANES
    bq = _tile(S, 256)
    nq = S // bq
    hp = SB_PAIRS_PER_PROGRAM
    nscale2 = -(SB_HEAD_DIM ** -0.5) * LOG2_E

    def body(q_ref, k_ref, v_ref, o_ref):
        lane = lax.broadcasted_iota(jnp.int32, (1, LANES), 1)
        head0 = lane < SB_HEAD_DIM
        rr = lax.broadcasted_iota(jnp.int32, (bq, bq), 0)
        cc = lax.broadcasted_iota(jnp.int32, (bq, bq), 1)
        strict = cc < rr
        after = jnp.where(rr > cc, 1.0, 0.0).astype(BF16)

        def blocks(heads, ks, carries, diag):
            n = range(len(heads))
            keep = (lambda t: jnp.where(strict, t, 0.0)) if diag else (lambda t: t)
            z = [_mm_nt(qh, k_ref[ks, cols]) for qh, cols in heads]
            wl = [_neg_log2_sigmoid(z[h] * nscale2) for h in n]
            lr = [keep(wl[h][0] - wl[h][1]) for h in n]
            parts = [_split(lr[h]) for h in n]
            suf = [_mm(parts[h][0], after) + _mm(parts[h][1], after) for h in n]
            a = [keep(jnp.exp2(suf[h] + carries[h][1] - wl[h][1])).astype(BF16) for h in n]
            o = [carries[h][0] + _mm(a[h], v_ref[ks, heads[h][1]]) for h in n]
            return tuple((o[h], carries[h][1] + (suf[h][:, :1] + lr[h][:, :1])) for h in n)

        def q_tile(i, _):
            qs = pl.ds(pl.multiple_of(i * bq, bq), bq)
            heads = []
            for pr in range(hp):
                cols = slice(pr * LANES, (pr + 1) * LANES)
                qv = q_ref[qs, cols]
                heads += [(jnp.where(head0, qv, jnp.zeros_like(qv)), cols),
                          (jnp.where(head0, jnp.zeros_like(qv), qv), cols)]
            zero = (jnp.zeros((bq, LANES), F32), jnp.zeros((bq, 1), F32))
            init = blocks(heads, qs, (zero,) * len(heads), True)

            def left(st):
                t, _, cr = st
                ks = pl.ds(pl.multiple_of((i - 1 - t) * bq, bq), bq)
                cr = blocks(heads, ks, cr, False)
                return t + 1, _sb_live([c for _, c in cr]), cr

            _, _, res = lax.while_loop(lambda st: jnp.logical_and(st[0] < i, st[1] > 0), left,
                                       (jnp.int32(0), _sb_live([c for _, c in init]), init))
            for pr in range(hp):
                o_ref[qs, heads[2 * pr][1]] = jnp.where(head0, res[2 * pr][0], res[2 * pr + 1][0]).astype(BF16)
            return 0

        lax.fori_loop(0, nq, q_tile, 0)

    def col(off):
        return pl.BlockSpec((None, S, hp * LANES), lambda b, p: (b, 0, off + p))

    n_pair //= hp
    return _call(
        body, name="sb_fwd", grid=(B, n_pair), comm=comm,
        in_specs=[col(0), col(n_pair), col(2 * n_pair)],
        out_specs=(col(0),),
        out_shape=(jax.ShapeDtypeStruct((B, S, W), BF16),),
        args=(qkv, qkv, qkv))


def _sb_bwd(qkv, do, B, S, comm=None):
    W = qkv.shape[2] // 3
    n_pair = W // LANES
    bq = _tile(S, 256)
    nq = S // bq
    hp = SB_PAIRS_PER_PROGRAM
    scale = SB_HEAD_DIM ** -0.5
    nscale2 = -scale * LOG2_E

    def body(q_ref, k_ref, v_ref, do_ref, dq_ref, dk_ref, dv_ref, dk_s, dv_s, e_s, sg_s, a_s):
        lane = lax.broadcasted_iota(jnp.int32, (1, LANES), 1)
        head0 = lane < SB_HEAD_DIM
        rr = lax.broadcasted_iota(jnp.int32, (bq, bq), 0)
        cc = lax.broadcasted_iota(jnp.int32, (bq, bq), 1)
        strict = cc < rr
        after = jnp.where(rr > cc, 1.0, 0.0).astype(BF16)
        before = jnp.where(rr < cc, 1.0, 0.0).astype(BF16)
        dk_s[...] = jnp.zeros_like(dk_s)
        dv_s[...] = jnp.zeros_like(dv_s)

        def weights(heads, ks, kb, rests, diag):
            n = range(len(heads))
            keep = (lambda t: jnp.where(strict, t, 0.0)) if diag else (lambda t: t)
            z = [_mm_nt(heads[h][0], k_ref[ks, heads[h][2]]) for h in n]
            da = [_mm_nt(heads[h][1], v_ref[ks, heads[h][2]]) for h in n]
            wl = [_neg_log2_sigmoid(z[h] * nscale2) for h in n]
            lr = [keep(wl[h][0] - wl[h][1]) for h in n]
            parts = [_split(lr[h]) for h in n]
            suf = [_mm(parts[h][0], after) + _mm(parts[h][1], after) for h in n]
            a = [keep(jnp.exp2(suf[h] + rests[h] - wl[h][1])) for h in n]
            for h in n:
                a_s[h * nq + kb] = a[h].astype(BF16)
                e_s[h * nq + kb] = a[h] * da[h]
                sg_s[h * nq + kb] = jnp.exp2(-wl[h][1])
            return tuple(rests[h] + (suf[h][:, :1] + lr[h][:, :1]) for h in n)

        def grads(heads, ks, kb, carries, diag):
            n = range(len(heads))
            keep = (lambda t: jnp.where(strict, t, 0.0)) if diag else (lambda t: t)
            e = [e_s[h * nq + kb] for h in n]
            parts = [_split(e[h]) for h in n]
            pex = [_mm(parts[h][0], before) + _mm(parts[h][1], before) for h in n]
            dz = [keep(e[h] - sg_s[h * nq + kb] * (e[h] + pex[h] + carries[h][1])).astype(BF16) for h in n]
            dq = [carries[h][0] + _mm(dz[h], k_ref[ks, heads[h][2]]) for h in n]
            for h in n:
                dk_s[ks, heads[h][2]] += _mm_tn(dz[h], heads[h][0])
                dv_s[ks, heads[h][2]] += _mm_tn(a_s[h * nq + kb], heads[h][1])
            return tuple((dq[h], carries[h][1] + (pex[h][:, bq - 1:] + e[h][:, bq - 1:])) for h in n)

        def q_tile(i, _):
            qs = pl.ds(pl.multiple_of(i * bq, bq), bq)
            heads = []
            for pr in range(hp):
                cols = slice(pr * LANES, (pr + 1) * LANES)
                qv, dov = q_ref[qs, cols], do_ref[qs, cols]
                zq, zd = jnp.zeros_like(qv), jnp.zeros_like(dov)
                heads += [(jnp.where(head0, qv, zq), jnp.where(head0, dov, zd), cols),
                          (jnp.where(head0, zq, qv), jnp.where(head0, zd, dov), cols)]
            key_block = lambda kb: pl.ds(pl.multiple_of(kb * bq, bq), bq)
            rests = weights(heads, qs, i, (jnp.zeros((bq, 1), F32),) * len(heads), True)

            def left(st):
                t, _, rs = st
                rs = weights(heads, key_block(i - 1 - t), i - 1 - t, rs, False)
                return t + 1, _sb_live(rs), rs

            n_left, _, _ = lax.while_loop(lambda st: jnp.logical_and(st[0] < i, st[1] > 0), left,
                                          (jnp.int32(0), _sb_live(rests), rests))
            zero = (jnp.zeros((bq, LANES), F32), jnp.zeros((bq, 1), F32))
            res = lax.fori_loop(0, n_left, lambda t, cr: grads(heads, key_block(i - n_left + t), i - n_left + t, cr, False),
                                (zero,) * len(heads))
            res = grads(heads, qs, i, res, True)
            for pr in range(hp):
                dq = jnp.where(head0, res[2 * pr][0], res[2 * pr + 1][0])
                dq_ref[qs, heads[2 * pr][2]] = (dq * scale).astype(BF16)
            return 0

        lax.fori_loop(0, nq, q_tile, 0)
        dk_ref[...] = (dk_s[...] * scale).astype(BF16)
        dv_ref[...] = dv_s[...].astype(BF16)

    def col(off):
        return pl.BlockSpec((None, S, hp * LANES), lambda b, p: (b, 0, off + p))

    n_pair //= hp
    shp = jax.ShapeDtypeStruct((B, S, W), BF16)
    slots = 2 * hp * nq
    return _call(
        body, name="sb_bwd", grid=(B, n_pair), comm=comm,
        in_specs=[col(0), col(n_pair), col(2 * n_pair), col(0)],
        out_specs=(col(0), col(0), col(0)),
        out_shape=(shp, shp, shp),
        scratch=[pltpu.VMEM((S, hp * LANES), F32), pltpu.VMEM((S, hp * LANES), F32),
                 pltpu.VMEM((slots, bq, bq), F32), pltpu.VMEM((slots, bq, bq), F32),
                 pltpu.VMEM((slots, bq, bq), BF16)],
        args=(qkv, qkv, qkv, do))


def _pool_counts(S):
    t = lax.broadcasted_iota(jnp.int32, (S, 1), 0)
    return t, [jnp.minimum(t + 1, w).astype(F32) for w in POOL_WINDOWS]


def _pool_fwd(u, B, S):
    W = u.shape[2]

    def body(u_ref, out_ref):
        t, counts = _pool_counts(S)
        for gi, w in enumerate(POOL_WINDOWS):
            cols = slice(gi * POOL_GROUP_DIM, (gi + 1) * POOL_GROUP_DIM)
            ug = u_ref[:, cols]
            s, k = ug, 1
            while k < w:
                s = s + jnp.where(t >= k, pltpu.roll(s, k, axis=0), 0.0)
                k *= 2
            out_ref[:, cols] = (s / counts[gi] - ug).astype(BF16)

    spec = pl.BlockSpec((None, S, W), lambda b: (b, 0, 0))
    return pl.pallas_call(
        body, name="pool_fwd", grid=(B,), in_specs=[spec], out_specs=spec,
        out_shape=jax.ShapeDtypeStruct((B, S, W), BF16), compiler_params=_params("arbitrary"),
    )(u)


def _pool_bwd(dpooled, B, S):
    W = dpooled.shape[2]

    def body(d_ref, out_ref):
        t, counts = _pool_counts(S)
        for gi, w in enumerate(POOL_WINDOWS):
            cols = slice(gi * POOL_GROUP_DIM, (gi + 1) * POOL_GROUP_DIM)
            d = d_ref[:, cols]
            s, k = d / counts[gi], 1
            while k < w:
                s = s + jnp.where(t < S - k, pltpu.roll(s, S - k, axis=0), 0.0)
                k *= 2
            out_ref[:, cols] = (s - d).astype(BF16)

    spec = pl.BlockSpec((None, S, W), lambda b: (b, 0, 0))
    return pl.pallas_call(
        body, name="pool_bwd", grid=(B,), in_specs=[spec], out_specs=spec,
        out_shape=jax.ShapeDtypeStruct((B, S, W), BF16), compiler_params=_params("arbitrary"),
    )(dpooled)


def _mix_out_fwd(x, o_sb, pooled, w_pool, pool_scale, w_out):
    T, D = x.shape
    W = o_sb.shape[1]
    G = w_pool.shape[0]
    gd = POOL_GROUP_DIM
    tm = _tile(T, 1024)

    def body(x_ref, osb_ref, pooled_ref, wp_ref, ps_ref, wo_ref, out_ref, mixed_ref):
        mixed_ref[:, :W] = osb_ref[...]
        for gi in range(G):
            cols = slice(gi * gd, (gi + 1) * gd)
            pw = _mm(pooled_ref[:, cols], wp_ref[gi])
            mixed_ref[:, W + gi * gd:W + (gi + 1) * gd] = (pw * ps_ref[:, cols]).astype(BF16)
        out_ref[...] = x_ref[...] + _mm(mixed_ref[...], wo_ref[...].reshape(D, D))

    row = lambda i: (i, 0)
    return pl.pallas_call(
        body, name="mix_out_fwd", grid=(T // tm,),
        in_specs=[pl.BlockSpec((tm, D), row), pl.BlockSpec((tm, W), row), pl.BlockSpec((tm, W), row),
                  pl.BlockSpec((G, gd, gd), lambda i: (0, 0, 0)), pl.BlockSpec((1, W), lambda i: (0, 0)),
                  pl.BlockSpec(w_out.shape, lambda i: (0, 0, 0))],
        out_specs=(pl.BlockSpec((tm, D), row), pl.BlockSpec((tm, D), row)),
        out_shape=(jax.ShapeDtypeStruct((T, D), F32), jax.ShapeDtypeStruct((T, D), BF16)),
        compiler_params=_params("arbitrary"),
    )(x, o_sb, pooled, w_pool, pool_scale, w_out)


def _mix_out_bwd(dx, pooled, w_pool, pool_scale, w_out):
    T, D = dx.shape
    W = pooled.shape[1]
    G = w_pool.shape[0]
    gd = POOL_GROUP_DIM
    tm = _tile(T, 1024)

    def body(dx_ref, pooled_ref, wp_ref, ps_ref, wo_ref, dxb_ref, dosb_ref, dpooled_ref, dwp_ref, dps_ref):
        i = pl.program_id(0)

        @pl.when(i == 0)
        def _():
            dwp_ref[...] = jnp.zeros_like(dwp_ref)
            dps_ref[...] = jnp.zeros_like(dps_ref)

        dxb = dx_ref[...].astype(BF16)
        dxb_ref[...] = dxb
        dmixed = _mm_nt(dxb, wo_ref[...].reshape(D, D))
        dosb_ref[...] = dmixed[:, :W].astype(BF16)
        for gi in range(G):
            cols = slice(gi * gd, (gi + 1) * gd)
            pg = pooled_ref[:, cols]
            dop = dmixed[:, W + gi * gd:W + (gi + 1) * gd]
            pw = _mm(pg, wp_ref[gi])
            dps_ref[:, cols] += jnp.sum(dop * pw, axis=0, keepdims=True)
            dpw = (dop * ps_ref[:, cols]).astype(BF16)
            dwp_ref[gi] += _mm_tn(pg, dpw)
            dpooled_ref[:, cols] = _mm_nt(dpw, wp_ref[gi])

    row = lambda i: (i, 0)
    return pl.pallas_call(
        body, name="mix_out_bwd", grid=(T // tm,),
        in_specs=[pl.BlockSpec((tm, D), row), pl.BlockSpec((tm, W), row),
                  pl.BlockSpec((G, gd, gd), lambda i: (0, 0, 0)), pl.BlockSpec((1, W), lambda i: (0, 0)),
                  pl.BlockSpec(w_out.shape, lambda i: (0, 0, 0))],
        out_specs=(pl.BlockSpec((tm, D), row), pl.BlockSpec((tm, W), row), pl.BlockSpec((tm, W), row),
                   pl.BlockSpec((G, gd, gd), lambda i: (0, 0, 0)), pl.BlockSpec((1, W), lambda i: (0, 0))),
        out_shape=(jax.ShapeDtypeStruct((T, D), BF16), jax.ShapeDtypeStruct((T, W), BF16),
                   jax.ShapeDtypeStruct((T, W), F32), jax.ShapeDtypeStruct((G, gd, gd), F32),
                   jax.ShapeDtypeStruct((1, W), F32)),
        compiler_params=_params("arbitrary"),
    )(dx, pooled, w_pool, pool_scale, w_out)


def _mem_kv_fwd(mem, gain, w_kv):
    B, M, D = mem.shape
    cs = w_kv.shape[2]

    def body(mem_ref, gain_ref, w_ref, memn_ref, kv_ref):
        _, xhat = _rms(mem_ref[...])
        mn = (xhat * gain_ref[...]).astype(BF16)
        memn_ref[...] = mn
        for b in range(N_DEV):
            kv_ref[:, b * cs:(b + 1) * cs] = _mm(mn, w_ref[b]).astype(BF16)

    return pl.pallas_call(
        body, name="mem_kv_fwd", grid=(B,),
        in_specs=[pl.BlockSpec((None, M, D), lambda b: (b, 0, 0)), pl.BlockSpec((1, D), lambda b: (0, 0)),
                  pl.BlockSpec((N_DEV, D, cs), lambda b: (0, 0, 0))],
        out_specs=(pl.BlockSpec((M, D), lambda b: (b, 0)), pl.BlockSpec((None, M, N_DEV * cs), lambda b: (b, 0, 0))),
        out_shape=(jax.ShapeDtypeStruct((B * M, D), BF16), jax.ShapeDtypeStruct((B, M, N_DEV * cs), BF16)),
        compiler_params=_params("arbitrary"),
    )(mem, gain, w_kv)


def _mem_kv_bwd(dkv, mem, w_kv):
    B, M, D = mem.shape
    cs = w_kv.shape[2]

    def body(dkv_ref, mem_ref, w_ref, dkvb_ref, dgain_ref):
        b_id = pl.program_id(0)

        @pl.when(b_id == 0)
        def _():
            dgain_ref[...] = jnp.zeros_like(dgain_ref)

        dkvb = dkv_ref[...].astype(BF16)
        dkvb_ref[...] = dkvb
        dmn = jnp.zeros((M, D), F32)
        for b in range(N_DEV):
            dmn = dmn + _mm_nt(dkvb[:, b * cs:(b + 1) * cs], w_ref[b])
        _, xhat = _rms(mem_ref[...])
        dgain_ref[...] += jnp.sum(dmn * xhat, axis=0, keepdims=True)

    return pl.pallas_call(
        body, name="mem_kv_bwd", grid=(B,),
        in_specs=[pl.BlockSpec((None, M, N_DEV * cs), lambda b: (b, 0, 0)),
                  pl.BlockSpec((None, M, D), lambda b: (b, 0, 0)),
                  pl.BlockSpec((N_DEV, D, cs), lambda b: (0, 0, 0))],
        out_specs=(pl.BlockSpec((M, N_DEV * cs), lambda b: (b, 0)), pl.BlockSpec((1, D), lambda b: (0, 0))),
        out_shape=(jax.ShapeDtypeStruct((B * M, N_DEV * cs), BF16), jax.ShapeDtypeStruct((1, D), F32)),
        compiler_params=_params("arbitrary"),
    )(dkv, mem, w_kv)


def _softmax_rows(s):
    p = jnp.exp(s - jnp.max(s, axis=1, keepdims=True))
    return p / jnp.sum(p, axis=1, keepdims=True)


def _cross_fwd(x, gain, kv, w_q, w_o, B, S, comm=None):
    T, D = x.shape
    M = kv.shape[1]
    hd = D // MEM_HEADS
    tm = _tile(S, 1024)
    per = S // tm
    scale = hd ** -0.5

    def body(x_ref, gain_ref, kv_ref, wq_ref, wo_ref, out_ref, hq_ref, q_ref, ocat_ref):
        _, xhat = _rms(x_ref[...])
        hq = (xhat * gain_ref[...]).astype(BF16)
        hq_ref[...] = hq
        q = _mm(hq, wq_ref[...].reshape(D, D)).astype(BF16)
        q_ref[...] = q
        for h in range(MEM_HEADS):
            cols = slice(h * hd, (h + 1) * hd)
            s = _mm_nt(q[:, cols], kv_ref[:, cols]) * scale
            p = _softmax_rows(s).astype(BF16)
            ocat_ref[:, cols] = _mm(p, kv_ref[:, D + h * hd:D + (h + 1) * hd]).astype(BF16)
        out_ref[...] = x_ref[...] + _mm(ocat_ref[...], wo_ref[...].reshape(D, D))

    row = lambda b, t: (b * per + t, 0)
    wspec = pl.BlockSpec(w_q.shape, lambda b, t: (0, 0, 0))
    return _call(
        body, name="cross_fwd", grid=(B, per), comm=comm,
        in_specs=[pl.BlockSpec((tm, D), row), pl.BlockSpec((1, D), lambda b, t: (0, 0)),
                  pl.BlockSpec((None, M, 2 * D), lambda b, t: (b, 0, 0)), wspec, wspec],
        out_specs=tuple(pl.BlockSpec((tm, D), row) for _ in range(4)),
        out_shape=(jax.ShapeDtypeStruct((T, D), F32),) + tuple(jax.ShapeDtypeStruct((T, D), BF16) for _ in range(3)),
        args=(x, gain, kv, w_q, w_o))


def _cross_bwd(dy, x, gain, q, kv, w_q, w_o, B, S, comm=None):
    T, D = x.shape
    M = kv.shape[1]
    hd = D // MEM_HEADS
    tm = _tile(S, 512)
    per = S // tm
    scale = hd ** -0.5

    def body(dy_ref, x_ref, gain_ref, q_ref, kv_ref, wq_ref, wo_ref,
             dx_ref, dyb_ref, dqb_ref, dkv_ref, dgain_ref):
        b_id, t_id = pl.program_id(0), pl.program_id(1)

        @pl.when((b_id == 0) & (t_id == 0))
        def _():
            dgain_ref[...] = jnp.zeros_like(dgain_ref)

        @pl.when(t_id == 0)
        def _():
            dkv_ref[...] = jnp.zeros_like(dkv_ref)

        dyb = dy_ref[...].astype(BF16)
        dyb_ref[...] = dyb
        docat = _mm_nt(dyb, wo_ref[...].reshape(D, D)).astype(BF16)
        for h in range(MEM_HEADS):
            cols = slice(h * hd, (h + 1) * hd)
            vcols = slice(D + h * hd, D + (h + 1) * hd)
            qh, kh, vh, doh = q_ref[:, cols], kv_ref[:, cols], kv_ref[:, vcols], docat[:, cols]
            p = _softmax_rows(_mm_nt(qh, kh) * scale)
            dp = _mm_nt(doh, vh)
            ds = (p * (dp - jnp.sum(dp * p, axis=1, keepdims=True)) * scale).astype(BF16)
            dqb_ref[:, cols] = _mm(ds, kh).astype(BF16)
            dkv_ref[:, cols] += _mm_tn(ds, qh)
            dkv_ref[:, vcols] += _mm_tn(p.astype(BF16), doh)
        dhq = _mm_nt(dqb_ref[...], wq_ref[...].reshape(D, D))
        r, xhat = _rms(x_ref[...])
        dgain_ref[...] += jnp.sum(dhq * xhat, axis=0, keepdims=True)
        dx_ref[...] = dy_ref[...] + _rms_bwd(dhq, gain_ref[...], r, xhat)

    row = lambda b, t: (b * per + t, 0)
    wspec = pl.BlockSpec(w_q.shape, lambda b, t: (0, 0, 0))
    one = pl.BlockSpec((1, D), lambda b, t: (0, 0))
    kvspec = pl.BlockSpec((None, M, 2 * D), lambda b, t: (b, 0, 0))
    return _call(
        body, name="cross_bwd", grid=(B, per), comm=comm,
        in_specs=[pl.BlockSpec((tm, D), row), pl.BlockSpec((tm, D), row), one, pl.BlockSpec((tm, D), row),
                  kvspec, wspec, wspec],
        out_specs=(pl.BlockSpec((tm, D), row), pl.BlockSpec((tm, D), row), pl.BlockSpec((tm, D), row), kvspec, one),
        out_shape=(jax.ShapeDtypeStruct((T, D), F32), jax.ShapeDtypeStruct((T, D), BF16),
                   jax.ShapeDtypeStruct((T, D), BF16), jax.ShapeDtypeStruct((B, M, 2 * D), F32),
                   jax.ShapeDtypeStruct((1, D), F32)),
        args=(dy, x, gain, q, kv, w_q, w_o))


def _ordered_sum(gp_ref):
    g = gp_ref[0].astype(F32)
    for s in range(1, N_DEV):
        g = g + gp_ref[s].astype(F32)
    return g


def _adam_write(g, w_ref, m_ref, v_ref, g_ref, d_ref, nm_ref, nv_ref):
    nm = ADAM_B1 * m_ref[...] + (1.0 - ADAM_B1) * g
    nv = ADAM_B2 * v_ref[...] + (1.0 - ADAM_B2) * (g * g)
    m_hat = nm / (1.0 - ADAM_B1 ** ADAM_STEP)
    v_hat = nv / (1.0 - ADAM_B2 ** ADAM_STEP)
    g_ref[...] = g
    nm_ref[...] = nm
    nv_ref[...] = nv
    d_ref[...] = -ADAM_LR * (m_hat / (jnp.sqrt(v_hat) + ADAM_EPS) + ADAM_WD * w_ref[...])


def _adamw(gparts, w, m, v, name):
    R, C = w.shape
    tr = _tile(R, 256)

    def body(gp_ref, w_ref, m_ref, v_ref, g_ref, d_ref, nm_ref, nv_ref):
        _adam_write(_ordered_sum(gp_ref), w_ref, m_ref, v_ref, g_ref, d_ref, nm_ref, nv_ref)

    spec = pl.BlockSpec((tr, C), lambda i: (i, 0))
    shp = jax.ShapeDtypeStruct((R, C), F32)
    return pl.pallas_call(
        body, name=name, grid=(R // tr,),
        in_specs=[pl.BlockSpec((N_DEV, tr, C), lambda i: (0, i, 0)), spec, spec, spec],
        out_specs=(spec, spec, spec, spec), out_shape=(shp, shp, shp, shp),
        compiler_params=_params("arbitrary"),
    )(gparts, w, m, v)


def _pack_rows(vectors, D):
    def body(*refs):
        out_ref = refs[-1]
        out_ref[...] = jnp.zeros_like(out_ref)
        for i, r in enumerate(refs[:-1]):
            out_ref[i:i + 1, :r.shape[1]] = r[...]

    vmem = pl.BlockSpec(memory_space=pltpu.VMEM)
    return pl.pallas_call(body, name="pack_small", in_specs=[vmem] * len(vectors), out_specs=vmem,
                          out_shape=jax.ShapeDtypeStruct((8, D), F32))(*vectors)


def _adamw_small(row_parts, mat_parts, vectors, matrix):
    n = len(vectors)

    def body(*refs):
        rp_ref, mp_ref = refs[0], refs[1]
        ins, outs = refs[2:2 + 3 * (n + 1)], refs[2 + 3 * (n + 1):]
        rows = _ordered_sum(rp_ref)
        for i in range(n):
            c = ins[3 * i].shape[1]
            _adam_write(rows[i:i + 1, :c], *ins[3 * i:3 * i + 3], *outs[4 * i:4 * i + 4])
        _adam_write(_ordered_sum(mp_ref), *ins[3 * n:3 * n + 3], *outs[4 * n:4 * n + 4])
        outs[-1][...] = rows[n:n + 1, :]

    flat = [a for wmv in vectors for a in wmv] + list(matrix)
    out_shape = [jax.ShapeDtypeStruct(wmv[0].shape, F32) for wmv in list(vectors) + [matrix] for _ in range(4)]
    out_shape.append(jax.ShapeDtypeStruct((1, row_parts.shape[2]), F32))
    vmem = pl.BlockSpec(memory_space=pltpu.VMEM)
    res = pl.pallas_call(body, name="adamw_small", in_specs=[vmem] * (2 + len(flat)),
                         out_specs=tuple([vmem] * len(out_shape)), out_shape=tuple(out_shape))(row_parts, mat_parts, *flat)
    return [res[4 * i:4 * i + 4] for i in range(n + 1)], res[-1]


def kernel(x, mem, ffn1_norm, ffn1_w_gate, ffn1_w_up, ffn1_w_down, mix_norm, w_in, w_pool, pool_scale, w_out, mem_q_norm, mem_kv_norm, mem_w_q, mem_w_kv, mem_w_o, ffn2_norm, ffn2_w_gate, ffn2_w_up, ffn2_w_down, final_norm, loss_target, m_ffn1_norm, m_ffn1_w_gate, m_ffn1_w_up, m_ffn1_w_down, m_mix_norm, m_w_in, m_w_pool, m_pool_scale, m_w_out, m_mem_q_norm, m_mem_kv_norm, m_mem_w_q, m_mem_w_kv, m_mem_w_o, m_ffn2_norm, m_ffn2_w_gate, m_ffn2_w_up, m_ffn2_w_down, m_final_norm, v_ffn1_norm, v_ffn1_w_gate, v_ffn1_w_up, v_ffn1_w_down, v_mix_norm, v_w_in, v_w_pool, v_pool_scale, v_w_out, v_mem_q_norm, v_mem_kv_norm, v_mem_w_q, v_mem_w_kv, v_mem_w_o, v_ffn2_norm, v_ffn2_w_gate, v_ffn2_w_up, v_ffn2_w_down, v_final_norm):
    B, S, D = x.shape
    T = B * S
    x0 = x.reshape(T, D)
    target = loss_target.reshape(T, D)
    final_gain = final_norm.reshape(1, D)

    big = dict(
        g1=ffn1_w_gate[0].T, u1=ffn1_w_up[0].T, d1=ffn1_w_down[0],
        g2=ffn2_w_gate[0].T, u2=ffn2_w_up[0].T, d2=ffn2_w_down[0],
        w_in=w_in[0], w_out=w_out[0], w_q=mem_w_q[0], w_kv=mem_w_kv[0], w_o=mem_w_o[0])
    names = list(big)
    shard = {k: big[k].astype(BF16) for k in names}
    wp = w_pool[0].astype(BF16)
    full, ffn_w = {}, {}

    def gathered(keys, arrs):
        full.update(zip(keys, arrs))
        ffn_w.update({k: full[k].reshape(-1, D) for k in keys if k[0] in "gud"})

    first, mid = ("g1", "u1", "d1"), ("w_in", "w_out", "w_q", "w_kv", "w_o")
    gathered(first, _gather_two_level([shard[k] for k in first], "gather_ffn1"))
    (x1, hn1, a1, s1, t1), got = _ffn_fwd(x0, ffn1_norm, ffn_w["g1"], ffn_w["u1"], ffn_w["d1"], "ffn1_fwd",
                                          comm=([shard[k] for k in mid], True))
    gathered(mid, got)
    hn2, qkv, u = _mix_in_fwd(x1, mix_norm, full["w_in"])
    qkv3 = qkv.reshape(B, S, -1)
    (o_sb,), got = _sb_fwd(qkv3, B, S, comm=([shard["g2"], shard["u2"]], True))
    gathered(("g2", "u2"), got)
    pooled = _pool_fwd(u.reshape(B, S, -1), B, S).reshape(T, -1)
    x2, mixed = _mix_out_fwd(x1, o_sb.reshape(T, -1), pooled, wp, pool_scale, full["w_out"])
    memn, kv = _mem_kv_fwd(mem, mem_kv_norm, full["w_kv"])
    (x3, hq, q, ocat), got = _cross_fwd(x2, mem_q_norm, kv, full["w_q"], full["w_o"], B, S,
                                        comm=([shard["d2"]], True))
    gathered(("d2",), got)
    (dx4, hn4, a2, s2, t2, d_final, loss_part), _ = _ffn_fwd(x3, ffn2_norm, ffn_w["g2"], ffn_w["u2"], ffn_w["d2"],
                                                            "ffn2_fwd", head=(final_gain, target))

    slab = lambda k: grads[k].reshape((N_DEV, -1) + grads[k].shape[-1:])
    got = {}
    dx3, dg2, du2, dyh2, d_ffn2 = _ffn_bwd(dx4, x3, ffn2_norm, s2, t2, ffn_w["g2"], ffn_w["u2"],
                                          ffn_w["d2"], "ffn2_bwd")
    ffn_slab = ffn1_w_gate.shape[2]
    grads = dict(g2=_wgrad(hn4, dg2, "dw_gate2", col_slab=ffn_slab), u2=_wgrad(hn4, du2, "dw_up2", col_slab=ffn_slab),
                 d2=_wgrad(a2, dyh2, "dw_down2"))
    (dx2, dx3b, dqb, dkv, d_q), (got["g2"],) = _cross_bwd(dx3, x2, mem_q_norm, q, kv, full["w_q"], full["w_o"], B, S,
                                                         comm=([slab("g2")], False))
    grads["w_o"] = _wgrad(ocat, dx3b, "dw_o")
    grads["w_q"] = _wgrad(hq, dqb, "dw_q")
    dkvb, d_kv = _mem_kv_bwd(dkv, mem, full["w_kv"])
    grads["w_kv"] = _wgrad(memn, dkvb, "dw_kv", col_slab=full["w_kv"].shape[2])
    dx2b, do_sb, dpooled, d_wpool, d_ps = _mix_out_bwd(dx2, pooled, wp, pool_scale, full["w_out"])
    grads["w_out"] = _wgrad(mixed, dx2b, "dw_out")
    du = _pool_bwd(dpooled.reshape(B, S, -1), B, S).reshape(T, -1)
    early = ("u2", "d2", "w_o", "w_q", "w_kv", "w_out")
    (dq, dk, dv), res = _sb_bwd(qkv3, do_sb.reshape(B, S, -1), B, S, comm=([slab(k) for k in early], False))
    got.update(zip(early, res))
    dx1, dproj, d_mix = _mix_in_bwd(dx2, dq.reshape(T, -1), dk.reshape(T, -1), dv.reshape(T, -1), du,
                                    x1, mix_norm, full["w_in"])
    grads["w_in"] = _wgrad(hn2, dproj, "dw_in", col_slab=full["w_in"].shape[2])
    dx0, dg1, du1, dyh1, d_ffn1 = _ffn_bwd(dx1, x0, ffn1_norm, s1, t1, ffn_w["g1"], ffn_w["u1"],
                                          ffn_w["d1"], "ffn1_bwd")

    small = [("ffn1_norm", d_ffn1, ffn1_norm, m_ffn1_norm, v_ffn1_norm),
             ("mix_norm", d_mix, mix_norm, m_mix_norm, v_mix_norm),
             ("pool_scale", d_ps, pool_scale, m_pool_scale, v_pool_scale),
             ("mem_q_norm", d_q, mem_q_norm, m_mem_q_norm, v_mem_q_norm),
             ("mem_kv_norm", d_kv, mem_kv_norm, m_mem_kv_norm, v_mem_kv_norm),
             ("ffn2_norm", d_ffn2, ffn2_norm, m_ffn2_norm, v_ffn2_norm),
             ("final_norm", d_final, final_gain, m_final_norm.reshape(1, D), v_final_norm.reshape(1, D))]
    row_pack = _pack_rows([t[1] for t in small] + [loss_part], D)
    as_rows = lambda t: t.reshape(-1, LANES)
    grads["g1"], (row_parts, pool_parts, got["w_in"]) = _wgrad(
        hn1, dg1, "dw_gate1", col_slab=ffn_slab,
        comm=([row_pack, as_rows(d_wpool), slab("w_in")], [True, True, False]))
    grads["u1"], (got["g1"],) = _wgrad(hn1, du1, "dw_up1", col_slab=ffn_slab, comm=([slab("g1")], False))
    grads["d1"], (got["u1"],) = _wgrad(a1, dyh1, "dw_down1", comm=([slab("u1")], False))
    got["d1"] = _exchange([slab("d1")], False, "scatter_last")[0]

    state = dict(
        g1=(ffn1_w_gate, m_ffn1_w_gate, v_ffn1_w_gate), u1=(ffn1_w_up, m_ffn1_w_up, v_ffn1_w_up),
        d1=(ffn1_w_down, m_ffn1_w_down, v_ffn1_w_down), g2=(ffn2_w_gate, m_ffn2_w_gate, v_ffn2_w_gate),
        u2=(ffn2_w_up, m_ffn2_w_up, v_ffn2_w_up), d2=(ffn2_w_down, m_ffn2_w_down, v_ffn2_w_down),
        w_in=(w_in, m_w_in, v_w_in), w_out=(w_out, m_w_out, v_w_out), w_q=(mem_w_q, m_mem_w_q, v_mem_w_q),
        w_kv=(mem_w_kv, m_mem_w_kv, v_mem_w_kv), w_o=(mem_w_o, m_mem_w_o, v_mem_w_o))
    big_out = {}
    for k in names:
        w_, m_, v_ = (t[0] for t in state[k])
        big_out[k] = [t[None] for t in _adamw(got[k], w_, m_, v_, "adamw_" + k)]

    small_res, loss_row = _adamw_small(row_parts, pool_parts, [t[2:] for t in small],
                                       [as_rows(t) for t in (w_pool, m_w_pool, v_w_pool)])
    small_out = {t[0]: small_res[i] for i, t in enumerate(small)}
    small_out["final_norm"] = [t.reshape(D) for t in small_out["final_norm"]]
    small_out["w_pool"] = [t.reshape(w_pool.shape) for t in small_res[-1]]
    loss = loss_row[0, 0]

    order = [("ffn1_norm", None), ("ffn1_w_gate", "g1"), ("ffn1_w_up", "u1"), ("ffn1_w_down", "d1"),
             ("mix_norm", None), ("w_in", "w_in"), ("w_pool", None), ("pool_scale", None), ("w_out", "w_out"),
             ("mem_q_norm", None), ("mem_kv_norm", None), ("mem_w_q", "w_q"), ("mem_w_kv", "w_kv"),
             ("mem_w_o", "w_o"), ("ffn2_norm", None), ("ffn2_w_gate", "g2"), ("ffn2_w_up", "u2"),
             ("ffn2_w_down", "d2"), ("final_norm", None)]
    res = [loss, dx0.reshape(B, S, D)]
    for which in range(4):
        for name, key in order:
            res.append(big_out[key][which] if key else small_out[name][which])
    return tuple(res)
```

```python
import functools
import math

import jax
import jax.numpy as jnp
from jax import lax
from jax.experimental import pallas as pl
from jax.experimental.pallas import tpu as pltpu

F32 = jnp.float32
BF16 = jnp.bfloat16

N_DEV = 8
EPS = 1e-6
SB_HEAD_DIM = 64
LANES = 128
POOL_WINDOWS = (2, 4, 8, 16)
POOL_GROUP_DIM = 128
MEM_HEADS = 4
FFN_RESIDUAL_WEIGHT = 0.5
ADAM_LR = 0.001
ADAM_B1 = 0.9
ADAM_B2 = 0.999
ADAM_EPS = 1e-08
ADAM_WD = 0.01
ADAM_STEP = 10
VMEM_LIMIT = 56 * 1024 * 1024

MESH_ID = pl.DeviceIdType.MESH


def _params(*sem):
    return pltpu.CompilerParams(dimension_semantics=sem, vmem_limit_bytes=VMEM_LIMIT)


def _tile(n, pref):
    if n <= pref:
        return n
    t = pref - pref % 8
    while n % t:
        t -= 8
    return t


def _mm(a, b):
    return jnp.dot(a, b, preferred_element_type=F32)


def _mm_nt(a, b):
    return lax.dot_general(a, b, (((1,), (1,)), ((), ())), preferred_element_type=F32)


def _mm_tn(a, b):
    return lax.dot_general(a, b, (((0,), (0,)), ((), ())), preferred_element_type=F32)


def _rms(xv):
    r = lax.rsqrt(jnp.mean(xv * xv, axis=-1, keepdims=True) + EPS)
    return r, xv * r


def _rms_bwd(dhn, gain, r, xhat):
    dxh = dhn * gain
    return r * (dxh - xhat * jnp.mean(dxh * xhat, axis=-1, keepdims=True))


def _sigmoid(z):
    return 0.5 * jnp.tanh(0.5 * z) + 0.5


def _flags(arrs, gather):
    return [gather] * len(arrs) if isinstance(gather, bool) else list(gather)


def _comm_shapes(arrs, gather):
    return tuple(jax.ShapeDtypeStruct(((N_DEV,) + tuple(a.shape)) if f else tuple(a.shape), a.dtype)
                 for a, f in zip(arrs, _flags(arrs, gather)))


def _comm_start(ins, outs, sems, gather):
    send_sems, recv_sems, local_sems = sems
    gather = _flags(ins, gather)
    x, y, c = lax.axis_index("x"), lax.axis_index("y"), lax.axis_index("c")
    me = 4 * x + 2 * y + c
    for i in range(len(ins)):
        src = ins[i] if gather[i] else ins[i].at[me]
        pltpu.make_async_copy(src, outs[i].at[me], local_sems.at[i]).start()
    for k in range(1, N_DEV):
        px = 1 - x if k & 4 else x
        py = 1 - y if k & 2 else y
        pc = 1 - c if k & 1 else c
        peer = 4 * px + 2 * py + pc
        for i in range(len(ins)):
            src = ins[i] if gather[i] else ins[i].at[peer]
            pltpu.make_async_remote_copy(
                src_ref=src, dst_ref=outs[i].at[me],
                send_sem=send_sems.at[i], recv_sem=recv_sems.at[i],
                device_id=(px, py, pc), device_id_type=MESH_ID).start()


def _comm_wait(ins, outs, sems, gather):
    send_sems, recv_sems, local_sems = sems
    gather = _flags(ins, gather)
    x, y, c = lax.axis_index("x"), lax.axis_index("y"), lax.axis_index("c")
    me = 4 * x + 2 * y + c
    for i in range(len(ins)):
        seven = outs[i].at[pl.ds(0, N_DEV - 1)]
        done = pltpu.make_async_remote_copy(
            src_ref=seven, dst_ref=seven,
            send_sem=send_sems.at[i], recv_sem=recv_sems.at[i],
            device_id=(x, y, c), device_id_type=MESH_ID)
        done.wait_send()
        done.wait_recv()
        src = ins[i] if gather[i] else ins[i].at[me]
        pltpu.make_async_copy(src, outs[i].at[me], local_sems.at[i]).wait()


def _comm_sems(n):
    return [pltpu.SemaphoreType.DMA((n,)) for _ in range(3)]


def _exchange(arrs, gather, name):
    n = len(arrs)

    def body(*refs):
        ins, outs, sems = refs[:n], refs[n:2 * n], refs[2 * n:]
        _comm_start(ins, outs, sems, gather)
        _comm_wait(ins, outs, sems, gather)

    any_spec = pl.BlockSpec(memory_space=pl.ANY)
    outs = pl.pallas_call(
        body, name=name, out_shape=_comm_shapes(arrs, gather),
        in_specs=[any_spec] * n, out_specs=tuple([any_spec] * n), scratch_shapes=_comm_sems(n),
    )(*arrs)
    return list(outs)


def _gather_two_level(arrs, name):
    n = len(arrs)

    def body(*refs):
        ins, outs = refs[:n], refs[n:2 * n]
        send_sems, recv_sems, local_sems = refs[2 * n:]
        x, y, c = lax.axis_index("x"), lax.axis_index("y"), lax.axis_index("c")
        me, sibling = (x, y, c), (x, y, 1 - c)
        chips = [(1 - x, y), (x, 1 - y), (1 - x, 1 - y)]

        def copy(i, k, block, to, own=False):
            slab = outs[i].at[4 * block[0] + 2 * block[1] + block[2]]
            return pltpu.make_async_remote_copy(
                src_ref=ins[i] if own else slab, dst_ref=slab,
                send_sem=send_sems.at[i, k], recv_sem=recv_sems.at[i, k],
                device_id=to, device_id_type=MESH_ID)

        mine = [pltpu.make_async_copy(ins[i], outs[i].at[4 * x + 2 * y + c], local_sems.at[i]) for i in range(n)]
        first = [copy(i, 0, me, sibling, own=True) for i in range(n)]
        first += [copy(i, 1 + j, me, (*chip, c), own=True) for j, chip in enumerate(chips) for i in range(n)]
        for cp in mine + first:
            cp.start()
        passed = []
        for j, chip in enumerate(chips):
            for i in range(n):
                copy(i, 1 + j, (*chip, c), me).wait_recv()
                passed.append(copy(i, 4 + j, (*chip, c), sibling))
                passed[-1].start()
        for i in range(n):
            copy(i, 0, sibling, me).wait_recv()
            for j, chip in enumerate(chips):
                copy(i, 4 + j, (*chip, 1 - c), me).wait_recv()
        for cp in first + passed:
            cp.wait_send()
        for cp in mine:
            cp.wait()

    any_spec = pl.BlockSpec(memory_space=pl.ANY)
    outs = pl.pallas_call(
        body, name=name, out_shape=_comm_shapes(arrs, True),
        in_specs=[any_spec] * n, out_specs=tuple([any_spec] * n),
        scratch_shapes=[pltpu.SemaphoreType.DMA((n, N_DEV - 1)), pltpu.SemaphoreType.DMA((n, N_DEV - 1)),
                        pltpu.SemaphoreType.DMA((n,))],
    )(*arrs)
    return list(outs)


def _call(body, *, name, grid, in_specs, out_specs, out_shape, args, scratch=(), comm=None):
    sem = ("arbitrary",) * len(grid)
    if comm is None:
        res = pl.pallas_call(body, name=name, grid=grid, in_specs=list(in_specs), out_specs=tuple(out_specs),
                             out_shape=tuple(out_shape), scratch_shapes=list(scratch),
                             compiler_params=_params(*sem))(*args)
        return tuple(res), []
    arrs, gather = comm
    n, n_in, n_out, n_sc = len(arrs), len(args), len(out_shape), len(scratch)

    def wrapped(*refs):
        ins, cin = refs[:n_in], refs[n_in:n_in + n]
        outs, cout = refs[n_in + n:n_in + n + n_out], refs[n_in + n + n_out:n_in + 2 * n + n_out]
        sc, sems = refs[n_in + 2 * n + n_out:n_in + 2 * n + n_out + n_sc], refs[n_in + 2 * n + n_out + n_sc:]
        ids = [pl.program_id(a) for a in range(len(grid))]
        first = functools.reduce(jnp.logical_and, [i == 0 for i in ids])
        last = functools.reduce(jnp.logical_and, [i == g - 1 for i, g in zip(ids, grid)])

        @pl.when(first)
        def _():
            _comm_start(cin, cout, sems, gather)

        body(*ins, *outs, *sc)

        @pl.when(last)
        def _():
            _comm_wait(cin, cout, sems, gather)

    any_spec = pl.BlockSpec(memory_space=pl.ANY)
    res = pl.pallas_call(
        wrapped, name=name, grid=grid, in_specs=list(in_specs) + [any_spec] * n,
        out_specs=tuple(out_specs) + (any_spec,) * n, out_shape=tuple(out_shape) + _comm_shapes(arrs, gather),
        scratch_shapes=list(scratch) + _comm_sems(n), compiler_params=_params(*sem))(*args, *arrs)
    return tuple(res[:n_out]), list(res[n_out:])


FFN_BWD_ROWS = 256


def _load_resident(pairs, sem):
    copies = [pltpu.make_async_copy(src, dst, sem.at[k]) for k, (src, dst) in enumerate(pairs)]
    for cp in copies:
        cp.start()
    for cp in copies:
        cp.wait()


def _ffn_fwd(x, gain, wgt, wut, wd, name, comm=None, head=None):
    T, D = x.shape
    F = wd.shape[0]
    tm, tf = _tile(T, 512), _tile(F, 256)

    def body(*refs):
        if head:
            (x_ref, gain_ref, wg_hbm, wu_hbm, wd_hbm, fgain_ref, tgt_ref,
             out_ref, hn_ref, a_ref, s_ref, t_ref, dfgain_ref, loss_ref, wg_s, wu_s, wd_s, sem) = refs
        else:
            (x_ref, gain_ref, wg_hbm, wu_hbm, wd_hbm,
             out_ref, hn_ref, a_ref, s_ref, t_ref, wg_s, wu_s, wd_s, sem) = refs

        @pl.when(pl.program_id(0) == 0)
        def _():
            _load_resident([(wg_hbm, wg_s), (wu_hbm, wu_s), (wd_hbm, wd_s)], sem)
            if head:
                dfgain_ref[...] = jnp.zeros_like(dfgain_ref)
                loss_ref[...] = jnp.zeros_like(loss_ref)

        _, xhat = _rms(x_ref[...])
        hn = (xhat * gain_ref[...]).astype(BF16)
        hn_ref[...] = hn
        for f0 in range(0, F, tf):
            cols = slice(f0, f0 + tf)
            g = _mm_nt(hn, wg_s[cols, :])
            u = _mm_nt(hn, wu_s[cols, :])
            sig = _sigmoid(g)
            s = g * sig
            a_ref[:, cols] = (s * u).astype(BF16)
            s_ref[:, cols] = s.astype(BF16)
            t_ref[:, cols] = (u * (sig + s * (1.0 - sig))).astype(BF16)
        y = x_ref[...] + FFN_RESIDUAL_WEIGHT * _mm(a_ref[...], wd_s[...])
        if head:
            r, yhat = _rms(y)
            err = yhat * fgain_ref[...] - tgt_ref[...]
            loss_ref[...] += 0.5 * jnp.sum(jnp.mean(err * err, axis=-1, keepdims=True), axis=0, keepdims=True)
            dy = err * (1.0 / D)
            dfgain_ref[...] += jnp.sum(dy * yhat, axis=0, keepdims=True)
            out_ref[...] = _rms_bwd(dy, fgain_ref[...], r, yhat)
        else:
            out_ref[...] = y

    row = lambda i: (i, 0)
    one = pl.BlockSpec((1, D), lambda i: (0, 0))
    hbm = pl.BlockSpec(memory_space=pl.ANY)
    in_specs = [pl.BlockSpec((tm, D), row), one, hbm, hbm, hbm]
    out_specs = [pl.BlockSpec((tm, D), row), pl.BlockSpec((tm, D), row)] + [pl.BlockSpec((tm, F), row) for _ in range(3)]
    out_shape = [jax.ShapeDtypeStruct((T, D), F32), jax.ShapeDtypeStruct((T, D), BF16)] \
        + [jax.ShapeDtypeStruct((T, F), BF16) for _ in range(3)]
    args = (x, gain, wgt, wut, wd)
    if head:
        in_specs += [one, pl.BlockSpec((tm, D), row)]
        out_specs += [one, one]
        out_shape += [jax.ShapeDtypeStruct((1, D), F32), jax.ShapeDtypeStruct((1, D), F32)]
        args += tuple(head)
    return _call(
        body, name=name, grid=(T // tm,), comm=comm, in_specs=in_specs, out_specs=out_specs, out_shape=out_shape,
        scratch=[pltpu.VMEM((F, D), BF16) for _ in range(3)] + [pltpu.SemaphoreType.DMA((3,))], args=args)


def _ffn_bwd(dy, x, gain, s, t, wgt, wut, wd, name):
    T, D = x.shape
    F = wd.shape[0]
    tr, tf = _tile(T, FFN_BWD_ROWS), _tile(F, 256)
    rows = lambda i: (i, 0)
    one = pl.BlockSpec((1, D), lambda i: (0, 0))
    any_spec = pl.BlockSpec(memory_space=pl.ANY)

    def body(dy_ref, x_ref, gain_ref, s_ref, t_ref, wg_hbm, wu_hbm, wd_hbm,
             dx_ref, dg_ref, du_ref, dyh_ref, dgain_ref, wg_s, wu_s, wd_s, sem):
        @pl.when(pl.program_id(0) == 0)
        def _():
            _load_resident([(wg_hbm, wg_s), (wu_hbm, wu_s), (wd_hbm, wd_s)], sem)
            dgain_ref[...] = jnp.zeros_like(dgain_ref)

        dyh = (FFN_RESIDUAL_WEIGHT * dy_ref[...]).astype(BF16)
        dyh_ref[...] = dyh
        for f0 in range(0, F, tf):
            cols = slice(f0, f0 + tf)
            da = _mm_nt(dyh, wd_s[cols, :])
            dg_ref[:, cols] = (da * t_ref[:, cols].astype(F32)).astype(BF16)
            du_ref[:, cols] = (da * s_ref[:, cols].astype(F32)).astype(BF16)
        dhn = _mm(dg_ref[...], wg_s[...]) + _mm(du_ref[...], wu_s[...])
        r, xhat = _rms(x_ref[...])
        dgain_ref[...] += jnp.sum(dhn * xhat, axis=0, keepdims=True)
        dx_ref[...] = dy_ref[...] + _rms_bwd(dhn, gain_ref[...], r, xhat)

    wide = jax.ShapeDtypeStruct((T, F), BF16)
    return pl.pallas_call(
        body, name=name, grid=(T // tr,),
        in_specs=[pl.BlockSpec((tr, D), rows), pl.BlockSpec((tr, D), rows), one, pl.BlockSpec((tr, F), rows),
                  pl.BlockSpec((tr, F), rows), any_spec, any_spec, any_spec],
        out_specs=(pl.BlockSpec((tr, D), rows), pl.BlockSpec((tr, F), rows), pl.BlockSpec((tr, F), rows),
                   pl.BlockSpec((tr, D), rows), one),
        out_shape=(jax.ShapeDtypeStruct((T, D), F32), wide, wide, jax.ShapeDtypeStruct((T, D), BF16),
                   jax.ShapeDtypeStruct((1, D), F32)),
        scratch_shapes=[pltpu.VMEM((F, D), BF16) for _ in range(3)] + [pltpu.SemaphoreType.DMA((3,))],
        compiler_params=_params("arbitrary"),
    )(dy, x, gain, s, t, wgt, wut, wd)


def _wgrad(a, b, name, col_slab=None, comm=None):
    T, M = a.shape
    N = b.shape[1]
    tmm = M if M <= 1024 else _tile(M, 1408)
    tn = _tile(N, 1024)
    if col_slab and col_slab % LANES:
        tn = col_slab * LANES // math.gcd(col_slab, LANES)
    tk = _tile(T, 1024)
    nk = T // tk
    per = tn // col_slab if col_slab else 0

    def body(a_ref, b_ref, out_ref, acc):
        k = pl.program_id(2)

        @pl.when(k == 0)
        def _():
            acc[...] = jnp.zeros_like(acc)

        acc[...] += _mm_tn(a_ref[...], b_ref[...])

        @pl.when(k == nk - 1)
        def _():
            if col_slab:
                for s in range(per):
                    out_ref[s] = acc[:, s * col_slab:(s + 1) * col_slab].astype(BF16)
            else:
                out_ref[...] = acc[...].astype(BF16)

    if col_slab:
        out_spec = pl.BlockSpec((per, tmm, col_slab), lambda m, n, k: (n, m, 0))
        out_shape = jax.ShapeDtypeStruct((N // col_slab, M, col_slab), BF16)
    else:
        out_spec = pl.BlockSpec((tmm, tn), lambda m, n, k: (m, n))
        out_shape = jax.ShapeDtypeStruct((M, N), BF16)
    (out,), got = _call(
        body, name=name, grid=(M // tmm, N // tn, nk), comm=comm,
        in_specs=[pl.BlockSpec((tk, tmm), lambda m, n, k: (k, m)), pl.BlockSpec((tk, tn), lambda m, n, k: (k, n))],
        out_specs=(out_spec,), out_shape=(out_shape,),
        scratch=[pltpu.VMEM((tmm, tn), F32)], args=(a, b))
    return (out, got) if comm else out


def _mix_in_fwd(x, gain, w_int):
    T, D = x.shape
    C = w_int.shape[0]
    n_qkv = 3 * C // 4
    tm = _tile(T, 1024)

    def body(x_ref, gain_ref, w_ref, hn_ref, qkv_ref, u_ref):
        _, xhat = _rms(x_ref[...])
        hn = (xhat * gain_ref[...]).astype(BF16)
        hn_ref[...] = hn
        proj = _mm_nt(hn, w_ref[...])
        qkv_ref[...] = proj[:, :n_qkv].astype(BF16)
        u_ref[...] = proj[:, n_qkv:]

    row = lambda i: (i, 0)
    return pl.pallas_call(
        body, name="mix_in_fwd", grid=(T // tm,),
        in_specs=[pl.BlockSpec((tm, D), row), pl.BlockSpec((1, D), lambda i: (0, 0)),
                  pl.BlockSpec((C, D), lambda i: (0, 0))],
        out_specs=(pl.BlockSpec((tm, D), row), pl.BlockSpec((tm, n_qkv), row), pl.BlockSpec((tm, C - n_qkv), row)),
        out_shape=(jax.ShapeDtypeStruct((T, D), BF16), jax.ShapeDtypeStruct((T, n_qkv), BF16),
                   jax.ShapeDtypeStruct((T, C - n_qkv), F32)),
        compiler_params=_params("arbitrary"),
    )(x, gain, w_int)


def _mix_in_bwd(dres, dq, dk, dv, du, x, gain, w_int):
    T, D = x.shape
    C = w_int.shape[0]
    W = dq.shape[1]
    tm = _tile(T, 512)

    def body(dres_ref, dq_ref, dk_ref, dv_ref, du_ref, x_ref, gain_ref, w_ref, dx_ref, dproj_ref, dgain_ref):
        @pl.when(pl.program_id(0) == 0)
        def _():
            dgain_ref[...] = jnp.zeros_like(dgain_ref)

        for part, ref in enumerate((dq_ref, dk_ref, dv_ref, du_ref)):
            dproj_ref[:, part * W:(part + 1) * W] = ref[...]
        dhn = _mm(dproj_ref[...], w_ref[...])
        r, xhat = _rms(x_ref[...])
        dgain_ref[...] += jnp.sum(dhn * xhat, axis=0, keepdims=True)
        dx_ref[...] = dres_ref[...] + _rms_bwd(dhn, gain_ref[...], r, xhat)

    row = lambda i: (i, 0)
    one = pl.BlockSpec((1, D), lambda i: (0, 0))
    part = pl.BlockSpec((tm, W), row)
    return pl.pallas_call(
        body, name="mix_in_bwd", grid=(T // tm,),
        in_specs=[pl.BlockSpec((tm, D), row), part, part, part, part, pl.BlockSpec((tm, D), row), one,
                  pl.BlockSpec((C, D), lambda i: (0, 0))],
        out_specs=(pl.BlockSpec((tm, D), row), pl.BlockSpec((tm, C), row), one),
        out_shape=(jax.ShapeDtypeStruct((T, D), F32), jax.ShapeDtypeStruct((T, C), BF16),
                   jax.ShapeDtypeStruct((1, D), F32)),
        compiler_params=_params("arbitrary"),
    )(dres, dq, dk, dv, du, x, gain, w_int)


SB_PAIRS_PER_PROGRAM = 2
LOG2_E = 1.4426950408889634
EXP2_CLAMP = 126.0


def _neg_log2_sigmoid(nz2):
    w = jnp.minimum(nz2, EXP2_CLAMP)
    return w, jnp.log2(1.0 + jnp.exp2(w))


SB_DEAD_LOG2 = -160.0


def _sb_live(rests):
    worst = functools.reduce(jnp.maximum, rests)
    return (jnp.max(worst) > SB_DEAD_LOG2).astype(jnp.int32)


def _split(v):
    hi = v.astype(BF16)
    return hi, (v - hi.astype(F32)).astype(BF16)


def _tri_sum(v, tri):
    hi, lo = _split(v)
    return _mm(hi, tri) + _mm(lo, tri)


def _sb_fwd(qkv, B, S, comm=None):
    W = qkv.shape[2] // 3
    n_pair = W // LANES
    bq = _tile(S, 256)
    nq = S // bq
    hp = SB_PAIRS_PER_PROGRAM
    nscale2 = -(SB_HEAD_DIM ** -0.5) * LOG2_E

    def body(q_ref, k_ref, v_ref, o_ref):
        lane = lax.broadcasted_iota(jnp.int32, (1, LANES), 1)
        head0 = lane < SB_HEAD_DIM
        rr = lax.broadcasted_iota(jnp.int32, (bq, bq), 0)
        cc = lax.broadcasted_iota(jnp.int32, (bq, bq), 1)
        strict = cc < rr
        after = jnp.where(rr > cc, 1.0, 0.0).astype(BF16)

        def blocks(heads, ks, carries, diag):
            n = range(len(heads))
            keep = (lambda t: jnp.where(strict, t, 0.0)) if diag else (lambda t: t)
            z = [_mm_nt(qh, k_ref[ks, cols]) for qh, cols in heads]
            wl = [_neg_log2_sigmoid(z[h] * nscale2) for h in n]
            lr = [keep(wl[h][0] - wl[h][1]) for h in n]
            parts = [_split(lr[h]) for h in n]
            suf = [_mm(parts[h][0], after) + _mm(parts[h][1], after) for h in n]
            a = [keep(jnp.exp2(suf[h] + carries[h][1] - wl[h][1])).astype(BF16) for h in n]
            o = [carries[h][0] + _mm(a[h], v_ref[ks, heads[h][1]]) for h in n]
            return tuple((o[h], carries[h][1] + (suf[h][:, :1] + lr[h][:, :1])) for h in n)

        def q_tile(i, _):
            qs = pl.ds(pl.multiple_of(i * bq, bq), bq)
            heads = []
            for pr in range(hp):
                cols = slice(pr * LANES, (pr + 1) * LANES)
                qv = q_ref[qs, cols]
                heads += [(jnp.where(head0, qv, jnp.zeros_like(qv)), cols),
                          (jnp.where(head0, jnp.zeros_like(qv), qv), cols)]
            zero = (jnp.zeros((bq, LANES), F32), jnp.zeros((bq, 1), F32))
            init = blocks(heads, qs, (zero,) * len(heads), True)

            def left(st):
                t, _, cr = st
                ks = pl.ds(pl.multiple_of((i - 1 - t) * bq, bq), bq)
                cr = blocks(heads, ks, cr, False)
                return t + 1, _sb_live([c for _, c in cr]), cr

            _, _, res = lax.while_loop(lambda st: jnp.logical_and(st[0] < i, st[1] > 0), left,
                                       (jnp.int32(0), _sb_live([c for _, c in init]), init))
            for pr in range(hp):
                o_ref[qs, heads[2 * pr][1]] = jnp.where(head0, res[2 * pr][0], res[2 * pr + 1][0]).astype(BF16)
            return 0

        lax.fori_loop(0, nq, q_tile, 0)

    def col(off):
        return pl.BlockSpec((None, S, hp * LANES), lambda b, p: (b, 0, off + p))

    n_pair //= hp
    return _call(
        body, name="sb_fwd", grid=(B, n_pair), comm=comm,
        in_specs=[col(0), col(n_pair), col(2 * n_pair)],
        out_specs=(col(0),),
        out_shape=(jax.ShapeDtypeStruct((B, S, W), BF16),),
        args=(qkv, qkv, qkv))


def _sb_bwd(qkv, do, B, S, comm=None):
    W = qkv.shape[2] // 3
    n_pair = W // LANES
    bq = _tile(S, 256)
    nq = S // bq
    hp = SB_PAIRS_PER_PROGRAM
    scale = SB_HEAD_DIM ** -0.5
    nscale2 = -scale * LOG2_E

    def body(q_ref, k_ref, v_ref, do_ref, dq_ref, dk_ref, dv_ref, dk_s, dv_s, e_s, sg_s, a_s):
        lane = lax.broadcasted_iota(jnp.int32, (1, LANES), 1)
        head0 = lane < SB_HEAD_DIM
        rr = lax.broadcasted_iota(jnp.int32, (bq, bq), 0)
        cc = lax.broadcasted_iota(jnp.int32, (bq, bq), 1)
        strict = cc < rr
        after = jnp.where(rr > cc, 1.0, 0.0).astype(BF16)
        before = jnp.where(rr < cc, 1.0, 0.0).astype(BF16)
        dk_s[...] = jnp.zeros_like(dk_s)
        dv_s[...] = jnp.zeros_like(dv_s)

        def weights(heads, ks, kb, rests, diag):
            n = range(len(heads))
            keep = (lambda t: jnp.where(strict, t, 0.0)) if diag else (lambda t: t)
            z = [_mm_nt(heads[h][0], k_ref[ks, heads[h][2]]) for h in n]
            da = [_mm_nt(heads[h][1], v_ref[ks, heads[h][2]]) for h in n]
            wl = [_neg_log2_sigmoid(z[h] * nscale2) for h in n]
            lr = [keep(wl[h][0] - wl[h][1]) for h in n]
            parts = [_split(lr[h]) for h in n]
            suf = [_mm(parts[h][0], after) + _mm(parts[h][1], after) for h in n]
            a = [keep(jnp.exp2(suf[h] + rests[h] - wl[h][1])) for h in n]
            for h in n:
                a_s[h * nq + kb] = a[h].astype(BF16)
                e_s[h * nq + kb] = a[h] * da[h]
                sg_s[h * nq + kb] = jnp.exp2(-wl[h][1])
            return tuple(rests[h] + (suf[h][:, :1] + lr[h][:, :1]) for h in n)

        def grads(heads, ks, kb, carries, diag):
            n = range(len(heads))
            keep = (lambda t: jnp.where(strict, t, 0.0)) if diag else (lambda t: t)
            e = [e_s[h * nq + kb] for h in n]
            parts = [_split(e[h]) for h in n]
            pex = [_mm(parts[h][0], before) + _mm(parts[h][1], before) for h in n]
            dz = [keep(e[h] - sg_s[h * nq + kb] * (e[h] + pex[h] + carries[h][1])).astype(BF16) for h in n]
            dq = [carries[h][0] + _mm(dz[h], k_ref[ks, heads[h][2]]) for h in n]
            for h in n:
                dk_s[ks, heads[h][2]] += _mm_tn(dz[h], heads[h][0])
                dv_s[ks, heads[h][2]] += _mm_tn(a_s[h * nq + kb], heads[h][1])
            return tuple((dq[h], carries[h][1] + (pex[h][:, bq - 1:] + e[h][:, bq - 1:])) for h in n)

        def q_tile(i, _):
            qs = pl.ds(pl.multiple_of(i * bq, bq), bq)
            heads = []
            for pr in range(hp):
                cols = slice(pr * LANES, (pr + 1) * LANES)
                qv, dov = q_ref[qs, cols], do_ref[qs, cols]
                zq, zd = jnp.zeros_like(qv), jnp.zeros_like(dov)
                heads += [(jnp.where(head0, qv, zq), jnp.where(head0, dov, zd), cols),
                          (jnp.where(head0, zq, qv), jnp.where(head0, zd, dov), cols)]
            key_block = lambda kb: pl.ds(pl.multiple_of(kb * bq, bq), bq)
            rests = weights(heads, qs, i, (jnp.zeros((bq, 1), F32),) * len(heads), True)

            def left(st):
                t, _, rs = st
                rs = weights(heads, key_block(i - 1 - t), i - 1 - t, rs, False)
                return t + 1, _sb_live(rs), rs

            n_left, _, _ = lax.while_loop(lambda st: jnp.logical_and(st[0] < i, st[1] > 0), left,
                                          (jnp.int32(0), _sb_live(rests), rests))
            zero = (jnp.zeros((bq, LANES), F32), jnp.zeros((bq, 1), F32))
            res = lax.fori_loop(0, n_left, lambda t, cr: grads(heads, key_block(i - n_left + t), i - n_left + t, cr, False),
                                (zero,) * len(heads))
            res = grads(heads, qs, i, res, True)
            for pr in range(hp):
                dq = jnp.where(head0, res[2 * pr][0], res[2 * pr + 1][0])
                dq_ref[qs, heads[2 * pr][2]] = (dq * scale).astype(BF16)
            return 0

        lax.fori_loop(0, nq, q_tile, 0)
        dk_ref[...] = (dk_s[...] * scale).astype(BF16)
        dv_ref[...] = dv_s[...].astype(BF16)

    def col(off):
        return pl.BlockSpec((None, S, hp * LANES), lambda b, p: (b, 0, off + p))

    n_pair //= hp
    shp = jax.ShapeDtypeStruct((B, S, W), BF16)
    slots = 2 * hp * nq
    return _call(
        body, name="sb_bwd", grid=(B, n_pair), comm=comm,
        in_specs=[col(0), col(n_pair), col(2 * n_pair), col(0)],
        out_specs=(col(0), col(0), col(0)),
        out_shape=(shp, shp, shp),
        scratch=[pltpu.VMEM((S, hp * LANES), F32), pltpu.VMEM((S, hp * LANES), F32),
                 pltpu.VMEM((slots, bq, bq), F32), pltpu.VMEM((slots, bq, bq), F32),
                 pltpu.VMEM((slots, bq, bq), BF16)],
        args=(qkv, qkv, qkv, do))


def _pool_counts(S):
    t = lax.broadcasted_iota(jnp.int32, (S, 1), 0)
    return t, [jnp.minimum(t + 1, w).astype(F32) for w in POOL_WINDOWS]


def _pool_fwd(u, B, S):
    W = u.shape[2]

    def body(u_ref, out_ref):
        t, counts = _pool_counts(S)
        for gi, w in enumerate(POOL_WINDOWS):
            cols = slice(gi * POOL_GROUP_DIM, (gi + 1) * POOL_GROUP_DIM)
            ug = u_ref[:, cols]
            s, k = ug, 1
            while k < w:
                s = s + jnp.where(t >= k, pltpu.roll(s, k, axis=0), 0.0)
                k *= 2
            out_ref[:, cols] = (s / counts[gi] - ug).astype(BF16)

    spec = pl.BlockSpec((None, S, W), lambda b: (b, 0, 0))
    return pl.pallas_call(
        body, name="pool_fwd", grid=(B,), in_specs=[spec], out_specs=spec,
        out_shape=jax.ShapeDtypeStruct((B, S, W), BF16), compiler_params=_params("arbitrary"),
    )(u)


def _pool_bwd(dpooled, B, S):
    W = dpooled.shape[2]

    def body(d_ref, out_ref):
        t, counts = _pool_counts(S)
        for gi, w in enumerate(POOL_WINDOWS):
            cols = slice(gi * POOL_GROUP_DIM, (gi + 1) * POOL_GROUP_DIM)
            d = d_ref[:, cols]
            s, k = d / counts[gi], 1
            while k < w:
                s = s + jnp.where(t < S - k, pltpu.roll(s, S - k, axis=0), 0.0)
                k *= 2
            out_ref[:, cols] = (s - d).astype(BF16)

    spec = pl.BlockSpec((None, S, W), lambda b: (b, 0, 0))
    return pl.pallas_call(
        body, name="pool_bwd", grid=(B,), in_specs=[spec], out_specs=spec,
        out_shape=jax.ShapeDtypeStruct((B, S, W), BF16), compiler_params=_params("arbitrary"),
    )(dpooled)


def _mix_out_fwd(x, o_sb, pooled, w_pool, pool_scale, w_out):
    T, D = x.shape
    W = o_sb.shape[1]
    G = w_pool.shape[0]
    gd = POOL_GROUP_DIM
    tm = _tile(T, 1024)

    def body(x_ref, osb_ref, pooled_ref, wp_ref, ps_ref, wo_ref, out_ref, mixed_ref):
        mixed_ref[:, :W] = osb_ref[...]
        for gi in range(G):
            cols = slice(gi * gd, (gi + 1) * gd)
            pw = _mm(pooled_ref[:, cols], wp_ref[gi])
            mixed_ref[:, W + gi * gd:W + (gi + 1) * gd] = (pw * ps_ref[:, cols]).astype(BF16)
        out_ref[...] = x_ref[...] + _mm(mixed_ref[...], wo_ref[...].reshape(D, D))

    row = lambda i: (i, 0)
    return pl.pallas_call(
        body, name="mix_out_fwd", grid=(T // tm,),
        in_specs=[pl.BlockSpec((tm, D), row), pl.BlockSpec((tm, W), row), pl.BlockSpec((tm, W), row),
                  pl.BlockSpec((G, gd, gd), lambda i: (0, 0, 0)), pl.BlockSpec((1, W), lambda i: (0, 0)),
                  pl.BlockSpec(w_out.shape, lambda i: (0, 0, 0))],
        out_specs=(pl.BlockSpec((tm, D), row), pl.BlockSpec((tm, D), row)),
        out_shape=(jax.ShapeDtypeStruct((T, D), F32), jax.ShapeDtypeStruct((T, D), BF16)),
        compiler_params=_params("arbitrary"),
    )(x, o_sb, pooled, w_pool, pool_scale, w_out)


def _mix_out_bwd(dx, pooled, w_pool, pool_scale, w_out):
    T, D = dx.shape
    W = pooled.shape[1]
    G = w_pool.shape[0]
    gd = POOL_GROUP_DIM
    tm = _tile(T, 1024)

    def body(dx_ref, pooled_ref, wp_ref, ps_ref, wo_ref, dxb_ref, dosb_ref, dpooled_ref, dwp_ref, dps_ref):
        i = pl.program_id(0)

        @pl.when(i == 0)
        def _():
            dwp_ref[...] = jnp.zeros_like(dwp_ref)
            dps_ref[...] = jnp.zeros_like(dps_ref)

        dxb = dx_ref[...].astype(BF16)
        dxb_ref[...] = dxb
        dmixed = _mm_nt(dxb, wo_ref[...].reshape(D, D))
        dosb_ref[...] = dmixed[:, :W].astype(BF16)
        for gi in range(G):
            cols = slice(gi * gd, (gi + 1) * gd)
            pg = pooled_ref[:, cols]
            dop = dmixed[:, W + gi * gd:W + (gi + 1) * gd]
            pw = _mm(pg, wp_ref[gi])
            dps_ref[:, cols] += jnp.sum(dop * pw, axis=0, keepdims=True)
            dpw = (dop * ps_ref[:, cols]).astype(BF16)
            dwp_ref[gi] += _mm_tn(pg, dpw)
            dpooled_ref[:, cols] = _mm_nt(dpw, wp_ref[gi])

    row = lambda i: (i, 0)
    return pl.pallas_call(
        body, name="mix_out_bwd", grid=(T // tm,),
        in_specs=[pl.BlockSpec((tm, D), row), pl.BlockSpec((tm, W), row),
                  pl.BlockSpec((G, gd, gd), lambda i: (0, 0, 0)), pl.BlockSpec((1, W), lambda i: (0, 0)),
                  pl.BlockSpec(w_out.shape, lambda i: (0, 0, 0))],
        out_specs=(pl.BlockSpec((tm, D), row), pl.BlockSpec((tm, W), row), pl.BlockSpec((tm, W), row),
                   pl.BlockSpec((G, gd, gd), lambda i: (0, 0, 0)), pl.BlockSpec((1, W), lambda i: (0, 0))),
        out_shape=(jax.ShapeDtypeStruct((T, D), BF16), jax.ShapeDtypeStruct((T, W), BF16),
                   jax.ShapeDtypeStruct((T, W), F32), jax.ShapeDtypeStruct((G, gd, gd), F32),
                   jax.ShapeDtypeStruct((1, W), F32)),
        compiler_params=_params("arbitrary"),
    )(dx, pooled, w_pool, pool_scale, w_out)


def _mem_kv_fwd(mem, gain, w_kvt):
    B, M, D = mem.shape
    C = w_kvt.shape[0]

    def body(mem_ref, gain_ref, w_ref, memn_ref, kv_ref):
        _, xhat = _rms(mem_ref[...])
        mn = (xhat * gain_ref[...]).astype(BF16)
        memn_ref[...] = mn
        kv_ref[...] = _mm_nt(mn, w_ref[...]).astype(BF16)

    return pl.pallas_call(
        body, name="mem_kv_fwd", grid=(B,),
        in_specs=[pl.BlockSpec((None, M, D), lambda b: (b, 0, 0)), pl.BlockSpec((1, D), lambda b: (0, 0)),
                  pl.BlockSpec((C, D), lambda b: (0, 0))],
        out_specs=(pl.BlockSpec((M, D), lambda b: (b, 0)), pl.BlockSpec((None, M, C), lambda b: (b, 0, 0))),
        out_shape=(jax.ShapeDtypeStruct((B * M, D), BF16), jax.ShapeDtypeStruct((B, M, C), BF16)),
        compiler_params=_params("arbitrary"),
    )(mem, gain, w_kvt)


def _mem_kv_bwd(dkv, mem, w_kvt):
    B, M, D = mem.shape
    C = w_kvt.shape[0]

    def body(dkv_ref, mem_ref, w_ref, dkvb_ref, dgain_ref):
        @pl.when(pl.program_id(0) == 0)
        def _():
            dgain_ref[...] = jnp.zeros_like(dgain_ref)

        dkvb = dkv_ref[...].astype(BF16)
        dkvb_ref[...] = dkvb
        dmn = _mm(dkvb, w_ref[...])
        _, xhat = _rms(mem_ref[...])
        dgain_ref[...] += jnp.sum(dmn * xhat, axis=0, keepdims=True)

    return pl.pallas_call(
        body, name="mem_kv_bwd", grid=(B,),
        in_specs=[pl.BlockSpec((None, M, C), lambda b: (b, 0, 0)), pl.BlockSpec((None, M, D), lambda b: (b, 0, 0)),
                  pl.BlockSpec((C, D), lambda b: (0, 0))],
        out_specs=(pl.BlockSpec((M, C), lambda b: (b, 0)), pl.BlockSpec((1, D), lambda b: (0, 0))),
        out_shape=(jax.ShapeDtypeStruct((B * M, C), BF16), jax.ShapeDtypeStruct((1, D), F32)),
        compiler_params=_params("arbitrary"),
    )(dkv, mem, w_kvt)


def _softmax_rows(s):
    p = jnp.exp(s - jnp.max(s, axis=1, keepdims=True))
    return p / jnp.sum(p, axis=1, keepdims=True)


def _cross_fwd(x, gain, kv, w_q, w_o, B, S, comm=None):
    T, D = x.shape
    M = kv.shape[1]
    hd = D // MEM_HEADS
    tm = _tile(S, 1024)
    per = S // tm
    scale = hd ** -0.5

    def body(x_ref, gain_ref, kv_ref, wq_ref, wo_ref, out_ref, hq_ref, q_ref, ocat_ref):
        _, xhat = _rms(x_ref[...])
        hq = (xhat * gain_ref[...]).astype(BF16)
        hq_ref[...] = hq
        q = _mm(hq, wq_ref[...].reshape(D, D)).astype(BF16)
        q_ref[...] = q
        for h in range(MEM_HEADS):
            cols = slice(h * hd, (h + 1) * hd)
            s = _mm_nt(q[:, cols], kv_ref[:, cols]) * scale
            p = _softmax_rows(s).astype(BF16)
            ocat_ref[:, cols] = _mm(p, kv_ref[:, D + h * hd:D + (h + 1) * hd]).astype(BF16)
        out_ref[...] = x_ref[...] + _mm(ocat_ref[...], wo_ref[...].reshape(D, D))

    row = lambda b, t: (b * per + t, 0)
    wspec = pl.BlockSpec(w_q.shape, lambda b, t: (0, 0, 0))
    return _call(
        body, name="cross_fwd", grid=(B, per), comm=comm,
        in_specs=[pl.BlockSpec((tm, D), row), pl.BlockSpec((1, D), lambda b, t: (0, 0)),
                  pl.BlockSpec((None, M, 2 * D), lambda b, t: (b, 0, 0)), wspec, wspec],
        out_specs=tuple(pl.BlockSpec((tm, D), row) for _ in range(4)),
        out_shape=(jax.ShapeDtypeStruct((T, D), F32),) + tuple(jax.ShapeDtypeStruct((T, D), BF16) for _ in range(3)),
        args=(x, gain, kv, w_q, w_o))


def _cross_bwd(dy, x, gain, q, kv, w_q, w_o, B, S, comm=None):
    T, D = x.shape
    M = kv.shape[1]
    hd = D // MEM_HEADS
    tm = _tile(S, 512)
    per = S // tm
    scale = hd ** -0.5

    def body(dy_ref, x_ref, gain_ref, q_ref, kv_ref, wq_ref, wo_ref,
             dx_ref, dyb_ref, dqb_ref, dkv_ref, dgain_ref):
        b_id, t_id = pl.program_id(0), pl.program_id(1)

        @pl.when((b_id == 0) & (t_id == 0))
        def _():
            dgain_ref[...] = jnp.zeros_like(dgain_ref)

        @pl.when(t_id == 0)
        def _():
            dkv_ref[...] = jnp.zeros_like(dkv_ref)

        dyb = dy_ref[...].astype(BF16)
        dyb_ref[...] = dyb
        docat = _mm_nt(dyb, wo_ref[...].reshape(D, D)).astype(BF16)
        for h in range(MEM_HEADS):
            cols = slice(h * hd, (h + 1) * hd)
            vcols = slice(D + h * hd, D + (h + 1) * hd)
            qh, kh, vh, doh = q_ref[:, cols], kv_ref[:, cols], kv_ref[:, vcols], docat[:, cols]
            p = _softmax_rows(_mm_nt(qh, kh) * scale)
            dp = _mm_nt(doh, vh)
            ds = (p * (dp - jnp.sum(dp * p, axis=1, keepdims=True)) * scale).astype(BF16)
            dqb_ref[:, cols] = _mm(ds, kh).astype(BF16)
            dkv_ref[:, cols] += _mm_tn(ds, qh)
            dkv_ref[:, vcols] += _mm_tn(p.astype(BF16), doh)
        dhq = _mm_nt(dqb_ref[...], wq_ref[...].reshape(D, D))
        r, xhat = _rms(x_ref[...])
        dgain_ref[...] += jnp.sum(dhq * xhat, axis=0, keepdims=True)
        dx_ref[...] = dy_ref[...] + _rms_bwd(dhq, gain_ref[...], r, xhat)

    row = lambda b, t: (b * per + t, 0)
    wspec = pl.BlockSpec(w_q.shape, lambda b, t: (0, 0, 0))
    one = pl.BlockSpec((1, D), lambda b, t: (0, 0))
    kvspec = pl.BlockSpec((None, M, 2 * D), lambda b, t: (b, 0, 0))
    return _call(
        body, name="cross_bwd", grid=(B, per), comm=comm,
        in_specs=[pl.BlockSpec((tm, D), row), pl.BlockSpec((tm, D), row), one, pl.BlockSpec((tm, D), row),
                  kvspec, wspec, wspec],
        out_specs=(pl.BlockSpec((tm, D), row), pl.BlockSpec((tm, D), row), pl.BlockSpec((tm, D), row), kvspec, one),
        out_shape=(jax.ShapeDtypeStruct((T, D), F32), jax.ShapeDtypeStruct((T, D), BF16),
                   jax.ShapeDtypeStruct((T, D), BF16), jax.ShapeDtypeStruct((B, M, 2 * D), F32),
                   jax.ShapeDtypeStruct((1, D), F32)),
        args=(dy, x, gain, q, kv, w_q, w_o))


def _ordered_sum(gp_ref):
    g = gp_ref[0].astype(F32)
    for s in range(1, N_DEV):
        g = g + gp_ref[s].astype(F32)
    return g


def _adam_write(g, w_ref, m_ref, v_ref, g_ref, d_ref, nm_ref, nv_ref):
    nm = ADAM_B1 * m_ref[...] + (1.0 - ADAM_B1) * g
    nv = ADAM_B2 * v_ref[...] + (1.0 - ADAM_B2) * (g * g)
    m_hat = nm / (1.0 - ADAM_B1 ** ADAM_STEP)
    v_hat = nv / (1.0 - ADAM_B2 ** ADAM_STEP)
    g_ref[...] = g
    nm_ref[...] = nm
    nv_ref[...] = nv
    d_ref[...] = -ADAM_LR * (m_hat / (jnp.sqrt(v_hat) + ADAM_EPS) + ADAM_WD * w_ref[...])


def _adamw(gparts, w, m, v, name):
    R, C = w.shape
    tr = _tile(R, 256)

    def body(gp_ref, w_ref, m_ref, v_ref, g_ref, d_ref, nm_ref, nv_ref):
        _adam_write(_ordered_sum(gp_ref), w_ref, m_ref, v_ref, g_ref, d_ref, nm_ref, nv_ref)

    spec = pl.BlockSpec((tr, C), lambda i: (i, 0))
    shp = jax.ShapeDtypeStruct((R, C), F32)
    return pl.pallas_call(
        body, name=name, grid=(R // tr,),
        in_specs=[pl.BlockSpec((N_DEV, tr, C), lambda i: (0, i, 0)), spec, spec, spec],
        out_specs=(spec, spec, spec, spec), out_shape=(shp, shp, shp, shp),
        compiler_params=_params("arbitrary"),
    )(gparts, w, m, v)


def _pack_rows(vectors, D):
    def body(*refs):
        out_ref = refs[-1]
        out_ref[...] = jnp.zeros_like(out_ref)
        for i, r in enumerate(refs[:-1]):
            out_ref[i:i + 1, :r.shape[1]] = r[...]

    vmem = pl.BlockSpec(memory_space=pltpu.VMEM)
    return pl.pallas_call(body, name="pack_small", in_specs=[vmem] * len(vectors), out_specs=vmem,
                          out_shape=jax.ShapeDtypeStruct((8, D), F32))(*vectors)


def _adamw_small(row_parts, mat_parts, vectors, matrix):
    n = len(vectors)

    def body(*refs):
        rp_ref, mp_ref = refs[0], refs[1]
        ins, outs = refs[2:2 + 3 * (n + 1)], refs[2 + 3 * (n + 1):]
        rows = _ordered_sum(rp_ref)
        for i in range(n):
            c = ins[3 * i].shape[1]
            _adam_write(rows[i:i + 1, :c], *ins[3 * i:3 * i + 3], *outs[4 * i:4 * i + 4])
        _adam_write(_ordered_sum(mp_ref), *ins[3 * n:3 * n + 3], *outs[4 * n:4 * n + 4])
        outs[-1][...] = rows[n:n + 1, :]

    flat = [a for wmv in vectors for a in wmv] + list(matrix)
    out_shape = [jax.ShapeDtypeStruct(wmv[0].shape, F32) for wmv in list(vectors) + [matrix] for _ in range(4)]
    out_shape.append(jax.ShapeDtypeStruct((1, row_parts.shape[2]), F32))
    vmem = pl.BlockSpec(memory_space=pltpu.VMEM)
    res = pl.pallas_call(body, name="adamw_small", in_specs=[vmem] * (2 + len(flat)),
                         out_specs=tuple([vmem] * len(out_shape)), out_shape=tuple(out_shape))(row_parts, mat_parts, *flat)
    return [res[4 * i:4 * i + 4] for i in range(n + 1)], res[-1]


def kernel(x, mem, ffn1_norm, ffn1_w_gate, ffn1_w_up, ffn1_w_down, mix_norm, w_in, w_pool, pool_scale, w_out, mem_q_norm, mem_kv_norm, mem_w_q, mem_w_kv, mem_w_o, ffn2_norm, ffn2_w_gate, ffn2_w_up, ffn2_w_down, final_norm, loss_target, m_ffn1_norm, m_ffn1_w_gate, m_ffn1_w_up, m_ffn1_w_down, m_mix_norm, m_w_in, m_w_pool, m_pool_scale, m_w_out, m_mem_q_norm, m_mem_kv_norm, m_mem_w_q, m_mem_w_kv, m_mem_w_o, m_ffn2_norm, m_ffn2_w_gate, m_ffn2_w_up, m_ffn2_w_down, m_final_norm, v_ffn1_norm, v_ffn1_w_gate, v_ffn1_w_up, v_ffn1_w_down, v_mix_norm, v_w_in, v_w_pool, v_pool_scale, v_w_out, v_mem_q_norm, v_mem_kv_norm, v_mem_w_q, v_mem_w_kv, v_mem_w_o, v_ffn2_norm, v_ffn2_w_gate, v_ffn2_w_up, v_ffn2_w_down, v_final_norm):
    B, S, D = x.shape
    T = B * S
    x0 = x.reshape(T, D)
    target = loss_target.reshape(T, D)
    final_gain = final_norm.reshape(1, D)

    big = dict(
        g1=ffn1_w_gate[0].T, u1=ffn1_w_up[0].T, d1=ffn1_w_down[0],
        g2=ffn2_w_gate[0].T, u2=ffn2_w_up[0].T, d2=ffn2_w_down[0],
        w_in=w_in[0].T, w_out=w_out[0], w_q=mem_w_q[0], w_kv=mem_w_kv[0].T, w_o=mem_w_o[0])
    names = list(big)
    shard = {k: big[k].astype(BF16) for k in names}
    wp = w_pool[0].astype(BF16)
    full, ffn_w = {}, {}
    stacked = ("g1", "u1", "d1", "g2", "u2", "d2", "w_in", "w_kv")

    def gathered(keys, arrs):
        full.update(zip(keys, arrs))
        ffn_w.update({k: full[k].reshape(-1, D) for k in keys if k in stacked})

    first, mid = ("g1", "u1", "d1"), ("w_in", "w_out", "w_q", "w_kv", "w_o")
    gathered(first, _gather_two_level([shard[k] for k in first], "gather_ffn1"))
    (x1, hn1, a1, s1, t1), got = _ffn_fwd(x0, ffn1_norm, ffn_w["g1"], ffn_w["u1"], ffn_w["d1"], "ffn1_fwd",
                                          comm=([shard[k] for k in mid], True))
    gathered(mid, got)
    hn2, qkv, u = _mix_in_fwd(x1, mix_norm, ffn_w["w_in"])
    qkv3 = qkv.reshape(B, S, -1)
    (o_sb,), got = _sb_fwd(qkv3, B, S, comm=([shard["g2"], shard["u2"]], True))
    gathered(("g2", "u2"), got)
    pooled = _pool_fwd(u.reshape(B, S, -1), B, S).reshape(T, -1)
    x2, mixed = _mix_out_fwd(x1, o_sb.reshape(T, -1), pooled, wp, pool_scale, full["w_out"])
    memn, kv = _mem_kv_fwd(mem, mem_kv_norm, ffn_w["w_kv"])
    (x3, hq, q, ocat), got = _cross_fwd(x2, mem_q_norm, kv, full["w_q"], full["w_o"], B, S,
                                        comm=([shard["d2"]], True))
    gathered(("d2",), got)
    (dx4, hn4, a2, s2, t2, d_final, loss_part), _ = _ffn_fwd(x3, ffn2_norm, ffn_w["g2"], ffn_w["u2"], ffn_w["d2"],
                                                            "ffn2_fwd", head=(final_gain, target))

    slab = lambda k: grads[k].reshape((N_DEV, -1) + grads[k].shape[-1:])
    got = {}
    dx3, dg2, du2, dyh2, d_ffn2 = _ffn_bwd(dx4, x3, ffn2_norm, s2, t2, ffn_w["g2"], ffn_w["u2"],
                                          ffn_w["d2"], "ffn2_bwd")
    ffn_slab = ffn1_w_gate.shape[2]
    grads = dict(g2=_wgrad(hn4, dg2, "dw_gate2", col_slab=ffn_slab), u2=_wgrad(hn4, du2, "dw_up2", col_slab=ffn_slab),
                 d2=_wgrad(a2, dyh2, "dw_down2"))
    (dx2, dx3b, dqb, dkv, d_q), (got["g2"],) = _cross_bwd(dx3, x2, mem_q_norm, q, kv, full["w_q"], full["w_o"], B, S,
                                                         comm=([slab("g2")], False))
    grads["w_o"] = _wgrad(ocat, dx3b, "dw_o")
    grads["w_q"] = _wgrad(hq, dqb, "dw_q")
    dkvb, d_kv = _mem_kv_bwd(dkv, mem, ffn_w["w_kv"])
    grads["w_kv"] = _wgrad(memn, dkvb, "dw_kv", col_slab=mem_w_kv.shape[2])
    dx2b, do_sb, dpooled, d_wpool, d_ps = _mix_out_bwd(dx2, pooled, wp, pool_scale, full["w_out"])
    grads["w_out"] = _wgrad(mixed, dx2b, "dw_out")
    du = _pool_bwd(dpooled.reshape(B, S, -1), B, S).reshape(T, -1)
    early = ("u2", "d2", "w_o", "w_q", "w_kv", "w_out")
    (dq, dk, dv), res = _sb_bwd(qkv3, do_sb.reshape(B, S, -1), B, S, comm=([slab(k) for k in early], False))
    got.update(zip(early, res))
    dx1, dproj, d_mix = _mix_in_bwd(dx2, dq.reshape(T, -1), dk.reshape(T, -1), dv.reshape(T, -1), du,
                                    x1, mix_norm, ffn_w["w_in"])
    grads["w_in"] = _wgrad(hn2, dproj, "dw_in", col_slab=w_in.shape[2])
    dx0, dg1, du1, dyh1, d_ffn1 = _ffn_bwd(dx1, x0, ffn1_norm, s1, t1, ffn_w["g1"], ffn_w["u1"],
                                          ffn_w["d1"], "ffn1_bwd")

    small = [("ffn1_norm", d_ffn1, ffn1_norm, m_ffn1_norm, v_ffn1_norm),
             ("mix_norm", d_mix, mix_norm, m_mix_norm, v_mix_norm),
             ("pool_scale", d_ps, pool_scale, m_pool_scale, v_pool_scale),
             ("mem_q_norm", d_q, mem_q_norm, m_mem_q_norm, v_mem_q_norm),
             ("mem_kv_norm", d_kv, mem_kv_norm, m_mem_kv_norm, v_mem_kv_norm),
             ("ffn2_norm", d_ffn2, ffn2_norm, m_ffn2_norm, v_ffn2_norm),
             ("final_norm", d_final, final_gain, m_final_norm.reshape(1, D), v_final_norm.reshape(1, D))]
    row_pack = _pack_rows([t[1] for t in small] + [loss_part], D)
    as_rows = lambda t: t.reshape(-1, LANES)
    grads["g1"], (row_parts, pool_parts, got["w_in"]) = _wgrad(
        hn1, dg1, "dw_gate1", col_slab=ffn_slab,
        comm=([row_pack, as_rows(d_wpool), slab("w_in")], [True, True, False]))
    grads["u1"], (got["g1"],) = _wgrad(hn1, du1, "dw_up1", col_slab=ffn_slab, comm=([slab("g1")], False))
    grads["d1"], (got["u1"],) = _wgrad(a1, dyh1, "dw_down1", comm=([slab("u1")], False))
    got["d1"] = _exchange([slab("d1")], False, "scatter_last")[0]

    state = dict(
        g1=(ffn1_w_gate, m_ffn1_w_gate, v_ffn1_w_gate), u1=(ffn1_w_up, m_ffn1_w_up, v_ffn1_w_up),
        d1=(ffn1_w_down, m_ffn1_w_down, v_ffn1_w_down), g2=(ffn2_w_gate, m_ffn2_w_gate, v_ffn2_w_gate),
        u2=(ffn2_w_up, m_ffn2_w_up, v_ffn2_w_up), d2=(ffn2_w_down, m_ffn2_w_down, v_ffn2_w_down),
        w_in=(w_in, m_w_in, v_w_in), w_out=(w_out, m_w_out, v_w_out), w_q=(mem_w_q, m_mem_w_q, v_mem_w_q),
        w_kv=(mem_w_kv, m_mem_w_kv, v_mem_w_kv), w_o=(mem_w_o, m_mem_w_o, v_mem_w_o))
    big_out = {}
    for k in names:
        w_, m_, v_ = (t[0] for t in state[k])
        big_out[k] = [t[None] for t in _adamw(got[k], w_, m_, v_, "adamw_" + k)]

    small_res, loss_row = _adamw_small(row_parts, pool_parts, [t[2:] for t in small],
                                       [as_rows(t) for t in (w_pool, m_w_pool, v_w_pool)])
    small_out = {t[0]: small_res[i] for i, t in enumerate(small)}
    small_out["final_norm"] = [t.reshape(D) for t in small_out["final_norm"]]
    small_out["w_pool"] = [t.reshape(w_pool.shape) for t in small_res[-1]]
    loss = loss_row[0, 0]

    order = [("ffn1_norm", None), ("ffn1_w_gate", "g1"), ("ffn1_w_up", "u1"), ("ffn1_w_down", "d1"),
             ("mix_norm", None), ("w_in", "w_in"), ("w_pool", None), ("pool_scale", None), ("w_out", "w_out"),
             ("mem_q_norm", None), ("mem_kv_norm", None), ("mem_w_q", "w_q"), ("mem_w_kv", "w_kv"),
             ("mem_w_o", "w_o"), ("ffn2_norm", None), ("ffn2_w_gate", "g2"), ("ffn2_w_up", "u2"),
             ("ffn2_w_down", "d2"), ("final_norm", None)]
    res = [loss, dx0.reshape(B, S, D)]
    for which in range(4):
        for name, key in order:
            res.append(big_out[key][which] if key else small_out[name][which])
    return tuple(res)
```

```python
import functools
import math

import jax
import jax.numpy as jnp
from jax import lax
from jax.experimental import pallas as pl
from jax.experimental.pallas import tpu as pltpu

F32 = jnp.float32
BF16 = jnp.bfloat16

N_DEV = 8
EPS = 1e-6
SB_HEAD_DIM = 64
LANES = 128
POOL_WINDOWS = (2, 4, 8, 16)
POOL_GROUP_DIM = 128
MEM_HEADS = 4
FFN_RESIDUAL_WEIGHT = 0.5
ADAM_LR = 0.001
ADAM_B1 = 0.9
ADAM_B2 = 0.999
ADAM_EPS = 1e-08
ADAM_WD = 0.01
ADAM_STEP = 10
VMEM_LIMIT = 56 * 1024 * 1024

MESH_ID = pl.DeviceIdType.MESH


def _params(*sem):
    return pltpu.CompilerParams(dimension_semantics=sem, vmem_limit_bytes=VMEM_LIMIT)


def _tile(n, pref):
    if n <= pref:
        return n
    t = pref - pref % 8
    while n % t:
        t -= 8
    return t


def _mm(a, b):
    return jnp.dot(a, b, preferred_element_type=F32)


def _mm_nt(a, b):
    return lax.dot_general(a, b, (((1,), (1,)), ((), ())), preferred_element_type=F32)


def _mm_tn(a, b):
    return lax.dot_general(a, b, (((0,), (0,)), ((), ())), preferred_element_type=F32)


def _rms(xv):
    r = lax.rsqrt(jnp.mean(xv * xv, axis=-1, keepdims=True) + EPS)
    return r, xv * r


def _rms_bwd(dhn, gain, r, xhat):
    dxh = dhn * gain
    return r * (dxh - xhat * jnp.mean(dxh * xhat, axis=-1, keepdims=True))


def _sigmoid(z):
    return 0.5 * jnp.tanh(0.5 * z) + 0.5


def _flags(arrs, gather):
    return [gather] * len(arrs) if isinstance(gather, bool) else list(gather)


def _comm_shapes(arrs, gather):
    return tuple(jax.ShapeDtypeStruct(((N_DEV,) + tuple(a.shape)) if f else tuple(a.shape), a.dtype)
                 for a, f in zip(arrs, _flags(arrs, gather)))


def _comm_start(ins, outs, sems, gather):
    send_sems, recv_sems, local_sems = sems
    gather = _flags(ins, gather)
    x, y, c = lax.axis_index("x"), lax.axis_index("y"), lax.axis_index("c")
    me = 4 * x + 2 * y + c
    for i in range(len(ins)):
        src = ins[i] if gather[i] else ins[i].at[me]
        pltpu.make_async_copy(src, outs[i].at[me], local_sems.at[i]).start()
    for k in range(1, N_DEV):
        px = 1 - x if k & 4 else x
        py = 1 - y if k & 2 else y
        pc = 1 - c if k & 1 else c
        peer = 4 * px + 2 * py + pc
        for i in range(len(ins)):
            src = ins[i] if gather[i] else ins[i].at[peer]
            pltpu.make_async_remote_copy(
                src_ref=src, dst_ref=outs[i].at[me],
                send_sem=send_sems.at[i], recv_sem=recv_sems.at[i],
                device_id=(px, py, pc), device_id_type=MESH_ID).start()


def _comm_wait(ins, outs, sems, gather):
    send_sems, recv_sems, local_sems = sems
    gather = _flags(ins, gather)
    x, y, c = lax.axis_index("x"), lax.axis_index("y"), lax.axis_index("c")
    me = 4 * x + 2 * y + c
    for i in range(len(ins)):
        seven = outs[i].at[pl.ds(0, N_DEV - 1)]
        done = pltpu.make_async_remote_copy(
            src_ref=seven, dst_ref=seven,
            send_sem=send_sems.at[i], recv_sem=recv_sems.at[i],
            device_id=(x, y, c), device_id_type=MESH_ID)
        done.wait_send()
        done.wait_recv()
        src = ins[i] if gather[i] else ins[i].at[me]
        pltpu.make_async_copy(src, outs[i].at[me], local_sems.at[i]).wait()


def _comm_sems(n):
    return [pltpu.SemaphoreType.DMA((n,)) for _ in range(3)]


def _exchange(arrs, gather, name):
    n = len(arrs)

    def body(*refs):
        ins, outs, sems = refs[:n], refs[n:2 * n], refs[2 * n:]
        _comm_start(ins, outs, sems, gather)
        _comm_wait(ins, outs, sems, gather)

    any_spec = pl.BlockSpec(memory_space=pl.ANY)
    outs = pl.pallas_call(
        body, name=name, out_shape=_comm_shapes(arrs, gather),
        in_specs=[any_spec] * n, out_specs=tuple([any_spec] * n), scratch_shapes=_comm_sems(n),
    )(*arrs)
    return list(outs)


def _gather_two_level(arrs, name):
    n = len(arrs)

    def body(*refs):
        ins, outs = refs[:n], refs[n:2 * n]
        send_sems, recv_sems, local_sems = refs[2 * n:]
        x, y, c = lax.axis_index("x"), lax.axis_index("y"), lax.axis_index("c")
        me, sibling = (x, y, c), (x, y, 1 - c)
        chips = [(1 - x, y), (x, 1 - y), (1 - x, 1 - y)]

        def copy(i, k, block, to, own=False):
            slab = outs[i].at[4 * block[0] + 2 * block[1] + block[2]]
            return pltpu.make_async_remote_copy(
                src_ref=ins[i] if own else slab, dst_ref=slab,
                send_sem=send_sems.at[i, k], recv_sem=recv_sems.at[i, k],
                device_id=to, device_id_type=MESH_ID)

        mine = [pltpu.make_async_copy(ins[i], outs[i].at[4 * x + 2 * y + c], local_sems.at[i]) for i in range(n)]
        first = [copy(i, 0, me, sibling, own=True) for i in range(n)]
        first += [copy(i, 1 + j, me, (*chip, c), own=True) for j, chip in enumerate(chips) for i in range(n)]
        for cp in mine + first:
            cp.start()
        passed = []
        for j, chip in enumerate(chips):
            for i in range(n):
                copy(i, 1 + j, (*chip, c), me).wait_recv()
                passed.append(copy(i, 4 + j, (*chip, c), sibling))
                passed[-1].start()
        for i in range(n):
            copy(i, 0, sibling, me).wait_recv()
            for j, chip in enumerate(chips):
                copy(i, 4 + j, (*chip, 1 - c), me).wait_recv()
        for cp in first + passed:
            cp.wait_send()
        for cp in mine:
            cp.wait()

    any_spec = pl.BlockSpec(memory_space=pl.ANY)
    outs = pl.pallas_call(
        body, name=name, out_shape=_comm_shapes(arrs, True),
        in_specs=[any_spec] * n, out_specs=tuple([any_spec] * n),
        scratch_shapes=[pltpu.SemaphoreType.DMA((n, N_DEV - 1)), pltpu.SemaphoreType.DMA((n, N_DEV - 1)),
                        pltpu.SemaphoreType.DMA((n,))],
    )(*arrs)
    return list(outs)


def _call(body, *, name, grid, in_specs, out_specs, out_shape, args, scratch=(), comm=None):
    sem = ("arbitrary",) * len(grid)
    if comm is None:
        res = pl.pallas_call(body, name=name, grid=grid, in_specs=list(in_specs), out_specs=tuple(out_specs),
                             out_shape=tuple(out_shape), scratch_shapes=list(scratch),
                             compiler_params=_params(*sem))(*args)
        return tuple(res), []
    arrs, gather = comm
    n, n_in, n_out, n_sc = len(arrs), len(args), len(out_shape), len(scratch)

    def wrapped(*refs):
        ins, cin = refs[:n_in], refs[n_in:n_in + n]
        outs, cout = refs[n_in + n:n_in + n + n_out], refs[n_in + n + n_out:n_in + 2 * n + n_out]
        sc, sems = refs[n_in + 2 * n + n_out:n_in + 2 * n + n_out + n_sc], refs[n_in + 2 * n + n_out + n_sc:]
        ids = [pl.program_id(a) for a in range(len(grid))]
        first = functools.reduce(jnp.logical_and, [i == 0 for i in ids])
        last = functools.reduce(jnp.logical_and, [i == g - 1 for i, g in zip(ids, grid)])

        @pl.when(first)
        def _():
            _comm_start(cin, cout, sems, gather)

        body(*ins, *outs, *sc)

        @pl.when(last)
        def _():
            _comm_wait(cin, cout, sems, gather)

    any_spec = pl.BlockSpec(memory_space=pl.ANY)
    res = pl.pallas_call(
        wrapped, name=name, grid=grid, in_specs=list(in_specs) + [any_spec] * n,
        out_specs=tuple(out_specs) + (any_spec,) * n, out_shape=tuple(out_shape) + _comm_shapes(arrs, gather),
        scratch_shapes=list(scratch) + _comm_sems(n), compiler_params=_params(*sem))(*args, *arrs)
    return tuple(res[:n_out]), list(res[n_out:])


FFN_BWD_ROWS = 256


def _load_resident(pairs, sem):
    copies = [pltpu.make_async_copy(src, dst, sem.at[k]) for k, (src, dst) in enumerate(pairs)]
    for cp in copies:
        cp.start()
    for cp in copies:
        cp.wait()


def _ffn_fwd(x, gain, wgt, wut, wd, name, comm=None, head=None):
    T, D = x.shape
    F = wd.shape[0]
    tm, tf = _tile(T, 512), _tile(F, 256)

    def body(*refs):
        if head:
            (x_ref, gain_ref, wg_hbm, wu_hbm, wd_hbm, fgain_ref, tgt_ref,
             out_ref, hn_ref, a_ref, s_ref, t_ref, dfgain_ref, loss_ref, wg_s, wu_s, wd_s, sem) = refs
        else:
            (x_ref, gain_ref, wg_hbm, wu_hbm, wd_hbm,
             out_ref, hn_ref, a_ref, s_ref, t_ref, wg_s, wu_s, wd_s, sem) = refs

        @pl.when(pl.program_id(0) == 0)
        def _():
            _load_resident([(wg_hbm, wg_s), (wu_hbm, wu_s), (wd_hbm, wd_s)], sem)
            if head:
                dfgain_ref[...] = jnp.zeros_like(dfgain_ref)
                loss_ref[...] = jnp.zeros_like(loss_ref)

        _, xhat = _rms(x_ref[...])
        hn = (xhat * gain_ref[...]).astype(BF16)
        hn_ref[...] = hn
        for f0 in range(0, F, tf):
            cols = slice(f0, f0 + tf)
            g = _mm_nt(hn, wg_s[cols, :])
            u = _mm_nt(hn, wu_s[cols, :])
            sig = _sigmoid(g)
            s = g * sig
            a_ref[:, cols] = (s * u).astype(BF16)
            s_ref[:, cols] = s.astype(BF16)
            t_ref[:, cols] = (u * (sig + s * (1.0 - sig))).astype(BF16)
        y = x_ref[...] + FFN_RESIDUAL_WEIGHT * _mm(a_ref[...], wd_s[...])
        if head:
            r, yhat = _rms(y)
            err = yhat * fgain_ref[...] - tgt_ref[...]
            loss_ref[...] += 0.5 * jnp.sum(jnp.mean(err * err, axis=-1, keepdims=True), axis=0, keepdims=True)
            dy = err * (1.0 / D)
            dfgain_ref[...] += jnp.sum(dy * yhat, axis=0, keepdims=True)
            out_ref[...] = _rms_bwd(dy, fgain_ref[...], r, yhat)
        else:
            out_ref[...] = y

    row = lambda i: (i, 0)
    one = pl.BlockSpec((1, D), lambda i: (0, 0))
    hbm = pl.BlockSpec(memory_space=pl.ANY)
    in_specs = [pl.BlockSpec((tm, D), row), one, hbm, hbm, hbm]
    out_specs = [pl.BlockSpec((tm, D), row), pl.BlockSpec((tm, D), row)] + [pl.BlockSpec((tm, F), row) for _ in range(3)]
    out_shape = [jax.ShapeDtypeStruct((T, D), F32), jax.ShapeDtypeStruct((T, D), BF16)] \
        + [jax.ShapeDtypeStruct((T, F), BF16) for _ in range(3)]
    args = (x, gain, wgt, wut, wd)
    if head:
        in_specs += [one, pl.BlockSpec((tm, D), row)]
        out_specs += [one, one]
        out_shape += [jax.ShapeDtypeStruct((1, D), F32), jax.ShapeDtypeStruct((1, D), F32)]
        args += tuple(head)
    return _call(
        body, name=name, grid=(T // tm,), comm=comm, in_specs=in_specs, out_specs=out_specs, out_shape=out_shape,
        scratch=[pltpu.VMEM((F, D), BF16) for _ in range(3)] + [pltpu.SemaphoreType.DMA((3,))], args=args)


def _ffn_bwd(dy, x, gain, s, t, wgt, wut, wd, name):
    T, D = x.shape
    F = wd.shape[0]
    tr, tf = _tile(T, FFN_BWD_ROWS), _tile(F, 256)
    rows = lambda i: (i, 0)
    one = pl.BlockSpec((1, D), lambda i: (0, 0))
    any_spec = pl.BlockSpec(memory_space=pl.ANY)

    def body(dy_ref, x_ref, gain_ref, s_ref, t_ref, wg_hbm, wu_hbm, wd_hbm,
             dx_ref, dg_ref, du_ref, dyh_ref, dgain_ref, wg_s, wu_s, wd_s, sem):
        @pl.when(pl.program_id(0) == 0)
        def _():
            _load_resident([(wg_hbm, wg_s), (wu_hbm, wu_s), (wd_hbm, wd_s)], sem)
            dgain_ref[...] = jnp.zeros_like(dgain_ref)

        dyh = (FFN_RESIDUAL_WEIGHT * dy_ref[...]).astype(BF16)
        dyh_ref[...] = dyh
        for f0 in range(0, F, tf):
            cols = slice(f0, f0 + tf)
            da = _mm_nt(dyh, wd_s[cols, :])
            dg_ref[:, cols] = (da * t_ref[:, cols].astype(F32)).astype(BF16)
            du_ref[:, cols] = (da * s_ref[:, cols].astype(F32)).astype(BF16)
        dhn = _mm(dg_ref[...], wg_s[...]) + _mm(du_ref[...], wu_s[...])
        r, xhat = _rms(x_ref[...])
        dgain_ref[...] += jnp.sum(dhn * xhat, axis=0, keepdims=True)
        dx_ref[...] = dy_ref[...] + _rms_bwd(dhn, gain_ref[...], r, xhat)

    wide = jax.ShapeDtypeStruct((T, F), BF16)
    return pl.pallas_call(
        body, name=name, grid=(T // tr,),
        in_specs=[pl.BlockSpec((tr, D), rows), pl.BlockSpec((tr, D), rows), one, pl.BlockSpec((tr, F), rows),
                  pl.BlockSpec((tr, F), rows), any_spec, any_spec, any_spec],
        out_specs=(pl.BlockSpec((tr, D), rows), pl.BlockSpec((tr, F), rows), pl.BlockSpec((tr, F), rows),
                   pl.BlockSpec((tr, D), rows), one),
        out_shape=(jax.ShapeDtypeStruct((T, D), F32), wide, wide, jax.ShapeDtypeStruct((T, D), BF16),
                   jax.ShapeDtypeStruct((1, D), F32)),
        scratch_shapes=[pltpu.VMEM((F, D), BF16) for _ in range(3)] + [pltpu.SemaphoreType.DMA((3,))],
        compiler_params=_params("arbitrary"),
    )(dy, x, gain, s, t, wgt, wut, wd)


def _wgrad(a, b, name, col_slab=None, comm=None):
    T, M = a.shape
    N = b.shape[1]
    tmm = M if M <= 1024 else _tile(M, 1408)
    tn = _tile(N, 1024)
    if col_slab and col_slab % LANES:
        tn = col_slab * LANES // math.gcd(col_slab, LANES)
    tk = _tile(T, 2048)
    nk = T // tk
    per = tn // col_slab if col_slab else 0

    def body(a_ref, b_ref, out_ref, acc):
        k = pl.program_id(2)

        @pl.when(k == 0)
        def _():
            acc[...] = jnp.zeros_like(acc)

        acc[...] += _mm_tn(a_ref[...], b_ref[...])

        @pl.when(k == nk - 1)
        def _():
            if col_slab:
                for s in range(per):
                    out_ref[s] = acc[:, s * col_slab:(s + 1) * col_slab].astype(BF16)
            else:
                out_ref[...] = acc[...].astype(BF16)

    if col_slab:
        out_spec = pl.BlockSpec((per, tmm, col_slab), lambda m, n, k: (n, m, 0))
        out_shape = jax.ShapeDtypeStruct((N // col_slab, M, col_slab), BF16)
    else:
        out_spec = pl.BlockSpec((tmm, tn), lambda m, n, k: (m, n))
        out_shape = jax.ShapeDtypeStruct((M, N), BF16)
    (out,), got = _call(
        body, name=name, grid=(M // tmm, N // tn, nk), comm=comm,
        in_specs=[pl.BlockSpec((tk, tmm), lambda m, n, k: (k, m)), pl.BlockSpec((tk, tn), lambda m, n, k: (k, n))],
        out_specs=(out_spec,), out_shape=(out_shape,),
        scratch=[pltpu.VMEM((tmm, tn), F32)], args=(a, b))
    return (out, got) if comm else out


def _mix_in_fwd(x, gain, w_int):
    T, D = x.shape
    C = w_int.shape[0]
    n_qkv = 3 * C // 4
    tm = _tile(T, 1024)

    def body(x_ref, gain_ref, w_ref, hn_ref, qkv_ref, u_ref):
        _, xhat = _rms(x_ref[...])
        hn = (xhat * gain_ref[...]).astype(BF16)
        hn_ref[...] = hn
        proj = _mm_nt(hn, w_ref[...])
        qkv_ref[...] = proj[:, :n_qkv].astype(BF16)
        u_ref[...] = proj[:, n_qkv:]

    row = lambda i: (i, 0)
    return pl.pallas_call(
        body, name="mix_in_fwd", grid=(T // tm,),
        in_specs=[pl.BlockSpec((tm, D), row), pl.BlockSpec((1, D), lambda i: (0, 0)),
                  pl.BlockSpec((C, D), lambda i: (0, 0))],
        out_specs=(pl.BlockSpec((tm, D), row), pl.BlockSpec((tm, n_qkv), row), pl.BlockSpec((tm, C - n_qkv), row)),
        out_shape=(jax.ShapeDtypeStruct((T, D), BF16), jax.ShapeDtypeStruct((T, n_qkv), BF16),
                   jax.ShapeDtypeStruct((T, C - n_qkv), F32)),
        compiler_params=_params("arbitrary"),
    )(x, gain, w_int)


def _mix_in_bwd(dres, dq, dk, dv, du, x, gain, w_int):
    T, D = x.shape
    C = w_int.shape[0]
    W = dq.shape[1]
    tm = _tile(T, 512)

    def body(dres_ref, dq_ref, dk_ref, dv_ref, du_ref, x_ref, gain_ref, w_ref, dx_ref, dproj_ref, dgain_ref):
        @pl.when(pl.program_id(0) == 0)
        def _():
            dgain_ref[...] = jnp.zeros_like(dgain_ref)

        for part, ref in enumerate((dq_ref, dk_ref, dv_ref, du_ref)):
            dproj_ref[:, part * W:(part + 1) * W] = ref[...]
        dhn = _mm(dproj_ref[...], w_ref[...])
        r, xhat = _rms(x_ref[...])
        dgain_ref[...] += jnp.sum(dhn * xhat, axis=0, keepdims=True)
        dx_ref[...] = dres_ref[...] + _rms_bwd(dhn, gain_ref[...], r, xhat)

    row = lambda i: (i, 0)
    one = pl.BlockSpec((1, D), lambda i: (0, 0))
    part = pl.BlockSpec((tm, W), row)
    return pl.pallas_call(
        body, name="mix_in_bwd", grid=(T // tm,),
        in_specs=[pl.BlockSpec((tm, D), row), part, part, part, part, pl.BlockSpec((tm, D), row), one,
                  pl.BlockSpec((C, D), lambda i: (0, 0))],
        out_specs=(pl.BlockSpec((tm, D), row), pl.BlockSpec((tm, C), row), one),
        out_shape=(jax.ShapeDtypeStruct((T, D), F32), jax.ShapeDtypeStruct((T, C), BF16),
                   jax.ShapeDtypeStruct((1, D), F32)),
        compiler_params=_params("arbitrary"),
    )(dres, dq, dk, dv, du, x, gain, w_int)


SB_PAIRS_PER_PROGRAM = 2
LOG2_E = 1.4426950408889634
EXP2_CLAMP = 126.0


def _neg_log2_sigmoid(nz2):
    w = jnp.minimum(nz2, EXP2_CLAMP)
    return w, jnp.log2(1.0 + jnp.exp2(w))


SB_DEAD_LOG2 = -160.0


def _sb_live(rests):
    worst = functools.reduce(jnp.maximum, rests)
    return (jnp.max(worst) > SB_DEAD_LOG2).astype(jnp.int32)


def _split(v):
    hi = v.astype(BF16)
    return hi, (v - hi.astype(F32)).astype(BF16)


def _tri_sum(v, tri):
    hi, lo = _split(v)
    return _mm(hi, tri) + _mm(lo, tri)


def _sb_fwd(qkv, B, S, comm=None):
    W = qkv.shape[2] // 3
    n_pair = W // LANES
    bq = _tile(S, 256)
    nq = S // bq
    hp = SB_PAIRS_PER_PROGRAM
    nscale2 = -(SB_HEAD_DIM ** -0.5) * LOG2_E

    def body(q_ref, k_ref, v_ref, o_ref):
        lane = lax.broadcasted_iota(jnp.int32, (1, LANES), 1)
        head0 = lane < SB_HEAD_DIM
        rr = lax.broadcasted_iota(jnp.int32, (bq, bq), 0)
        cc = lax.broadcasted_iota(jnp.int32, (bq, bq), 1)
        strict = cc < rr
        after = jnp.where(rr > cc, 1.0, 0.0).astype(BF16)

        def blocks(heads, ks, carries, diag):
            n = range(len(heads))
            keep = (lambda t: jnp.where(strict, t, 0.0)) if diag else (lambda t: t)
            z = [_mm_nt(qh, k_ref[ks, cols]) for qh, cols in heads]
            wl = [_neg_log2_sigmoid(z[h] * nscale2) for h in n]
            lr = [keep(wl[h][0] - wl[h][1]) for h in n]
            parts = [_split(lr[h]) for h in n]
            suf = [_mm(parts[h][0], after) + _mm(parts[h][1], after) for h in n]
            a = [keep(jnp.exp2(suf[h] + carries[h][1] - wl[h][1])).astype(BF16) for h in n]
            o = [carries[h][0] + _mm(a[h], v_ref[ks, heads[h][1]]) for h in n]
            return tuple((o[h], carries[h][1] + (suf[h][:, :1] + lr[h][:, :1])) for h in n)

        def q_tile(i, _):
            qs = pl.ds(pl.multiple_of(i * bq, bq), bq)
            heads = []
            for pr in range(hp):
                cols = slice(pr * LANES, (pr + 1) * LANES)
                qv = q_ref[qs, cols]
                heads += [(jnp.where(head0, qv, jnp.zeros_like(qv)), cols),
                          (jnp.where(head0, jnp.zeros_like(qv), qv), cols)]
            zero = (jnp.zeros((bq, LANES), F32), jnp.zeros((bq, 1), F32))
            init = blocks(heads, qs, (zero,) * len(heads), True)

            def left(st):
                t, _, cr = st
                ks = pl.ds(pl.multiple_of((i - 1 - t) * bq, bq), bq)
                cr = blocks(heads, ks, cr, False)
                return t + 1, _sb_live([c for _, c in cr]), cr

            _, _, res = lax.while_loop(lambda st: jnp.logical_and(st[0] < i, st[1] > 0), left,
                                       (jnp.int32(0), _sb_live([c for _, c in init]), init))
            for pr in range(hp):
                o_ref[qs, heads[2 * pr][1]] = jnp.where(head0, res[2 * pr][0], res[2 * pr + 1][0]).astype(BF16)
            return 0

        lax.fori_loop(0, nq, q_tile, 0)

    def col(off):
        return pl.BlockSpec((None, S, hp * LANES), lambda b, p: (b, 0, off + p))

    n_pair //= hp
    return _call(
        body, name="sb_fwd", grid=(B, n_pair), comm=comm,
        in_specs=[col(0), col(n_pair), col(2 * n_pair)],
        out_specs=(col(0),),
        out_shape=(jax.ShapeDtypeStruct((B, S, W), BF16),),
        args=(qkv, qkv, qkv))


def _sb_bwd(qkv, do, B, S, comm=None):
    W = qkv.shape[2] // 3
    n_pair = W // LANES
    bq = _tile(S, 256)
    nq = S // bq
    hp = SB_PAIRS_PER_PROGRAM
    scale = SB_HEAD_DIM ** -0.5
    nscale2 = -scale * LOG2_E

    def body(q_ref, k_ref, v_ref, do_ref, dq_ref, dk_ref, dv_ref, dk_s, dv_s, e_s, sg_s, a_s):
        lane = lax.broadcasted_iota(jnp.int32, (1, LANES), 1)
        head0 = lane < SB_HEAD_DIM
        rr = lax.broadcasted_iota(jnp.int32, (bq, bq), 0)
        cc = lax.broadcasted_iota(jnp.int32, (bq, bq), 1)
        strict = cc < rr
        after = jnp.where(rr > cc, 1.0, 0.0).astype(BF16)
        before = jnp.where(rr < cc, 1.0, 0.0).astype(BF16)
        dk_s[...] = jnp.zeros_like(dk_s)
        dv_s[...] = jnp.zeros_like(dv_s)

        def weights(heads, ks, kb, rests, diag):
            n = range(len(heads))
            keep = (lambda t: jnp.where(strict, t, 0.0)) if diag else (lambda t: t)
            z = [_mm_nt(heads[h][0], k_ref[ks, heads[h][2]]) for h in n]
            da = [_mm_nt(heads[h][1], v_ref[ks, heads[h][2]]) for h in n]
            wl = [_neg_log2_sigmoid(z[h] * nscale2) for h in n]
            lr = [keep(wl[h][0] - wl[h][1]) for h in n]
            parts = [_split(lr[h]) for h in n]
            suf = [_mm(parts[h][0], after) + _mm(parts[h][1], after) for h in n]
            a = [keep(jnp.exp2(suf[h] + rests[h] - wl[h][1])) for h in n]
            for h in n:
                a_s[h * nq + kb] = a[h].astype(BF16)
                e_s[h * nq + kb] = a[h] * da[h]
                sg_s[h * nq + kb] = jnp.exp2(-wl[h][1])
            return tuple(rests[h] + (suf[h][:, :1] + lr[h][:, :1]) for h in n)

        def grads(heads, ks, kb, carries, diag):
            n = range(len(heads))
            keep = (lambda t: jnp.where(strict, t, 0.0)) if diag else (lambda t: t)
            e = [e_s[h * nq + kb] for h in n]
            parts = [_split(e[h]) for h in n]
            pex = [_mm(parts[h][0], before) + _mm(parts[h][1], before) for h in n]
            dz = [keep(e[h] - sg_s[h * nq + kb] * (e[h] + pex[h] + carries[h][1])).astype(BF16) for h in n]
            dq = [carries[h][0] + _mm(dz[h], k_ref[ks, heads[h][2]]) for h in n]
            for h in n:
                dk_s[ks, heads[h][2]] += _mm_tn(dz[h], heads[h][0])
                dv_s[ks, heads[h][2]] += _mm_tn(a_s[h * nq + kb], heads[h][1])
            return tuple((dq[h], carries[h][1] + (pex[h][:, bq - 1:] + e[h][:, bq - 1:])) for h in n)

        def q_tile(i, _):
            qs = pl.ds(pl.multiple_of(i * bq, bq), bq)
            heads = []
            for pr in range(hp):
                cols = slice(pr * LANES, (pr + 1) * LANES)
                qv, dov = q_ref[qs, cols], do_ref[qs, cols]
                zq, zd = jnp.zeros_like(qv), jnp.zeros_like(dov)
                heads += [(jnp.where(head0, qv, zq), jnp.where(head0, dov, zd), cols),
                          (jnp.where(head0, zq, qv), jnp.where(head0, zd, dov), cols)]
            key_block = lambda kb: pl.ds(pl.multiple_of(kb * bq, bq), bq)
            rests = weights(heads, qs, i, (jnp.zeros((bq, 1), F32),) * len(heads), True)

            def left(st):
                t, _, rs = st
                rs = weights(heads, key_block(i - 1 - t), i - 1 - t, rs, False)
                return t + 1, _sb_live(rs), rs

            n_left, _, _ = lax.while_loop(lambda st: jnp.logical_and(st[0] < i, st[1] > 0), left,
                                          (jnp.int32(0), _sb_live(rests), rests))
            zero = (jnp.zeros((bq, LANES), F32), jnp.zeros((bq, 1), F32))
            res = lax.fori_loop(0, n_left, lambda t, cr: grads(heads, key_block(i - n_left + t), i - n_left + t, cr, False),
                                (zero,) * len(heads))
            res = grads(heads, qs, i, res, True)
            for pr in range(hp):
                dq = jnp.where(head0, res[2 * pr][0], res[2 * pr + 1][0])
                dq_ref[qs, heads[2 * pr][2]] = (dq * scale).astype(BF16)
            return 0

        lax.fori_loop(0, nq, q_tile, 0)
        dk_ref[...] = (dk_s[...] * scale).astype(BF16)
        dv_ref[...] = dv_s[...].astype(BF16)

    def col(off):
        return pl.BlockSpec((None, S, hp * LANES), lambda b, p: (b, 0, off + p))

    n_pair //= hp
    shp = jax.ShapeDtypeStruct((B, S, W), BF16)
    slots = 2 * hp * nq
    return _call(
        body, name="sb_bwd", grid=(B, n_pair), comm=comm,
        in_specs=[col(0), col(n_pair), col(2 * n_pair), col(0)],
        out_specs=(col(0), col(0), col(0)),
        out_shape=(shp, shp, shp),
        scratch=[pltpu.VMEM((S, hp * LANES), F32), pltpu.VMEM((S, hp * LANES), F32),
                 pltpu.VMEM((slots, bq, bq), F32), pltpu.VMEM((slots, bq, bq), F32),
                 pltpu.VMEM((slots, bq, bq), BF16)],
        args=(qkv, qkv, qkv, do))


def _pool_counts(S):
    t = lax.broadcasted_iota(jnp.int32, (S, 1), 0)
    return t, [jnp.minimum(t + 1, w).astype(F32) for w in POOL_WINDOWS]


def _pool_fwd(u, B, S):
    W = u.shape[2]

    def body(u_ref, out_ref):
        t, counts = _pool_counts(S)
        for gi, w in enumerate(POOL_WINDOWS):
            cols = slice(gi * POOL_GROUP_DIM, (gi + 1) * POOL_GROUP_DIM)
            ug = u_ref[:, cols]
            s, k = ug, 1
            while k < w:
                s = s + jnp.where(t >= k, pltpu.roll(s, k, axis=0), 0.0)
                k *= 2
            out_ref[:, cols] = (s / counts[gi] - ug).astype(BF16)

    spec = pl.BlockSpec((None, S, W), lambda b: (b, 0, 0))
    return pl.pallas_call(
        body, name="pool_fwd", grid=(B,), in_specs=[spec], out_specs=spec,
        out_shape=jax.ShapeDtypeStruct((B, S, W), BF16), compiler_params=_params("arbitrary"),
    )(u)


def _pool_bwd(dpooled, B, S):
    W = dpooled.shape[2]

    def body(d_ref, out_ref):
        t, counts = _pool_counts(S)
        for gi, w in enumerate(POOL_WINDOWS):
            cols = slice(gi * POOL_GROUP_DIM, (gi + 1) * POOL_GROUP_DIM)
            d = d_ref[:, cols]
            s, k = d / counts[gi], 1
            while k < w:
                s = s + jnp.where(t < S - k, pltpu.roll(s, S - k, axis=0), 0.0)
                k *= 2
            out_ref[:, cols] = (s - d).astype(BF16)

    spec = pl.BlockSpec((None, S, W), lambda b: (b, 0, 0))
    return pl.pallas_call(
        body, name="pool_bwd", grid=(B,), in_specs=[spec], out_specs=spec,
        out_shape=jax.ShapeDtypeStruct((B, S, W), BF16), compiler_params=_params("arbitrary"),
    )(dpooled)


def _mix_out_fwd(x, o_sb, pooled, w_pool, pool_scale, w_out):
    T, D = x.shape
    W = o_sb.shape[1]
    G = w_pool.shape[0]
    gd = POOL_GROUP_DIM
    tm = _tile(T, 1024)

    def body(x_ref, osb_ref, pooled_ref, wp_ref, ps_ref, wo_ref, out_ref, mixed_ref):
        mixed_ref[:, :W] = osb_ref[...]
        for gi in range(G):
            cols = slice(gi * gd, (gi + 1) * gd)
            pw = _mm(pooled_ref[:, cols], wp_ref[gi])
            mixed_ref[:, W + gi * gd:W + (gi + 1) * gd] = (pw * ps_ref[:, cols]).astype(BF16)
        out_ref[...] = x_ref[...] + _mm(mixed_ref[...], wo_ref[...].reshape(D, D))

    row = lambda i: (i, 0)
    return pl.pallas_call(
        body, name="mix_out_fwd", grid=(T // tm,),
        in_specs=[pl.BlockSpec((tm, D), row), pl.BlockSpec((tm, W), row), pl.BlockSpec((tm, W), row),
                  pl.BlockSpec((G, gd, gd), lambda i: (0, 0, 0)), pl.BlockSpec((1, W), lambda i: (0, 0)),
                  pl.BlockSpec(w_out.shape, lambda i: (0, 0, 0))],
        out_specs=(pl.BlockSpec((tm, D), row), pl.BlockSpec((tm, D), row)),
        out_shape=(jax.ShapeDtypeStruct((T, D), F32), jax.ShapeDtypeStruct((T, D), BF16)),
        compiler_params=_params("arbitrary"),
    )(x, o_sb, pooled, w_pool, pool_scale, w_out)


def _mix_out_bwd(dx, pooled, w_pool, pool_scale, w_out):
    T, D = dx.shape
    W = pooled.shape[1]
    G = w_pool.shape[0]
    gd = POOL_GROUP_DIM
    tm = _tile(T, 1024)

    def body(dx_ref, pooled_ref, wp_ref, ps_ref, wo_ref, dxb_ref, dosb_ref, dpooled_ref, dwp_ref, dps_ref):
        i = pl.program_id(0)

        @pl.when(i == 0)
        def _():
            dwp_ref[...] = jnp.zeros_like(dwp_ref)
            dps_ref[...] = jnp.zeros_like(dps_ref)

        dxb = dx_ref[...].astype(BF16)
        dxb_ref[...] = dxb
        dmixed = _mm_nt(dxb, wo_ref[...].reshape(D, D))
        dosb_ref[...] = dmixed[:, :W].astype(BF16)
        for gi in range(G):
            cols = slice(gi * gd, (gi + 1) * gd)
            pg = pooled_ref[:, cols]
            dop = dmixed[:, W + gi * gd:W + (gi + 1) * gd]
            pw = _mm(pg, wp_ref[gi])
            dps_ref[:, cols] += jnp.sum(dop * pw, axis=0, keepdims=True)
            dpw = (dop * ps_ref[:, cols]).astype(BF16)
            dwp_ref[gi] += _mm_tn(pg, dpw)
            dpooled_ref[:, cols] = _mm_nt(dpw, wp_ref[gi])

    row = lambda i: (i, 0)
    return pl.pallas_call(
        body, name="mix_out_bwd", grid=(T // tm,),
        in_specs=[pl.BlockSpec((tm, D), row), pl.BlockSpec((tm, W), row),
                  pl.BlockSpec((G, gd, gd), lambda i: (0, 0, 0)), pl.BlockSpec((1, W), lambda i: (0, 0)),
                  pl.BlockSpec(w_out.shape, lambda i: (0, 0, 0))],
        out_specs=(pl.BlockSpec((tm, D), row), pl.BlockSpec((tm, W), row), pl.BlockSpec((tm, W), row),
                   pl.BlockSpec((G, gd, gd), lambda i: (0, 0, 0)), pl.BlockSpec((1, W), lambda i: (0, 0))),
        out_shape=(jax.ShapeDtypeStruct((T, D), BF16), jax.ShapeDtypeStruct((T, W), BF16),
                   jax.ShapeDtypeStruct((T, W), F32), jax.ShapeDtypeStruct((G, gd, gd), F32),
                   jax.ShapeDtypeStruct((1, W), F32)),
        compiler_params=_params("arbitrary"),
    )(dx, pooled, w_pool, pool_scale, w_out)


def _mem_kv_fwd(mem, gain, w_kvt):
    B, M, D = mem.shape
    C = w_kvt.shape[0]

    def body(mem_ref, gain_ref, w_ref, memn_ref, kv_ref):
        _, xhat = _rms(mem_ref[...])
        mn = (xhat * gain_ref[...]).astype(BF16)
        memn_ref[...] = mn
        kv_ref[...] = _mm_nt(mn, w_ref[...]).astype(BF16)

    return pl.pallas_call(
        body, name="mem_kv_fwd", grid=(B,),
        in_specs=[pl.BlockSpec((None, M, D), lambda b: (b, 0, 0)), pl.BlockSpec((1, D), lambda b: (0, 0)),
                  pl.BlockSpec((C, D), lambda b: (0, 0))],
        out_specs=(pl.BlockSpec((M, D), lambda b: (b, 0)), pl.BlockSpec((None, M, C), lambda b: (b, 0, 0))),
        out_shape=(jax.ShapeDtypeStruct((B * M, D), BF16), jax.ShapeDtypeStruct((B, M, C), BF16)),
        compiler_params=_params("arbitrary"),
    )(mem, gain, w_kvt)


def _mem_kv_bwd(dkv, mem, w_kvt):
    B, M, D = mem.shape
    C = w_kvt.shape[0]

    def body(dkv_ref, mem_ref, w_ref, dkvb_ref, dgain_ref):
        @pl.when(pl.program_id(0) == 0)
        def _():
            dgain_ref[...] = jnp.zeros_like(dgain_ref)

        dkvb = dkv_ref[...].astype(BF16)
        dkvb_ref[...] = dkvb
        dmn = _mm(dkvb, w_ref[...])
        _, xhat = _rms(mem_ref[...])
        dgain_ref[...] += jnp.sum(dmn * xhat, axis=0, keepdims=True)

    return pl.pallas_call(
        body, name="mem_kv_bwd", grid=(B,),
        in_specs=[pl.BlockSpec((None, M, C), lambda b: (b, 0, 0)), pl.BlockSpec((None, M, D), lambda b: (b, 0, 0)),
                  pl.BlockSpec((C, D), lambda b: (0, 0))],
        out_specs=(pl.BlockSpec((M, C), lambda b: (b, 0)), pl.BlockSpec((1, D), lambda b: (0, 0))),
        out_shape=(jax.ShapeDtypeStruct((B * M, C), BF16), jax.ShapeDtypeStruct((1, D), F32)),
        compiler_params=_params("arbitrary"),
    )(dkv, mem, w_kvt)


def _softmax_rows(s):
    p = jnp.exp(s - jnp.max(s, axis=1, keepdims=True))
    return p / jnp.sum(p, axis=1, keepdims=True)


def _cross_fwd(x, gain, kv, w_q, w_o, B, S, comm=None):
    T, D = x.shape
    M = kv.shape[1]
    hd = D // MEM_HEADS
    tm = _tile(S, 1024)
    per = S // tm
    scale = hd ** -0.5

    def body(x_ref, gain_ref, kv_ref, wq_ref, wo_ref, out_ref, hq_ref, q_ref, ocat_ref):
        _, xhat = _rms(x_ref[...])
        hq = (xhat * gain_ref[...]).astype(BF16)
        hq_ref[...] = hq
        q = _mm(hq, wq_ref[...].reshape(D, D)).astype(BF16)
        q_ref[...] = q
        for h in range(MEM_HEADS):
            cols = slice(h * hd, (h + 1) * hd)
            s = _mm_nt(q[:, cols], kv_ref[:, cols]) * scale
            p = _softmax_rows(s).astype(BF16)
            ocat_ref[:, cols] = _mm(p, kv_ref[:, D + h * hd:D + (h + 1) * hd]).astype(BF16)
        out_ref[...] = x_ref[...] + _mm(ocat_ref[...], wo_ref[...].reshape(D, D))

    row = lambda b, t: (b * per + t, 0)
    wspec = pl.BlockSpec(w_q.shape, lambda b, t: (0, 0, 0))
    return _call(
        body, name="cross_fwd", grid=(B, per), comm=comm,
        in_specs=[pl.BlockSpec((tm, D), row), pl.BlockSpec((1, D), lambda b, t: (0, 0)),
                  pl.BlockSpec((None, M, 2 * D), lambda b, t: (b, 0, 0)), wspec, wspec],
        out_specs=tuple(pl.BlockSpec((tm, D), row) for _ in range(4)),
        out_shape=(jax.ShapeDtypeStruct((T, D), F32),) + tuple(jax.ShapeDtypeStruct((T, D), BF16) for _ in range(3)),
        args=(x, gain, kv, w_q, w_o))


def _cross_bwd(dy, x, gain, q, kv, w_q, w_o, B, S, comm=None):
    T, D = x.shape
    M = kv.shape[1]
    hd = D // MEM_HEADS
    tm = _tile(S, 512)
    per = S // tm
    scale = hd ** -0.5

    def body(dy_ref, x_ref, gain_ref, q_ref, kv_ref, wq_ref, wo_ref,
             dx_ref, dyb_ref, dqb_ref, dkv_ref, dgain_ref):
        b_id, t_id = pl.program_id(0), pl.program_id(1)

        @pl.when((b_id == 0) & (t_id == 0))
        def _():
            dgain_ref[...] = jnp.zeros_like(dgain_ref)

        @pl.when(t_id == 0)
        def _():
            dkv_ref[...] = jnp.zeros_like(dkv_ref)

        dyb = dy_ref[...].astype(BF16)
        dyb_ref[...] = dyb
        docat = _mm_nt(dyb, wo_ref[...].reshape(D, D)).astype(BF16)
        for h in range(MEM_HEADS):
            cols = slice(h * hd, (h + 1) * hd)
            vcols = slice(D + h * hd, D + (h + 1) * hd)
            qh, kh, vh, doh = q_ref[:, cols], kv_ref[:, cols], kv_ref[:, vcols], docat[:, cols]
            p = _softmax_rows(_mm_nt(qh, kh) * scale)
            dp = _mm_nt(doh, vh)
            ds = (p * (dp - jnp.sum(dp * p, axis=1, keepdims=True)) * scale).astype(BF16)
            dqb_ref[:, cols] = _mm(ds, kh).astype(BF16)
            dkv_ref[:, cols] += _mm_tn(ds, qh)
            dkv_ref[:, vcols] += _mm_tn(p.astype(BF16), doh)
        dhq = _mm_nt(dqb_ref[...], wq_ref[...].reshape(D, D))
        r, xhat = _rms(x_ref[...])
        dgain_ref[...] += jnp.sum(dhq * xhat, axis=0, keepdims=True)
        dx_ref[...] = dy_ref[...] + _rms_bwd(dhq, gain_ref[...], r, xhat)

    row = lambda b, t: (b * per + t, 0)
    wspec = pl.BlockSpec(w_q.shape, lambda b, t: (0, 0, 0))
    one = pl.BlockSpec((1, D), lambda b, t: (0, 0))
    kvspec = pl.BlockSpec((None, M, 2 * D), lambda b, t: (b, 0, 0))
    return _call(
        body, name="cross_bwd", grid=(B, per), comm=comm,
        in_specs=[pl.BlockSpec((tm, D), row), pl.BlockSpec((tm, D), row), one, pl.BlockSpec((tm, D), row),
                  kvspec, wspec, wspec],
        out_specs=(pl.BlockSpec((tm, D), row), pl.BlockSpec((tm, D), row), pl.BlockSpec((tm, D), row), kvspec, one),
        out_shape=(jax.ShapeDtypeStruct((T, D), F32), jax.ShapeDtypeStruct((T, D), BF16),
                   jax.ShapeDtypeStruct((T, D), BF16), jax.ShapeDtypeStruct((B, M, 2 * D), F32),
                   jax.ShapeDtypeStruct((1, D), F32)),
        args=(dy, x, gain, q, kv, w_q, w_o))


def _ordered_sum(gp_ref):
    g = gp_ref[0].astype(F32)
    for s in range(1, N_DEV):
        g = g + gp_ref[s].astype(F32)
    return g


def _adam_write(g, w_ref, m_ref, v_ref, g_ref, d_ref, nm_ref, nv_ref):
    nm = ADAM_B1 * m_ref[...] + (1.0 - ADAM_B1) * g
    nv = ADAM_B2 * v_ref[...] + (1.0 - ADAM_B2) * (g * g)
    m_hat = nm / (1.0 - ADAM_B1 ** ADAM_STEP)
    v_hat = nv / (1.0 - ADAM_B2 ** ADAM_STEP)
    g_ref[...] = g
    nm_ref[...] = nm
    nv_ref[...] = nv
    d_ref[...] = -ADAM_LR * (m_hat / (jnp.sqrt(v_hat) + ADAM_EPS) + ADAM_WD * w_ref[...])


def _adamw(items, name):
    R = items[0][1].shape[0]
    tr = next(t for t in (_tile(R, 128), _tile(R, 256)) if t % 16 == 0)
    n = len(items)

    def body(*refs):
        ins, outs = refs[:4 * n], refs[4 * n:]
        for k in range(n):
            gp_ref, w_ref, m_ref, v_ref = ins[4 * k:4 * k + 4]
            _adam_write(_ordered_sum(gp_ref), w_ref, m_ref, v_ref, *outs[4 * k:4 * k + 4])

    in_specs, out_specs, out_shape = [], [], []
    for _, w, _, _ in items:
        C = w.shape[1]
        spec = pl.BlockSpec((tr, C), lambda i: (i, 0))
        in_specs += [pl.BlockSpec((N_DEV, tr, C), lambda i: (0, i, 0)), spec, spec, spec]
        out_specs += [spec] * 4
        out_shape += [jax.ShapeDtypeStruct(w.shape, F32)] * 4
    res = pl.pallas_call(
        body, name=name, grid=(R // tr,), in_specs=in_specs, out_specs=tuple(out_specs), out_shape=tuple(out_shape),
        compiler_params=_params("arbitrary"),
    )(*[a for item in items for a in item])
    return [res[4 * k:4 * k + 4] for k in range(n)]


def _pack_rows(vectors, D):
    def body(*refs):
        out_ref = refs[-1]
        out_ref[...] = jnp.zeros_like(out_ref)
        for i, r in enumerate(refs[:-1]):
            out_ref[i:i + 1, :r.shape[1]] = r[...]

    vmem = pl.BlockSpec(memory_space=pltpu.VMEM)
    return pl.pallas_call(body, name="pack_small", in_specs=[vmem] * len(vectors), out_specs=vmem,
                          out_shape=jax.ShapeDtypeStruct((8, D), F32))(*vectors)


def _adamw_small(row_parts, mat_parts, vectors, matrix):
    n = len(vectors)

    def body(*refs):
        rp_ref, mp_ref = refs[0], refs[1]
        ins, outs = refs[2:2 + 3 * (n + 1)], refs[2 + 3 * (n + 1):]
        rows = _ordered_sum(rp_ref)
        for i in range(n):
            c = ins[3 * i].shape[1]
            _adam_write(rows[i:i + 1, :c], *ins[3 * i:3 * i + 3], *outs[4 * i:4 * i + 4])
        _adam_write(_ordered_sum(mp_ref), *ins[3 * n:3 * n + 3], *outs[4 * n:4 * n + 4])
        outs[-1][...] = rows[n:n + 1, :]

    flat = [a for wmv in vectors for a in wmv] + list(matrix)
    out_shape = [jax.ShapeDtypeStruct(wmv[0].shape, F32) for wmv in list(vectors) + [matrix] for _ in range(4)]
    out_shape.append(jax.ShapeDtypeStruct((1, row_parts.shape[2]), F32))
    vmem = pl.BlockSpec(memory_space=pltpu.VMEM)
    res = pl.pallas_call(body, name="adamw_small", in_specs=[vmem] * (2 + len(flat)),
                         out_specs=tuple([vmem] * len(out_shape)), out_shape=tuple(out_shape))(row_parts, mat_parts, *flat)
    return [res[4 * i:4 * i + 4] for i in range(n + 1)], res[-1]


def kernel(x, mem, ffn1_norm, ffn1_w_gate, ffn1_w_up, ffn1_w_down, mix_norm, w_in, w_pool, pool_scale, w_out, mem_q_norm, mem_kv_norm, mem_w_q, mem_w_kv, mem_w_o, ffn2_norm, ffn2_w_gate, ffn2_w_up, ffn2_w_down, final_norm, loss_target, m_ffn1_norm, m_ffn1_w_gate, m_ffn1_w_up, m_ffn1_w_down, m_mix_norm, m_w_in, m_w_pool, m_pool_scale, m_w_out, m_mem_q_norm, m_mem_kv_norm, m_mem_w_q, m_mem_w_kv, m_mem_w_o, m_ffn2_norm, m_ffn2_w_gate, m_ffn2_w_up, m_ffn2_w_down, m_final_norm, v_ffn1_norm, v_ffn1_w_gate, v_ffn1_w_up, v_ffn1_w_down, v_mix_norm, v_w_in, v_w_pool, v_pool_scale, v_w_out, v_mem_q_norm, v_mem_kv_norm, v_mem_w_q, v_mem_w_kv, v_mem_w_o, v_ffn2_norm, v_ffn2_w_gate, v_ffn2_w_up, v_ffn2_w_down, v_final_norm):
    B, S, D = x.shape
    T = B * S
    x0 = x.reshape(T, D)
    target = loss_target.reshape(T, D)
    final_gain = final_norm.reshape(1, D)

    big = dict(
        g1=ffn1_w_gate[0].T, u1=ffn1_w_up[0].T, d1=ffn1_w_down[0],
        g2=ffn2_w_gate[0].T, u2=ffn2_w_up[0].T, d2=ffn2_w_down[0],
        w_in=w_in[0].T, w_out=w_out[0], w_q=mem_w_q[0], w_kv=mem_w_kv[0].T, w_o=mem_w_o[0])
    names = list(big)
    shard = {k: big[k].astype(BF16) for k in names}
    wp = w_pool[0].astype(BF16)
    full, ffn_w = {}, {}
    stacked = ("g1", "u1", "d1", "g2", "u2", "d2", "w_in", "w_kv")

    def gathered(keys, arrs):
        full.update(zip(keys, arrs))
        ffn_w.update({k: full[k].reshape(-1, D) for k in keys if k in stacked})

    first, mid = ("g1", "u1", "d1"), ("w_in", "w_out", "w_q", "w_kv", "w_o")
    gathered(first, _gather_two_level([shard[k] for k in first], "gather_ffn1"))
    (x1, hn1, a1, s1, t1), got = _ffn_fwd(x0, ffn1_norm, ffn_w["g1"], ffn_w["u1"], ffn_w["d1"], "ffn1_fwd",
                                          comm=([shard[k] for k in mid], True))
    gathered(mid, got)
    hn2, qkv, u = _mix_in_fwd(x1, mix_norm, ffn_w["w_in"])
    qkv3 = qkv.reshape(B, S, -1)
    (o_sb,), got = _sb_fwd(qkv3, B, S, comm=([shard["g2"], shard["u2"]], True))
    gathered(("g2", "u2"), got)
    pooled = _pool_fwd(u.reshape(B, S, -1), B, S).reshape(T, -1)
    x2, mixed = _mix_out_fwd(x1, o_sb.reshape(T, -1), pooled, wp, pool_scale, full["w_out"])
    memn, kv = _mem_kv_fwd(mem, mem_kv_norm, ffn_w["w_kv"])
    (x3, hq, q, ocat), got = _cross_fwd(x2, mem_q_norm, kv, full["w_q"], full["w_o"], B, S,
                                        comm=([shard["d2"]], True))
    gathered(("d2",), got)
    (dx4, hn4, a2, s2, t2, d_final, loss_part), _ = _ffn_fwd(x3, ffn2_norm, ffn_w["g2"], ffn_w["u2"], ffn_w["d2"],
                                                            "ffn2_fwd", head=(final_gain, target))

    slab = lambda k: grads[k].reshape((N_DEV, -1) + grads[k].shape[-1:])
    got = {}
    dx3, dg2, du2, dyh2, d_ffn2 = _ffn_bwd(dx4, x3, ffn2_norm, s2, t2, ffn_w["g2"], ffn_w["u2"],
                                          ffn_w["d2"], "ffn2_bwd")
    ffn_slab = ffn1_w_gate.shape[2]
    grads = dict(g2=_wgrad(hn4, dg2, "dw_gate2", col_slab=ffn_slab), u2=_wgrad(hn4, du2, "dw_up2", col_slab=ffn_slab),
                 d2=_wgrad(a2, dyh2, "dw_down2"))
    (dx2, dx3b, dqb, dkv, d_q), (got["g2"],) = _cross_bwd(dx3, x2, mem_q_norm, q, kv, full["w_q"], full["w_o"], B, S,
                                                         comm=([slab("g2")], False))
    grads["w_o"] = _wgrad(ocat, dx3b, "dw_o")
    grads["w_q"] = _wgrad(hq, dqb, "dw_q")
    dkvb, d_kv = _mem_kv_bwd(dkv, mem, ffn_w["w_kv"])
    grads["w_kv"] = _wgrad(memn, dkvb, "dw_kv", col_slab=mem_w_kv.shape[2])
    dx2b, do_sb, dpooled, d_wpool, d_ps = _mix_out_bwd(dx2, pooled, wp, pool_scale, full["w_out"])
    grads["w_out"] = _wgrad(mixed, dx2b, "dw_out")
    du = _pool_bwd(dpooled.reshape(B, S, -1), B, S).reshape(T, -1)
    early = ("u2", "d2", "w_o", "w_q", "w_kv", "w_out")
    (dq, dk, dv), res = _sb_bwd(qkv3, do_sb.reshape(B, S, -1), B, S, comm=([slab(k) for k in early], False))
    got.update(zip(early, res))
    dx1, dproj, d_mix = _mix_in_bwd(dx2, dq.reshape(T, -1), dk.reshape(T, -1), dv.reshape(T, -1), du,
                                    x1, mix_norm, ffn_w["w_in"])
    grads["w_in"] = _wgrad(hn2, dproj, "dw_in", col_slab=w_in.shape[2])
    dx0, dg1, du1, dyh1, d_ffn1 = _ffn_bwd(dx1, x0, ffn1_norm, s1, t1, ffn_w["g1"], ffn_w["u1"],
                                          ffn_w["d1"], "ffn1_bwd")

    small = [("ffn1_norm", d_ffn1, ffn1_norm, m_ffn1_norm, v_ffn1_norm),
             ("mix_norm", d_mix, mix_norm, m_mix_norm, v_mix_norm),
             ("pool_scale", d_ps, pool_scale, m_pool_scale, v_pool_scale),
             ("mem_q_norm", d_q, mem_q_norm, m_mem_q_norm, v_mem_q_norm),
             ("mem_kv_norm", d_kv, mem_kv_norm, m_mem_kv_norm, v_mem_kv_norm),
             ("ffn2_norm", d_ffn2, ffn2_norm, m_ffn2_norm, v_ffn2_norm),
             ("final_norm", d_final, final_gain, m_final_norm.reshape(1, D), v_final_norm.reshape(1, D))]
    row_pack = _pack_rows([t[1] for t in small] + [loss_part], D)
    as_rows = lambda t: t.reshape(-1, LANES)
    grads["g1"], (row_parts, pool_parts, got["w_in"]) = _wgrad(
        hn1, dg1, "dw_gate1", col_slab=ffn_slab,
        comm=([row_pack, as_rows(d_wpool), slab("w_in")], [True, True, False]))
    grads["u1"], (got["g1"],) = _wgrad(hn1, du1, "dw_up1", col_slab=ffn_slab, comm=([slab("g1")], False))
    grads["d1"], (got["u1"],) = _wgrad(a1, dyh1, "dw_down1", comm=([slab("u1")], False))
    got["d1"] = _exchange([slab("d1")], False, "scatter_last")[0]

    state = dict(
        g1=(ffn1_w_gate, m_ffn1_w_gate, v_ffn1_w_gate), u1=(ffn1_w_up, m_ffn1_w_up, v_ffn1_w_up),
        d1=(ffn1_w_down, m_ffn1_w_down, v_ffn1_w_down), g2=(ffn2_w_gate, m_ffn2_w_gate, v_ffn2_w_gate),
        u2=(ffn2_w_up, m_ffn2_w_up, v_ffn2_w_up), d2=(ffn2_w_down, m_ffn2_w_down, v_ffn2_w_down),
        w_in=(w_in, m_w_in, v_w_in), w_out=(w_out, m_w_out, v_w_out), w_q=(mem_w_q, m_mem_w_q, v_mem_w_q),
        w_kv=(mem_w_kv, m_mem_w_kv, v_mem_w_kv), w_o=(mem_w_o, m_mem_w_o, v_mem_w_o))
    big_out = {}
    groups = dict(gate_up=("g1", "u1", "g2", "u2"), in_kv=("w_in", "w_kv"), square=("w_out", "w_q", "w_o"),
                  down=("d1", "d2"))
    for label, keys in groups.items():
        items = [(got[k],) + tuple(t[0] for t in state[k]) for k in keys]
        for k, outs in zip(keys, _adamw(items, "adamw_" + label)):
            big_out[k] = [t[None] for t in outs]

    small_res, loss_row = _adamw_small(row_parts, pool_parts, [t[2:] for t in small],
                                       [as_rows(t) for t in (w_pool, m_w_pool, v_w_pool)])
    small_out = {t[0]: small_res[i] for i, t in enumerate(small)}
    small_out["final_norm"] = [t.reshape(D) for t in small_out["final_norm"]]
    small_out["w_pool"] = [t.reshape(w_pool.shape) for t in small_res[-1]]
    loss = loss_row[0, 0]

    order = [("ffn1_norm", None), ("ffn1_w_gate", "g1"), ("ffn1_w_up", "u1"), ("ffn1_w_down", "d1"),
             ("mix_norm", None), ("w_in", "w_in"), ("w_pool", None), ("pool_scale", None), ("w_out", "w_out"),
             ("mem_q_norm", None), ("mem_kv_norm", None), ("mem_w_q", "w_q"), ("mem_w_kv", "w_kv"),
             ("mem_w_o", "w_o"), ("ffn2_norm", None), ("ffn2_w_gate", "g2"), ("ffn2_w_up", "u2"),
             ("ffn2_w_down", "d2"), ("final_norm", None)]
    res = [loss, dx0.reshape(B, S, D)]
    for which in range(4):
        for name, key in order:
            res.append(big_out[key][which] if key else small_out[name][which])
    return tuple(res)
```

```python
import functools
import math

import jax
import jax.numpy as jnp
from jax import lax
from jax.experimental import pallas as pl
from jax.experimental.pallas import tpu as pltpu

F32 = jnp.float32
BF16 = jnp.bfloat16

N_DEV = 8
EPS = 1e-6
SB_HEAD_DIM = 64
LANES = 128
POOL_WINDOWS = (2, 4, 8, 16)
POOL_GROUP_DIM = 128
MEM_HEADS = 4
FFN_RESIDUAL_WEIGHT = 0.5
ADAM_LR = 0.001
ADAM_B1 = 0.9
ADAM_B2 = 0.999
ADAM_EPS = 1e-08
ADAM_WD = 0.01
ADAM_STEP = 10
VMEM_LIMIT = 56 * 1024 * 1024

MESH_ID = pl.DeviceIdType.MESH


def _params(*sem):
    return pltpu.CompilerParams(dimension_semantics=sem, vmem_limit_bytes=VMEM_LIMIT)


def _tile(n, pref):
    if n <= pref:
        return n
    t = pref - pref % 8
    while n % t:
        t -= 8
    return t


def _mm(a, b):
    return jnp.dot(a, b, preferred_element_type=F32)


def _mm_nt(a, b):
    return lax.dot_general(a, b, (((1,), (1,)), ((), ())), preferred_element_type=F32)


def _mm_tn(a, b):
    return lax.dot_general(a, b, (((0,), (0,)), ((), ())), preferred_element_type=F32)


def _rms(xv):
    r = lax.rsqrt(jnp.mean(xv * xv, axis=-1, keepdims=True) + EPS)
    return r, xv * r


def _rms_bwd(dhn, gain, r, xhat):
    dxh = dhn * gain
    return r * (dxh - xhat * jnp.mean(dxh * xhat, axis=-1, keepdims=True))


def _sigmoid(z):
    return 0.5 * jnp.tanh(0.5 * z) + 0.5


def _flags(arrs, gather):
    return [gather] * len(arrs) if isinstance(gather, bool) else list(gather)


def _comm_shapes(arrs, gather):
    return tuple(jax.ShapeDtypeStruct(((N_DEV,) + tuple(a.shape)) if f else tuple(a.shape), a.dtype)
                 for a, f in zip(arrs, _flags(arrs, gather)))


def _comm_start(ins, outs, sems, gather):
    send_sems, recv_sems, local_sems = sems
    gather = _flags(ins, gather)
    x, y, c = lax.axis_index("x"), lax.axis_index("y"), lax.axis_index("c")
    me = 4 * x + 2 * y + c
    for i in range(len(ins)):
        src = ins[i] if gather[i] else ins[i].at[me]
        pltpu.make_async_copy(src, outs[i].at[me], local_sems.at[i]).start()
    for k in range(1, N_DEV):
        px = 1 - x if k & 4 else x
        py = 1 - y if k & 2 else y
        pc = 1 - c if k & 1 else c
        peer = 4 * px + 2 * py + pc
        for i in range(len(ins)):
            src = ins[i] if gather[i] else ins[i].at[peer]
            pltpu.make_async_remote_copy(
                src_ref=src, dst_ref=outs[i].at[me],
                send_sem=send_sems.at[i], recv_sem=recv_sems.at[i],
                device_id=(px, py, pc), device_id_type=MESH_ID).start()


def _comm_wait(ins, outs, sems, gather):
    send_sems, recv_sems, local_sems = sems
    gather = _flags(ins, gather)
    x, y, c = lax.axis_index("x"), lax.axis_index("y"), lax.axis_index("c")
    me = 4 * x + 2 * y + c
    for i in range(len(ins)):
        seven = outs[i].at[pl.ds(0, N_DEV - 1)]
        done = pltpu.make_async_remote_copy(
            src_ref=seven, dst_ref=seven,
            send_sem=send_sems.at[i], recv_sem=recv_sems.at[i],
            device_id=(x, y, c), device_id_type=MESH_ID)
        done.wait_send()
        done.wait_recv()
        src = ins[i] if gather[i] else ins[i].at[me]
        pltpu.make_async_copy(src, outs[i].at[me], local_sems.at[i]).wait()


def _comm_sems(n):
    return [pltpu.SemaphoreType.DMA((n,)) for _ in range(3)]


def _exchange(arrs, gather, name):
    n = len(arrs)

    def body(*refs):
        ins, outs, sems = refs[:n], refs[n:2 * n], refs[2 * n:]
        _comm_start(ins, outs, sems, gather)
        _comm_wait(ins, outs, sems, gather)

    any_spec = pl.BlockSpec(memory_space=pl.ANY)
    outs = pl.pallas_call(
        body, name=name, out_shape=_comm_shapes(arrs, gather),
        in_specs=[any_spec] * n, out_specs=tuple([any_spec] * n), scratch_shapes=_comm_sems(n),
    )(*arrs)
    return list(outs)


def _gather_two_level(arrs, name):
    n = len(arrs)

    def body(*refs):
        ins, outs = refs[:n], refs[n:2 * n]
        send_sems, recv_sems, local_sems = refs[2 * n:]
        x, y, c = lax.axis_index("x"), lax.axis_index("y"), lax.axis_index("c")
        me, sibling = (x, y, c), (x, y, 1 - c)
        chips = [(1 - x, y), (x, 1 - y), (1 - x, 1 - y)]

        def copy(i, k, block, to, own=False):
            slab = outs[i].at[4 * block[0] + 2 * block[1] + block[2]]
            return pltpu.make_async_remote_copy(
                src_ref=ins[i] if own else slab, dst_ref=slab,
                send_sem=send_sems.at[i, k], recv_sem=recv_sems.at[i, k],
                device_id=to, device_id_type=MESH_ID)

        mine = [pltpu.make_async_copy(ins[i], outs[i].at[4 * x + 2 * y + c], local_sems.at[i]) for i in range(n)]
        first = [copy(i, 0, me, sibling, own=True) for i in range(n)]
        first += [copy(i, 1 + j, me, (*chip, c), own=True) for j, chip in enumerate(chips) for i in range(n)]
        for cp in mine + first:
            cp.start()
        passed = []
        for j, chip in enumerate(chips):
            for i in range(n):
                copy(i, 1 + j, (*chip, c), me).wait_recv()
                passed.append(copy(i, 4 + j, (*chip, c), sibling))
                passed[-1].start()
        for i in range(n):
            copy(i, 0, sibling, me).wait_recv()
            for j, chip in enumerate(chips):
                copy(i, 4 + j, (*chip, 1 - c), me).wait_recv()
        for cp in first + passed:
            cp.wait_send()
        for cp in mine:
            cp.wait()

    any_spec = pl.BlockSpec(memory_space=pl.ANY)
    outs = pl.pallas_call(
        body, name=name, out_shape=_comm_shapes(arrs, True),
        in_specs=[any_spec] * n, out_specs=tuple([any_spec] * n),
        scratch_shapes=[pltpu.SemaphoreType.DMA((n, N_DEV - 1)), pltpu.SemaphoreType.DMA((n, N_DEV - 1)),
                        pltpu.SemaphoreType.DMA((n,))],
    )(*arrs)
    return list(outs)


def _call(body, *, name, grid, in_specs, out_specs, out_shape, args, scratch=(), comm=None):
    sem = ("arbitrary",) * len(grid)
    if comm is None:
        res = pl.pallas_call(body, name=name, grid=grid, in_specs=list(in_specs), out_specs=tuple(out_specs),
                             out_shape=tuple(out_shape), scratch_shapes=list(scratch),
                             compiler_params=_params(*sem))(*args)
        return tuple(res), []
    arrs, gather = comm
    n, n_in, n_out, n_sc = len(arrs), len(args), len(out_shape), len(scratch)

    def wrapped(*refs):
        ins, cin = refs[:n_in], refs[n_in:n_in + n]
        outs, cout = refs[n_in + n:n_in + n + n_out], refs[n_in + n + n_out:n_in + 2 * n + n_out]
        sc, sems = refs[n_in + 2 * n + n_out:n_in + 2 * n + n_out + n_sc], refs[n_in + 2 * n + n_out + n_sc:]
        ids = [pl.program_id(a) for a in range(len(grid))]
        first = functools.reduce(jnp.logical_and, [i == 0 for i in ids])
        last = functools.reduce(jnp.logical_and, [i == g - 1 for i, g in zip(ids, grid)])

        @pl.when(first)
        def _():
            _comm_start(cin, cout, sems, gather)

        body(*ins, *outs, *sc)

        @pl.when(last)
        def _():
            _comm_wait(cin, cout, sems, gather)

    any_spec = pl.BlockSpec(memory_space=pl.ANY)
    res = pl.pallas_call(
        wrapped, name=name, grid=grid, in_specs=list(in_specs) + [any_spec] * n,
        out_specs=tuple(out_specs) + (any_spec,) * n, out_shape=tuple(out_shape) + _comm_shapes(arrs, gather),
        scratch_shapes=list(scratch) + _comm_sems(n), compiler_params=_params(*sem))(*args, *arrs)
    return tuple(res[:n_out]), list(res[n_out:])


FFN_BWD_ROWS = 256


def _load_resident(pairs, sem):
    copies = [pltpu.make_async_copy(src, dst, sem.at[k]) for k, (src, dst) in enumerate(pairs)]
    for cp in copies:
        cp.start()
    for cp in copies:
        cp.wait()


def _ffn_fwd(x, gain, wgt, wut, wd, name, comm=None, head=None):
    T, D = x.shape
    F = wd.shape[0]
    tm, tf = _tile(T, 512), _tile(F, 256)

    def body(*refs):
        if head:
            (x_ref, gain_ref, wg_hbm, wu_hbm, wd_hbm, fgain_ref, tgt_ref,
             out_ref, hn_ref, a_ref, s_ref, t_ref, dfgain_ref, loss_ref, wg_s, wu_s, wd_s, sem) = refs
        else:
            (x_ref, gain_ref, wg_hbm, wu_hbm, wd_hbm,
             out_ref, hn_ref, a_ref, s_ref, t_ref, wg_s, wu_s, wd_s, sem) = refs

        @pl.when(pl.program_id(0) == 0)
        def _():
            _load_resident([(wg_hbm, wg_s), (wu_hbm, wu_s), (wd_hbm, wd_s)], sem)
            if head:
                dfgain_ref[...] = jnp.zeros_like(dfgain_ref)
                loss_ref[...] = jnp.zeros_like(loss_ref)

        _, xhat = _rms(x_ref[...])
        hn = (xhat * gain_ref[...]).astype(BF16)
        hn_ref[...] = hn
        for f0 in range(0, F, tf):
            cols = slice(f0, f0 + tf)
            g = _mm_nt(hn, wg_s[cols, :])
            u = _mm_nt(hn, wu_s[cols, :])
            sig = _sigmoid(g)
            s = g * sig
            a_ref[:, cols] = (s * u).astype(BF16)
            s_ref[:, cols] = s.astype(BF16)
            t_ref[:, cols] = (u * (sig + s * (1.0 - sig))).astype(BF16)
        y = x_ref[...] + FFN_RESIDUAL_WEIGHT * _mm(a_ref[...], wd_s[...])
        if head:
            r, yhat = _rms(y)
            err = yhat * fgain_ref[...] - tgt_ref[...]
            loss_ref[...] += 0.5 * jnp.sum(jnp.mean(err * err, axis=-1, keepdims=True), axis=0, keepdims=True)
            dy = err * (1.0 / D)
            dfgain_ref[...] += jnp.sum(dy * yhat, axis=0, keepdims=True)
            out_ref[...] = _rms_bwd(dy, fgain_ref[...], r, yhat)
        else:
            out_ref[...] = y

    row = lambda i: (i, 0)
    one = pl.BlockSpec((1, D), lambda i: (0, 0))
    hbm = pl.BlockSpec(memory_space=pl.ANY)
    in_specs = [pl.BlockSpec((tm, D), row), one, hbm, hbm, hbm]
    out_specs = [pl.BlockSpec((tm, D), row), pl.BlockSpec((tm, D), row)] + [pl.BlockSpec((tm, F), row) for _ in range(3)]
    out_shape = [jax.ShapeDtypeStruct((T, D), F32), jax.ShapeDtypeStruct((T, D), BF16)] \
        + [jax.ShapeDtypeStruct((T, F), BF16) for _ in range(3)]
    args = (x, gain, wgt, wut, wd)
    if head:
        in_specs += [one, pl.BlockSpec((tm, D), row)]
        out_specs += [one, one]
        out_shape += [jax.ShapeDtypeStruct((1, D), F32), jax.ShapeDtypeStruct((1, D), F32)]
        args += tuple(head)
    return _call(
        body, name=name, grid=(T // tm,), comm=comm, in_specs=in_specs, out_specs=out_specs, out_shape=out_shape,
        scratch=[pltpu.VMEM((F, D), BF16) for _ in range(3)] + [pltpu.SemaphoreType.DMA((3,))], args=args)


def _ffn_bwd(dy, x, gain, s, t, wgt, wut, wd, name):
    T, D = x.shape
    F = wd.shape[0]
    tr, tf = _tile(T, FFN_BWD_ROWS), _tile(F, 256)
    rows = lambda i: (i, 0)
    one = pl.BlockSpec((1, D), lambda i: (0, 0))
    any_spec = pl.BlockSpec(memory_space=pl.ANY)

    def body(dy_ref, x_ref, gain_ref, s_ref, t_ref, wg_hbm, wu_hbm, wd_hbm,
             dx_ref, dg_ref, du_ref, dyh_ref, dgain_ref, wg_s, wu_s, wd_s, sem):
        @pl.when(pl.program_id(0) == 0)
        def _():
            _load_resident([(wg_hbm, wg_s), (wu_hbm, wu_s), (wd_hbm, wd_s)], sem)
            dgain_ref[...] = jnp.zeros_like(dgain_ref)

        dyh = (FFN_RESIDUAL_WEIGHT * dy_ref[...]).astype(BF16)
        dyh_ref[...] = dyh
        for f0 in range(0, F, tf):
            cols = slice(f0, f0 + tf)
            da = _mm_nt(dyh, wd_s[cols, :])
            dg_ref[:, cols] = (da * t_ref[:, cols].astype(F32)).astype(BF16)
            du_ref[:, cols] = (da * s_ref[:, cols].astype(F32)).astype(BF16)
        dhn = _mm(dg_ref[...], wg_s[...]) + _mm(du_ref[...], wu_s[...])
        r, xhat = _rms(x_ref[...])
        dgain_ref[...] += jnp.sum(dhn * xhat, axis=0, keepdims=True)
        dx_ref[...] = dy_ref[...] + _rms_bwd(dhn, gain_ref[...], r, xhat)

    wide = jax.ShapeDtypeStruct((T, F), BF16)
    return pl.pallas_call(
        body, name=name, grid=(T // tr,),
        in_specs=[pl.BlockSpec((tr, D), rows), pl.BlockSpec((tr, D), rows), one, pl.BlockSpec((tr, F), rows),
                  pl.BlockSpec((tr, F), rows), any_spec, any_spec, any_spec],
        out_specs=(pl.BlockSpec((tr, D), rows), pl.BlockSpec((tr, F), rows), pl.BlockSpec((tr, F), rows),
                   pl.BlockSpec((tr, D), rows), one),
        out_shape=(jax.ShapeDtypeStruct((T, D), F32), wide, wide, jax.ShapeDtypeStruct((T, D), BF16),
                   jax.ShapeDtypeStruct((1, D), F32)),
        scratch_shapes=[pltpu.VMEM((F, D), BF16) for _ in range(3)] + [pltpu.SemaphoreType.DMA((3,))],
        compiler_params=_params("arbitrary"),
    )(dy, x, gain, s, t, wgt, wut, wd)


def _wgrad(a, b, name, col_slab=None, comm=None):
    T, M = a.shape
    N = b.shape[1]
    tmm = M if M <= 1024 else _tile(M, 1408)
    tn = _tile(N, 1024)
    if col_slab and col_slab % LANES:
        tn = col_slab * LANES // math.gcd(col_slab, LANES)
    tk = _tile(T, 2048)
    nk = T // tk
    per = tn // col_slab if col_slab else 0

    def body(a_ref, b_ref, out_ref, acc):
        k = pl.program_id(2)

        @pl.when(k == 0)
        def _():
            acc[...] = jnp.zeros_like(acc)

        acc[...] += _mm_tn(a_ref[...], b_ref[...])

        @pl.when(k == nk - 1)
        def _():
            if col_slab:
                for s in range(per):
                    out_ref[s] = acc[:, s * col_slab:(s + 1) * col_slab].astype(BF16)
            else:
                out_ref[...] = acc[...].astype(BF16)

    if col_slab:
        out_spec = pl.BlockSpec((per, tmm, col_slab), lambda m, n, k: (n, m, 0))
        out_shape = jax.ShapeDtypeStruct((N // col_slab, M, col_slab), BF16)
    else:
        out_spec = pl.BlockSpec((tmm, tn), lambda m, n, k: (m, n))
        out_shape = jax.ShapeDtypeStruct((M, N), BF16)
    (out,), got = _call(
        body, name=name, grid=(M // tmm, N // tn, nk), comm=comm,
        in_specs=[pl.BlockSpec((tk, tmm), lambda m, n, k: (k, m)), pl.BlockSpec((tk, tn), lambda m, n, k: (k, n))],
        out_specs=(out_spec,), out_shape=(out_shape,),
        scratch=[pltpu.VMEM((tmm, tn), F32)], args=(a, b))
    return (out, got) if comm else out


def _mix_in_fwd(x, gain, w_int):
    T, D = x.shape
    C = w_int.shape[0]
    n_qkv = 3 * C // 4
    tm = _tile(T, 1024)

    def body(x_ref, gain_ref, w_ref, hn_ref, qkv_ref, u_ref):
        _, xhat = _rms(x_ref[...])
        hn = (xhat * gain_ref[...]).astype(BF16)
        hn_ref[...] = hn
        proj = _mm_nt(hn, w_ref[...])
        qkv_ref[...] = proj[:, :n_qkv].astype(BF16)
        u_ref[...] = proj[:, n_qkv:]

    row = lambda i: (i, 0)
    return pl.pallas_call(
        body, name="mix_in_fwd", grid=(T // tm,),
        in_specs=[pl.BlockSpec((tm, D), row), pl.BlockSpec((1, D), lambda i: (0, 0)),
                  pl.BlockSpec((C, D), lambda i: (0, 0))],
        out_specs=(pl.BlockSpec((tm, D), row), pl.BlockSpec((tm, n_qkv), row), pl.BlockSpec((tm, C - n_qkv), row)),
        out_shape=(jax.ShapeDtypeStruct((T, D), BF16), jax.ShapeDtypeStruct((T, n_qkv), BF16),
                   jax.ShapeDtypeStruct((T, C - n_qkv), F32)),
        compiler_params=_params("arbitrary"),
    )(x, gain, w_int)


def _mix_in_bwd(dres, dq, dk, dv, du, x, gain, w_int):
    T, D = x.shape
    C = w_int.shape[0]
    W = dq.shape[1]
    tm = _tile(T, 512)

    def body(dres_ref, dq_ref, dk_ref, dv_ref, du_ref, x_ref, gain_ref, w_ref, dx_ref, dproj_ref, dgain_ref):
        @pl.when(pl.program_id(0) == 0)
        def _():
            dgain_ref[...] = jnp.zeros_like(dgain_ref)

        for part, ref in enumerate((dq_ref, dk_ref, dv_ref, du_ref)):
            dproj_ref[:, part * W:(part + 1) * W] = ref[...]
        dhn = _mm(dproj_ref[...], w_ref[...])
        r, xhat = _rms(x_ref[...])
        dgain_ref[...] += jnp.sum(dhn * xhat, axis=0, keepdims=True)
        dx_ref[...] = dres_ref[...] + _rms_bwd(dhn, gain_ref[...], r, xhat)

    row = lambda i: (i, 0)
    one = pl.BlockSpec((1, D), lambda i: (0, 0))
    part = pl.BlockSpec((tm, W), row)
    return pl.pallas_call(
        body, name="mix_in_bwd", grid=(T // tm,),
        in_specs=[pl.BlockSpec((tm, D), row), part, part, part, part, pl.BlockSpec((tm, D), row), one,
                  pl.BlockSpec((C, D), lambda i: (0, 0))],
        out_specs=(pl.BlockSpec((tm, D), row), pl.BlockSpec((tm, C), row), one),
        out_shape=(jax.ShapeDtypeStruct((T, D), F32), jax.ShapeDtypeStruct((T, C), BF16),
                   jax.ShapeDtypeStruct((1, D), F32)),
        compiler_params=_params("arbitrary"),
    )(dres, dq, dk, dv, du, x, gain, w_int)


SB_PAIRS_PER_PROGRAM = 2
SB_FWD_PAIRS_PER_PROGRAM = 4
LOG2_E = 1.4426950408889634
EXP2_CLAMP = 126.0


def _neg_log2_sigmoid(nz2):
    w = jnp.minimum(nz2, EXP2_CLAMP)
    return w, jnp.log2(1.0 + jnp.exp2(w))


SB_DEAD_LOG2 = -160.0


def _sb_live(rests):
    worst = functools.reduce(jnp.maximum, rests)
    return (jnp.max(worst) > SB_DEAD_LOG2).astype(jnp.int32)


def _split(v):
    hi = v.astype(BF16)
    return hi, (v - hi.astype(F32)).astype(BF16)


def _tri_sum(v, tri):
    hi, lo = _split(v)
    return _mm(hi, tri) + _mm(lo, tri)


def _sb_fwd(qkv, B, S, comm=None):
    W = qkv.shape[2] // 3
    n_pair = W // LANES
    bq = _tile(S, 256)
    nq = S // bq
    hp = SB_FWD_PAIRS_PER_PROGRAM
    nscale2 = -(SB_HEAD_DIM ** -0.5) * LOG2_E

    def body(q_ref, k_ref, v_ref, o_ref):
        lane = lax.broadcasted_iota(jnp.int32, (1, LANES), 1)
        head0 = lane < SB_HEAD_DIM
        rr = lax.broadcasted_iota(jnp.int32, (bq, bq), 0)
        cc = lax.broadcasted_iota(jnp.int32, (bq, bq), 1)
        strict = cc < rr
        after = jnp.where(rr > cc, 1.0, 0.0).astype(BF16)

        def blocks(heads, ks, carries, diag):
            n = range(len(heads))
            keep = (lambda t: jnp.where(strict, t, 0.0)) if diag else (lambda t: t)
            z = [_mm_nt(qh, k_ref[ks, cols]) for qh, cols in heads]
            wl = [_neg_log2_sigmoid(z[h] * nscale2) for h in n]
            lr = [keep(wl[h][0] - wl[h][1]) for h in n]
            parts = [_split(lr[h]) for h in n]
            suf = [_mm(parts[h][0], after) + _mm(parts[h][1], after) for h in n]
            a = [keep(jnp.exp2(suf[h] + carries[h][1] - wl[h][1])).astype(BF16) for h in n]
            o = [carries[h][0] + _mm(a[h], v_ref[ks, heads[h][1]]) for h in n]
            return tuple((o[h], carries[h][1] + (suf[h][:, :1] + lr[h][:, :1])) for h in n)

        def q_tile(i, _):
            qs = pl.ds(pl.multiple_of(i * bq, bq), bq)
            heads = []
            for pr in range(hp):
                cols = slice(pr * LANES, (pr + 1) * LANES)
                qv = q_ref[qs, cols]
                heads += [(jnp.where(head0, qv, jnp.zeros_like(qv)), cols),
                          (jnp.where(head0, jnp.zeros_like(qv), qv), cols)]
            zero = (jnp.zeros((bq, LANES), F32), jnp.zeros((bq, 1), F32))
            init = blocks(heads, qs, (zero,) * len(heads), True)

            def left(st):
                t, _, cr = st
                ks = pl.ds(pl.multiple_of((i - 1 - t) * bq, bq), bq)
                cr = blocks(heads, ks, cr, False)
                return t + 1, _sb_live([c for _, c in cr]), cr

            _, _, res = lax.while_loop(lambda st: jnp.logical_and(st[0] < i, st[1] > 0), left,
                                       (jnp.int32(0), _sb_live([c for _, c in init]), init))
            for pr in range(hp):
                o_ref[qs, heads[2 * pr][1]] = jnp.where(head0, res[2 * pr][0], res[2 * pr + 1][0]).astype(BF16)
            return 0

        lax.fori_loop(0, nq, q_tile, 0)

    def col(off):
        return pl.BlockSpec((None, S, hp * LANES), lambda b, p: (b, 0, off + p))

    n_pair //= hp
    return _call(
        body, name="sb_fwd", grid=(B, n_pair), comm=comm,
        in_specs=[col(0), col(n_pair), col(2 * n_pair)],
        out_specs=(col(0),),
        out_shape=(jax.ShapeDtypeStruct((B, S, W), BF16),),
        args=(qkv, qkv, qkv))


def _sb_bwd(qkv, do, B, S, comm=None):
    W = qkv.shape[2] // 3
    n_pair = W // LANES
    bq = _tile(S, 256)
    nq = S // bq
    hp = SB_PAIRS_PER_PROGRAM
    scale = SB_HEAD_DIM ** -0.5
    nscale2 = -scale * LOG2_E

    def body(q_ref, k_ref, v_ref, do_ref, dq_ref, dk_ref, dv_ref, dk_s, dv_s, e_s, sg_s, a_s):
        lane = lax.broadcasted_iota(jnp.int32, (1, LANES), 1)
        head0 = lane < SB_HEAD_DIM
        rr = lax.broadcasted_iota(jnp.int32, (bq, bq), 0)
        cc = lax.broadcasted_iota(jnp.int32, (bq, bq), 1)
        strict = cc < rr
        after = jnp.where(rr > cc, 1.0, 0.0).astype(BF16)
        before = jnp.where(rr < cc, 1.0, 0.0).astype(BF16)
        dk_s[...] = jnp.zeros_like(dk_s)
        dv_s[...] = jnp.zeros_like(dv_s)

        def weights(heads, ks, kb, rests, diag):
            n = range(len(heads))
            keep = (lambda t: jnp.where(strict, t, 0.0)) if diag else (lambda t: t)
            z = [_mm_nt(heads[h][0], k_ref[ks, heads[h][2]]) for h in n]
            da = [_mm_nt(heads[h][1], v_ref[ks, heads[h][2]]) for h in n]
            wl = [_neg_log2_sigmoid(z[h] * nscale2) for h in n]
            lr = [keep(wl[h][0] - wl[h][1]) for h in n]
            parts = [_split(lr[h]) for h in n]
            suf = [_mm(parts[h][0], after) + _mm(parts[h][1], after) for h in n]
            a = [keep(jnp.exp2(suf[h] + rests[h] - wl[h][1])) for h in n]
            for h in n:
                a_s[h * nq + kb] = a[h].astype(BF16)
                e_s[h * nq + kb] = a[h] * da[h]
                sg_s[h * nq + kb] = jnp.exp2(-wl[h][1])
            return tuple(rests[h] + (suf[h][:, :1] + lr[h][:, :1]) for h in n)

        def grads(heads, ks, kb, carries, diag):
            n = range(len(heads))
            keep = (lambda t: jnp.where(strict, t, 0.0)) if diag else (lambda t: t)
            e = [e_s[h * nq + kb] for h in n]
            parts = [_split(e[h]) for h in n]
            pex = [_mm(parts[h][0], before) + _mm(parts[h][1], before) for h in n]
            dz = [keep(e[h] - sg_s[h * nq + kb] * (e[h] + pex[h] + carries[h][1])).astype(BF16) for h in n]
            dq = [carries[h][0] + _mm(dz[h], k_ref[ks, heads[h][2]]) for h in n]
            for h in n:
                dk_s[ks, heads[h][2]] += _mm_tn(dz[h], heads[h][0])
                dv_s[ks, heads[h][2]] += _mm_tn(a_s[h * nq + kb], heads[h][1])
            return tuple((dq[h], carries[h][1] + (pex[h][:, bq - 1:] + e[h][:, bq - 1:])) for h in n)

        def q_tile(i, _):
            qs = pl.ds(pl.multiple_of(i * bq, bq), bq)
            heads = []
            for pr in range(hp):
                cols = slice(pr * LANES, (pr + 1) * LANES)
                qv, dov = q_ref[qs, cols], do_ref[qs, cols]
                zq, zd = jnp.zeros_like(qv), jnp.zeros_like(dov)
                heads += [(jnp.where(head0, qv, zq), jnp.where(head0, dov, zd), cols),
                          (jnp.where(head0, zq, qv), jnp.where(head0, zd, dov), cols)]
            key_block = lambda kb: pl.ds(pl.multiple_of(kb * bq, bq), bq)
            rests = weights(heads, qs, i, (jnp.zeros((bq, 1), F32),) * len(heads), True)

            def left(st):
                t, _, rs = st
                rs = weights(heads, key_block(i - 1 - t), i - 1 - t, rs, False)
                return t + 1, _sb_live(rs), rs

            n_left, _, _ = lax.while_loop(lambda st: jnp.logical_and(st[0] < i, st[1] > 0), left,
                                          (jnp.int32(0), _sb_live(rests), rests))
            zero = (jnp.zeros((bq, LANES), F32), jnp.zeros((bq, 1), F32))
            res = lax.fori_loop(0, n_left, lambda t, cr: grads(heads, key_block(i - n_left + t), i - n_left + t, cr, False),
                                (zero,) * len(heads))
            res = grads(heads, qs, i, res, True)
            for pr in range(hp):
                dq = jnp.where(head0, res[2 * pr][0], res[2 * pr + 1][0])
                dq_ref[qs, heads[2 * pr][2]] = (dq * scale).astype(BF16)
            return 0

        lax.fori_loop(0, nq, q_tile, 0)
        dk_ref[...] = (dk_s[...] * scale).astype(BF16)
        dv_ref[...] = dv_s[...].astype(BF16)

    def col(off):
        return pl.BlockSpec((None, S, hp * LANES), lambda b, p: (b, 0, off + p))

    n_pair //= hp
    shp = jax.ShapeDtypeStruct((B, S, W), BF16)
    slots = 2 * hp * nq
    return _call(
        body, name="sb_bwd", grid=(B, n_pair), comm=comm,
        in_specs=[col(0), col(n_pair), col(2 * n_pair), col(0)],
        out_specs=(col(0), col(0), col(0)),
        out_shape=(shp, shp, shp),
        scratch=[pltpu.VMEM((S, hp * LANES), F32), pltpu.VMEM((S, hp * LANES), F32),
                 pltpu.VMEM((slots, bq, bq), F32), pltpu.VMEM((slots, bq, bq), F32),
                 pltpu.VMEM((slots, bq, bq), BF16)],
        args=(qkv, qkv, qkv, do))


def _pool_counts(S):
    t = lax.broadcasted_iota(jnp.int32, (S, 1), 0)
    return t, [jnp.minimum(t + 1, w).astype(F32) for w in POOL_WINDOWS]


def _pool_fwd(u, B, S):
    W = u.shape[2]

    def body(u_ref, out_ref):
        t, counts = _pool_counts(S)
        for gi, w in enumerate(POOL_WINDOWS):
            cols = slice(gi * POOL_GROUP_DIM, (gi + 1) * POOL_GROUP_DIM)
            ug = u_ref[:, cols]
            s, k = ug, 1
            while k < w:
                s = s + jnp.where(t >= k, pltpu.roll(s, k, axis=0), 0.0)
                k *= 2
            out_ref[:, cols] = (s / counts[gi] - ug).astype(BF16)

    spec = pl.BlockSpec((None, S, W), lambda b: (b, 0, 0))
    return pl.pallas_call(
        body, name="pool_fwd", grid=(B,), in_specs=[spec], out_specs=spec,
        out_shape=jax.ShapeDtypeStruct((B, S, W), BF16), compiler_params=_params("arbitrary"),
    )(u)


def _pool_bwd(dpooled, B, S):
    W = dpooled.shape[2]

    def body(d_ref, out_ref):
        t, counts = _pool_counts(S)
        for gi, w in enumerate(POOL_WINDOWS):
            cols = slice(gi * POOL_GROUP_DIM, (gi + 1) * POOL_GROUP_DIM)
            d = d_ref[:, cols]
            s, k = d / counts[gi], 1
            while k < w:
                s = s + jnp.where(t < S - k, pltpu.roll(s, S - k, axis=0), 0.0)
                k *= 2
            out_ref[:, cols] = (s - d).astype(BF16)

    spec = pl.BlockSpec((None, S, W), lambda b: (b, 0, 0))
    return pl.pallas_call(
        body, name="pool_bwd", grid=(B,), in_specs=[spec], out_specs=spec,
        out_shape=jax.ShapeDtypeStruct((B, S, W), BF16), compiler_params=_params("arbitrary"),
    )(dpooled)


def _mix_out_fwd(x, o_sb, pooled, w_pool, pool_scale, w_out):
    T, D = x.shape
    W = o_sb.shape[1]
    G = w_pool.shape[0]
    gd = POOL_GROUP_DIM
    tm = _tile(T, 1024)

    def body(x_ref, osb_ref, pooled_ref, wp_ref, ps_ref, wo_ref, out_ref, mixed_ref):
        mixed_ref[:, :W] = osb_ref[...]
        for gi in range(G):
            cols = slice(gi * gd, (gi + 1) * gd)
            pw = _mm(pooled_ref[:, cols], wp_ref[gi])
            mixed_ref[:, W + gi * gd:W + (gi + 1) * gd] = (pw * ps_ref[:, cols]).astype(BF16)
        out_ref[...] = x_ref[...] + _mm(mixed_ref[...], wo_ref[...].reshape(D, D))

    row = lambda i: (i, 0)
    return pl.pallas_call(
        body, name="mix_out_fwd", grid=(T // tm,),
        in_specs=[pl.BlockSpec((tm, D), row), pl.BlockSpec((tm, W), row), pl.BlockSpec((tm, W), row),
                  pl.BlockSpec((G, gd, gd), lambda i: (0, 0, 0)), pl.BlockSpec((1, W), lambda i: (0, 0)),
                  pl.BlockSpec(w_out.shape, lambda i: (0, 0, 0))],
        out_specs=(pl.BlockSpec((tm, D), row), pl.BlockSpec((tm, D), row)),
        out_shape=(jax.ShapeDtypeStruct((T, D), F32), jax.ShapeDtypeStruct((T, D), BF16)),
        compiler_params=_params("arbitrary"),
    )(x, o_sb, pooled, w_pool, pool_scale, w_out)


def _mix_out_bwd(dx, pooled, w_pool, pool_scale, w_out):
    T, D = dx.shape
    W = pooled.shape[1]
    G = w_pool.shape[0]
    gd = POOL_GROUP_DIM
    tm = _tile(T, 1024)

    def body(dx_ref, pooled_ref, wp_ref, ps_ref, wo_ref, dxb_ref, dosb_ref, dpooled_ref, dwp_ref, dps_ref):
        i = pl.program_id(0)

        @pl.when(i == 0)
        def _():
            dwp_ref[...] = jnp.zeros_like(dwp_ref)
            dps_ref[...] = jnp.zeros_like(dps_ref)

        dxb = dx_ref[...].astype(BF16)
        dxb_ref[...] = dxb
        dmixed = _mm_nt(dxb, wo_ref[...].reshape(D, D))
        dosb_ref[...] = dmixed[:, :W].astype(BF16)
        for gi in range(G):
            cols = slice(gi * gd, (gi + 1) * gd)
            pg = pooled_ref[:, cols]
            dop = dmixed[:, W + gi * gd:W + (gi + 1) * gd]
            pw = _mm(pg, wp_ref[gi])
            dps_ref[:, cols] += jnp.sum(dop * pw, axis=0, keepdims=True)
            dpw = (dop * ps_ref[:, cols]).astype(BF16)
            dwp_ref[gi] += _mm_tn(pg, dpw)
            dpooled_ref[:, cols] = _mm_nt(dpw, wp_ref[gi])

    row = lambda i: (i, 0)
    return pl.pallas_call(
        body, name="mix_out_bwd", grid=(T // tm,),
        in_specs=[pl.BlockSpec((tm, D), row), pl.BlockSpec((tm, W), row),
                  pl.BlockSpec((G, gd, gd), lambda i: (0, 0, 0)), pl.BlockSpec((1, W), lambda i: (0, 0)),
                  pl.BlockSpec(w_out.shape, lambda i: (0, 0, 0))],
        out_specs=(pl.BlockSpec((tm, D), row), pl.BlockSpec((tm, W), row), pl.BlockSpec((tm, W), row),
                   pl.BlockSpec((G, gd, gd), lambda i: (0, 0, 0)), pl.BlockSpec((1, W), lambda i: (0, 0))),
        out_shape=(jax.ShapeDtypeStruct((T, D), BF16), jax.ShapeDtypeStruct((T, W), BF16),
                   jax.ShapeDtypeStruct((T, W), F32), jax.ShapeDtypeStruct((G, gd, gd), F32),
                   jax.ShapeDtypeStruct((1, W), F32)),
        compiler_params=_params("arbitrary"),
    )(dx, pooled, w_pool, pool_scale, w_out)


def _mem_kv_fwd(mem, gain, w_kvt):
    B, M, D = mem.shape
    C = w_kvt.shape[0]

    def body(mem_ref, gain_ref, w_ref, memn_ref, kv_ref):
        _, xhat = _rms(mem_ref[...])
        mn = (xhat * gain_ref[...]).astype(BF16)
        memn_ref[...] = mn
        kv_ref[...] = _mm_nt(mn, w_ref[...]).astype(BF16)

    return pl.pallas_call(
        body, name="mem_kv_fwd", grid=(B,),
        in_specs=[pl.BlockSpec((None, M, D), lambda b: (b, 0, 0)), pl.BlockSpec((1, D), lambda b: (0, 0)),
                  pl.BlockSpec((C, D), lambda b: (0, 0))],
        out_specs=(pl.BlockSpec((M, D), lambda b: (b, 0)), pl.BlockSpec((None, M, C), lambda b: (b, 0, 0))),
        out_shape=(jax.ShapeDtypeStruct((B * M, D), BF16), jax.ShapeDtypeStruct((B, M, C), BF16)),
        compiler_params=_params("arbitrary"),
    )(mem, gain, w_kvt)


def _mem_kv_bwd(dkv, mem, w_kvt):
    B, M, D = mem.shape
    C = w_kvt.shape[0]

    def body(dkv_ref, mem_ref, w_ref, dkvb_ref, dgain_ref):
        @pl.when(pl.program_id(0) == 0)
        def _():
            dgain_ref[...] = jnp.zeros_like(dgain_ref)

        dkvb = dkv_ref[...].astype(BF16)
        dkvb_ref[...] = dkvb
        dmn = _mm(dkvb, w_ref[...])
        _, xhat = _rms(mem_ref[...])
        dgain_ref[...] += jnp.sum(dmn * xhat, axis=0, keepdims=True)

    return pl.pallas_call(
        body, name="mem_kv_bwd", grid=(B,),
        in_specs=[pl.BlockSpec((None, M, C), lambda b: (b, 0, 0)), pl.BlockSpec((None, M, D), lambda b: (b, 0, 0)),
                  pl.BlockSpec((C, D), lambda b: (0, 0))],
        out_specs=(pl.BlockSpec((M, C), lambda b: (b, 0)), pl.BlockSpec((1, D), lambda b: (0, 0))),
        out_shape=(jax.ShapeDtypeStruct((B * M, C), BF16), jax.ShapeDtypeStruct((1, D), F32)),
        compiler_params=_params("arbitrary"),
    )(dkv, mem, w_kvt)


def _softmax_rows(s):
    p = jnp.exp(s - jnp.max(s, axis=1, keepdims=True))
    return p / jnp.sum(p, axis=1, keepdims=True)


def _cross_fwd(x, gain, kv, w_q, w_o, B, S, comm=None):
    T, D = x.shape
    M = kv.shape[1]
    hd = D // MEM_HEADS
    tm = _tile(S, 1024)
    per = S // tm
    scale = hd ** -0.5

    def body(x_ref, gain_ref, kv_ref, wq_ref, wo_ref, out_ref, hq_ref, q_ref, ocat_ref):
        _, xhat = _rms(x_ref[...])
        hq = (xhat * gain_ref[...]).astype(BF16)
        hq_ref[...] = hq
        q = _mm(hq, wq_ref[...].reshape(D, D)).astype(BF16)
        q_ref[...] = q
        for h in range(MEM_HEADS):
            cols = slice(h * hd, (h + 1) * hd)
            s = _mm_nt(q[:, cols], kv_ref[:, cols]) * scale
            p = _softmax_rows(s).astype(BF16)
            ocat_ref[:, cols] = _mm(p, kv_ref[:, D + h * hd:D + (h + 1) * hd]).astype(BF16)
        out_ref[...] = x_ref[...] + _mm(ocat_ref[...], wo_ref[...].reshape(D, D))

    row = lambda b, t: (b * per + t, 0)
    wspec = pl.BlockSpec(w_q.shape, lambda b, t: (0, 0, 0))
    return _call(
        body, name="cross_fwd", grid=(B, per), comm=comm,
        in_specs=[pl.BlockSpec((tm, D), row), pl.BlockSpec((1, D), lambda b, t: (0, 0)),
                  pl.BlockSpec((None, M, 2 * D), lambda b, t: (b, 0, 0)), wspec, wspec],
        out_specs=tuple(pl.BlockSpec((tm, D), row) for _ in range(4)),
        out_shape=(jax.ShapeDtypeStruct((T, D), F32),) + tuple(jax.ShapeDtypeStruct((T, D), BF16) for _ in range(3)),
        args=(x, gain, kv, w_q, w_o))


def _cross_bwd(dy, x, gain, q, kv, w_q, w_o, B, S, comm=None):
    T, D = x.shape
    M = kv.shape[1]
    hd = D // MEM_HEADS
    tm = _tile(S, 512)
    per = S // tm
    scale = hd ** -0.5

    def body(dy_ref, x_ref, gain_ref, q_ref, kv_ref, wq_ref, wo_ref,
             dx_ref, dyb_ref, dqb_ref, dkv_ref, dgain_ref):
        b_id, t_id = pl.program_id(0), pl.program_id(1)

        @pl.when((b_id == 0) & (t_id == 0))
        def _():
            dgain_ref[...] = jnp.zeros_like(dgain_ref)

        @pl.when(t_id == 0)
        def _():
            dkv_ref[...] = jnp.zeros_like(dkv_ref)

        dyb = dy_ref[...].astype(BF16)
        dyb_ref[...] = dyb
        docat = _mm_nt(dyb, wo_ref[...].reshape(D, D)).astype(BF16)
        for h in range(MEM_HEADS):
            cols = slice(h * hd, (h + 1) * hd)
            vcols = slice(D + h * hd, D + (h + 1) * hd)
            qh, kh, vh, doh = q_ref[:, cols], kv_ref[:, cols], kv_ref[:, vcols], docat[:, cols]
            p = _softmax_rows(_mm_nt(qh, kh) * scale)
            dp = _mm_nt(doh, vh)
            ds = (p * (dp - jnp.sum(dp * p, axis=1, keepdims=True)) * scale).astype(BF16)
            dqb_ref[:, cols] = _mm(ds, kh).astype(BF16)
            dkv_ref[:, cols] += _mm_tn(ds, qh)
            dkv_ref[:, vcols] += _mm_tn(p.astype(BF16), doh)
        dhq = _mm_nt(dqb_ref[...], wq_ref[...].reshape(D, D))
        r, xhat = _rms(x_ref[...])
        dgain_ref[...] += jnp.sum(dhq * xhat, axis=0, keepdims=True)
        dx_ref[...] = dy_ref[...] + _rms_bwd(dhq, gain_ref[...], r, xhat)

    row = lambda b, t: (b * per + t, 0)
    wspec = pl.BlockSpec(w_q.shape, lambda b, t: (0, 0, 0))
    one = pl.BlockSpec((1, D), lambda b, t: (0, 0))
    kvspec = pl.BlockSpec((None, M, 2 * D), lambda b, t: (b, 0, 0))
    return _call(
        body, name="cross_bwd", grid=(B, per), comm=comm,
        in_specs=[pl.BlockSpec((tm, D), row), pl.BlockSpec((tm, D), row), one, pl.BlockSpec((tm, D), row),
                  kvspec, wspec, wspec],
        out_specs=(pl.BlockSpec((tm, D), row), pl.BlockSpec((tm, D), row), pl.BlockSpec((tm, D), row), kvspec, one),
        out_shape=(jax.ShapeDtypeStruct((T, D), F32), jax.ShapeDtypeStruct((T, D), BF16),
                   jax.ShapeDtypeStruct((T, D), BF16), jax.ShapeDtypeStruct((B, M, 2 * D), F32),
                   jax.ShapeDtypeStruct((1, D), F32)),
        args=(dy, x, gain, q, kv, w_q, w_o))


def _ordered_sum(gp_ref):
    g = gp_ref[0].astype(F32)
    for s in range(1, N_DEV):
        g = g + gp_ref[s].astype(F32)
    return g


def _adam_write(g, w_ref, m_ref, v_ref, g_ref, d_ref, nm_ref, nv_ref):
    nm = ADAM_B1 * m_ref[...] + (1.0 - ADAM_B1) * g
    nv = ADAM_B2 * v_ref[...] + (1.0 - ADAM_B2) * (g * g)
    m_hat = nm / (1.0 - ADAM_B1 ** ADAM_STEP)
    v_hat = nv / (1.0 - ADAM_B2 ** ADAM_STEP)
    g_ref[...] = g
    nm_ref[...] = nm
    nv_ref[...] = nv
    d_ref[...] = -ADAM_LR * (m_hat / (jnp.sqrt(v_hat) + ADAM_EPS) + ADAM_WD * w_ref[...])


def _adamw(items, name):
    R = items[0][1].shape[0]
    tr = next(t for t in (_tile(R, 128), _tile(R, 256)) if t % 16 == 0)
    n = len(items)

    def body(*refs):
        ins, outs = refs[:4 * n], refs[4 * n:]
        for k in range(n):
            gp_ref, w_ref, m_ref, v_ref = ins[4 * k:4 * k + 4]
            _adam_write(_ordered_sum(gp_ref), w_ref, m_ref, v_ref, *outs[4 * k:4 * k + 4])

    in_specs, out_specs, out_shape = [], [], []
    for _, w, _, _ in items:
        C = w.shape[1]
        spec = pl.BlockSpec((tr, C), lambda i: (i, 0))
        in_specs += [pl.BlockSpec((N_DEV, tr, C), lambda i: (0, i, 0)), spec, spec, spec]
        out_specs += [spec] * 4
        out_shape += [jax.ShapeDtypeStruct(w.shape, F32)] * 4
    res = pl.pallas_call(
        body, name=name, grid=(R // tr,), in_specs=in_specs, out_specs=tuple(out_specs), out_shape=tuple(out_shape),
        compiler_params=_params("arbitrary"),
    )(*[a for item in items for a in item])
    return [res[4 * k:4 * k + 4] for k in range(n)]


def _pack_rows(vectors, D):
    def body(*refs):
        out_ref = refs[-1]
        out_ref[...] = jnp.zeros_like(out_ref)
        for i, r in enumerate(refs[:-1]):
            out_ref[i:i + 1, :r.shape[1]] = r[...]

    vmem = pl.BlockSpec(memory_space=pltpu.VMEM)
    return pl.pallas_call(body, name="pack_small", in_specs=[vmem] * len(vectors), out_specs=vmem,
                          out_shape=jax.ShapeDtypeStruct((8, D), F32))(*vectors)


def _adamw_small(row_parts, mat_parts, vectors, matrix):
    n = len(vectors)

    def body(*refs):
        rp_ref, mp_ref = refs[0], refs[1]
        ins, outs = refs[2:2 + 3 * (n + 1)], refs[2 + 3 * (n + 1):]
        rows = _ordered_sum(rp_ref)
        for i in range(n):
            c = ins[3 * i].shape[1]
            _adam_write(rows[i:i + 1, :c], *ins[3 * i:3 * i + 3], *outs[4 * i:4 * i + 4])
        _adam_write(_ordered_sum(mp_ref), *ins[3 * n:3 * n + 3], *outs[4 * n:4 * n + 4])
        outs[-1][...] = rows[n:n + 1, :]

    flat = [a for wmv in vectors for a in wmv] + list(matrix)
    out_shape = [jax.ShapeDtypeStruct(wmv[0].shape, F32) for wmv in list(vectors) + [matrix] for _ in range(4)]
    out_shape.append(jax.ShapeDtypeStruct((1, row_parts.shape[2]), F32))
    vmem = pl.BlockSpec(memory_space=pltpu.VMEM)
    res = pl.pallas_call(body, name="adamw_small", in_specs=[vmem] * (2 + len(flat)),
                         out_specs=tuple([vmem] * len(out_shape)), out_shape=tuple(out_shape))(row_parts, mat_parts, *flat)
    return [res[4 * i:4 * i + 4] for i in range(n + 1)], res[-1]


def kernel(x, mem, ffn1_norm, ffn1_w_gate, ffn1_w_up, ffn1_w_down, mix_norm, w_in, w_pool, pool_scale, w_out, mem_q_norm, mem_kv_norm, mem_w_q, mem_w_kv, mem_w_o, ffn2_norm, ffn2_w_gate, ffn2_w_up, ffn2_w_down, final_norm, loss_target, m_ffn1_norm, m_ffn1_w_gate, m_ffn1_w_up, m_ffn1_w_down, m_mix_norm, m_w_in, m_w_pool, m_pool_scale, m_w_out, m_mem_q_norm, m_mem_kv_norm, m_mem_w_q, m_mem_w_kv, m_mem_w_o, m_ffn2_norm, m_ffn2_w_gate, m_ffn2_w_up, m_ffn2_w_down, m_final_norm, v_ffn1_norm, v_ffn1_w_gate, v_ffn1_w_up, v_ffn1_w_down, v_mix_norm, v_w_in, v_w_pool, v_pool_scale, v_w_out, v_mem_q_norm, v_mem_kv_norm, v_mem_w_q, v_mem_w_kv, v_mem_w_o, v_ffn2_norm, v_ffn2_w_gate, v_ffn2_w_up, v_ffn2_w_down, v_final_norm):
    B, S, D = x.shape
    T = B * S
    x0 = x.reshape(T, D)
    target = loss_target.reshape(T, D)
    final_gain = final_norm.reshape(1, D)

    big = dict(
        g1=ffn1_w_gate[0].T, u1=ffn1_w_up[0].T, d1=ffn1_w_down[0],
        g2=ffn2_w_gate[0].T, u2=ffn2_w_up[0].T, d2=ffn2_w_down[0],
        w_in=w_in[0].T, w_out=w_out[0], w_q=mem_w_q[0], w_kv=mem_w_kv[0].T, w_o=mem_w_o[0])
    names = list(big)
    shard = {k: big[k].astype(BF16) for k in names}
    wp = w_pool[0].astype(BF16)
    full, ffn_w = {}, {}
    stacked = ("g1", "u1", "d1", "g2", "u2", "d2", "w_in", "w_kv")

    def gathered(keys, arrs):
        full.update(zip(keys, arrs))
        ffn_w.update({k: full[k].reshape(-1, D) for k in keys if k in stacked})

    first, mid = ("g1", "u1", "d1"), ("w_in", "w_out", "w_q", "w_kv", "w_o")
    gathered(first, _gather_two_level([shard[k] for k in first], "gather_ffn1"))
    (x1, hn1, a1, s1, t1), got = _ffn_fwd(x0, ffn1_norm, ffn_w["g1"], ffn_w["u1"], ffn_w["d1"], "ffn1_fwd",
                                          comm=([shard[k] for k in mid], True))
    gathered(mid, got)
    hn2, qkv, u = _mix_in_fwd(x1, mix_norm, ffn_w["w_in"])
    qkv3 = qkv.reshape(B, S, -1)
    (o_sb,), got = _sb_fwd(qkv3, B, S, comm=([shard["g2"], shard["u2"]], True))
    gathered(("g2", "u2"), got)
    pooled = _pool_fwd(u.reshape(B, S, -1), B, S).reshape(T, -1)
    x2, mixed = _mix_out_fwd(x1, o_sb.reshape(T, -1), pooled, wp, pool_scale, full["w_out"])
    memn, kv = _mem_kv_fwd(mem, mem_kv_norm, ffn_w["w_kv"])
    (x3, hq, q, ocat), got = _cross_fwd(x2, mem_q_norm, kv, full["w_q"], full["w_o"], B, S,
                                        comm=([shard["d2"]], True))
    gathered(("d2",), got)
    (dx4, hn4, a2, s2, t2, d_final, loss_part), _ = _ffn_fwd(x3, ffn2_norm, ffn_w["g2"], ffn_w["u2"], ffn_w["d2"],
                                                            "ffn2_fwd", head=(final_gain, target))

    slab = lambda k: grads[k].reshape((N_DEV, -1) + grads[k].shape[-1:])
    got = {}
    dx3, dg2, du2, dyh2, d_ffn2 = _ffn_bwd(dx4, x3, ffn2_norm, s2, t2, ffn_w["g2"], ffn_w["u2"],
                                          ffn_w["d2"], "ffn2_bwd")
    ffn_slab = ffn1_w_gate.shape[2]
    grads = dict(g2=_wgrad(hn4, dg2, "dw_gate2", col_slab=ffn_slab), u2=_wgrad(hn4, du2, "dw_up2", col_slab=ffn_slab),
                 d2=_wgrad(a2, dyh2, "dw_down2"))
    (dx2, dx3b, dqb, dkv, d_q), (got["g2"],) = _cross_bwd(dx3, x2, mem_q_norm, q, kv, full["w_q"], full["w_o"], B, S,
                                                         comm=([slab("g2")], False))
    grads["w_o"] = _wgrad(ocat, dx3b, "dw_o")
    grads["w_q"] = _wgrad(hq, dqb, "dw_q")
    dkvb, d_kv = _mem_kv_bwd(dkv, mem, ffn_w["w_kv"])
    grads["w_kv"] = _wgrad(memn, dkvb, "dw_kv", col_slab=mem_w_kv.shape[2])
    dx2b, do_sb, dpooled, d_wpool, d_ps = _mix_out_bwd(dx2, pooled, wp, pool_scale, full["w_out"])
    grads["w_out"] = _wgrad(mixed, dx2b, "dw_out")
    du = _pool_bwd(dpooled.reshape(B, S, -1), B, S).reshape(T, -1)
    early = ("u2", "d2", "w_o", "w_q", "w_kv", "w_out")
    (dq, dk, dv), res = _sb_bwd(qkv3, do_sb.reshape(B, S, -1), B, S, comm=([slab(k) for k in early], False))
    got.update(zip(early, res))
    dx1, dproj, d_mix = _mix_in_bwd(dx2, dq.reshape(T, -1), dk.reshape(T, -1), dv.reshape(T, -1), du,
                                    x1, mix_norm, ffn_w["w_in"])
    grads["w_in"] = _wgrad(hn2, dproj, "dw_in", col_slab=w_in.shape[2])
    dx0, dg1, du1, dyh1, d_ffn1 = _ffn_bwd(dx1, x0, ffn1_norm, s1, t1, ffn_w["g1"], ffn_w["u1"],
                                          ffn_w["d1"], "ffn1_bwd")

    small = [("ffn1_norm", d_ffn1, ffn1_norm, m_ffn1_norm, v_ffn1_norm),
             ("mix_norm", d_mix, mix_norm, m_mix_norm, v_mix_norm),
             ("pool_scale", d_ps, pool_scale, m_pool_scale, v_pool_scale),
             ("mem_q_norm", d_q, mem_q_norm, m_mem_q_norm, v_mem_q_norm),
             ("mem_kv_norm", d_kv, mem_kv_norm, m_mem_kv_norm, v_mem_kv_norm),
             ("ffn2_norm", d_ffn2, ffn2_norm, m_ffn2_norm, v_ffn2_norm),
             ("final_norm", d_final, final_gain, m_final_norm.reshape(1, D), v_final_norm.reshape(1, D))]
    row_pack = _pack_rows([t[1] for t in small] + [loss_part], D)
    as_rows = lambda t: t.reshape(-1, LANES)
    grads["g1"], (row_parts, pool_parts, got["w_in"]) = _wgrad(
        hn1, dg1, "dw_gate1", col_slab=ffn_slab,
        comm=([row_pack, as_rows(d_wpool), slab("w_in")], [True, True, False]))
    grads["u1"], (got["g1"],) = _wgrad(hn1, du1, "dw_up1", col_slab=ffn_slab, comm=([slab("g1")], False))
    grads["d1"], (got["u1"],) = _wgrad(a1, dyh1, "dw_down1", comm=([slab("u1")], False))
    got["d1"] = _exchange([slab("d1")], False, "scatter_last")[0]

    state = dict(
        g1=(ffn1_w_gate, m_ffn1_w_gate, v_ffn1_w_gate), u1=(ffn1_w_up, m_ffn1_w_up, v_ffn1_w_up),
        d1=(ffn1_w_down, m_ffn1_w_down, v_ffn1_w_down), g2=(ffn2_w_gate, m_ffn2_w_gate, v_ffn2_w_gate),
        u2=(ffn2_w_up, m_ffn2_w_up, v_ffn2_w_up), d2=(ffn2_w_down, m_ffn2_w_down, v_ffn2_w_down),
        w_in=(w_in, m_w_in, v_w_in), w_out=(w_out, m_w_out, v_w_out), w_q=(mem_w_q, m_mem_w_q, v_mem_w_q),
        w_kv=(mem_w_kv, m_mem_w_kv, v_mem_w_kv), w_o=(mem_w_o, m_mem_w_o, v_mem_w_o))
    big_out = {}
    groups = dict(gate_up=("g1", "u1", "g2", "u2"), in_kv=("w_in", "w_kv"), square=("w_out", "w_q", "w_o"),
                  down=("d1", "d2"))
    for label, keys in groups.items():
        items = [(got[k],) + tuple(t[0] for t in state[k]) for k in keys]
        for k, outs in zip(keys, _adamw(items, "adamw_" + label)):
            big_out[k] = [t[None] for t in outs]

    small_res, loss_row = _adamw_small(row_parts, pool_parts, [t[2:] for t in small],
                                       [as_rows(t) for t in (w_pool, m_w_pool, v_w_pool)])
    small_out = {t[0]: small_res[i] for i, t in enumerate(small)}
    small_out["final_norm"] = [t.reshape(D) for t in small_out["final_norm"]]
    small_out["w_pool"] = [t.reshape(w_pool.shape) for t in small_res[-1]]
    loss = loss_row[0, 0]

    order = [("ffn1_norm", None), ("ffn1_w_gate", "g1"), ("ffn1_w_up", "u1"), ("ffn1_w_down", "d1"),
             ("mix_norm", None), ("w_in", "w_in"), ("w_pool", None), ("pool_scale", None), ("w_out", "w_out"),
             ("mem_q_norm", None), ("mem_kv_norm", None), ("mem_w_q", "w_q"), ("mem_w_kv", "w_kv"),
             ("mem_w_o", "w_o"), ("ffn2_norm", None), ("ffn2_w_gate", "g2"), ("ffn2_w_up", "u2"),
             ("ffn2_w_down", "d2"), ("final_norm", None)]
    res = [loss, dx0.reshape(B, S, D)]
    for which in range(4):
        for name, key in order:
            res.append(big_out[key][which] if key else small_out[name][which])
    return tuple(res)
```

```python
import functools
import math

import jax
import jax.numpy as jnp
from jax import lax
from jax.experimental import pallas as pl
from jax.experimental.pallas import tpu as pltpu

F32 = jnp.float32
BF16 = jnp.bfloat16

N_DEV = 8
EPS = 1e-6
SB_HEAD_DIM = 64
LANES = 128
POOL_WINDOWS = (2, 4, 8, 16)
POOL_GROUP_DIM = 128
MEM_HEADS = 4
FFN_RESIDUAL_WEIGHT = 0.5
ADAM_LR = 0.001
ADAM_B1 = 0.9
ADAM_B2 = 0.999
ADAM_EPS = 1e-08
ADAM_WD = 0.01
ADAM_STEP = 10
VMEM_LIMIT = 56 * 1024 * 1024

MESH_ID = pl.DeviceIdType.MESH


def _params(*sem):
    return pltpu.CompilerParams(dimension_semantics=sem, vmem_limit_bytes=VMEM_LIMIT)


def _tile(n, pref):
    if n <= pref:
        return n
    t = pref - pref % 8
    while n % t:
        t -= 8
    return t


def _mm(a, b):
    return jnp.dot(a, b, preferred_element_type=F32)


def _mm_nt(a, b):
    return lax.dot_general(a, b, (((1,), (1,)), ((), ())), preferred_element_type=F32)


def _mm_tn(a, b):
    return lax.dot_general(a, b, (((0,), (0,)), ((), ())), preferred_element_type=F32)


def _rms(xv):
    r = lax.rsqrt(jnp.mean(xv * xv, axis=-1, keepdims=True) + EPS)
    return r, xv * r


def _rms_bwd(dhn, gain, r, xhat):
    dxh = dhn * gain
    return r * (dxh - xhat * jnp.mean(dxh * xhat, axis=-1, keepdims=True))


def _sigmoid(z):
    return 0.5 * jnp.tanh(0.5 * z) + 0.5


def _flags(arrs, gather):
    return [gather] * len(arrs) if isinstance(gather, bool) else list(gather)


def _comm_shapes(arrs, gather):
    return tuple(jax.ShapeDtypeStruct(((N_DEV,) + tuple(a.shape)) if f else tuple(a.shape), a.dtype)
                 for a, f in zip(arrs, _flags(arrs, gather)))


def _comm_start(ins, outs, sems, gather):
    send_sems, recv_sems, local_sems = sems
    gather = _flags(ins, gather)
    x, y, c = lax.axis_index("x"), lax.axis_index("y"), lax.axis_index("c")
    me = 4 * x + 2 * y + c
    for i in range(len(ins)):
        src = ins[i] if gather[i] else ins[i].at[me]
        pltpu.make_async_copy(src, outs[i].at[me], local_sems.at[i]).start()
    for k in range(1, N_DEV):
        px = 1 - x if k & 4 else x
        py = 1 - y if k & 2 else y
        pc = 1 - c if k & 1 else c
        peer = 4 * px + 2 * py + pc
        for i in range(len(ins)):
            src = ins[i] if gather[i] else ins[i].at[peer]
            pltpu.make_async_remote_copy(
                src_ref=src, dst_ref=outs[i].at[me],
                send_sem=send_sems.at[i], recv_sem=recv_sems.at[i],
                device_id=(px, py, pc), device_id_type=MESH_ID).start()


def _comm_wait(ins, outs, sems, gather):
    send_sems, recv_sems, local_sems = sems
    gather = _flags(ins, gather)
    x, y, c = lax.axis_index("x"), lax.axis_index("y"), lax.axis_index("c")
    me = 4 * x + 2 * y + c
    for i in range(len(ins)):
        seven = outs[i].at[pl.ds(0, N_DEV - 1)]
        done = pltpu.make_async_remote_copy(
            src_ref=seven, dst_ref=seven,
            send_sem=send_sems.at[i], recv_sem=recv_sems.at[i],
            device_id=(x, y, c), device_id_type=MESH_ID)
        done.wait_send()
        done.wait_recv()
        src = ins[i] if gather[i] else ins[i].at[me]
        pltpu.make_async_copy(src, outs[i].at[me], local_sems.at[i]).wait()


def _comm_sems(n):
    return [pltpu.SemaphoreType.DMA((n,)) for _ in range(3)]


def _exchange(arrs, gather, name):
    n = len(arrs)

    def body(*refs):
        ins, outs, sems = refs[:n], refs[n:2 * n], refs[2 * n:]
        _comm_start(ins, outs, sems, gather)
        _comm_wait(ins, outs, sems, gather)

    any_spec = pl.BlockSpec(memory_space=pl.ANY)
    outs = pl.pallas_call(
        body, name=name, out_shape=_comm_shapes(arrs, gather),
        in_specs=[any_spec] * n, out_specs=tuple([any_spec] * n), scratch_shapes=_comm_sems(n),
    )(*arrs)
    return list(outs)


def _gather_two_level(arrs, name):
    n = len(arrs)

    def body(*refs):
        ins, outs = refs[:n], refs[n:2 * n]
        send_sems, recv_sems, local_sems = refs[2 * n:]
        x, y, c = lax.axis_index("x"), lax.axis_index("y"), lax.axis_index("c")
        me, sibling = (x, y, c), (x, y, 1 - c)
        chips = [(1 - x, y), (x, 1 - y), (1 - x, 1 - y)]

        def copy(i, k, block, to, own=False):
            slab = outs[i].at[4 * block[0] + 2 * block[1] + block[2]]
            return pltpu.make_async_remote_copy(
                src_ref=ins[i] if own else slab, dst_ref=slab,
                send_sem=send_sems.at[i, k], recv_sem=recv_sems.at[i, k],
                device_id=to, device_id_type=MESH_ID)

        mine = [pltpu.make_async_copy(ins[i], outs[i].at[4 * x + 2 * y + c], local_sems.at[i]) for i in range(n)]
        first = [copy(i, 0, me, sibling, own=True) for i in range(n)]
        first += [copy(i, 1 + j, me, (*chip, c), own=True) for j, chip in enumerate(chips) for i in range(n)]
        for cp in mine + first:
            cp.start()
        passed = []
        for j, chip in enumerate(chips):
            for i in range(n):
                copy(i, 1 + j, (*chip, c), me).wait_recv()
                passed.append(copy(i, 4 + j, (*chip, c), sibling))
                passed[-1].start()
        for i in range(n):
            copy(i, 0, sibling, me).wait_recv()
            for j, chip in enumerate(chips):
                copy(i, 4 + j, (*chip, 1 - c), me).wait_recv()
        for cp in first + passed:
            cp.wait_send()
        for cp in mine:
            cp.wait()

    any_spec = pl.BlockSpec(memory_space=pl.ANY)
    outs = pl.pallas_call(
        body, name=name, out_shape=_comm_shapes(arrs, True),
        in_specs=[any_spec] * n, out_specs=tuple([any_spec] * n),
        scratch_shapes=[pltpu.SemaphoreType.DMA((n, N_DEV - 1)), pltpu.SemaphoreType.DMA((n, N_DEV - 1)),
                        pltpu.SemaphoreType.DMA((n,))],
    )(*arrs)
    return list(outs)


CHIPS = N_DEV // 2
PAIRSUM = "pairsum"


def _pairsum_shapes(arrs):
    return tuple(jax.ShapeDtypeStruct((CHIPS,) + tuple(a.shape[1:]), a.dtype) for a in arrs)


def _pairsum_scratch(arrs):
    bufs = [pltpu.VMEM((CHIPS,) + tuple(a.shape[1:]), a.dtype) for a in arrs for _ in range(3)]
    return bufs + [pltpu.SemaphoreType.DMA((len(arrs),)) for _ in range(6)]


def _pairsum_start(ins, outs, sc):
    n = len(ins)
    load, d2d_send, d2d_recv = sc[3 * n], sc[3 * n + 1], sc[3 * n + 2]
    x, y, c = lax.axis_index("x"), lax.axis_index("y"), lax.axis_index("c")
    for i in range(n):
        mine, theirs = sc[3 * i], sc[3 * i + 1]
        for q in range(CHIPS):
            pltpu.make_async_copy(ins[i].at[2 * q + c], mine.at[q], load.at[i]).start()
            pltpu.make_async_remote_copy(
                src_ref=ins[i].at[2 * q + (1 - c)], dst_ref=theirs.at[q], send_sem=d2d_send.at[i],
                recv_sem=d2d_recv.at[i], device_id=(x, y, 1 - c), device_id_type=MESH_ID).start()


def _pairsum_middle(ins, outs, sc):
    n = len(ins)
    load, d2d_send, d2d_recv, ici_send, ici_recv, store = sc[3 * n:3 * n + 6]
    x, y, c = lax.axis_index("x"), lax.axis_index("y"), lax.axis_index("c")
    here = 2 * x + y
    for i in range(n):
        mine, theirs, total = sc[3 * i:3 * i + 3]
        pltpu.make_async_copy(mine, mine, load.at[i]).wait()
        pltpu.make_async_remote_copy(src_ref=theirs, dst_ref=theirs, send_sem=d2d_send.at[i], recv_sem=d2d_recv.at[i],
                                     device_id=(x, y, c), device_id_type=MESH_ID).wait_recv()
        for q in range(CHIPS):
            total[q] = (mine[q].astype(F32) + theirs[q].astype(F32)).astype(total.dtype)
        for k in range(1, CHIPS):
            px = 1 - x if k & 2 else x
            py = 1 - y if k & 1 else y
            pltpu.make_async_remote_copy(
                src_ref=total.at[2 * px + py], dst_ref=outs[i].at[here], send_sem=ici_send.at[i],
                recv_sem=ici_recv.at[i], device_id=(px, py, c), device_id_type=MESH_ID).start()
        pltpu.make_async_copy(total.at[here], outs[i].at[here], store.at[i]).start()


def _pairsum_wait(ins, outs, sc):
    n = len(ins)
    load, d2d_send, d2d_recv, ici_send, ici_recv, store = sc[3 * n:3 * n + 6]
    x, y, c = lax.axis_index("x"), lax.axis_index("y"), lax.axis_index("c")
    here = 2 * x + y
    for i in range(n):
        theirs, total = sc[3 * i + 1], sc[3 * i + 2]
        three = outs[i].at[pl.ds(0, CHIPS - 1)]
        pltpu.make_async_remote_copy(src_ref=theirs, dst_ref=theirs, send_sem=d2d_send.at[i], recv_sem=d2d_recv.at[i],
                                     device_id=(x, y, c), device_id_type=MESH_ID).wait_send()
        done = pltpu.make_async_remote_copy(src_ref=three, dst_ref=three, send_sem=ici_send.at[i],
                                            recv_sem=ici_recv.at[i], device_id=(x, y, c), device_id_type=MESH_ID)
        done.wait_send()
        done.wait_recv()
        pltpu.make_async_copy(total.at[here], outs[i].at[here], store.at[i]).wait()


def _pairsum_exchange(arrs, name):
    n = len(arrs)

    def body(*refs):
        ins, outs, sc = refs[:n], refs[n:2 * n], refs[2 * n:]
        _pairsum_start(ins, outs, sc)
        _pairsum_middle(ins, outs, sc)
        _pairsum_wait(ins, outs, sc)

    any_spec = pl.BlockSpec(memory_space=pl.ANY)
    outs = pl.pallas_call(
        body, name=name, out_shape=_pairsum_shapes(arrs), in_specs=[any_spec] * n, out_specs=tuple([any_spec] * n),
        scratch_shapes=_pairsum_scratch(arrs), compiler_params=pltpu.CompilerParams(vmem_limit_bytes=VMEM_LIMIT),
    )(*arrs)
    return list(outs)


def _call(body, *, name, grid, in_specs, out_specs, out_shape, args, scratch=(), comm=None):
    sem = ("arbitrary",) * len(grid)
    if comm is None:
        res = pl.pallas_call(body, name=name, grid=grid, in_specs=list(in_specs), out_specs=tuple(out_specs),
                             out_shape=tuple(out_shape), scratch_shapes=list(scratch),
                             compiler_params=_params(*sem))(*args)
        return tuple(res), []
    arrs, gather = comm
    pairsum = isinstance(gather, str) and gather == PAIRSUM
    n, n_in, n_out, n_sc = len(arrs), len(args), len(out_shape), len(scratch)
    steps = math.prod(grid)

    def wrapped(*refs):
        ins, cin = refs[:n_in], refs[n_in:n_in + n]
        outs, cout = refs[n_in + n:n_in + n + n_out], refs[n_in + n + n_out:n_in + 2 * n + n_out]
        sc, sems = refs[n_in + 2 * n + n_out:n_in + 2 * n + n_out + n_sc], refs[n_in + 2 * n + n_out + n_sc:]
        step = functools.reduce(lambda acc, ax: acc * grid[ax] + pl.program_id(ax), range(len(grid)), 0)

        @pl.when(step == 0)
        def _():
            if pairsum:
                _pairsum_start(cin, cout, sems)
            else:
                _comm_start(cin, cout, sems, gather)

        if pairsum:
            @pl.when(step == steps // 2)
            def _():
                _pairsum_middle(cin, cout, sems)

        body(*ins, *outs, *sc)

        @pl.when(step == steps - 1)
        def _():
            if pairsum:
                _pairsum_wait(cin, cout, sems)
            else:
                _comm_wait(cin, cout, sems, gather)

    any_spec = pl.BlockSpec(memory_space=pl.ANY)
    res = pl.pallas_call(
        wrapped, name=name, grid=grid, in_specs=list(in_specs) + [any_spec] * n,
        out_specs=tuple(out_specs) + (any_spec,) * n,
        out_shape=tuple(out_shape) + (_pairsum_shapes(arrs) if pairsum else _comm_shapes(arrs, gather)),
        scratch_shapes=list(scratch) + (_pairsum_scratch(arrs) if pairsum else _comm_sems(n)),
        compiler_params=_params(*sem))(*args, *arrs)
    return tuple(res[:n_out]), list(res[n_out:])


FFN_BWD_ROWS = 256


def _load_resident(pairs, sem):
    copies = [pltpu.make_async_copy(src, dst, sem.at[k]) for k, (src, dst) in enumerate(pairs)]
    for cp in copies:
        cp.start()
    for cp in copies:
        cp.wait()


def _ffn_fwd(x, gain, wgt, wut, wd, name, comm=None, head=None):
    T, D = x.shape
    F = wd.shape[0]
    tm, tf = _tile(T, 512), _tile(F, 256)

    def body(*refs):
        if head:
            (x_ref, gain_ref, wg_hbm, wu_hbm, wd_hbm, fgain_ref, tgt_ref,
             out_ref, hn_ref, a_ref, s_ref, t_ref, dfgain_ref, loss_ref, wg_s, wu_s, wd_s, sem) = refs
        else:
            (x_ref, gain_ref, wg_hbm, wu_hbm, wd_hbm,
             out_ref, hn_ref, a_ref, s_ref, t_ref, wg_s, wu_s, wd_s, sem) = refs

        @pl.when(pl.program_id(0) == 0)
        def _():
            _load_resident([(wg_hbm, wg_s), (wu_hbm, wu_s), (wd_hbm, wd_s)], sem)
            if head:
                dfgain_ref[...] = jnp.zeros_like(dfgain_ref)
                loss_ref[...] = jnp.zeros_like(loss_ref)

        _, xhat = _rms(x_ref[...])
        hn = (xhat * gain_ref[...]).astype(BF16)
        hn_ref[...] = hn
        for f0 in range(0, F, tf):
            cols = slice(f0, f0 + tf)
            g = _mm_nt(hn, wg_s[cols, :])
            u = _mm_nt(hn, wu_s[cols, :])
            sig = _sigmoid(g)
            s = g * sig
            a_ref[:, cols] = (s * u).astype(BF16)
            s_ref[:, cols] = s.astype(BF16)
            t_ref[:, cols] = (u * (sig + s * (1.0 - sig))).astype(BF16)
        y = x_ref[...] + FFN_RESIDUAL_WEIGHT * _mm(a_ref[...], wd_s[...])
        if head:
            r, yhat = _rms(y)
            err = yhat * fgain_ref[...] - tgt_ref[...]
            loss_ref[...] += 0.5 * jnp.sum(jnp.mean(err * err, axis=-1, keepdims=True), axis=0, keepdims=True)
            dy = err * (1.0 / D)
            dfgain_ref[...] += jnp.sum(dy * yhat, axis=0, keepdims=True)
            out_ref[...] = _rms_bwd(dy, fgain_ref[...], r, yhat)
        else:
            out_ref[...] = y

    row = lambda i: (i, 0)
    one = pl.BlockSpec((1, D), lambda i: (0, 0))
    hbm = pl.BlockSpec(memory_space=pl.ANY)
    in_specs = [pl.BlockSpec((tm, D), row), one, hbm, hbm, hbm]
    out_specs = [pl.BlockSpec((tm, D), row), pl.BlockSpec((tm, D), row)] + [pl.BlockSpec((tm, F), row) for _ in range(3)]
    out_shape = [jax.ShapeDtypeStruct((T, D), F32), jax.ShapeDtypeStruct((T, D), BF16)] \
        + [jax.ShapeDtypeStruct((T, F), BF16) for _ in range(3)]
    args = (x, gain, wgt, wut, wd)
    if head:
        in_specs += [one, pl.BlockSpec((tm, D), row)]
        out_specs += [one, one]
        out_shape += [jax.ShapeDtypeStruct((1, D), F32), jax.ShapeDtypeStruct((1, D), F32)]
        args += tuple(head)
    return _call(
        body, name=name, grid=(T // tm,), comm=comm, in_specs=in_specs, out_specs=out_specs, out_shape=out_shape,
        scratch=[pltpu.VMEM((F, D), BF16) for _ in range(3)] + [pltpu.SemaphoreType.DMA((3,))], args=args)


def _ffn_bwd(dy, x, gain, s, t, wgt, wut, wd, name):
    T, D = x.shape
    F = wd.shape[0]
    tr, tf = _tile(T, FFN_BWD_ROWS), _tile(F, 256)
    rows = lambda i: (i, 0)
    one = pl.BlockSpec((1, D), lambda i: (0, 0))
    any_spec = pl.BlockSpec(memory_space=pl.ANY)

    def body(dy_ref, x_ref, gain_ref, s_ref, t_ref, wg_hbm, wu_hbm, wd_hbm,
             dx_ref, dg_ref, du_ref, dyh_ref, dgain_ref, wg_s, wu_s, wd_s, sem):
        @pl.when(pl.program_id(0) == 0)
        def _():
            _load_resident([(wg_hbm, wg_s), (wu_hbm, wu_s), (wd_hbm, wd_s)], sem)
            dgain_ref[...] = jnp.zeros_like(dgain_ref)

        dyh = (FFN_RESIDUAL_WEIGHT * dy_ref[...]).astype(BF16)
        dyh_ref[...] = dyh
        for f0 in range(0, F, tf):
            cols = slice(f0, f0 + tf)
            da = _mm_nt(dyh, wd_s[cols, :])
            dg_ref[:, cols] = (da * t_ref[:, cols].astype(F32)).astype(BF16)
            du_ref[:, cols] = (da * s_ref[:, cols].astype(F32)).astype(BF16)
        dhn = _mm(dg_ref[...], wg_s[...]) + _mm(du_ref[...], wu_s[...])
        r, xhat = _rms(x_ref[...])
        dgain_ref[...] += jnp.sum(dhn * xhat, axis=0, keepdims=True)
        dx_ref[...] = dy_ref[...] + _rms_bwd(dhn, gain_ref[...], r, xhat)

    wide = jax.ShapeDtypeStruct((T, F), BF16)
    return pl.pallas_call(
        body, name=name, grid=(T // tr,),
        in_specs=[pl.BlockSpec((tr, D), rows), pl.BlockSpec((tr, D), rows), one, pl.BlockSpec((tr, F), rows),
                  pl.BlockSpec((tr, F), rows), any_spec, any_spec, any_spec],
        out_specs=(pl.BlockSpec((tr, D), rows), pl.BlockSpec((tr, F), rows), pl.BlockSpec((tr, F), rows),
                   pl.BlockSpec((tr, D), rows), one),
        out_shape=(jax.ShapeDtypeStruct((T, D), F32), wide, wide, jax.ShapeDtypeStruct((T, D), BF16),
                   jax.ShapeDtypeStruct((1, D), F32)),
        scratch_shapes=[pltpu.VMEM((F, D), BF16) for _ in range(3)] + [pltpu.SemaphoreType.DMA((3,))],
        compiler_params=_params("arbitrary"),
    )(dy, x, gain, s, t, wgt, wut, wd)


def _wgrad(a, b, name, col_slab=None, comm=None):
    T, M = a.shape
    N = b.shape[1]
    tmm = M if M <= 1024 else _tile(M, 1408)
    tn = _tile(N, 1024)
    if col_slab and col_slab % LANES:
        tn = col_slab * LANES // math.gcd(col_slab, LANES)
    tk = _tile(T, 2048)
    nk = T // tk
    per = tn // col_slab if col_slab else 0

    def body(a_ref, b_ref, out_ref, acc):
        k = pl.program_id(2)

        @pl.when(k == 0)
        def _():
            acc[...] = jnp.zeros_like(acc)

        acc[...] += _mm_tn(a_ref[...], b_ref[...])

        @pl.when(k == nk - 1)
        def _():
            if col_slab:
                for s in range(per):
                    out_ref[s] = acc[:, s * col_slab:(s + 1) * col_slab].astype(BF16)
            else:
                out_ref[...] = acc[...].astype(BF16)

    if col_slab:
        out_spec = pl.BlockSpec((per, tmm, col_slab), lambda m, n, k: (n, m, 0))
        out_shape = jax.ShapeDtypeStruct((N // col_slab, M, col_slab), BF16)
    else:
        out_spec = pl.BlockSpec((tmm, tn), lambda m, n, k: (m, n))
        out_shape = jax.ShapeDtypeStruct((M, N), BF16)
    (out,), got = _call(
        body, name=name, grid=(M // tmm, N // tn, nk), comm=comm,
        in_specs=[pl.BlockSpec((tk, tmm), lambda m, n, k: (k, m)), pl.BlockSpec((tk, tn), lambda m, n, k: (k, n))],
        out_specs=(out_spec,), out_shape=(out_shape,),
        scratch=[pltpu.VMEM((tmm, tn), F32)], args=(a, b))
    return (out, got) if comm else out


def _mix_in_fwd(x, gain, w_int):
    T, D = x.shape
    C = w_int.shape[0]
    n_qkv = 3 * C // 4
    tm = _tile(T, 1024)

    def body(x_ref, gain_ref, w_ref, hn_ref, qkv_ref, u_ref):
        _, xhat = _rms(x_ref[...])
        hn = (xhat * gain_ref[...]).astype(BF16)
        hn_ref[...] = hn
        proj = _mm_nt(hn, w_ref[...])
        qkv_ref[...] = proj[:, :n_qkv].astype(BF16)
        u_ref[...] = proj[:, n_qkv:]

    row = lambda i: (i, 0)
    return pl.pallas_call(
        body, name="mix_in_fwd", grid=(T // tm,),
        in_specs=[pl.BlockSpec((tm, D), row), pl.BlockSpec((1, D), lambda i: (0, 0)),
                  pl.BlockSpec((C, D), lambda i: (0, 0))],
        out_specs=(pl.BlockSpec((tm, D), row), pl.BlockSpec((tm, n_qkv), row), pl.BlockSpec((tm, C - n_qkv), row)),
        out_shape=(jax.ShapeDtypeStruct((T, D), BF16), jax.ShapeDtypeStruct((T, n_qkv), BF16),
                   jax.ShapeDtypeStruct((T, C - n_qkv), F32)),
        compiler_params=_params("arbitrary"),
    )(x, gain, w_int)


def _mix_in_bwd(dres, dq, dk, dv, du, x, gain, w_int):
    T, D = x.shape
    C = w_int.shape[0]
    W = dq.shape[1]
    tm = _tile(T, 512)

    def body(dres_ref, dq_ref, dk_ref, dv_ref, du_ref, x_ref, gain_ref, w_ref, dx_ref, dproj_ref, dgain_ref):
        @pl.when(pl.program_id(0) == 0)
        def _():
            dgain_ref[...] = jnp.zeros_like(dgain_ref)

        for part, ref in enumerate((dq_ref, dk_ref, dv_ref, du_ref)):
            dproj_ref[:, part * W:(part + 1) * W] = ref[...]
        dhn = _mm(dproj_ref[...], w_ref[...])
        r, xhat = _rms(x_ref[...])
        dgain_ref[...] += jnp.sum(dhn * xhat, axis=0, keepdims=True)
        dx_ref[...] = dres_ref[...] + _rms_bwd(dhn, gain_ref[...], r, xhat)

    row = lambda i: (i, 0)
    one = pl.BlockSpec((1, D), lambda i: (0, 0))
    part = pl.BlockSpec((tm, W), row)
    return pl.pallas_call(
        body, name="mix_in_bwd", grid=(T // tm,),
        in_specs=[pl.BlockSpec((tm, D), row), part, part, part, part, pl.BlockSpec((tm, D), row), one,
                  pl.BlockSpec((C, D), lambda i: (0, 0))],
        out_specs=(pl.BlockSpec((tm, D), row), pl.BlockSpec((tm, C), row), one),
        out_shape=(jax.ShapeDtypeStruct((T, D), F32), jax.ShapeDtypeStruct((T, C), BF16),
                   jax.ShapeDtypeStruct((1, D), F32)),
        compiler_params=_params("arbitrary"),
    )(dres, dq, dk, dv, du, x, gain, w_int)


SB_PAIRS_PER_PROGRAM = 2
SB_FWD_PAIRS_PER_PROGRAM = 4
LOG2_E = 1.4426950408889634
EXP2_CLAMP = 126.0


def _neg_log2_sigmoid(nz2):
    w = jnp.minimum(nz2, EXP2_CLAMP)
    return w, jnp.log2(1.0 + jnp.exp2(w))


SB_DEAD_LOG2 = -160.0


def _sb_live(rests):
    worst = functools.reduce(jnp.maximum, rests)
    return (jnp.max(worst) > SB_DEAD_LOG2).astype(jnp.int32)


def _split(v):
    hi = v.astype(BF16)
    return hi, (v - hi.astype(F32)).astype(BF16)


def _sb_fwd(qkv, B, S, comm=None):
    W = qkv.shape[2] // 3
    n_pair = W // LANES
    bq = _tile(S, 256)
    nq = S // bq
    hp = SB_FWD_PAIRS_PER_PROGRAM
    nscale2 = -(SB_HEAD_DIM ** -0.5) * LOG2_E

    def body(q_ref, k_ref, v_ref, o_ref):
        lane = lax.broadcasted_iota(jnp.int32, (1, LANES), 1)
        head0 = lane < SB_HEAD_DIM
        rr = lax.broadcasted_iota(jnp.int32, (bq, bq), 0)
        cc = lax.broadcasted_iota(jnp.int32, (bq, bq), 1)
        strict = cc < rr
        after = jnp.where(rr > cc, 1.0, 0.0).astype(BF16)

        def blocks(heads, ks, carries, diag):
            n = range(len(heads))
            keep = (lambda t: jnp.where(strict, t, 0.0)) if diag else (lambda t: t)
            z = [_mm_nt(qh, k_ref[ks, cols]) for qh, cols in heads]
            wl = [_neg_log2_sigmoid(z[h] * nscale2) for h in n]
            lr = [keep(wl[h][0] - wl[h][1]) for h in n]
            parts = [_split(lr[h]) for h in n]
            suf = [_mm(parts[h][0], after) + _mm(parts[h][1], after) for h in n]
            a = [keep(jnp.exp2(suf[h] + carries[h][1] - wl[h][1])).astype(BF16) for h in n]
            o = [carries[h][0] + _mm(a[h], v_ref[ks, heads[h][1]]) for h in n]
            return tuple((o[h], carries[h][1] + (suf[h][:, :1] + lr[h][:, :1])) for h in n)

        def q_tile(i, _):
            qs = pl.ds(pl.multiple_of(i * bq, bq), bq)
            heads = []
            for pr in range(hp):
                cols = slice(pr * LANES, (pr + 1) * LANES)
                qv = q_ref[qs, cols]
                heads += [(jnp.where(head0, qv, jnp.zeros_like(qv)), cols),
                          (jnp.where(head0, jnp.zeros_like(qv), qv), cols)]
            zero = (jnp.zeros((bq, LANES), F32), jnp.zeros((bq, 1), F32))
            init = blocks(heads, qs, (zero,) * len(heads), True)

            def left(st):
                t, _, cr = st
                ks = pl.ds(pl.multiple_of((i - 1 - t) * bq, bq), bq)
                cr = blocks(heads, ks, cr, False)
                return t + 1, _sb_live([c for _, c in cr]), cr

            _, _, res = lax.while_loop(lambda st: jnp.logical_and(st[0] < i, st[1] > 0), left,
                                       (jnp.int32(0), _sb_live([c for _, c in init]), init))
            for pr in range(hp):
                o_ref[qs, heads[2 * pr][1]] = jnp.where(head0, res[2 * pr][0], res[2 * pr + 1][0]).astype(BF16)
            return 0

        lax.fori_loop(0, nq, q_tile, 0)

    def col(off):
        return pl.BlockSpec((None, S, hp * LANES), lambda b, p: (b, 0, off + p))

    n_pair //= hp
    return _call(
        body, name="sb_fwd", grid=(B, n_pair), comm=comm,
        in_specs=[col(0), col(n_pair), col(2 * n_pair)],
        out_specs=(col(0),),
        out_shape=(jax.ShapeDtypeStruct((B, S, W), BF16),),
        args=(qkv, qkv, qkv))


def _sb_bwd(qkv, do, B, S, comm=None):
    W = qkv.shape[2] // 3
    n_pair = W // LANES
    bq = _tile(S, 256)
    nq = S // bq
    hp = SB_PAIRS_PER_PROGRAM
    scale = SB_HEAD_DIM ** -0.5
    nscale2 = -scale * LOG2_E

    def body(q_ref, k_ref, v_ref, do_ref, dq_ref, dk_ref, dv_ref, dk_s, dv_s, e_s, sg_s, a_s):
        lane = lax.broadcasted_iota(jnp.int32, (1, LANES), 1)
        head0 = lane < SB_HEAD_DIM
        rr = lax.broadcasted_iota(jnp.int32, (bq, bq), 0)
        cc = lax.broadcasted_iota(jnp.int32, (bq, bq), 1)
        strict = cc < rr
        after = jnp.where(rr > cc, 1.0, 0.0).astype(BF16)
        before = jnp.where(rr < cc, 1.0, 0.0).astype(BF16)
        dk_s[...] = jnp.zeros_like(dk_s)
        dv_s[...] = jnp.zeros_like(dv_s)

        def weights(heads, ks, kb, rests, diag):
            n = range(len(heads))
            keep = (lambda t: jnp.where(strict, t, 0.0)) if diag else (lambda t: t)
            z = [_mm_nt(heads[h][0], k_ref[ks, heads[h][2]]) for h in n]
            da = [_mm_nt(heads[h][1], v_ref[ks, heads[h][2]]) for h in n]
            wl = [_neg_log2_sigmoid(z[h] * nscale2) for h in n]
            lr = [keep(wl[h][0] - wl[h][1]) for h in n]
            parts = [_split(lr[h]) for h in n]
            suf = [_mm(parts[h][0], after) + _mm(parts[h][1], after) for h in n]
            a = [keep(jnp.exp2(suf[h] + rests[h] - wl[h][1])) for h in n]
            for h in n:
                a_s[h * nq + kb] = a[h].astype(BF16)
                e_s[h * nq + kb] = a[h] * da[h]
                sg_s[h * nq + kb] = jnp.exp2(-wl[h][1])
            return tuple(rests[h] + (suf[h][:, :1] + lr[h][:, :1]) for h in n)

        def grads(heads, ks, kb, carries, diag):
            n = range(len(heads))
            keep = (lambda t: jnp.where(strict, t, 0.0)) if diag else (lambda t: t)
            e = [e_s[h * nq + kb] for h in n]
            parts = [_split(e[h]) for h in n]
            pex = [_mm(parts[h][0], before) + _mm(parts[h][1], before) for h in n]
            dz = [keep(e[h] - sg_s[h * nq + kb] * (e[h] + pex[h] + carries[h][1])).astype(BF16) for h in n]
            dq = [carries[h][0] + _mm(dz[h], k_ref[ks, heads[h][2]]) for h in n]
            for h in n:
                dk_s[ks, heads[h][2]] += _mm_tn(dz[h], heads[h][0])
                dv_s[ks, heads[h][2]] += _mm_tn(a_s[h * nq + kb], heads[h][1])
            return tuple((dq[h], carries[h][1] + (pex[h][:, bq - 1:] + e[h][:, bq - 1:])) for h in n)

        def q_tile(i, _):
            qs = pl.ds(pl.multiple_of(i * bq, bq), bq)
            heads = []
            for pr in range(hp):
                cols = slice(pr * LANES, (pr + 1) * LANES)
                qv, dov = q_ref[qs, cols], do_ref[qs, cols]
                zq, zd = jnp.zeros_like(qv), jnp.zeros_like(dov)
                heads += [(jnp.where(head0, qv, zq), jnp.where(head0, dov, zd), cols),
                          (jnp.where(head0, zq, qv), jnp.where(head0, zd, dov), cols)]
            key_block = lambda kb: pl.ds(pl.multiple_of(kb * bq, bq), bq)
            rests = weights(heads, qs, i, (jnp.zeros((bq, 1), F32),) * len(heads), True)

            def left(st):
                t, _, rs = st
                rs = weights(heads, key_block(i - 1 - t), i - 1 - t, rs, False)
                return t + 1, _sb_live(rs), rs

            n_left, _, _ = lax.while_loop(lambda st: jnp.logical_and(st[0] < i, st[1] > 0), left,
                                          (jnp.int32(0), _sb_live(rests), rests))
            zero = (jnp.zeros((bq, LANES), F32), jnp.zeros((bq, 1), F32))
            res = lax.fori_loop(0, n_left, lambda t, cr: grads(heads, key_block(i - n_left + t), i - n_left + t, cr, False),
                                (zero,) * len(heads))
            res = grads(heads, qs, i, res, True)
            for pr in range(hp):
                dq = jnp.where(head0, res[2 * pr][0], res[2 * pr + 1][0])
                dq_ref[qs, heads[2 * pr][2]] = (dq * scale).astype(BF16)
            return 0

        lax.fori_loop(0, nq, q_tile, 0)
        dk_ref[...] = (dk_s[...] * scale).astype(BF16)
        dv_ref[...] = dv_s[...].astype(BF16)

    def col(off):
        return pl.BlockSpec((None, S, hp * LANES), lambda b, p: (b, 0, off + p))

    n_pair //= hp
    shp = jax.ShapeDtypeStruct((B, S, W), BF16)
    slots = 2 * hp * nq
    return _call(
        body, name="sb_bwd", grid=(B, n_pair), comm=comm,
        in_specs=[col(0), col(n_pair), col(2 * n_pair), col(0)],
        out_specs=(col(0), col(0), col(0)),
        out_shape=(shp, shp, shp),
        scratch=[pltpu.VMEM((S, hp * LANES), F32), pltpu.VMEM((S, hp * LANES), F32),
                 pltpu.VMEM((slots, bq, bq), F32), pltpu.VMEM((slots, bq, bq), F32),
                 pltpu.VMEM((slots, bq, bq), BF16)],
        args=(qkv, qkv, qkv, do))


def _pool_counts(S):
    t = lax.broadcasted_iota(jnp.int32, (S, 1), 0)
    return t, [jnp.minimum(t + 1, w).astype(F32) for w in POOL_WINDOWS]


def _pool_fwd(u, B, S):
    W = u.shape[2]

    def body(u_ref, out_ref):
        t, counts = _pool_counts(S)
        for gi, w in enumerate(POOL_WINDOWS):
            cols = slice(gi * POOL_GROUP_DIM, (gi + 1) * POOL_GROUP_DIM)
            ug = u_ref[:, cols]
            s, k = ug, 1
            while k < w:
                s = s + jnp.where(t >= k, pltpu.roll(s, k, axis=0), 0.0)
                k *= 2
            out_ref[:, cols] = (s / counts[gi] - ug).astype(BF16)

    spec = pl.BlockSpec((None, S, W), lambda b: (b, 0, 0))
    return pl.pallas_call(
        body, name="pool_fwd", grid=(B,), in_specs=[spec], out_specs=spec,
        out_shape=jax.ShapeDtypeStruct((B, S, W), BF16), compiler_params=_params("arbitrary"),
    )(u)


def _pool_bwd(dpooled, B, S):
    W = dpooled.shape[2]

    def body(d_ref, out_ref):
        t, counts = _pool_counts(S)
        for gi, w in enumerate(POOL_WINDOWS):
            cols = slice(gi * POOL_GROUP_DIM, (gi + 1) * POOL_GROUP_DIM)
            d = d_ref[:, cols]
            s, k = d / counts[gi], 1
            while k < w:
                s = s + jnp.where(t < S - k, pltpu.roll(s, S - k, axis=0), 0.0)
                k *= 2
            out_ref[:, cols] = (s - d).astype(BF16)

    spec = pl.BlockSpec((None, S, W), lambda b: (b, 0, 0))
    return pl.pallas_call(
        body, name="pool_bwd", grid=(B,), in_specs=[spec], out_specs=spec,
        out_shape=jax.ShapeDtypeStruct((B, S, W), BF16), compiler_params=_params("arbitrary"),
    )(dpooled)


def _mix_out_fwd(x, o_sb, pooled, w_pool, pool_scale, w_out):
    T, D = x.shape
    W = o_sb.shape[1]
    G = w_pool.shape[0]
    gd = POOL_GROUP_DIM
    tm = _tile(T, 1024)

    def body(x_ref, osb_ref, pooled_ref, wp_ref, ps_ref, wo_ref, out_ref, mixed_ref):
        mixed_ref[:, :W] = osb_ref[...]
        for gi in range(G):
            cols = slice(gi * gd, (gi + 1) * gd)
            pw = _mm(pooled_ref[:, cols], wp_ref[gi])
            mixed_ref[:, W + gi * gd:W + (gi + 1) * gd] = (pw * ps_ref[:, cols]).astype(BF16)
        out_ref[...] = x_ref[...] + _mm(mixed_ref[...], wo_ref[...].reshape(D, D))

    row = lambda i: (i, 0)
    return pl.pallas_call(
        body, name="mix_out_fwd", grid=(T // tm,),
        in_specs=[pl.BlockSpec((tm, D), row), pl.BlockSpec((tm, W), row), pl.BlockSpec((tm, W), row),
                  pl.BlockSpec((G, gd, gd), lambda i: (0, 0, 0)), pl.BlockSpec((1, W), lambda i: (0, 0)),
                  pl.BlockSpec(w_out.shape, lambda i: (0, 0, 0))],
        out_specs=(pl.BlockSpec((tm, D), row), pl.BlockSpec((tm, D), row)),
        out_shape=(jax.ShapeDtypeStruct((T, D), F32), jax.ShapeDtypeStruct((T, D), BF16)),
        compiler_params=_params("arbitrary"),
    )(x, o_sb, pooled, w_pool, pool_scale, w_out)


def _mix_out_bwd(dx, pooled, w_pool, pool_scale, w_out):
    T, D = dx.shape
    W = pooled.shape[1]
    G = w_pool.shape[0]
    gd = POOL_GROUP_DIM
    tm = _tile(T, 1024)

    def body(dx_ref, pooled_ref, wp_ref, ps_ref, wo_ref, dxb_ref, dosb_ref, dpooled_ref, dwp_ref, dps_ref):
        i = pl.program_id(0)

        @pl.when(i == 0)
        def _():
            dwp_ref[...] = jnp.zeros_like(dwp_ref)
            dps_ref[...] = jnp.zeros_like(dps_ref)

        dxb = dx_ref[...].astype(BF16)
        dxb_ref[...] = dxb
        dmixed = _mm_nt(dxb, wo_ref[...].reshape(D, D))
        dosb_ref[...] = dmixed[:, :W].astype(BF16)
        for gi in range(G):
            cols = slice(gi * gd, (gi + 1) * gd)
            pg = pooled_ref[:, cols]
            dop = dmixed[:, W + gi * gd:W + (gi + 1) * gd]
            pw = _mm(pg, wp_ref[gi])
            dps_ref[:, cols] += jnp.sum(dop * pw, axis=0, keepdims=True)
            dpw = (dop * ps_ref[:, cols]).astype(BF16)
            dwp_ref[gi] += _mm_tn(pg, dpw)
            dpooled_ref[:, cols] = _mm_nt(dpw, wp_ref[gi])

    row = lambda i: (i, 0)
    return pl.pallas_call(
        body, name="mix_out_bwd", grid=(T // tm,),
        in_specs=[pl.BlockSpec((tm, D), row), pl.BlockSpec((tm, W), row),
                  pl.BlockSpec((G, gd, gd), lambda i: (0, 0, 0)), pl.BlockSpec((1, W), lambda i: (0, 0)),
                  pl.BlockSpec(w_out.shape, lambda i: (0, 0, 0))],
        out_specs=(pl.BlockSpec((tm, D), row), pl.BlockSpec((tm, W), row), pl.BlockSpec((tm, W), row),
                   pl.BlockSpec((G, gd, gd), lambda i: (0, 0, 0)), pl.BlockSpec((1, W), lambda i: (0, 0))),
        out_shape=(jax.ShapeDtypeStruct((T, D), BF16), jax.ShapeDtypeStruct((T, W), BF16),
                   jax.ShapeDtypeStruct((T, W), F32), jax.ShapeDtypeStruct((G, gd, gd), F32),
                   jax.ShapeDtypeStruct((1, W), F32)),
        compiler_params=_params("arbitrary"),
    )(dx, pooled, w_pool, pool_scale, w_out)


def _mem_kv_fwd(mem, gain, w_kvt):
    B, M, D = mem.shape
    C = w_kvt.shape[0]

    def body(mem_ref, gain_ref, w_ref, memn_ref, kv_ref):
        _, xhat = _rms(mem_ref[...])
        mn = (xhat * gain_ref[...]).astype(BF16)
        memn_ref[...] = mn
        kv_ref[...] = _mm_nt(mn, w_ref[...]).astype(BF16)

    return pl.pallas_call(
        body, name="mem_kv_fwd", grid=(B,),
        in_specs=[pl.BlockSpec((None, M, D), lambda b: (b, 0, 0)), pl.BlockSpec((1, D), lambda b: (0, 0)),
                  pl.BlockSpec((C, D), lambda b: (0, 0))],
        out_specs=(pl.BlockSpec((M, D), lambda b: (b, 0)), pl.BlockSpec((None, M, C), lambda b: (b, 0, 0))),
        out_shape=(jax.ShapeDtypeStruct((B * M, D), BF16), jax.ShapeDtypeStruct((B, M, C), BF16)),
        compiler_params=_params("arbitrary"),
    )(mem, gain, w_kvt)


def _mem_kv_bwd(dkv, mem, w_kvt):
    B, M, D = mem.shape
    C = w_kvt.shape[0]

    def body(dkv_ref, mem_ref, w_ref, dkvb_ref, dgain_ref):
        @pl.when(pl.program_id(0) == 0)
        def _():
            dgain_ref[...] = jnp.zeros_like(dgain_ref)

        dkvb = dkv_ref[...].astype(BF16)
        dkvb_ref[...] = dkvb
        dmn = _mm(dkvb, w_ref[...])
        _, xhat = _rms(mem_ref[...])
        dgain_ref[...] += jnp.sum(dmn * xhat, axis=0, keepdims=True)

    return pl.pallas_call(
        body, name="mem_kv_bwd", grid=(B,),
        in_specs=[pl.BlockSpec((None, M, C), lambda b: (b, 0, 0)), pl.BlockSpec((None, M, D), lambda b: (b, 0, 0)),
                  pl.BlockSpec((C, D), lambda b: (0, 0))],
        out_specs=(pl.BlockSpec((M, C), lambda b: (b, 0)), pl.BlockSpec((1, D), lambda b: (0, 0))),
        out_shape=(jax.ShapeDtypeStruct((B * M, C), BF16), jax.ShapeDtypeStruct((1, D), F32)),
        compiler_params=_params("arbitrary"),
    )(dkv, mem, w_kvt)


def _softmax_rows(s):
    p = jnp.exp(s - jnp.max(s, axis=1, keepdims=True))
    return p / jnp.sum(p, axis=1, keepdims=True)


def _cross_fwd(x, gain, kv, w_q, w_o, B, S, comm=None):
    T, D = x.shape
    M = kv.shape[1]
    hd = D // MEM_HEADS
    tm = _tile(S, 1024)
    per = S // tm
    scale = hd ** -0.5

    def body(x_ref, gain_ref, kv_ref, wq_ref, wo_ref, out_ref, hq_ref, q_ref, ocat_ref):
        _, xhat = _rms(x_ref[...])
        hq = (xhat * gain_ref[...]).astype(BF16)
        hq_ref[...] = hq
        q = _mm(hq, wq_ref[...].reshape(D, D)).astype(BF16)
        q_ref[...] = q
        for h in range(MEM_HEADS):
            cols = slice(h * hd, (h + 1) * hd)
            s = _mm_nt(q[:, cols], kv_ref[:, cols]) * scale
            p = _softmax_rows(s).astype(BF16)
            ocat_ref[:, cols] = _mm(p, kv_ref[:, D + h * hd:D + (h + 1) * hd]).astype(BF16)
        out_ref[...] = x_ref[...] + _mm(ocat_ref[...], wo_ref[...].reshape(D, D))

    row = lambda b, t: (b * per + t, 0)
    wspec = pl.BlockSpec(w_q.shape, lambda b, t: (0, 0, 0))
    return _call(
        body, name="cross_fwd", grid=(B, per), comm=comm,
        in_specs=[pl.BlockSpec((tm, D), row), pl.BlockSpec((1, D), lambda b, t: (0, 0)),
                  pl.BlockSpec((None, M, 2 * D), lambda b, t: (b, 0, 0)), wspec, wspec],
        out_specs=tuple(pl.BlockSpec((tm, D), row) for _ in range(4)),
        out_shape=(jax.ShapeDtypeStruct((T, D), F32),) + tuple(jax.ShapeDtypeStruct((T, D), BF16) for _ in range(3)),
        args=(x, gain, kv, w_q, w_o))


def _cross_bwd(dy, x, gain, q, kv, w_q, w_o, B, S, comm=None):
    T, D = x.shape
    M = kv.shape[1]
    hd = D // MEM_HEADS
    tm = _tile(S, 512)
    per = S // tm
    scale = hd ** -0.5

    def body(dy_ref, x_ref, gain_ref, q_ref, kv_ref, wq_ref, wo_ref,
             dx_ref, dyb_ref, dqb_ref, dkv_ref, dgain_ref):
        b_id, t_id = pl.program_id(0), pl.program_id(1)

        @pl.when((b_id == 0) & (t_id == 0))
        def _():
            dgain_ref[...] = jnp.zeros_like(dgain_ref)

        @pl.when(t_id == 0)
        def _():
            dkv_ref[...] = jnp.zeros_like(dkv_ref)

        dyb = dy_ref[...].astype(BF16)
        dyb_ref[...] = dyb
        docat = _mm_nt(dyb, wo_ref[...].reshape(D, D)).astype(BF16)
        for h in range(MEM_HEADS):
            cols = slice(h * hd, (h + 1) * hd)
            vcols = slice(D + h * hd, D + (h + 1) * hd)
            qh, kh, vh, doh = q_ref[:, cols], kv_ref[:, cols], kv_ref[:, vcols], docat[:, cols]
            p = _softmax_rows(_mm_nt(qh, kh) * scale)
            dp = _mm_nt(doh, vh)
            ds = (p * (dp - jnp.sum(dp * p, axis=1, keepdims=True)) * scale).astype(BF16)
            dqb_ref[:, cols] = _mm(ds, kh).astype(BF16)
            dkv_ref[:, cols] += _mm_tn(ds, qh)
            dkv_ref[:, vcols] += _mm_tn(p.astype(BF16), doh)
        dhq = _mm_nt(dqb_ref[...], wq_ref[...].reshape(D, D))
        r, xhat = _rms(x_ref[...])
        dgain_ref[...] += jnp.sum(dhq * xhat, axis=0, keepdims=True)
        dx_ref[...] = dy_ref[...] + _rms_bwd(dhq, gain_ref[...], r, xhat)

    row = lambda b, t: (b * per + t, 0)
    wspec = pl.BlockSpec(w_q.shape, lambda b, t: (0, 0, 0))
    one = pl.BlockSpec((1, D), lambda b, t: (0, 0))
    kvspec = pl.BlockSpec((None, M, 2 * D), lambda b, t: (b, 0, 0))
    return _call(
        body, name="cross_bwd", grid=(B, per), comm=comm,
        in_specs=[pl.BlockSpec((tm, D), row), pl.BlockSpec((tm, D), row), one, pl.BlockSpec((tm, D), row),
                  kvspec, wspec, wspec],
        out_specs=(pl.BlockSpec((tm, D), row), pl.BlockSpec((tm, D), row), pl.BlockSpec((tm, D), row), kvspec, one),
        out_shape=(jax.ShapeDtypeStruct((T, D), F32), jax.ShapeDtypeStruct((T, D), BF16),
                   jax.ShapeDtypeStruct((T, D), BF16), jax.ShapeDtypeStruct((B, M, 2 * D), F32),
                   jax.ShapeDtypeStruct((1, D), F32)),
        args=(dy, x, gain, q, kv, w_q, w_o))


def _ordered_sum(gp_ref):
    g = gp_ref[0].astype(F32)
    for s in range(1, gp_ref.shape[0]):
        g = g + gp_ref[s].astype(F32)
    return g


def _adam_write(g, w_ref, m_ref, v_ref, g_ref, d_ref, nm_ref, nv_ref):
    nm = ADAM_B1 * m_ref[...] + (1.0 - ADAM_B1) * g
    nv = ADAM_B2 * v_ref[...] + (1.0 - ADAM_B2) * (g * g)
    m_hat = nm / (1.0 - ADAM_B1 ** ADAM_STEP)
    v_hat = nv / (1.0 - ADAM_B2 ** ADAM_STEP)
    g_ref[...] = g
    nm_ref[...] = nm
    nv_ref[...] = nv
    d_ref[...] = -ADAM_LR * (m_hat / (jnp.sqrt(v_hat) + ADAM_EPS) + ADAM_WD * w_ref[...])


def _adamw(items, name):
    R = items[0][1].shape[0]
    tr = next(t for t in (_tile(R, 128), _tile(R, 256)) if t % 16 == 0)
    n = len(items)

    def body(*refs):
        ins, outs = refs[:4 * n], refs[4 * n:]
        for k in range(n):
            gp_ref, w_ref, m_ref, v_ref = ins[4 * k:4 * k + 4]
            _adam_write(_ordered_sum(gp_ref), w_ref, m_ref, v_ref, *outs[4 * k:4 * k + 4])

    in_specs, out_specs, out_shape = [], [], []
    for parts, w, _, _ in items:
        C = w.shape[1]
        spec = pl.BlockSpec((tr, C), lambda i: (i, 0))
        in_specs += [pl.BlockSpec((parts.shape[0], tr, C), lambda i: (0, i, 0)), spec, spec, spec]
        out_specs += [spec] * 4
        out_shape += [jax.ShapeDtypeStruct(w.shape, F32)] * 4
    res = pl.pallas_call(
        body, name=name, grid=(R // tr,), in_specs=in_specs, out_specs=tuple(out_specs), out_shape=tuple(out_shape),
        compiler_params=_params("arbitrary"),
    )(*[a for item in items for a in item])
    return [res[4 * k:4 * k + 4] for k in range(n)]


def _pack_rows(vectors, D):
    def body(*refs):
        out_ref = refs[-1]
        out_ref[...] = jnp.zeros_like(out_ref)
        for i, r in enumerate(refs[:-1]):
            out_ref[i:i + 1, :r.shape[1]] = r[...]

    vmem = pl.BlockSpec(memory_space=pltpu.VMEM)
    return pl.pallas_call(body, name="pack_small", in_specs=[vmem] * len(vectors), out_specs=vmem,
                          out_shape=jax.ShapeDtypeStruct((8, D), F32))(*vectors)


def _adamw_small(row_parts, mat_parts, vectors, matrix):
    n = len(vectors)

    def body(*refs):
        rp_ref, mp_ref = refs[0], refs[1]
        ins, outs = refs[2:2 + 3 * (n + 1)], refs[2 + 3 * (n + 1):]
        rows = _ordered_sum(rp_ref)
        for i in range(n):
            c = ins[3 * i].shape[1]
            _adam_write(rows[i:i + 1, :c], *ins[3 * i:3 * i + 3], *outs[4 * i:4 * i + 4])
        _adam_write(_ordered_sum(mp_ref), *ins[3 * n:3 * n + 3], *outs[4 * n:4 * n + 4])
        outs[-1][...] = rows[n:n + 1, :]

    flat = [a for wmv in vectors for a in wmv] + list(matrix)
    out_shape = [jax.ShapeDtypeStruct(wmv[0].shape, F32) for wmv in list(vectors) + [matrix] for _ in range(4)]
    out_shape.append(jax.ShapeDtypeStruct((1, row_parts.shape[2]), F32))
    vmem = pl.BlockSpec(memory_space=pltpu.VMEM)
    res = pl.pallas_call(body, name="adamw_small", in_specs=[vmem] * (2 + len(flat)),
                         out_specs=tuple([vmem] * len(out_shape)), out_shape=tuple(out_shape))(row_parts, mat_parts, *flat)
    return [res[4 * i:4 * i + 4] for i in range(n + 1)], res[-1]


def kernel(x, mem, ffn1_norm, ffn1_w_gate, ffn1_w_up, ffn1_w_down, mix_norm, w_in, w_pool, pool_scale, w_out, mem_q_norm, mem_kv_norm, mem_w_q, mem_w_kv, mem_w_o, ffn2_norm, ffn2_w_gate, ffn2_w_up, ffn2_w_down, final_norm, loss_target, m_ffn1_norm, m_ffn1_w_gate, m_ffn1_w_up, m_ffn1_w_down, m_mix_norm, m_w_in, m_w_pool, m_pool_scale, m_w_out, m_mem_q_norm, m_mem_kv_norm, m_mem_w_q, m_mem_w_kv, m_mem_w_o, m_ffn2_norm, m_ffn2_w_gate, m_ffn2_w_up, m_ffn2_w_down, m_final_norm, v_ffn1_norm, v_ffn1_w_gate, v_ffn1_w_up, v_ffn1_w_down, v_mix_norm, v_w_in, v_w_pool, v_pool_scale, v_w_out, v_mem_q_norm, v_mem_kv_norm, v_mem_w_q, v_mem_w_kv, v_mem_w_o, v_ffn2_norm, v_ffn2_w_gate, v_ffn2_w_up, v_ffn2_w_down, v_final_norm):
    B, S, D = x.shape
    T = B * S
    x0 = x.reshape(T, D)
    target = loss_target.reshape(T, D)
    final_gain = final_norm.reshape(1, D)

    big = dict(
        g1=ffn1_w_gate[0].T, u1=ffn1_w_up[0].T, d1=ffn1_w_down[0],
        g2=ffn2_w_gate[0].T, u2=ffn2_w_up[0].T, d2=ffn2_w_down[0],
        w_in=w_in[0].T, w_out=w_out[0], w_q=mem_w_q[0], w_kv=mem_w_kv[0].T, w_o=mem_w_o[0])
    names = list(big)
    shard = {k: big[k].astype(BF16) for k in names}
    wp = w_pool[0].astype(BF16)
    full, ffn_w = {}, {}
    stacked = ("g1", "u1", "d1", "g2", "u2", "d2", "w_in", "w_kv")

    def gathered(keys, arrs):
        full.update(zip(keys, arrs))
        ffn_w.update({k: full[k].reshape(-1, D) for k in keys if k in stacked})

    first, mid = ("g1", "u1", "d1"), ("w_in", "w_out", "w_q", "w_kv", "w_o")
    gathered(first, _gather_two_level([shard[k] for k in first], "gather_ffn1"))
    (x1, hn1, a1, s1, t1), got = _ffn_fwd(x0, ffn1_norm, ffn_w["g1"], ffn_w["u1"], ffn_w["d1"], "ffn1_fwd",
                                          comm=([shard[k] for k in mid], True))
    gathered(mid, got)
    hn2, qkv, u = _mix_in_fwd(x1, mix_norm, ffn_w["w_in"])
    qkv3 = qkv.reshape(B, S, -1)
    (o_sb,), got = _sb_fwd(qkv3, B, S, comm=([shard["g2"], shard["u2"]], True))
    gathered(("g2", "u2"), got)
    pooled = _pool_fwd(u.reshape(B, S, -1), B, S).reshape(T, -1)
    x2, mixed = _mix_out_fwd(x1, o_sb.reshape(T, -1), pooled, wp, pool_scale, full["w_out"])
    memn, kv = _mem_kv_fwd(mem, mem_kv_norm, ffn_w["w_kv"])
    (x3, hq, q, ocat), got = _cross_fwd(x2, mem_q_norm, kv, full["w_q"], full["w_o"], B, S,
                                        comm=([shard["d2"]], True))
    gathered(("d2",), got)
    (dx4, hn4, a2, s2, t2, d_final, loss_part), _ = _ffn_fwd(x3, ffn2_norm, ffn_w["g2"], ffn_w["u2"], ffn_w["d2"],
                                                            "ffn2_fwd", head=(final_gain, target))

    slab = lambda k: grads[k].reshape((N_DEV, -1) + grads[k].shape[-1:])
    got = {}
    dx3, dg2, du2, dyh2, d_ffn2 = _ffn_bwd(dx4, x3, ffn2_norm, s2, t2, ffn_w["g2"], ffn_w["u2"],
                                          ffn_w["d2"], "ffn2_bwd")
    ffn_slab = ffn1_w_gate.shape[2]
    grads = dict(g2=_wgrad(hn4, dg2, "dw_gate2", col_slab=ffn_slab), u2=_wgrad(hn4, du2, "dw_up2", col_slab=ffn_slab),
                 d2=_wgrad(a2, dyh2, "dw_down2"))
    (dx2, dx3b, dqb, dkv, d_q), (got["g2"],) = _cross_bwd(dx3, x2, mem_q_norm, q, kv, full["w_q"], full["w_o"], B, S,
                                                         comm=([slab("g2")], False))
    grads["w_o"] = _wgrad(ocat, dx3b, "dw_o")
    grads["w_q"] = _wgrad(hq, dqb, "dw_q")
    dkvb, d_kv = _mem_kv_bwd(dkv, mem, ffn_w["w_kv"])
    grads["w_kv"] = _wgrad(memn, dkvb, "dw_kv", col_slab=mem_w_kv.shape[2])
    dx2b, do_sb, dpooled, d_wpool, d_ps = _mix_out_bwd(dx2, pooled, wp, pool_scale, full["w_out"])
    grads["w_out"] = _wgrad(mixed, dx2b, "dw_out")
    du = _pool_bwd(dpooled.reshape(B, S, -1), B, S).reshape(T, -1)
    early = ("u2", "d2", "w_o", "w_q", "w_kv", "w_out")
    (dq, dk, dv), res = _sb_bwd(qkv3, do_sb.reshape(B, S, -1), B, S, comm=([slab(k) for k in early], False))
    got.update(zip(early, res))
    dx1, dproj, d_mix = _mix_in_bwd(dx2, dq.reshape(T, -1), dk.reshape(T, -1), dv.reshape(T, -1), du,
                                    x1, mix_norm, ffn_w["w_in"])
    grads["w_in"] = _wgrad(hn2, dproj, "dw_in", col_slab=w_in.shape[2])
    dx0, dg1, du1, dyh1, d_ffn1 = _ffn_bwd(dx1, x0, ffn1_norm, s1, t1, ffn_w["g1"], ffn_w["u1"],
                                          ffn_w["d1"], "ffn1_bwd")

    small = [("ffn1_norm", d_ffn1, ffn1_norm, m_ffn1_norm, v_ffn1_norm),
             ("mix_norm", d_mix, mix_norm, m_mix_norm, v_mix_norm),
             ("pool_scale", d_ps, pool_scale, m_pool_scale, v_pool_scale),
             ("mem_q_norm", d_q, mem_q_norm, m_mem_q_norm, v_mem_q_norm),
             ("mem_kv_norm", d_kv, mem_kv_norm, m_mem_kv_norm, v_mem_kv_norm),
             ("ffn2_norm", d_ffn2, ffn2_norm, m_ffn2_norm, v_ffn2_norm),
             ("final_norm", d_final, final_gain, m_final_norm.reshape(1, D), v_final_norm.reshape(1, D))]
    row_pack = _pack_rows([t[1] for t in small] + [loss_part], D)
    as_rows = lambda t: t.reshape(-1, LANES)
    grads["g1"], (row_parts, pool_parts, got["w_in"]) = _wgrad(
        hn1, dg1, "dw_gate1", col_slab=ffn_slab,
        comm=([row_pack, as_rows(d_wpool), slab("w_in")], [True, True, False]))
    grads["u1"], (got["g1"],) = _wgrad(hn1, du1, "dw_up1", col_slab=ffn_slab, comm=([slab("g1")], PAIRSUM))
    grads["d1"], (got["u1"],) = _wgrad(a1, dyh1, "dw_down1", comm=([slab("u1")], PAIRSUM))
    got["d1"] = _pairsum_exchange([slab("d1")], "scatter_last")[0]

    state = dict(
        g1=(ffn1_w_gate, m_ffn1_w_gate, v_ffn1_w_gate), u1=(ffn1_w_up, m_ffn1_w_up, v_ffn1_w_up),
        d1=(ffn1_w_down, m_ffn1_w_down, v_ffn1_w_down), g2=(ffn2_w_gate, m_ffn2_w_gate, v_ffn2_w_gate),
        u2=(ffn2_w_up, m_ffn2_w_up, v_ffn2_w_up), d2=(ffn2_w_down, m_ffn2_w_down, v_ffn2_w_down),
        w_in=(w_in, m_w_in, v_w_in), w_out=(w_out, m_w_out, v_w_out), w_q=(mem_w_q, m_mem_w_q, v_mem_w_q),
        w_kv=(mem_w_kv, m_mem_w_kv, v_mem_w_kv), w_o=(mem_w_o, m_mem_w_o, v_mem_w_o))
    big_out = {}
    groups = dict(gate_up=("g1", "u1", "g2", "u2"), in_kv=("w_in", "w_kv"), square=("w_out", "w_q", "w_o"),
                  down=("d1", "d2"))
    for label, keys in groups.items():
        items = [(got[k],) + tuple(t[0] for t in state[k]) for k in keys]
        for k, outs in zip(keys, _adamw(items, "adamw_" + label)):
            big_out[k] = [t[None] for t in outs]

    small_res, loss_row = _adamw_small(row_parts, pool_parts, [t[2:] for t in small],
                                       [as_rows(t) for t in (w_pool, m_w_pool, v_w_pool)])
    small_out = {t[0]: small_res[i] for i, t in enumerate(small)}
    small_out["final_norm"] = [t.reshape(D) for t in small_out["final_norm"]]
    small_out["w_pool"] = [t.reshape(w_pool.shape) for t in small_res[-1]]
    loss = loss_row[0, 0]

    order = [("ffn1_norm", None), ("ffn1_w_gate", "g1"), ("ffn1_w_up", "u1"), ("ffn1_w_down", "d1"),
             ("mix_norm", None), ("w_in", "w_in"), ("w_pool", None), ("pool_scale", None), ("w_out", "w_out"),
             ("mem_q_norm", None), ("mem_kv_norm", None), ("mem_w_q", "w_q"), ("mem_w_kv", "w_kv"),
             ("mem_w_o", "w_o"), ("ffn2_norm", None), ("ffn2_w_gate", "g2"), ("ffn2_w_up", "u2"),
             ("ffn2_w_down", "d2"), ("final_norm", None)]
    res = [loss, dx0.reshape(B, S, D)]
    for which in range(4):
        for name, key in order:
            res.append(big_out[key][which] if key else small_out[name][which])
    return tuple(res)
```

```python
import functools
import math

import jax
import jax.numpy as jnp
from jax import lax
from jax.experimental import pallas as pl
from jax.experimental.pallas import tpu as pltpu

F32 = jnp.float32
BF16 = jnp.bfloat16

N_DEV = 8
EPS = 1e-6
SB_HEAD_DIM = 64
LANES = 128
POOL_WINDOWS = (2, 4, 8, 16)
POOL_GROUP_DIM = 128
MEM_HEADS = 4
FFN_RESIDUAL_WEIGHT = 0.5
ADAM_LR = 0.001
ADAM_B1 = 0.9
ADAM_B2 = 0.999
ADAM_EPS = 1e-08
ADAM_WD = 0.01
ADAM_STEP = 10
VMEM_LIMIT = 56 * 1024 * 1024

MESH_ID = pl.DeviceIdType.MESH


def _params(*sem):
    return pltpu.CompilerParams(dimension_semantics=sem, vmem_limit_bytes=VMEM_LIMIT)


def _tile(n, pref):
    if n <= pref:
        return n
    t = pref - pref % 8
    while n % t:
        t -= 8
    return t


def _mm(a, b):
    return jnp.dot(a, b, preferred_element_type=F32)


def _mm_nt(a, b):
    return lax.dot_general(a, b, (((1,), (1,)), ((), ())), preferred_element_type=F32)


def _mm_tn(a, b):
    return lax.dot_general(a, b, (((0,), (0,)), ((), ())), preferred_element_type=F32)


def _rms(xv):
    r = lax.rsqrt(jnp.mean(xv * xv, axis=-1, keepdims=True) + EPS)
    return r, xv * r


def _rms_bwd(dhn, gain, r, xhat):
    dxh = dhn * gain
    return r * (dxh - xhat * jnp.mean(dxh * xhat, axis=-1, keepdims=True))


def _sigmoid(z):
    return 0.5 * jnp.tanh(0.5 * z) + 0.5


def _flags(arrs, gather):
    return [gather] * len(arrs) if isinstance(gather, bool) else list(gather)


def _comm_shapes(arrs, gather):
    return tuple(jax.ShapeDtypeStruct(((N_DEV,) + tuple(a.shape)) if f else tuple(a.shape), a.dtype)
                 for a, f in zip(arrs, _flags(arrs, gather)))


def _comm_start(ins, outs, sems, gather):
    send_sems, recv_sems, local_sems = sems
    gather = _flags(ins, gather)
    x, y, c = lax.axis_index("x"), lax.axis_index("y"), lax.axis_index("c")
    me = 4 * x + 2 * y + c
    for i in range(len(ins)):
        src = ins[i] if gather[i] else ins[i].at[me]
        pltpu.make_async_copy(src, outs[i].at[me], local_sems.at[i]).start()
    for k in range(1, N_DEV):
        px = 1 - x if k & 4 else x
        py = 1 - y if k & 2 else y
        pc = 1 - c if k & 1 else c
        peer = 4 * px + 2 * py + pc
        for i in range(len(ins)):
            src = ins[i] if gather[i] else ins[i].at[peer]
            pltpu.make_async_remote_copy(
                src_ref=src, dst_ref=outs[i].at[me],
                send_sem=send_sems.at[i], recv_sem=recv_sems.at[i],
                device_id=(px, py, pc), device_id_type=MESH_ID).start()


def _comm_wait(ins, outs, sems, gather):
    send_sems, recv_sems, local_sems = sems
    gather = _flags(ins, gather)
    x, y, c = lax.axis_index("x"), lax.axis_index("y"), lax.axis_index("c")
    me = 4 * x + 2 * y + c
    for i in range(len(ins)):
        seven = outs[i].at[pl.ds(0, N_DEV - 1)]
        done = pltpu.make_async_remote_copy(
            src_ref=seven, dst_ref=seven,
            send_sem=send_sems.at[i], recv_sem=recv_sems.at[i],
            device_id=(x, y, c), device_id_type=MESH_ID)
        done.wait_send()
        done.wait_recv()
        src = ins[i] if gather[i] else ins[i].at[me]
        pltpu.make_async_copy(src, outs[i].at[me], local_sems.at[i]).wait()


def _comm_sems(n):
    return [pltpu.SemaphoreType.DMA((n,)) for _ in range(3)]


def _exchange(arrs, gather, name):
    n = len(arrs)

    def body(*refs):
        ins, outs, sems = refs[:n], refs[n:2 * n], refs[2 * n:]
        _comm_start(ins, outs, sems, gather)
        _comm_wait(ins, outs, sems, gather)

    any_spec = pl.BlockSpec(memory_space=pl.ANY)
    outs = pl.pallas_call(
        body, name=name, out_shape=_comm_shapes(arrs, gather),
        in_specs=[any_spec] * n, out_specs=tuple([any_spec] * n), scratch_shapes=_comm_sems(n),
    )(*arrs)
    return list(outs)


GATHER2 = "gather2"


def _gather2_copies(ins, outs, sems):
    n = len(ins)
    send_sems, recv_sems, local_sems = sems
    x, y, c = lax.axis_index("x"), lax.axis_index("y"), lax.axis_index("c")
    me, sibling = (x, y, c), (x, y, 1 - c)
    chips = [(1 - x, y), (x, 1 - y), (1 - x, 1 - y)]

    def copy(i, k, block, to, own=False):
        slab = outs[i].at[4 * block[0] + 2 * block[1] + block[2]]
        return pltpu.make_async_remote_copy(
            src_ref=ins[i] if own else slab, dst_ref=slab,
            send_sem=send_sems.at[i, k], recv_sem=recv_sems.at[i, k],
            device_id=to, device_id_type=MESH_ID)

    each = [(j, chip, i) for j, chip in enumerate(chips) for i in range(n)]
    return dict(
        mine=lambda: [pltpu.make_async_copy(ins[i], outs[i].at[4 * x + 2 * y + c], local_sems.at[i]) for i in range(n)],
        first=lambda: [copy(i, 0, me, sibling, own=True) for i in range(n)]
        + [copy(i, 1 + j, me, (*chip, c), own=True) for j, chip, i in each],
        landed=lambda: [copy(i, 1 + j, (*chip, c), me) for j, chip, i in each],
        passed=lambda: [copy(i, 4 + j, (*chip, c), sibling) for j, chip, i in each],
        from_sibling=lambda: [copy(i, 0, sibling, me) for i in range(n)]
        + [copy(i, 4 + j, (*chip, 1 - c), me) for j, chip, i in each])


def _gather2_start(ins, outs, sems):
    make = _gather2_copies(ins, outs, sems)
    for cp in make["mine"]() + make["first"]():
        cp.start()


def _gather2_middle(ins, outs, sems):
    make = _gather2_copies(ins, outs, sems)
    for arrival, onward in zip(make["landed"](), make["passed"]()):
        arrival.wait_recv()
        onward.start()


def _gather2_wait(ins, outs, sems):
    make = _gather2_copies(ins, outs, sems)
    for cp in make["from_sibling"]():
        cp.wait_recv()
    for cp in make["first"]() + make["passed"]():
        cp.wait_send()
    for cp in make["mine"]():
        cp.wait()


def _gather2_sems(n):
    return [pltpu.SemaphoreType.DMA((n, N_DEV - 1)), pltpu.SemaphoreType.DMA((n, N_DEV - 1)),
            pltpu.SemaphoreType.DMA((n,))]


def _gather_two_level(arrs, name):
    n = len(arrs)

    def body(*refs):
        ins, outs, sems = refs[:n], refs[n:2 * n], refs[2 * n:]
        _gather2_start(ins, outs, sems)
        _gather2_middle(ins, outs, sems)
        _gather2_wait(ins, outs, sems)

    any_spec = pl.BlockSpec(memory_space=pl.ANY)
    outs = pl.pallas_call(
        body, name=name, out_shape=_comm_shapes(arrs, True),
        in_specs=[any_spec] * n, out_specs=tuple([any_spec] * n), scratch_shapes=_gather2_sems(n),
    )(*arrs)
    return list(outs)


CHIPS = N_DEV // 2
PAIRSUM = "pairsum"


def _pairsum_shapes(arrs):
    return tuple(jax.ShapeDtypeStruct((CHIPS,) + tuple(a.shape[1:]), a.dtype) for a in arrs)


def _pairsum_scratch(arrs):
    bufs = [pltpu.VMEM((CHIPS,) + tuple(a.shape[1:]), a.dtype) for a in arrs for _ in range(3)]
    return bufs + [pltpu.SemaphoreType.DMA((len(arrs),)) for _ in range(6)]


def _pairsum_start(ins, outs, sc):
    n = len(ins)
    load, d2d_send, d2d_recv = sc[3 * n], sc[3 * n + 1], sc[3 * n + 2]
    x, y, c = lax.axis_index("x"), lax.axis_index("y"), lax.axis_index("c")
    for i in range(n):
        mine, theirs = sc[3 * i], sc[3 * i + 1]
        for q in range(CHIPS):
            pltpu.make_async_copy(ins[i].at[2 * q + c], mine.at[q], load.at[i]).start()
            pltpu.make_async_remote_copy(
                src_ref=ins[i].at[2 * q + (1 - c)], dst_ref=theirs.at[q], send_sem=d2d_send.at[i],
                recv_sem=d2d_recv.at[i], device_id=(x, y, 1 - c), device_id_type=MESH_ID).start()


def _pairsum_middle(ins, outs, sc):
    n = len(ins)
    load, d2d_send, d2d_recv, ici_send, ici_recv, store = sc[3 * n:3 * n + 6]
    x, y, c = lax.axis_index("x"), lax.axis_index("y"), lax.axis_index("c")
    here = 2 * x + y
    for i in range(n):
        mine, theirs, total = sc[3 * i:3 * i + 3]
        pltpu.make_async_copy(mine, mine, load.at[i]).wait()
        pltpu.make_async_remote_copy(src_ref=theirs, dst_ref=theirs, send_sem=d2d_send.at[i], recv_sem=d2d_recv.at[i],
                                     device_id=(x, y, c), device_id_type=MESH_ID).wait_recv()
        for q in range(CHIPS):
            total[q] = (mine[q].astype(F32) + theirs[q].astype(F32)).astype(total.dtype)
        for k in range(1, CHIPS):
            px = 1 - x if k & 2 else x
            py = 1 - y if k & 1 else y
            pltpu.make_async_remote_copy(
                src_ref=total.at[2 * px + py], dst_ref=outs[i].at[here], send_sem=ici_send.at[i],
                recv_sem=ici_recv.at[i], device_id=(px, py, c), device_id_type=MESH_ID).start()
        pltpu.make_async_copy(total.at[here], outs[i].at[here], store.at[i]).start()


def _pairsum_wait(ins, outs, sc):
    n = len(ins)
    load, d2d_send, d2d_recv, ici_send, ici_recv, store = sc[3 * n:3 * n + 6]
    x, y, c = lax.axis_index("x"), lax.axis_index("y"), lax.axis_index("c")
    here = 2 * x + y
    for i in range(n):
        theirs, total = sc[3 * i + 1], sc[3 * i + 2]
        three = outs[i].at[pl.ds(0, CHIPS - 1)]
        pltpu.make_async_remote_copy(src_ref=theirs, dst_ref=theirs, send_sem=d2d_send.at[i], recv_sem=d2d_recv.at[i],
                                     device_id=(x, y, c), device_id_type=MESH_ID).wait_send()
        done = pltpu.make_async_remote_copy(src_ref=three, dst_ref=three, send_sem=ici_send.at[i],
                                            recv_sem=ici_recv.at[i], device_id=(x, y, c), device_id_type=MESH_ID)
        done.wait_send()
        done.wait_recv()
        pltpu.make_async_copy(total.at[here], outs[i].at[here], store.at[i]).wait()


def _pairsum_exchange(arrs, name):
    n = len(arrs)

    def body(*refs):
        ins, outs, sc = refs[:n], refs[n:2 * n], refs[2 * n:]
        _pairsum_start(ins, outs, sc)
        _pairsum_middle(ins, outs, sc)
        _pairsum_wait(ins, outs, sc)

    any_spec = pl.BlockSpec(memory_space=pl.ANY)
    outs = pl.pallas_call(
        body, name=name, out_shape=_pairsum_shapes(arrs), in_specs=[any_spec] * n, out_specs=tuple([any_spec] * n),
        scratch_shapes=_pairsum_scratch(arrs), compiler_params=pltpu.CompilerParams(vmem_limit_bytes=VMEM_LIMIT),
    )(*arrs)
    return list(outs)


def _call(body, *, name, grid, in_specs, out_specs, out_shape, args, scratch=(), comm=None):
    sem = ("arbitrary",) * len(grid)
    if comm is None:
        res = pl.pallas_call(body, name=name, grid=grid, in_specs=list(in_specs), out_specs=tuple(out_specs),
                             out_shape=tuple(out_shape), scratch_shapes=list(scratch),
                             compiler_params=_params(*sem))(*args)
        return tuple(res), []
    arrs, gather = comm
    n, n_in, n_out, n_sc = len(arrs), len(args), len(out_shape), len(scratch)
    steps = math.prod(grid)
    if isinstance(gather, str) and gather == PAIRSUM:
        start, middle, wait = _pairsum_start, _pairsum_middle, _pairsum_wait
        shapes, extra, mid_step = _pairsum_shapes(arrs), _pairsum_scratch(arrs), steps // 4
    elif isinstance(gather, str) and gather == GATHER2:
        start, middle, wait = _gather2_start, _gather2_middle, _gather2_wait
        shapes, extra, mid_step = _comm_shapes(arrs, True), _gather2_sems(n), 3 * steps // 4
    else:
        start = functools.partial(_comm_start, gather=gather)
        wait = functools.partial(_comm_wait, gather=gather)
        middle, shapes, extra, mid_step = None, _comm_shapes(arrs, gather), _comm_sems(n), 0

    def wrapped(*refs):
        ins, cin = refs[:n_in], refs[n_in:n_in + n]
        outs, cout = refs[n_in + n:n_in + n + n_out], refs[n_in + n + n_out:n_in + 2 * n + n_out]
        sc, sems = refs[n_in + 2 * n + n_out:n_in + 2 * n + n_out + n_sc], refs[n_in + 2 * n + n_out + n_sc:]
        step = functools.reduce(lambda acc, ax: acc * grid[ax] + pl.program_id(ax), range(len(grid)), 0)

        @pl.when(step == 0)
        def _():
            start(cin, cout, sems)

        if middle:
            @pl.when(step == mid_step)
            def _():
                middle(cin, cout, sems)

        body(*ins, *outs, *sc)

        @pl.when(step == steps - 1)
        def _():
            wait(cin, cout, sems)

    any_spec = pl.BlockSpec(memory_space=pl.ANY)
    res = pl.pallas_call(
        wrapped, name=name, grid=grid, in_specs=list(in_specs) + [any_spec] * n,
        out_specs=tuple(out_specs) + (any_spec,) * n, out_shape=tuple(out_shape) + tuple(shapes),
        scratch_shapes=list(scratch) + list(extra), compiler_params=_params(*sem))(*args, *arrs)
    return tuple(res[:n_out]), list(res[n_out:])


FFN_BWD_ROWS = 256


def _load_resident(pairs, sem):
    copies = [pltpu.make_async_copy(src, dst, sem.at[k]) for k, (src, dst) in enumerate(pairs)]
    for cp in copies:
        cp.start()
    for cp in copies:
        cp.wait()


def _ffn_fwd(x, gain, wgt, wut, wd, name, comm=None, head=None):
    T, D = x.shape
    F = wd.shape[0]
    tm, tf = _tile(T, 512), _tile(F, 256)

    def body(*refs):
        if head:
            (x_ref, gain_ref, wg_hbm, wu_hbm, wd_hbm, fgain_ref, tgt_ref,
             out_ref, hn_ref, a_ref, s_ref, t_ref, dfgain_ref, loss_ref, wg_s, wu_s, wd_s, sem) = refs
        else:
            (x_ref, gain_ref, wg_hbm, wu_hbm, wd_hbm,
             out_ref, hn_ref, a_ref, s_ref, t_ref, wg_s, wu_s, wd_s, sem) = refs

        @pl.when(pl.program_id(0) == 0)
        def _():
            _load_resident([(wg_hbm, wg_s), (wu_hbm, wu_s), (wd_hbm, wd_s)], sem)
            if head:
                dfgain_ref[...] = jnp.zeros_like(dfgain_ref)
                loss_ref[...] = jnp.zeros_like(loss_ref)

        _, xhat = _rms(x_ref[...])
        hn = (xhat * gain_ref[...]).astype(BF16)
        hn_ref[...] = hn
        for f0 in range(0, F, tf):
            cols = slice(f0, f0 + tf)
            g = _mm_nt(hn, wg_s[cols, :])
            u = _mm_nt(hn, wu_s[cols, :])
            sig = _sigmoid(g)
            s = g * sig
            a_ref[:, cols] = (s * u).astype(BF16)
            s_ref[:, cols] = s.astype(BF16)
            t_ref[:, cols] = (u * (sig + s * (1.0 - sig))).astype(BF16)
        y = x_ref[...] + FFN_RESIDUAL_WEIGHT * _mm(a_ref[...], wd_s[...])
        if head:
            r, yhat = _rms(y)
            err = yhat * fgain_ref[...] - tgt_ref[...]
            loss_ref[...] += 0.5 * jnp.sum(jnp.mean(err * err, axis=-1, keepdims=True), axis=0, keepdims=True)
            dy = err * (1.0 / D)
            dfgain_ref[...] += jnp.sum(dy * yhat, axis=0, keepdims=True)
            out_ref[...] = _rms_bwd(dy, fgain_ref[...], r, yhat)
        else:
            out_ref[...] = y

    row = lambda i: (i, 0)
    one = pl.BlockSpec((1, D), lambda i: (0, 0))
    hbm = pl.BlockSpec(memory_space=pl.ANY)
    in_specs = [pl.BlockSpec((tm, D), row), one, hbm, hbm, hbm]
    out_specs = [pl.BlockSpec((tm, D), row), pl.BlockSpec((tm, D), row)] + [pl.BlockSpec((tm, F), row) for _ in range(3)]
    out_shape = [jax.ShapeDtypeStruct((T, D), F32), jax.ShapeDtypeStruct((T, D), BF16)] \
        + [jax.ShapeDtypeStruct((T, F), BF16) for _ in range(3)]
    args = (x, gain, wgt, wut, wd)
    if head:
        in_specs += [one, pl.BlockSpec((tm, D), row)]
        out_specs += [one, one]
        out_shape += [jax.ShapeDtypeStruct((1, D), F32), jax.ShapeDtypeStruct((1, D), F32)]
        args += tuple(head)
    return _call(
        body, name=name, grid=(T // tm,), comm=comm, in_specs=in_specs, out_specs=out_specs, out_shape=out_shape,
        scratch=[pltpu.VMEM((F, D), BF16) for _ in range(3)] + [pltpu.SemaphoreType.DMA((3,))], args=args)


def _ffn_bwd(dy, x, gain, s, t, wgt, wut, wd, name):
    T, D = x.shape
    F = wd.shape[0]
    tr, tf = _tile(T, FFN_BWD_ROWS), _tile(F, 256)
    rows = lambda i: (i, 0)
    one = pl.BlockSpec((1, D), lambda i: (0, 0))
    any_spec = pl.BlockSpec(memory_space=pl.ANY)

    def body(dy_ref, x_ref, gain_ref, s_ref, t_ref, wg_hbm, wu_hbm, wd_hbm,
             dx_ref, dg_ref, du_ref, dyh_ref, dgain_ref, wg_s, wu_s, wd_s, sem):
        @pl.when(pl.program_id(0) == 0)
        def _():
            _load_resident([(wg_hbm, wg_s), (wu_hbm, wu_s), (wd_hbm, wd_s)], sem)
            dgain_ref[...] = jnp.zeros_like(dgain_ref)

        dyh = (FFN_RESIDUAL_WEIGHT * dy_ref[...]).astype(BF16)
        dyh_ref[...] = dyh
        for f0 in range(0, F, tf):
            cols = slice(f0, f0 + tf)
            da = _mm_nt(dyh, wd_s[cols, :])
            dg_ref[:, cols] = (da * t_ref[:, cols].astype(F32)).astype(BF16)
            du_ref[:, cols] = (da * s_ref[:, cols].astype(F32)).astype(BF16)
        dhn = _mm(dg_ref[...], wg_s[...]) + _mm(du_ref[...], wu_s[...])
        r, xhat = _rms(x_ref[...])
        dgain_ref[...] += jnp.sum(dhn * xhat, axis=0, keepdims=True)
        dx_ref[...] = dy_ref[...] + _rms_bwd(dhn, gain_ref[...], r, xhat)

    wide = jax.ShapeDtypeStruct((T, F), BF16)
    return pl.pallas_call(
        body, name=name, grid=(T // tr,),
        in_specs=[pl.BlockSpec((tr, D), rows), pl.BlockSpec((tr, D), rows), one, pl.BlockSpec((tr, F), rows),
                  pl.BlockSpec((tr, F), rows), any_spec, any_spec, any_spec],
        out_specs=(pl.BlockSpec((tr, D), rows), pl.BlockSpec((tr, F), rows), pl.BlockSpec((tr, F), rows),
                   pl.BlockSpec((tr, D), rows), one),
        out_shape=(jax.ShapeDtypeStruct((T, D), F32), wide, wide, jax.ShapeDtypeStruct((T, D), BF16),
                   jax.ShapeDtypeStruct((1, D), F32)),
        scratch_shapes=[pltpu.VMEM((F, D), BF16) for _ in range(3)] + [pltpu.SemaphoreType.DMA((3,))],
        compiler_params=_params("arbitrary"),
    )(dy, x, gain, s, t, wgt, wut, wd)


def _wgrad(a, b, name, col_slab=None, comm=None):
    T, M = a.shape
    N = b.shape[1]
    tmm = M if M <= 1024 else _tile(M, 1408)
    tn = _tile(N, 1024)
    if col_slab and col_slab % LANES:
        tn = col_slab * LANES // math.gcd(col_slab, LANES)
    tk = _tile(T, 2048)
    nk = T // tk
    per = tn // col_slab if col_slab else 0

    def body(a_ref, b_ref, out_ref, acc):
        k = pl.program_id(2)

        @pl.when(k == 0)
        def _():
            acc[...] = jnp.zeros_like(acc)

        acc[...] += _mm_tn(a_ref[...], b_ref[...])

        @pl.when(k == nk - 1)
        def _():
            if col_slab:
                for s in range(per):
                    out_ref[s] = acc[:, s * col_slab:(s + 1) * col_slab].astype(BF16)
            else:
                out_ref[...] = acc[...].astype(BF16)

    if col_slab:
        out_spec = pl.BlockSpec((per, tmm, col_slab), lambda m, n, k: (n, m, 0))
        out_shape = jax.ShapeDtypeStruct((N // col_slab, M, col_slab), BF16)
    else:
        out_spec = pl.BlockSpec((tmm, tn), lambda m, n, k: (m, n))
        out_shape = jax.ShapeDtypeStruct((M, N), BF16)
    (out,), got = _call(
        body, name=name, grid=(M // tmm, N // tn, nk), comm=comm,
        in_specs=[pl.BlockSpec((tk, tmm), lambda m, n, k: (k, m)), pl.BlockSpec((tk, tn), lambda m, n, k: (k, n))],
        out_specs=(out_spec,), out_shape=(out_shape,),
        scratch=[pltpu.VMEM((tmm, tn), F32)], args=(a, b))
    return (out, got) if comm else out


def _mix_in_fwd(x, gain, w_int):
    T, D = x.shape
    C = w_int.shape[0]
    n_qkv = 3 * C // 4
    tm = _tile(T, 1024)

    def body(x_ref, gain_ref, w_ref, hn_ref, qkv_ref, u_ref):
        _, xhat = _rms(x_ref[...])
        hn = (xhat * gain_ref[...]).astype(BF16)
        hn_ref[...] = hn
        proj = _mm_nt(hn, w_ref[...])
        qkv_ref[...] = proj[:, :n_qkv].astype(BF16)
        u_ref[...] = proj[:, n_qkv:]

    row = lambda i: (i, 0)
    return pl.pallas_call(
        body, name="mix_in_fwd", grid=(T // tm,),
        in_specs=[pl.BlockSpec((tm, D), row), pl.BlockSpec((1, D), lambda i: (0, 0)),
                  pl.BlockSpec((C, D), lambda i: (0, 0))],
        out_specs=(pl.BlockSpec((tm, D), row), pl.BlockSpec((tm, n_qkv), row), pl.BlockSpec((tm, C - n_qkv), row)),
        out_shape=(jax.ShapeDtypeStruct((T, D), BF16), jax.ShapeDtypeStruct((T, n_qkv), BF16),
                   jax.ShapeDtypeStruct((T, C - n_qkv), F32)),
        compiler_params=_params("arbitrary"),
    )(x, gain, w_int)


def _mix_in_bwd(dres, dq, dk, dv, du, x, gain, w_int):
    T, D = x.shape
    C = w_int.shape[0]
    W = dq.shape[1]
    tm = _tile(T, 512)

    def body(dres_ref, dq_ref, dk_ref, dv_ref, du_ref, x_ref, gain_ref, w_ref, dx_ref, dproj_ref, dgain_ref):
        @pl.when(pl.program_id(0) == 0)
        def _():
            dgain_ref[...] = jnp.zeros_like(dgain_ref)

        for part, ref in enumerate((dq_ref, dk_ref, dv_ref, du_ref)):
            dproj_ref[:, part * W:(part + 1) * W] = ref[...]
        dhn = _mm(dproj_ref[...], w_ref[...])
        r, xhat = _rms(x_ref[...])
        dgain_ref[...] += jnp.sum(dhn * xhat, axis=0, keepdims=True)
        dx_ref[...] = dres_ref[...] + _rms_bwd(dhn, gain_ref[...], r, xhat)

    row = lambda i: (i, 0)
    one = pl.BlockSpec((1, D), lambda i: (0, 0))
    part = pl.BlockSpec((tm, W), row)
    return pl.pallas_call(
        body, name="mix_in_bwd", grid=(T // tm,),
        in_specs=[pl.BlockSpec((tm, D), row), part, part, part, part, pl.BlockSpec((tm, D), row), one,
                  pl.BlockSpec((C, D), lambda i: (0, 0))],
        out_specs=(pl.BlockSpec((tm, D), row), pl.BlockSpec((tm, C), row), one),
        out_shape=(jax.ShapeDtypeStruct((T, D), F32), jax.ShapeDtypeStruct((T, C), BF16),
                   jax.ShapeDtypeStruct((1, D), F32)),
        compiler_params=_params("arbitrary"),
    )(dres, dq, dk, dv, du, x, gain, w_int)


SB_PAIRS_PER_PROGRAM = 2
SB_FWD_PAIRS_PER_PROGRAM = 4
LOG2_E = 1.4426950408889634
EXP2_CLAMP = 126.0


def _neg_log2_sigmoid(nz2):
    w = jnp.minimum(nz2, EXP2_CLAMP)
    return w, jnp.log2(1.0 + jnp.exp2(w))


SB_DEAD_LOG2 = -160.0


def _sb_live(rests):
    worst = functools.reduce(jnp.maximum, rests)
    return (jnp.max(worst) > SB_DEAD_LOG2).astype(jnp.int32)


def _split(v):
    hi = v.astype(BF16)
    return hi, (v - hi.astype(F32)).astype(BF16)


def _sb_fwd(qkv, B, S, comm=None):
    W = qkv.shape[2] // 3
    n_pair = W // LANES
    bq = _tile(S, 256)
    nq = S // bq
    hp = SB_FWD_PAIRS_PER_PROGRAM
    nscale2 = -(SB_HEAD_DIM ** -0.5) * LOG2_E

    def body(q_ref, k_ref, v_ref, o_ref):
        lane = lax.broadcasted_iota(jnp.int32, (1, LANES), 1)
        head0 = lane < SB_HEAD_DIM
        rr = lax.broadcasted_iota(jnp.int32, (bq, bq), 0)
        cc = lax.broadcasted_iota(jnp.int32, (bq, bq), 1)
        strict = cc < rr
        after = jnp.where(rr > cc, 1.0, 0.0).astype(BF16)

        def blocks(heads, ks, carries, diag):
            n = range(len(heads))
            keep = (lambda t: jnp.where(strict, t, 0.0)) if diag else (lambda t: t)
            z = [_mm_nt(qh, k_ref[ks, cols]) for qh, cols in heads]
            wl = [_neg_log2_sigmoid(z[h] * nscale2) for h in n]
            lr = [keep(wl[h][0] - wl[h][1]) for h in n]
            parts = [_split(lr[h]) for h in n]
            suf = [_mm(parts[h][0], after) + _mm(parts[h][1], after) for h in n]
            a = [keep(jnp.exp2(suf[h] + carries[h][1] - wl[h][1])).astype(BF16) for h in n]
            o = [carries[h][0] + _mm(a[h], v_ref[ks, heads[h][1]]) for h in n]
            return tuple((o[h], carries[h][1] + (suf[h][:, :1] + lr[h][:, :1])) for h in n)

        def q_tile(i, _):
            qs = pl.ds(pl.multiple_of(i * bq, bq), bq)
            heads = []
            for pr in range(hp):
                cols = slice(pr * LANES, (pr + 1) * LANES)
                qv = q_ref[qs, cols]
                heads += [(jnp.where(head0, qv, jnp.zeros_like(qv)), cols),
                          (jnp.where(head0, jnp.zeros_like(qv), qv), cols)]
            zero = (jnp.zeros((bq, LANES), F32), jnp.zeros((bq, 1), F32))
            init = blocks(heads, qs, (zero,) * len(heads), True)

            def left(st):
                t, _, cr = st
                ks = pl.ds(pl.multiple_of((i - 1 - t) * bq, bq), bq)
                cr = blocks(heads, ks, cr, False)
                return t + 1, _sb_live([c for _, c in cr]), cr

            _, _, res = lax.while_loop(lambda st: jnp.logical_and(st[0] < i, st[1] > 0), left,
                                       (jnp.int32(0), _sb_live([c for _, c in init]), init))
            for pr in range(hp):
                o_ref[qs, heads[2 * pr][1]] = jnp.where(head0, res[2 * pr][0], res[2 * pr + 1][0]).astype(BF16)
            return 0

        lax.fori_loop(0, nq, q_tile, 0)

    def col(off):
        return pl.BlockSpec((None, S, hp * LANES), lambda b, p: (b, 0, off + p))

    n_pair //= hp
    return _call(
        body, name="sb_fwd", grid=(B, n_pair), comm=comm,
        in_specs=[col(0), col(n_pair), col(2 * n_pair)],
        out_specs=(col(0),),
        out_shape=(jax.ShapeDtypeStruct((B, S, W), BF16),),
        args=(qkv, qkv, qkv))


def _sb_bwd(qkv, do, B, S, comm=None):
    W = qkv.shape[2] // 3
    n_pair = W // LANES
    bq = _tile(S, 256)
    nq = S // bq
    hp = SB_PAIRS_PER_PROGRAM
    scale = SB_HEAD_DIM ** -0.5
    nscale2 = -scale * LOG2_E

    def body(q_ref, k_ref, v_ref, do_ref, dq_ref, dk_ref, dv_ref, dk_s, dv_s, e_s, sg_s, a_s):
        lane = lax.broadcasted_iota(jnp.int32, (1, LANES), 1)
        head0 = lane < SB_HEAD_DIM
        rr = lax.broadcasted_iota(jnp.int32, (bq, bq), 0)
        cc = lax.broadcasted_iota(jnp.int32, (bq, bq), 1)
        strict = cc < rr
        after = jnp.where(rr > cc, 1.0, 0.0).astype(BF16)
        before = jnp.where(rr < cc, 1.0, 0.0).astype(BF16)
        dk_s[...] = jnp.zeros_like(dk_s)
        dv_s[...] = jnp.zeros_like(dv_s)

        def weights(heads, ks, kb, rests, diag):
            n = range(len(heads))
            keep = (lambda t: jnp.where(strict, t, 0.0)) if diag else (lambda t: t)
            z = [_mm_nt(heads[h][0], k_ref[ks, heads[h][2]]) for h in n]
            da = [_mm_nt(heads[h][1], v_ref[ks, heads[h][2]]) for h in n]
            wl = [_neg_log2_sigmoid(z[h] * nscale2) for h in n]
            lr = [keep(wl[h][0] - wl[h][1]) for h in n]
            parts = [_split(lr[h]) for h in n]
            suf = [_mm(parts[h][0], after) + _mm(parts[h][1], after) for h in n]
            a = [keep(jnp.exp2(suf[h] + rests[h] - wl[h][1])) for h in n]
            for h in n:
                a_s[h * nq + kb] = a[h].astype(BF16)
                e_s[h * nq + kb] = a[h] * da[h]
                sg_s[h * nq + kb] = jnp.exp2(-wl[h][1])
            return tuple(rests[h] + (suf[h][:, :1] + lr[h][:, :1]) for h in n)

        def grads(heads, ks, kb, carries, diag):
            n = range(len(heads))
            keep = (lambda t: jnp.where(strict, t, 0.0)) if diag else (lambda t: t)
            e = [e_s[h * nq + kb] for h in n]
            parts = [_split(e[h]) for h in n]
            pex = [_mm(parts[h][0], before) + _mm(parts[h][1], before) for h in n]
            dz = [keep(e[h] - sg_s[h * nq + kb] * (e[h] + pex[h] + carries[h][1])).astype(BF16) for h in n]
            dq = [carries[h][0] + _mm(dz[h], k_ref[ks, heads[h][2]]) for h in n]
            for h in n:
                dk_s[ks, heads[h][2]] += _mm_tn(dz[h], heads[h][0])
                dv_s[ks, heads[h][2]] += _mm_tn(a_s[h * nq + kb], heads[h][1])
            return tuple((dq[h], carries[h][1] + (pex[h][:, bq - 1:] + e[h][:, bq - 1:])) for h in n)

        def q_tile(i, _):
            qs = pl.ds(pl.multiple_of(i * bq, bq), bq)
            heads = []
            for pr in range(hp):
                cols = slice(pr * LANES, (pr + 1) * LANES)
                qv, dov = q_ref[qs, cols], do_ref[qs, cols]
                zq, zd = jnp.zeros_like(qv), jnp.zeros_like(dov)
                heads += [(jnp.where(head0, qv, zq), jnp.where(head0, dov, zd), cols),
                          (jnp.where(head0, zq, qv), jnp.where(head0, zd, dov), cols)]
            key_block = lambda kb: pl.ds(pl.multiple_of(kb * bq, bq), bq)
            rests = weights(heads, qs, i, (jnp.zeros((bq, 1), F32),) * len(heads), True)

            def left(st):
                t, _, rs = st
                rs = weights(heads, key_block(i - 1 - t), i - 1 - t, rs, False)
                return t + 1, _sb_live(rs), rs

            n_left, _, _ = lax.while_loop(lambda st: jnp.logical_and(st[0] < i, st[1] > 0), left,
                                          (jnp.int32(0), _sb_live(rests), rests))
            zero = (jnp.zeros((bq, LANES), F32), jnp.zeros((bq, 1), F32))
            res = lax.fori_loop(0, n_left, lambda t, cr: grads(heads, key_block(i - n_left + t), i - n_left + t, cr, False),
                                (zero,) * len(heads))
            res = grads(heads, qs, i, res, True)
            for pr in range(hp):
                dq = jnp.where(head0, res[2 * pr][0], res[2 * pr + 1][0])
                dq_ref[qs, heads[2 * pr][2]] = (dq * scale).astype(BF16)
            return 0

        lax.fori_loop(0, nq, q_tile, 0)
        dk_ref[...] = (dk_s[...] * scale).astype(BF16)
        dv_ref[...] = dv_s[...].astype(BF16)

    def col(off):
        return pl.BlockSpec((None, S, hp * LANES), lambda b, p: (b, 0, off + p))

    n_pair //= hp
    shp = jax.ShapeDtypeStruct((B, S, W), BF16)
    slots = 2 * hp * nq
    return _call(
        body, name="sb_bwd", grid=(B, n_pair), comm=comm,
        in_specs=[col(0), col(n_pair), col(2 * n_pair), col(0)],
        out_specs=(col(0), col(0), col(0)),
        out_shape=(shp, shp, shp),
        scratch=[pltpu.VMEM((S, hp * LANES), F32), pltpu.VMEM((S, hp * LANES), F32),
                 pltpu.VMEM((slots, bq, bq), F32), pltpu.VMEM((slots, bq, bq), F32),
                 pltpu.VMEM((slots, bq, bq), BF16)],
        args=(qkv, qkv, qkv, do))


def _pool_counts(S):
    t = lax.broadcasted_iota(jnp.int32, (S, 1), 0)
    return t, [jnp.minimum(t + 1, w).astype(F32) for w in POOL_WINDOWS]


def _pool_fwd(u, B, S):
    W = u.shape[2]

    def body(u_ref, out_ref):
        t, counts = _pool_counts(S)
        for gi, w in enumerate(POOL_WINDOWS):
            cols = slice(gi * POOL_GROUP_DIM, (gi + 1) * POOL_GROUP_DIM)
            ug = u_ref[:, cols]
            s, k = ug, 1
            while k < w:
                s = s + jnp.where(t >= k, pltpu.roll(s, k, axis=0), 0.0)
                k *= 2
            out_ref[:, cols] = (s / counts[gi] - ug).astype(BF16)

    spec = pl.BlockSpec((None, S, W), lambda b: (b, 0, 0))
    return pl.pallas_call(
        body, name="pool_fwd", grid=(B,), in_specs=[spec], out_specs=spec,
        out_shape=jax.ShapeDtypeStruct((B, S, W), BF16), compiler_params=_params("arbitrary"),
    )(u)


def _pool_bwd(dpooled, B, S):
    W = dpooled.shape[2]

    def body(d_ref, out_ref):
        t, counts = _pool_counts(S)
        for gi, w in enumerate(POOL_WINDOWS):
            cols = slice(gi * POOL_GROUP_DIM, (gi + 1) * POOL_GROUP_DIM)
            d = d_ref[:, cols]
            s, k = d / counts[gi], 1
            while k < w:
                s = s + jnp.where(t < S - k, pltpu.roll(s, S - k, axis=0), 0.0)
                k *= 2
            out_ref[:, cols] = (s - d).astype(BF16)

    spec = pl.BlockSpec((None, S, W), lambda b: (b, 0, 0))
    return pl.pallas_call(
        body, name="pool_bwd", grid=(B,), in_specs=[spec], out_specs=spec,
        out_shape=jax.ShapeDtypeStruct((B, S, W), BF16), compiler_params=_params("arbitrary"),
    )(dpooled)


def _mix_out_fwd(x, o_sb, pooled, w_pool, pool_scale, w_out):
    T, D = x.shape
    W = o_sb.shape[1]
    G = w_pool.shape[0]
    gd = POOL_GROUP_DIM
    tm = _tile(T, 1024)

    def body(x_ref, osb_ref, pooled_ref, wp_ref, ps_ref, wo_ref, out_ref, mixed_ref):
        mixed_ref[:, :W] = osb_ref[...]
        for gi in range(G):
            cols = slice(gi * gd, (gi + 1) * gd)
            pw = _mm(pooled_ref[:, cols], wp_ref[gi])
            mixed_ref[:, W + gi * gd:W + (gi + 1) * gd] = (pw * ps_ref[:, cols]).astype(BF16)
        out_ref[...] = x_ref[...] + _mm(mixed_ref[...], wo_ref[...].reshape(D, D))

    row = lambda i: (i, 0)
    return pl.pallas_call(
        body, name="mix_out_fwd", grid=(T // tm,),
        in_specs=[pl.BlockSpec((tm, D), row), pl.BlockSpec((tm, W), row), pl.BlockSpec((tm, W), row),
                  pl.BlockSpec((G, gd, gd), lambda i: (0, 0, 0)), pl.BlockSpec((1, W), lambda i: (0, 0)),
                  pl.BlockSpec(w_out.shape, lambda i: (0, 0, 0))],
        out_specs=(pl.BlockSpec((tm, D), row), pl.BlockSpec((tm, D), row)),
        out_shape=(jax.ShapeDtypeStruct((T, D), F32), jax.ShapeDtypeStruct((T, D), BF16)),
        compiler_params=_params("arbitrary"),
    )(x, o_sb, pooled, w_pool, pool_scale, w_out)


def _mix_out_bwd(dx, pooled, w_pool, pool_scale, w_out):
    T, D = dx.shape
    W = pooled.shape[1]
    G = w_pool.shape[0]
    gd = POOL_GROUP_DIM
    tm = _tile(T, 1024)

    def body(dx_ref, pooled_ref, wp_ref, ps_ref, wo_ref, dxb_ref, dosb_ref, dpooled_ref, dwp_ref, dps_ref):
        i = pl.program_id(0)

        @pl.when(i == 0)
        def _():
            dwp_ref[...] = jnp.zeros_like(dwp_ref)
            dps_ref[...] = jnp.zeros_like(dps_ref)

        dxb = dx_ref[...].astype(BF16)
        dxb_ref[...] = dxb
        dmixed = _mm_nt(dxb, wo_ref[...].reshape(D, D))
        dosb_ref[...] = dmixed[:, :W].astype(BF16)
        for gi in range(G):
            cols = slice(gi * gd, (gi + 1) * gd)
            pg = pooled_ref[:, cols]
            dop = dmixed[:, W + gi * gd:W + (gi + 1) * gd]
            pw = _mm(pg, wp_ref[gi])
            dps_ref[:, cols] += jnp.sum(dop * pw, axis=0, keepdims=True)
            dpw = (dop * ps_ref[:, cols]).astype(BF16)
            dwp_ref[gi] += _mm_tn(pg, dpw)
            dpooled_ref[:, cols] = _mm_nt(dpw, wp_ref[gi])

    row = lambda i: (i, 0)
    return pl.pallas_call(
        body, name="mix_out_bwd", grid=(T // tm,),
        in_specs=[pl.BlockSpec((tm, D), row), pl.BlockSpec((tm, W), row),
                  pl.BlockSpec((G, gd, gd), lambda i: (0, 0, 0)), pl.BlockSpec((1, W), lambda i: (0, 0)),
                  pl.BlockSpec(w_out.shape, lambda i: (0, 0, 0))],
        out_specs=(pl.BlockSpec((tm, D), row), pl.BlockSpec((tm, W), row), pl.BlockSpec((tm, W), row),
                   pl.BlockSpec((G, gd, gd), lambda i: (0, 0, 0)), pl.BlockSpec((1, W), lambda i: (0, 0))),
        out_shape=(jax.ShapeDtypeStruct((T, D), BF16), jax.ShapeDtypeStruct((T, W), BF16),
                   jax.ShapeDtypeStruct((T, W), F32), jax.ShapeDtypeStruct((G, gd, gd), F32),
                   jax.ShapeDtypeStruct((1, W), F32)),
        compiler_params=_params("arbitrary"),
    )(dx, pooled, w_pool, pool_scale, w_out)


def _mem_kv_fwd(mem, gain, w_kvt):
    B, M, D = mem.shape
    C = w_kvt.shape[0]

    def body(mem_ref, gain_ref, w_ref, memn_ref, kv_ref):
        _, xhat = _rms(mem_ref[...])
        mn = (xhat * gain_ref[...]).astype(BF16)
        memn_ref[...] = mn
        kv_ref[...] = _mm_nt(mn, w_ref[...]).astype(BF16)

    return pl.pallas_call(
        body, name="mem_kv_fwd", grid=(B,),
        in_specs=[pl.BlockSpec((None, M, D), lambda b: (b, 0, 0)), pl.BlockSpec((1, D), lambda b: (0, 0)),
                  pl.BlockSpec((C, D), lambda b: (0, 0))],
        out_specs=(pl.BlockSpec((M, D), lambda b: (b, 0)), pl.BlockSpec((None, M, C), lambda b: (b, 0, 0))),
        out_shape=(jax.ShapeDtypeStruct((B * M, D), BF16), jax.ShapeDtypeStruct((B, M, C), BF16)),
        compiler_params=_params("arbitrary"),
    )(mem, gain, w_kvt)


def _mem_kv_bwd(dkv, mem, w_kvt):
    B, M, D = mem.shape
    C = w_kvt.shape[0]

    def body(dkv_ref, mem_ref, w_ref, dkvb_ref, dgain_ref):
        @pl.when(pl.program_id(0) == 0)
        def _():
            dgain_ref[...] = jnp.zeros_like(dgain_ref)

        dkvb = dkv_ref[...].astype(BF16)
        dkvb_ref[...] = dkvb
        dmn = _mm(dkvb, w_ref[...])
        _, xhat = _rms(mem_ref[...])
        dgain_ref[...] += jnp.sum(dmn * xhat, axis=0, keepdims=True)

    return pl.pallas_call(
        body, name="mem_kv_bwd", grid=(B,),
        in_specs=[pl.BlockSpec((None, M, C), lambda b: (b, 0, 0)), pl.BlockSpec((None, M, D), lambda b: (b, 0, 0)),
                  pl.BlockSpec((C, D), lambda b: (0, 0))],
        out_specs=(pl.BlockSpec((M, C), lambda b: (b, 0)), pl.BlockSpec((1, D), lambda b: (0, 0))),
        out_shape=(jax.ShapeDtypeStruct((B * M, C), BF16), jax.ShapeDtypeStruct((1, D), F32)),
        compiler_params=_params("arbitrary"),
    )(dkv, mem, w_kvt)


def _softmax_rows(s):
    p = jnp.exp(s - jnp.max(s, axis=1, keepdims=True))
    return p / jnp.sum(p, axis=1, keepdims=True)


def _cross_fwd(x, gain, kv, w_q, w_o, B, S, comm=None):
    T, D = x.shape
    M = kv.shape[1]
    hd = D // MEM_HEADS
    tm = _tile(S, 1024)
    per = S // tm
    scale = hd ** -0.5

    def body(x_ref, gain_ref, kv_ref, wq_ref, wo_ref, out_ref, hq_ref, q_ref, ocat_ref):
        _, xhat = _rms(x_ref[...])
        hq = (xhat * gain_ref[...]).astype(BF16)
        hq_ref[...] = hq
        q = _mm(hq, wq_ref[...].reshape(D, D)).astype(BF16)
        q_ref[...] = q
        for h in range(MEM_HEADS):
            cols = slice(h * hd, (h + 1) * hd)
            s = _mm_nt(q[:, cols], kv_ref[:, cols]) * scale
            p = _softmax_rows(s).astype(BF16)
            ocat_ref[:, cols] = _mm(p, kv_ref[:, D + h * hd:D + (h + 1) * hd]).astype(BF16)
        out_ref[...] = x_ref[...] + _mm(ocat_ref[...], wo_ref[...].reshape(D, D))

    row = lambda b, t: (b * per + t, 0)
    wspec = pl.BlockSpec(w_q.shape, lambda b, t: (0, 0, 0))
    return _call(
        body, name="cross_fwd", grid=(B, per), comm=comm,
        in_specs=[pl.BlockSpec((tm, D), row), pl.BlockSpec((1, D), lambda b, t: (0, 0)),
                  pl.BlockSpec((None, M, 2 * D), lambda b, t: (b, 0, 0)), wspec, wspec],
        out_specs=tuple(pl.BlockSpec((tm, D), row) for _ in range(4)),
        out_shape=(jax.ShapeDtypeStruct((T, D), F32),) + tuple(jax.ShapeDtypeStruct((T, D), BF16) for _ in range(3)),
        args=(x, gain, kv, w_q, w_o))


def _cross_bwd(dy, x, gain, q, kv, w_q, w_o, B, S, comm=None):
    T, D = x.shape
    M = kv.shape[1]
    hd = D // MEM_HEADS
    tm = _tile(S, 512)
    per = S // tm
    scale = hd ** -0.5

    def body(dy_ref, x_ref, gain_ref, q_ref, kv_ref, wq_ref, wo_ref,
             dx_ref, dyb_ref, dqb_ref, dkv_ref, dgain_ref):
        b_id, t_id = pl.program_id(0), pl.program_id(1)

        @pl.when((b_id == 0) & (t_id == 0))
        def _():
            dgain_ref[...] = jnp.zeros_like(dgain_ref)

        @pl.when(t_id == 0)
        def _():
            dkv_ref[...] = jnp.zeros_like(dkv_ref)

        dyb = dy_ref[...].astype(BF16)
        dyb_ref[...] = dyb
        docat = _mm_nt(dyb, wo_ref[...].reshape(D, D)).astype(BF16)
        for h in range(MEM_HEADS):
            cols = slice(h * hd, (h + 1) * hd)
            vcols = slice(D + h * hd, D + (h + 1) * hd)
            qh, kh, vh, doh = q_ref[:, cols], kv_ref[:, cols], kv_ref[:, vcols], docat[:, cols]
            p = _softmax_rows(_mm_nt(qh, kh) * scale)
            dp = _mm_nt(doh, vh)
            ds = (p * (dp - jnp.sum(dp * p, axis=1, keepdims=True)) * scale).astype(BF16)
            dqb_ref[:, cols] = _mm(ds, kh).astype(BF16)
            dkv_ref[:, cols] += _mm_tn(ds, qh)
            dkv_ref[:, vcols] += _mm_tn(p.astype(BF16), doh)
        dhq = _mm_nt(dqb_ref[...], wq_ref[...].reshape(D, D))
        r, xhat = _rms(x_ref[...])
        dgain_ref[...] += jnp.sum(dhq * xhat, axis=0, keepdims=True)
        dx_ref[...] = dy_ref[...] + _rms_bwd(dhq, gain_ref[...], r, xhat)

    row = lambda b, t: (b * per + t, 0)
    wspec = pl.BlockSpec(w_q.shape, lambda b, t: (0, 0, 0))
    one = pl.BlockSpec((1, D), lambda b, t: (0, 0))
    kvspec = pl.BlockSpec((None, M, 2 * D), lambda b, t: (b, 0, 0))
    return _call(
        body, name="cross_bwd", grid=(B, per), comm=comm,
        in_specs=[pl.BlockSpec((tm, D), row), pl.BlockSpec((tm, D), row), one, pl.BlockSpec((tm, D), row),
                  kvspec, wspec, wspec],
        out_specs=(pl.BlockSpec((tm, D), row), pl.BlockSpec((tm, D), row), pl.BlockSpec((tm, D), row), kvspec, one),
        out_shape=(jax.ShapeDtypeStruct((T, D), F32), jax.ShapeDtypeStruct((T, D), BF16),
                   jax.ShapeDtypeStruct((T, D), BF16), jax.ShapeDtypeStruct((B, M, 2 * D), F32),
                   jax.ShapeDtypeStruct((1, D), F32)),
        args=(dy, x, gain, q, kv, w_q, w_o))


def _ordered_sum(gp_ref):
    g = gp_ref[0].astype(F32)
    for s in range(1, gp_ref.shape[0]):
        g = g + gp_ref[s].astype(F32)
    return g


def _adam_write(g, w_ref, m_ref, v_ref, g_ref, d_ref, nm_ref, nv_ref):
    nm = ADAM_B1 * m_ref[...] + (1.0 - ADAM_B1) * g
    nv = ADAM_B2 * v_ref[...] + (1.0 - ADAM_B2) * (g * g)
    m_hat = nm / (1.0 - ADAM_B1 ** ADAM_STEP)
    v_hat = nv / (1.0 - ADAM_B2 ** ADAM_STEP)
    g_ref[...] = g
    nm_ref[...] = nm
    nv_ref[...] = nv
    d_ref[...] = -ADAM_LR * (m_hat / (jnp.sqrt(v_hat) + ADAM_EPS) + ADAM_WD * w_ref[...])


def _adamw(items, name):
    R = items[0][1].shape[0]
    tr = next(t for t in (_tile(R, 128), _tile(R, 256)) if t % 16 == 0)
    n = len(items)

    def body(*refs):
        ins, outs = refs[:4 * n], refs[4 * n:]
        for k in range(n):
            gp_ref, w_ref, m_ref, v_ref = ins[4 * k:4 * k + 4]
            _adam_write(_ordered_sum(gp_ref), w_ref, m_ref, v_ref, *outs[4 * k:4 * k + 4])

    in_specs, out_specs, out_shape = [], [], []
    for parts, w, _, _ in items:
        C = w.shape[1]
        spec = pl.BlockSpec((tr, C), lambda i: (i, 0))
        in_specs += [pl.BlockSpec((parts.shape[0], tr, C), lambda i: (0, i, 0)), spec, spec, spec]
        out_specs += [spec] * 4
        out_shape += [jax.ShapeDtypeStruct(w.shape, F32)] * 4
    res = pl.pallas_call(
        body, name=name, grid=(R // tr,), in_specs=in_specs, out_specs=tuple(out_specs), out_shape=tuple(out_shape),
        compiler_params=_params("arbitrary"),
    )(*[a for item in items for a in item])
    return [res[4 * k:4 * k + 4] for k in range(n)]


def _pack_rows(vectors, D):
    def body(*refs):
        out_ref = refs[-1]
        out_ref[...] = jnp.zeros_like(out_ref)
        for i, r in enumerate(refs[:-1]):
            out_ref[i:i + 1, :r.shape[1]] = r[...]

    vmem = pl.BlockSpec(memory_space=pltpu.VMEM)
    return pl.pallas_call(body, name="pack_small", in_specs=[vmem] * len(vectors), out_specs=vmem,
                          out_shape=jax.ShapeDtypeStruct((8, D), F32))(*vectors)


def _adamw_small(row_parts, mat_parts, vectors, matrix):
    n = len(vectors)

    def body(*refs):
        rp_ref, mp_ref = refs[0], refs[1]
        ins, outs = refs[2:2 + 3 * (n + 1)], refs[2 + 3 * (n + 1):]
        rows = _ordered_sum(rp_ref)
        for i in range(n):
            c = ins[3 * i].shape[1]
            _adam_write(rows[i:i + 1, :c], *ins[3 * i:3 * i + 3], *outs[4 * i:4 * i + 4])
        _adam_write(_ordered_sum(mp_ref), *ins[3 * n:3 * n + 3], *outs[4 * n:4 * n + 4])
        outs[-1][...] = rows[n:n + 1, :]

    flat = [a for wmv in vectors for a in wmv] + list(matrix)
    out_shape = [jax.ShapeDtypeStruct(wmv[0].shape, F32) for wmv in list(vectors) + [matrix] for _ in range(4)]
    out_shape.append(jax.ShapeDtypeStruct((1, row_parts.shape[2]), F32))
    vmem = pl.BlockSpec(memory_space=pltpu.VMEM)
    res = pl.pallas_call(body, name="adamw_small", in_specs=[vmem] * (2 + len(flat)),
                         out_specs=tuple([vmem] * len(out_shape)), out_shape=tuple(out_shape))(row_parts, mat_parts, *flat)
    return [res[4 * i:4 * i + 4] for i in range(n + 1)], res[-1]


def kernel(x, mem, ffn1_norm, ffn1_w_gate, ffn1_w_up, ffn1_w_down, mix_norm, w_in, w_pool, pool_scale, w_out, mem_q_norm, mem_kv_norm, mem_w_q, mem_w_kv, mem_w_o, ffn2_norm, ffn2_w_gate, ffn2_w_up, ffn2_w_down, final_norm, loss_target, m_ffn1_norm, m_ffn1_w_gate, m_ffn1_w_up, m_ffn1_w_down, m_mix_norm, m_w_in, m_w_pool, m_pool_scale, m_w_out, m_mem_q_norm, m_mem_kv_norm, m_mem_w_q, m_mem_w_kv, m_mem_w_o, m_ffn2_norm, m_ffn2_w_gate, m_ffn2_w_up, m_ffn2_w_down, m_final_norm, v_ffn1_norm, v_ffn1_w_gate, v_ffn1_w_up, v_ffn1_w_down, v_mix_norm, v_w_in, v_w_pool, v_pool_scale, v_w_out, v_mem_q_norm, v_mem_kv_norm, v_mem_w_q, v_mem_w_kv, v_mem_w_o, v_ffn2_norm, v_ffn2_w_gate, v_ffn2_w_up, v_ffn2_w_down, v_final_norm):
    B, S, D = x.shape
    T = B * S
    x0 = x.reshape(T, D)
    target = loss_target.reshape(T, D)
    final_gain = final_norm.reshape(1, D)

    big = dict(
        g1=ffn1_w_gate[0].T, u1=ffn1_w_up[0].T, d1=ffn1_w_down[0],
        g2=ffn2_w_gate[0].T, u2=ffn2_w_up[0].T, d2=ffn2_w_down[0],
        w_in=w_in[0].T, w_out=w_out[0], w_q=mem_w_q[0], w_kv=mem_w_kv[0].T, w_o=mem_w_o[0])
    names = list(big)
    shard = {k: big[k].astype(BF16) for k in names}
    wp = w_pool[0].astype(BF16)
    full, ffn_w = {}, {}
    stacked = ("g1", "u1", "d1", "g2", "u2", "d2", "w_in", "w_kv")

    def gathered(keys, arrs):
        full.update(zip(keys, arrs))
        ffn_w.update({k: full[k].reshape(-1, D) for k in keys if k in stacked})

    first, mid = ("g1", "u1", "d1"), ("w_in", "w_out", "w_q", "w_kv", "w_o")
    gathered(first, _gather_two_level([shard[k] for k in first], "gather_ffn1"))
    (x1, hn1, a1, s1, t1), got = _ffn_fwd(x0, ffn1_norm, ffn_w["g1"], ffn_w["u1"], ffn_w["d1"], "ffn1_fwd",
                                          comm=([shard[k] for k in mid], GATHER2))
    gathered(mid, got)
    hn2, qkv, u = _mix_in_fwd(x1, mix_norm, ffn_w["w_in"])
    qkv3 = qkv.reshape(B, S, -1)
    (o_sb,), got = _sb_fwd(qkv3, B, S, comm=([shard["g2"], shard["u2"]], GATHER2))
    gathered(("g2", "u2"), got)
    pooled = _pool_fwd(u.reshape(B, S, -1), B, S).reshape(T, -1)
    x2, mixed = _mix_out_fwd(x1, o_sb.reshape(T, -1), pooled, wp, pool_scale, full["w_out"])
    memn, kv = _mem_kv_fwd(mem, mem_kv_norm, ffn_w["w_kv"])
    (x3, hq, q, ocat), got = _cross_fwd(x2, mem_q_norm, kv, full["w_q"], full["w_o"], B, S,
                                        comm=([shard["d2"]], GATHER2))
    gathered(("d2",), got)
    (dx4, hn4, a2, s2, t2, d_final, loss_part), _ = _ffn_fwd(x3, ffn2_norm, ffn_w["g2"], ffn_w["u2"], ffn_w["d2"],
                                                            "ffn2_fwd", head=(final_gain, target))

    slab = lambda k: grads[k].reshape((N_DEV, -1) + grads[k].shape[-1:])
    got = {}
    dx3, dg2, du2, dyh2, d_ffn2 = _ffn_bwd(dx4, x3, ffn2_norm, s2, t2, ffn_w["g2"], ffn_w["u2"],
                                          ffn_w["d2"], "ffn2_bwd")
    ffn_slab = ffn1_w_gate.shape[2]
    grads = dict(g2=_wgrad(hn4, dg2, "dw_gate2", col_slab=ffn_slab), u2=_wgrad(hn4, du2, "dw_up2", col_slab=ffn_slab),
                 d2=_wgrad(a2, dyh2, "dw_down2"))
    (dx2, dx3b, dqb, dkv, d_q), (got["g2"],) = _cross_bwd(dx3, x2, mem_q_norm, q, kv, full["w_q"], full["w_o"], B, S,
                                                         comm=([slab("g2")], False))
    grads["w_o"] = _wgrad(ocat, dx3b, "dw_o")
    grads["w_q"] = _wgrad(hq, dqb, "dw_q")
    dkvb, d_kv = _mem_kv_bwd(dkv, mem, ffn_w["w_kv"])
    grads["w_kv"] = _wgrad(memn, dkvb, "dw_kv", col_slab=mem_w_kv.shape[2])
    dx2b, do_sb, dpooled, d_wpool, d_ps = _mix_out_bwd(dx2, pooled, wp, pool_scale, full["w_out"])
    grads["w_out"] = _wgrad(mixed, dx2b, "dw_out")
    du = _pool_bwd(dpooled.reshape(B, S, -1), B, S).reshape(T, -1)
    early = ("u2", "d2", "w_o", "w_q", "w_kv", "w_out")
    (dq, dk, dv), res = _sb_bwd(qkv3, do_sb.reshape(B, S, -1), B, S, comm=([slab(k) for k in early], False))
    got.update(zip(early, res))
    dx1, dproj, d_mix = _mix_in_bwd(dx2, dq.reshape(T, -1), dk.reshape(T, -1), dv.reshape(T, -1), du,
                                    x1, mix_norm, ffn_w["w_in"])
    grads["w_in"] = _wgrad(hn2, dproj, "dw_in", col_slab=w_in.shape[2])
    dx0, dg1, du1, dyh1, d_ffn1 = _ffn_bwd(dx1, x0, ffn1_norm, s1, t1, ffn_w["g1"], ffn_w["u1"],
                                          ffn_w["d1"], "ffn1_bwd")

    small = [("ffn1_norm", d_ffn1, ffn1_norm, m_ffn1_norm, v_ffn1_norm),
             ("mix_norm", d_mix, mix_norm, m_mix_norm, v_mix_norm),
             ("pool_scale", d_ps, pool_scale, m_pool_scale, v_pool_scale),
             ("mem_q_norm", d_q, mem_q_norm, m_mem_q_norm, v_mem_q_norm),
             ("mem_kv_norm", d_kv, mem_kv_norm, m_mem_kv_norm, v_mem_kv_norm),
             ("ffn2_norm", d_ffn2, ffn2_norm, m_ffn2_norm, v_ffn2_norm),
             ("final_norm", d_final, final_gain, m_final_norm.reshape(1, D), v_final_norm.reshape(1, D))]
    row_pack = _pack_rows([t[1] for t in small] + [loss_part], D)
    as_rows = lambda t: t.reshape(-1, LANES)
    grads["g1"], (row_parts, pool_parts) = _wgrad(hn1, dg1, "dw_gate1", col_slab=ffn_slab,
                                                  comm=([row_pack, as_rows(d_wpool)], True))
    grads["u1"], (got["g1"], got["w_in"]) = _wgrad(hn1, du1, "dw_up1", col_slab=ffn_slab,
                                                   comm=([slab("g1"), slab("w_in")], PAIRSUM))
    grads["d1"], (got["u1"],) = _wgrad(a1, dyh1, "dw_down1", comm=([slab("u1")], PAIRSUM))
    got["d1"] = _pairsum_exchange([slab("d1")], "scatter_last")[0]

    state = dict(
        g1=(ffn1_w_gate, m_ffn1_w_gate, v_ffn1_w_gate), u1=(ffn1_w_up, m_ffn1_w_up, v_ffn1_w_up),
        d1=(ffn1_w_down, m_ffn1_w_down, v_ffn1_w_down), g2=(ffn2_w_gate, m_ffn2_w_gate, v_ffn2_w_gate),
        u2=(ffn2_w_up, m_ffn2_w_up, v_ffn2_w_up), d2=(ffn2_w_down, m_ffn2_w_down, v_ffn2_w_down),
        w_in=(w_in, m_w_in, v_w_in), w_out=(w_out, m_w_out, v_w_out), w_q=(mem_w_q, m_mem_w_q, v_mem_w_q),
        w_kv=(mem_w_kv, m_mem_w_kv, v_mem_w_kv), w_o=(mem_w_o, m_mem_w_o, v_mem_w_o))
    big_out = {}
    groups = dict(gate_up=("g1", "u1", "g2", "u2"), in_kv=("w_in", "w_kv"), square=("w_out", "w_q", "w_o"),
                  down=("d1", "d2"))
    for label, keys in groups.items():
        items = [(got[k],) + tuple(t[0] for t in state[k]) for k in keys]
        for k, outs in zip(keys, _adamw(items, "adamw_" + label)):
            big_out[k] = [t[None] for t in outs]

    small_res, loss_row = _adamw_small(row_parts, pool_parts, [t[2:] for t in small],
                                       [as_rows(t) for t in (w_pool, m_w_pool, v_w_pool)])
    small_out = {t[0]: small_res[i] for i, t in enumerate(small)}
    small_out["final_norm"] = [t.reshape(D) for t in small_out["final_norm"]]
    small_out["w_pool"] = [t.reshape(w_pool.shape) for t in small_res[-1]]
    loss = loss_row[0, 0]

    order = [("ffn1_norm", None), ("ffn1_w_gate", "g1"), ("ffn1_w_up", "u1"), ("ffn1_w_down", "d1"),
             ("mix_norm", None), ("w_in", "w_in"), ("w_pool", None), ("pool_scale", None), ("w_out", "w_out"),
             ("mem_q_norm", None), ("mem_kv_norm", None), ("mem_w_q", "w_q"), ("mem_w_kv", "w_kv"),
             ("mem_w_o", "w_o"), ("ffn2_norm", None), ("ffn2_w_gate", "g2"), ("ffn2_w_up", "u2"),
             ("ffn2_w_down", "d2"), ("final_norm", None)]
    res = [loss, dx0.reshape(B, S, D)]
    for which in range(4):
        for name, key in order:
            res.append(big_out[key][which] if key else small_out[name][which])
    return tuple(res)
```

```python
import functools
import math

import jax
import jax.numpy as jnp
from jax import lax
from jax.experimental import pallas as pl
from jax.experimental.pallas import tpu as pltpu

F32 = jnp.float32
BF16 = jnp.bfloat16

N_DEV = 8
EPS = 1e-6
SB_HEAD_DIM = 64
LANES = 128
POOL_WINDOWS = (2, 4, 8, 16)
POOL_GROUP_DIM = 128
MEM_HEADS = 4
FFN_RESIDUAL_WEIGHT = 0.5
ADAM_LR = 0.001
ADAM_B1 = 0.9
ADAM_B2 = 0.999
ADAM_EPS = 1e-08
ADAM_WD = 0.01
ADAM_STEP = 10
VMEM_LIMIT = 58 * 1024 * 1024

MESH_ID = pl.DeviceIdType.MESH


def _params(*sem):
    return pltpu.CompilerParams(dimension_semantics=sem, vmem_limit_bytes=VMEM_LIMIT)


def _tile(n, pref):
    if n <= pref:
        return n
    t = pref - pref % 8
    while n % t:
        t -= 8
    return t


def _mm(a, b):
    return jnp.dot(a, b, preferred_element_type=F32)


def _mm_nt(a, b):
    return lax.dot_general(a, b, (((1,), (1,)), ((), ())), preferred_element_type=F32)


def _mm_tn(a, b):
    return lax.dot_general(a, b, (((0,), (0,)), ((), ())), preferred_element_type=F32)


def _rms(xv):
    r = lax.rsqrt(jnp.mean(xv * xv, axis=-1, keepdims=True) + EPS)
    return r, xv * r


def _rms_bwd(dhn, gain, r, xhat):
    dxh = dhn * gain
    return r * (dxh - xhat * jnp.mean(dxh * xhat, axis=-1, keepdims=True))


def _sigmoid(z):
    return 0.5 * jnp.tanh(0.5 * z) + 0.5


def _flags(arrs, gather):
    return [gather] * len(arrs) if isinstance(gather, bool) else list(gather)


def _comm_shapes(arrs, gather):
    return tuple(jax.ShapeDtypeStruct(((N_DEV,) + tuple(a.shape)) if f else tuple(a.shape), a.dtype)
                 for a, f in zip(arrs, _flags(arrs, gather)))


def _comm_start(ins, outs, sems, gather):
    send_sems, recv_sems, local_sems = sems
    gather = _flags(ins, gather)
    x, y, c = lax.axis_index("x"), lax.axis_index("y"), lax.axis_index("c")
    me = 4 * x + 2 * y + c
    for i in range(len(ins)):
        src = ins[i] if gather[i] else ins[i].at[me]
        pltpu.make_async_copy(src, outs[i].at[me], local_sems.at[i]).start()
    for k in range(1, N_DEV):
        px = 1 - x if k & 4 else x
        py = 1 - y if k & 2 else y
        pc = 1 - c if k & 1 else c
        peer = 4 * px + 2 * py + pc
        for i in range(len(ins)):
            src = ins[i] if gather[i] else ins[i].at[peer]
            pltpu.make_async_remote_copy(
                src_ref=src, dst_ref=outs[i].at[me],
                send_sem=send_sems.at[i], recv_sem=recv_sems.at[i],
                device_id=(px, py, pc), device_id_type=MESH_ID).start()


def _comm_wait(ins, outs, sems, gather):
    send_sems, recv_sems, local_sems = sems
    gather = _flags(ins, gather)
    x, y, c = lax.axis_index("x"), lax.axis_index("y"), lax.axis_index("c")
    me = 4 * x + 2 * y + c
    for i in range(len(ins)):
        seven = outs[i].at[pl.ds(0, N_DEV - 1)]
        done = pltpu.make_async_remote_copy(
            src_ref=seven, dst_ref=seven,
            send_sem=send_sems.at[i], recv_sem=recv_sems.at[i],
            device_id=(x, y, c), device_id_type=MESH_ID)
        done.wait_send()
        done.wait_recv()
        src = ins[i] if gather[i] else ins[i].at[me]
        pltpu.make_async_copy(src, outs[i].at[me], local_sems.at[i]).wait()


def _comm_sems(n):
    return [pltpu.SemaphoreType.DMA((n,)) for _ in range(3)]


def _exchange(arrs, gather, name):
    n = len(arrs)

    def body(*refs):
        ins, outs, sems = refs[:n], refs[n:2 * n], refs[2 * n:]
        _comm_start(ins, outs, sems, gather)
        _comm_wait(ins, outs, sems, gather)

    any_spec = pl.BlockSpec(memory_space=pl.ANY)
    outs = pl.pallas_call(
        body, name=name, out_shape=_comm_shapes(arrs, gather),
        in_specs=[any_spec] * n, out_specs=tuple([any_spec] * n), scratch_shapes=_comm_sems(n),
    )(*arrs)
    return list(outs)


GATHER2 = "gather2"


def _gather2_copies(ins, outs, sems):
    n = len(ins)
    send_sems, recv_sems, local_sems = sems
    x, y, c = lax.axis_index("x"), lax.axis_index("y"), lax.axis_index("c")
    me, sibling = (x, y, c), (x, y, 1 - c)
    chips = [(1 - x, y), (x, 1 - y), (1 - x, 1 - y)]

    def copy(i, k, block, to, own=False):
        slab = outs[i].at[4 * block[0] + 2 * block[1] + block[2]]
        return pltpu.make_async_remote_copy(
            src_ref=ins[i] if own else slab, dst_ref=slab,
            send_sem=send_sems.at[i, k], recv_sem=recv_sems.at[i, k],
            device_id=to, device_id_type=MESH_ID)

    each = [(j, chip, i) for j, chip in enumerate(chips) for i in range(n)]
    return dict(
        mine=lambda: [pltpu.make_async_copy(ins[i], outs[i].at[4 * x + 2 * y + c], local_sems.at[i]) for i in range(n)],
        first=lambda: [copy(i, 0, me, sibling, own=True) for i in range(n)]
        + [copy(i, 1 + j, me, (*chip, c), own=True) for j, chip, i in each],
        landed=lambda: [copy(i, 1 + j, (*chip, c), me) for j, chip, i in each],
        passed=lambda: [copy(i, 4 + j, (*chip, c), sibling) for j, chip, i in each],
        from_sibling=lambda: [copy(i, 0, sibling, me) for i in range(n)]
        + [copy(i, 4 + j, (*chip, 1 - c), me) for j, chip, i in each])


def _gather2_start(ins, outs, sems):
    make = _gather2_copies(ins, outs, sems)
    for cp in make["mine"]() + make["first"]():
        cp.start()


def _gather2_middle(ins, outs, sems):
    make = _gather2_copies(ins, outs, sems)
    for arrival, onward in zip(make["landed"](), make["passed"]()):
        arrival.wait_recv()
        onward.start()


def _gather2_wait(ins, outs, sems):
    make = _gather2_copies(ins, outs, sems)
    for cp in make["from_sibling"]():
        cp.wait_recv()
    for cp in make["first"]() + make["passed"]():
        cp.wait_send()
    for cp in make["mine"]():
        cp.wait()


def _gather2_sems(n):
    return [pltpu.SemaphoreType.DMA((n, N_DEV - 1)), pltpu.SemaphoreType.DMA((n, N_DEV - 1)),
            pltpu.SemaphoreType.DMA((n,))]


def _gather_two_level(arrs, name):
    n = len(arrs)

    def body(*refs):
        ins, outs, sems = refs[:n], refs[n:2 * n], refs[2 * n:]
        _gather2_start(ins, outs, sems)
        _gather2_middle(ins, outs, sems)
        _gather2_wait(ins, outs, sems)

    any_spec = pl.BlockSpec(memory_space=pl.ANY)
    outs = pl.pallas_call(
        body, name=name, out_shape=_comm_shapes(arrs, True),
        in_specs=[any_spec] * n, out_specs=tuple([any_spec] * n), scratch_shapes=_gather2_sems(n),
    )(*arrs)
    return list(outs)


CHIPS = N_DEV // 2
PAIRSUM = "pairsum"


def _pairsum_shapes(arrs):
    return tuple(jax.ShapeDtypeStruct((CHIPS,) + tuple(a.shape[1:]), a.dtype) for a in arrs)


def _pairsum_scratch(arrs):
    bufs = [pltpu.VMEM((CHIPS,) + tuple(a.shape[1:]), a.dtype) for a in arrs for _ in range(3)]
    return bufs + [pltpu.SemaphoreType.DMA((len(arrs),)) for _ in range(6)]


def _pairsum_start(ins, outs, sc):
    n = len(ins)
    load, d2d_send, d2d_recv = sc[3 * n], sc[3 * n + 1], sc[3 * n + 2]
    x, y, c = lax.axis_index("x"), lax.axis_index("y"), lax.axis_index("c")
    for i in range(n):
        mine, theirs = sc[3 * i], sc[3 * i + 1]
        for q in range(CHIPS):
            pltpu.make_async_copy(ins[i].at[2 * q + c], mine.at[q], load.at[i]).start()
            pltpu.make_async_remote_copy(
                src_ref=ins[i].at[2 * q + (1 - c)], dst_ref=theirs.at[q], send_sem=d2d_send.at[i],
                recv_sem=d2d_recv.at[i], device_id=(x, y, 1 - c), device_id_type=MESH_ID).start()


def _pairsum_middle(ins, outs, sc):
    n = len(ins)
    load, d2d_send, d2d_recv, ici_send, ici_recv, store = sc[3 * n:3 * n + 6]
    x, y, c = lax.axis_index("x"), lax.axis_index("y"), lax.axis_index("c")
    here = 2 * x + y
    for i in range(n):
        mine, theirs, total = sc[3 * i:3 * i + 3]
        pltpu.make_async_copy(mine, mine, load.at[i]).wait()
        pltpu.make_async_remote_copy(src_ref=theirs, dst_ref=theirs, send_sem=d2d_send.at[i], recv_sem=d2d_recv.at[i],
                                     device_id=(x, y, c), device_id_type=MESH_ID).wait_recv()
        for q in range(CHIPS):
            total[q] = (mine[q].astype(F32) + theirs[q].astype(F32)).astype(total.dtype)
        for k in range(1, CHIPS):
            px = 1 - x if k & 2 else x
            py = 1 - y if k & 1 else y
            pltpu.make_async_remote_copy(
                src_ref=total.at[2 * px + py], dst_ref=outs[i].at[here], send_sem=ici_send.at[i],
                recv_sem=ici_recv.at[i], device_id=(px, py, c), device_id_type=MESH_ID).start()
        pltpu.make_async_copy(total.at[here], outs[i].at[here], store.at[i]).start()


def _pairsum_wait(ins, outs, sc):
    n = len(ins)
    load, d2d_send, d2d_recv, ici_send, ici_recv, store = sc[3 * n:3 * n + 6]
    x, y, c = lax.axis_index("x"), lax.axis_index("y"), lax.axis_index("c")
    here = 2 * x + y
    for i in range(n):
        theirs, total = sc[3 * i + 1], sc[3 * i + 2]
        three = outs[i].at[pl.ds(0, CHIPS - 1)]
        pltpu.make_async_remote_copy(src_ref=theirs, dst_ref=theirs, send_sem=d2d_send.at[i], recv_sem=d2d_recv.at[i],
                                     device_id=(x, y, c), device_id_type=MESH_ID).wait_send()
        done = pltpu.make_async_remote_copy(src_ref=three, dst_ref=three, send_sem=ici_send.at[i],
                                            recv_sem=ici_recv.at[i], device_id=(x, y, c), device_id_type=MESH_ID)
        done.wait_send()
        done.wait_recv()
        pltpu.make_async_copy(total.at[here], outs[i].at[here], store.at[i]).wait()


def _pairsum_exchange(arrs, name):
    n = len(arrs)

    def body(*refs):
        ins, outs, sc = refs[:n], refs[n:2 * n], refs[2 * n:]
        _pairsum_start(ins, outs, sc)
        _pairsum_middle(ins, outs, sc)
        _pairsum_wait(ins, outs, sc)

    any_spec = pl.BlockSpec(memory_space=pl.ANY)
    outs = pl.pallas_call(
        body, name=name, out_shape=_pairsum_shapes(arrs), in_specs=[any_spec] * n, out_specs=tuple([any_spec] * n),
        scratch_shapes=_pairsum_scratch(arrs), compiler_params=pltpu.CompilerParams(vmem_limit_bytes=VMEM_LIMIT),
    )(*arrs)
    return list(outs)


def _call(body, *, name, grid, in_specs, out_specs, out_shape, args, scratch=(), comm=None):
    sem = ("arbitrary",) * len(grid)
    if comm is None:
        res = pl.pallas_call(body, name=name, grid=grid, in_specs=list(in_specs), out_specs=tuple(out_specs),
                             out_shape=tuple(out_shape), scratch_shapes=list(scratch),
                             compiler_params=_params(*sem))(*args)
        return tuple(res), []
    arrs, gather = comm
    n, n_in, n_out, n_sc = len(arrs), len(args), len(out_shape), len(scratch)
    steps = math.prod(grid)
    if isinstance(gather, str) and gather == PAIRSUM:
        start, middle, wait = _pairsum_start, _pairsum_middle, _pairsum_wait
        shapes, extra, mid_step = _pairsum_shapes(arrs), _pairsum_scratch(arrs), steps // 4
    elif isinstance(gather, str) and gather == GATHER2:
        start, middle, wait = _gather2_start, _gather2_middle, _gather2_wait
        shapes, extra, mid_step = _comm_shapes(arrs, True), _gather2_sems(n), 3 * steps // 4
    else:
        start = functools.partial(_comm_start, gather=gather)
        wait = functools.partial(_comm_wait, gather=gather)
        middle, shapes, extra, mid_step = None, _comm_shapes(arrs, gather), _comm_sems(n), 0

    def wrapped(*refs):
        ins, cin = refs[:n_in], refs[n_in:n_in + n]
        outs, cout = refs[n_in + n:n_in + n + n_out], refs[n_in + n + n_out:n_in + 2 * n + n_out]
        sc, sems = refs[n_in + 2 * n + n_out:n_in + 2 * n + n_out + n_sc], refs[n_in + 2 * n + n_out + n_sc:]
        step = functools.reduce(lambda acc, ax: acc * grid[ax] + pl.program_id(ax), range(len(grid)), 0)

        @pl.when(step == 0)
        def _():
            start(cin, cout, sems)

        if middle:
            @pl.when(step == mid_step)
            def _():
                middle(cin, cout, sems)

        body(*ins, *outs, *sc)

        @pl.when(step == steps - 1)
        def _():
            wait(cin, cout, sems)

    any_spec = pl.BlockSpec(memory_space=pl.ANY)
    res = pl.pallas_call(
        wrapped, name=name, grid=grid, in_specs=list(in_specs) + [any_spec] * n,
        out_specs=tuple(out_specs) + (any_spec,) * n, out_shape=tuple(out_shape) + tuple(shapes),
        scratch_shapes=list(scratch) + list(extra), compiler_params=_params(*sem))(*args, *arrs)
    return tuple(res[:n_out]), list(res[n_out:])


FFN_BWD_ROWS = 512


def _load_resident(pairs, sem):
    copies = [pltpu.make_async_copy(src, dst, sem.at[k]) for k, (src, dst) in enumerate(pairs)]
    for cp in copies:
        cp.start()
    for cp in copies:
        cp.wait()


def _ffn_fwd(x, gain, wgt, wut, wd, name, comm=None, head=None):
    T, D = x.shape
    F = wd.shape[0]
    tm, tf = _tile(T, 512), _tile(F, 256)

    def body(*refs):
        if head:
            (x_ref, gain_ref, wg_hbm, wu_hbm, wd_hbm, fgain_ref, tgt_ref,
             out_ref, hn_ref, a_ref, s_ref, t_ref, dfgain_ref, loss_ref, wg_s, wu_s, wd_s, sem) = refs
        else:
            (x_ref, gain_ref, wg_hbm, wu_hbm, wd_hbm,
             out_ref, hn_ref, a_ref, s_ref, t_ref, wg_s, wu_s, wd_s, sem) = refs

        @pl.when(pl.program_id(0) == 0)
        def _():
            _load_resident([(wg_hbm, wg_s), (wu_hbm, wu_s), (wd_hbm, wd_s)], sem)
            if head:
                dfgain_ref[...] = jnp.zeros_like(dfgain_ref)
                loss_ref[...] = jnp.zeros_like(loss_ref)

        _, xhat = _rms(x_ref[...])
        hn = (xhat * gain_ref[...]).astype(BF16)
        hn_ref[...] = hn
        for f0 in range(0, F, tf):
            cols = slice(f0, f0 + tf)
            g = _mm_nt(hn, wg_s[cols, :])
            u = _mm_nt(hn, wu_s[cols, :])
            sig = _sigmoid(g)
            s = g * sig
            a_ref[:, cols] = (s * u).astype(BF16)
            s_ref[:, cols] = s.astype(BF16)
            t_ref[:, cols] = (u * (sig + s * (1.0 - sig))).astype(BF16)
        y = x_ref[...] + FFN_RESIDUAL_WEIGHT * _mm(a_ref[...], wd_s[...])
        if head:
            r, yhat = _rms(y)
            err = yhat * fgain_ref[...] - tgt_ref[...]
            loss_ref[...] += 0.5 * jnp.sum(jnp.mean(err * err, axis=-1, keepdims=True), axis=0, keepdims=True)
            dy = err * (1.0 / D)
            dfgain_ref[...] += jnp.sum(dy * yhat, axis=0, keepdims=True)
            out_ref[...] = _rms_bwd(dy, fgain_ref[...], r, yhat)
        else:
            out_ref[...] = y

    row = lambda i: (i, 0)
    one = pl.BlockSpec((1, D), lambda i: (0, 0))
    hbm = pl.BlockSpec(memory_space=pl.ANY)
    in_specs = [pl.BlockSpec((tm, D), row), one, hbm, hbm, hbm]
    out_specs = [pl.BlockSpec((tm, D), row), pl.BlockSpec((tm, D), row)] + [pl.BlockSpec((tm, F), row) for _ in range(3)]
    out_shape = [jax.ShapeDtypeStruct((T, D), F32), jax.ShapeDtypeStruct((T, D), BF16)] \
        + [jax.ShapeDtypeStruct((T, F), BF16) for _ in range(3)]
    args = (x, gain, wgt, wut, wd)
    if head:
        in_specs += [one, pl.BlockSpec((tm, D), row)]
        out_specs += [one, one]
        out_shape += [jax.ShapeDtypeStruct((1, D), F32), jax.ShapeDtypeStruct((1, D), F32)]
        args += tuple(head)
    return _call(
        body, name=name, grid=(T // tm,), comm=comm, in_specs=in_specs, out_specs=out_specs, out_shape=out_shape,
        scratch=[pltpu.VMEM((F, D), BF16) for _ in range(3)] + [pltpu.SemaphoreType.DMA((3,))], args=args)


def _ffn_bwd(dy, x, gain, s, t, wgt, wut, wd, name):
    T, D = x.shape
    F = wd.shape[0]
    tr, tf = _tile(T, FFN_BWD_ROWS), _tile(F, 256)
    rows = lambda i: (i, 0)
    one = pl.BlockSpec((1, D), lambda i: (0, 0))
    any_spec = pl.BlockSpec(memory_space=pl.ANY)

    def body(dy_ref, x_ref, gain_ref, s_ref, t_ref, wg_hbm, wu_hbm, wd_hbm,
             dx_ref, dg_ref, du_ref, dyh_ref, dgain_ref, wg_s, wu_s, wd_s, sem):
        @pl.when(pl.program_id(0) == 0)
        def _():
            _load_resident([(wg_hbm, wg_s), (wu_hbm, wu_s), (wd_hbm, wd_s)], sem)
            dgain_ref[...] = jnp.zeros_like(dgain_ref)

        dyh = (FFN_RESIDUAL_WEIGHT * dy_ref[...]).astype(BF16)
        dyh_ref[...] = dyh
        for f0 in range(0, F, tf):
            cols = slice(f0, f0 + tf)
            da = _mm_nt(dyh, wd_s[cols, :])
            dg_ref[:, cols] = (da * t_ref[:, cols].astype(F32)).astype(BF16)
            du_ref[:, cols] = (da * s_ref[:, cols].astype(F32)).astype(BF16)
        dhn = _mm(dg_ref[...], wg_s[...]) + _mm(du_ref[...], wu_s[...])
        r, xhat = _rms(x_ref[...])
        dgain_ref[...] += jnp.sum(dhn * xhat, axis=0, keepdims=True)
        dx_ref[...] = dy_ref[...] + _rms_bwd(dhn, gain_ref[...], r, xhat)

    wide = jax.ShapeDtypeStruct((T, F), BF16)
    return pl.pallas_call(
        body, name=name, grid=(T // tr,),
        in_specs=[pl.BlockSpec((tr, D), rows), pl.BlockSpec((tr, D), rows), one, pl.BlockSpec((tr, F), rows),
                  pl.BlockSpec((tr, F), rows), any_spec, any_spec, any_spec],
        out_specs=(pl.BlockSpec((tr, D), rows), pl.BlockSpec((tr, F), rows), pl.BlockSpec((tr, F), rows),
                   pl.BlockSpec((tr, D), rows), one),
        out_shape=(jax.ShapeDtypeStruct((T, D), F32), wide, wide, jax.ShapeDtypeStruct((T, D), BF16),
                   jax.ShapeDtypeStruct((1, D), F32)),
        scratch_shapes=[pltpu.VMEM((F, D), BF16) for _ in range(3)] + [pltpu.SemaphoreType.DMA((3,))],
        compiler_params=_params("arbitrary"),
    )(dy, x, gain, s, t, wgt, wut, wd)


def _wgrad(a, b, name, col_slab=None, comm=None):
    T, M = a.shape
    N = b.shape[1]
    tmm = M if M <= 1024 else _tile(M, 1408)
    tn = _tile(N, 1024)
    if col_slab and col_slab % LANES:
        tn = col_slab * LANES // math.gcd(col_slab, LANES)
    tk = _tile(T, 2048)
    nk = T // tk
    per = tn // col_slab if col_slab else 0

    def body(a_ref, b_ref, out_ref, acc):
        k = pl.program_id(2)

        @pl.when(k == 0)
        def _():
            acc[...] = jnp.zeros_like(acc)

        acc[...] += _mm_tn(a_ref[...], b_ref[...])

        @pl.when(k == nk - 1)
        def _():
            if col_slab:
                for s in range(per):
                    out_ref[s] = acc[:, s * col_slab:(s + 1) * col_slab].astype(BF16)
            else:
                out_ref[...] = acc[...].astype(BF16)

    if col_slab:
        out_spec = pl.BlockSpec((per, tmm, col_slab), lambda m, n, k: (n, m, 0))
        out_shape = jax.ShapeDtypeStruct((N // col_slab, M, col_slab), BF16)
    else:
        out_spec = pl.BlockSpec((tmm, tn), lambda m, n, k: (m, n))
        out_shape = jax.ShapeDtypeStruct((M, N), BF16)
    (out,), got = _call(
        body, name=name, grid=(M // tmm, N // tn, nk), comm=comm,
        in_specs=[pl.BlockSpec((tk, tmm), lambda m, n, k: (k, m)), pl.BlockSpec((tk, tn), lambda m, n, k: (k, n))],
        out_specs=(out_spec,), out_shape=(out_shape,),
        scratch=[pltpu.VMEM((tmm, tn), F32)], args=(a, b))
    return (out, got) if comm else out


def _mix_in_fwd(x, gain, w_int):
    T, D = x.shape
    C = w_int.shape[0]
    n_qkv = 3 * C // 4
    tm = _tile(T, 1024)

    def body(x_ref, gain_ref, w_ref, hn_ref, qkv_ref, u_ref):
        _, xhat = _rms(x_ref[...])
        hn = (xhat * gain_ref[...]).astype(BF16)
        hn_ref[...] = hn
        proj = _mm_nt(hn, w_ref[...])
        qkv_ref[...] = proj[:, :n_qkv].astype(BF16)
        u_ref[...] = proj[:, n_qkv:]

    row = lambda i: (i, 0)
    return pl.pallas_call(
        body, name="mix_in_fwd", grid=(T // tm,),
        in_specs=[pl.BlockSpec((tm, D), row), pl.BlockSpec((1, D), lambda i: (0, 0)),
                  pl.BlockSpec((C, D), lambda i: (0, 0))],
        out_specs=(pl.BlockSpec((tm, D), row), pl.BlockSpec((tm, n_qkv), row), pl.BlockSpec((tm, C - n_qkv), row)),
        out_shape=(jax.ShapeDtypeStruct((T, D), BF16), jax.ShapeDtypeStruct((T, n_qkv), BF16),
                   jax.ShapeDtypeStruct((T, C - n_qkv), F32)),
        compiler_params=_params("arbitrary"),
    )(x, gain, w_int)


def _mix_in_bwd(dres, dq, dk, dv, du, x, gain, w_int):
    T, D = x.shape
    C = w_int.shape[0]
    W = dq.shape[1]
    tm = _tile(T, 512)

    def body(dres_ref, dq_ref, dk_ref, dv_ref, du_ref, x_ref, gain_ref, w_ref, dx_ref, dproj_ref, dgain_ref):
        @pl.when(pl.program_id(0) == 0)
        def _():
            dgain_ref[...] = jnp.zeros_like(dgain_ref)

        for part, ref in enumerate((dq_ref, dk_ref, dv_ref, du_ref)):
            dproj_ref[:, part * W:(part + 1) * W] = ref[...]
        dhn = _mm(dproj_ref[...], w_ref[...])
        r, xhat = _rms(x_ref[...])
        dgain_ref[...] += jnp.sum(dhn * xhat, axis=0, keepdims=True)
        dx_ref[...] = dres_ref[...] + _rms_bwd(dhn, gain_ref[...], r, xhat)

    row = lambda i: (i, 0)
    one = pl.BlockSpec((1, D), lambda i: (0, 0))
    part = pl.BlockSpec((tm, W), row)
    return pl.pallas_call(
        body, name="mix_in_bwd", grid=(T // tm,),
        in_specs=[pl.BlockSpec((tm, D), row), part, part, part, part, pl.BlockSpec((tm, D), row), one,
                  pl.BlockSpec((C, D), lambda i: (0, 0))],
        out_specs=(pl.BlockSpec((tm, D), row), pl.BlockSpec((tm, C), row), one),
        out_shape=(jax.ShapeDtypeStruct((T, D), F32), jax.ShapeDtypeStruct((T, C), BF16),
                   jax.ShapeDtypeStruct((1, D), F32)),
        compiler_params=_params("arbitrary"),
    )(dres, dq, dk, dv, du, x, gain, w_int)


SB_PAIRS_PER_PROGRAM = 2
SB_FWD_PAIRS_PER_PROGRAM = 4
LOG2_E = 1.4426950408889634
EXP2_CLAMP = 126.0


def _neg_log2_sigmoid(nz2):
    w = jnp.minimum(nz2, EXP2_CLAMP)
    return w, jnp.log2(1.0 + jnp.exp2(w))


SB_DEAD_LOG2 = -160.0


def _sb_live(rests):
    worst = functools.reduce(jnp.maximum, rests)
    return (jnp.max(worst) > SB_DEAD_LOG2).astype(jnp.int32)


def _split(v):
    hi = v.astype(BF16)
    return hi, (v - hi.astype(F32)).astype(BF16)


def _sb_fwd(qkv, B, S, comm=None):
    W = qkv.shape[2] // 3
    n_pair = W // LANES
    bq = _tile(S, 256)
    nq = S // bq
    hp = SB_FWD_PAIRS_PER_PROGRAM
    nscale2 = -(SB_HEAD_DIM ** -0.5) * LOG2_E

    def body(q_ref, k_ref, v_ref, o_ref):
        lane = lax.broadcasted_iota(jnp.int32, (1, LANES), 1)
        head0 = lane < SB_HEAD_DIM
        rr = lax.broadcasted_iota(jnp.int32, (bq, bq), 0)
        cc = lax.broadcasted_iota(jnp.int32, (bq, bq), 1)
        strict = cc < rr
        after = jnp.where(rr > cc, 1.0, 0.0).astype(BF16)

        def blocks(heads, ks, carries, diag):
            n = range(len(heads))
            keep = (lambda t: jnp.where(strict, t, 0.0)) if diag else (lambda t: t)
            z = [_mm_nt(qh, k_ref[ks, cols]) for qh, cols in heads]
            wl = [_neg_log2_sigmoid(z[h] * nscale2) for h in n]
            lr = [keep(wl[h][0] - wl[h][1]) for h in n]
            parts = [_split(lr[h]) for h in n]
            suf = [_mm(parts[h][0], after) + _mm(parts[h][1], after) for h in n]
            a = [keep(jnp.exp2(suf[h] + carries[h][1] - wl[h][1])).astype(BF16) for h in n]
            o = [carries[h][0] + _mm(a[h], v_ref[ks, heads[h][1]]) for h in n]
            return tuple((o[h], carries[h][1] + (suf[h][:, :1] + lr[h][:, :1])) for h in n)

        def q_tile(i, _):
            qs = pl.ds(pl.multiple_of(i * bq, bq), bq)
            heads = []
            for pr in range(hp):
                cols = slice(pr * LANES, (pr + 1) * LANES)
                qv = q_ref[qs, cols]
                heads += [(jnp.where(head0, qv, jnp.zeros_like(qv)), cols),
                          (jnp.where(head0, jnp.zeros_like(qv), qv), cols)]
            zero = (jnp.zeros((bq, LANES), F32), jnp.zeros((bq, 1), F32))
            init = blocks(heads, qs, (zero,) * len(heads), True)

            def left(st):
                t, _, cr = st
                ks = pl.ds(pl.multiple_of((i - 1 - t) * bq, bq), bq)
                cr = blocks(heads, ks, cr, False)
                return t + 1, _sb_live([c for _, c in cr]), cr

            _, _, res = lax.while_loop(lambda st: jnp.logical_and(st[0] < i, st[1] > 0), left,
                                       (jnp.int32(0), _sb_live([c for _, c in init]), init))
            for pr in range(hp):
                o_ref[qs, heads[2 * pr][1]] = jnp.where(head0, res[2 * pr][0], res[2 * pr + 1][0]).astype(BF16)
            return 0

        lax.fori_loop(0, nq, q_tile, 0)

    def col(off):
        return pl.BlockSpec((None, S, hp * LANES), lambda b, p: (b, 0, off + p))

    n_pair //= hp
    return _call(
        body, name="sb_fwd", grid=(B, n_pair), comm=comm,
        in_specs=[col(0), col(n_pair), col(2 * n_pair)],
        out_specs=(col(0),),
        out_shape=(jax.ShapeDtypeStruct((B, S, W), BF16),),
        args=(qkv, qkv, qkv))


def _sb_bwd(qkv, do, B, S, comm=None):
    W = qkv.shape[2] // 3
    n_pair = W // LANES
    bq = _tile(S, 256)
    nq = S // bq
    hp = SB_PAIRS_PER_PROGRAM
    scale = SB_HEAD_DIM ** -0.5
    nscale2 = -scale * LOG2_E

    def body(q_ref, k_ref, v_ref, do_ref, dq_ref, dk_ref, dv_ref, dk_s, dv_s, e_s, sg_s, a_s):
        lane = lax.broadcasted_iota(jnp.int32, (1, LANES), 1)
        head0 = lane < SB_HEAD_DIM
        rr = lax.broadcasted_iota(jnp.int32, (bq, bq), 0)
        cc = lax.broadcasted_iota(jnp.int32, (bq, bq), 1)
        strict = cc < rr
        after = jnp.where(rr > cc, 1.0, 0.0).astype(BF16)
        before = jnp.where(rr < cc, 1.0, 0.0).astype(BF16)
        dk_s[...] = jnp.zeros_like(dk_s)
        dv_s[...] = jnp.zeros_like(dv_s)

        def weights(heads, ks, kb, rests, diag):
            n = range(len(heads))
            keep = (lambda t: jnp.where(strict, t, 0.0)) if diag else (lambda t: t)
            z = [_mm_nt(heads[h][0], k_ref[ks, heads[h][2]]) for h in n]
            da = [_mm_nt(heads[h][1], v_ref[ks, heads[h][2]]) for h in n]
            wl = [_neg_log2_sigmoid(z[h] * nscale2) for h in n]
            lr = [keep(wl[h][0] - wl[h][1]) for h in n]
            parts = [_split(lr[h]) for h in n]
            suf = [_mm(parts[h][0], after) + _mm(parts[h][1], after) for h in n]
            a = [keep(jnp.exp2(suf[h] + rests[h] - wl[h][1])) for h in n]
            for h in n:
                a_s[h * nq + kb] = a[h].astype(BF16)
                e_s[h * nq + kb] = a[h] * da[h]
                sg_s[h * nq + kb] = jnp.exp2(-wl[h][1])
            return tuple(rests[h] + (suf[h][:, :1] + lr[h][:, :1]) for h in n)

        def grads(heads, ks, kb, carries, diag):
            n = range(len(heads))
            keep = (lambda t: jnp.where(strict, t, 0.0)) if diag else (lambda t: t)
            e = [e_s[h * nq + kb] for h in n]
            parts = [_split(e[h]) for h in n]
            pex = [_mm(parts[h][0], before) + _mm(parts[h][1], before) for h in n]
            dz = [keep(e[h] - sg_s[h * nq + kb] * (e[h] + pex[h] + carries[h][1])).astype(BF16) for h in n]
            dq = [carries[h][0] + _mm(dz[h], k_ref[ks, heads[h][2]]) for h in n]
            for h in n:
                dk_s[ks, heads[h][2]] += _mm_tn(dz[h], heads[h][0])
                dv_s[ks, heads[h][2]] += _mm_tn(a_s[h * nq + kb], heads[h][1])
            return tuple((dq[h], carries[h][1] + (pex[h][:, bq - 1:] + e[h][:, bq - 1:])) for h in n)

        def q_tile(i, _):
            qs = pl.ds(pl.multiple_of(i * bq, bq), bq)
            heads = []
            for pr in range(hp):
                cols = slice(pr * LANES, (pr + 1) * LANES)
                qv, dov = q_ref[qs, cols], do_ref[qs, cols]
                zq, zd = jnp.zeros_like(qv), jnp.zeros_like(dov)
                heads += [(jnp.where(head0, qv, zq), jnp.where(head0, dov, zd), cols),
                          (jnp.where(head0, zq, qv), jnp.where(head0, zd, dov), cols)]
            key_block = lambda kb: pl.ds(pl.multiple_of(kb * bq, bq), bq)
            rests = weights(heads, qs, i, (jnp.zeros((bq, 1), F32),) * len(heads), True)

            def left(st):
                t, _, rs = st
                rs = weights(heads, key_block(i - 1 - t), i - 1 - t, rs, False)
                return t + 1, _sb_live(rs), rs

            n_left, _, _ = lax.while_loop(lambda st: jnp.logical_and(st[0] < i, st[1] > 0), left,
                                          (jnp.int32(0), _sb_live(rests), rests))
            zero = (jnp.zeros((bq, LANES), F32), jnp.zeros((bq, 1), F32))
            res = lax.fori_loop(0, n_left, lambda t, cr: grads(heads, key_block(i - n_left + t), i - n_left + t, cr, False),
                                (zero,) * len(heads))
            res = grads(heads, qs, i, res, True)
            for pr in range(hp):
                dq = jnp.where(head0, res[2 * pr][0], res[2 * pr + 1][0])
                dq_ref[qs, heads[2 * pr][2]] = (dq * scale).astype(BF16)
            return 0

        lax.fori_loop(0, nq, q_tile, 0)
        dk_ref[...] = (dk_s[...] * scale).astype(BF16)
        dv_ref[...] = dv_s[...].astype(BF16)

    def col(off):
        return pl.BlockSpec((None, S, hp * LANES), lambda b, p: (b, 0, off + p))

    n_pair //= hp
    shp = jax.ShapeDtypeStruct((B, S, W), BF16)
    slots = 2 * hp * nq
    return _call(
        body, name="sb_bwd", grid=(B, n_pair), comm=comm,
        in_specs=[col(0), col(n_pair), col(2 * n_pair), col(0)],
        out_specs=(col(0), col(0), col(0)),
        out_shape=(shp, shp, shp),
        scratch=[pltpu.VMEM((S, hp * LANES), F32), pltpu.VMEM((S, hp * LANES), F32),
                 pltpu.VMEM((slots, bq, bq), F32), pltpu.VMEM((slots, bq, bq), F32),
                 pltpu.VMEM((slots, bq, bq), BF16)],
        args=(qkv, qkv, qkv, do))


def _pool_counts(S):
    t = lax.broadcasted_iota(jnp.int32, (S, 1), 0)
    return t, [jnp.minimum(t + 1, w).astype(F32) for w in POOL_WINDOWS]


def _pool_fwd(u, B, S):
    W = u.shape[2]

    def body(u_ref, out_ref):
        t, counts = _pool_counts(S)
        for gi, w in enumerate(POOL_WINDOWS):
            cols = slice(gi * POOL_GROUP_DIM, (gi + 1) * POOL_GROUP_DIM)
            ug = u_ref[:, cols]
            s, k = ug, 1
            while k < w:
                s = s + jnp.where(t >= k, pltpu.roll(s, k, axis=0), 0.0)
                k *= 2
            out_ref[:, cols] = (s / counts[gi] - ug).astype(BF16)

    spec = pl.BlockSpec((None, S, W), lambda b: (b, 0, 0))
    return pl.pallas_call(
        body, name="pool_fwd", grid=(B,), in_specs=[spec], out_specs=spec,
        out_shape=jax.ShapeDtypeStruct((B, S, W), BF16), compiler_params=_params("arbitrary"),
    )(u)


def _pool_bwd(dpooled, B, S):
    W = dpooled.shape[2]

    def body(d_ref, out_ref):
        t, counts = _pool_counts(S)
        for gi, w in enumerate(POOL_WINDOWS):
            cols = slice(gi * POOL_GROUP_DIM, (gi + 1) * POOL_GROUP_DIM)
            d = d_ref[:, cols]
            s, k = d / counts[gi], 1
            while k < w:
                s = s + jnp.where(t < S - k, pltpu.roll(s, S - k, axis=0), 0.0)
                k *= 2
            out_ref[:, cols] = (s - d).astype(BF16)

    spec = pl.BlockSpec((None, S, W), lambda b: (b, 0, 0))
    return pl.pallas_call(
        body, name="pool_bwd", grid=(B,), in_specs=[spec], out_specs=spec,
        out_shape=jax.ShapeDtypeStruct((B, S, W), BF16), compiler_params=_params("arbitrary"),
    )(dpooled)


def _mix_out_fwd(x, o_sb, pooled, w_pool, pool_scale, w_out):
    T, D = x.shape
    W = o_sb.shape[1]
    G = w_pool.shape[0]
    gd = POOL_GROUP_DIM
    tm = _tile(T, 1024)

    def body(x_ref, osb_ref, pooled_ref, wp_ref, ps_ref, wo_ref, out_ref, mixed_ref):
        mixed_ref[:, :W] = osb_ref[...]
        for gi in range(G):
            cols = slice(gi * gd, (gi + 1) * gd)
            pw = _mm(pooled_ref[:, cols], wp_ref[gi])
            mixed_ref[:, W + gi * gd:W + (gi + 1) * gd] = (pw * ps_ref[:, cols]).astype(BF16)
        out_ref[...] = x_ref[...] + _mm(mixed_ref[...], wo_ref[...].reshape(D, D))

    row = lambda i: (i, 0)
    return pl.pallas_call(
        body, name="mix_out_fwd", grid=(T // tm,),
        in_specs=[pl.BlockSpec((tm, D), row), pl.BlockSpec((tm, W), row), pl.BlockSpec((tm, W), row),
                  pl.BlockSpec((G, gd, gd), lambda i: (0, 0, 0)), pl.BlockSpec((1, W), lambda i: (0, 0)),
                  pl.BlockSpec(w_out.shape, lambda i: (0, 0, 0))],
        out_specs=(pl.BlockSpec((tm, D), row), pl.BlockSpec((tm, D), row)),
        out_shape=(jax.ShapeDtypeStruct((T, D), F32), jax.ShapeDtypeStruct((T, D), BF16)),
        compiler_params=_params("arbitrary"),
    )(x, o_sb, pooled, w_pool, pool_scale, w_out)


def _mix_out_bwd(dx, pooled, w_pool, pool_scale, w_out):
    T, D = dx.shape
    W = pooled.shape[1]
    G = w_pool.shape[0]
    gd = POOL_GROUP_DIM
    tm = _tile(T, 1024)

    def body(dx_ref, pooled_ref, wp_ref, ps_ref, wo_ref, dxb_ref, dosb_ref, dpooled_ref, dwp_ref, dps_ref):
        i = pl.program_id(0)

        @pl.when(i == 0)
        def _():
            dwp_ref[...] = jnp.zeros_like(dwp_ref)
            dps_ref[...] = jnp.zeros_like(dps_ref)

        dxb = dx_ref[...].astype(BF16)
        dxb_ref[...] = dxb
        dmixed = _mm_nt(dxb, wo_ref[...].reshape(D, D))
        dosb_ref[...] = dmixed[:, :W].astype(BF16)
        for gi in range(G):
            cols = slice(gi * gd, (gi + 1) * gd)
            pg = pooled_ref[:, cols]
            dop = dmixed[:, W + gi * gd:W + (gi + 1) * gd]
            pw = _mm(pg, wp_ref[gi])
            dps_ref[:, cols] += jnp.sum(dop * pw, axis=0, keepdims=True)
            dpw = (dop * ps_ref[:, cols]).astype(BF16)
            dwp_ref[gi] += _mm_tn(pg, dpw)
            dpooled_ref[:, cols] = _mm_nt(dpw, wp_ref[gi])

    row = lambda i: (i, 0)
    return pl.pallas_call(
        body, name="mix_out_bwd", grid=(T // tm,),
        in_specs=[pl.BlockSpec((tm, D), row), pl.BlockSpec((tm, W), row),
                  pl.BlockSpec((G, gd, gd), lambda i: (0, 0, 0)), pl.BlockSpec((1, W), lambda i: (0, 0)),
                  pl.BlockSpec(w_out.shape, lambda i: (0, 0, 0))],
        out_specs=(pl.BlockSpec((tm, D), row), pl.BlockSpec((tm, W), row), pl.BlockSpec((tm, W), row),
                   pl.BlockSpec((G, gd, gd), lambda i: (0, 0, 0)), pl.BlockSpec((1, W), lambda i: (0, 0))),
        out_shape=(jax.ShapeDtypeStruct((T, D), BF16), jax.ShapeDtypeStruct((T, W), BF16),
                   jax.ShapeDtypeStruct((T, W), F32), jax.ShapeDtypeStruct((G, gd, gd), F32),
                   jax.ShapeDtypeStruct((1, W), F32)),
        compiler_params=_params("arbitrary"),
    )(dx, pooled, w_pool, pool_scale, w_out)


def _mem_kv_fwd(mem, gain, w_kvt):
    B, M, D = mem.shape
    C = w_kvt.shape[0]

    def body(mem_ref, gain_ref, w_ref, memn_ref, kv_ref):
        _, xhat = _rms(mem_ref[...])
        mn = (xhat * gain_ref[...]).astype(BF16)
        memn_ref[...] = mn
        kv_ref[...] = _mm_nt(mn, w_ref[...]).astype(BF16)

    return pl.pallas_call(
        body, name="mem_kv_fwd", grid=(B,),
        in_specs=[pl.BlockSpec((None, M, D), lambda b: (b, 0, 0)), pl.BlockSpec((1, D), lambda b: (0, 0)),
                  pl.BlockSpec((C, D), lambda b: (0, 0))],
        out_specs=(pl.BlockSpec((M, D), lambda b: (b, 0)), pl.BlockSpec((None, M, C), lambda b: (b, 0, 0))),
        out_shape=(jax.ShapeDtypeStruct((B * M, D), BF16), jax.ShapeDtypeStruct((B, M, C), BF16)),
        compiler_params=_params("arbitrary"),
    )(mem, gain, w_kvt)


def _mem_kv_bwd(dkv, mem, w_kvt):
    B, M, D = mem.shape
    C = w_kvt.shape[0]

    def body(dkv_ref, mem_ref, w_ref, dkvb_ref, dgain_ref):
        @pl.when(pl.program_id(0) == 0)
        def _():
            dgain_ref[...] = jnp.zeros_like(dgain_ref)

        dkvb = dkv_ref[...].astype(BF16)
        dkvb_ref[...] = dkvb
        dmn = _mm(dkvb, w_ref[...])
        _, xhat = _rms(mem_ref[...])
        dgain_ref[...] += jnp.sum(dmn * xhat, axis=0, keepdims=True)

    return pl.pallas_call(
        body, name="mem_kv_bwd", grid=(B,),
        in_specs=[pl.BlockSpec((None, M, C), lambda b: (b, 0, 0)), pl.BlockSpec((None, M, D), lambda b: (b, 0, 0)),
                  pl.BlockSpec((C, D), lambda b: (0, 0))],
        out_specs=(pl.BlockSpec((M, C), lambda b: (b, 0)), pl.BlockSpec((1, D), lambda b: (0, 0))),
        out_shape=(jax.ShapeDtypeStruct((B * M, C), BF16), jax.ShapeDtypeStruct((1, D), F32)),
        compiler_params=_params("arbitrary"),
    )(dkv, mem, w_kvt)


def _softmax_rows(s):
    p = jnp.exp(s - jnp.max(s, axis=1, keepdims=True))
    return p / jnp.sum(p, axis=1, keepdims=True)


def _cross_fwd(x, gain, kv, w_q, w_o, B, S, comm=None):
    T, D = x.shape
    M = kv.shape[1]
    hd = D // MEM_HEADS
    tm = _tile(S, 1024)
    per = S // tm
    scale = hd ** -0.5

    def body(x_ref, gain_ref, kv_ref, wq_ref, wo_ref, out_ref, hq_ref, q_ref, ocat_ref):
        _, xhat = _rms(x_ref[...])
        hq = (xhat * gain_ref[...]).astype(BF16)
        hq_ref[...] = hq
        q = _mm(hq, wq_ref[...].reshape(D, D)).astype(BF16)
        q_ref[...] = q
        for h in range(MEM_HEADS):
            cols = slice(h * hd, (h + 1) * hd)
            s = _mm_nt(q[:, cols], kv_ref[:, cols]) * scale
            p = _softmax_rows(s).astype(BF16)
            ocat_ref[:, cols] = _mm(p, kv_ref[:, D + h * hd:D + (h + 1) * hd]).astype(BF16)
        out_ref[...] = x_ref[...] + _mm(ocat_ref[...], wo_ref[...].reshape(D, D))

    row = lambda b, t: (b * per + t, 0)
    wspec = pl.BlockSpec(w_q.shape, lambda b, t: (0, 0, 0))
    return _call(
        body, name="cross_fwd", grid=(B, per), comm=comm,
        in_specs=[pl.BlockSpec((tm, D), row), pl.BlockSpec((1, D), lambda b, t: (0, 0)),
                  pl.BlockSpec((None, M, 2 * D), lambda b, t: (b, 0, 0)), wspec, wspec],
        out_specs=tuple(pl.BlockSpec((tm, D), row) for _ in range(4)),
        out_shape=(jax.ShapeDtypeStruct((T, D), F32),) + tuple(jax.ShapeDtypeStruct((T, D), BF16) for _ in range(3)),
        args=(x, gain, kv, w_q, w_o))


def _cross_bwd(dy, x, gain, q, kv, w_q, w_o, B, S, comm=None):
    T, D = x.shape
    M = kv.shape[1]
    hd = D // MEM_HEADS
    tm = _tile(S, 512)
    per = S // tm
    scale = hd ** -0.5

    def body(dy_ref, x_ref, gain_ref, q_ref, kv_ref, wq_ref, wo_ref,
             dx_ref, dyb_ref, dqb_ref, dkv_ref, dgain_ref):
        b_id, t_id = pl.program_id(0), pl.program_id(1)

        @pl.when((b_id == 0) & (t_id == 0))
        def _():
            dgain_ref[...] = jnp.zeros_like(dgain_ref)

        @pl.when(t_id == 0)
        def _():
            dkv_ref[...] = jnp.zeros_like(dkv_ref)

        dyb = dy_ref[...].astype(BF16)
        dyb_ref[...] = dyb
        docat = _mm_nt(dyb, wo_ref[...].reshape(D, D)).astype(BF16)
        for h in range(MEM_HEADS):
            cols = slice(h * hd, (h + 1) * hd)
            vcols = slice(D + h * hd, D + (h + 1) * hd)
            qh, kh, vh, doh = q_ref[:, cols], kv_ref[:, cols], kv_ref[:, vcols], docat[:, cols]
            p = _softmax_rows(_mm_nt(qh, kh) * scale)
            dp = _mm_nt(doh, vh)
            ds = (p * (dp - jnp.sum(dp * p, axis=1, keepdims=True)) * scale).astype(BF16)
            dqb_ref[:, cols] = _mm(ds, kh).astype(BF16)
            dkv_ref[:, cols] += _mm_tn(ds, qh)
            dkv_ref[:, vcols] += _mm_tn(p.astype(BF16), doh)
        dhq = _mm_nt(dqb_ref[...], wq_ref[...].reshape(D, D))
        r, xhat = _rms(x_ref[...])
        dgain_ref[...] += jnp.sum(dhq * xhat, axis=0, keepdims=True)
        dx_ref[...] = dy_ref[...] + _rms_bwd(dhq, gain_ref[...], r, xhat)

    row = lambda b, t: (b * per + t, 0)
    wspec = pl.BlockSpec(w_q.shape, lambda b, t: (0, 0, 0))
    one = pl.BlockSpec((1, D), lambda b, t: (0, 0))
    kvspec = pl.BlockSpec((None, M, 2 * D), lambda b, t: (b, 0, 0))
    return _call(
        body, name="cross_bwd", grid=(B, per), comm=comm,
        in_specs=[pl.BlockSpec((tm, D), row), pl.BlockSpec((tm, D), row), one, pl.BlockSpec((tm, D), row),
                  kvspec, wspec, wspec],
        out_specs=(pl.BlockSpec((tm, D), row), pl.BlockSpec((tm, D), row), pl.BlockSpec((tm, D), row), kvspec, one),
        out_shape=(jax.ShapeDtypeStruct((T, D), F32), jax.ShapeDtypeStruct((T, D), BF16),
                   jax.ShapeDtypeStruct((T, D), BF16), jax.ShapeDtypeStruct((B, M, 2 * D), F32),
                   jax.ShapeDtypeStruct((1, D), F32)),
        args=(dy, x, gain, q, kv, w_q, w_o))


def _ordered_sum(gp_ref):
    g = gp_ref[0].astype(F32)
    for s in range(1, gp_ref.shape[0]):
        g = g + gp_ref[s].astype(F32)
    return g


def _adam_write(g, w_ref, m_ref, v_ref, g_ref, d_ref, nm_ref, nv_ref):
    nm = ADAM_B1 * m_ref[...] + (1.0 - ADAM_B1) * g
    nv = ADAM_B2 * v_ref[...] + (1.0 - ADAM_B2) * (g * g)
    m_hat = nm / (1.0 - ADAM_B1 ** ADAM_STEP)
    v_hat = nv / (1.0 - ADAM_B2 ** ADAM_STEP)
    g_ref[...] = g
    nm_ref[...] = nm
    nv_ref[...] = nv
    d_ref[...] = -ADAM_LR * (m_hat / (jnp.sqrt(v_hat) + ADAM_EPS) + ADAM_WD * w_ref[...])


def _adamw(items, name):
    R = items[0][1].shape[0]
    tr = next(t for t in (_tile(R, 128), _tile(R, 256)) if t % 16 == 0)
    n = len(items)

    def body(*refs):
        ins, outs = refs[:4 * n], refs[4 * n:]
        for k in range(n):
            gp_ref, w_ref, m_ref, v_ref = ins[4 * k:4 * k + 4]
            _adam_write(_ordered_sum(gp_ref), w_ref, m_ref, v_ref, *outs[4 * k:4 * k + 4])

    in_specs, out_specs, out_shape = [], [], []
    for parts, w, _, _ in items:
        C = w.shape[1]
        spec = pl.BlockSpec((tr, C), lambda i: (i, 0))
        in_specs += [pl.BlockSpec((parts.shape[0], tr, C), lambda i: (0, i, 0)), spec, spec, spec]
        out_specs += [spec] * 4
        out_shape += [jax.ShapeDtypeStruct(w.shape, F32)] * 4
    res = pl.pallas_call(
        body, name=name, grid=(R // tr,), in_specs=in_specs, out_specs=tuple(out_specs), out_shape=tuple(out_shape),
        compiler_params=_params("arbitrary"),
    )(*[a for item in items for a in item])
    return [res[4 * k:4 * k + 4] for k in range(n)]


def _pack_rows(vectors, D):
    def body(*refs):
        out_ref = refs[-1]
        out_ref[...] = jnp.zeros_like(out_ref)
        for i, r in enumerate(refs[:-1]):
            out_ref[i:i + 1, :r.shape[1]] = r[...]

    vmem = pl.BlockSpec(memory_space=pltpu.VMEM)
    return pl.pallas_call(body, name="pack_small", in_specs=[vmem] * len(vectors), out_specs=vmem,
                          out_shape=jax.ShapeDtypeStruct((8, D), F32))(*vectors)


def _adamw_small(row_parts, mat_parts, vectors, matrix):
    n = len(vectors)

    def body(*refs):
        rp_ref, mp_ref = refs[0], refs[1]
        ins, outs = refs[2:2 + 3 * (n + 1)], refs[2 + 3 * (n + 1):]
        rows = _ordered_sum(rp_ref)
        for i in range(n):
            c = ins[3 * i].shape[1]
            _adam_write(rows[i:i + 1, :c], *ins[3 * i:3 * i + 3], *outs[4 * i:4 * i + 4])
        _adam_write(_ordered_sum(mp_ref), *ins[3 * n:3 * n + 3], *outs[4 * n:4 * n + 4])
        outs[-1][...] = rows[n:n + 1, :]

    flat = [a for wmv in vectors for a in wmv] + list(matrix)
    out_shape = [jax.ShapeDtypeStruct(wmv[0].shape, F32) for wmv in list(vectors) + [matrix] for _ in range(4)]
    out_shape.append(jax.ShapeDtypeStruct((1, row_parts.shape[2]), F32))
    vmem = pl.BlockSpec(memory_space=pltpu.VMEM)
    res = pl.pallas_call(body, name="adamw_small", in_specs=[vmem] * (2 + len(flat)),
                         out_specs=tuple([vmem] * len(out_shape)), out_shape=tuple(out_shape))(row_parts, mat_parts, *flat)
    return [res[4 * i:4 * i + 4] for i in range(n + 1)], res[-1]


def kernel(x, mem, ffn1_norm, ffn1_w_gate, ffn1_w_up, ffn1_w_down, mix_norm, w_in, w_pool, pool_scale, w_out, mem_q_norm, mem_kv_norm, mem_w_q, mem_w_kv, mem_w_o, ffn2_norm, ffn2_w_gate, ffn2_w_up, ffn2_w_down, final_norm, loss_target, m_ffn1_norm, m_ffn1_w_gate, m_ffn1_w_up, m_ffn1_w_down, m_mix_norm, m_w_in, m_w_pool, m_pool_scale, m_w_out, m_mem_q_norm, m_mem_kv_norm, m_mem_w_q, m_mem_w_kv, m_mem_w_o, m_ffn2_norm, m_ffn2_w_gate, m_ffn2_w_up, m_ffn2_w_down, m_final_norm, v_ffn1_norm, v_ffn1_w_gate, v_ffn1_w_up, v_ffn1_w_down, v_mix_norm, v_w_in, v_w_pool, v_pool_scale, v_w_out, v_mem_q_norm, v_mem_kv_norm, v_mem_w_q, v_mem_w_kv, v_mem_w_o, v_ffn2_norm, v_ffn2_w_gate, v_ffn2_w_up, v_ffn2_w_down, v_final_norm):
    B, S, D = x.shape
    T = B * S
    x0 = x.reshape(T, D)
    target = loss_target.reshape(T, D)
    final_gain = final_norm.reshape(1, D)

    big = dict(
        g1=ffn1_w_gate[0].T, u1=ffn1_w_up[0].T, d1=ffn1_w_down[0],
        g2=ffn2_w_gate[0].T, u2=ffn2_w_up[0].T, d2=ffn2_w_down[0],
        w_in=w_in[0].T, w_out=w_out[0], w_q=mem_w_q[0], w_kv=mem_w_kv[0].T, w_o=mem_w_o[0])
    names = list(big)
    shard = {k: big[k].astype(BF16) for k in names}
    wp = w_pool[0].astype(BF16)
    full, ffn_w = {}, {}
    stacked = ("g1", "u1", "d1", "g2", "u2", "d2", "w_in", "w_kv")

    def gathered(keys, arrs):
        full.update(zip(keys, arrs))
        ffn_w.update({k: full[k].reshape(-1, D) for k in keys if k in stacked})

    first, mid = ("g1", "u1", "d1"), ("w_in", "w_out", "w_q", "w_kv", "w_o")
    gathered(first, _gather_two_level([shard[k] for k in first], "gather_ffn1"))
    (x1, hn1, a1, s1, t1), got = _ffn_fwd(x0, ffn1_norm, ffn_w["g1"], ffn_w["u1"], ffn_w["d1"], "ffn1_fwd",
                                          comm=([shard[k] for k in mid], GATHER2))
    gathered(mid, got)
    hn2, qkv, u = _mix_in_fwd(x1, mix_norm, ffn_w["w_in"])
    qkv3 = qkv.reshape(B, S, -1)
    (o_sb,), got = _sb_fwd(qkv3, B, S, comm=([shard["g2"], shard["u2"]], GATHER2))
    gathered(("g2", "u2"), got)
    pooled = _pool_fwd(u.reshape(B, S, -1), B, S).reshape(T, -1)
    x2, mixed = _mix_out_fwd(x1, o_sb.reshape(T, -1), pooled, wp, pool_scale, full["w_out"])
    memn, kv = _mem_kv_fwd(mem, mem_kv_norm, ffn_w["w_kv"])
    (x3, hq, q, ocat), got = _cross_fwd(x2, mem_q_norm, kv, full["w_q"], full["w_o"], B, S,
                                        comm=([shard["d2"]], GATHER2))
    gathered(("d2",), got)
    (dx4, hn4, a2, s2, t2, d_final, loss_part), _ = _ffn_fwd(x3, ffn2_norm, ffn_w["g2"], ffn_w["u2"], ffn_w["d2"],
                                                            "ffn2_fwd", head=(final_gain, target))

    slab = lambda k: grads[k].reshape((N_DEV, -1) + grads[k].shape[-1:])
    got = {}
    dx3, dg2, du2, dyh2, d_ffn2 = _ffn_bwd(dx4, x3, ffn2_norm, s2, t2, ffn_w["g2"], ffn_w["u2"],
                                          ffn_w["d2"], "ffn2_bwd")
    ffn_slab = ffn1_w_gate.shape[2]
    grads = dict(g2=_wgrad(hn4, dg2, "dw_gate2", col_slab=ffn_slab), u2=_wgrad(hn4, du2, "dw_up2", col_slab=ffn_slab),
                 d2=_wgrad(a2, dyh2, "dw_down2"))
    (dx2, dx3b, dqb, dkv, d_q), (got["g2"],) = _cross_bwd(dx3, x2, mem_q_norm, q, kv, full["w_q"], full["w_o"], B, S,
                                                         comm=([slab("g2")], False))
    grads["w_o"] = _wgrad(ocat, dx3b, "dw_o")
    grads["w_q"] = _wgrad(hq, dqb, "dw_q")
    dkvb, d_kv = _mem_kv_bwd(dkv, mem, ffn_w["w_kv"])
    grads["w_kv"] = _wgrad(memn, dkvb, "dw_kv", col_slab=mem_w_kv.shape[2])
    dx2b, do_sb, dpooled, d_wpool, d_ps = _mix_out_bwd(dx2, pooled, wp, pool_scale, full["w_out"])
    grads["w_out"] = _wgrad(mixed, dx2b, "dw_out")
    du = _pool_bwd(dpooled.reshape(B, S, -1), B, S).reshape(T, -1)
    early = ("u2", "d2", "w_o", "w_q", "w_kv", "w_out")
    (dq, dk, dv), res = _sb_bwd(qkv3, do_sb.reshape(B, S, -1), B, S, comm=([slab(k) for k in early], False))
    got.update(zip(early, res))
    dx1, dproj, d_mix = _mix_in_bwd(dx2, dq.reshape(T, -1), dk.reshape(T, -1), dv.reshape(T, -1), du,
                                    x1, mix_norm, ffn_w["w_in"])
    grads["w_in"] = _wgrad(hn2, dproj, "dw_in", col_slab=w_in.shape[2])
    dx0, dg1, du1, dyh1, d_ffn1 = _ffn_bwd(dx1, x0, ffn1_norm, s1, t1, ffn_w["g1"], ffn_w["u1"],
                                          ffn_w["d1"], "ffn1_bwd")

    small = [("ffn1_norm", d_ffn1, ffn1_norm, m_ffn1_norm, v_ffn1_norm),
             ("mix_norm", d_mix, mix_norm, m_mix_norm, v_mix_norm),
             ("pool_scale", d_ps, pool_scale, m_pool_scale, v_pool_scale),
             ("mem_q_norm", d_q, mem_q_norm, m_mem_q_norm, v_mem_q_norm),
             ("mem_kv_norm", d_kv, mem_kv_norm, m_mem_kv_norm, v_mem_kv_norm),
             ("ffn2_norm", d_ffn2, ffn2_norm, m_ffn2_norm, v_ffn2_norm),
             ("final_norm", d_final, final_gain, m_final_norm.reshape(1, D), v_final_norm.reshape(1, D))]
    row_pack = _pack_rows([t[1] for t in small] + [loss_part], D)
    as_rows = lambda t: t.reshape(-1, LANES)
    grads["g1"], (row_parts, pool_parts) = _wgrad(hn1, dg1, "dw_gate1", col_slab=ffn_slab,
                                                  comm=([row_pack, as_rows(d_wpool)], True))
    grads["u1"], (got["g1"], got["w_in"]) = _wgrad(hn1, du1, "dw_up1", col_slab=ffn_slab,
                                                   comm=([slab("g1"), slab("w_in")], PAIRSUM))
    grads["d1"], (got["u1"],) = _wgrad(a1, dyh1, "dw_down1", comm=([slab("u1")], PAIRSUM))
    got["d1"] = _pairsum_exchange([slab("d1")], "scatter_last")[0]

    state = dict(
        g1=(ffn1_w_gate, m_ffn1_w_gate, v_ffn1_w_gate), u1=(ffn1_w_up, m_ffn1_w_up, v_ffn1_w_up),
        d1=(ffn1_w_down, m_ffn1_w_down, v_ffn1_w_down), g2=(ffn2_w_gate, m_ffn2_w_gate, v_ffn2_w_gate),
        u2=(ffn2_w_up, m_ffn2_w_up, v_ffn2_w_up), d2=(ffn2_w_down, m_ffn2_w_down, v_ffn2_w_down),
        w_in=(w_in, m_w_in, v_w_in), w_out=(w_out, m_w_out, v_w_out), w_q=(mem_w_q, m_mem_w_q, v_mem_w_q),
        w_kv=(mem_w_kv, m_mem_w_kv, v_mem_w_kv), w_o=(mem_w_o, m_mem_w_o, v_mem_w_o))
    big_out = {}
    groups = dict(gate_up=("g1", "u1", "g2", "u2"), in_kv=("w_in", "w_kv"), square=("w_out", "w_q", "w_o"),
                  down=("d1", "d2"))
    for label, keys in groups.items():
        items = [(got[k],) + tuple(t[0] for t in state[k]) for k in keys]
        for k, outs in zip(keys, _adamw(items, "adamw_" + label)):
            big_out[k] = [t[None] for t in outs]

    small_res, loss_row = _adamw_small(row_parts, pool_parts, [t[2:] for t in small],
                                       [as_rows(t) for t in (w_pool, m_w_pool, v_w_pool)])
    small_out = {t[0]: small_res[i] for i, t in enumerate(small)}
    small_out["final_norm"] = [t.reshape(D) for t in small_out["final_norm"]]
    small_out["w_pool"] = [t.reshape(w_pool.shape) for t in small_res[-1]]
    loss = loss_row[0, 0]

    order = [("ffn1_norm", None), ("ffn1_w_gate", "g1"), ("ffn1_w_up", "u1"), ("ffn1_w_down", "d1"),
             ("mix_norm", None), ("w_in", "w_in"), ("w_pool", None), ("pool_scale", None), ("w_out", "w_out"),
             ("mem_q_norm", None), ("mem_kv_norm", None), ("mem_w_q", "w_q"), ("mem_w_kv", "w_kv"),
             ("mem_w_o", "w_o"), ("ffn2_norm", None), ("ffn2_w_gate", "g2"), ("ffn2_w_up", "u2"),
             ("ffn2_w_down", "d2"), ("final_norm", None)]
    res = [loss, dx0.reshape(B, S, D)]
    for which in range(4):
        for name, key in order:
            res.append(big_out[key][which] if key else small_out[name][which])
    return tuple(res)
```

```python
import functools
import math

import jax
import jax.numpy as jnp
from jax import lax
from jax.experimental import pallas as pl
from jax.experimental.pallas import tpu as pltpu

F32 = jnp.float32
BF16 = jnp.bfloat16

N_DEV = 8
EPS = 1e-6
SB_HEAD_DIM = 64
LANES = 128
POOL_WINDOWS = (2, 4, 8, 16)
POOL_GROUP_DIM = 128
MEM_HEADS = 4
FFN_RESIDUAL_WEIGHT = 0.5
ADAM_LR = 0.001
ADAM_B1 = 0.9
ADAM_B2 = 0.999
ADAM_EPS = 1e-08
ADAM_WD = 0.01
ADAM_STEP = 10
VMEM_LIMIT = 56 * 1024 * 1024

MESH_ID = pl.DeviceIdType.MESH


def _params(*sem):
    return pltpu.CompilerParams(dimension_semantics=sem, vmem_limit_bytes=VMEM_LIMIT)


def _tile(n, pref):
    if n <= pref:
        return n
    t = pref - pref % 8
    while n % t:
        t -= 8
    return t


def _mm(a, b):
    return jnp.dot(a, b, preferred_element_type=F32)


def _mm_nt(a, b):
    return lax.dot_general(a, b, (((1,), (1,)), ((), ())), preferred_element_type=F32)


def _mm_tn(a, b):
    return lax.dot_general(a, b, (((0,), (0,)), ((), ())), preferred_element_type=F32)


def _rms(xv):
    r = lax.rsqrt(jnp.mean(xv * xv, axis=-1, keepdims=True) + EPS)
    return r, xv * r


def _rms_bwd(dhn, gain, r, xhat):
    dxh = dhn * gain
    return r * (dxh - xhat * jnp.mean(dxh * xhat, axis=-1, keepdims=True))


def _sigmoid(z):
    return 0.5 * jnp.tanh(0.5 * z) + 0.5


def _flags(arrs, gather):
    return [gather] * len(arrs) if isinstance(gather, bool) else list(gather)


def _comm_shapes(arrs, gather):
    return tuple(jax.ShapeDtypeStruct(((N_DEV,) + tuple(a.shape)) if f else tuple(a.shape), a.dtype)
                 for a, f in zip(arrs, _flags(arrs, gather)))


def _comm_start(ins, outs, sems, gather):
    send_sems, recv_sems, local_sems = sems
    gather = _flags(ins, gather)
    x, y, c = lax.axis_index("x"), lax.axis_index("y"), lax.axis_index("c")
    me = 4 * x + 2 * y + c
    for i in range(len(ins)):
        src = ins[i] if gather[i] else ins[i].at[me]
        pltpu.make_async_copy(src, outs[i].at[me], local_sems.at[i]).start()
    for k in range(1, N_DEV):
        px = 1 - x if k & 4 else x
        py = 1 - y if k & 2 else y
        pc = 1 - c if k & 1 else c
        peer = 4 * px + 2 * py + pc
        for i in range(len(ins)):
            src = ins[i] if gather[i] else ins[i].at[peer]
            pltpu.make_async_remote_copy(
                src_ref=src, dst_ref=outs[i].at[me],
                send_sem=send_sems.at[i], recv_sem=recv_sems.at[i],
                device_id=(px, py, pc), device_id_type=MESH_ID).start()


def _comm_wait(ins, outs, sems, gather):
    send_sems, recv_sems, local_sems = sems
    gather = _flags(ins, gather)
    x, y, c = lax.axis_index("x"), lax.axis_index("y"), lax.axis_index("c")
    me = 4 * x + 2 * y + c
    for i in range(len(ins)):
        seven = outs[i].at[pl.ds(0, N_DEV - 1)]
        done = pltpu.make_async_remote_copy(
            src_ref=seven, dst_ref=seven,
            send_sem=send_sems.at[i], recv_sem=recv_sems.at[i],
            device_id=(x, y, c), device_id_type=MESH_ID)
        done.wait_send()
        done.wait_recv()
        src = ins[i] if gather[i] else ins[i].at[me]
        pltpu.make_async_copy(src, outs[i].at[me], local_sems.at[i]).wait()


def _comm_sems(n):
    return [pltpu.SemaphoreType.DMA((n,)) for _ in range(3)]


def _exchange(arrs, gather, name):
    n = len(arrs)

    def body(*refs):
        ins, outs, sems = refs[:n], refs[n:2 * n], refs[2 * n:]
        _comm_start(ins, outs, sems, gather)
        _comm_wait(ins, outs, sems, gather)

    any_spec = pl.BlockSpec(memory_space=pl.ANY)
    outs = pl.pallas_call(
        body, name=name, out_shape=_comm_shapes(arrs, gather),
        in_specs=[any_spec] * n, out_specs=tuple([any_spec] * n), scratch_shapes=_comm_sems(n),
    )(*arrs)
    return list(outs)


GATHER2 = "gather2"


def _gather2_copies(ins, outs, sems):
    n = len(ins)
    send_sems, recv_sems, local_sems = sems
    x, y, c = lax.axis_index("x"), lax.axis_index("y"), lax.axis_index("c")
    me, sibling = (x, y, c), (x, y, 1 - c)
    chips = [(1 - x, y), (x, 1 - y), (1 - x, 1 - y)]

    def copy(i, k, block, to, own=False):
        slab = outs[i].at[4 * block[0] + 2 * block[1] + block[2]]
        return pltpu.make_async_remote_copy(
            src_ref=ins[i] if own else slab, dst_ref=slab,
            send_sem=send_sems.at[i, k], recv_sem=recv_sems.at[i, k],
            device_id=to, device_id_type=MESH_ID)

    each = [(j, chip, i) for j, chip in enumerate(chips) for i in range(n)]
    return dict(
        mine=lambda: [pltpu.make_async_copy(ins[i], outs[i].at[4 * x + 2 * y + c], local_sems.at[i]) for i in range(n)],
        first=lambda: [copy(i, 0, me, sibling, own=True) for i in range(n)]
        + [copy(i, 1 + j, me, (*chip, c), own=True) for j, chip, i in each],
        landed=lambda: [copy(i, 1 + j, (*chip, c), me) for j, chip, i in each],
        passed=lambda: [copy(i, 4 + j, (*chip, c), sibling) for j, chip, i in each],
        from_sibling=lambda: [copy(i, 0, sibling, me) for i in range(n)]
        + [copy(i, 4 + j, (*chip, 1 - c), me) for j, chip, i in each])


def _gather2_start(ins, outs, sems):
    make = _gather2_copies(ins, outs, sems)
    for cp in make["mine"]() + make["first"]():
        cp.start()


def _gather2_middle(ins, outs, sems):
    make = _gather2_copies(ins, outs, sems)
    for arrival, onward in zip(make["landed"](), make["passed"]()):
        arrival.wait_recv()
        onward.start()


def _gather2_wait(ins, outs, sems):
    make = _gather2_copies(ins, outs, sems)
    for cp in make["from_sibling"]():
        cp.wait_recv()
    for cp in make["first"]() + make["passed"]():
        cp.wait_send()
    for cp in make["mine"]():
        cp.wait()


def _gather2_sems(n):
    return [pltpu.SemaphoreType.DMA((n, N_DEV - 1)), pltpu.SemaphoreType.DMA((n, N_DEV - 1)),
            pltpu.SemaphoreType.DMA((n,))]


def _gather_two_level(arrs, name):
    n = len(arrs)

    def body(*refs):
        ins, outs, sems = refs[:n], refs[n:2 * n], refs[2 * n:]
        _gather2_start(ins, outs, sems)
        _gather2_middle(ins, outs, sems)
        _gather2_wait(ins, outs, sems)

    any_spec = pl.BlockSpec(memory_space=pl.ANY)
    outs = pl.pallas_call(
        body, name=name, out_shape=_comm_shapes(arrs, True),
        in_specs=[any_spec] * n, out_specs=tuple([any_spec] * n), scratch_shapes=_gather2_sems(n),
    )(*arrs)
    return list(outs)


CHIPS = N_DEV // 2
PAIRSUM = "pairsum"


def _pairsum_shapes(arrs):
    return tuple(jax.ShapeDtypeStruct((CHIPS,) + tuple(a.shape[1:]), a.dtype) for a in arrs)


def _pairsum_scratch(arrs):
    bufs = [pltpu.VMEM((CHIPS,) + tuple(a.shape[1:]), a.dtype) for a in arrs for _ in range(3)]
    return bufs + [pltpu.SemaphoreType.DMA((len(arrs),)) for _ in range(6)]


def _pairsum_start(ins, outs, sc):
    n = len(ins)
    load, d2d_send, d2d_recv = sc[3 * n], sc[3 * n + 1], sc[3 * n + 2]
    x, y, c = lax.axis_index("x"), lax.axis_index("y"), lax.axis_index("c")
    for i in range(n):
        mine, theirs = sc[3 * i], sc[3 * i + 1]
        for q in range(CHIPS):
            pltpu.make_async_copy(ins[i].at[2 * q + c], mine.at[q], load.at[i]).start()
            pltpu.make_async_remote_copy(
                src_ref=ins[i].at[2 * q + (1 - c)], dst_ref=theirs.at[q], send_sem=d2d_send.at[i],
                recv_sem=d2d_recv.at[i], device_id=(x, y, 1 - c), device_id_type=MESH_ID).start()


def _pairsum_middle(ins, outs, sc):
    n = len(ins)
    load, d2d_send, d2d_recv, ici_send, ici_recv, store = sc[3 * n:3 * n + 6]
    x, y, c = lax.axis_index("x"), lax.axis_index("y"), lax.axis_index("c")
    here = 2 * x + y
    for i in range(n):
        mine, theirs, total = sc[3 * i:3 * i + 3]
        pltpu.make_async_copy(mine, mine, load.at[i]).wait()
        pltpu.make_async_remote_copy(src_ref=theirs, dst_ref=theirs, send_sem=d2d_send.at[i], recv_sem=d2d_recv.at[i],
                                     device_id=(x, y, c), device_id_type=MESH_ID).wait_recv()
        for q in range(CHIPS):
            total[q] = (mine[q].astype(F32) + theirs[q].astype(F32)).astype(total.dtype)
        for k in range(1, CHIPS):
            px = 1 - x if k & 2 else x
            py = 1 - y if k & 1 else y
            pltpu.make_async_remote_copy(
                src_ref=total.at[2 * px + py], dst_ref=outs[i].at[here], send_sem=ici_send.at[i],
                recv_sem=ici_recv.at[i], device_id=(px, py, c), device_id_type=MESH_ID).start()
        pltpu.make_async_copy(total.at[here], outs[i].at[here], store.at[i]).start()


def _pairsum_wait(ins, outs, sc):
    n = len(ins)
    load, d2d_send, d2d_recv, ici_send, ici_recv, store = sc[3 * n:3 * n + 6]
    x, y, c = lax.axis_index("x"), lax.axis_index("y"), lax.axis_index("c")
    here = 2 * x + y
    for i in range(n):
        theirs, total = sc[3 * i + 1], sc[3 * i + 2]
        three = outs[i].at[pl.ds(0, CHIPS - 1)]
        pltpu.make_async_remote_copy(src_ref=theirs, dst_ref=theirs, send_sem=d2d_send.at[i], recv_sem=d2d_recv.at[i],
                                     device_id=(x, y, c), device_id_type=MESH_ID).wait_send()
        done = pltpu.make_async_remote_copy(src_ref=three, dst_ref=three, send_sem=ici_send.at[i],
                                            recv_sem=ici_recv.at[i], device_id=(x, y, c), device_id_type=MESH_ID)
        done.wait_send()
        done.wait_recv()
        pltpu.make_async_copy(total.at[here], outs[i].at[here], store.at[i]).wait()


def _pairsum_exchange(arrs, name):
    n = len(arrs)

    def body(*refs):
        ins, outs, sc = refs[:n], refs[n:2 * n], refs[2 * n:]
        _pairsum_start(ins, outs, sc)
        _pairsum_middle(ins, outs, sc)
        _pairsum_wait(ins, outs, sc)

    any_spec = pl.BlockSpec(memory_space=pl.ANY)
    outs = pl.pallas_call(
        body, name=name, out_shape=_pairsum_shapes(arrs), in_specs=[any_spec] * n, out_specs=tuple([any_spec] * n),
        scratch_shapes=_pairsum_scratch(arrs), compiler_params=pltpu.CompilerParams(vmem_limit_bytes=VMEM_LIMIT),
    )(*arrs)
    return list(outs)


def _call(body, *, name, grid, in_specs, out_specs, out_shape, args, scratch=(), comm=None):
    sem = ("arbitrary",) * len(grid)
    if comm is None:
        res = pl.pallas_call(body, name=name, grid=grid, in_specs=list(in_specs), out_specs=tuple(out_specs),
                             out_shape=tuple(out_shape), scratch_shapes=list(scratch),
                             compiler_params=_params(*sem))(*args)
        return tuple(res), []
    arrs, gather = comm
    n, n_in, n_out, n_sc = len(arrs), len(args), len(out_shape), len(scratch)
    steps = math.prod(grid)
    if isinstance(gather, str) and gather == PAIRSUM:
        start, middle, wait = _pairsum_start, _pairsum_middle, _pairsum_wait
        shapes, extra, mid_step = _pairsum_shapes(arrs), _pairsum_scratch(arrs), steps // 4
    elif isinstance(gather, str) and gather == GATHER2:
        start, middle, wait = _gather2_start, _gather2_middle, _gather2_wait
        shapes, extra, mid_step = _comm_shapes(arrs, True), _gather2_sems(n), 3 * steps // 4
    else:
        start = functools.partial(_comm_start, gather=gather)
        wait = functools.partial(_comm_wait, gather=gather)
        middle, shapes, extra, mid_step = None, _comm_shapes(arrs, gather), _comm_sems(n), 0

    def wrapped(*refs):
        ins, cin = refs[:n_in], refs[n_in:n_in + n]
        outs, cout = refs[n_in + n:n_in + n + n_out], refs[n_in + n + n_out:n_in + 2 * n + n_out]
        sc, sems = refs[n_in + 2 * n + n_out:n_in + 2 * n + n_out + n_sc], refs[n_in + 2 * n + n_out + n_sc:]
        step = functools.reduce(lambda acc, ax: acc * grid[ax] + pl.program_id(ax), range(len(grid)), 0)

        @pl.when(step == 0)
        def _():
            start(cin, cout, sems)

        if middle:
            @pl.when(step == mid_step)
            def _():
                middle(cin, cout, sems)

        body(*ins, *outs, *sc)

        @pl.when(step == steps - 1)
        def _():
            wait(cin, cout, sems)

    any_spec = pl.BlockSpec(memory_space=pl.ANY)
    res = pl.pallas_call(
        wrapped, name=name, grid=grid, in_specs=list(in_specs) + [any_spec] * n,
        out_specs=tuple(out_specs) + (any_spec,) * n, out_shape=tuple(out_shape) + tuple(shapes),
        scratch_shapes=list(scratch) + list(extra), compiler_params=_params(*sem))(*args, *arrs)
    return tuple(res[:n_out]), list(res[n_out:])


FFN_BWD_ROWS = 256


def _load_resident(pairs, sem):
    copies = [pltpu.make_async_copy(src, dst, sem.at[k]) for k, (src, dst) in enumerate(pairs)]
    for cp in copies:
        cp.start()
    for cp in copies:
        cp.wait()


def _ffn_fwd(x, gain, wgt, wut, wd, name, comm=None, head=None):
    T, D = x.shape
    F = wd.shape[0]
    tm, tf = _tile(T, 512), _tile(F, 256)

    def body(*refs):
        if head:
            (x_ref, gain_ref, wg_hbm, wu_hbm, wd_hbm, fgain_ref, tgt_ref,
             out_ref, hn_ref, a_ref, s_ref, t_ref, dfgain_ref, loss_ref, wg_s, wu_s, wd_s, sem) = refs
        else:
            (x_ref, gain_ref, wg_hbm, wu_hbm, wd_hbm,
             out_ref, hn_ref, a_ref, s_ref, t_ref, wg_s, wu_s, wd_s, sem) = refs

        @pl.when(pl.program_id(0) == 0)
        def _():
            _load_resident([(wg_hbm, wg_s), (wu_hbm, wu_s), (wd_hbm, wd_s)], sem)
            if head:
                dfgain_ref[...] = jnp.zeros_like(dfgain_ref)
                loss_ref[...] = jnp.zeros_like(loss_ref)

        _, xhat = _rms(x_ref[...])
        hn = (xhat * gain_ref[...]).astype(BF16)
        hn_ref[...] = hn
        for f0 in range(0, F, tf):
            cols = slice(f0, f0 + tf)
            g = _mm_nt(hn, wg_s[cols, :])
            u = _mm_nt(hn, wu_s[cols, :])
            sig = _sigmoid(g)
            s = g * sig
            a_ref[:, cols] = (s * u).astype(BF16)
            s_ref[:, cols] = s.astype(BF16)
            t_ref[:, cols] = (u * (sig + s * (1.0 - sig))).astype(BF16)
        y = x_ref[...] + FFN_RESIDUAL_WEIGHT * _mm(a_ref[...], wd_s[...])
        if head:
            r, yhat = _rms(y)
            err = yhat * fgain_ref[...] - tgt_ref[...]
            loss_ref[...] += 0.5 * jnp.sum(jnp.mean(err * err, axis=-1, keepdims=True), axis=0, keepdims=True)
            dy = err * (1.0 / D)
            dfgain_ref[...] += jnp.sum(dy * yhat, axis=0, keepdims=True)
            out_ref[...] = _rms_bwd(dy, fgain_ref[...], r, yhat)
        else:
            out_ref[...] = y

    row = lambda i: (i, 0)
    one = pl.BlockSpec((1, D), lambda i: (0, 0))
    hbm = pl.BlockSpec(memory_space=pl.ANY)
    in_specs = [pl.BlockSpec((tm, D), row), one, hbm, hbm, hbm]
    out_specs = [pl.BlockSpec((tm, D), row), pl.BlockSpec((tm, D), row)] + [pl.BlockSpec((tm, F), row) for _ in range(3)]
    out_shape = [jax.ShapeDtypeStruct((T, D), F32), jax.ShapeDtypeStruct((T, D), BF16)] \
        + [jax.ShapeDtypeStruct((T, F), BF16) for _ in range(3)]
    args = (x, gain, wgt, wut, wd)
    if head:
        in_specs += [one, pl.BlockSpec((tm, D), row)]
        out_specs += [one, one]
        out_shape += [jax.ShapeDtypeStruct((1, D), F32), jax.ShapeDtypeStruct((1, D), F32)]
        args += tuple(head)
    return _call(
        body, name=name, grid=(T // tm,), comm=comm, in_specs=in_specs, out_specs=out_specs, out_shape=out_shape,
        scratch=[pltpu.VMEM((F, D), BF16) for _ in range(3)] + [pltpu.SemaphoreType.DMA((3,))], args=args)


def _ffn_bwd(dy, x, gain, s, t, wgt, wut, wd, name):
    T, D = x.shape
    F = wd.shape[0]
    tr, tf = _tile(T, FFN_BWD_ROWS), _tile(F, 256)
    rows = lambda i: (i, 0)
    one = pl.BlockSpec((1, D), lambda i: (0, 0))
    any_spec = pl.BlockSpec(memory_space=pl.ANY)

    def body(dy_ref, x_ref, gain_ref, s_ref, t_ref, wg_hbm, wu_hbm, wd_hbm,
             dx_ref, dg_ref, du_ref, dyh_ref, dgain_ref, wg_s, wu_s, wd_s, sem):
        @pl.when(pl.program_id(0) == 0)
        def _():
            _load_resident([(wg_hbm, wg_s), (wu_hbm, wu_s), (wd_hbm, wd_s)], sem)
            dgain_ref[...] = jnp.zeros_like(dgain_ref)

        dyh = (FFN_RESIDUAL_WEIGHT * dy_ref[...]).astype(BF16)
        dyh_ref[...] = dyh
        for f0 in range(0, F, tf):
            cols = slice(f0, f0 + tf)
            da = _mm_nt(dyh, wd_s[cols, :])
            dg_ref[:, cols] = (da * t_ref[:, cols].astype(F32)).astype(BF16)
            du_ref[:, cols] = (da * s_ref[:, cols].astype(F32)).astype(BF16)
        dhn = _mm(dg_ref[...], wg_s[...]) + _mm(du_ref[...], wu_s[...])
        r, xhat = _rms(x_ref[...])
        dgain_ref[...] += jnp.sum(dhn * xhat, axis=0, keepdims=True)
        dx_ref[...] = dy_ref[...] + _rms_bwd(dhn, gain_ref[...], r, xhat)

    wide = jax.ShapeDtypeStruct((T, F), BF16)
    return pl.pallas_call(
        body, name=name, grid=(T // tr,),
        in_specs=[pl.BlockSpec((tr, D), rows), pl.BlockSpec((tr, D), rows), one, pl.BlockSpec((tr, F), rows),
                  pl.BlockSpec((tr, F), rows), any_spec, any_spec, any_spec],
        out_specs=(pl.BlockSpec((tr, D), rows), pl.BlockSpec((tr, F), rows), pl.BlockSpec((tr, F), rows),
                   pl.BlockSpec((tr, D), rows), one),
        out_shape=(jax.ShapeDtypeStruct((T, D), F32), wide, wide, jax.ShapeDtypeStruct((T, D), BF16),
                   jax.ShapeDtypeStruct((1, D), F32)),
        scratch_shapes=[pltpu.VMEM((F, D), BF16) for _ in range(3)] + [pltpu.SemaphoreType.DMA((3,))],
        compiler_params=_params("arbitrary"),
    )(dy, x, gain, s, t, wgt, wut, wd)


def _wgrad(a, b, name, col_slab=None, comm=None):
    T, M = a.shape
    N = b.shape[1]
    tmm = M if M <= 1024 else _tile(M, 1408)
    tn = _tile(N, 1024)
    if col_slab and col_slab % LANES:
        tn = col_slab * LANES // math.gcd(col_slab, LANES)
    tk = _tile(T, 2048)
    nk = T // tk
    per = tn // col_slab if col_slab else 0

    def body(a_ref, b_ref, out_ref, acc):
        k = pl.program_id(2)

        @pl.when(k == 0)
        def _():
            acc[...] = jnp.zeros_like(acc)

        acc[...] += _mm_tn(a_ref[...], b_ref[...])

        @pl.when(k == nk - 1)
        def _():
            if col_slab:
                for s in range(per):
                    out_ref[s] = acc[:, s * col_slab:(s + 1) * col_slab].astype(BF16)
            else:
                out_ref[...] = acc[...].astype(BF16)

    if col_slab:
        out_spec = pl.BlockSpec((per, tmm, col_slab), lambda m, n, k: (n, m, 0))
        out_shape = jax.ShapeDtypeStruct((N // col_slab, M, col_slab), BF16)
    else:
        out_spec = pl.BlockSpec((tmm, tn), lambda m, n, k: (m, n))
        out_shape = jax.ShapeDtypeStruct((M, N), BF16)
    (out,), got = _call(
        body, name=name, grid=(M // tmm, N // tn, nk), comm=comm,
        in_specs=[pl.BlockSpec((tk, tmm), lambda m, n, k: (k, m)), pl.BlockSpec((tk, tn), lambda m, n, k: (k, n))],
        out_specs=(out_spec,), out_shape=(out_shape,),
        scratch=[pltpu.VMEM((tmm, tn), F32)], args=(a, b))
    return (out, got) if comm else out


def _wgrad_pair(a, b1, b2, name, col_slab):
    T, M = a.shape
    N = b1.shape[1]
    tn = col_slab * LANES // math.gcd(col_slab, LANES)
    tk = _tile(T, 1024)
    nk = T // tk
    per = tn // col_slab

    def body(a_ref, b1_ref, b2_ref, out1_ref, out2_ref, acc1, acc2):
        k = pl.program_id(1)

        @pl.when(k == 0)
        def _():
            acc1[...] = jnp.zeros_like(acc1)
            acc2[...] = jnp.zeros_like(acc2)

        av = a_ref[...]
        acc1[...] += _mm_tn(av, b1_ref[...])
        acc2[...] += _mm_tn(av, b2_ref[...])

        @pl.when(k == nk - 1)
        def _():
            for s in range(per):
                out1_ref[s] = acc1[:, s * col_slab:(s + 1) * col_slab].astype(BF16)
                out2_ref[s] = acc2[:, s * col_slab:(s + 1) * col_slab].astype(BF16)

    bspec = pl.BlockSpec((tk, tn), lambda n, k: (k, n))
    ospec = pl.BlockSpec((per, M, col_slab), lambda n, k: (n, 0, 0))
    shp = jax.ShapeDtypeStruct((N // col_slab, M, col_slab), BF16)
    return pl.pallas_call(
        body, name=name, grid=(N // tn, nk),
        in_specs=[pl.BlockSpec((tk, M), lambda n, k: (k, 0)), bspec, bspec],
        out_specs=(ospec, ospec), out_shape=(shp, shp),
        scratch_shapes=[pltpu.VMEM((M, tn), F32), pltpu.VMEM((M, tn), F32)],
        compiler_params=_params("arbitrary", "arbitrary"),
    )(a, b1, b2)


def _mix_in_fwd(x, gain, w_int):
    T, D = x.shape
    C = w_int.shape[0]
    n_qkv = 3 * C // 4
    tm = _tile(T, 1024)

    def body(x_ref, gain_ref, w_ref, hn_ref, qkv_ref, u_ref):
        _, xhat = _rms(x_ref[...])
        hn = (xhat * gain_ref[...]).astype(BF16)
        hn_ref[...] = hn
        proj = _mm_nt(hn, w_ref[...])
        qkv_ref[...] = proj[:, :n_qkv].astype(BF16)
        u_ref[...] = proj[:, n_qkv:]

    row = lambda i: (i, 0)
    return pl.pallas_call(
        body, name="mix_in_fwd", grid=(T // tm,),
        in_specs=[pl.BlockSpec((tm, D), row), pl.BlockSpec((1, D), lambda i: (0, 0)),
                  pl.BlockSpec((C, D), lambda i: (0, 0))],
        out_specs=(pl.BlockSpec((tm, D), row), pl.BlockSpec((tm, n_qkv), row), pl.BlockSpec((tm, C - n_qkv), row)),
        out_shape=(jax.ShapeDtypeStruct((T, D), BF16), jax.ShapeDtypeStruct((T, n_qkv), BF16),
                   jax.ShapeDtypeStruct((T, C - n_qkv), F32)),
        compiler_params=_params("arbitrary"),
    )(x, gain, w_int)


def _mix_in_bwd(dres, dq, dk, dv, du, x, gain, w_int):
    T, D = x.shape
    C = w_int.shape[0]
    W = dq.shape[1]
    tm = _tile(T, 512)

    def body(dres_ref, dq_ref, dk_ref, dv_ref, du_ref, x_ref, gain_ref, w_ref, dx_ref, dproj_ref, dgain_ref):
        @pl.when(pl.program_id(0) == 0)
        def _():
            dgain_ref[...] = jnp.zeros_like(dgain_ref)

        for part, ref in enumerate((dq_ref, dk_ref, dv_ref, du_ref)):
            dproj_ref[:, part * W:(part + 1) * W] = ref[...]
        dhn = _mm(dproj_ref[...], w_ref[...])
        r, xhat = _rms(x_ref[...])
        dgain_ref[...] += jnp.sum(dhn * xhat, axis=0, keepdims=True)
        dx_ref[...] = dres_ref[...] + _rms_bwd(dhn, gain_ref[...], r, xhat)

    row = lambda i: (i, 0)
    one = pl.BlockSpec((1, D), lambda i: (0, 0))
    part = pl.BlockSpec((tm, W), row)
    return pl.pallas_call(
        body, name="mix_in_bwd", grid=(T // tm,),
        in_specs=[pl.BlockSpec((tm, D), row), part, part, part, part, pl.BlockSpec((tm, D), row), one,
                  pl.BlockSpec((C, D), lambda i: (0, 0))],
        out_specs=(pl.BlockSpec((tm, D), row), pl.BlockSpec((tm, C), row), one),
        out_shape=(jax.ShapeDtypeStruct((T, D), F32), jax.ShapeDtypeStruct((T, C), BF16),
                   jax.ShapeDtypeStruct((1, D), F32)),
        compiler_params=_params("arbitrary"),
    )(dres, dq, dk, dv, du, x, gain, w_int)


SB_PAIRS_PER_PROGRAM = 2
SB_FWD_PAIRS_PER_PROGRAM = 4
LOG2_E = 1.4426950408889634
EXP2_CLAMP = 126.0


def _neg_log2_sigmoid(nz2):
    w = jnp.minimum(nz2, EXP2_CLAMP)
    return w, jnp.log2(1.0 + jnp.exp2(w))


SB_DEAD_LOG2 = -160.0


def _sb_live(rests):
    worst = functools.reduce(jnp.maximum, rests)
    return (jnp.max(worst) > SB_DEAD_LOG2).astype(jnp.int32)


def _split(v):
    hi = v.astype(BF16)
    return hi, (v - hi.astype(F32)).astype(BF16)


def _sb_fwd(qkv, B, S, comm=None):
    W = qkv.shape[2] // 3
    n_pair = W // LANES
    bq = _tile(S, 256)
    nq = S // bq
    hp = SB_FWD_PAIRS_PER_PROGRAM
    nscale2 = -(SB_HEAD_DIM ** -0.5) * LOG2_E

    def body(q_ref, k_ref, v_ref, o_ref):
        lane = lax.broadcasted_iota(jnp.int32, (1, LANES), 1)
        head0 = lane < SB_HEAD_DIM
        rr = lax.broadcasted_iota(jnp.int32, (bq, bq), 0)
        cc = lax.broadcasted_iota(jnp.int32, (bq, bq), 1)
        strict = cc < rr
        after = jnp.where(rr > cc, 1.0, 0.0).astype(BF16)

        def blocks(heads, ks, carries, diag):
            n = range(len(heads))
            keep = (lambda t: jnp.where(strict, t, 0.0)) if diag else (lambda t: t)
            z = [_mm_nt(qh, k_ref[ks, cols]) for qh, cols in heads]
            wl = [_neg_log2_sigmoid(z[h] * nscale2) for h in n]
            lr = [keep(wl[h][0] - wl[h][1]) for h in n]
            parts = [_split(lr[h]) for h in n]
            suf = [_mm(parts[h][0], after) + _mm(parts[h][1], after) for h in n]
            a = [keep(jnp.exp2(suf[h] + carries[h][1] - wl[h][1])).astype(BF16) for h in n]
            o = [carries[h][0] + _mm(a[h], v_ref[ks, heads[h][1]]) for h in n]
            return tuple((o[h], carries[h][1] + (suf[h][:, :1] + lr[h][:, :1])) for h in n)

        def q_tile(i, _):
            qs = pl.ds(pl.multiple_of(i * bq, bq), bq)
            heads = []
            for pr in range(hp):
                cols = slice(pr * LANES, (pr + 1) * LANES)
                qv = q_ref[qs, cols]
                heads += [(jnp.where(head0, qv, jnp.zeros_like(qv)), cols),
                          (jnp.where(head0, jnp.zeros_like(qv), qv), cols)]
            zero = (jnp.zeros((bq, LANES), F32), jnp.zeros((bq, 1), F32))
            init = blocks(heads, qs, (zero,) * len(heads), True)

            def left(st):
                t, _, cr = st
                ks = pl.ds(pl.multiple_of((i - 1 - t) * bq, bq), bq)
                cr = blocks(heads, ks, cr, False)
                return t + 1, _sb_live([c for _, c in cr]), cr

            _, _, res = lax.while_loop(lambda st: jnp.logical_and(st[0] < i, st[1] > 0), left,
                                       (jnp.int32(0), _sb_live([c for _, c in init]), init))
            for pr in range(hp):
                o_ref[qs, heads[2 * pr][1]] = jnp.where(head0, res[2 * pr][0], res[2 * pr + 1][0]).astype(BF16)
            return 0

        lax.fori_loop(0, nq, q_tile, 0)

    def col(off):
        return pl.BlockSpec((None, S, hp * LANES), lambda b, p: (b, 0, off + p))

    n_pair //= hp
    return _call(
        body, name="sb_fwd", grid=(B, n_pair), comm=comm,
        in_specs=[col(0), col(n_pair), col(2 * n_pair)],
        out_specs=(col(0),),
        out_shape=(jax.ShapeDtypeStruct((B, S, W), BF16),),
        args=(qkv, qkv, qkv))


def _sb_bwd(qkv, do, B, S, comm=None):
    W = qkv.shape[2] // 3
    n_pair = W // LANES
    bq = _tile(S, 256)
    nq = S // bq
    hp = SB_PAIRS_PER_PROGRAM
    scale = SB_HEAD_DIM ** -0.5
    nscale2 = -scale * LOG2_E

    def body(q_ref, k_ref, v_ref, do_ref, dq_ref, dk_ref, dv_ref, dk_s, dv_s, e_s, sg_s, a_s):
        lane = lax.broadcasted_iota(jnp.int32, (1, LANES), 1)
        head0 = lane < SB_HEAD_DIM
        rr = lax.broadcasted_iota(jnp.int32, (bq, bq), 0)
        cc = lax.broadcasted_iota(jnp.int32, (bq, bq), 1)
        strict = cc < rr
        after = jnp.where(rr > cc, 1.0, 0.0).astype(BF16)
        before = jnp.where(rr < cc, 1.0, 0.0).astype(BF16)
        dk_s[...] = jnp.zeros_like(dk_s)
        dv_s[...] = jnp.zeros_like(dv_s)

        def weights(heads, ks, kb, rests, diag):
            n = range(len(heads))
            keep = (lambda t: jnp.where(strict, t, 0.0)) if diag else (lambda t: t)
            z = [_mm_nt(heads[h][0], k_ref[ks, heads[h][2]]) for h in n]
            da = [_mm_nt(heads[h][1], v_ref[ks, heads[h][2]]) for h in n]
            wl = [_neg_log2_sigmoid(z[h] * nscale2) for h in n]
            lr = [keep(wl[h][0] - wl[h][1]) for h in n]
            parts = [_split(lr[h]) for h in n]
            suf = [_mm(parts[h][0], after) + _mm(parts[h][1], after) for h in n]
            a = [keep(jnp.exp2(suf[h] + rests[h] - wl[h][1])) for h in n]
            for h in n:
                a_s[h * nq + kb] = a[h].astype(BF16)
                e_s[h * nq + kb] = a[h] * da[h]
                sg_s[h * nq + kb] = jnp.exp2(-wl[h][1])
            return tuple(rests[h] + (suf[h][:, :1] + lr[h][:, :1]) for h in n)

        def grads(heads, ks, kb, carries, diag):
            n = range(len(heads))
            keep = (lambda t: jnp.where(strict, t, 0.0)) if diag else (lambda t: t)
            e = [e_s[h * nq + kb] for h in n]
            parts = [_split(e[h]) for h in n]
            pex = [_mm(parts[h][0], before) + _mm(parts[h][1], before) for h in n]
            dz = [keep(e[h] - sg_s[h * nq + kb] * (e[h] + pex[h] + carries[h][1])).astype(BF16) for h in n]
            dq = [carries[h][0] + _mm(dz[h], k_ref[ks, heads[h][2]]) for h in n]
            for h in n:
                dk_s[ks, heads[h][2]] += _mm_tn(dz[h], heads[h][0])
                dv_s[ks, heads[h][2]] += _mm_tn(a_s[h * nq + kb], heads[h][1])
            return tuple((dq[h], carries[h][1] + (pex[h][:, bq - 1:] + e[h][:, bq - 1:])) for h in n)

        def q_tile(i, _):
            qs = pl.ds(pl.multiple_of(i * bq, bq), bq)
            heads = []
            for pr in range(hp):
                cols = slice(pr * LANES, (pr + 1) * LANES)
                qv, dov = q_ref[qs, cols], do_ref[qs, cols]
                zq, zd = jnp.zeros_like(qv), jnp.zeros_like(dov)
                heads += [(jnp.where(head0, qv, zq), jnp.where(head0, dov, zd), cols),
                          (jnp.where(head0, zq, qv), jnp.where(head0, zd, dov), cols)]
            key_block = lambda kb: pl.ds(pl.multiple_of(kb * bq, bq), bq)
            rests = weights(heads, qs, i, (jnp.zeros((bq, 1), F32),) * len(heads), True)

            def left(st):
                t, _, rs = st
                rs = weights(heads, key_block(i - 1 - t), i - 1 - t, rs, False)
                return t + 1, _sb_live(rs), rs

            n_left, _, _ = lax.while_loop(lambda st: jnp.logical_and(st[0] < i, st[1] > 0), left,
                                          (jnp.int32(0), _sb_live(rests), rests))
            zero = (jnp.zeros((bq, LANES), F32), jnp.zeros((bq, 1), F32))
            res = lax.fori_loop(0, n_left, lambda t, cr: grads(heads, key_block(i - n_left + t), i - n_left + t, cr, False),
                                (zero,) * len(heads))
            res = grads(heads, qs, i, res, True)
            for pr in range(hp):
                dq = jnp.where(head0, res[2 * pr][0], res[2 * pr + 1][0])
                dq_ref[qs, heads[2 * pr][2]] = (dq * scale).astype(BF16)
            return 0

        lax.fori_loop(0, nq, q_tile, 0)
        dk_ref[...] = (dk_s[...] * scale).astype(BF16)
        dv_ref[...] = dv_s[...].astype(BF16)

    def col(off):
        return pl.BlockSpec((None, S, hp * LANES), lambda b, p: (b, 0, off + p))

    n_pair //= hp
    shp = jax.ShapeDtypeStruct((B, S, W), BF16)
    slots = 2 * hp * nq
    return _call(
        body, name="sb_bwd", grid=(B, n_pair), comm=comm,
        in_specs=[col(0), col(n_pair), col(2 * n_pair), col(0)],
        out_specs=(col(0), col(0), col(0)),
        out_shape=(shp, shp, shp),
        scratch=[pltpu.VMEM((S, hp * LANES), F32), pltpu.VMEM((S, hp * LANES), F32),
                 pltpu.VMEM((slots, bq, bq), F32), pltpu.VMEM((slots, bq, bq), F32),
                 pltpu.VMEM((slots, bq, bq), BF16)],
        args=(qkv, qkv, qkv, do))


def _pool_counts(S):
    t = lax.broadcasted_iota(jnp.int32, (S, 1), 0)
    return t, [jnp.minimum(t + 1, w).astype(F32) for w in POOL_WINDOWS]


def _pool_fwd(u, B, S):
    W = u.shape[2]

    def body(u_ref, out_ref):
        t, counts = _pool_counts(S)
        for gi, w in enumerate(POOL_WINDOWS):
            cols = slice(gi * POOL_GROUP_DIM, (gi + 1) * POOL_GROUP_DIM)
            ug = u_ref[:, cols]
            s, k = ug, 1
            while k < w:
                s = s + jnp.where(t >= k, pltpu.roll(s, k, axis=0), 0.0)
                k *= 2
            out_ref[:, cols] = (s / counts[gi] - ug).astype(BF16)

    spec = pl.BlockSpec((None, S, W), lambda b: (b, 0, 0))
    return pl.pallas_call(
        body, name="pool_fwd", grid=(B,), in_specs=[spec], out_specs=spec,
        out_shape=jax.ShapeDtypeStruct((B, S, W), BF16), compiler_params=_params("arbitrary"),
    )(u)


def _pool_bwd(dpooled, B, S):
    W = dpooled.shape[2]

    def body(d_ref, out_ref):
        t, counts = _pool_counts(S)
        for gi, w in enumerate(POOL_WINDOWS):
            cols = slice(gi * POOL_GROUP_DIM, (gi + 1) * POOL_GROUP_DIM)
            d = d_ref[:, cols]
            s, k = d / counts[gi], 1
            while k < w:
                s = s + jnp.where(t < S - k, pltpu.roll(s, S - k, axis=0), 0.0)
                k *= 2
            out_ref[:, cols] = (s - d).astype(BF16)

    spec = pl.BlockSpec((None, S, W), lambda b: (b, 0, 0))
    return pl.pallas_call(
        body, name="pool_bwd", grid=(B,), in_specs=[spec], out_specs=spec,
        out_shape=jax.ShapeDtypeStruct((B, S, W), BF16), compiler_params=_params("arbitrary"),
    )(dpooled)


def _mix_out_fwd(x, o_sb, pooled, w_pool, pool_scale, w_out):
    T, D = x.shape
    W = o_sb.shape[1]
    G = w_pool.shape[0]
    gd = POOL_GROUP_DIM
    tm = _tile(T, 1024)

    def body(x_ref, osb_ref, pooled_ref, wp_ref, ps_ref, wo_ref, out_ref, mixed_ref):
        mixed_ref[:, :W] = osb_ref[...]
        for gi in range(G):
            cols = slice(gi * gd, (gi + 1) * gd)
            pw = _mm(pooled_ref[:, cols], wp_ref[gi])
            mixed_ref[:, W + gi * gd:W + (gi + 1) * gd] = (pw * ps_ref[:, cols]).astype(BF16)
        out_ref[...] = x_ref[...] + _mm(mixed_ref[...], wo_ref[...].reshape(D, D))

    row = lambda i: (i, 0)
    return pl.pallas_call(
        body, name="mix_out_fwd", grid=(T // tm,),
        in_specs=[pl.BlockSpec((tm, D), row), pl.BlockSpec((tm, W), row), pl.BlockSpec((tm, W), row),
                  pl.BlockSpec((G, gd, gd), lambda i: (0, 0, 0)), pl.BlockSpec((1, W), lambda i: (0, 0)),
                  pl.BlockSpec(w_out.shape, lambda i: (0, 0, 0))],
        out_specs=(pl.BlockSpec((tm, D), row), pl.BlockSpec((tm, D), row)),
        out_shape=(jax.ShapeDtypeStruct((T, D), F32), jax.ShapeDtypeStruct((T, D), BF16)),
        compiler_params=_params("arbitrary"),
    )(x, o_sb, pooled, w_pool, pool_scale, w_out)


def _mix_out_bwd(dx, pooled, w_pool, pool_scale, w_out):
    T, D = dx.shape
    W = pooled.shape[1]
    G = w_pool.shape[0]
    gd = POOL_GROUP_DIM
    tm = _tile(T, 1024)

    def body(dx_ref, pooled_ref, wp_ref, ps_ref, wo_ref, dxb_ref, dosb_ref, dpooled_ref, dwp_ref, dps_ref):
        i = pl.program_id(0)

        @pl.when(i == 0)
        def _():
            dwp_ref[...] = jnp.zeros_like(dwp_ref)
            dps_ref[...] = jnp.zeros_like(dps_ref)

        dxb = dx_ref[...].astype(BF16)
        dxb_ref[...] = dxb
        dmixed = _mm_nt(dxb, wo_ref[...].reshape(D, D))
        dosb_ref[...] = dmixed[:, :W].astype(BF16)
        for gi in range(G):
            cols = slice(gi * gd, (gi + 1) * gd)
            pg = pooled_ref[:, cols]
            dop = dmixed[:, W + gi * gd:W + (gi + 1) * gd]
            pw = _mm(pg, wp_ref[gi])
            dps_ref[:, cols] += jnp.sum(dop * pw, axis=0, keepdims=True)
            dpw = (dop * ps_ref[:, cols]).astype(BF16)
            dwp_ref[gi] += _mm_tn(pg, dpw)
            dpooled_ref[:, cols] = _mm_nt(dpw, wp_ref[gi])

    row = lambda i: (i, 0)
    return pl.pallas_call(
        body, name="mix_out_bwd", grid=(T // tm,),
        in_specs=[pl.BlockSpec((tm, D), row), pl.BlockSpec((tm, W), row),
                  pl.BlockSpec((G, gd, gd), lambda i: (0, 0, 0)), pl.BlockSpec((1, W), lambda i: (0, 0)),
                  pl.BlockSpec(w_out.shape, lambda i: (0, 0, 0))],
        out_specs=(pl.BlockSpec((tm, D), row), pl.BlockSpec((tm, W), row), pl.BlockSpec((tm, W), row),
                   pl.BlockSpec((G, gd, gd), lambda i: (0, 0, 0)), pl.BlockSpec((1, W), lambda i: (0, 0))),
        out_shape=(jax.ShapeDtypeStruct((T, D), BF16), jax.ShapeDtypeStruct((T, W), BF16),
                   jax.ShapeDtypeStruct((T, W), F32), jax.ShapeDtypeStruct((G, gd, gd), F32),
                   jax.ShapeDtypeStruct((1, W), F32)),
        compiler_params=_params("arbitrary"),
    )(dx, pooled, w_pool, pool_scale, w_out)


def _mem_kv_fwd(mem, gain, w_kvt):
    B, M, D = mem.shape
    C = w_kvt.shape[0]

    def body(mem_ref, gain_ref, w_ref, memn_ref, kv_ref):
        _, xhat = _rms(mem_ref[...])
        mn = (xhat * gain_ref[...]).astype(BF16)
        memn_ref[...] = mn
        kv_ref[...] = _mm_nt(mn, w_ref[...]).astype(BF16)

    return pl.pallas_call(
        body, name="mem_kv_fwd", grid=(B,),
        in_specs=[pl.BlockSpec((None, M, D), lambda b: (b, 0, 0)), pl.BlockSpec((1, D), lambda b: (0, 0)),
                  pl.BlockSpec((C, D), lambda b: (0, 0))],
        out_specs=(pl.BlockSpec((M, D), lambda b: (b, 0)), pl.BlockSpec((None, M, C), lambda b: (b, 0, 0))),
        out_shape=(jax.ShapeDtypeStruct((B * M, D), BF16), jax.ShapeDtypeStruct((B, M, C), BF16)),
        compiler_params=_params("arbitrary"),
    )(mem, gain, w_kvt)


def _mem_kv_bwd(dkv, mem, w_kvt):
    B, M, D = mem.shape
    C = w_kvt.shape[0]

    def body(dkv_ref, mem_ref, w_ref, dkvb_ref, dgain_ref):
        @pl.when(pl.program_id(0) == 0)
        def _():
            dgain_ref[...] = jnp.zeros_like(dgain_ref)

        dkvb = dkv_ref[...].astype(BF16)
        dkvb_ref[...] = dkvb
        dmn = _mm(dkvb, w_ref[...])
        _, xhat = _rms(mem_ref[...])
        dgain_ref[...] += jnp.sum(dmn * xhat, axis=0, keepdims=True)

    return pl.pallas_call(
        body, name="mem_kv_bwd", grid=(B,),
        in_specs=[pl.BlockSpec((None, M, C), lambda b: (b, 0, 0)), pl.BlockSpec((None, M, D), lambda b: (b, 0, 0)),
                  pl.BlockSpec((C, D), lambda b: (0, 0))],
        out_specs=(pl.BlockSpec((M, C), lambda b: (b, 0)), pl.BlockSpec((1, D), lambda b: (0, 0))),
        out_shape=(jax.ShapeDtypeStruct((B * M, C), BF16), jax.ShapeDtypeStruct((1, D), F32)),
        compiler_params=_params("arbitrary"),
    )(dkv, mem, w_kvt)


def _softmax_rows(s):
    p = jnp.exp(s - jnp.max(s, axis=1, keepdims=True))
    return p / jnp.sum(p, axis=1, keepdims=True)


def _cross_fwd(x, gain, kv, w_q, w_o, B, S, comm=None):
    T, D = x.shape
    M = kv.shape[1]
    hd = D // MEM_HEADS
    tm = _tile(S, 1024)
    per = S // tm
    scale = hd ** -0.5

    def body(x_ref, gain_ref, kv_ref, wq_ref, wo_ref, out_ref, hq_ref, q_ref, ocat_ref):
        _, xhat = _rms(x_ref[...])
        hq = (xhat * gain_ref[...]).astype(BF16)
        hq_ref[...] = hq
        q = _mm(hq, wq_ref[...].reshape(D, D)).astype(BF16)
        q_ref[...] = q
        for h in range(MEM_HEADS):
            cols = slice(h * hd, (h + 1) * hd)
            s = _mm_nt(q[:, cols], kv_ref[:, cols]) * scale
            p = _softmax_rows(s).astype(BF16)
            ocat_ref[:, cols] = _mm(p, kv_ref[:, D + h * hd:D + (h + 1) * hd]).astype(BF16)
        out_ref[...] = x_ref[...] + _mm(ocat_ref[...], wo_ref[...].reshape(D, D))

    row = lambda b, t: (b * per + t, 0)
    wspec = pl.BlockSpec(w_q.shape, lambda b, t: (0, 0, 0))
    return _call(
        body, name="cross_fwd", grid=(B, per), comm=comm,
        in_specs=[pl.BlockSpec((tm, D), row), pl.BlockSpec((1, D), lambda b, t: (0, 0)),
                  pl.BlockSpec((None, M, 2 * D), lambda b, t: (b, 0, 0)), wspec, wspec],
        out_specs=tuple(pl.BlockSpec((tm, D), row) for _ in range(4)),
        out_shape=(jax.ShapeDtypeStruct((T, D), F32),) + tuple(jax.ShapeDtypeStruct((T, D), BF16) for _ in range(3)),
        args=(x, gain, kv, w_q, w_o))


def _cross_bwd(dy, x, gain, q, kv, w_q, w_o, B, S, comm=None):
    T, D = x.shape
    M = kv.shape[1]
    hd = D // MEM_HEADS
    tm = _tile(S, 512)
    per = S // tm
    scale = hd ** -0.5

    def body(dy_ref, x_ref, gain_ref, q_ref, kv_ref, wq_ref, wo_ref,
             dx_ref, dyb_ref, dqb_ref, dkv_ref, dgain_ref):
        b_id, t_id = pl.program_id(0), pl.program_id(1)

        @pl.when((b_id == 0) & (t_id == 0))
        def _():
            dgain_ref[...] = jnp.zeros_like(dgain_ref)

        @pl.when(t_id == 0)
        def _():
            dkv_ref[...] = jnp.zeros_like(dkv_ref)

        dyb = dy_ref[...].astype(BF16)
        dyb_ref[...] = dyb
        docat = _mm_nt(dyb, wo_ref[...].reshape(D, D)).astype(BF16)
        for h in range(MEM_HEADS):
            cols = slice(h * hd, (h + 1) * hd)
            vcols = slice(D + h * hd, D + (h + 1) * hd)
            qh, kh, vh, doh = q_ref[:, cols], kv_ref[:, cols], kv_ref[:, vcols], docat[:, cols]
            p = _softmax_rows(_mm_nt(qh, kh) * scale)
            dp = _mm_nt(doh, vh)
            ds = (p * (dp - jnp.sum(dp * p, axis=1, keepdims=True)) * scale).astype(BF16)
            dqb_ref[:, cols] = _mm(ds, kh).astype(BF16)
            dkv_ref[:, cols] += _mm_tn(ds, qh)
            dkv_ref[:, vcols] += _mm_tn(p.astype(BF16), doh)
        dhq = _mm_nt(dqb_ref[...], wq_ref[...].reshape(D, D))
        r, xhat = _rms(x_ref[...])
        dgain_ref[...] += jnp.sum(dhq * xhat, axis=0, keepdims=True)
        dx_ref[...] = dy_ref[...] + _rms_bwd(dhq, gain_ref[...], r, xhat)

    row = lambda b, t: (b * per + t, 0)
    wspec = pl.BlockSpec(w_q.shape, lambda b, t: (0, 0, 0))
    one = pl.BlockSpec((1, D), lambda b, t: (0, 0))
    kvspec = pl.BlockSpec((None, M, 2 * D), lambda b, t: (b, 0, 0))
    return _call(
        body, name="cross_bwd", grid=(B, per), comm=comm,
        in_specs=[pl.BlockSpec((tm, D), row), pl.BlockSpec((tm, D), row), one, pl.BlockSpec((tm, D), row),
                  kvspec, wspec, wspec],
        out_specs=(pl.BlockSpec((tm, D), row), pl.BlockSpec((tm, D), row), pl.BlockSpec((tm, D), row), kvspec, one),
        out_shape=(jax.ShapeDtypeStruct((T, D), F32), jax.ShapeDtypeStruct((T, D), BF16),
                   jax.ShapeDtypeStruct((T, D), BF16), jax.ShapeDtypeStruct((B, M, 2 * D), F32),
                   jax.ShapeDtypeStruct((1, D), F32)),
        args=(dy, x, gain, q, kv, w_q, w_o))


def _ordered_sum(gp_ref):
    g = gp_ref[0].astype(F32)
    for s in range(1, gp_ref.shape[0]):
        g = g + gp_ref[s].astype(F32)
    return g


def _adam_write(g, w_ref, m_ref, v_ref, g_ref, d_ref, nm_ref, nv_ref):
    nm = ADAM_B1 * m_ref[...] + (1.0 - ADAM_B1) * g
    nv = ADAM_B2 * v_ref[...] + (1.0 - ADAM_B2) * (g * g)
    m_hat = nm / (1.0 - ADAM_B1 ** ADAM_STEP)
    v_hat = nv / (1.0 - ADAM_B2 ** ADAM_STEP)
    g_ref[...] = g
    nm_ref[...] = nm
    nv_ref[...] = nv
    d_ref[...] = -ADAM_LR * (m_hat / (jnp.sqrt(v_hat) + ADAM_EPS) + ADAM_WD * w_ref[...])


def _adamw(items, name):
    R = items[0][1].shape[0]
    tr = next(t for t in (_tile(R, 128), _tile(R, 256)) if t % 16 == 0)
    n = len(items)

    def body(*refs):
        ins, outs = refs[:4 * n], refs[4 * n:]
        for k in range(n):
            gp_ref, w_ref, m_ref, v_ref = ins[4 * k:4 * k + 4]
            _adam_write(_ordered_sum(gp_ref), w_ref, m_ref, v_ref, *outs[4 * k:4 * k + 4])

    in_specs, out_specs, out_shape = [], [], []
    for parts, w, _, _ in items:
        C = w.shape[1]
        spec = pl.BlockSpec((tr, C), lambda i: (i, 0))
        in_specs += [pl.BlockSpec((parts.shape[0], tr, C), lambda i: (0, i, 0)), spec, spec, spec]
        out_specs += [spec] * 4
        out_shape += [jax.ShapeDtypeStruct(w.shape, F32)] * 4
    res = pl.pallas_call(
        body, name=name, grid=(R // tr,), in_specs=in_specs, out_specs=tuple(out_specs), out_shape=tuple(out_shape),
        compiler_params=_params("arbitrary"),
    )(*[a for item in items for a in item])
    return [res[4 * k:4 * k + 4] for k in range(n)]


def _pack_rows(vectors, D):
    def body(*refs):
        out_ref = refs[-1]
        out_ref[...] = jnp.zeros_like(out_ref)
        for i, r in enumerate(refs[:-1]):
            out_ref[i:i + 1, :r.shape[1]] = r[...]

    vmem = pl.BlockSpec(memory_space=pltpu.VMEM)
    return pl.pallas_call(body, name="pack_small", in_specs=[vmem] * len(vectors), out_specs=vmem,
                          out_shape=jax.ShapeDtypeStruct((8, D), F32))(*vectors)


def _adamw_small(row_parts, mat_parts, vectors, matrix):
    n = len(vectors)

    def body(*refs):
        rp_ref, mp_ref = refs[0], refs[1]
        ins, outs = refs[2:2 + 3 * (n + 1)], refs[2 + 3 * (n + 1):]
        rows = _ordered_sum(rp_ref)
        for i in range(n):
            c = ins[3 * i].shape[1]
            _adam_write(rows[i:i + 1, :c], *ins[3 * i:3 * i + 3], *outs[4 * i:4 * i + 4])
        _adam_write(_ordered_sum(mp_ref), *ins[3 * n:3 * n + 3], *outs[4 * n:4 * n + 4])
        outs[-1][...] = rows[n:n + 1, :]

    flat = [a for wmv in vectors for a in wmv] + list(matrix)
    out_shape = [jax.ShapeDtypeStruct(wmv[0].shape, F32) for wmv in list(vectors) + [matrix] for _ in range(4)]
    out_shape.append(jax.ShapeDtypeStruct((1, row_parts.shape[2]), F32))
    vmem = pl.BlockSpec(memory_space=pltpu.VMEM)
    res = pl.pallas_call(body, name="adamw_small", in_specs=[vmem] * (2 + len(flat)),
                         out_specs=tuple([vmem] * len(out_shape)), out_shape=tuple(out_shape))(row_parts, mat_parts, *flat)
    return [res[4 * i:4 * i + 4] for i in range(n + 1)], res[-1]


def kernel(x, mem, ffn1_norm, ffn1_w_gate, ffn1_w_up, ffn1_w_down, mix_norm, w_in, w_pool, pool_scale, w_out, mem_q_norm, mem_kv_norm, mem_w_q, mem_w_kv, mem_w_o, ffn2_norm, ffn2_w_gate, ffn2_w_up, ffn2_w_down, final_norm, loss_target, m_ffn1_norm, m_ffn1_w_gate, m_ffn1_w_up, m_ffn1_w_down, m_mix_norm, m_w_in, m_w_pool, m_pool_scale, m_w_out, m_mem_q_norm, m_mem_kv_norm, m_mem_w_q, m_mem_w_kv, m_mem_w_o, m_ffn2_norm, m_ffn2_w_gate, m_ffn2_w_up, m_ffn2_w_down, m_final_norm, v_ffn1_norm, v_ffn1_w_gate, v_ffn1_w_up, v_ffn1_w_down, v_mix_norm, v_w_in, v_w_pool, v_pool_scale, v_w_out, v_mem_q_norm, v_mem_kv_norm, v_mem_w_q, v_mem_w_kv, v_mem_w_o, v_ffn2_norm, v_ffn2_w_gate, v_ffn2_w_up, v_ffn2_w_down, v_final_norm):
    B, S, D = x.shape
    T = B * S
    x0 = x.reshape(T, D)
    target = loss_target.reshape(T, D)
    final_gain = final_norm.reshape(1, D)

    big = dict(
        g1=ffn1_w_gate[0].T, u1=ffn1_w_up[0].T, d1=ffn1_w_down[0],
        g2=ffn2_w_gate[0].T, u2=ffn2_w_up[0].T, d2=ffn2_w_down[0],
        w_in=w_in[0].T, w_out=w_out[0], w_q=mem_w_q[0], w_kv=mem_w_kv[0].T, w_o=mem_w_o[0])
    names = list(big)
    shard = {k: big[k].astype(BF16) for k in names}
    wp = w_pool[0].astype(BF16)
    full, ffn_w = {}, {}
    stacked = ("g1", "u1", "d1", "g2", "u2", "d2", "w_in", "w_kv")

    def gathered(keys, arrs):
        full.update(zip(keys, arrs))
        ffn_w.update({k: full[k].reshape(-1, D) for k in keys if k in stacked})

    first, mid = ("g1", "u1", "d1"), ("w_in", "w_out", "w_q", "w_kv", "w_o")
    gathered(first, _gather_two_level([shard[k] for k in first], "gather_ffn1"))
    (x1, hn1, a1, s1, t1), got = _ffn_fwd(x0, ffn1_norm, ffn_w["g1"], ffn_w["u1"], ffn_w["d1"], "ffn1_fwd",
                                          comm=([shard[k] for k in mid], GATHER2))
    gathered(mid, got)
    hn2, qkv, u = _mix_in_fwd(x1, mix_norm, ffn_w["w_in"])
    qkv3 = qkv.reshape(B, S, -1)
    (o_sb,), got = _sb_fwd(qkv3, B, S, comm=([shard["g2"], shard["u2"]], GATHER2))
    gathered(("g2", "u2"), got)
    pooled = _pool_fwd(u.reshape(B, S, -1), B, S).reshape(T, -1)
    x2, mixed = _mix_out_fwd(x1, o_sb.reshape(T, -1), pooled, wp, pool_scale, full["w_out"])
    memn, kv = _mem_kv_fwd(mem, mem_kv_norm, ffn_w["w_kv"])
    (x3, hq, q, ocat), got = _cross_fwd(x2, mem_q_norm, kv, full["w_q"], full["w_o"], B, S,
                                        comm=([shard["d2"]], GATHER2))
    gathered(("d2",), got)
    (dx4, hn4, a2, s2, t2, d_final, loss_part), _ = _ffn_fwd(x3, ffn2_norm, ffn_w["g2"], ffn_w["u2"], ffn_w["d2"],
                                                            "ffn2_fwd", head=(final_gain, target))

    slab = lambda k: grads[k].reshape((N_DEV, -1) + grads[k].shape[-1:])
    got = {}
    dx3, dg2, du2, dyh2, d_ffn2 = _ffn_bwd(dx4, x3, ffn2_norm, s2, t2, ffn_w["g2"], ffn_w["u2"],
                                          ffn_w["d2"], "ffn2_bwd")
    ffn_slab = ffn1_w_gate.shape[2]
    dw_g2, dw_u2 = _wgrad_pair(hn4, dg2, du2, "dw_gate_up2", ffn_slab)
    grads = dict(g2=dw_g2, u2=dw_u2, d2=_wgrad(a2, dyh2, "dw_down2"))
    (dx2, dx3b, dqb, dkv, d_q), (got["g2"],) = _cross_bwd(dx3, x2, mem_q_norm, q, kv, full["w_q"], full["w_o"], B, S,
                                                         comm=([slab("g2")], False))
    grads["w_o"] = _wgrad(ocat, dx3b, "dw_o")
    grads["w_q"] = _wgrad(hq, dqb, "dw_q")
    dkvb, d_kv = _mem_kv_bwd(dkv, mem, ffn_w["w_kv"])
    grads["w_kv"] = _wgrad(memn, dkvb, "dw_kv", col_slab=mem_w_kv.shape[2])
    dx2b, do_sb, dpooled, d_wpool, d_ps = _mix_out_bwd(dx2, pooled, wp, pool_scale, full["w_out"])
    grads["w_out"] = _wgrad(mixed, dx2b, "dw_out")
    du = _pool_bwd(dpooled.reshape(B, S, -1), B, S).reshape(T, -1)
    early = ("u2", "d2", "w_o", "w_q", "w_kv", "w_out")
    (dq, dk, dv), res = _sb_bwd(qkv3, do_sb.reshape(B, S, -1), B, S, comm=([slab(k) for k in early], False))
    got.update(zip(early, res))
    dx1, dproj, d_mix = _mix_in_bwd(dx2, dq.reshape(T, -1), dk.reshape(T, -1), dv.reshape(T, -1), du,
                                    x1, mix_norm, ffn_w["w_in"])
    grads["w_in"] = _wgrad(hn2, dproj, "dw_in", col_slab=w_in.shape[2])
    dx0, dg1, du1, dyh1, d_ffn1 = _ffn_bwd(dx1, x0, ffn1_norm, s1, t1, ffn_w["g1"], ffn_w["u1"],
                                          ffn_w["d1"], "ffn1_bwd")

    small = [("ffn1_norm", d_ffn1, ffn1_norm, m_ffn1_norm, v_ffn1_norm),
             ("mix_norm", d_mix, mix_norm, m_mix_norm, v_mix_norm),
             ("pool_scale", d_ps, pool_scale, m_pool_scale, v_pool_scale),
             ("mem_q_norm", d_q, mem_q_norm, m_mem_q_norm, v_mem_q_norm),
             ("mem_kv_norm", d_kv, mem_kv_norm, m_mem_kv_norm, v_mem_kv_norm),
             ("ffn2_norm", d_ffn2, ffn2_norm, m_ffn2_norm, v_ffn2_norm),
             ("final_norm", d_final, final_gain, m_final_norm.reshape(1, D), v_final_norm.reshape(1, D))]
    row_pack = _pack_rows([t[1] for t in small] + [loss_part], D)
    as_rows = lambda t: t.reshape(-1, LANES)
    grads["g1"], (row_parts, pool_parts) = _wgrad(hn1, dg1, "dw_gate1", col_slab=ffn_slab,
                                                  comm=([row_pack, as_rows(d_wpool)], True))
    grads["u1"], (got["g1"], got["w_in"]) = _wgrad(hn1, du1, "dw_up1", col_slab=ffn_slab,
                                                   comm=([slab("g1"), slab("w_in")], PAIRSUM))
    grads["d1"], (got["u1"],) = _wgrad(a1, dyh1, "dw_down1", comm=([slab("u1")], PAIRSUM))
    got["d1"] = _pairsum_exchange([slab("d1")], "scatter_last")[0]

    state = dict(
        g1=(ffn1_w_gate, m_ffn1_w_gate, v_ffn1_w_gate), u1=(ffn1_w_up, m_ffn1_w_up, v_ffn1_w_up),
        d1=(ffn1_w_down, m_ffn1_w_down, v_ffn1_w_down), g2=(ffn2_w_gate, m_ffn2_w_gate, v_ffn2_w_gate),
        u2=(ffn2_w_up, m_ffn2_w_up, v_ffn2_w_up), d2=(ffn2_w_down, m_ffn2_w_down, v_ffn2_w_down),
        w_in=(w_in, m_w_in, v_w_in), w_out=(w_out, m_w_out, v_w_out), w_q=(mem_w_q, m_mem_w_q, v_mem_w_q),
        w_kv=(mem_w_kv, m_mem_w_kv, v_mem_w_kv), w_o=(mem_w_o, m_mem_w_o, v_mem_w_o))
    big_out = {}
    groups = dict(gate_up=("g1", "u1", "g2", "u2"), in_kv=("w_in", "w_kv"), square=("w_out", "w_q", "w_o"),
                  down=("d1", "d2"))
    for label, keys in groups.items():
        items = [(got[k],) + tuple(t[0] for t in state[k]) for k in keys]
        for k, outs in zip(keys, _adamw(items, "adamw_" + label)):
            big_out[k] = [t[None] for t in outs]

    small_res, loss_row = _adamw_small(row_parts, pool_parts, [t[2:] for t in small],
                                       [as_rows(t) for t in (w_pool, m_w_pool, v_w_pool)])
    small_out = {t[0]: small_res[i] for i, t in enumerate(small)}
    small_out["final_norm"] = [t.reshape(D) for t in small_out["final_norm"]]
    small_out["w_pool"] = [t.reshape(w_pool.shape) for t in small_res[-1]]
    loss = loss_row[0, 0]

    order = [("ffn1_norm", None), ("ffn1_w_gate", "g1"), ("ffn1_w_up", "u1"), ("ffn1_w_down", "d1"),
             ("mix_norm", None), ("w_in", "w_in"), ("w_pool", None), ("pool_scale", None), ("w_out", "w_out"),
             ("mem_q_norm", None), ("mem_kv_norm", None), ("mem_w_q", "w_q"), ("mem_w_kv", "w_kv"),
             ("mem_w_o", "w_o"), ("ffn2_norm", None), ("ffn2_w_gate", "g2"), ("ffn2_w_up", "u2"),
             ("ffn2_w_down", "d2"), ("final_norm", None)]
    res = [loss, dx0.reshape(B, S, D)]
    for which in range(4):
        for name, key in order:
            res.append(big_out[key][which] if key else small_out[name][which])
    return tuple(res)
```
